```python
import math
import jax, jax.numpy as jnp
from jax import lax
import numpy as np


D_MODEL = 1024
BATCH = 32
SEQ = 2048
DEPTH = 4

D_MIX = D_MODEL
DN_HEADS = 4
DN_HEAD_DIM = 128
DN_WIDTH = DN_HEADS * DN_HEAD_DIM
LRU_WIDTH = D_MIX - DN_WIDTH
LRU_BLOCKS = 8
LRU_BLOCK = LRU_WIDTH // LRU_BLOCKS
LRU_C = 8.0
SHORT_CONV = 4
SHORT_CONV_LEFT = 2
FFN_CONV = 3
FFN_CONV_LEFT = 1
D_FF = 2816
PLE_DIM = 256
CHUNK = 64
EPS = 1e-6

Q_OFF = 0
K_OFF = DN_WIDTH
V_OFF = 2 * DN_WIDTH
Z_OFF = 3 * DN_WIDTH
BETA_OFF = 4 * DN_WIDTH
ALPHA_OFF = BETA_OFF + 2 * DN_HEADS
LX_OFF = ALPHA_OFF + 2 * DN_HEADS
LG_OFF = LX_OFF + LRU_WIDTH
IN_COLS = LG_OFF + LRU_WIDTH

kernel_name = 'hymba_gdn_rglru_convglu_ple_encoder'


def rmsnorm(x, g):
    x32 = x.astype(jnp.float32)
    y = x32 * lax.rsqrt(jnp.mean(x32 * x32, axis=-1, keepdims=True) + EPS)
    return (y * g.astype(jnp.float32)).astype(x.dtype)


def l2norm(x):
    x32 = x.astype(jnp.float32)
    return (x32 * lax.rsqrt(jnp.sum(x32 * x32, axis=-1, keepdims=True) + EPS)).astype(x.dtype)


def dwconv(x, w, left):
    k = w.shape[0]
    s = x.shape[1]
    xp = jnp.pad(x, ((0, 0), (left, k - 1 - left), (0, 0)))
    out = xp[:, 0:s] * w[0]
    for j in range(1, k):
        out = out + xp[:, j:j + s] * w[j]
    return out


def flip(t):
    return jnp.flip(t, axis=1)


def gated_delta_chunked(q, k, v, g, beta):
    B, S, H, Dk = q.shape
    Dv = v.shape[-1]
    N = S // CHUNK
    f32 = jnp.float32

    def chunks(t):
        t = t.astype(f32).reshape((B, N, CHUNK) + t.shape[2:])
        return jnp.moveaxis(t, 3, 1)

    qc = chunks(q) * (Dk ** -0.5)
    kc, vc, gc, bc = chunks(k), chunks(v), chunks(g), chunks(beta)
    gcum = jnp.cumsum(gc, axis=-1)
    idx = jnp.arange(CHUNK)
    incl = idx[:, None] >= idx[None, :]
    strict = idx[:, None] > idx[None, :]
    decay = jnp.exp(jnp.where(incl, gcum[..., :, None] - gcum[..., None, :], -jnp.inf))
    kb = kc * bc[..., None]
    a_mat = jnp.where(strict, jnp.einsum('bhnid,bhnjd->bhnij', kb, kc) * decay, 0.0)
    rhs = jnp.concatenate([vc * bc[..., None], kb * jnp.exp(gcum)[..., None]], axis=-1)
    sol = lax.linalg.triangular_solve(a_mat, rhs, left_side=True, lower=True, unit_diagonal=True)
    u, w = sol[..., :Dv], sol[..., Dv:]
    attn = jnp.einsum('bhnid,bhnjd->bhnij', qc, kc) * decay
    glast = gcum[..., -1:]
    q_dec = qc * jnp.exp(gcum)[..., None]
    k_dec = kc * jnp.exp(glast - gcum)[..., None]
    cdec = jnp.exp(glast[..., 0])
    xs = tuple(jnp.moveaxis(t, 2, 0) for t in (q_dec, k_dec, w, u, attn, cdec))

    def step(state, inp):
        qd, kd, wi, ui, ai, cd = inp
        v_new = ui - jnp.einsum('bhcd,bhde->bhce', wi, state)
        o = jnp.einsum('bhcd,bhde->bhce', qd, state) + jnp.einsum('bhij,bhje->bhie', ai, v_new)
        state = state * cd[..., None, None] + jnp.einsum('bhcd,bhce->bhde', kd, v_new)
        return state, o

    state0 = jnp.zeros((B, H, Dk, Dv), f32)
    _, o = lax.scan(step, state0, xs)
    o = jnp.transpose(o, (1, 0, 3, 2, 4)).reshape(B, S, H, Dv)
    return o.astype(v.dtype)


def rglru(x, wa, ba, wx, bx, lam):
    B, S, W = x.shape
    xr = x.reshape(B, S, LRU_BLOCKS, LRU_BLOCK)
    r = jax.nn.sigmoid(jnp.einsum('bsnc,ncd->bsnd', xr, wa).reshape(B, S, W) + ba)
    ig = jax.nn.sigmoid(jnp.einsum('bsnc,ncd->bsnd', xr, wx).reshape(B, S, W) + bx)
    log_a = -LRU_C * r.astype(jnp.float32) * jax.nn.softplus(-lam.astype(jnp.float32))
    a = jnp.exp(log_a)
    b = jnp.sqrt(-jnp.expm1(2.0 * log_a)) * (ig * x).astype(jnp.float32)

    def combine(e1, e2):
        a1, b1 = e1
        a2, b2 = e2
        return a1 * a2, a2 * b1 + b2

    _, h = lax.associative_scan(combine, (a, b), axis=1)
    return h.astype(x.dtype)


def _fwd_setup_inputs(seed: int = 0) -> dict:
    key = jax.random.key(seed)
    ks = iter(jax.random.split(key, 32))
    f32 = jnp.float32

    def nrm(shape, scale):
        return scale * jax.random.normal(next(ks), shape, f32)

    def gain(shape):
        return 1.0 + nrm(shape, 0.02)

    L, H = DEPTH, DN_HEADS
    x = nrm((BATCH, SEQ, D_MODEL), 1.0)
    p = nrm((DEPTH, BATCH, SEQ, PLE_DIM), 1.0)
    norm1_g = gain((L, D_MODEL))
    w_in = nrm((L, D_MODEL, IN_COLS), D_MODEL ** -0.5)
    dn_conv_w = nrm((L, SHORT_CONV, 3 * DN_WIDTH), SHORT_CONV ** -0.5)
    dn_a_log = jnp.log(jax.random.uniform(next(ks), (L, 2, H), f32, 1.0, 16.0))
    dt = jnp.exp(jax.random.uniform(next(ks), (L, 2, H), f32, math.log(1e-3), math.log(1e-1)))
    dn_dt_bias = dt + jnp.log(-jnp.expm1(-dt))
    dn_norm_g = gain((L, DN_HEAD_DIM))
    lru_conv_w = nrm((L, SHORT_CONV, LRU_WIDTH), SHORT_CONV ** -0.5)
    lru_conv_b = nrm((L, LRU_WIDTH), 0.02)
    lru_wa = nrm((L, 2, LRU_BLOCKS, LRU_BLOCK, LRU_BLOCK), LRU_BLOCK ** -0.5)
    lru_ba = nrm((L, 2, LRU_WIDTH), 0.02)
    lru_wx = nrm((L, 2, LRU_BLOCKS, LRU_BLOCK, LRU_BLOCK), LRU_BLOCK ** -0.5)
    lru_bx = nrm((L, 2, LRU_WIDTH), 0.02)
    a0 = jax.random.uniform(next(ks), (L, 2, LRU_WIDTH), f32, 0.9, 0.999) ** (1.0 / LRU_C)
    lru_lambda = jnp.log(a0) - jnp.log1p(-a0)
    lru_norm_g = gain((L, LRU_WIDTH))
    w_out = nrm((L, D_MIX, D_MODEL), D_MIX ** -0.5)
    norm2_g = gain((L, D_MODEL))
    ffn_wg = nrm((L, D_MODEL, D_FF), D_MODEL ** -0.5)
    ffn_wu = nrm((L, D_MODEL, D_FF), D_MODEL ** -0.5)
    ffn_conv_w = nrm((L, FFN_CONV, D_FF), FFN_CONV ** -0.5)
    ffn_conv_b = nrm((L, D_FF), 0.02)
    ffn_wd = nrm((L, D_FF, D_MODEL), D_FF ** -0.5)
    ple_norm_g = gain((L, D_MODEL))
    ple_wg = nrm((L, D_MODEL, D_MODEL), D_MODEL ** -0.5)
    ple_bg = nrm((L, D_MODEL), 0.02)
    ple_wp = nrm((L, PLE_DIM, D_MODEL), PLE_DIM ** -0.5)
    final_g = gain((D_MODEL,))
    return {'x': x, 'p': p, 'norm1_g': norm1_g, 'w_in': w_in, 'dn_conv_w': dn_conv_w,
            'dn_a_log': dn_a_log, 'dn_dt_bias': dn_dt_bias, 'dn_norm_g': dn_norm_g,
            'lru_conv_w': lru_conv_w, 'lru_conv_b': lru_conv_b, 'lru_wa': lru_wa, 'lru_ba': lru_ba,
            'lru_wx': lru_wx, 'lru_bx': lru_bx, 'lru_lambda': lru_lambda, 'lru_norm_g': lru_norm_g,
            'w_out': w_out, 'norm2_g': norm2_g, 'ffn_wg': ffn_wg, 'ffn_wu': ffn_wu,
            'ffn_conv_w': ffn_conv_w, 'ffn_conv_b': ffn_conv_b, 'ffn_wd': ffn_wd,
            'ple_norm_g': ple_norm_g, 'ple_wg': ple_wg, 'ple_bg': ple_bg, 'ple_wp': ple_wp,
            'final_g': final_g}


def _fwd_reference(x, p, norm1_g, w_in, dn_conv_w, dn_a_log, dn_dt_bias, dn_norm_g,
              lru_conv_w, lru_conv_b, lru_wa, lru_ba, lru_wx, lru_bx, lru_lambda, lru_norm_g,
              w_out, norm2_g, ffn_wg, ffn_wu, ffn_conv_w, ffn_conv_b, ffn_wd,
              ple_norm_g, ple_wg, ple_bg, ple_wp, final_g):
    B, S, _ = x.shape
    H, Dh = DN_HEADS, DN_HEAD_DIM
    r = x
    for i in range(DEPTH):
        h = rmsnorm(r, norm1_g[i])
        proj = h @ w_in[i]
        qkv = jax.nn.silu(dwconv(proj[..., Q_OFF:Z_OFF], dn_conv_w[i], SHORT_CONV_LEFT))
        q = l2norm(qkv[..., Q_OFF:K_OFF].reshape(B, S, H, Dh))
        k = l2norm(qkv[..., K_OFF:V_OFF].reshape(B, S, H, Dh))
        v = qkv[..., V_OFF:Z_OFF].reshape(B, S, H, Dh)
        z = proj[..., Z_OFF:BETA_OFF].reshape(B, S, H, Dh)
        beta = jax.nn.sigmoid(proj[..., BETA_OFF:ALPHA_OFF].reshape(B, S, 2, H))
        alpha = proj[..., ALPHA_OFF:LX_OFF].reshape(B, S, 2, H).astype(jnp.float32)
        g = -jnp.exp(dn_a_log[i].astype(jnp.float32)) * jax.nn.softplus(alpha + dn_dt_bias[i].astype(jnp.float32))
        o_f = gated_delta_chunked(q, k, v, g[:, :, 0], beta[:, :, 0])
        o_b = flip(gated_delta_chunked(flip(q), flip(k), flip(v), flip(g[:, :, 1]), flip(beta[:, :, 1])))
        dn_out = (rmsnorm(o_f + o_b, dn_norm_g[i]) * jax.nn.silu(z)).reshape(B, S, DN_WIDTH)
        xc = dwconv(proj[..., LX_OFF:LG_OFF], lru_conv_w[i], SHORT_CONV_LEFT) + lru_conv_b[i]
        h_f = rglru(xc, lru_wa[i, 0], lru_ba[i, 0], lru_wx[i, 0], lru_bx[i, 0], lru_lambda[i, 0])
        h_b = flip(rglru(flip(xc), lru_wa[i, 1], lru_ba[i, 1], lru_wx[i, 1], lru_bx[i, 1], lru_lambda[i, 1]))
        lru_out = rmsnorm(jax.nn.gelu(proj[..., LG_OFF:IN_COLS]) * (h_f + h_b), lru_norm_g[i])
        r = r + jnp.concatenate([dn_out, lru_out], axis=-1) @ w_out[i]
        h2 = rmsnorm(r, norm2_g[i])
        gate = dwconv(h2 @ ffn_wg[i], ffn_conv_w[i], FFN_CONV_LEFT) + ffn_conv_b[i]
        r = r + (jax.nn.gelu(gate) * (h2 @ ffn_wu[i])) @ ffn_wd[i]
        pg = jax.nn.sigmoid(rmsnorm(r, ple_norm_g[i]) @ ple_wg[i] + ple_bg[i])
        r = r + pg * (p[i] @ ple_wp[i])
    return rmsnorm(r, final_g)


import jax as _jax
import jax.numpy as _jnp

TWIN_FORMAT = 'train_step'
FWD_PARAMS = ['x', 'p', 'norm1_g', 'w_in', 'dn_conv_w', 'dn_a_log', 'dn_dt_bias', 'dn_norm_g', 'lru_conv_w', 'lru_conv_b', 'lru_wa', 'lru_ba', 'lru_wx', 'lru_bx', 'lru_lambda', 'lru_norm_g', 'w_out', 'norm2_g', 'ffn_wg', 'ffn_wu', 'ffn_conv_w', 'ffn_conv_b', 'ffn_wd', 'ple_norm_g', 'ple_wg', 'ple_bg', 'ple_wp', 'final_g']
TWIN_WEIGHTS = ['norm1_g', 'w_in', 'dn_conv_w', 'dn_a_log', 'dn_dt_bias', 'dn_norm_g', 'lru_conv_w', 'lru_conv_b', 'lru_wa', 'lru_ba', 'lru_wx', 'lru_bx', 'lru_lambda', 'lru_norm_g', 'w_out', 'norm2_g', 'ffn_wg', 'ffn_wu', 'ffn_conv_w', 'ffn_conv_b', 'ffn_wd', 'ple_norm_g', 'ple_wg', 'ple_bg', 'ple_wp', 'final_g']
TWIN_DIFF_INPUT = 'x'
TWIN_INPUTS = ['x', 'p', 'norm1_g', 'w_in', 'dn_conv_w', 'dn_a_log', 'dn_dt_bias', 'dn_norm_g', 'lru_conv_w', 'lru_conv_b', 'lru_wa', 'lru_ba', 'lru_wx', 'lru_bx', 'lru_lambda', 'lru_norm_g', 'w_out', 'norm2_g', 'ffn_wg', 'ffn_wu', 'ffn_conv_w', 'ffn_conv_b', 'ffn_wd', 'ple_norm_g', 'ple_wg', 'ple_bg', 'ple_wp', 'final_g', 'loss_target', 'm_norm1_g', 'm_w_in', 'm_dn_conv_w', 'm_dn_a_log', 'm_dn_dt_bias', 'm_dn_norm_g', 'm_lru_conv_w', 'm_lru_conv_b', 'm_lru_wa', 'm_lru_ba', 'm_lru_wx', 'm_lru_bx', 'm_lru_lambda', 'm_lru_norm_g', 'm_w_out', 'm_norm2_g', 'm_ffn_wg', 'm_ffn_wu', 'm_ffn_conv_w', 'm_ffn_conv_b', 'm_ffn_wd', 'm_ple_norm_g', 'm_ple_wg', 'm_ple_bg', 'm_ple_wp', 'm_final_g', 'v_norm1_g', 'v_w_in', 'v_dn_conv_w', 'v_dn_a_log', 'v_dn_dt_bias', 'v_dn_norm_g', 'v_lru_conv_w', 'v_lru_conv_b', 'v_lru_wa', 'v_lru_ba', 'v_lru_wx', 'v_lru_bx', 'v_lru_lambda', 'v_lru_norm_g', 'v_w_out', 'v_norm2_g', 'v_ffn_wg', 'v_ffn_wu', 'v_ffn_conv_w', 'v_ffn_conv_b', 'v_ffn_wd', 'v_ple_norm_g', 'v_ple_wg', 'v_ple_bg', 'v_ple_wp', 'v_final_g']
TWIN_OUTPUTS = ['loss', 'grad_x', 'grad_norm1_g', 'grad_w_in', 'grad_dn_conv_w', 'grad_dn_a_log', 'grad_dn_dt_bias', 'grad_dn_norm_g', 'grad_lru_conv_w', 'grad_lru_conv_b', 'grad_lru_wa', 'grad_lru_ba', 'grad_lru_wx', 'grad_lru_bx', 'grad_lru_lambda', 'grad_lru_norm_g', 'grad_w_out', 'grad_norm2_g', 'grad_ffn_wg', 'grad_ffn_wu', 'grad_ffn_conv_w', 'grad_ffn_conv_b', 'grad_ffn_wd', 'grad_ple_norm_g', 'grad_ple_wg', 'grad_ple_bg', 'grad_ple_wp', 'grad_final_g', 'delta_norm1_g', 'delta_w_in', 'delta_dn_conv_w', 'delta_dn_a_log', 'delta_dn_dt_bias', 'delta_dn_norm_g', 'delta_lru_conv_w', 'delta_lru_conv_b', 'delta_lru_wa', 'delta_lru_ba', 'delta_lru_wx', 'delta_lru_bx', 'delta_lru_lambda', 'delta_lru_norm_g', 'delta_w_out', 'delta_norm2_g', 'delta_ffn_wg', 'delta_ffn_wu', 'delta_ffn_conv_w', 'delta_ffn_conv_b', 'delta_ffn_wd', 'delta_ple_norm_g', 'delta_ple_wg', 'delta_ple_bg', 'delta_ple_wp', 'delta_final_g', 'new_m_norm1_g', 'new_m_w_in', 'new_m_dn_conv_w', 'new_m_dn_a_log', 'new_m_dn_dt_bias', 'new_m_dn_norm_g', 'new_m_lru_conv_w', 'new_m_lru_conv_b', 'new_m_lru_wa', 'new_m_lru_ba', 'new_m_lru_wx', 'new_m_lru_bx', 'new_m_lru_lambda', 'new_m_lru_norm_g', 'new_m_w_out', 'new_m_norm2_g', 'new_m_ffn_wg', 'new_m_ffn_wu', 'new_m_ffn_conv_w', 'new_m_ffn_conv_b', 'new_m_ffn_wd', 'new_m_ple_norm_g', 'new_m_ple_wg', 'new_m_ple_bg', 'new_m_ple_wp', 'new_m_final_g', 'new_v_norm1_g', 'new_v_w_in', 'new_v_dn_conv_w', 'new_v_dn_a_log', 'new_v_dn_dt_bias', 'new_v_dn_norm_g', 'new_v_lru_conv_w', 'new_v_lru_conv_b', 'new_v_lru_wa', 'new_v_lru_ba', 'new_v_lru_wx', 'new_v_lru_bx', 'new_v_lru_lambda', 'new_v_lru_norm_g', 'new_v_w_out', 'new_v_norm2_g', 'new_v_ffn_wg', 'new_v_ffn_wu', 'new_v_ffn_conv_w', 'new_v_ffn_conv_b', 'new_v_ffn_wd', 'new_v_ple_norm_g', 'new_v_ple_wg', 'new_v_ple_bg', 'new_v_ple_wp', 'new_v_final_g']
TWIN_LEAF_KINDS = {'loss': 'loss', 'grad_x': 'grad_x', 'grad_norm1_g': 'grad_w', 'grad_w_in': 'grad_w', 'grad_dn_conv_w': 'grad_w', 'grad_dn_a_log': 'grad_w', 'grad_dn_dt_bias': 'grad_w', 'grad_dn_norm_g': 'grad_w', 'grad_lru_conv_w': 'grad_w', 'grad_lru_conv_b': 'grad_w', 'grad_lru_wa': 'grad_w', 'grad_lru_ba': 'grad_w', 'grad_lru_wx': 'grad_w', 'grad_lru_bx': 'grad_w', 'grad_lru_lambda': 'grad_w', 'grad_lru_norm_g': 'grad_w', 'grad_w_out': 'grad_w', 'grad_norm2_g': 'grad_w', 'grad_ffn_wg': 'grad_w', 'grad_ffn_wu': 'grad_w', 'grad_ffn_conv_w': 'grad_w', 'grad_ffn_conv_b': 'grad_w', 'grad_ffn_wd': 'grad_w', 'grad_ple_norm_g': 'grad_w', 'grad_ple_wg': 'grad_w', 'grad_ple_bg': 'grad_w', 'grad_ple_wp': 'grad_w', 'grad_final_g': 'grad_w', 'delta_norm1_g': 'delta_w', 'delta_w_in': 'delta_w', 'delta_dn_conv_w': 'delta_w', 'delta_dn_a_log': 'delta_w', 'delta_dn_dt_bias': 'delta_w', 'delta_dn_norm_g': 'delta_w', 'delta_lru_conv_w': 'delta_w', 'delta_lru_conv_b': 'delta_w', 'delta_lru_wa': 'delta_w', 'delta_lru_ba': 'delta_w', 'delta_lru_wx': 'delta_w', 'delta_lru_bx': 'delta_w', 'delta_lru_lambda': 'delta_w', 'delta_lru_norm_g': 'delta_w', 'delta_w_out': 'delta_w', 'delta_norm2_g': 'delta_w', 'delta_ffn_wg': 'delta_w', 'delta_ffn_wu': 'delta_w', 'delta_ffn_conv_w': 'delta_w', 'delta_ffn_conv_b': 'delta_w', 'delta_ffn_wd': 'delta_w', 'delta_ple_norm_g': 'delta_w', 'delta_ple_wg': 'delta_w', 'delta_ple_bg': 'delta_w', 'delta_ple_wp': 'delta_w', 'delta_final_g': 'delta_w', 'new_m_norm1_g': 'new_m', 'new_m_w_in': 'new_m', 'new_m_dn_conv_w': 'new_m', 'new_m_dn_a_log': 'new_m', 'new_m_dn_dt_bias': 'new_m', 'new_m_dn_norm_g': 'new_m', 'new_m_lru_conv_w': 'new_m', 'new_m_lru_conv_b': 'new_m', 'new_m_lru_wa': 'new_m', 'new_m_lru_ba': 'new_m', 'new_m_lru_wx': 'new_m', 'new_m_lru_bx': 'new_m', 'new_m_lru_lambda': 'new_m', 'new_m_lru_norm_g': 'new_m', 'new_m_w_out': 'new_m', 'new_m_norm2_g': 'new_m', 'new_m_ffn_wg': 'new_m', 'new_m_ffn_wu': 'new_m', 'new_m_ffn_conv_w': 'new_m', 'new_m_ffn_conv_b': 'new_m', 'new_m_ffn_wd': 'new_m', 'new_m_ple_norm_g': 'new_m', 'new_m_ple_wg': 'new_m', 'new_m_ple_bg': 'new_m', 'new_m_ple_wp': 'new_m', 'new_m_final_g': 'new_m', 'new_v_norm1_g': 'new_v', 'new_v_w_in': 'new_v', 'new_v_dn_conv_w': 'new_v', 'new_v_dn_a_log': 'new_v', 'new_v_dn_dt_bias': 'new_v', 'new_v_dn_norm_g': 'new_v', 'new_v_lru_conv_w': 'new_v', 'new_v_lru_conv_b': 'new_v', 'new_v_lru_wa': 'new_v', 'new_v_lru_ba': 'new_v', 'new_v_lru_wx': 'new_v', 'new_v_lru_bx': 'new_v', 'new_v_lru_lambda': 'new_v', 'new_v_lru_norm_g': 'new_v', 'new_v_w_out': 'new_v', 'new_v_norm2_g': 'new_v', 'new_v_ffn_wg': 'new_v', 'new_v_ffn_wu': 'new_v', 'new_v_ffn_conv_w': 'new_v', 'new_v_ffn_conv_b': 'new_v', 'new_v_ffn_wd': 'new_v', 'new_v_ple_norm_g': 'new_v', 'new_v_ple_wg': 'new_v', 'new_v_ple_bg': 'new_v', 'new_v_ple_wp': 'new_v', 'new_v_final_g': 'new_v'}


def _forward(args):
    return _fwd_reference(*[args[k] for k in FWD_PARAMS])


def _output_shape():
    out = _jax.eval_shape(lambda: _forward(_fwd_setup_inputs(0)))
    return out.shape, out.dtype

N_MICROBATCH = 1
ADAM_LR = 0.001
ADAM_B1 = 0.9
ADAM_B2 = 0.999
ADAM_EPS = 1e-08
ADAM_WD = 0.01
ADAM_STEP = 10
PER_EXAMPLE_BATCH_AXIS = {'x': 0, 'p': 1, 'loss_target': 0}
SHARED_INPUTS = []
_WEIGHT_DTYPES = {'norm1_g': _jnp.float32, 'w_in': _jnp.float32, 'dn_conv_w': _jnp.float32, 'dn_a_log': _jnp.float32, 'dn_dt_bias': _jnp.float32, 'dn_norm_g': _jnp.float32, 'lru_conv_w': _jnp.float32, 'lru_conv_b': _jnp.float32, 'lru_wa': _jnp.float32, 'lru_ba': _jnp.float32, 'lru_wx': _jnp.float32, 'lru_bx': _jnp.float32, 'lru_lambda': _jnp.float32, 'lru_norm_g': _jnp.float32, 'w_out': _jnp.float32, 'norm2_g': _jnp.float32, 'ffn_wg': _jnp.float32, 'ffn_wu': _jnp.float32, 'ffn_conv_w': _jnp.float32, 'ffn_conv_b': _jnp.float32, 'ffn_wd': _jnp.float32, 'ple_norm_g': _jnp.float32, 'ple_wg': _jnp.float32, 'ple_bg': _jnp.float32, 'ple_wp': _jnp.float32, 'final_g': _jnp.float32}
MOMENT_SCALE = {'norm1_g': 2.247895e-01, 'w_in': 1.290837e-01, 'dn_conv_w': 8.082963e-02, 'dn_a_log': 2.595625e-01, 'dn_dt_bias': 2.508858e-01, 'dn_norm_g': 2.499137e-01, 'lru_conv_w': 2.187053e-01, 'lru_conv_b': 2.803062e+00, 'lru_wa': 5.081341e-02, 'lru_ba': 4.483228e-02, 'lru_wx': 9.530933e-02, 'lru_bx': 4.643409e-02, 'lru_lambda': 8.143782e-02, 'lru_norm_g': 2.074750e-01, 'w_out': 1.772650e-01, 'norm2_g': 1.462071e-01, 'ffn_wg': 6.023103e-02, 'ffn_wu': 5.971876e-02, 'ffn_conv_w': 6.261709e-02, 'ffn_conv_b': 6.179536e-02, 'ffn_wd': 9.878093e-02, 'ple_norm_g': 3.109590e-02, 'ple_wg': 3.075874e-02, 'ple_bg': 3.427176e-02, 'ple_wp': 7.855910e-02, 'final_g': 6.412754e+01}


def _to_microbatches(a, axis):
    t = _jnp.moveaxis(a, axis, 0)
    t = t.reshape((N_MICROBATCH, t.shape[0] // N_MICROBATCH) + t.shape[1:])
    return _jnp.moveaxis(t, 1, axis + 1)


def setup_inputs(seed: int = 0) -> dict:
    inp = _fwd_setup_inputs(seed)
    key = _jax.random.fold_in(_jax.random.key(seed), 7919)
    shape, _ = _output_shape()
    out = dict(inp)
    out["loss_target"] = _jax.random.normal(_jax.random.fold_in(key, 0), shape, _jnp.float32)
    for i, name in enumerate(TWIN_WEIGHTS):
        w = inp[name].astype(_jnp.float32)
        if MOMENT_SCALE is None:
            s = _jnp.sqrt(_jnp.mean(_jnp.square(w)) + 1e-30)
        else:
            s = MOMENT_SCALE[name]
        km, kv = _jax.random.split(_jax.random.fold_in(key, i + 1))
        out[name] = w
        out["m_" + name] = s * _jax.random.normal(km, w.shape, _jnp.float32)
        out["v_" + name] = (s * s) * _jax.random.uniform(kv, w.shape, _jnp.float32, 0.5, 1.5)
    if N_MICROBATCH > 1:
        for name, axis in PER_EXAMPLE_BATCH_AXIS.items():
            out[name] = _to_microbatches(out[name], axis)
    return {'x': out['x'], 'p': out['p'], 'norm1_g': out['norm1_g'], 'w_in': out['w_in'], 'dn_conv_w': out['dn_conv_w'], 'dn_a_log': out['dn_a_log'], 'dn_dt_bias': out['dn_dt_bias'], 'dn_norm_g': out['dn_norm_g'], 'lru_conv_w': out['lru_conv_w'], 'lru_conv_b': out['lru_conv_b'], 'lru_wa': out['lru_wa'], 'lru_ba': out['lru_ba'], 'lru_wx': out['lru_wx'], 'lru_bx': out['lru_bx'], 'lru_lambda': out['lru_lambda'], 'lru_norm_g': out['lru_norm_g'], 'w_out': out['w_out'], 'norm2_g': out['norm2_g'], 'ffn_wg': out['ffn_wg'], 'ffn_wu': out['ffn_wu'], 'ffn_conv_w': out['ffn_conv_w'], 'ffn_conv_b': out['ffn_conv_b'], 'ffn_wd': out['ffn_wd'], 'ple_norm_g': out['ple_norm_g'], 'ple_wg': out['ple_wg'], 'ple_bg': out['ple_bg'], 'ple_wp': out['ple_wp'], 'final_g': out['final_g'], 'loss_target': out['loss_target'], 'm_norm1_g': out['m_norm1_g'], 'm_w_in': out['m_w_in'], 'm_dn_conv_w': out['m_dn_conv_w'], 'm_dn_a_log': out['m_dn_a_log'], 'm_dn_dt_bias': out['m_dn_dt_bias'], 'm_dn_norm_g': out['m_dn_norm_g'], 'm_lru_conv_w': out['m_lru_conv_w'], 'm_lru_conv_b': out['m_lru_conv_b'], 'm_lru_wa': out['m_lru_wa'], 'm_lru_ba': out['m_lru_ba'], 'm_lru_wx': out['m_lru_wx'], 'm_lru_bx': out['m_lru_bx'], 'm_lru_lambda': out['m_lru_lambda'], 'm_lru_norm_g': out['m_lru_norm_g'], 'm_w_out': out['m_w_out'], 'm_norm2_g': out['m_norm2_g'], 'm_ffn_wg': out['m_ffn_wg'], 'm_ffn_wu': out['m_ffn_wu'], 'm_ffn_conv_w': out['m_ffn_conv_w'], 'm_ffn_conv_b': out['m_ffn_conv_b'], 'm_ffn_wd': out['m_ffn_wd'], 'm_ple_norm_g': out['m_ple_norm_g'], 'm_ple_wg': out['m_ple_wg'], 'm_ple_bg': out['m_ple_bg'], 'm_ple_wp': out['m_ple_wp'], 'm_final_g': out['m_final_g'], 'v_norm1_g': out['v_norm1_g'], 'v_w_in': out['v_w_in'], 'v_dn_conv_w': out['v_dn_conv_w'], 'v_dn_a_log': out['v_dn_a_log'], 'v_dn_dt_bias': out['v_dn_dt_bias'], 'v_dn_norm_g': out['v_dn_norm_g'], 'v_lru_conv_w': out['v_lru_conv_w'], 'v_lru_conv_b': out['v_lru_conv_b'], 'v_lru_wa': out['v_lru_wa'], 'v_lru_ba': out['v_lru_ba'], 'v_lru_wx': out['v_lru_wx'], 'v_lru_bx': out['v_lru_bx'], 'v_lru_lambda': out['v_lru_lambda'], 'v_lru_norm_g': out['v_lru_norm_g'], 'v_w_out': out['v_w_out'], 'v_norm2_g': out['v_norm2_g'], 'v_ffn_wg': out['v_ffn_wg'], 'v_ffn_wu': out['v_ffn_wu'], 'v_ffn_conv_w': out['v_ffn_conv_w'], 'v_ffn_conv_b': out['v_ffn_conv_b'], 'v_ffn_wd': out['v_ffn_wd'], 'v_ple_norm_g': out['v_ple_norm_g'], 'v_ple_wg': out['v_ple_wg'], 'v_ple_bg': out['v_ple_bg'], 'v_ple_wp': out['v_ple_wp'], 'v_final_g': out['v_final_g']}


def _loss(weights, diff, rest, loss_target):
    with _jax.named_scope("forward"):
        args = {**rest, TWIN_DIFF_INPUT: diff, **{k: w.astype(_WEIGHT_DTYPES[k]) for k, w in weights.items()}}
        y = _forward(args)
    with _jax.named_scope("loss_head"):
        err = _jnp.square(y.astype(_jnp.float32) - loss_target)
        return 0.5 * _jnp.sum(_jnp.mean(err, axis=-1)) if err.ndim else 0.5 * err


def _adamw(w, g, m, v):
    m = ADAM_B1 * m + (1.0 - ADAM_B1) * g
    v = ADAM_B2 * v + (1.0 - ADAM_B2) * _jnp.square(g)
    m_hat = m / (1.0 - ADAM_B1 ** ADAM_STEP)
    v_hat = v / (1.0 - ADAM_B2 ** ADAM_STEP)
    delta = -ADAM_LR * (m_hat / (_jnp.sqrt(v_hat) + ADAM_EPS) + ADAM_WD * w)
    return delta, m, v


def reference(x, p, norm1_g, w_in, dn_conv_w, dn_a_log, dn_dt_bias, dn_norm_g, lru_conv_w, lru_conv_b, lru_wa, lru_ba, lru_wx, lru_bx, lru_lambda, lru_norm_g, w_out, norm2_g, ffn_wg, ffn_wu, ffn_conv_w, ffn_conv_b, ffn_wd, ple_norm_g, ple_wg, ple_bg, ple_wp, final_g, loss_target, m_norm1_g, m_w_in, m_dn_conv_w, m_dn_a_log, m_dn_dt_bias, m_dn_norm_g, m_lru_conv_w, m_lru_conv_b, m_lru_wa, m_lru_ba, m_lru_wx, m_lru_bx, m_lru_lambda, m_lru_norm_g, m_w_out, m_norm2_g, m_ffn_wg, m_ffn_wu, m_ffn_conv_w, m_ffn_conv_b, m_ffn_wd, m_ple_norm_g, m_ple_wg, m_ple_bg, m_ple_wp, m_final_g, v_norm1_g, v_w_in, v_dn_conv_w, v_dn_a_log, v_dn_dt_bias, v_dn_norm_g, v_lru_conv_w, v_lru_conv_b, v_lru_wa, v_lru_ba, v_lru_wx, v_lru_bx, v_lru_lambda, v_lru_norm_g, v_w_out, v_norm2_g, v_ffn_wg, v_ffn_wu, v_ffn_conv_w, v_ffn_conv_b, v_ffn_wd, v_ple_norm_g, v_ple_wg, v_ple_bg, v_ple_wp, v_final_g):
    given = dict(x=x, p=p, norm1_g=norm1_g, w_in=w_in, dn_conv_w=dn_conv_w, dn_a_log=dn_a_log, dn_dt_bias=dn_dt_bias, dn_norm_g=dn_norm_g, lru_conv_w=lru_conv_w, lru_conv_b=lru_conv_b, lru_wa=lru_wa, lru_ba=lru_ba, lru_wx=lru_wx, lru_bx=lru_bx, lru_lambda=lru_lambda, lru_norm_g=lru_norm_g, w_out=w_out, norm2_g=norm2_g, ffn_wg=ffn_wg, ffn_wu=ffn_wu, ffn_conv_w=ffn_conv_w, ffn_conv_b=ffn_conv_b, ffn_wd=ffn_wd, ple_norm_g=ple_norm_g, ple_wg=ple_wg, ple_bg=ple_bg, ple_wp=ple_wp, final_g=final_g, loss_target=loss_target, m_norm1_g=m_norm1_g, m_w_in=m_w_in, m_dn_conv_w=m_dn_conv_w, m_dn_a_log=m_dn_a_log, m_dn_dt_bias=m_dn_dt_bias, m_dn_norm_g=m_dn_norm_g, m_lru_conv_w=m_lru_conv_w, m_lru_conv_b=m_lru_conv_b, m_lru_wa=m_lru_wa, m_lru_ba=m_lru_ba, m_lru_wx=m_lru_wx, m_lru_bx=m_lru_bx, m_lru_lambda=m_lru_lambda, m_lru_norm_g=m_lru_norm_g, m_w_out=m_w_out, m_norm2_g=m_norm2_g, m_ffn_wg=m_ffn_wg, m_ffn_wu=m_ffn_wu, m_ffn_conv_w=m_ffn_conv_w, m_ffn_conv_b=m_ffn_conv_b, m_ffn_wd=m_ffn_wd, m_ple_norm_g=m_ple_norm_g, m_ple_wg=m_ple_wg, m_ple_bg=m_ple_bg, m_ple_wp=m_ple_wp, m_final_g=m_final_g, v_norm1_g=v_norm1_g, v_w_in=v_w_in, v_dn_conv_w=v_dn_conv_w, v_dn_a_log=v_dn_a_log, v_dn_dt_bias=v_dn_dt_bias, v_dn_norm_g=v_dn_norm_g, v_lru_conv_w=v_lru_conv_w, v_lru_conv_b=v_lru_conv_b, v_lru_wa=v_lru_wa, v_lru_ba=v_lru_ba, v_lru_wx=v_lru_wx, v_lru_bx=v_lru_bx, v_lru_lambda=v_lru_lambda, v_lru_norm_g=v_lru_norm_g, v_w_out=v_w_out, v_norm2_g=v_norm2_g, v_ffn_wg=v_ffn_wg, v_ffn_wu=v_ffn_wu, v_ffn_conv_w=v_ffn_conv_w, v_ffn_conv_b=v_ffn_conv_b, v_ffn_wd=v_ffn_wd, v_ple_norm_g=v_ple_norm_g, v_ple_wg=v_ple_wg, v_ple_bg=v_ple_bg, v_ple_wp=v_ple_wp, v_final_g=v_final_g)
    weights = {n: given[n] for n in TWIN_WEIGHTS}
    shared = {n: given[n] for n in SHARED_INPUTS}
    per_example = {n: given[n] for n in ['x', 'p']}
    grad_fn = _jax.value_and_grad(_loss, argnums=(0, 1))

    def one_microbatch(ex, loss_target):
        ex = dict(ex)
        diff = ex.pop(TWIN_DIFF_INPUT)
        return grad_fn(weights, diff, {**shared, **ex}, loss_target)

    if N_MICROBATCH == 1:
        loss, (grad_w, grad_x) = one_microbatch(per_example, given["loss_target"])
    else:
        def body(carry, xs):
            loss_sum, grad_sum = carry
            l_k, (gw_k, gx_k) = one_microbatch(xs[0], xs[1])
            with _jax.named_scope("update"):
                return (loss_sum + l_k, _jax.tree.map(_jnp.add, grad_sum, gw_k)), gx_k

        init = (_jnp.zeros((), _jnp.float32), _jax.tree.map(_jnp.zeros_like, weights))
        (loss, grad_w), grad_x = _jax.lax.scan(body, init, (per_example, given["loss_target"]))
    with _jax.named_scope("update"):
        delta_w, new_m, new_v = {}, {}, {}
        for n in TWIN_WEIGHTS:
            delta_w[n], new_m[n], new_v[n] = _adamw(weights[n], grad_w[n], given["m_" + n], given["v_" + n])
    return (loss, grad_x, *[grad_w[n] for n in TWIN_WEIGHTS], *[delta_w[n] for n in TWIN_WEIGHTS],
            *[new_m[n] for n in TWIN_WEIGHTS], *[new_v[n] for n in TWIN_WEIGHTS])
```

```python
import functools
import math

import jax
import jax.numpy as jnp
from jax import lax
from jax.experimental import pallas as pl
from jax.experimental.pallas import tpu as pltpu

f32 = jnp.float32
bf16 = jnp.bfloat16
HI = lax.Precision.HIGHEST
MESH = pl.DeviceIdType.MESH

N_DEV = 8
LANES = 128
EPS = 1e-6
LRU_C = 8.0
ADAM_LR, ADAM_B1, ADAM_B2, ADAM_EPS, ADAM_WD, ADAM_STEP = 0.001, 0.9, 0.999, 1e-08, 0.01, 10
VMEM_LIMIT = 56 * 1024 * 1024


class Cfg:
    def __init__(self, d_model=1024, bl=4, seq=2048, depth=4, heads=4, lru_width=512, d_ff=2816, ple=256,
                 tm=512, tm_ffn=256, ff_tile=256):
        self.D, self.BL, self.S, self.L, self.H = d_model, bl, seq, depth, heads
        self.DH = 128
        self.DN = heads * self.DH
        self.LW = lru_width
        self.NB = lru_width // 64
        self.FF, self.PD = d_ff, ple
        self.C = 64
        self.NC = seq // self.C
        self.T = bl * seq
        self.TM = min(tm, self.T)
        self.TMF = min(tm_ffn, self.T)
        self.FT = ff_tile
        self.ZO = 3 * self.DN
        self.LXO = 4 * self.DN
        self.LGO = self.LXO + self.LW
        self.BAO = self.LGO + self.LW
        self.PC = self.BAO + LANES
        self.IN_COLS = 4 * self.DN + 4 * heads + 2 * self.LW


CFG = Cfg()


def _mm(a, b):
    return jnp.dot(a.astype(bf16), b.astype(bf16), preferred_element_type=f32)


def _mm_nt(a, b):
    return lax.dot_general(a.astype(bf16), b.astype(bf16), (((1,), (1,)), ((), ())), preferred_element_type=f32)


def _mm_tn(a, b):
    return lax.dot_general(a.astype(bf16), b.astype(bf16), (((0,), (0,)), ((), ())), preferred_element_type=f32)


def _hmm(a, b):
    return jnp.dot(a, b, precision=HI, preferred_element_type=f32)


def _hmm_tn(a, b):
    return lax.dot_general(a, b, (((0,), (0,)), ((), ())), precision=HI, preferred_element_type=f32)


def _hmm_nt(a, b):
    return lax.dot_general(a, b, (((1,), (1,)), ((), ())), precision=HI, preferred_element_type=f32)


def _rms_fwd(x, g):
    inv = lax.rsqrt(jnp.mean(x * x, axis=-1, keepdims=True) + EPS)
    xh = x * inv
    return xh * g, xh, inv


def _rms_bwd(dy, xh, inv, g):
    dxh = dy * g
    dx = inv * (dxh - xh * jnp.mean(dxh * xh, axis=-1, keepdims=True))
    dg = jnp.sum(dy * xh, axis=0, keepdims=True)
    return dx, dg


def _sigmoid(x):
    return 1.0 / (1.0 + jnp.exp(-x))


def _softplus(x):
    return jnp.maximum(x, 0.0) + jnp.log(1.0 + jnp.exp(-jnp.abs(x)))


def _silu(x):
    s = _sigmoid(x)
    return x * s, s * (1.0 + x * (1.0 - s))


_GC = math.sqrt(2.0 / math.pi)


def _gelu(x):
    t = jnp.tanh(_GC * (x + 0.044715 * x * x * x))
    y = 0.5 * x * (1.0 + t)
    dy = 0.5 * (1.0 + t) + 0.5 * x * (1.0 - t * t) * _GC * (1.0 + 3.0 * 0.044715 * x * x)
    return y, dy


def _nexpm1(x):
    ser = -x * (1.0 + x * 0.5 * (1.0 + x * (1.0 / 3.0) * (1.0 + x * 0.25 * (1.0 + x * 0.2))))
    return jnp.where(x > -0.1, ser, 1.0 - jnp.exp(x))


def _shift(x, s, fill=0.0):
    if s == 0:
        return x
    n = x.shape[0]
    t = lax.broadcasted_iota(jnp.int32, x.shape, 0)
    r = pltpu.roll(x, (-s) % n, 0)
    return jnp.where((t + s >= 0) & (t + s < n), r, fill)


def _conv_fwd(x, w_ref, left):
    k = w_ref.shape[0]
    out = _shift(x, -left) * w_ref[0:1, :]
    for j in range(1, k):
        out = out + _shift(x, j - left) * w_ref[j:j + 1, :]
    return out


def _conv_bwd(dout, x, w_ref, left):
    k = w_ref.shape[0]
    dx = None
    dws = []
    for j in range(k):
        term = _shift(dout, -(j - left)) * w_ref[j:j + 1, :]
        dx = term if dx is None else dx + term
        dws.append(jnp.sum(dout * _shift(x, j - left), axis=0, keepdims=True))
    return dx, dws


def _lin_scan(a, b, rev):
    n = a.shape[0]
    d = 1
    while d < n:
        s = d if rev else -d
        b = a * _shift(b, s, 0.0) + b
        a = a * _shift(a, s, 1.0)
        d *= 2
    return b


def _tri_masks(c, rev):
    i = lax.broadcasted_iota(jnp.int32, (c, c), 0)
    j = lax.broadcasted_iota(jnp.int32, (c, c), 1)
    incl = (i <= j) if rev else (i >= j)
    strict = (i < j) if rev else (i > j)
    incl_t = (i >= j) if rev else (i <= j)
    return incl, strict, incl_t


def _tri_inv(a):
    c = a.shape[0]
    i = lax.broadcasted_iota(jnp.int32, (c, c), 0)
    j = lax.broadcasted_iota(jnp.int32, (c, c), 1)
    t = jnp.where(i == j, 1.0, 0.0) - a
    pw = _hmm(a, a)
    for it in range(5):
        t = t + _hmm(t, pw)
        if it < 4:
            pw = _hmm(pw, pw)
    return t


def _gdn_prep(q, k, v, g, beta, rev):
    c = q.shape[0]
    incl, strict, incl_t = _tri_masks(c, rev)
    inclf = incl.astype(f32)
    incltf = incl_t.astype(f32)
    gb = jnp.broadcast_to(g, (c, c))
    mcol = _hmm(inclf, gb)
    mrow = _hmm(jnp.ones((c, c), f32), gb * incltf)
    gcum = mcol[:, 0:1]
    dec = jnp.exp(jnp.where(incl, mcol - mrow, -1e30))
    glast = jnp.sum(g, axis=0, keepdims=True)
    e = jnp.exp(gcum)
    el = jnp.exp(glast - gcum)
    cd = jnp.exp(glast)
    qs = q * (q.shape[1] ** -0.5)
    kb = k * beta
    a = jnp.where(strict, _mm_nt(kb, k) * dec, 0.0)
    tm = _tri_inv(a)
    u = _hmm(tm, v * beta)
    w = _hmm(tm, kb * e)
    p = jnp.where(incl, _mm_nt(qs, k) * dec, 0.0)
    return dict(incl=incl, strict=strict, incltf=incltf, dec=dec, e=e, el=el, cd=cd, qs=qs, kb=kb, a=a, tm=tm,
                u=u, w=w, p=p, qd=qs * e, kd=k * el)


def _lane_pick(x, lane):
    l = lax.broadcasted_iota(jnp.int32, x.shape, 1)
    return jnp.sum(jnp.where(l == lane, x, 0.0), axis=1, keepdims=True)


def _params(sem, vmem=VMEM_LIMIT):
    return pltpu.CompilerParams(dimension_semantics=sem, vmem_limit_bytes=vmem)


def _row_call(name, body, t, tm, row_ins, full_ins, row_outs, acc_outs):
    in_specs = [pl.BlockSpec((tm, w), functools.partial(lambda i, c: (i, c), c=c)) for (_, w, c) in row_ins]
    for a in full_ins:
        in_specs.append(pl.BlockSpec(a.shape, functools.partial(lambda i, n: (0,) * n, n=a.ndim)))
    out_specs = [pl.BlockSpec((tm, w), lambda i: (i, 0)) for (w, _) in row_outs]
    out_shape = [jax.ShapeDtypeStruct((t, w), dt) for (w, dt) in row_outs]
    for shp, dt in acc_outs:
        out_specs.append(pl.BlockSpec(shp, functools.partial(lambda i, n: (0,) * n, n=len(shp))))
        out_shape.append(jax.ShapeDtypeStruct(shp, dt))
    return pl.pallas_call(
        body, name=name, grid=(t // tm,), in_specs=in_specs, out_specs=out_specs, out_shape=out_shape,
        compiler_params=_params(("arbitrary",)),
    )(*[a for (a, _, _) in row_ins], *full_ins)


def _k_in(cfg, r, g1, wcat):
    def body(r_ref, g_ref, w_ref, h_ref, proj_ref):
        y, _, _ = _rms_fwd(r_ref[...], g_ref[...])
        hb = y.astype(bf16)
        h_ref[...] = hb
        proj_ref[...] = jnp.dot(hb, w_ref[...], preferred_element_type=f32)

    return _row_call("in_proj", body, cfg.T, cfg.TM, [(r, cfg.D, 0)], [g1, wcat],
                     [(cfg.D, bf16), (cfg.PC, f32)], [])


def _k_prep(cfg, proj, conv_w):
    dn, s = cfg.DN, cfg.S

    def body(x_ref, w_ref, o_ref):
        sec = pl.program_id(1)
        c = _conv_fwd(x_ref[...], w_ref, 2)
        y, _ = _silu(c)
        for h in range(cfg.H):
            yh = y[:, h * 128:(h + 1) * 128]
            nh = yh * lax.rsqrt(jnp.sum(yh * yh, axis=1, keepdims=True) + EPS)
            o_ref[:, h * 128:(h + 1) * 128] = jnp.where(sec < 2, nh, yh)

    return pl.pallas_call(
        body, name="dn_prep", grid=(cfg.BL, 3),
        in_specs=[pl.BlockSpec((s, dn), lambda b, j: (b, j)), pl.BlockSpec((4, dn), lambda b, j: (0, j))],
        out_specs=pl.BlockSpec((s, dn), lambda b, j: (b, j)),
        out_shape=jax.ShapeDtypeStruct((cfg.T, 3 * dn), f32),
        compiler_params=_params(("arbitrary", "arbitrary")),
    )(proj, conv_w)


def _gate_cols(ba, alog_row, dt_row, lane_b, lane_a):
    beta = _sigmoid(_lane_pick(ba, lane_b))
    alpha = _lane_pick(ba, lane_a)
    aexp = jnp.exp(_lane_pick(alog_row, lane_a))
    dtb = _lane_pick(dt_row, lane_a)
    xa = alpha + dtb
    g = -aexp * _softplus(xa)
    return beta, g, aexp, xa


def _k_gdn_fwd(cfg, qkv, proj, alog_row, dt_row):
    s, c, nc, hh = cfg.S, cfg.C, cfg.NC, cfg.H
    ba_blk = cfg.BAO // LANES

    def body(q_ref, k_ref, v_ref, ba_ref, al_ref, dt_ref, o_ref, u_s, w_s, qd_s, kd_s, p_s, cd_s):
        h = pl.program_id(1)
        for d in range(2):
            rev = d == 1
            lane_b = d * hh + h
            lane_a = 2 * hh + d * hh + h

            def prep(n, carry):
                rows = pl.ds(pl.multiple_of(n * c, c), c)
                beta, g, _, _ = _gate_cols(ba_ref[rows, :], al_ref[...], dt_ref[...], lane_b, lane_a)
                z = _gdn_prep(q_ref[rows, :], k_ref[rows, :], v_ref[rows, :], g, beta, rev)
                u_s[rows, :] = z["u"]
                w_s[rows, :] = z["w"]
                qd_s[rows, :] = z["qd"]
                kd_s[rows, :] = z["kd"]
                p_s[n] = z["p"]
                cd_s[n] = jnp.broadcast_to(z["cd"], (8, LANES))
                return carry

            lax.fori_loop(0, nc, prep, 0)

            def step(i, st):
                n = (nc - 1 - i) if rev else i
                rows = pl.ds(pl.multiple_of(n * c, c), c)
                vn = u_s[rows, :] - _mm(w_s[rows, :], st)
                o = _mm(qd_s[rows, :], st) + _mm(p_s[n], vn)
                if rev:
                    o_ref[rows, :] = o_ref[rows, :] + o
                else:
                    o_ref[rows, :] = o
                return st * cd_s[n][0:1, :] + _mm_tn(kd_s[rows, :], vn)

            lax.fori_loop(0, nc, step, jnp.zeros((128, 128), f32))

    blk = lambda off: pl.BlockSpec((s, 128), functools.partial(lambda b, h, off: (b, off + h), off=off))
    row = pl.BlockSpec((1, LANES), lambda b, h: (0, 0))
    return pl.pallas_call(
        body, name="gdn_fwd", grid=(cfg.BL, hh),
        in_specs=[blk(0), blk(hh), blk(2 * hh), pl.BlockSpec((s, LANES), lambda b, h: (b, ba_blk)), row, row],
        out_specs=pl.BlockSpec((s, 128), lambda b, h: (b, h)),
        out_shape=jax.ShapeDtypeStruct((cfg.T, cfg.DN), f32),
        scratch_shapes=[pltpu.VMEM((s, 128), f32)] * 4 + [pltpu.VMEM((nc, c, c), f32), pltpu.VMEM((nc, 8, LANES), f32)],
        compiler_params=_params(("arbitrary", "arbitrary")),
    )(qkv, qkv, qkv, proj, alog_row, dt_row)


def _lru_dir_fwd(xc, wa, wx, ba, bx, lam, rev):
    ra = _sigmoid(_mm(xc, wa) + ba)
    ig = _sigmoid(_mm(xc, wx) + bx)
    sp = _softplus(-lam)
    la = -LRU_C * ra * sp
    a = jnp.exp(la)
    m = jnp.sqrt(_nexpm1(2.0 * la))
    gx = ig * xc
    h = _lin_scan(a, m * gx, rev)
    return dict(ra=ra, ig=ig, sp=sp, a=a, m=m, gx=gx, h=h)


def _lru_specs(cfg, outer_b):
    s = cfg.S
    lx_blk = cfg.LXO // LANES
    if outer_b:
        ix = lambda f: (lambda b, ct: f(b, ct))
    else:
        ix = lambda f: (lambda ct, b: f(b, ct))
    return dict(
        lx=pl.BlockSpec((s, LANES), ix(lambda b, ct: (b, lx_blk + ct))),
        tok=pl.BlockSpec((s, LANES), ix(lambda b, ct: (b, ct))),
        cw=pl.BlockSpec((4, LANES), ix(lambda b, ct: (0, ct))),
        row=pl.BlockSpec((1, LANES), ix(lambda b, ct: (0, ct))),
        w=pl.BlockSpec((2, 1, LANES, LANES), ix(lambda b, ct: (0, ct, 0, 0))),
        two=pl.BlockSpec((2, LANES), ix(lambda b, ct: (0, ct))),
    )


def _k_lru_fwd(cfg, proj, conv_w, conv_b, wa, wx, ba, bx, lam):
    def body(lx_ref, cw_ref, cb_ref, wa_ref, wx_ref, ba_ref, bx_ref, lam_ref, o_ref):
        xc = _conv_fwd(lx_ref[...], cw_ref, 2) + cb_ref[...]
        acc = None
        for d in range(2):
            z = _lru_dir_fwd(xc, wa_ref[d, 0], wx_ref[d, 0], ba_ref[d:d + 1, :], bx_ref[d:d + 1, :], lam_ref[d:d + 1, :], d == 1)
            acc = z["h"] if acc is None else acc + z["h"]
        o_ref[...] = acc

    sp = _lru_specs(cfg, True)
    return pl.pallas_call(
        body, name="lru_fwd", grid=(cfg.BL, cfg.LW // LANES),
        in_specs=[sp["lx"], sp["cw"], sp["row"], sp["w"], sp["w"], sp["two"], sp["two"], sp["two"]],
        out_specs=sp["tok"], out_shape=jax.ShapeDtypeStruct((cfg.T, cfg.LW), f32),
        compiler_params=_params(("arbitrary", "arbitrary")),
    )(proj, conv_w, conv_b, wa, wx, ba, bx, lam)


def _mix_parts(cfg, o, z, lg, hs, dng, lrg):
    heads = []
    for h in range(cfg.H):
        sl = slice(h * 128, (h + 1) * 128)
        y, xh, inv = _rms_fwd(o[:, sl], dng)
        sz, dsz = _silu(z[:, sl])
        heads.append((y, xh, inv, sz, dsz))
    gl, dgl = _gelu(lg)
    y2, xh2, inv2 = _rms_fwd(gl * hs, lrg)
    return heads, (y2, xh2, inv2, gl, dgl)


def _k_mix(cfg, o, proj, hs, r, dng, lrg, wout):
    dn = cfg.DN

    def body(o_ref, z_ref, lg_ref, hs_ref, r_ref, dng_ref, lrg_ref, w_ref, mix_ref, out_ref):
        heads, lru = _mix_parts(cfg, o_ref[...], z_ref[...], lg_ref[...], hs_ref[...], dng_ref[...], lrg_ref[...])
        for h, (y, _, _, sz, _) in enumerate(heads):
            mix_ref[:, h * 128:(h + 1) * 128] = (y * sz).astype(bf16)
        mix_ref[:, dn:] = lru[0].astype(bf16)
        out_ref[...] = r_ref[...] + jnp.dot(mix_ref[...], w_ref[...], preferred_element_type=f32)

    return _row_call("mix_out", body, cfg.T, cfg.TM,
                     [(o, dn, 0), (proj, dn, cfg.ZO // dn), (proj, cfg.LW, cfg.LGO // cfg.LW), (hs, cfg.LW, 0), (r, cfg.D, 0)],
                     [dng, lrg, wout], [(cfg.D, bf16), (cfg.D, f32)], [])


def _k_ffn_a(cfg, r, g2, wg, wu):
    def body(r_ref, g_ref, wg_ref, wu_ref, h_ref, gp_ref, up_ref):
        y, _, _ = _rms_fwd(r_ref[...], g_ref[...])
        hb = y.astype(bf16)
        h_ref[...] = hb
        gp_ref[...] = jnp.dot(hb, wg_ref[...], preferred_element_type=f32)
        up_ref[...] = jnp.dot(hb, wu_ref[...], preferred_element_type=f32)

    return _row_call("ffn_in", body, cfg.T, cfg.TMF, [(r, cfg.D, 0)], [g2, wg, wu],
                     [(cfg.D, bf16), (cfg.FF, f32), (cfg.FF, f32)], [])


def _k_ffn_b(cfg, gp, up, conv_w, conv_b):
    s, ft = cfg.S, cfg.FT

    def body(gp_ref, up_ref, w_ref, b_ref, o_ref):
        gate = _conv_fwd(gp_ref[...], w_ref, 1) + b_ref[...]
        gl, _ = _gelu(gate)
        o_ref[...] = (gl * up_ref[...]).astype(bf16)

    tok = pl.BlockSpec((s, ft), lambda b, j: (b, j))
    return pl.pallas_call(
        body, name="ffn_act", grid=(cfg.BL, cfg.FF // ft),
        in_specs=[tok, tok, pl.BlockSpec((3, ft), lambda b, j: (0, j)), pl.BlockSpec((1, ft), lambda b, j: (0, j))],
        out_specs=tok, out_shape=jax.ShapeDtypeStruct((cfg.T, cfg.FF), bf16),
        compiler_params=_params(("arbitrary", "arbitrary")),
    )(gp, up, conv_w, conv_b)


def _k_ffn_c(cfg, act, r, wd):
    def body(a_ref, r_ref, w_ref, o_ref):
        o_ref[...] = r_ref[...] + jnp.dot(a_ref[...], w_ref[...], preferred_element_type=f32)

    return _row_call("ffn_out", body, cfg.T, cfg.TM, [(act, cfg.FF, 0), (r, cfg.D, 0)], [wd], [(cfg.D, f32)], [])[0]


def _k_ple(cfg, r, p, gp, wpg, bg, wpp):
    def body(r_ref, p_ref, g_ref, wg_ref, bg_ref, wp_ref, pn_ref, o_ref):
        x = r_ref[...]
        y, _, _ = _rms_fwd(x, g_ref[...])
        pn = y.astype(bf16)
        pn_ref[...] = pn
        pg = _sigmoid(jnp.dot(pn, wg_ref[...], preferred_element_type=f32) + bg_ref[...])
        o_ref[...] = x + pg * _mm(p_ref[...], wp_ref[...])

    return _row_call("ple", body, cfg.T, cfg.TM, [(r, cfg.D, 0), (p, cfg.PD, 0)], [gp, wpg, bg, wpp],
                     [(cfg.D, bf16), (cfg.D, f32)], [])


def _k_loss(cfg, r, tgt, gf):
    d = cfg.D

    def body(r_ref, t_ref, g_ref, dr_ref, loss_ref, dg_ref):
        @pl.when(pl.program_id(0) == 0)
        def _():
            loss_ref[...] = jnp.zeros_like(loss_ref)
            dg_ref[...] = jnp.zeros_like(dg_ref)

        g = g_ref[...]
        y, xh, inv = _rms_fwd(r_ref[...], g)
        err = y - t_ref[...]
        loss_ref[...] = loss_ref[...] + (0.5 / d) * jnp.sum(err * err)
        dx, dg = _rms_bwd(err * (1.0 / d), xh, inv, g)
        dr_ref[...] = dx
        dg_ref[...] = dg_ref[...] + dg

    return _row_call("loss_head", body, cfg.T, cfg.TM, [(r, d, 0), (tgt, d, 0)], [gf], [(d, f32)],
                     [((1, LANES), f32), ((1, d), f32)])


def _zero_at_first(cond, *refs):
    @pl.when(cond)
    def _():
        for r in refs:
            r[...] = jnp.zeros_like(r)


def _b_ple(cfg, dr3, r2, p, gp, wpg, bg, wpp):
    d = cfg.D

    def body(dr_ref, r_ref, p_ref, g_ref, wg_ref, bg_ref, wp_ref, dr2_ref, dlog_ref, dpp_ref, dgp_ref, dbg_ref):
        _zero_at_first(pl.program_id(0) == 0, dgp_ref, dbg_ref)
        g = g_ref[...]
        dr = dr_ref[...]
        y, xh, inv = _rms_fwd(r_ref[...], g)
        pg = _sigmoid(jnp.dot(y.astype(bf16), wg_ref[...], preferred_element_type=f32) + bg_ref[...])
        pp = _mm(p_ref[...], wp_ref[...])
        dpp_ref[...] = (dr * pg).astype(bf16)
        dlog = dr * pp * pg * (1.0 - pg)
        dlog_ref[...] = dlog.astype(bf16)
        dbg_ref[...] = dbg_ref[...] + jnp.sum(dlog, axis=0, keepdims=True)
        dx, dg = _rms_bwd(_mm_nt(dlog, wg_ref[...]), xh, inv, g)
        dgp_ref[...] = dgp_ref[...] + dg
        dr2_ref[...] = dr + dx

    return _row_call("ple_bwd", body, cfg.T, cfg.TM, [(dr3, d, 0), (r2, d, 0), (p, cfg.PD, 0)], [gp, wpg, bg, wpp],
                     [(d, f32), (d, bf16), (d, bf16)], [((1, d), f32), ((1, d), f32)])


def _b_ffn_bc(cfg, dr2, wd, gp, up, conv_w, conv_b):
    s, ft, d = cfg.S, cfg.FT, cfg.D

    def body(dr_ref, wd_ref, gp_ref, up_ref, w_ref, b_ref, dgp_ref, dup_ref, dwd_ref, dcw_ref, dcb_ref):
        _zero_at_first(pl.program_id(1) == 0, dwd_ref, dcw_ref, dcb_ref)
        drb = dr_ref[...].astype(bf16)
        dact = _mm_nt(drb, wd_ref[...])
        gpre = gp_ref[...]
        up = up_ref[...]
        gl, dgl = _gelu(_conv_fwd(gpre, w_ref, 1) + b_ref[...])
        dup_ref[...] = (dact * gl).astype(bf16)
        dgate = dact * up * dgl
        dcb_ref[...] = dcb_ref[...] + jnp.sum(dgate, axis=0, keepdims=True)
        dx, dws = _conv_bwd(dgate, gpre, w_ref, 1)
        for j, dw in enumerate(dws):
            dcw_ref[j:j + 1, :] = dcw_ref[j:j + 1, :] + dw
        dgp_ref[...] = dx.astype(bf16)
        dwd_ref[...] = dwd_ref[...] + _mm_tn((gl * up).astype(bf16), drb)

    tok = pl.BlockSpec((s, ft), lambda j, b: (b, j))
    return pl.pallas_call(
        body, name="ffn_act_bwd", grid=(cfg.FF // ft, cfg.BL),
        in_specs=[pl.BlockSpec((s, d), lambda j, b: (b, 0)), pl.BlockSpec((ft, d), lambda j, b: (j, 0)), tok, tok,
                  pl.BlockSpec((3, ft), lambda j, b: (0, j)), pl.BlockSpec((1, ft), lambda j, b: (0, j))],
        out_specs=[tok, tok, pl.BlockSpec((ft, d), lambda j, b: (j, 0)), pl.BlockSpec((3, ft), lambda j, b: (0, j)),
                   pl.BlockSpec((1, ft), lambda j, b: (0, j))],
        out_shape=[jax.ShapeDtypeStruct((cfg.T, cfg.FF), bf16), jax.ShapeDtypeStruct((cfg.T, cfg.FF), bf16),
                   jax.ShapeDtypeStruct((cfg.FF, d), f32), jax.ShapeDtypeStruct((3, cfg.FF), f32),
                   jax.ShapeDtypeStruct((1, cfg.FF), f32)],
        compiler_params=_params(("arbitrary", "arbitrary")),
    )(dr2, wd, gp, up, conv_w, conv_b)


def _b_ffn_a(cfg, dgp, dup, dr2, r_mid, g2, wg, wu):
    d = cfg.D

    def body(dgp_ref, dup_ref, dr_ref, r_ref, g_ref, wg_ref, wu_ref, o_ref, dg_ref):
        _zero_at_first(pl.program_id(0) == 0, dg_ref)
        g = g_ref[...]
        _, xh, inv = _rms_fwd(r_ref[...], g)
        dh = _mm_nt(dgp_ref[...], wg_ref[...]) + _mm_nt(dup_ref[...], wu_ref[...])
        dx, dg = _rms_bwd(dh, xh, inv, g)
        dg_ref[...] = dg_ref[...] + dg
        o_ref[...] = dr_ref[...] + dx

    return _row_call("ffn_in_bwd", body, cfg.T, cfg.TMF, [(dgp, cfg.FF, 0), (dup, cfg.FF, 0), (dr2, d, 0), (r_mid, d, 0)],
                     [g2, wg, wu], [(d, f32)], [((1, d), f32)])


def _mm_tn_call(cfg, name, x, dy, tn):
    t, k = x.shape
    n = dy.shape[1]
    tm = cfg.TM

    def body(x_ref, dy_ref, o_ref):
        _zero_at_first(pl.program_id(1) == 0, o_ref)
        o_ref[...] = o_ref[...] + _mm_tn(x_ref[...], dy_ref[...])

    return pl.pallas_call(
        body, name=name, grid=(n // tn, t // tm),
        in_specs=[pl.BlockSpec((tm, k), lambda j, i: (i, 0)), pl.BlockSpec((tm, tn), lambda j, i: (i, j))],
        out_specs=pl.BlockSpec((k, tn), lambda j, i: (0, j)),
        out_shape=jax.ShapeDtypeStruct((k, n), f32),
        compiler_params=_params(("arbitrary", "arbitrary")),
    )(x, dy)


def _b_mix(cfg, dr, o, proj, hs, dng, lrg, wout):
    dn, lw, d = cfg.DN, cfg.LW, cfg.D

    def body(dr_ref, o_ref, z_ref, lg_ref, hs_ref, dng_ref, lrg_ref, w_ref, do_ref, dz_ref, dlg_ref, dhs_ref, ddn_ref, dlr_ref):
        _zero_at_first(pl.program_id(0) == 0, ddn_ref, dlr_ref)
        dng, lrg = dng_ref[...], lrg_ref[...]
        hs = hs_ref[...]
        o, z = o_ref[...], z_ref[...]
        heads, lru = _mix_parts(cfg, o, z, lg_ref[...], hs, dng, lrg)
        drb = dr_ref[...].astype(bf16)
        dmix_dn = _mm_nt(drb, w_ref[0:dn, :])
        dmix_lr = _mm_nt(drb, w_ref[dn:, :])
        dgn = jnp.zeros_like(dng)
        for h, (y, xh, inv, sz, dsz) in enumerate(heads):
            sl = slice(h * 128, (h + 1) * 128)
            dm = dmix_dn[:, sl]
            dz_ref[:, sl] = dm * y * dsz
            dx, dg = _rms_bwd(dm * sz, xh, inv, dng)
            do_ref[:, sl] = dx
            dgn = dgn + dg
        ddn_ref[...] = ddn_ref[...] + dgn
        _, xh2, inv2, gl, dgl = lru
        dx2, dg2 = _rms_bwd(dmix_lr, xh2, inv2, lrg)
        dlr_ref[...] = dlr_ref[...] + dg2
        dlg_ref[...] = dx2 * hs * dgl
        dhs_ref[...] = dx2 * gl

    return _row_call("mix_bwd", body, cfg.T, cfg.TM,
                     [(dr, d, 0), (o, dn, 0), (proj, dn, cfg.ZO // dn), (proj, lw, cfg.LGO // lw), (hs, lw, 0)],
                     [dng, lrg, wout], [(dn, f32), (dn, f32), (lw, f32), (lw, f32)], [((1, 128), f32), ((1, lw), f32)])


def _b_lru(cfg, proj, dhs, conv_w, conv_b, wa, wx, ba, bx, lam):
    def body(lx_ref, dh_ref, cw_ref, cb_ref, wa_ref, wx_ref, ba_ref, bx_ref, lam_ref,
             dlx_ref, dcw_ref, dcb_ref, dwa_ref, dwx_ref, dba_ref, dbx_ref, dlam_ref):
        _zero_at_first(pl.program_id(1) == 0, dcw_ref, dcb_ref, dwa_ref, dwx_ref, dba_ref, dbx_ref, dlam_ref)
        lx = lx_ref[...]
        dh = dh_ref[...]
        xc = _conv_fwd(lx, cw_ref, 2) + cb_ref[...]
        dxc = jnp.zeros_like(xc)
        for d in range(2):
            rev = d == 1
            lam = lam_ref[d:d + 1, :]
            z = _lru_dir_fwd(xc, wa_ref[d, 0], wx_ref[d, 0], ba_ref[d:d + 1, :], bx_ref[d:d + 1, :], lam, rev)
            a, m, ra, ig, sp = z["a"], z["m"], z["ra"], z["ig"], z["sp"]
            a_next = _shift(a, -1 if rev else 1, 0.0)
            lmb = _lin_scan(a_next, dh, not rev)
            h_prev = _shift(z["h"], 1 if rev else -1, 0.0)
            da = lmb * h_prev
            dm = lmb * z["gx"]
            dgx = lmb * m
            dla = da * a - dm * (a * a) / jnp.maximum(m, 1e-30)
            dra = dla * (-LRU_C) * sp
            dsp = jnp.sum(dla * (-LRU_C) * ra, axis=0, keepdims=True)
            dlam_ref[d:d + 1, :] = dlam_ref[d:d + 1, :] - dsp * _sigmoid(-lam)
            dpa = dra * ra * (1.0 - ra)
            dpx = dgx * xc * ig * (1.0 - ig)
            dba_ref[d:d + 1, :] = dba_ref[d:d + 1, :] + jnp.sum(dpa, axis=0, keepdims=True)
            dbx_ref[d:d + 1, :] = dbx_ref[d:d + 1, :] + jnp.sum(dpx, axis=0, keepdims=True)
            dwa_ref[d, 0] = dwa_ref[d, 0] + _mm_tn(xc, dpa)
            dwx_ref[d, 0] = dwx_ref[d, 0] + _mm_tn(xc, dpx)
            dxc = dxc + dgx * ig + _mm_nt(dpa, wa_ref[d, 0]) + _mm_nt(dpx, wx_ref[d, 0])
        dcb_ref[...] = dcb_ref[...] + jnp.sum(dxc, axis=0, keepdims=True)
        dx, dws = _conv_bwd(dxc, lx, cw_ref, 2)
        for j, dw in enumerate(dws):
            dcw_ref[j:j + 1, :] = dcw_ref[j:j + 1, :] + dw
        dlx_ref[...] = dx

    sp = _lru_specs(cfg, False)
    nct = cfg.LW // LANES
    return pl.pallas_call(
        body, name="lru_bwd", grid=(nct, cfg.BL),
        in_specs=[sp["lx"], sp["tok"], sp["cw"], sp["row"], sp["w"], sp["w"], sp["two"], sp["two"], sp["two"]],
        out_specs=[sp["tok"], sp["cw"], sp["row"], sp["w"], sp["w"], sp["two"], sp["two"], sp["two"]],
        out_shape=[jax.ShapeDtypeStruct((cfg.T, cfg.LW), f32), jax.ShapeDtypeStruct((4, cfg.LW), f32),
                   jax.ShapeDtypeStruct((1, cfg.LW), f32), jax.ShapeDtypeStruct((2, nct, LANES, LANES), f32),
                   jax.ShapeDtypeStruct((2, nct, LANES, LANES), f32), jax.ShapeDtypeStruct((2, cfg.LW), f32),
                   jax.ShapeDtypeStruct((2, cfg.LW), f32), jax.ShapeDtypeStruct((2, cfg.LW), f32)],
        compiler_params=_params(("arbitrary", "arbitrary")),
    )(proj, dhs, conv_w, conv_b, wa, wx, ba, bx, lam)


def _b_gdn(cfg, qkv, proj, do, alog_row, dt_row):
    s, c, nc, hh = cfg.S, cfg.C, cfg.NC, cfg.H
    ba_blk = cfg.BAO // LANES

    def body(q_ref, k_ref, v_ref, ba_ref, do_ref, al_ref, dt_ref, dqkv_ref, dba_ref, dal_ref, ddt_ref,
             u_s, w_s, qd_s, kd_s, vn_s, p_s, t_s, cd_s, st_s):
        h = pl.program_id(1)
        _zero_at_first((pl.program_id(0) == 0) & (h == 0), dal_ref, ddt_ref)
        _zero_at_first(h == 0, dba_ref)
        lane = lax.broadcasted_iota(jnp.int32, (c, LANES), 1)
        lane1 = lax.broadcasted_iota(jnp.int32, (1, LANES), 1)
        for d in range(2):
            rev = d == 1
            lane_b = d * hh + h
            lane_a = 2 * hh + d * hh + h

            def gates(rows):
                return _gate_cols(ba_ref[rows, :], al_ref[...], dt_ref[...], lane_b, lane_a)

            def prep(n, carry):
                rows = pl.ds(pl.multiple_of(n * c, c), c)
                beta, g, _, _ = gates(rows)
                z = _gdn_prep(q_ref[rows, :], k_ref[rows, :], v_ref[rows, :], g, beta, rev)
                u_s[rows, :] = z["u"]
                w_s[rows, :] = z["w"]
                qd_s[rows, :] = z["qd"]
                kd_s[rows, :] = z["kd"]
                p_s[n] = z["p"]
                t_s[n] = z["tm"]
                cd_s[n] = jnp.broadcast_to(z["cd"], (8, LANES))
                return carry

            lax.fori_loop(0, nc, prep, 0)

            def fstep(i, st):
                n = (nc - 1 - i) if rev else i
                rows = pl.ds(pl.multiple_of(n * c, c), c)
                st_s[n] = st
                vn = u_s[rows, :] - _mm(w_s[rows, :], st)
                vn_s[rows, :] = vn
                return st * cd_s[n][0:1, :] + _mm_tn(kd_s[rows, :], vn)

            lax.fori_loop(0, nc, fstep, jnp.zeros((128, 128), f32))

            def bstep(i, dst):
                n = i if rev else (nc - 1 - i)
                rows = pl.ds(pl.multiple_of(n * c, c), c)
                q, k, v = q_ref[rows, :], k_ref[rows, :], v_ref[rows, :]
                beta, g, aexp, xa = gates(rows)
                incl, strict, incl_t = _tri_masks(c, rev)
                incltf = incl_t.astype(f32)
                st = st_s[n]
                vn, u, w, qd, kd, p, tm = vn_s[rows, :], u_s[rows, :], w_s[rows, :], qd_s[rows, :], kd_s[rows, :], p_s[n], t_s[n]
                cd = cd_s[n][0:1, 0:1]
                dob = do_ref[rows, :]
                dvn = _mm_tn(p, dob) + _mm(kd, dst)
                dp = jnp.where(incl, _mm_nt(dob, vn), 0.0)
                dqd = _mm_nt(dob, st)
                dkd = _mm_nt(vn, dst)
                dw = -_mm_nt(dvn, st)
                dcd = jnp.sum(jnp.sum(st * dst, axis=1, keepdims=True), axis=0, keepdims=True)
                dst_new = _mm_tn(qd, dob) + cd * dst - _mm_tn(w, dvn)
                gb = jnp.broadcast_to(g, (c, c))
                mcol = _hmm(incl.astype(f32), gb)
                mrow = _hmm(jnp.ones((c, c), f32), gb * incltf)
                gcum = mcol[:, 0:1]
                dec = jnp.exp(jnp.where(incl, mcol - mrow, -1e30))
                glast = jnp.sum(g, axis=0, keepdims=True)
                e = jnp.exp(gcum)
                el = jnp.exp(glast - gcum)
                qs = q * (128 ** -0.5)
                kb = k * beta
                a = jnp.where(strict, _mm_nt(kb, k) * dec, 0.0)
                dvb = _hmm_tn(tm, dvn)
                dkbe = _hmm_tn(tm, dw)
                da = -jnp.where(strict, _hmm_nt(dvb, u) + _hmm_nt(dkbe, w), 0.0)
                g1 = da * dec
                g2 = dp * dec
                dkb = _mm(g1, k) + dkbe * e
                dk = _mm_tn(g1, kb) + _mm_tn(g2, qs) + dkb * beta + dkd * el
                dqs = _mm(g2, k) + dqd * e
                ddd = da * a + dp * p
                ones = jnp.ones((c, LANES), f32)
                dgcum = (_hmm(ddd, ones) - _hmm_tn(ddd, ones))[:, 0:1]
                dbeta = jnp.sum(dvb * v, axis=1, keepdims=True) + jnp.sum(dkb * k, axis=1, keepdims=True)
                de = jnp.sum(dkbe * kb, axis=1, keepdims=True) + jnp.sum(dqd * qs, axis=1, keepdims=True)
                del_ = jnp.sum(dkd * k, axis=1, keepdims=True)
                dgcum = dgcum + de * e - del_ * el
                dglast = jnp.sum(del_ * el, axis=0, keepdims=True) + dcd * cd
                dg = _hmm(incltf, jnp.broadcast_to(dgcum, (c, LANES)))[:, 0:1] + dglast
                dqkv_ref[0, rows, :] = (dqs * (128 ** -0.5)) if not rev else dqkv_ref[0, rows, :] + dqs * (128 ** -0.5)
                dqkv_ref[1, rows, :] = dk if not rev else dqkv_ref[1, rows, :] + dk
                dqkv_ref[2, rows, :] = (dvb * beta) if not rev else dqkv_ref[2, rows, :] + dvb * beta
                dlb = dbeta * beta * (1.0 - beta)
                dalpha = -dg * aexp * _sigmoid(xa)
                dba_ref[rows, :] = dba_ref[rows, :] + jnp.where(lane == lane_b, dlb, 0.0) + jnp.where(lane == lane_a, dalpha, 0.0)
                dal_ref[...] = dal_ref[...] + jnp.where(lane1 == lane_a, jnp.sum(dg * g, axis=0, keepdims=True), 0.0)
                ddt_ref[...] = ddt_ref[...] + jnp.where(lane1 == lane_a, jnp.sum(dalpha, axis=0, keepdims=True), 0.0)
                return dst_new

            lax.fori_loop(0, nc, bstep, jnp.zeros((128, 128), f32))

    blk = lambda off: pl.BlockSpec((s, 128), functools.partial(lambda b, h, off: (b, off + h), off=off))
    row = pl.BlockSpec((1, LANES), lambda b, h: (0, 0))
    return pl.pallas_call(
        body, name="gdn_bwd", grid=(cfg.BL, hh),
        in_specs=[blk(0), blk(hh), blk(2 * hh), pl.BlockSpec((s, LANES), lambda b, h: (b, ba_blk)),
                  pl.BlockSpec((s, 128), lambda b, h: (b, h)), row, row],
        out_specs=[pl.BlockSpec((3, s, 128), lambda b, h: (0, b, h)), pl.BlockSpec((s, LANES), lambda b, h: (b, 0)), row, row],
        out_shape=[jax.ShapeDtypeStruct((3, cfg.T, cfg.DN), f32), jax.ShapeDtypeStruct((cfg.T, LANES), f32),
                   jax.ShapeDtypeStruct((1, LANES), f32), jax.ShapeDtypeStruct((1, LANES), f32)],
        scratch_shapes=[pltpu.VMEM((s, 128), f32)] * 5 + [pltpu.VMEM((nc, c, c), f32)] * 2
        + [pltpu.VMEM((nc, 8, LANES), f32), pltpu.VMEM((nc, 128, 128), f32)],
        compiler_params=_params(("arbitrary", "arbitrary")),
    )(qkv, qkv, qkv, proj, do, alog_row, dt_row)


def _b_prep(cfg, proj, dqkv, conv_w):
    dn, s = cfg.DN, cfg.S

    def body(x_ref, dy_ref, w_ref, dx_ref, dw_ref):
        _zero_at_first(pl.program_id(1) == 0, dw_ref)
        sec = pl.program_id(0)
        x = x_ref[...]
        c = _conv_fwd(x, w_ref, 2)
        y, dsilu = _silu(c)
        dy = dy_ref[0]
        parts = []
        for h in range(cfg.H):
            sl = slice(h * 128, (h + 1) * 128)
            yh, dyh = y[:, sl], dy[:, sl]
            inv = lax.rsqrt(jnp.sum(yh * yh, axis=1, keepdims=True) + EPS)
            dn_h = inv * dyh - yh * (inv * inv * inv) * jnp.sum(dyh * yh, axis=1, keepdims=True)
            parts.append(jnp.where(sec < 2, dn_h, dyh))
        ds = jnp.concatenate(parts, axis=1) if len(parts) > 1 else parts[0]
        dx, dws = _conv_bwd(ds * dsilu, x, w_ref, 2)
        for j, dw in enumerate(dws):
            dw_ref[j:j + 1, :] = dw_ref[j:j + 1, :] + dw
        dx_ref[...] = dx

    return pl.pallas_call(
        body, name="dn_prep_bwd", grid=(3, cfg.BL),
        in_specs=[pl.BlockSpec((s, dn), lambda j, b: (b, j)), pl.BlockSpec((1, s, dn), lambda j, b: (j, b, 0)),
                  pl.BlockSpec((4, dn), lambda j, b: (0, j))],
        out_specs=[pl.BlockSpec((s, dn), lambda j, b: (b, j)), pl.BlockSpec((4, dn), lambda j, b: (0, j))],
        out_shape=[jax.ShapeDtypeStruct((cfg.T, 3 * dn), f32), jax.ShapeDtypeStruct((4, 3 * dn), f32)],
        compiler_params=_params(("arbitrary", "arbitrary")),
    )(proj, dqkv, conv_w)


def _b_in(cfg, dqkv_pre, dz, dlx, dlg, dba, dr_mid, r_in, g1, wcat):
    d, dn, lw = cfg.D, cfg.DN, cfg.LW

    def body(dq_ref, dz_ref, dlx_ref, dlg_ref, dba_ref, dr_ref, r_ref, g_ref, w_ref, o_ref, dp_ref, dg_ref):
        _zero_at_first(pl.program_id(0) == 0, dg_ref)
        dp_ref[:, 0:cfg.ZO] = dq_ref[...].astype(bf16)
        dp_ref[:, cfg.ZO:cfg.LXO] = dz_ref[...].astype(bf16)
        dp_ref[:, cfg.LXO:cfg.LGO] = dlx_ref[...].astype(bf16)
        dp_ref[:, cfg.LGO:cfg.BAO] = dlg_ref[...].astype(bf16)
        dp_ref[:, cfg.BAO:] = dba_ref[...].astype(bf16)
        g = g_ref[...]
        _, xh, inv = _rms_fwd(r_ref[...], g)
        dx, dg = _rms_bwd(_mm_nt(dp_ref[...], w_ref[...]), xh, inv, g)
        dg_ref[...] = dg_ref[...] + dg
        o_ref[...] = dr_ref[...] + dx

    return _row_call("in_proj_bwd", body, cfg.T, cfg.TM,
                     [(dqkv_pre, 3 * dn, 0), (dz, dn, 0), (dlx, lw, 0), (dlg, lw, 0), (dba, LANES, 0), (dr_mid, d, 0), (r_in, d, 0)],
                     [g1, wcat], [(d, f32), (cfg.PC, bf16)], [((1, d), f32)])


def _adam_call(name, w, g, m, v, tr):
    rows, cols = w.shape
    bc1 = 1.0 - ADAM_B1 ** ADAM_STEP
    bc2 = 1.0 - ADAM_B2 ** ADAM_STEP

    def body(w_ref, g_ref, m_ref, v_ref, d_ref, nm_ref, nv_ref):
        g = g_ref[...]
        m = ADAM_B1 * m_ref[...] + (1.0 - ADAM_B1) * g
        v = ADAM_B2 * v_ref[...] + (1.0 - ADAM_B2) * (g * g)
        nm_ref[...] = m
        nv_ref[...] = v
        d_ref[...] = -ADAM_LR * ((m / bc1) / (jnp.sqrt(v / bc2) + ADAM_EPS) + ADAM_WD * w_ref[...])

    spec = pl.BlockSpec((tr, cols), lambda i: (i, 0))
    return pl.pallas_call(
        body, name=name, grid=(rows // tr,), in_specs=[spec] * 4, out_specs=[spec] * 3,
        out_shape=[jax.ShapeDtypeStruct((rows, cols), f32)] * 3, compiler_params=_params(("arbitrary",)),
    )(w, g, m, v)


def _sum8_call(name, x, tr):
    _, rows, cols = x.shape

    def body(x_ref, o_ref):
        acc = x_ref[0].astype(f32)
        for j in range(1, N_DEV):
            acc = acc + x_ref[j].astype(f32)
        o_ref[...] = acc

    return pl.pallas_call(
        body, name=name, grid=(rows // tr,), in_specs=[pl.BlockSpec((N_DEV, tr, cols), lambda i: (0, i, 0))],
        out_specs=pl.BlockSpec((tr, cols), lambda i: (i, 0)), out_shape=jax.ShapeDtypeStruct((rows, cols), f32),
        compiler_params=_params(("arbitrary",)),
    )(x)


def _all_gather(name, shard):
    rows, cols = shard.shape

    def body(x_ref, out_ref, send_sems, recv_sems, local_sem):
        x, y, c = lax.axis_index("x"), lax.axis_index("y"), lax.axis_index("c")
        me, sibling = (x, y, c), (x, y, 1 - c)
        chips = [(1 - x, y), (x, 1 - y), (1 - x, 1 - y)]

        def slot(px, py, pc):
            return out_ref.at[4 * px + 2 * py + pc]

        def copy(k, block, to, src=None):
            return pltpu.make_async_remote_copy(
                src_ref=slot(*block) if src is None else src, dst_ref=slot(*block),
                send_sem=send_sems.at[k], recv_sem=recv_sems.at[k], device_id=to, device_id_type=MESH)

        mine = pltpu.make_async_copy(x_ref, slot(*me), local_sem)
        mine.start()
        first = [copy(0, me, sibling, src=x_ref)]
        first += [copy(1 + j, me, (*chip, c), src=x_ref) for j, chip in enumerate(chips)]
        for cp in first:
            cp.start()
        passed = [copy(4 + j, (*chip, c), sibling) for j, chip in enumerate(chips)]
        for j, chip in enumerate(chips):
            copy(1 + j, (*chip, c), me).wait_recv()
            passed[j].start()
        copy(0, sibling, me).wait_recv()
        for j, chip in enumerate(chips):
            copy(4 + j, (*chip, 1 - c), me).wait_recv()
        for cp in first + passed:
            cp.wait_send()
        mine.wait()

    return pl.pallas_call(
        body, name=name, out_shape=jax.ShapeDtypeStruct((N_DEV, rows, cols), shard.dtype),
        in_specs=[pl.BlockSpec(memory_space=pltpu.HBM)], out_specs=pl.BlockSpec(memory_space=pltpu.HBM),
        scratch_shapes=[pltpu.SemaphoreType.DMA((7,)), pltpu.SemaphoreType.DMA((7,)), pltpu.SemaphoreType.DMA],
    )(shard)


def _all_to_all(name, blocks):
    def body(x_ref, out_ref, send_sems, recv_sems, local_sem):
        x, y, c = lax.axis_index("x"), lax.axis_index("y"), lax.axis_index("c")
        me = 4 * x + 2 * y + c
        mine = pltpu.make_async_copy(x_ref.at[me], out_ref.at[me], local_sem)
        mine.start()
        copies = []
        for k in range(1, N_DEV):
            px, py, pc = x ^ (k >> 2), y ^ ((k >> 1) & 1), c ^ (k & 1)
            copies.append(pltpu.make_async_remote_copy(
                src_ref=x_ref.at[4 * px + 2 * py + pc], dst_ref=out_ref.at[me],
                send_sem=send_sems.at[k - 1], recv_sem=recv_sems.at[k - 1], device_id=(px, py, pc), device_id_type=MESH))
        for cp in copies:
            cp.start()
        for cp in copies:
            cp.wait_recv()
        for cp in copies:
            cp.wait_send()
        mine.wait()

    return pl.pallas_call(
        body, name=name, out_shape=jax.ShapeDtypeStruct(blocks.shape, blocks.dtype),
        in_specs=[pl.BlockSpec(memory_space=pltpu.HBM)], out_specs=pl.BlockSpec(memory_space=pltpu.HBM),
        scratch_shapes=[pltpu.SemaphoreType.DMA((7,)), pltpu.SemaphoreType.DMA((7,)), pltpu.SemaphoreType.DMA],
    )(blocks)


def _layer_fwd(cfg, w, r, p):
    h1, proj = _k_in(cfg, r, w["norm1_g"], w["wcat"])
    qkv = _k_prep(cfg, proj, w["dn_conv_w"])
    o = _k_gdn_fwd(cfg, qkv, proj, w["alog_row"], w["dt_row"])
    hs = _k_lru_fwd(cfg, proj, w["lru_conv_w"], w["lru_conv_b"], w["wa"], w["wx"], w["lru_ba"], w["lru_bx"], w["lru_lambda"])
    mix, r_mid = _k_mix(cfg, o, proj, hs, r, w["dn_norm_g"], w["lru_norm_g"], w["w_out"])
    h2, gp, up = _k_ffn_a(cfg, r_mid, w["norm2_g"], w["ffn_wg"], w["ffn_wu"])
    act = _k_ffn_b(cfg, gp, up, w["ffn_conv_w"], w["ffn_conv_b"])
    r2 = _k_ffn_c(cfg, act, r_mid, w["ffn_wd"])
    pn, r3 = _k_ple(cfg, r2, p, w["ple_norm_g"], w["ple_wg"], w["ple_bg"], w["ple_wp"])
    saved = dict(r=r, h1=h1, proj=proj, qkv=qkv, o=o, hs=hs, mix=mix, r_mid=r_mid, h2=h2, gp=gp, up=up, r2=r2, pn=pn, p=p)
    return r3, saved


def _layer_bwd(cfg, w, sv, dr3):
    g = {}
    dr2, dlog, dpp, g["ple_norm_g"], g["ple_bg"] = _b_ple(cfg, dr3, sv["r2"], sv["p"], w["ple_norm_g"], w["ple_wg"], w["ple_bg"], w["ple_wp"])
    g["ple_wg"] = _mm_tn_call(cfg, "d_ple_wg", sv["pn"], dlog, min(512, cfg.D))
    g["ple_wp"] = _mm_tn_call(cfg, "d_ple_wp", sv["p"], dpp, min(512, cfg.D))
    dgp, dup, g["ffn_wd"], g["ffn_conv_w"], g["ffn_conv_b"] = _b_ffn_bc(cfg, dr2, w["ffn_wd"], sv["gp"], sv["up"], w["ffn_conv_w"], w["ffn_conv_b"])
    dr_mid, g["norm2_g"] = _b_ffn_a(cfg, dgp, dup, dr2, sv["r_mid"], w["norm2_g"], w["ffn_wg"], w["ffn_wu"])
    g["ffn_wg"] = _mm_tn_call(cfg, "d_ffn_wg", sv["h2"], dgp, cfg.FT)
    g["ffn_wu"] = _mm_tn_call(cfg, "d_ffn_wu", sv["h2"], dup, cfg.FT)
    do, dz, dlg, dhs, g["dn_norm_g"], g["lru_norm_g"] = _b_mix(cfg, dr_mid, sv["o"], sv["proj"], sv["hs"], w["dn_norm_g"], w["lru_norm_g"], w["w_out"])
    g["w_out"] = _mm_tn_call(cfg, "d_w_out", sv["mix"], dr_mid, min(512, cfg.D))
    dlx, g["lru_conv_w"], g["lru_conv_b"], g["wa"], g["wx"], g["lru_ba"], g["lru_bx"], g["lru_lambda"] = _b_lru(
        cfg, sv["proj"], dhs, w["lru_conv_w"], w["lru_conv_b"], w["wa"], w["wx"], w["lru_ba"], w["lru_bx"], w["lru_lambda"])
    dqkv, dba, g["alog_row"], g["dt_row"] = _b_gdn(cfg, sv["qkv"], sv["proj"], do, w["alog_row"], w["dt_row"])
    dqkv_pre, g["dn_conv_w"] = _b_prep(cfg, sv["proj"], dqkv, w["dn_conv_w"])
    dr, dproj, g["norm1_g"] = _b_in(cfg, dqkv_pre, dz, dlx, dlg, dba, dr_mid, sv["r"], w["norm1_g"], w["wcat"])
    g["wcat"] = _mm_tn_call(cfg, "d_w_in", sv["h1"], dproj, LANES * 5 if cfg.PC % (LANES * 5) == 0 else LANES)
    return dr, g


BIG = ("w_in", "w_out", "ffn_wg", "ffn_wu", "ffn_wd", "ple_wg", "ple_wp")
BIG_AXIS = {"w_in": 2, "w_out": 1, "ffn_wg": 2, "ffn_wu": 2, "ffn_wd": 1, "ple_wg": 1, "ple_wp": 2}
SMALL_SHARDED = ("dn_conv_w", "lru_conv_w", "lru_ba", "lru_bx", "lru_lambda", "ffn_conv_w")
SMALL_REPL = ("norm1_g", "dn_a_log", "dn_dt_bias", "dn_norm_g", "lru_conv_b", "lru_wa", "lru_wx", "lru_norm_g", "norm2_g",
              "ffn_conv_b", "ple_norm_g", "ple_bg", "final_g")
WEIGHTS = ("norm1_g", "w_in", "dn_conv_w", "dn_a_log", "dn_dt_bias", "dn_norm_g", "lru_conv_w", "lru_conv_b", "lru_wa",
           "lru_ba", "lru_wx", "lru_bx", "lru_lambda", "lru_norm_g", "w_out", "norm2_g", "ffn_wg", "ffn_wu", "ffn_conv_w",
           "ffn_conv_b", "ffn_wd", "ple_norm_g", "ple_wg", "ple_bg", "ple_wp", "final_g")


def _pad_rows(flat, cols, mult):
    n = flat.shape[0]
    rows = -(-n // cols)
    rows = -(-rows // mult) * mult
    return jnp.pad(flat, (0, rows * cols - n)).reshape(rows, cols)


def _pack(arrs, cols, mult, dtype):
    return _pad_rows(jnp.concatenate([a.reshape(-1).astype(dtype) for a in arrs]), cols, mult)


def _unpack(flat, shapes):
    out, off = [], 0
    for shp in shapes:
        n = math.prod(shp)
        out.append(flat[off:off + n].reshape(shp))
        off += n
    return out


def _unpack8(g8, shapes, axes):
    out, off = [], 0
    for shp, ax in zip(shapes, axes):
        n = math.prod(shp)
        a = g8[:, off:off + n].reshape((N_DEV,) + tuple(shp))
        a = jnp.moveaxis(a, 0, ax)
        out.append(a.reshape(shp[:ax] + (N_DEV * shp[ax],) + shp[ax + 1:]))
        off += n
    return out


def _to8(full, ax):
    shp = full.shape
    a = full.reshape(shp[:ax] + (N_DEV, shp[ax] // N_DEV) + shp[ax + 1:])
    return jnp.moveaxis(a, ax, 0).reshape(N_DEV, -1)


def _wcat_from_w_in(cfg, w_in):
    nba = 4 * cfg.H
    pad = jnp.zeros(w_in.shape[:-1] + (LANES - nba,), w_in.dtype)
    return jnp.concatenate([w_in[..., :cfg.LXO], w_in[..., cfg.LXO + nba:], w_in[..., cfg.LXO:cfg.LXO + nba], pad], axis=-1)


def _w_in_from_wcat(cfg, wc):
    nba = 4 * cfg.H
    return jnp.concatenate([wc[..., :cfg.LXO], wc[..., cfg.BAO:cfg.BAO + nba], wc[..., cfg.LXO:cfg.BAO]], axis=-1)


def _gate_row(cfg, a):
    return jnp.zeros((1, LANES), f32).at[0, 2 * cfg.H:4 * cfg.H].set(a.reshape(-1))


def _blockdiag(cfg, w):
    w = w.reshape(2, cfg.NB // 2, 2, 64, 64)
    z = jnp.zeros_like(w[:, :, 0])
    top = jnp.concatenate([w[:, :, 0], z], axis=-1)
    bot = jnp.concatenate([z, w[:, :, 1]], axis=-1)
    return jnp.concatenate([top, bot], axis=-2).astype(bf16)


def _unblockdiag(cfg, g):
    a = g[:, :, :64, :64]
    b = g[:, :, 64:, 64:]
    return jnp.stack([a, b], axis=2).reshape(2, cfg.NB, 64, 64)


def _layer_operands(cfg, full, i):
    w = dict(
        wcat=_wcat_from_w_in(cfg, full["w_in"][i]), w_out=full["w_out"][i], ffn_wg=full["ffn_wg"][i], ffn_wu=full["ffn_wu"][i],
        ffn_wd=full["ffn_wd"][i], ple_wg=full["ple_wg"][i], ple_wp=full["ple_wp"][i],
        norm1_g=full["norm1_g"][i][None], dn_conv_w=full["dn_conv_w"][i], alog_row=_gate_row(cfg, full["dn_a_log"][i]),
        dt_row=_gate_row(cfg, full["dn_dt_bias"][i]), dn_norm_g=full["dn_norm_g"][i][None],
        lru_conv_w=full["lru_conv_w"][i], lru_conv_b=full["lru_conv_b"][i][None],
        wa=_blockdiag(cfg, full["lru_wa"][i]), wx=_blockdiag(cfg, full["lru_wx"][i]),
        lru_ba=full["lru_ba"][i], lru_bx=full["lru_bx"][i], lru_lambda=full["lru_lambda"][i],
        lru_norm_g=full["lru_norm_g"][i][None], norm2_g=full["norm2_g"][i][None], ffn_conv_w=full["ffn_conv_w"][i],
        ffn_conv_b=full["ffn_conv_b"][i][None], ple_norm_g=full["ple_norm_g"][i][None], ple_bg=full["ple_bg"][i][None],
    )
    return w


def _layer_grads_to_problem(cfg, g):
    h = cfg.H
    return dict(
        w_in=_w_in_from_wcat(cfg, g["wcat"]), w_out=g["w_out"], ffn_wg=g["ffn_wg"], ffn_wu=g["ffn_wu"], ffn_wd=g["ffn_wd"],
        ple_wg=g["ple_wg"], ple_wp=g["ple_wp"], norm1_g=g["norm1_g"][0], dn_conv_w=g["dn_conv_w"],
        dn_a_log=g["alog_row"][0, 2 * h:4 * h].reshape(2, h), dn_dt_bias=g["dt_row"][0, 2 * h:4 * h].reshape(2, h),
        dn_norm_g=g["dn_norm_g"][0], lru_conv_w=g["lru_conv_w"], lru_conv_b=g["lru_conv_b"][0],
        lru_wa=_unblockdiag(cfg, g["wa"]), lru_wx=_unblockdiag(cfg, g["wx"]), lru_ba=g["lru_ba"], lru_bx=g["lru_bx"],
        lru_lambda=g["lru_lambda"], lru_norm_g=g["lru_norm_g"][0], norm2_g=g["norm2_g"][0], ffn_conv_w=g["ffn_conv_w"],
        ffn_conv_b=g["ffn_conv_b"][0], ple_norm_g=g["ple_norm_g"][0], ple_bg=g["ple_bg"][0],
    )


def _local_step(cfg, full, x, p, target):
    r = x.reshape(cfg.T, cfg.D)
    ops, saved = [], []
    for i in range(cfg.L):
        w = _layer_operands(cfg, full, i)
        r, sv = _layer_fwd(cfg, w, r, p[i].reshape(cfg.T, cfg.PD))
        ops.append(w)
        saved.append(sv)
    dr, loss, dgf = _k_loss(cfg, r, target.reshape(cfg.T, cfg.D), full["final_g"][None])
    per_layer = [None] * cfg.L
    for i in reversed(range(cfg.L)):
        dr, g = _layer_bwd(cfg, ops[i], saved[i], dr)
        per_layer[i] = _layer_grads_to_problem(cfg, g)
    grads = {k: jnp.stack([pl_[k] for pl_ in per_layer]) for k in per_layer[0]}
    grads["final_g"] = dgf[0]
    return loss, dr, grads


def _adam_group(name, ws, gs, ms, vs, cols, tr):
    shapes = [w.shape for w in ws]
    pk = lambda arrs: _pack(arrs, cols, tr, f32)
    w2 = pk(ws)
    d, nm, nv = _adam_call(name, w2, pk(gs), pk(ms), pk(vs), min(tr, w2.shape[0]))
    return [_unpack(a.reshape(-1), shapes) for a in (d, nm, nv)]


def kernel(x, p, norm1_g, w_in, dn_conv_w, dn_a_log, dn_dt_bias, dn_norm_g, lru_conv_w, lru_conv_b, lru_wa, lru_ba, lru_wx, lru_bx, lru_lambda, lru_norm_g, w_out, norm2_g, ffn_wg, ffn_wu, ffn_conv_w, ffn_conv_b, ffn_wd, ple_norm_g, ple_wg, ple_bg, ple_wp, final_g, loss_target, m_norm1_g, m_w_in, m_dn_conv_w, m_dn_a_log, m_dn_dt_bias, m_dn_norm_g, m_lru_conv_w, m_lru_conv_b, m_lru_wa, m_lru_ba, m_lru_wx, m_lru_bx, m_lru_lambda, m_lru_norm_g, m_w_out, m_norm2_g, m_ffn_wg, m_ffn_wu, m_ffn_conv_w, m_ffn_conv_b, m_ffn_wd, m_ple_norm_g, m_ple_wg, m_ple_bg, m_ple_wp, m_final_g, v_norm1_g, v_w_in, v_dn_conv_w, v_dn_a_log, v_dn_dt_bias, v_dn_norm_g, v_lru_conv_w, v_lru_conv_b, v_lru_wa, v_lru_ba, v_lru_wx, v_lru_bx, v_lru_lambda, v_lru_norm_g, v_w_out, v_norm2_g, v_ffn_wg, v_ffn_wu, v_ffn_conv_w, v_ffn_conv_b, v_ffn_wd, v_ple_norm_g, v_ple_wg, v_ple_bg, v_ple_wp, v_final_g):
    cfg = CFG
    a = dict(locals())
    wl = {n: a[n] for n in WEIGHTS}
    ml = {n: a["m_" + n] for n in WEIGHTS}
    vl = {n: a["v_" + n] for n in WEIGHTS}
    me = 4 * lax.axis_index("x") + 2 * lax.axis_index("y") + lax.axis_index("c")

    big_shapes = [wl[n].shape for n in BIG]
    big_axes = [BIG_AXIS[n] for n in BIG]
    g8 = _all_gather("gather_weights", _pack([wl[n] for n in BIG], 1024, 16, bf16))
    full = dict(zip(BIG, _unpack8(g8.reshape(N_DEV, -1), big_shapes, big_axes)))
    ss_shapes = [wl[n].shape for n in SMALL_SHARDED]
    s8 = _all_gather("gather_small", _pack([wl[n] for n in SMALL_SHARDED], LANES, 8, f32))
    full.update(zip(SMALL_SHARDED, _unpack8(s8.reshape(N_DEV, -1), ss_shapes, [2] * len(ss_shapes))))
    full.update({n: wl[n] for n in SMALL_REPL})

    loss_part, dr, grads = _local_step(cfg, full, x, p, loss_target)
    grad_x = dr.reshape(x.shape)

    gb = jnp.concatenate([_to8(grads[n], BIG_AXIS[n]) for n in BIG], axis=1).astype(bf16)
    n_big = gb.shape[1]
    rows_big = -(-n_big // (1024 * 128)) * 128
    gb = jnp.pad(gb, ((0, 0), (0, rows_big * 1024 - n_big))).reshape(N_DEV, rows_big, 1024)
    big_sum = _sum8_call("sum_big", _all_to_all("exchange_grads", gb), 128).reshape(-1)
    gbig = dict(zip(BIG, _unpack(big_sum, big_shapes)))

    small_names = SMALL_REPL + SMALL_SHARDED
    small_shapes = [grads[n].shape for n in small_names]
    sv = _pack([grads[n] for n in small_names] + [loss_part[0, 0:1]], LANES, 8, f32)
    small_sum = _sum8_call("sum_small", _all_gather("gather_small_grads", sv), 8).reshape(-1)
    gsmall = dict(zip(small_names, _unpack(small_sum, small_shapes)))
    loss = small_sum[sum(math.prod(s) for s in small_shapes)]
    for n in SMALL_SHARDED:
        shard = wl[n].shape[2]
        gsmall[n] = lax.dynamic_slice_in_dim(gsmall[n], me * shard, shard, axis=2)
    gl = {**gbig, **gsmall}

    outs = {}
    for gname, names, cols, tr in (("adam_big", BIG, 1024, 64), ("adam_small", SMALL_REPL + SMALL_SHARDED, LANES, 64)):
        d, nm, nv = _adam_group(gname, [wl[n] for n in names], [gl[n] for n in names], [ml[n] for n in names],
                                [vl[n] for n in names], cols, tr)
        for j, n in enumerate(names):
            outs[n] = (d[j], nm[j], nv[j])
    return (loss, grad_x, *[gl[n] for n in WEIGHTS], *[outs[n][0] for n in WEIGHTS], *[outs[n][1] for n in WEIGHTS],
            *[outs[n][2] for n in WEIGHTS])
```

```python
import functools
import math

import jax
import jax.numpy as jnp
from jax import lax
from jax.experimental import pallas as pl
from jax.experimental.pallas import tpu as pltpu

f32 = jnp.float32
bf16 = jnp.bfloat16
MESH = pl.DeviceIdType.MESH

N_DEV = 8
LANES = 128
EPS = 1e-6
LRU_C = 8.0
ADAM_LR, ADAM_B1, ADAM_B2, ADAM_EPS, ADAM_WD, ADAM_STEP = 0.001, 0.9, 0.999, 1e-08, 0.01, 10
VMEM_LIMIT = 56 * 1024 * 1024


class Cfg:
    def __init__(self, d_model=1024, bl=4, seq=2048, depth=4, heads=4, lru_width=512, d_ff=2816, ple=256,
                 tm=512, tm_ffn=256, ff_tile=256):
        self.D, self.BL, self.S, self.L, self.H = d_model, bl, seq, depth, heads
        self.DH = 128
        self.DN = heads * self.DH
        self.LW = lru_width
        self.NB = lru_width // 64
        self.FF, self.PD = d_ff, ple
        self.C = 64
        self.NC = seq // self.C
        self.T = bl * seq
        self.TM = min(tm, self.T)
        self.TMF = min(tm_ffn, self.T)
        self.FT = ff_tile
        self.ZO = 3 * self.DN
        self.LXO = 4 * self.DN
        self.LGO = self.LXO + self.LW
        self.BAO = self.LGO + self.LW
        self.PC = self.BAO + LANES
        self.IN_COLS = 4 * self.DN + 4 * heads + 2 * self.LW


CFG = Cfg()


def _mm(a, b):
    return jnp.dot(a.astype(bf16), b.astype(bf16), preferred_element_type=f32)


def _mm_nt(a, b):
    return lax.dot_general(a.astype(bf16), b.astype(bf16), (((1,), (1,)), ((), ())), preferred_element_type=f32)


def _mm_tn(a, b):
    return lax.dot_general(a.astype(bf16), b.astype(bf16), (((0,), (0,)), ((), ())), preferred_element_type=f32)


def _split2(a):
    hi = a.astype(bf16)
    return hi, (a - hi.astype(f32)).astype(bf16)


def _hmm3(a, b, dims=(((1,), (0,)), ((), ()))):
    ah, al = _split2(a)
    bh, bl = _split2(b)
    dot = functools.partial(lax.dot_general, dimension_numbers=dims, preferred_element_type=f32)
    return dot(ah, bh) + dot(ah, bl) + dot(al, bh)


def _hmm3_tn(a, b):
    return _hmm3(a, b, (((0,), (0,)), ((), ())))


def _cum_mm(mask, x, dims=(((1,), (0,)), ((), ()))):
    m = mask.astype(bf16)
    x1 = x.astype(bf16)
    r = x - x1.astype(f32)
    x2 = r.astype(bf16)
    x3 = (r - x2.astype(f32)).astype(bf16)
    dot = functools.partial(lax.dot_general, dimension_numbers=dims, preferred_element_type=f32)
    return dot(m, x1) + dot(m, x2) + dot(m, x3)


def _rms_fwd(x, g):
    inv = lax.rsqrt(jnp.mean(x * x, axis=-1, keepdims=True) + EPS)
    xh = x * inv
    return xh * g, xh, inv


def _rms_bwd(dy, xh, inv, g):
    dxh = dy * g
    dx = inv * (dxh - xh * jnp.mean(dxh * xh, axis=-1, keepdims=True))
    dg = jnp.sum(dy * xh, axis=0, keepdims=True)
    return dx, dg


def _sigmoid(x):
    return 1.0 / (1.0 + jnp.exp(-x))


def _softplus(x):
    return jnp.maximum(x, 0.0) + jnp.log(1.0 + jnp.exp(-jnp.abs(x)))


def _silu(x):
    s = _sigmoid(x)
    return x * s, s * (1.0 + x * (1.0 - s))


_GC = math.sqrt(2.0 / math.pi)


def _gelu(x):
    t = jnp.tanh(_GC * (x + 0.044715 * x * x * x))
    y = 0.5 * x * (1.0 + t)
    dy = 0.5 * (1.0 + t) + 0.5 * x * (1.0 - t * t) * _GC * (1.0 + 3.0 * 0.044715 * x * x)
    return y, dy


def _nexpm1(x):
    ser = -x * (1.0 + x * 0.5 * (1.0 + x * (1.0 / 3.0) * (1.0 + x * 0.25 * (1.0 + x * 0.2))))
    return jnp.where(x > -0.1, ser, 1.0 - jnp.exp(x))


def _shift(x, s, fill=0.0):
    if s == 0:
        return x
    n = x.shape[0]
    t = lax.broadcasted_iota(jnp.int32, x.shape, 0)
    r = pltpu.roll(x, (-s) % n, 0)
    return jnp.where((t + s >= 0) & (t + s < n), r, fill)


def _conv_fwd(x, w_ref, left):
    k = w_ref.shape[0]
    out = _shift(x, -left) * w_ref[0:1, :]
    for j in range(1, k):
        out = out + _shift(x, j - left) * w_ref[j:j + 1, :]
    return out


def _conv_bwd(dout, x, w_ref, left):
    k = w_ref.shape[0]
    dx = None
    dws = []
    for j in range(k):
        term = _shift(dout, -(j - left)) * w_ref[j:j + 1, :]
        dx = term if dx is None else dx + term
        dws.append(jnp.sum(dout * _shift(x, j - left), axis=0, keepdims=True))
    return dx, dws


def _lin_scan(a, b, rev):
    n = a.shape[0]
    d = 1
    while d < n:
        s = d if rev else -d
        b = a * _shift(b, s, 0.0) + b
        a = a * _shift(a, s, 1.0)
        d *= 2
    return b


def _tri_masks(c, rev):
    i = lax.broadcasted_iota(jnp.int32, (c, c), 0)
    j = lax.broadcasted_iota(jnp.int32, (c, c), 1)
    incl = (i <= j) if rev else (i >= j)
    strict = (i < j) if rev else (i > j)
    incl_t = (i >= j) if rev else (i <= j)
    return incl, strict, incl_t


def _map(f, *lists):
    return [f(*a) for a in zip(*lists)]


def _tri_inv(mats):
    c = mats[0].shape[0]
    i = lax.broadcasted_iota(jnp.int32, (c, c), 0)
    j = lax.broadcasted_iota(jnp.int32, (c, c), 1)
    eye = jnp.where(i == j, 1.0, 0.0)
    t = [eye - a for a in mats]
    pw = _map(_hmm3, mats, mats)
    for it in range(5):
        t = _map(lambda ti, ui: ti + ui, t, _map(_hmm3, t, pw))
        if it < 4:
            pw = _map(_hmm3, pw, pw)
    return t


def _gdn_decay(gs, revs):
    c = gs[0].shape[0]
    masks = [_tri_masks(c, r) for r in revs]
    gb = [jnp.broadcast_to(g, (c, c)) for g in gs]
    mcol = _map(lambda m, x: _cum_mm(m[0], x), masks, gb)
    mrow = _map(lambda m, x: jnp.sum(jnp.where(m[2], x, 0.0), axis=0, keepdims=True), masks, gb)
    dec = _map(lambda m, a, b: jnp.exp(jnp.where(m[0], a - b, -1e30)), masks, mcol, mrow)
    return [m[:, 0:1] for m in mcol], [jnp.sum(g, axis=0, keepdims=True) for g in gs], dec


def _gdn_prep(qs, k, v, g, beta, kk, qk, revs):
    c = k[0].shape[0]
    masks = [_tri_masks(c, r) for r in revs]
    gcum, glast, dec = _gdn_decay(g, revs)
    e = [jnp.exp(x) for x in gcum]
    tm = _tri_inv(_map(lambda m, b, x, d: jnp.where(m[1], b * x * d, 0.0), masks, beta, kk, dec))
    u = _map(lambda t, vi, b: _hmm3(t, vi * b), tm, v, beta)
    w = _map(lambda t, ki, b, ei: _hmm3(t, ki * b * ei), tm, k, beta, e)
    p = _map(lambda m, x, d: jnp.where(m[0], x * d, 0.0), masks, qk, dec)
    return dict(dec=dec, cd=[jnp.exp(x) for x in glast], tm=tm, u=u, w=w, p=p, qd=_map(lambda a, b: a * b, qs, e),
                kd=_map(lambda ki, gl, gc: ki * jnp.exp(gl - gc), k, glast, gcum))


def _lane_pick(x, lane):
    l = lax.broadcasted_iota(jnp.int32, x.shape, 1)
    return jnp.sum(jnp.where(l == lane, x, 0.0), axis=1, keepdims=True)


def _params(sem, vmem=VMEM_LIMIT):
    return pltpu.CompilerParams(dimension_semantics=sem, vmem_limit_bytes=vmem)


def _row_call(name, body, t, tm, row_ins, full_ins, row_outs, acc_outs):
    in_specs = [pl.BlockSpec((tm, w), functools.partial(lambda i, c: (i, c), c=c)) for (_, w, c) in row_ins]
    for a in full_ins:
        in_specs.append(pl.BlockSpec(a.shape, functools.partial(lambda i, n: (0,) * n, n=a.ndim)))
    out_specs = [pl.BlockSpec((tm, w), lambda i: (i, 0)) for (w, _) in row_outs]
    out_shape = [jax.ShapeDtypeStruct((t, w), dt) for (w, dt) in row_outs]
    for shp, dt in acc_outs:
        out_specs.append(pl.BlockSpec(shp, functools.partial(lambda i, n: (0,) * n, n=len(shp))))
        out_shape.append(jax.ShapeDtypeStruct(shp, dt))
    return pl.pallas_call(
        body, name=name, grid=(t // tm,), in_specs=in_specs, out_specs=out_specs, out_shape=out_shape,
        compiler_params=_params(("arbitrary",)),
    )(*[a for (a, _, _) in row_ins], *full_ins)


def _k_in(cfg, r, g1, wcat):
    def body(r_ref, g_ref, w_ref, h_ref, proj_ref):
        y, _, _ = _rms_fwd(r_ref[...], g_ref[...])
        hb = y.astype(bf16)
        h_ref[...] = hb
        proj_ref[...] = _mm_nt(hb, w_ref[...])

    return _row_call("in_proj", body, cfg.T, cfg.TM, [(r, cfg.D, 0)], [g1, wcat],
                     [(cfg.D, bf16), (cfg.PC, f32)], [])


def _k_prep(cfg, proj, conv_w):
    dn, s = cfg.DN, cfg.S

    def body(x_ref, w_ref, o_ref):
        sec = pl.program_id(1)
        c = _conv_fwd(x_ref[...], w_ref, 2)
        y, _ = _silu(c)
        for h in range(cfg.H):
            yh = y[:, h * 128:(h + 1) * 128]
            nh = yh * lax.rsqrt(jnp.sum(yh * yh, axis=1, keepdims=True) + EPS)
            o_ref[:, h * 128:(h + 1) * 128] = jnp.where(sec < 2, nh, yh)

    return pl.pallas_call(
        body, name="dn_prep", grid=(cfg.BL, 3),
        in_specs=[pl.BlockSpec((s, dn), lambda b, j: (b, j)), pl.BlockSpec((4, dn), lambda b, j: (0, j))],
        out_specs=pl.BlockSpec((s, dn), lambda b, j: (b, j)),
        out_shape=jax.ShapeDtypeStruct((cfg.T, 3 * dn), f32),
        compiler_params=_params(("arbitrary", "arbitrary")),
    )(proj, conv_w)


def _gate_cols(ba, alog_row, dt_row, lane_b, lane_a):
    beta = _sigmoid(_lane_pick(ba, lane_b))
    alpha = _lane_pick(ba, lane_a)
    aexp = jnp.exp(_lane_pick(alog_row, lane_a))
    dtb = _lane_pick(dt_row, lane_a)
    xa = alpha + dtb
    g = -aexp * _softplus(xa)
    return beta, g, aexp, xa


def _gdn_specs(cfg):
    s, c, nc = cfg.S, cfg.C, cfg.NC
    tok2 = pl.BlockSpec((2, s, 128), lambda b, h: (0, b, h))
    mat2 = pl.BlockSpec((2, 1, nc, c, c), lambda b, h: (0, b, h, 0, 0))
    cd2 = pl.BlockSpec((2, 1, nc, 8, LANES), lambda b, h: (0, b, h, 0, 0))
    shapes = dict(
        tok32=jax.ShapeDtypeStruct((2, cfg.T, cfg.DN), f32), tok16=jax.ShapeDtypeStruct((2, cfg.T, cfg.DN), bf16),
        mat32=jax.ShapeDtypeStruct((2, cfg.BL, cfg.H * nc, c, c), f32), mat16=jax.ShapeDtypeStruct((2, cfg.BL, cfg.H * nc, c, c), bf16),
        cd=jax.ShapeDtypeStruct((2, cfg.BL, cfg.H * nc, 8, LANES), f32))
    return tok2, mat2, cd2, shapes


def _k_gdn_fwd(cfg, qkv, proj, alog_row, dt_row):
    s, c, nc, hh = cfg.S, cfg.C, cfg.NC, cfg.H
    ba_blk = cfg.BAO // LANES
    per = 2 if nc % 2 == 0 else 1

    def body(q_ref, k_ref, v_ref, ba_ref, al_ref, dt_ref, o_ref, u_o, w_o, qd_o, kd_o, p_o, t_o, dec_o, cd_o):
        h = pl.program_id(1)

        def prep(i, carry):
            chains, qs, k, v, kk, qk, g, beta = [], [], [], [], [], [], [], []
            for j in range(per):
                n = i * per + j
                rows = pl.ds(pl.multiple_of(n * c, c), c)
                kj, vj = k_ref[rows, :], v_ref[rows, :]
                qj = q_ref[rows, :] * (128 ** -0.5)
                kkj, qkj = _mm_nt(kj, kj), _mm_nt(qj, kj)
                ba = ba_ref[rows, :]
                for d in range(2):
                    bd, gd, _, _ = _gate_cols(ba, al_ref[...], dt_ref[...], d * hh + h, 2 * hh + d * hh + h)
                    chains.append((d, n, rows))
                    for lst, val in ((qs, qj), (k, kj), (v, vj), (kk, kkj), (qk, qkj), (g, gd), (beta, bd)):
                        lst.append(val)
            z = _gdn_prep(qs, k, v, g, beta, kk, qk, [d == 1 for d, _, _ in chains])
            for x, (d, n, rows) in enumerate(chains):
                u_o[d, rows, :] = z["u"][x]
                w_o[d, rows, :] = z["w"][x].astype(bf16)
                qd_o[d, rows, :] = z["qd"][x].astype(bf16)
                kd_o[d, rows, :] = z["kd"][x].astype(bf16)
                p_o[d, 0, n] = z["p"][x].astype(bf16)
                t_o[d, 0, n] = z["tm"][x]
                dec_o[d, 0, n] = z["dec"][x]
                cd_o[d, 0, n] = jnp.broadcast_to(z["cd"][x], (8, LANES))
            return carry

        lax.fori_loop(0, nc // per, prep, 0)
        o_ref[...] = jnp.zeros_like(o_ref)

        def step(i, sts):
            ns = [i, nc - 1 - i]
            rows = [pl.ds(pl.multiple_of(n * c, c), c) for n in ns]
            vn = [u_o[d, rows[d], :] - _mm(w_o[d, rows[d], :], sts[d]) for d in range(2)]
            o = [_mm(qd_o[d, rows[d], :], sts[d]) + _mm(p_o[d, 0, ns[d]], vn[d]) for d in range(2)]
            new = [sts[d] * cd_o[d, 0, ns[d]][0:1, :] + _mm_tn(kd_o[d, rows[d], :], vn[d]) for d in range(2)]
            for d in range(2):
                o_ref[rows[d], :] = o_ref[rows[d], :] + o[d]
            return tuple(new)

        z0 = jnp.zeros((128, 128), f32)
        lax.fori_loop(0, nc, step, (z0, z0))

    blk = lambda off: pl.BlockSpec((s, 128), functools.partial(lambda b, h, off: (b, off + h), off=off))
    row = pl.BlockSpec((1, LANES), lambda b, h: (0, 0))
    tok2, mat2, cd2, shp = _gdn_specs(cfg)
    return pl.pallas_call(
        body, name="gdn_fwd", grid=(cfg.BL, hh),
        in_specs=[blk(0), blk(hh), blk(2 * hh), pl.BlockSpec((s, LANES), lambda b, h: (b, ba_blk)), row, row],
        out_specs=[pl.BlockSpec((s, 128), lambda b, h: (b, h)), tok2, tok2, tok2, tok2, mat2, mat2, mat2, cd2],
        out_shape=[jax.ShapeDtypeStruct((cfg.T, cfg.DN), f32), shp["tok32"], shp["tok16"], shp["tok16"], shp["tok16"],
                   shp["mat16"], shp["mat32"], shp["mat32"], shp["cd"]],
        compiler_params=_params(("arbitrary", "arbitrary")),
    )(qkv, qkv, qkv, proj, alog_row, dt_row)


def _lru_dir_fwd(xc, wa, wx, ba, bx, lam, rev):
    ra = _sigmoid(_mm(xc, wa) + ba)
    ig = _sigmoid(_mm(xc, wx) + bx)
    sp = _softplus(-lam)
    la = -LRU_C * ra * sp
    a = jnp.exp(la)
    m = jnp.sqrt(_nexpm1(2.0 * la))
    gx = ig * xc
    h = _lin_scan(a, m * gx, rev)
    return dict(ra=ra, ig=ig, sp=sp, a=a, m=m, gx=gx, h=h)


def _lru_specs(cfg, outer_b):
    s = cfg.S
    lx_blk = cfg.LXO // LANES
    if outer_b:
        ix = lambda f: (lambda b, ct: f(b, ct))
    else:
        ix = lambda f: (lambda ct, b: f(b, ct))
    return dict(
        lx=pl.BlockSpec((s, LANES), ix(lambda b, ct: (b, lx_blk + ct))),
        tok=pl.BlockSpec((s, LANES), ix(lambda b, ct: (b, ct))),
        cw=pl.BlockSpec((4, LANES), ix(lambda b, ct: (0, ct))),
        row=pl.BlockSpec((1, LANES), ix(lambda b, ct: (0, ct))),
        w=pl.BlockSpec((2, 1, LANES, LANES), ix(lambda b, ct: (0, ct, 0, 0))),
        two=pl.BlockSpec((2, LANES), ix(lambda b, ct: (0, ct))),
    )


def _k_lru_fwd(cfg, proj, conv_w, conv_b, wa, wx, ba, bx, lam):
    def body(lx_ref, cw_ref, cb_ref, wa_ref, wx_ref, ba_ref, bx_ref, lam_ref, o_ref):
        xc = _conv_fwd(lx_ref[...], cw_ref, 2) + cb_ref[...]
        acc = None
        for d in range(2):
            z = _lru_dir_fwd(xc, wa_ref[d, 0], wx_ref[d, 0], ba_ref[d:d + 1, :], bx_ref[d:d + 1, :], lam_ref[d:d + 1, :], d == 1)
            acc = z["h"] if acc is None else acc + z["h"]
        o_ref[...] = acc

    sp = _lru_specs(cfg, True)
    return pl.pallas_call(
        body, name="lru_fwd", grid=(cfg.BL, cfg.LW // LANES),
        in_specs=[sp["lx"], sp["cw"], sp["row"], sp["w"], sp["w"], sp["two"], sp["two"], sp["two"]],
        out_specs=sp["tok"], out_shape=jax.ShapeDtypeStruct((cfg.T, cfg.LW), f32),
        compiler_params=_params(("arbitrary", "arbitrary")),
    )(proj, conv_w, conv_b, wa, wx, ba, bx, lam)


def _mix_parts(cfg, o, z, lg, hs, dng, lrg):
    heads = []
    for h in range(cfg.H):
        sl = slice(h * 128, (h + 1) * 128)
        y, xh, inv = _rms_fwd(o[:, sl], dng)
        sz, dsz = _silu(z[:, sl])
        heads.append((y, xh, inv, sz, dsz))
    gl, dgl = _gelu(lg)
    y2, xh2, inv2 = _rms_fwd(gl * hs, lrg)
    return heads, (y2, xh2, inv2, gl, dgl)


def _k_mix(cfg, o, proj, hs, r, dng, lrg, wout):
    dn = cfg.DN

    def body(o_ref, z_ref, lg_ref, hs_ref, r_ref, dng_ref, lrg_ref, w_ref, mix_ref, out_ref):
        heads, lru = _mix_parts(cfg, o_ref[...], z_ref[...], lg_ref[...], hs_ref[...], dng_ref[...], lrg_ref[...])
        for h, (y, _, _, sz, _) in enumerate(heads):
            mix_ref[:, h * 128:(h + 1) * 128] = (y * sz).astype(bf16)
        mix_ref[:, dn:] = lru[0].astype(bf16)
        out_ref[...] = r_ref[...] + jnp.dot(mix_ref[...], w_ref[...], preferred_element_type=f32)

    return _row_call("mix_out", body, cfg.T, cfg.TM,
                     [(o, dn, 0), (proj, dn, cfg.ZO // dn), (proj, cfg.LW, cfg.LGO // cfg.LW), (hs, cfg.LW, 0), (r, cfg.D, 0)],
                     [dng, lrg, wout], [(cfg.D, bf16), (cfg.D, f32)], [])


def _k_ffn_a(cfg, r, g2, wg, wu):
    def body(r_ref, g_ref, wg_ref, wu_ref, h_ref, gp_ref, up_ref):
        y, _, _ = _rms_fwd(r_ref[...], g_ref[...])
        hb = y.astype(bf16)
        h_ref[...] = hb
        gp_ref[...] = _mm_nt(hb, wg_ref[...])
        up_ref[...] = _mm_nt(hb, wu_ref[...])

    return _row_call("ffn_in", body, cfg.T, cfg.TMF, [(r, cfg.D, 0)], [g2, wg, wu],
                     [(cfg.D, bf16), (cfg.FF, f32), (cfg.FF, f32)], [])


def _k_ffn_b(cfg, gp, up, conv_w, conv_b):
    s, ft = cfg.S, cfg.FT

    def body(gp_ref, up_ref, w_ref, b_ref, o_ref):
        gate = _conv_fwd(gp_ref[...], w_ref, 1) + b_ref[...]
        gl, _ = _gelu(gate)
        o_ref[...] = (gl * up_ref[...]).astype(bf16)

    tok = pl.BlockSpec((s, ft), lambda b, j: (b, j))
    return pl.pallas_call(
        body, name="ffn_act", grid=(cfg.BL, cfg.FF // ft),
        in_specs=[tok, tok, pl.BlockSpec((3, ft), lambda b, j: (0, j)), pl.BlockSpec((1, ft), lambda b, j: (0, j))],
        out_specs=tok, out_shape=jax.ShapeDtypeStruct((cfg.T, cfg.FF), bf16),
        compiler_params=_params(("arbitrary", "arbitrary")),
    )(gp, up, conv_w, conv_b)


def _k_ffn_c(cfg, act, r, wd):
    def body(a_ref, r_ref, w_ref, o_ref):
        o_ref[...] = r_ref[...] + jnp.dot(a_ref[...], w_ref[...], preferred_element_type=f32)

    return _row_call("ffn_out", body, cfg.T, cfg.TM, [(act, cfg.FF, 0), (r, cfg.D, 0)], [wd], [(cfg.D, f32)], [])[0]


def _k_ple(cfg, r, p, gp, wpg, bg, wpp):
    def body(r_ref, p_ref, g_ref, wg_ref, bg_ref, wp_ref, pn_ref, o_ref):
        x = r_ref[...]
        y, _, _ = _rms_fwd(x, g_ref[...])
        pn = y.astype(bf16)
        pn_ref[...] = pn
        pg = _sigmoid(jnp.dot(pn, wg_ref[...], preferred_element_type=f32) + bg_ref[...])
        o_ref[...] = x + pg * _mm_nt(p_ref[...], wp_ref[...])

    return _row_call("ple", body, cfg.T, cfg.TM, [(r, cfg.D, 0), (p, cfg.PD, 0)], [gp, wpg, bg, wpp],
                     [(cfg.D, bf16), (cfg.D, f32)], [])


def _k_loss(cfg, r, tgt, gf):
    d = cfg.D

    def body(r_ref, t_ref, g_ref, dr_ref, loss_ref, dg_ref):
        @pl.when(pl.program_id(0) == 0)
        def _():
            loss_ref[...] = jnp.zeros_like(loss_ref)
            dg_ref[...] = jnp.zeros_like(dg_ref)

        g = g_ref[...]
        y, xh, inv = _rms_fwd(r_ref[...], g)
        err = y - t_ref[...]
        loss_ref[...] = loss_ref[...] + (0.5 / d) * jnp.sum(err * err)
        dx, dg = _rms_bwd(err * (1.0 / d), xh, inv, g)
        dr_ref[...] = dx
        dg_ref[...] = dg_ref[...] + dg

    return _row_call("loss_head", body, cfg.T, cfg.TM, [(r, d, 0), (tgt, d, 0)], [gf], [(d, f32)],
                     [((1, LANES), f32), ((1, d), f32)])


def _zero_at_first(cond, *refs):
    @pl.when(cond)
    def _():
        for r in refs:
            r[...] = jnp.zeros_like(r)


def _b_ple(cfg, dr3, r2, p, gp, wpg, bg, wpp):
    d = cfg.D

    def body(dr_ref, r_ref, p_ref, g_ref, wg_ref, bg_ref, wp_ref, dr2_ref, dlog_ref, dpp_ref, dgp_ref, dbg_ref):
        _zero_at_first(pl.program_id(0) == 0, dgp_ref, dbg_ref)
        g = g_ref[...]
        dr = dr_ref[...]
        y, xh, inv = _rms_fwd(r_ref[...], g)
        pg = _sigmoid(jnp.dot(y.astype(bf16), wg_ref[...], preferred_element_type=f32) + bg_ref[...])
        pp = _mm_nt(p_ref[...], wp_ref[...])
        dpp_ref[...] = (dr * pg).astype(bf16)
        dlog = dr * pp * pg * (1.0 - pg)
        dlog_ref[...] = dlog.astype(bf16)
        dbg_ref[...] = dbg_ref[...] + jnp.sum(dlog, axis=0, keepdims=True)
        dx, dg = _rms_bwd(_mm_nt(dlog, wg_ref[...]), xh, inv, g)
        dgp_ref[...] = dgp_ref[...] + dg
        dr2_ref[...] = dr + dx

    return _row_call("ple_bwd", body, cfg.T, cfg.TM, [(dr3, d, 0), (r2, d, 0), (p, cfg.PD, 0)], [gp, wpg, bg, wpp],
                     [(d, f32), (d, bf16), (d, bf16)], [((1, d), f32), ((1, d), f32)])


def _b_ffn_bc(cfg, dr2, wd, gp, up, conv_w, conv_b):
    s, ft, d = cfg.S, cfg.FT, cfg.D

    def body(dr_ref, wd_ref, gp_ref, up_ref, w_ref, b_ref, dgp_ref, dup_ref, dwd_ref, dcw_ref, dcb_ref):
        _zero_at_first(pl.program_id(1) == 0, dwd_ref, dcw_ref, dcb_ref)
        drb = dr_ref[...].astype(bf16)
        dact = _mm_nt(drb, wd_ref[...])
        gpre = gp_ref[...]
        up = up_ref[...]
        gl, dgl = _gelu(_conv_fwd(gpre, w_ref, 1) + b_ref[...])
        dup_ref[...] = (dact * gl).astype(bf16)
        dgate = dact * up * dgl
        dcb_ref[...] = dcb_ref[...] + jnp.sum(dgate, axis=0, keepdims=True)
        dx, dws = _conv_bwd(dgate, gpre, w_ref, 1)
        for j, dw in enumerate(dws):
            dcw_ref[j:j + 1, :] = dcw_ref[j:j + 1, :] + dw
        dgp_ref[...] = dx.astype(bf16)
        dwd_ref[...] = dwd_ref[...] + _mm_tn((gl * up).astype(bf16), drb)

    tok = pl.BlockSpec((s, ft), lambda j, b: (b, j))
    return pl.pallas_call(
        body, name="ffn_act_bwd", grid=(cfg.FF // ft, cfg.BL),
        in_specs=[pl.BlockSpec((s, d), lambda j, b: (b, 0)), pl.BlockSpec((ft, d), lambda j, b: (j, 0)), tok, tok,
                  pl.BlockSpec((3, ft), lambda j, b: (0, j)), pl.BlockSpec((1, ft), lambda j, b: (0, j))],
        out_specs=[tok, tok, pl.BlockSpec((ft, d), lambda j, b: (j, 0)), pl.BlockSpec((3, ft), lambda j, b: (0, j)),
                   pl.BlockSpec((1, ft), lambda j, b: (0, j))],
        out_shape=[jax.ShapeDtypeStruct((cfg.T, cfg.FF), bf16), jax.ShapeDtypeStruct((cfg.T, cfg.FF), bf16),
                   jax.ShapeDtypeStruct((cfg.FF, d), f32), jax.ShapeDtypeStruct((3, cfg.FF), f32),
                   jax.ShapeDtypeStruct((1, cfg.FF), f32)],
        compiler_params=_params(("arbitrary", "arbitrary")),
    )(dr2, wd, gp, up, conv_w, conv_b)


def _b_ffn_a(cfg, dgp, dup, dr2, r_mid, g2, wg, wu):
    d = cfg.D

    def body(dgp_ref, dup_ref, dr_ref, r_ref, g_ref, wg_ref, wu_ref, o_ref, dg_ref):
        _zero_at_first(pl.program_id(0) == 0, dg_ref)
        g = g_ref[...]
        _, xh, inv = _rms_fwd(r_ref[...], g)
        dh = _mm(dgp_ref[...], wg_ref[...]) + _mm(dup_ref[...], wu_ref[...])
        dx, dg = _rms_bwd(dh, xh, inv, g)
        dg_ref[...] = dg_ref[...] + dg
        o_ref[...] = dr_ref[...] + dx

    return _row_call("ffn_in_bwd", body, cfg.T, cfg.TMF, [(dgp, cfg.FF, 0), (dup, cfg.FF, 0), (dr2, d, 0), (r_mid, d, 0)],
                     [g2, wg, wu], [(d, f32)], [((1, d), f32)])


def _mm_tn_call(cfg, name, x, dy, tn):
    t, k = x.shape
    n = dy.shape[1]
    tm = cfg.TM

    def body(x_ref, dy_ref, o_ref):
        _zero_at_first(pl.program_id(1) == 0, o_ref)
        o_ref[...] = o_ref[...] + _mm_tn(x_ref[...], dy_ref[...])

    return pl.pallas_call(
        body, name=name, grid=(n // tn, t // tm),
        in_specs=[pl.BlockSpec((tm, k), lambda j, i: (i, 0)), pl.BlockSpec((tm, tn), lambda j, i: (i, j))],
        out_specs=pl.BlockSpec((k, tn), lambda j, i: (0, j)),
        out_shape=jax.ShapeDtypeStruct((k, n), f32),
        compiler_params=_params(("arbitrary", "arbitrary")),
    )(x, dy)


def _b_mix(cfg, dr, o, proj, hs, dng, lrg, wout):
    dn, lw, d = cfg.DN, cfg.LW, cfg.D

    def body(dr_ref, o_ref, z_ref, lg_ref, hs_ref, dng_ref, lrg_ref, w_ref, do_ref, dz_ref, dlg_ref, dhs_ref, ddn_ref, dlr_ref):
        _zero_at_first(pl.program_id(0) == 0, ddn_ref, dlr_ref)
        dng, lrg = dng_ref[...], lrg_ref[...]
        hs = hs_ref[...]
        o, z = o_ref[...], z_ref[...]
        heads, lru = _mix_parts(cfg, o, z, lg_ref[...], hs, dng, lrg)
        drb = dr_ref[...].astype(bf16)
        dmix_dn = _mm_nt(drb, w_ref[0:dn, :])
        dmix_lr = _mm_nt(drb, w_ref[dn:, :])
        dgn = jnp.zeros_like(dng)
        for h, (y, xh, inv, sz, dsz) in enumerate(heads):
            sl = slice(h * 128, (h + 1) * 128)
            dm = dmix_dn[:, sl]
            dz_ref[:, sl] = dm * y * dsz
            dx, dg = _rms_bwd(dm * sz, xh, inv, dng)
            do_ref[:, sl] = dx
            dgn = dgn + dg
        ddn_ref[...] = ddn_ref[...] + dgn
        _, xh2, inv2, gl, dgl = lru
        dx2, dg2 = _rms_bwd(dmix_lr, xh2, inv2, lrg)
        dlr_ref[...] = dlr_ref[...] + dg2
        dlg_ref[...] = dx2 * hs * dgl
        dhs_ref[...] = dx2 * gl

    return _row_call("mix_bwd", body, cfg.T, cfg.TM,
                     [(dr, d, 0), (o, dn, 0), (proj, dn, cfg.ZO // dn), (proj, lw, cfg.LGO // lw), (hs, lw, 0)],
                     [dng, lrg, wout], [(dn, f32), (dn, f32), (lw, f32), (lw, f32)], [((1, 128), f32), ((1, lw), f32)])


def _b_lru(cfg, proj, dhs, conv_w, conv_b, wa, wx, ba, bx, lam):
    def body(lx_ref, dh_ref, cw_ref, cb_ref, wa_ref, wx_ref, ba_ref, bx_ref, lam_ref,
             dlx_ref, dcw_ref, dcb_ref, dwa_ref, dwx_ref, dba_ref, dbx_ref, dlam_ref):
        _zero_at_first(pl.program_id(1) == 0, dcw_ref, dcb_ref, dwa_ref, dwx_ref, dba_ref, dbx_ref, dlam_ref)
        lx = lx_ref[...]
        dh = dh_ref[...]
        xc = _conv_fwd(lx, cw_ref, 2) + cb_ref[...]
        dxc = jnp.zeros_like(xc)
        for d in range(2):
            rev = d == 1
            lam = lam_ref[d:d + 1, :]
            z = _lru_dir_fwd(xc, wa_ref[d, 0], wx_ref[d, 0], ba_ref[d:d + 1, :], bx_ref[d:d + 1, :], lam, rev)
            a, m, ra, ig, sp = z["a"], z["m"], z["ra"], z["ig"], z["sp"]
            a_next = _shift(a, -1 if rev else 1, 0.0)
            lmb = _lin_scan(a_next, dh, not rev)
            h_prev = _shift(z["h"], 1 if rev else -1, 0.0)
            da = lmb * h_prev
            dm = lmb * z["gx"]
            dgx = lmb * m
            dla = da * a - dm * (a * a) / jnp.maximum(m, 1e-30)
            dra = dla * (-LRU_C) * sp
            dsp = jnp.sum(dla * (-LRU_C) * ra, axis=0, keepdims=True)
            dlam_ref[d:d + 1, :] = dlam_ref[d:d + 1, :] - dsp * _sigmoid(-lam)
            dpa = dra * ra * (1.0 - ra)
            dpx = dgx * xc * ig * (1.0 - ig)
            dba_ref[d:d + 1, :] = dba_ref[d:d + 1, :] + jnp.sum(dpa, axis=0, keepdims=True)
            dbx_ref[d:d + 1, :] = dbx_ref[d:d + 1, :] + jnp.sum(dpx, axis=0, keepdims=True)
            dwa_ref[d, 0] = dwa_ref[d, 0] + _mm_tn(xc, dpa)
            dwx_ref[d, 0] = dwx_ref[d, 0] + _mm_tn(xc, dpx)
            dxc = dxc + dgx * ig + _mm_nt(dpa, wa_ref[d, 0]) + _mm_nt(dpx, wx_ref[d, 0])
        dcb_ref[...] = dcb_ref[...] + jnp.sum(dxc, axis=0, keepdims=True)
        dx, dws = _conv_bwd(dxc, lx, cw_ref, 2)
        for j, dw in enumerate(dws):
            dcw_ref[j:j + 1, :] = dcw_ref[j:j + 1, :] + dw
        dlx_ref[...] = dx

    sp = _lru_specs(cfg, False)
    nct = cfg.LW // LANES
    return pl.pallas_call(
        body, name="lru_bwd", grid=(nct, cfg.BL),
        in_specs=[sp["lx"], sp["tok"], sp["cw"], sp["row"], sp["w"], sp["w"], sp["two"], sp["two"], sp["two"]],
        out_specs=[sp["tok"], sp["cw"], sp["row"], sp["w"], sp["w"], sp["two"], sp["two"], sp["two"]],
        out_shape=[jax.ShapeDtypeStruct((cfg.T, cfg.LW), f32), jax.ShapeDtypeStruct((4, cfg.LW), f32),
                   jax.ShapeDtypeStruct((1, cfg.LW), f32), jax.ShapeDtypeStruct((2, nct, LANES, LANES), f32),
                   jax.ShapeDtypeStruct((2, nct, LANES, LANES), f32), jax.ShapeDtypeStruct((2, cfg.LW), f32),
                   jax.ShapeDtypeStruct((2, cfg.LW), f32), jax.ShapeDtypeStruct((2, cfg.LW), f32)],
        compiler_params=_params(("arbitrary", "arbitrary")),
    )(proj, dhs, conv_w, conv_b, wa, wx, ba, bx, lam)


def _b_gdn(cfg, qkv, proj, do, alog_row, dt_row, saved):
    s, c, nc, hh = cfg.S, cfg.C, cfg.NC, cfg.H
    ba_blk = cfg.BAO // LANES
    scale = 128 ** -0.5

    def body(q_ref, k_ref, v_ref, ba_ref, do_ref, al_ref, dt_ref, u_i, w_i, qd_i, kd_i, p_i, t_i, dec_i, cd_i,
             dqkv_ref, dba_ref, dal_ref, ddt_ref, vn_s, st_s):
        h = pl.program_id(1)
        _zero_at_first((pl.program_id(0) == 0) & (h == 0), dal_ref, ddt_ref)
        _zero_at_first(h == 0, dba_ref)
        dqkv_ref[...] = jnp.zeros_like(dqkv_ref)
        lane = lax.broadcasted_iota(jnp.int32, (c, LANES), 1)
        lane1 = lax.broadcasted_iota(jnp.int32, (1, LANES), 1)
        dirs = (0, 1)
        lane_b = [d * hh + h for d in dirs]
        lane_a = [2 * hh + d * hh + h for d in dirs]
        masks = [_tri_masks(c, d == 1) for d in dirs]

        def fstep(i, sts):
            ns = [i, nc - 1 - i]
            rows = [pl.ds(pl.multiple_of(n * c, c), c) for n in ns]
            for d in dirs:
                st_s[d, ns[d]] = sts[d]
            vn = [u_i[d, rows[d], :] - _mm(w_i[d, rows[d], :], sts[d]) for d in dirs]
            for d in dirs:
                vn_s[d, rows[d], :] = vn[d]
            return tuple(sts[d] * cd_i[d, 0, ns[d]][0:1, :] + _mm_tn(kd_i[d, rows[d], :], vn[d]) for d in dirs)

        z0 = jnp.zeros((128, 128), f32)
        lax.fori_loop(0, nc, fstep, (z0, z0))

        def bstep(i, dst):
            ns = [nc - 1 - i, i]
            rows = [pl.ds(pl.multiple_of(n * c, c), c) for n in ns]
            ld = lambda ref: [ref[d, rows[d], :] for d in dirs]
            ldm = lambda ref: [ref[d, 0, ns[d]] for d in dirs]
            q, k, v = ([ref[rows[d], :] for d in dirs] for ref in (q_ref, k_ref, v_ref))
            dob = [do_ref[rows[d], :] for d in dirs]
            gates = [_gate_cols(ba_ref[rows[d], :], al_ref[...], dt_ref[...], lane_b[d], lane_a[d]) for d in dirs]
            beta, g, aexp, xa = ([gt[x] for gt in gates] for x in range(4))
            st = [st_s[d, ns[d]] for d in dirs]
            vn, u, w, qd, kd = ld(vn_s), ld(u_i), ld(w_i), ld(qd_i), ld(kd_i)
            p = [x.astype(f32) for x in ldm(p_i)]
            tm, dec = ldm(t_i), ldm(dec_i)
            cd = [cd_i[d, 0, ns[d]][0:1, 0:1] for d in dirs]
            dvn = _map(lambda pi, di, ki, si: _mm_tn(pi, di) + _mm(ki, si), p, dob, kd, dst)
            dst_new = _map(lambda a, b, ci, si, wi, dv: _mm_tn(a, b) + ci * si - _mm_tn(wi, dv), qd, dob, cd, dst, w, dvn)
            dp = _map(lambda m, a, b: jnp.where(m[0], _mm_nt(a, b), 0.0), masks, dob, vn)
            dqd = _map(_mm_nt, dob, st)
            dkd = _map(_mm_nt, vn, dst)
            dw = _map(lambda a, b: -_mm_nt(a, b), dvn, st)
            dcd = _map(lambda a, b: jnp.sum(jnp.sum(a * b, axis=1, keepdims=True), axis=0, keepdims=True), st, dst)
            gcum = _map(lambda m, x: _cum_mm(m[0], jnp.broadcast_to(x, (c, c)))[:, 0:1], masks, g)
            glast = [jnp.sum(x, axis=0, keepdims=True) for x in g]
            e = [jnp.exp(x) for x in gcum]
            el = _map(lambda a, b: jnp.exp(a - b), glast, gcum)
            qs = [x * scale for x in q]
            kb = _map(lambda a, b: a * b, k, beta)
            a = _map(lambda m, b, ki, d: jnp.where(m[1], b * _mm_nt(ki, ki) * d, 0.0), masks, beta, k, dec)
            dvb = _map(_hmm3_tn, tm, dvn)
            dkbe = _map(_hmm3_tn, tm, dw)
            da = _map(lambda m, x, ui, y, wi: -jnp.where(m[1], _mm_nt(x, ui) + _mm_nt(y, wi), 0.0), masks, dvb, u, dkbe, w)
            g1 = _map(lambda x, y: x * y, da, dec)
            g2 = _map(lambda x, y: x * y, dp, dec)
            dkb = _map(lambda x, ki, y, ei: _mm(x, ki) + y * ei, g1, k, dkbe, e)
            dk = _map(lambda x, kbi, y, qi, z, b, t, l: _mm_tn(x, kbi) + _mm_tn(y, qi) + z * b + t * l,
                      g1, kb, g2, qs, dkb, beta, dkd, el)
            dqs = _map(lambda y, ki, x, ei: _mm(y, ki) + x * ei, g2, k, dqd, e)
            ddd = _map(lambda x, ai, y, pi: x * ai + y * pi, da, a, dp, p)
            ones = jnp.ones((c, LANES), f32)
            dgcum = _map(lambda x: jnp.sum(x, axis=1, keepdims=True) - _hmm3_tn(x, ones)[:, 0:1], ddd)
            for d in dirs:
                dbeta = jnp.sum(dvb[d] * v[d], axis=1, keepdims=True) + jnp.sum(dkb[d] * k[d], axis=1, keepdims=True)
                de = jnp.sum(dkbe[d] * kb[d], axis=1, keepdims=True) + jnp.sum(dqd[d] * qs[d], axis=1, keepdims=True)
                del_ = jnp.sum(dkd[d] * k[d], axis=1, keepdims=True)
                dgc = dgcum[d] + de * e[d] - del_ * el[d]
                dglast = jnp.sum(del_ * el[d], axis=0, keepdims=True) + dcd[d] * cd[d]
                dg = _cum_mm(masks[d][2], jnp.broadcast_to(dgc, (c, LANES)))[:, 0:1] + dglast
                r = rows[d]
                dqkv_ref[0, r, :] = dqkv_ref[0, r, :] + dqs[d] * scale
                dqkv_ref[1, r, :] = dqkv_ref[1, r, :] + dk[d]
                dqkv_ref[2, r, :] = dqkv_ref[2, r, :] + dvb[d] * beta[d]
                dlb = dbeta * beta[d] * (1.0 - beta[d])
                dalpha = -dg * aexp[d] * _sigmoid(xa[d])
                dba_ref[r, :] = dba_ref[r, :] + jnp.where(lane == lane_b[d], dlb, 0.0) + jnp.where(lane == lane_a[d], dalpha, 0.0)
                dal_ref[...] = dal_ref[...] + jnp.where(lane1 == lane_a[d], jnp.sum(dg * g[d], axis=0, keepdims=True), 0.0)
                ddt_ref[...] = ddt_ref[...] + jnp.where(lane1 == lane_a[d], jnp.sum(dalpha, axis=0, keepdims=True), 0.0)
            return tuple(dst_new)

        lax.fori_loop(0, nc, bstep, (z0, z0))

    blk = lambda off: pl.BlockSpec((s, 128), functools.partial(lambda b, h, off: (b, off + h), off=off))
    row = pl.BlockSpec((1, LANES), lambda b, h: (0, 0))
    tok2, mat2, cd2, _ = _gdn_specs(cfg)
    return pl.pallas_call(
        body, name="gdn_bwd", grid=(cfg.BL, hh),
        in_specs=[blk(0), blk(hh), blk(2 * hh), pl.BlockSpec((s, LANES), lambda b, h: (b, ba_blk)),
                  pl.BlockSpec((s, 128), lambda b, h: (b, h)), row, row, tok2, tok2, tok2, tok2, mat2, mat2, mat2, cd2],
        out_specs=[pl.BlockSpec((3, s, 128), lambda b, h: (0, b, h)), pl.BlockSpec((s, LANES), lambda b, h: (b, 0)), row, row],
        out_shape=[jax.ShapeDtypeStruct((3, cfg.T, cfg.DN), f32), jax.ShapeDtypeStruct((cfg.T, LANES), f32),
                   jax.ShapeDtypeStruct((1, LANES), f32), jax.ShapeDtypeStruct((1, LANES), f32)],
        scratch_shapes=[pltpu.VMEM((2, s, 128), f32), pltpu.VMEM((2, nc, 128, 128), f32)],
        compiler_params=_params(("arbitrary", "arbitrary")),
    )(qkv, qkv, qkv, proj, do, alog_row, dt_row, *saved)


def _b_prep(cfg, proj, dqkv, conv_w):
    dn, s = cfg.DN, cfg.S

    def body(x_ref, dy_ref, w_ref, dx_ref, dw_ref):
        _zero_at_first(pl.program_id(1) == 0, dw_ref)
        sec = pl.program_id(0)
        x = x_ref[...]
        c = _conv_fwd(x, w_ref, 2)
        y, dsilu = _silu(c)
        dy = dy_ref[0]
        parts = []
        for h in range(cfg.H):
            sl = slice(h * 128, (h + 1) * 128)
            yh, dyh = y[:, sl], dy[:, sl]
            inv = lax.rsqrt(jnp.sum(yh * yh, axis=1, keepdims=True) + EPS)
            dn_h = inv * dyh - yh * (inv * inv * inv) * jnp.sum(dyh * yh, axis=1, keepdims=True)
            parts.append(jnp.where(sec < 2, dn_h, dyh))
        ds = jnp.concatenate(parts, axis=1) if len(parts) > 1 else parts[0]
        dx, dws = _conv_bwd(ds * dsilu, x, w_ref, 2)
        for j, dw in enumerate(dws):
            dw_ref[j:j + 1, :] = dw_ref[j:j + 1, :] + dw
        dx_ref[...] = dx

    return pl.pallas_call(
        body, name="dn_prep_bwd", grid=(3, cfg.BL),
        in_specs=[pl.BlockSpec((s, dn), lambda j, b: (b, j)), pl.BlockSpec((1, s, dn), lambda j, b: (j, b, 0)),
                  pl.BlockSpec((4, dn), lambda j, b: (0, j))],
        out_specs=[pl.BlockSpec((s, dn), lambda j, b: (b, j)), pl.BlockSpec((4, dn), lambda j, b: (0, j))],
        out_shape=[jax.ShapeDtypeStruct((cfg.T, 3 * dn), f32), jax.ShapeDtypeStruct((4, 3 * dn), f32)],
        compiler_params=_params(("arbitrary", "arbitrary")),
    )(proj, dqkv, conv_w)


def _b_in(cfg, dqkv_pre, dz, dlx, dlg, dba, dr_mid, r_in, g1, wcat):
    d, dn, lw = cfg.D, cfg.DN, cfg.LW

    def body(dq_ref, dz_ref, dlx_ref, dlg_ref, dba_ref, dr_ref, r_ref, g_ref, w_ref, o_ref, dp_ref, dg_ref):
        _zero_at_first(pl.program_id(0) == 0, dg_ref)
        dp_ref[:, 0:cfg.ZO] = dq_ref[...].astype(bf16)
        dp_ref[:, cfg.ZO:cfg.LXO] = dz_ref[...].astype(bf16)
        dp_ref[:, cfg.LXO:cfg.LGO] = dlx_ref[...].astype(bf16)
        dp_ref[:, cfg.LGO:cfg.BAO] = dlg_ref[...].astype(bf16)
        dp_ref[:, cfg.BAO:] = dba_ref[...].astype(bf16)
        g = g_ref[...]
        _, xh, inv = _rms_fwd(r_ref[...], g)
        dx, dg = _rms_bwd(_mm(dp_ref[...], w_ref[...]), xh, inv, g)
        dg_ref[...] = dg_ref[...] + dg
        o_ref[...] = dr_ref[...] + dx

    return _row_call("in_proj_bwd", body, cfg.T, cfg.TM,
                     [(dqkv_pre, 3 * dn, 0), (dz, dn, 0), (dlx, lw, 0), (dlg, lw, 0), (dba, LANES, 0), (dr_mid, d, 0), (r_in, d, 0)],
                     [g1, wcat], [(d, f32), (cfg.PC, bf16)], [((1, d), f32)])


def _adam_call(name, w, g, m, v, tr):
    rows, cols = w.shape
    bc1 = 1.0 - ADAM_B1 ** ADAM_STEP
    bc2 = 1.0 - ADAM_B2 ** ADAM_STEP

    def body(w_ref, g_ref, m_ref, v_ref, d_ref, nm_ref, nv_ref):
        g = g_ref[...]
        m = ADAM_B1 * m_ref[...] + (1.0 - ADAM_B1) * g
        v = ADAM_B2 * v_ref[...] + (1.0 - ADAM_B2) * (g * g)
        nm_ref[...] = m
        nv_ref[...] = v
        d_ref[...] = -ADAM_LR * ((m / bc1) / (jnp.sqrt(v / bc2) + ADAM_EPS) + ADAM_WD * w_ref[...])

    spec = pl.BlockSpec((tr, cols), lambda i: (i, 0))
    return pl.pallas_call(
        body, name=name, grid=(rows // tr,), in_specs=[spec] * 4, out_specs=[spec] * 3,
        out_shape=[jax.ShapeDtypeStruct((rows, cols), f32)] * 3, compiler_params=_params(("arbitrary",)),
    )(w, g, m, v)


def _sum8_call(name, x, tr):
    _, rows, cols = x.shape

    def body(x_ref, o_ref):
        acc = x_ref[0].astype(f32)
        for j in range(1, N_DEV):
            acc = acc + x_ref[j].astype(f32)
        o_ref[...] = acc

    return pl.pallas_call(
        body, name=name, grid=(rows // tr,), in_specs=[pl.BlockSpec((N_DEV, tr, cols), lambda i: (0, i, 0))],
        out_specs=pl.BlockSpec((tr, cols), lambda i: (i, 0)), out_shape=jax.ShapeDtypeStruct((rows, cols), f32),
        compiler_params=_params(("arbitrary",)),
    )(x)


def _all_gather(name, shards):
    na = len(shards)

    def body(*refs):
        xs, outs = refs[:na], refs[na:2 * na]
        send_sems, recv_sems, local_sems = refs[2 * na:]
        x, y, c = lax.axis_index("x"), lax.axis_index("y"), lax.axis_index("c")
        me, sibling = (x, y, c), (x, y, 1 - c)
        chips = [(1 - x, y), (x, 1 - y), (1 - x, 1 - y)]

        def copy(a, k, block, to, src=None):
            px, py, pc = block
            slot = outs[a].at[4 * px + 2 * py + pc]
            return pltpu.make_async_remote_copy(
                src_ref=slot if src is None else src, dst_ref=slot,
                send_sem=send_sems.at[a, k], recv_sem=recv_sems.at[a, k], device_id=to, device_id_type=MESH)

        mine = [pltpu.make_async_copy(xs[a], outs[a].at[4 * x + 2 * y + c], local_sems.at[a]) for a in range(na)]
        for cp in mine:
            cp.start()
        first = []
        for a in range(na):
            first.append(copy(a, 0, me, sibling, src=xs[a]))
            first += [copy(a, 1 + j, me, (*chip, c), src=xs[a]) for j, chip in enumerate(chips)]
        for cp in first:
            cp.start()
        passed = []
        for j, chip in enumerate(chips):
            for a in range(na):
                copy(a, 1 + j, (*chip, c), me).wait_recv()
                cp = copy(a, 4 + j, (*chip, c), sibling)
                cp.start()
                passed.append(cp)
        for a in range(na):
            copy(a, 0, sibling, me).wait_recv()
            for j, chip in enumerate(chips):
                copy(a, 4 + j, (*chip, 1 - c), me).wait_recv()
        for cp in first + passed:
            cp.wait_send()
        for cp in mine:
            cp.wait()

    hbm = pl.BlockSpec(memory_space=pltpu.HBM)
    return pl.pallas_call(
        body, name=name, out_shape=[jax.ShapeDtypeStruct((N_DEV,) + s.shape, s.dtype) for s in shards],
        in_specs=[hbm] * na, out_specs=[hbm] * na,
        scratch_shapes=[pltpu.SemaphoreType.DMA((na, 7)), pltpu.SemaphoreType.DMA((na, 7)), pltpu.SemaphoreType.DMA((na,))],
    )(*shards)


def _all_to_all(name, blocks):
    na = len(blocks)

    def body(*refs):
        xs, outs = refs[:na], refs[na:2 * na]
        send_sems, recv_sems, local_sems = refs[2 * na:]
        x, y, c = lax.axis_index("x"), lax.axis_index("y"), lax.axis_index("c")
        me = 4 * x + 2 * y + c
        mine = [pltpu.make_async_copy(xs[a].at[me], outs[a].at[me], local_sems.at[a]) for a in range(na)]
        for cp in mine:
            cp.start()
        copies = []
        for k in range(1, N_DEV):
            px, py, pc = x ^ (k >> 2), y ^ ((k >> 1) & 1), c ^ (k & 1)
            for a in range(na):
                copies.append(pltpu.make_async_remote_copy(
                    src_ref=xs[a].at[4 * px + 2 * py + pc], dst_ref=outs[a].at[me],
                    send_sem=send_sems.at[a, k - 1], recv_sem=recv_sems.at[a, k - 1], device_id=(px, py, pc),
                    device_id_type=MESH))
        for cp in copies:
            cp.start()
        for cp in copies:
            cp.wait_recv()
        for cp in copies:
            cp.wait_send()
        for cp in mine:
            cp.wait()

    hbm = pl.BlockSpec(memory_space=pltpu.HBM)
    return pl.pallas_call(
        body, name=name, out_shape=[jax.ShapeDtypeStruct(b.shape, b.dtype) for b in blocks],
        in_specs=[hbm] * na, out_specs=[hbm] * na,
        scratch_shapes=[pltpu.SemaphoreType.DMA((na, 7)), pltpu.SemaphoreType.DMA((na, 7)), pltpu.SemaphoreType.DMA((na,))],
    )(*blocks)


def _sum8_layers(name, x):
    _, nl, rows, cols = x.shape

    def body(x_ref, o_ref):
        acc = x_ref[0, 0].astype(f32)
        for j in range(1, N_DEV):
            acc = acc + x_ref[j, 0].astype(f32)
        o_ref[0] = acc

    return pl.pallas_call(
        body, name=name, grid=(nl,), in_specs=[pl.BlockSpec((N_DEV, 1, rows, cols), lambda i: (0, i, 0, 0))],
        out_specs=pl.BlockSpec((1, rows, cols), lambda i: (i, 0, 0)), out_shape=jax.ShapeDtypeStruct((nl, rows, cols), f32),
        compiler_params=_params(("arbitrary",)),
    )(x)


def _layer_fwd(cfg, w, r, p):
    h1, proj = _k_in(cfg, r, w["norm1_g"], w["wcat_t"])
    qkv = _k_prep(cfg, proj, w["dn_conv_w"])
    o, *gdn_saved = _k_gdn_fwd(cfg, qkv, proj, w["alog_row"], w["dt_row"])
    hs = _k_lru_fwd(cfg, proj, w["lru_conv_w"], w["lru_conv_b"], w["wa"], w["wx"], w["lru_ba"], w["lru_bx"], w["lru_lambda"])
    mix, r_mid = _k_mix(cfg, o, proj, hs, r, w["dn_norm_g"], w["lru_norm_g"], w["w_out"])
    h2, gp, up = _k_ffn_a(cfg, r_mid, w["norm2_g"], w["ffn_wg_t"], w["ffn_wu_t"])
    act = _k_ffn_b(cfg, gp, up, w["ffn_conv_w"], w["ffn_conv_b"])
    r2 = _k_ffn_c(cfg, act, r_mid, w["ffn_wd"])
    pn, r3 = _k_ple(cfg, r2, p, w["ple_norm_g"], w["ple_wg"], w["ple_bg"], w["ple_wp_t"])
    saved = dict(r=r, h1=h1, proj=proj, qkv=qkv, o=o, gdn=gdn_saved, hs=hs, mix=mix, r_mid=r_mid, h2=h2, gp=gp, up=up,
                 r2=r2, pn=pn, p=p)
    return r3, saved


def _layer_bwd(cfg, w, sv, dr3):
    g = {}
    dt = min(512, cfg.D)
    dr2, dlog, dpp, g["ple_norm_g"], g["ple_bg"] = _b_ple(cfg, dr3, sv["r2"], sv["p"], w["ple_norm_g"], w["ple_wg"], w["ple_bg"], w["ple_wp_t"])
    g["ple_wg"] = _mm_tn_call(cfg, "d_ple_wg", sv["pn"], dlog, dt)
    g["ple_wp_t"] = _mm_tn_call(cfg, "d_ple_wp", dpp, sv["p"], cfg.PD)
    dgp, dup, g["ffn_wd"], g["ffn_conv_w"], g["ffn_conv_b"] = _b_ffn_bc(cfg, dr2, w["ffn_wd"], sv["gp"], sv["up"], w["ffn_conv_w"], w["ffn_conv_b"])
    dr_mid, g["norm2_g"] = _b_ffn_a(cfg, dgp, dup, dr2, sv["r_mid"], w["norm2_g"], w["ffn_wg_t"], w["ffn_wu_t"])
    g["ffn_wg_t"] = _mm_tn_call(cfg, "d_ffn_wg", dgp, sv["h2"], dt)
    g["ffn_wu_t"] = _mm_tn_call(cfg, "d_ffn_wu", dup, sv["h2"], dt)
    do, dz, dlg, dhs, g["dn_norm_g"], g["lru_norm_g"] = _b_mix(cfg, dr_mid, sv["o"], sv["proj"], sv["hs"], w["dn_norm_g"], w["lru_norm_g"], w["w_out"])
    g["w_out"] = _mm_tn_call(cfg, "d_w_out", sv["mix"], dr_mid, dt)
    dlx, g["lru_conv_w"], g["lru_conv_b"], g["wa"], g["wx"], g["lru_ba"], g["lru_bx"], g["lru_lambda"] = _b_lru(
        cfg, sv["proj"], dhs, w["lru_conv_w"], w["lru_conv_b"], w["wa"], w["wx"], w["lru_ba"], w["lru_bx"], w["lru_lambda"])
    dqkv, dba, g["alog_row"], g["dt_row"] = _b_gdn(cfg, sv["qkv"], sv["proj"], do, w["alog_row"], w["dt_row"], sv["gdn"])
    dqkv_pre, g["dn_conv_w"] = _b_prep(cfg, sv["proj"], dqkv, w["dn_conv_w"])
    dr, dproj, g["norm1_g"] = _b_in(cfg, dqkv_pre, dz, dlx, dlg, dba, dr_mid, sv["r"], w["norm1_g"], w["wcat_t"])
    g["wcat_t"] = _mm_tn_call(cfg, "d_w_in", dproj, sv["h1"], dt)
    return dr, g


BIG = ("w_in", "w_out", "ffn_wg", "ffn_wu", "ffn_wd", "ple_wg", "ple_wp")
BIG_T = {"w_in": True, "w_out": False, "ffn_wg": True, "ffn_wu": True, "ffn_wd": False, "ple_wg": False, "ple_wp": True}
BIG_OPERAND = {"w_in": "wcat_t", "w_out": "w_out", "ffn_wg": "ffn_wg_t", "ffn_wu": "ffn_wu_t", "ffn_wd": "ffn_wd",
               "ple_wg": "ple_wg", "ple_wp": "ple_wp_t"}
SMALL_SHARDED = ("dn_conv_w", "lru_conv_w", "lru_ba", "lru_bx", "lru_lambda", "ffn_conv_w")
SMALL_REPL = ("norm1_g", "dn_a_log", "dn_dt_bias", "dn_norm_g", "lru_conv_b", "lru_wa", "lru_wx", "lru_norm_g", "norm2_g",
              "ffn_conv_b", "ple_norm_g", "ple_bg", "final_g")
WEIGHTS = ("norm1_g", "w_in", "dn_conv_w", "dn_a_log", "dn_dt_bias", "dn_norm_g", "lru_conv_w", "lru_conv_b", "lru_wa",
           "lru_ba", "lru_wx", "lru_bx", "lru_lambda", "lru_norm_g", "w_out", "norm2_g", "ffn_wg", "ffn_wu", "ffn_conv_w",
           "ffn_conv_b", "ffn_wd", "ple_norm_g", "ple_wg", "ple_bg", "ple_wp", "final_g")


def _pad_rows(flat, cols, mult):
    n = flat.shape[0]
    rows = -(-n // cols)
    rows = -(-rows // mult) * mult
    return jnp.pad(flat, (0, rows * cols - n)).reshape(rows, cols)


def _pack(arrs, cols, mult, dtype):
    return _pad_rows(jnp.concatenate([a.reshape(-1).astype(dtype) for a in arrs]), cols, mult)


def _unpack(flat, shapes):
    out, off = [], 0
    for shp in shapes:
        n = math.prod(shp)
        out.append(flat[off:off + n].reshape(shp))
        off += n
    return out


def _unpack8(g8, shapes, axes):
    out, off = [], 0
    for shp, ax in zip(shapes, axes):
        n = math.prod(shp)
        a = g8[:, off:off + n].reshape((N_DEV,) + tuple(shp))
        a = jnp.moveaxis(a, 0, ax)
        out.append(a.reshape(shp[:ax] + (N_DEV * shp[ax],) + shp[ax + 1:]))
        off += n
    return out


def _wcat_t_from_w_in_t(cfg, wt):
    nba = 4 * cfg.H
    pad = jnp.zeros((LANES - nba, wt.shape[1]), wt.dtype)
    return jnp.concatenate([wt[:cfg.LXO], wt[cfg.LXO + nba:], wt[cfg.LXO:cfg.LXO + nba], pad], axis=0)


def _w_in_t_from_wcat_t(cfg, wc):
    nba = 4 * cfg.H
    return jnp.concatenate([wc[:cfg.LXO], wc[cfg.BAO:cfg.BAO + nba], wc[cfg.LXO:cfg.BAO]], axis=0)


def _gate_row(cfg, a):
    h2 = 2 * cfg.H
    return jnp.concatenate([jnp.zeros((1, h2), f32), a.reshape(1, h2), jnp.zeros((1, LANES - 2 * h2), f32)], axis=1)


def _blockdiag(cfg, w):
    w = w.reshape(2, cfg.NB // 2, 2, 64, 64)
    z = jnp.zeros_like(w[:, :, 0])
    top = jnp.concatenate([w[:, :, 0], z], axis=-1)
    bot = jnp.concatenate([z, w[:, :, 1]], axis=-1)
    return jnp.concatenate([top, bot], axis=-2).astype(bf16)


def _unblockdiag(cfg, g):
    a = g[:, :, :64, :64]
    b = g[:, :, 64:, 64:]
    return jnp.stack([a, b], axis=2).reshape(2, cfg.NB, 64, 64)


def _layer_operands(cfg, big, small, i):
    return dict(
        wcat_t=_wcat_t_from_w_in_t(cfg, big["w_in"][i]), w_out=big["w_out"][i], ffn_wg_t=big["ffn_wg"][i],
        ffn_wu_t=big["ffn_wu"][i], ffn_wd=big["ffn_wd"][i], ple_wg=big["ple_wg"][i], ple_wp_t=big["ple_wp"][i],
        norm1_g=small["norm1_g"][i][None], dn_conv_w=small["dn_conv_w"][i], alog_row=_gate_row(cfg, small["dn_a_log"][i]),
        dt_row=_gate_row(cfg, small["dn_dt_bias"][i]), dn_norm_g=small["dn_norm_g"][i][None],
        lru_conv_w=small["lru_conv_w"][i], lru_conv_b=small["lru_conv_b"][i][None],
        wa=_blockdiag(cfg, small["lru_wa"][i]), wx=_blockdiag(cfg, small["lru_wx"][i]),
        lru_ba=small["lru_ba"][i], lru_bx=small["lru_bx"][i], lru_lambda=small["lru_lambda"][i],
        lru_norm_g=small["lru_norm_g"][i][None], norm2_g=small["norm2_g"][i][None], ffn_conv_w=small["ffn_conv_w"][i],
        ffn_conv_b=small["ffn_conv_b"][i][None], ple_norm_g=small["ple_norm_g"][i][None], ple_bg=small["ple_bg"][i][None],
    )


def _small_grads_to_problem(cfg, g):
    h = cfg.H
    return dict(
        norm1_g=g["norm1_g"][0], dn_conv_w=g["dn_conv_w"],
        dn_a_log=g["alog_row"][0, 2 * h:4 * h].reshape(2, h), dn_dt_bias=g["dt_row"][0, 2 * h:4 * h].reshape(2, h),
        dn_norm_g=g["dn_norm_g"][0], lru_conv_w=g["lru_conv_w"], lru_conv_b=g["lru_conv_b"][0],
        lru_wa=_unblockdiag(cfg, g["wa"]), lru_wx=_unblockdiag(cfg, g["wx"]), lru_ba=g["lru_ba"], lru_bx=g["lru_bx"],
        lru_lambda=g["lru_lambda"], lru_norm_g=g["lru_norm_g"][0], norm2_g=g["norm2_g"][0], ffn_conv_w=g["ffn_conv_w"],
        ffn_conv_b=g["ffn_conv_b"][0], ple_norm_g=g["ple_norm_g"][0], ple_bg=g["ple_bg"][0],
    )


def _local_step(cfg, big, small, x, p, target):
    r = x.reshape(cfg.T, cfg.D)
    ops, saved = [], []
    for i in range(cfg.L):
        w = _layer_operands(cfg, big, small, i)
        r, sv = _layer_fwd(cfg, w, r, p[i].reshape(cfg.T, cfg.PD))
        ops.append(w)
        saved.append(sv)
    dr, loss, dgf = _k_loss(cfg, r, target.reshape(cfg.T, cfg.D), small["final_g"][None])
    gbig, gsmall = [None] * cfg.L, [None] * cfg.L
    for i in reversed(range(cfg.L)):
        dr, g = _layer_bwd(cfg, ops[i], saved[i], dr)
        gbig[i] = {n: (_w_in_t_from_wcat_t(cfg, g["wcat_t"]) if n == "w_in" else g[BIG_OPERAND[n]]) for n in BIG}
        gsmall[i] = _small_grads_to_problem(cfg, g)
    gs = {k: jnp.stack([gl[k] for gl in gsmall]) for k in gsmall[0]}
    gs["final_g"] = dgf[0]
    return loss, dr, gbig, gs


def _row_tile(rows, limit=512):
    best = rows
    for t in range(8, min(rows, limit) + 1, 8):
        if rows % t == 0:
            best = t
    return best if best <= limit or rows <= limit else rows


def _adam_group(name, ws, gs, ms, vs, cols, tr):
    shapes = [w.shape for w in ws]
    pk = lambda arrs: _pack(arrs, cols, tr, f32)
    w2 = pk(ws)
    d, nm, nv = _adam_call(name, w2, pk(gs), pk(ms), pk(vs), min(tr, w2.shape[0]))
    return [_unpack(a.reshape(-1), shapes) for a in (d, nm, nv)]


def kernel(x, p, norm1_g, w_in, dn_conv_w, dn_a_log, dn_dt_bias, dn_norm_g, lru_conv_w, lru_conv_b, lru_wa, lru_ba, lru_wx, lru_bx, lru_lambda, lru_norm_g, w_out, norm2_g, ffn_wg, ffn_wu, ffn_conv_w, ffn_conv_b, ffn_wd, ple_norm_g, ple_wg, ple_bg, ple_wp, final_g, loss_target, m_norm1_g, m_w_in, m_dn_conv_w, m_dn_a_log, m_dn_dt_bias, m_dn_norm_g, m_lru_conv_w, m_lru_conv_b, m_lru_wa, m_lru_ba, m_lru_wx, m_lru_bx, m_lru_lambda, m_lru_norm_g, m_w_out, m_norm2_g, m_ffn_wg, m_ffn_wu, m_ffn_conv_w, m_ffn_conv_b, m_ffn_wd, m_ple_norm_g, m_ple_wg, m_ple_bg, m_ple_wp, m_final_g, v_norm1_g, v_w_in, v_dn_conv_w, v_dn_a_log, v_dn_dt_bias, v_dn_norm_g, v_lru_conv_w, v_lru_conv_b, v_lru_wa, v_lru_ba, v_lru_wx, v_lru_bx, v_lru_lambda, v_lru_norm_g, v_w_out, v_norm2_g, v_ffn_wg, v_ffn_wu, v_ffn_conv_w, v_ffn_conv_b, v_ffn_wd, v_ple_norm_g, v_ple_wg, v_ple_bg, v_ple_wp, v_final_g):
    cfg = CFG
    a = dict(locals())
    wl = {n: a[n] for n in WEIGHTS}
    ml = {n: a["m_" + n] for n in WEIGHTS}
    vl = {n: a["v_" + n] for n in WEIGHTS}
    me = 4 * lax.axis_index("x") + 2 * lax.axis_index("y") + lax.axis_index("c")
    nl = cfg.L

    blocks = [(jnp.swapaxes(wl[n], 1, 2) if BIG_T[n] else wl[n]).astype(bf16) for n in BIG]
    ss_shapes = [wl[n].shape for n in SMALL_SHARDED]
    *g8, s8 = _all_gather("gather_weights", blocks + [_pack([wl[n] for n in SMALL_SHARDED], LANES, 8, f32)])
    big = {n: jnp.moveaxis(g, 0, 1).reshape(nl, N_DEV * g.shape[2], g.shape[3]) for n, g in zip(BIG, g8)}
    small = dict(zip(SMALL_SHARDED, _unpack8(s8.reshape(N_DEV, -1), ss_shapes, [2] * len(ss_shapes))))
    small.update({n: wl[n] for n in SMALL_REPL})

    loss_part, dr, gbig, gsmall = _local_step(cfg, big, small, x, p, loss_target)
    grad_x = dr.reshape(x.shape)

    send = []
    for n, blk in zip(BIG, blocks):
        g = jnp.stack([gbig[i][n].reshape((N_DEV,) + blk.shape[1:]) for i in range(nl)], axis=1)
        send.append(g.astype(bf16))
    recv = _all_to_all("exchange_grads", send)
    gl = {}
    for n, r8 in zip(BIG, recv):
        s = _sum8_layers("sum_" + n, r8)
        gl[n] = jnp.swapaxes(s, 1, 2) if BIG_T[n] else s

    small_names = SMALL_REPL + SMALL_SHARDED
    small_shapes = [gsmall[n].shape for n in small_names]
    sv = _pack([gsmall[n] for n in small_names] + [loss_part[0, 0:1]], LANES, 512, f32)
    small_sum = _sum8_call("sum_small", _all_gather("gather_small_grads", [sv])[0], 512).reshape(-1)
    gl.update(zip(small_names, _unpack(small_sum, small_shapes)))
    loss = small_sum[sum(math.prod(s) for s in small_shapes)]
    for n in SMALL_SHARDED:
        shard = wl[n].shape[2]
        gl[n] = lax.dynamic_slice_in_dim(gl[n], me * shard, shard, axis=2)

    outs = {}
    for n in BIG:
        shp = wl[n].shape
        two = lambda t: t.reshape(-1, shp[-1])
        d, nm, nv = _adam_call("adam_" + n, two(wl[n]), two(gl[n]), two(ml[n]), two(vl[n]), _row_tile(math.prod(shp[:-1])))
        outs[n] = (d.reshape(shp), nm.reshape(shp), nv.reshape(shp))
    d, nm, nv = _adam_group("adam_small", [wl[n] for n in small_names], [gl[n] for n in small_names],
                            [ml[n] for n in small_names], [vl[n] for n in small_names], LANES, 64)
    for j, n in enumerate(small_names):
        outs[n] = (d[j], nm[j], nv[j])
    return (loss, grad_x, *[gl[n] for n in WEIGHTS], *[outs[n][0] for n in WEIGHTS], *[outs[n][1] for n in WEIGHTS],
            *[outs[n][2] for n in WEIGHTS])
```

```python
import functools
import math

import jax
import jax.numpy as jnp
from jax import lax
from jax.experimental import pallas as pl
from jax.experimental.pallas import tpu as pltpu

f32 = jnp.float32
bf16 = jnp.bfloat16
MESH = pl.DeviceIdType.MESH

N_DEV = 8
LANES = 128
EPS = 1e-6
LRU_C = 8.0
ADAM_LR, ADAM_B1, ADAM_B2, ADAM_EPS, ADAM_WD, ADAM_STEP = 0.001, 0.9, 0.999, 1e-08, 0.01, 10
VMEM_LIMIT = 56 * 1024 * 1024


class Cfg:
    def __init__(self, d_model=1024, bl=4, seq=2048, depth=4, heads=4, lru_width=512, d_ff=2816, ple=256,
                 tm=512, tm_ffn=256, ff_tile=256):
        self.D, self.BL, self.S, self.L, self.H = d_model, bl, seq, depth, heads
        self.DH = 128
        self.DN = heads * self.DH
        self.LW = lru_width
        self.NB = lru_width // 64
        self.FF, self.PD = d_ff, ple
        self.C = 64
        self.NC = seq // self.C
        self.T = bl * seq
        self.TM = min(tm, self.T)
        self.TMF = min(tm_ffn, self.T)
        self.FT = ff_tile
        self.ZO = 3 * self.DN
        self.LXO = 4 * self.DN
        self.LGO = self.LXO + self.LW
        self.BAO = self.LGO + self.LW
        self.PC = self.BAO + LANES
        self.IN_COLS = 4 * self.DN + 4 * heads + 2 * self.LW


CFG = Cfg()


def _mm(a, b):
    return jnp.dot(a.astype(bf16), b.astype(bf16), preferred_element_type=f32)


def _mm_nt(a, b):
    return lax.dot_general(a.astype(bf16), b.astype(bf16), (((1,), (1,)), ((), ())), preferred_element_type=f32)


def _mm_tn(a, b):
    return lax.dot_general(a.astype(bf16), b.astype(bf16), (((0,), (0,)), ((), ())), preferred_element_type=f32)


def _split2(a):
    hi = a.astype(bf16)
    return hi, (a - hi.astype(f32)).astype(bf16)


def _hmm3(a, b, dims=(((1,), (0,)), ((), ()))):
    ah, al = _split2(a)
    bh, bl = _split2(b)
    dot = functools.partial(lax.dot_general, dimension_numbers=dims, preferred_element_type=f32)
    return dot(ah, bh) + dot(ah, bl) + dot(al, bh)


def _hmm3_tn(a, b):
    return _hmm3(a, b, (((0,), (0,)), ((), ())))


def _cum_mm(mask, x, dims=(((1,), (0,)), ((), ()))):
    m = mask.astype(bf16)
    x1 = x.astype(bf16)
    r = x - x1.astype(f32)
    x2 = r.astype(bf16)
    x3 = (r - x2.astype(f32)).astype(bf16)
    dot = functools.partial(lax.dot_general, dimension_numbers=dims, preferred_element_type=f32)
    return dot(m, x1) + dot(m, x2) + dot(m, x3)


def _rms_fwd(x, g):
    inv = lax.rsqrt(jnp.mean(x * x, axis=-1, keepdims=True) + EPS)
    xh = x * inv
    return xh * g, xh, inv


def _rms_bwd(dy, xh, inv, g):
    dxh = dy * g
    dx = inv * (dxh - xh * jnp.mean(dxh * xh, axis=-1, keepdims=True))
    dg = jnp.sum(dy * xh, axis=0, keepdims=True)
    return dx, dg


def _sigmoid(x):
    return 1.0 / (1.0 + jnp.exp(-x))


def _softplus(x):
    return jnp.maximum(x, 0.0) + jnp.log(1.0 + jnp.exp(-jnp.abs(x)))


def _silu(x):
    s = _sigmoid(x)
    return x * s, s * (1.0 + x * (1.0 - s))


_GC = math.sqrt(2.0 / math.pi)


def _gelu(x):
    t = jnp.tanh(_GC * (x + 0.044715 * x * x * x))
    y = 0.5 * x * (1.0 + t)
    dy = 0.5 * (1.0 + t) + 0.5 * x * (1.0 - t * t) * _GC * (1.0 + 3.0 * 0.044715 * x * x)
    return y, dy


def _nexpm1(x):
    ser = -x * (1.0 + x * 0.5 * (1.0 + x * (1.0 / 3.0) * (1.0 + x * 0.25 * (1.0 + x * 0.2))))
    return jnp.where(x > -0.1, ser, 1.0 - jnp.exp(x))


def _shift(x, s, fill=0.0):
    if s == 0:
        return x
    n = x.shape[0]
    t = lax.broadcasted_iota(jnp.int32, x.shape, 0)
    r = pltpu.roll(x, (-s) % n, 0)
    return jnp.where((t + s >= 0) & (t + s < n), r, fill)


def _conv_fwd(x, w_ref, left):
    k = w_ref.shape[0]
    out = _shift(x, -left) * w_ref[0:1, :]
    for j in range(1, k):
        out = out + _shift(x, j - left) * w_ref[j:j + 1, :]
    return out


def _conv_bwd(dout, x, w_ref, left):
    k = w_ref.shape[0]
    dx = None
    dws = []
    for j in range(k):
        term = _shift(dout, -(j - left)) * w_ref[j:j + 1, :]
        dx = term if dx is None else dx + term
        dws.append(jnp.sum(dout * _shift(x, j - left), axis=0, keepdims=True))
    return dx, dws


def _lin_scan(a, b, rev):
    n = a.shape[0]
    d = 1
    while d < n:
        s = d if rev else -d
        b = a * _shift(b, s, 0.0) + b
        a = a * _shift(a, s, 1.0)
        d *= 2
    return b


def _tri_masks(c, rev):
    i = lax.broadcasted_iota(jnp.int32, (c, c), 0)
    j = lax.broadcasted_iota(jnp.int32, (c, c), 1)
    incl = (i <= j) if rev else (i >= j)
    strict = (i < j) if rev else (i > j)
    incl_t = (i >= j) if rev else (i <= j)
    return incl, strict, incl_t


def _map(f, *lists):
    return [f(*a) for a in zip(*lists)]


def _tri_inv(mats):
    c = mats[0].shape[0]
    i = lax.broadcasted_iota(jnp.int32, (c, c), 0)
    j = lax.broadcasted_iota(jnp.int32, (c, c), 1)
    eye = jnp.where(i == j, 1.0, 0.0)
    t = [eye - a for a in mats]
    pw = _map(_hmm3, mats, mats)
    for it in range(5):
        t = _map(lambda ti, ui: ti + ui, t, _map(_hmm3, t, pw))
        if it < 4:
            pw = _map(_hmm3, pw, pw)
    return t


def _gdn_decay(gs, revs):
    c = gs[0].shape[0]
    masks = [_tri_masks(c, r) for r in revs]
    gb = [jnp.broadcast_to(g, (c, c)) for g in gs]
    mcol = _map(lambda m, x: _cum_mm(m[0], x), masks, gb)
    mrow = _map(lambda m, x: jnp.sum(jnp.where(m[2], x, 0.0), axis=0, keepdims=True), masks, gb)
    dec = _map(lambda m, a, b: jnp.exp(jnp.where(m[0], a - b, -1e30)), masks, mcol, mrow)
    return [m[:, 0:1] for m in mcol], [jnp.sum(g, axis=0, keepdims=True) for g in gs], dec


def _gdn_prep(qs, k, v, g, beta, kk, qk, revs):
    c = k[0].shape[0]
    masks = [_tri_masks(c, r) for r in revs]
    gcum, glast, dec = _gdn_decay(g, revs)
    e = [jnp.exp(x) for x in gcum]
    tm = _tri_inv(_map(lambda m, b, x, d: jnp.where(m[1], b * x * d, 0.0), masks, beta, kk, dec))
    u = _map(lambda t, vi, b: _hmm3(t, vi * b), tm, v, beta)
    w = _map(lambda t, ki, b, ei: _hmm3(t, ki * b * ei), tm, k, beta, e)
    p = _map(lambda m, x, d: jnp.where(m[0], x * d, 0.0), masks, qk, dec)
    return dict(dec=dec, cd=[jnp.exp(x) for x in glast], tm=tm, u=u, w=w, p=p, qd=_map(lambda a, b: a * b, qs, e),
                kd=_map(lambda ki, gl, gc: ki * jnp.exp(gl - gc), k, glast, gcum))


def _lane_pick(x, lane):
    l = lax.broadcasted_iota(jnp.int32, x.shape, 1)
    return jnp.sum(jnp.where(l == lane, x, 0.0), axis=1, keepdims=True)


def _params(sem, vmem=VMEM_LIMIT):
    return pltpu.CompilerParams(dimension_semantics=sem, vmem_limit_bytes=vmem)


def _row_call(name, body, t, tm, row_ins, full_ins, row_outs, acc_outs):
    in_specs = [pl.BlockSpec((tm, w), functools.partial(lambda i, c: (i, c), c=c)) for (_, w, c) in row_ins]
    for a in full_ins:
        in_specs.append(pl.BlockSpec(a.shape, functools.partial(lambda i, n: (0,) * n, n=a.ndim)))
    out_specs = [pl.BlockSpec((tm, w), lambda i: (i, 0)) for (w, _) in row_outs]
    out_shape = [jax.ShapeDtypeStruct((t, w), dt) for (w, dt) in row_outs]
    for shp, dt in acc_outs:
        out_specs.append(pl.BlockSpec(shp, functools.partial(lambda i, n: (0,) * n, n=len(shp))))
        out_shape.append(jax.ShapeDtypeStruct(shp, dt))
    return pl.pallas_call(
        body, name=name, grid=(t // tm,), in_specs=in_specs, out_specs=out_specs, out_shape=out_shape,
        compiler_params=_params(("arbitrary",)),
    )(*[a for (a, _, _) in row_ins], *full_ins)


def _k_in(cfg, r, g1, wcat):
    def body(r_ref, g_ref, w_ref, h_ref, proj_ref):
        y, _, _ = _rms_fwd(r_ref[...], g_ref[...])
        hb = y.astype(bf16)
        h_ref[...] = hb
        proj_ref[...] = _mm_nt(hb, w_ref[...])

    return _row_call("in_proj", body, cfg.T, cfg.TM, [(r, cfg.D, 0)], [g1, wcat],
                     [(cfg.D, bf16), (cfg.PC, f32)], [])


def _k_prep(cfg, proj, conv_w):
    dn, s = cfg.DN, cfg.S

    def body(x_ref, w_ref, o_ref):
        sec = pl.program_id(1)
        c = _conv_fwd(x_ref[...], w_ref, 2)
        y, _ = _silu(c)
        for h in range(cfg.H):
            yh = y[:, h * 128:(h + 1) * 128]
            nh = yh * lax.rsqrt(jnp.sum(yh * yh, axis=1, keepdims=True) + EPS)
            o_ref[:, h * 128:(h + 1) * 128] = jnp.where(sec < 2, nh, yh)

    return pl.pallas_call(
        body, name="dn_prep", grid=(cfg.BL, 3),
        in_specs=[pl.BlockSpec((s, dn), lambda b, j: (b, j)), pl.BlockSpec((4, dn), lambda b, j: (0, j))],
        out_specs=pl.BlockSpec((s, dn), lambda b, j: (b, j)),
        out_shape=jax.ShapeDtypeStruct((cfg.T, 3 * dn), f32),
        compiler_params=_params(("arbitrary", "arbitrary")),
    )(proj, conv_w)


def _gate_cols(ba, alog_row, dt_row, lane_b, lane_a):
    beta = _sigmoid(_lane_pick(ba, lane_b))
    alpha = _lane_pick(ba, lane_a)
    aexp = jnp.exp(_lane_pick(alog_row, lane_a))
    dtb = _lane_pick(dt_row, lane_a)
    xa = alpha + dtb
    g = -aexp * _softplus(xa)
    return beta, g, aexp, xa


def _gdn_specs(cfg):
    s, c, nc = cfg.S, cfg.C, cfg.NC
    tok2 = pl.BlockSpec((2, s, 128), lambda b, h: (0, b, h))
    mat2 = pl.BlockSpec((2, 1, nc, c, c), lambda b, h: (0, b, h, 0, 0))
    cd2 = pl.BlockSpec((2, 1, nc, 8, LANES), lambda b, h: (0, b, h, 0, 0))
    shapes = dict(
        tok32=jax.ShapeDtypeStruct((2, cfg.T, cfg.DN), f32), tok16=jax.ShapeDtypeStruct((2, cfg.T, cfg.DN), bf16),
        mat32=jax.ShapeDtypeStruct((2, cfg.BL, cfg.H * nc, c, c), f32), mat16=jax.ShapeDtypeStruct((2, cfg.BL, cfg.H * nc, c, c), bf16),
        cd=jax.ShapeDtypeStruct((2, cfg.BL, cfg.H * nc, 8, LANES), f32))
    return tok2, mat2, cd2, shapes


def _k_gdn_fwd(cfg, qkv, proj, alog_row, dt_row):
    s, c, nc, hh = cfg.S, cfg.C, cfg.NC, cfg.H
    ba_blk = cfg.BAO // LANES
    per = 2 if nc % 2 == 0 else 1

    def body(q_ref, k_ref, v_ref, ba_ref, al_ref, dt_ref, o_ref, u_o, w_o, qd_o, kd_o, p_o, t_o, dec_o, cd_o):
        h = pl.program_id(1)

        def prep(i, carry):
            chains, qs, k, v, kk, qk, g, beta = [], [], [], [], [], [], [], []
            for j in range(per):
                n = i * per + j
                rows = pl.ds(pl.multiple_of(n * c, c), c)
                kj, vj = k_ref[rows, :], v_ref[rows, :]
                qj = q_ref[rows, :] * (128 ** -0.5)
                kkj, qkj = _mm_nt(kj, kj), _mm_nt(qj, kj)
                ba = ba_ref[rows, :]
                for d in range(2):
                    bd, gd, _, _ = _gate_cols(ba, al_ref[...], dt_ref[...], d * hh + h, 2 * hh + d * hh + h)
                    chains.append((d, n, rows))
                    for lst, val in ((qs, qj), (k, kj), (v, vj), (kk, kkj), (qk, qkj), (g, gd), (beta, bd)):
                        lst.append(val)
            z = _gdn_prep(qs, k, v, g, beta, kk, qk, [d == 1 for d, _, _ in chains])
            for x, (d, n, rows) in enumerate(chains):
                u_o[d, rows, :] = z["u"][x]
                w_o[d, rows, :] = z["w"][x].astype(bf16)
                qd_o[d, rows, :] = z["qd"][x].astype(bf16)
                kd_o[d, rows, :] = z["kd"][x].astype(bf16)
                p_o[d, 0, n] = z["p"][x].astype(bf16)
                t_o[d, 0, n] = z["tm"][x]
                dec_o[d, 0, n] = z["dec"][x]
                cd_o[d, 0, n] = jnp.broadcast_to(z["cd"][x], (8, LANES))
            return carry

        lax.fori_loop(0, nc // per, prep, 0)
        o_ref[...] = jnp.zeros_like(o_ref)

        def step(i, sts):
            ns = [i, nc - 1 - i]
            rows = [pl.ds(pl.multiple_of(n * c, c), c) for n in ns]
            vn = [u_o[d, rows[d], :] - _mm(w_o[d, rows[d], :], sts[d]) for d in range(2)]
            o = [_mm(qd_o[d, rows[d], :], sts[d]) + _mm(p_o[d, 0, ns[d]], vn[d]) for d in range(2)]
            new = [sts[d] * cd_o[d, 0, ns[d]][0:1, :] + _mm_tn(kd_o[d, rows[d], :], vn[d]) for d in range(2)]
            for d in range(2):
                o_ref[rows[d], :] = o_ref[rows[d], :] + o[d]
            return tuple(new)

        z0 = jnp.zeros((128, 128), f32)
        lax.fori_loop(0, nc, step, (z0, z0))

    blk = lambda off: pl.BlockSpec((s, 128), functools.partial(lambda b, h, off: (b, off + h), off=off))
    row = pl.BlockSpec((1, LANES), lambda b, h: (0, 0))
    tok2, mat2, cd2, shp = _gdn_specs(cfg)
    return pl.pallas_call(
        body, name="gdn_fwd", grid=(cfg.BL, hh),
        in_specs=[blk(0), blk(hh), blk(2 * hh), pl.BlockSpec((s, LANES), lambda b, h: (b, ba_blk)), row, row],
        out_specs=[pl.BlockSpec((s, 128), lambda b, h: (b, h)), tok2, tok2, tok2, tok2, mat2, mat2, mat2, cd2],
        out_shape=[jax.ShapeDtypeStruct((cfg.T, cfg.DN), f32), shp["tok32"], shp["tok16"], shp["tok16"], shp["tok16"],
                   shp["mat16"], shp["mat32"], shp["mat32"], shp["cd"]],
        compiler_params=_params(("arbitrary", "arbitrary")),
    )(qkv, qkv, qkv, proj, alog_row, dt_row)


def _lru_dir_fwd(xc, wa, wx, ba, bx, lam, rev):
    ra = _sigmoid(_mm(xc, wa) + ba)
    ig = _sigmoid(_mm(xc, wx) + bx)
    sp = _softplus(-lam)
    la = -LRU_C * ra * sp
    a = jnp.exp(la)
    m = jnp.sqrt(_nexpm1(2.0 * la))
    gx = ig * xc
    h = _lin_scan(a, m * gx, rev)
    return dict(ra=ra, ig=ig, sp=sp, a=a, m=m, gx=gx, h=h)


def _lru_specs(cfg, outer_b):
    s = cfg.S
    lx_blk = cfg.LXO // LANES
    if outer_b:
        ix = lambda f: (lambda b, ct: f(b, ct))
    else:
        ix = lambda f: (lambda ct, b: f(b, ct))
    return dict(
        lx=pl.BlockSpec((s, LANES), ix(lambda b, ct: (b, lx_blk + ct))),
        tok=pl.BlockSpec((s, LANES), ix(lambda b, ct: (b, ct))),
        cw=pl.BlockSpec((4, LANES), ix(lambda b, ct: (0, ct))),
        row=pl.BlockSpec((1, LANES), ix(lambda b, ct: (0, ct))),
        w=pl.BlockSpec((2, 1, LANES, LANES), ix(lambda b, ct: (0, ct, 0, 0))),
        two=pl.BlockSpec((2, LANES), ix(lambda b, ct: (0, ct))),
    )


def _k_lru_fwd(cfg, proj, conv_w, conv_b, wa, wx, ba, bx, lam):
    def body(lx_ref, cw_ref, cb_ref, wa_ref, wx_ref, ba_ref, bx_ref, lam_ref, o_ref):
        xc = _conv_fwd(lx_ref[...], cw_ref, 2) + cb_ref[...]
        acc = None
        for d in range(2):
            z = _lru_dir_fwd(xc, wa_ref[d, 0], wx_ref[d, 0], ba_ref[d:d + 1, :], bx_ref[d:d + 1, :], lam_ref[d:d + 1, :], d == 1)
            acc = z["h"] if acc is None else acc + z["h"]
        o_ref[...] = acc

    sp = _lru_specs(cfg, True)
    return pl.pallas_call(
        body, name="lru_fwd", grid=(cfg.BL, cfg.LW // LANES),
        in_specs=[sp["lx"], sp["cw"], sp["row"], sp["w"], sp["w"], sp["two"], sp["two"], sp["two"]],
        out_specs=sp["tok"], out_shape=jax.ShapeDtypeStruct((cfg.T, cfg.LW), f32),
        compiler_params=_params(("arbitrary", "arbitrary")),
    )(proj, conv_w, conv_b, wa, wx, ba, bx, lam)


def _mix_parts(cfg, o, z, lg, hs, dng, lrg):
    heads = []
    for h in range(cfg.H):
        sl = slice(h * 128, (h + 1) * 128)
        y, xh, inv = _rms_fwd(o[:, sl], dng)
        sz, dsz = _silu(z[:, sl])
        heads.append((y, xh, inv, sz, dsz))
    gl, dgl = _gelu(lg)
    y2, xh2, inv2 = _rms_fwd(gl * hs, lrg)
    return heads, (y2, xh2, inv2, gl, dgl)


def _k_mix(cfg, o, proj, hs, r, dng, lrg, wout):
    dn = cfg.DN

    def body(o_ref, z_ref, lg_ref, hs_ref, r_ref, dng_ref, lrg_ref, w_ref, mix_ref, out_ref):
        heads, lru = _mix_parts(cfg, o_ref[...], z_ref[...], lg_ref[...], hs_ref[...], dng_ref[...], lrg_ref[...])
        for h, (y, _, _, sz, _) in enumerate(heads):
            mix_ref[:, h * 128:(h + 1) * 128] = (y * sz).astype(bf16)
        mix_ref[:, dn:] = lru[0].astype(bf16)
        out_ref[...] = r_ref[...] + jnp.dot(mix_ref[...], w_ref[...], preferred_element_type=f32)

    return _row_call("mix_out", body, cfg.T, cfg.TM,
                     [(o, dn, 0), (proj, dn, cfg.ZO // dn), (proj, cfg.LW, cfg.LGO // cfg.LW), (hs, cfg.LW, 0), (r, cfg.D, 0)],
                     [dng, lrg, wout], [(cfg.D, bf16), (cfg.D, f32)], [])


def _k_ffn_a(cfg, r, g2, wg, wu):
    def body(r_ref, g_ref, wg_ref, wu_ref, h_ref, gp_ref, up_ref):
        y, _, _ = _rms_fwd(r_ref[...], g_ref[...])
        hb = y.astype(bf16)
        h_ref[...] = hb
        gp_ref[...] = _mm_nt(hb, wg_ref[...])
        up_ref[...] = _mm_nt(hb, wu_ref[...])

    return _row_call("ffn_in", body, cfg.T, cfg.TMF, [(r, cfg.D, 0)], [g2, wg, wu],
                     [(cfg.D, bf16), (cfg.FF, f32), (cfg.FF, f32)], [])


def _k_ffn_b(cfg, gp, up, conv_w, conv_b):
    s, ft = cfg.S, cfg.FT

    def body(gp_ref, up_ref, w_ref, b_ref, o_ref):
        gate = _conv_fwd(gp_ref[...], w_ref, 1) + b_ref[...]
        gl, _ = _gelu(gate)
        o_ref[...] = (gl * up_ref[...]).astype(bf16)

    tok = pl.BlockSpec((s, ft), lambda b, j: (b, j))
    return pl.pallas_call(
        body, name="ffn_act", grid=(cfg.BL, cfg.FF // ft),
        in_specs=[tok, tok, pl.BlockSpec((3, ft), lambda b, j: (0, j)), pl.BlockSpec((1, ft), lambda b, j: (0, j))],
        out_specs=tok, out_shape=jax.ShapeDtypeStruct((cfg.T, cfg.FF), bf16),
        compiler_params=_params(("arbitrary", "arbitrary")),
    )(gp, up, conv_w, conv_b)


def _k_ffn_c(cfg, act, r, wd):
    def body(a_ref, r_ref, w_ref, o_ref):
        o_ref[...] = r_ref[...] + jnp.dot(a_ref[...], w_ref[...], preferred_element_type=f32)

    return _row_call("ffn_out", body, cfg.T, cfg.TM, [(act, cfg.FF, 0), (r, cfg.D, 0)], [wd], [(cfg.D, f32)], [])[0]


def _k_ple(cfg, r, p, gp, wpg, bg, wpp):
    def body(r_ref, p_ref, g_ref, wg_ref, bg_ref, wp_ref, pn_ref, o_ref):
        x = r_ref[...]
        y, _, _ = _rms_fwd(x, g_ref[...])
        pn = y.astype(bf16)
        pn_ref[...] = pn
        pg = _sigmoid(jnp.dot(pn, wg_ref[...], preferred_element_type=f32) + bg_ref[...])
        o_ref[...] = x + pg * _mm_nt(p_ref[...], wp_ref[...])

    return _row_call("ple", body, cfg.T, cfg.TM, [(r, cfg.D, 0), (p, cfg.PD, 0)], [gp, wpg, bg, wpp],
                     [(cfg.D, bf16), (cfg.D, f32)], [])


def _k_loss(cfg, r, tgt, gf):
    d = cfg.D

    def body(r_ref, t_ref, g_ref, dr_ref, loss_ref, dg_ref):
        @pl.when(pl.program_id(0) == 0)
        def _():
            loss_ref[...] = jnp.zeros_like(loss_ref)
            dg_ref[...] = jnp.zeros_like(dg_ref)

        g = g_ref[...]
        y, xh, inv = _rms_fwd(r_ref[...], g)
        err = y - t_ref[...]
        loss_ref[...] = loss_ref[...] + (0.5 / d) * jnp.sum(err * err)
        dx, dg = _rms_bwd(err * (1.0 / d), xh, inv, g)
        dr_ref[...] = dx
        dg_ref[...] = dg_ref[...] + dg

    return _row_call("loss_head", body, cfg.T, cfg.TM, [(r, d, 0), (tgt, d, 0)], [gf], [(d, f32)],
                     [((1, LANES), f32), ((1, d), f32)])


def _zero_at_first(cond, *refs):
    @pl.when(cond)
    def _():
        for r in refs:
            r[...] = jnp.zeros_like(r)


def _b_ple(cfg, dr3, r2, p, gp, wpg, bg, wpp):
    d = cfg.D

    def body(dr_ref, r_ref, p_ref, g_ref, wg_ref, bg_ref, wp_ref, dr2_ref, dlog_ref, dpp_ref, dgp_ref, dbg_ref):
        _zero_at_first(pl.program_id(0) == 0, dgp_ref, dbg_ref)
        g = g_ref[...]
        dr = dr_ref[...]
        y, xh, inv = _rms_fwd(r_ref[...], g)
        pg = _sigmoid(jnp.dot(y.astype(bf16), wg_ref[...], preferred_element_type=f32) + bg_ref[...])
        pp = _mm_nt(p_ref[...], wp_ref[...])
        dpp_ref[...] = (dr * pg).astype(bf16)
        dlog = dr * pp * pg * (1.0 - pg)
        dlog_ref[...] = dlog.astype(bf16)
        dbg_ref[...] = dbg_ref[...] + jnp.sum(dlog, axis=0, keepdims=True)
        dx, dg = _rms_bwd(_mm_nt(dlog, wg_ref[...]), xh, inv, g)
        dgp_ref[...] = dgp_ref[...] + dg
        dr2_ref[...] = dr + dx

    return _row_call("ple_bwd", body, cfg.T, cfg.TM, [(dr3, d, 0), (r2, d, 0), (p, cfg.PD, 0)], [gp, wpg, bg, wpp],
                     [(d, f32), (d, bf16), (d, bf16)], [((1, d), f32), ((1, d), f32)])


def _b_ffn_bc(cfg, dr2, wd, gp, up, conv_w, conv_b):
    s, ft, d = cfg.S, cfg.FT, cfg.D

    def body(dr_ref, wd_ref, gp_ref, up_ref, w_ref, b_ref, dgp_ref, dup_ref, dwd_ref, dcw_ref, dcb_ref):
        _zero_at_first(pl.program_id(1) == 0, dwd_ref, dcw_ref, dcb_ref)
        drb = dr_ref[...].astype(bf16)
        dact = _mm_nt(drb, wd_ref[...])
        gpre = gp_ref[...]
        up = up_ref[...]
        gl, dgl = _gelu(_conv_fwd(gpre, w_ref, 1) + b_ref[...])
        dup_ref[...] = (dact * gl).astype(bf16)
        dgate = dact * up * dgl
        dcb_ref[...] = dcb_ref[...] + jnp.sum(dgate, axis=0, keepdims=True)
        dx, dws = _conv_bwd(dgate, gpre, w_ref, 1)
        for j, dw in enumerate(dws):
            dcw_ref[j:j + 1, :] = dcw_ref[j:j + 1, :] + dw
        dgp_ref[...] = dx.astype(bf16)
        dwd_ref[...] = dwd_ref[...] + _mm_tn((gl * up).astype(bf16), drb)

    tok = pl.BlockSpec((s, ft), lambda j, b: (b, j))
    return pl.pallas_call(
        body, name="ffn_act_bwd", grid=(cfg.FF // ft, cfg.BL),
        in_specs=[pl.BlockSpec((s, d), lambda j, b: (b, 0)), pl.BlockSpec((ft, d), lambda j, b: (j, 0)), tok, tok,
                  pl.BlockSpec((3, ft), lambda j, b: (0, j)), pl.BlockSpec((1, ft), lambda j, b: (0, j))],
        out_specs=[tok, tok, pl.BlockSpec((ft, d), lambda j, b: (j, 0)), pl.BlockSpec((3, ft), lambda j, b: (0, j)),
                   pl.BlockSpec((1, ft), lambda j, b: (0, j))],
        out_shape=[jax.ShapeDtypeStruct((cfg.T, cfg.FF), bf16), jax.ShapeDtypeStruct((cfg.T, cfg.FF), bf16),
                   jax.ShapeDtypeStruct((cfg.FF, d), f32), jax.ShapeDtypeStruct((3, cfg.FF), f32),
                   jax.ShapeDtypeStruct((1, cfg.FF), f32)],
        compiler_params=_params(("arbitrary", "arbitrary")),
    )(dr2, wd, gp, up, conv_w, conv_b)


def _b_ffn_a(cfg, dgp, dup, dr2, r_mid, g2, wg, wu):
    d = cfg.D

    def body(dgp_ref, dup_ref, dr_ref, r_ref, g_ref, wg_ref, wu_ref, o_ref, dg_ref):
        _zero_at_first(pl.program_id(0) == 0, dg_ref)
        g = g_ref[...]
        _, xh, inv = _rms_fwd(r_ref[...], g)
        dh = _mm(dgp_ref[...], wg_ref[...]) + _mm(dup_ref[...], wu_ref[...])
        dx, dg = _rms_bwd(dh, xh, inv, g)
        dg_ref[...] = dg_ref[...] + dg
        o_ref[...] = dr_ref[...] + dx

    return _row_call("ffn_in_bwd", body, cfg.T, cfg.TMF, [(dgp, cfg.FF, 0), (dup, cfg.FF, 0), (dr2, d, 0), (r_mid, d, 0)],
                     [g2, wg, wu], [(d, f32)], [((1, d), f32)])


def _mm_tn_call(cfg, name, x, dy, tn):
    t, k = x.shape
    n = dy.shape[1]
    tm = cfg.TM

    def body(x_ref, dy_ref, o_ref):
        _zero_at_first(pl.program_id(1) == 0, o_ref)
        o_ref[...] = o_ref[...] + _mm_tn(x_ref[...], dy_ref[...])

    return pl.pallas_call(
        body, name=name, grid=(n // tn, t // tm),
        in_specs=[pl.BlockSpec((tm, k), lambda j, i: (i, 0)), pl.BlockSpec((tm, tn), lambda j, i: (i, j))],
        out_specs=pl.BlockSpec((k, tn), lambda j, i: (0, j)),
        out_shape=jax.ShapeDtypeStruct((k, n), f32),
        compiler_params=_params(("arbitrary", "arbitrary")),
    )(x, dy)


def _b_mix(cfg, dr, o, proj, hs, dng, lrg, wout):
    dn, lw, d = cfg.DN, cfg.LW, cfg.D

    def body(dr_ref, o_ref, z_ref, lg_ref, hs_ref, dng_ref, lrg_ref, w_ref, do_ref, dz_ref, dlg_ref, dhs_ref, ddn_ref, dlr_ref):
        _zero_at_first(pl.program_id(0) == 0, ddn_ref, dlr_ref)
        dng, lrg = dng_ref[...], lrg_ref[...]
        hs = hs_ref[...]
        o, z = o_ref[...], z_ref[...]
        heads, lru = _mix_parts(cfg, o, z, lg_ref[...], hs, dng, lrg)
        drb = dr_ref[...].astype(bf16)
        dmix_dn = _mm_nt(drb, w_ref[0:dn, :])
        dmix_lr = _mm_nt(drb, w_ref[dn:, :])
        dgn = jnp.zeros_like(dng)
        for h, (y, xh, inv, sz, dsz) in enumerate(heads):
            sl = slice(h * 128, (h + 1) * 128)
            dm = dmix_dn[:, sl]
            dz_ref[:, sl] = dm * y * dsz
            dx, dg = _rms_bwd(dm * sz, xh, inv, dng)
            do_ref[:, sl] = dx
            dgn = dgn + dg
        ddn_ref[...] = ddn_ref[...] + dgn
        _, xh2, inv2, gl, dgl = lru
        dx2, dg2 = _rms_bwd(dmix_lr, xh2, inv2, lrg)
        dlr_ref[...] = dlr_ref[...] + dg2
        dlg_ref[...] = dx2 * hs * dgl
        dhs_ref[...] = dx2 * gl

    return _row_call("mix_bwd", body, cfg.T, cfg.TM,
                     [(dr, d, 0), (o, dn, 0), (proj, dn, cfg.ZO // dn), (proj, lw, cfg.LGO // lw), (hs, lw, 0)],
                     [dng, lrg, wout], [(dn, f32), (dn, f32), (lw, f32), (lw, f32)], [((1, 128), f32), ((1, lw), f32)])


def _b_lru(cfg, proj, dhs, conv_w, conv_b, wa, wx, ba, bx, lam):
    def body(lx_ref, dh_ref, cw_ref, cb_ref, wa_ref, wx_ref, ba_ref, bx_ref, lam_ref,
             dlx_ref, dcw_ref, dcb_ref, dwa_ref, dwx_ref, dba_ref, dbx_ref, dlam_ref):
        _zero_at_first(pl.program_id(1) == 0, dcw_ref, dcb_ref, dwa_ref, dwx_ref, dba_ref, dbx_ref, dlam_ref)
        lx = lx_ref[...]
        dh = dh_ref[...]
        xc = _conv_fwd(lx, cw_ref, 2) + cb_ref[...]
        dxc = jnp.zeros_like(xc)
        for d in range(2):
            rev = d == 1
            lam = lam_ref[d:d + 1, :]
            z = _lru_dir_fwd(xc, wa_ref[d, 0], wx_ref[d, 0], ba_ref[d:d + 1, :], bx_ref[d:d + 1, :], lam, rev)
            a, m, ra, ig, sp = z["a"], z["m"], z["ra"], z["ig"], z["sp"]
            a_next = _shift(a, -1 if rev else 1, 0.0)
            lmb = _lin_scan(a_next, dh, not rev)
            h_prev = _shift(z["h"], 1 if rev else -1, 0.0)
            da = lmb * h_prev
            dm = lmb * z["gx"]
            dgx = lmb * m
            dla = da * a - dm * (a * a) / jnp.maximum(m, 1e-30)
            dra = dla * (-LRU_C) * sp
            dsp = jnp.sum(dla * (-LRU_C) * ra, axis=0, keepdims=True)
            dlam_ref[d:d + 1, :] = dlam_ref[d:d + 1, :] - dsp * _sigmoid(-lam)
            dpa = dra * ra * (1.0 - ra)
            dpx = dgx * xc * ig * (1.0 - ig)
            dba_ref[d:d + 1, :] = dba_ref[d:d + 1, :] + jnp.sum(dpa, axis=0, keepdims=True)
            dbx_ref[d:d + 1, :] = dbx_ref[d:d + 1, :] + jnp.sum(dpx, axis=0, keepdims=True)
            dwa_ref[d, 0] = dwa_ref[d, 0] + _mm_tn(xc, dpa)
            dwx_ref[d, 0] = dwx_ref[d, 0] + _mm_tn(xc, dpx)
            dxc = dxc + dgx * ig + _mm_nt(dpa, wa_ref[d, 0]) + _mm_nt(dpx, wx_ref[d, 0])
        dcb_ref[...] = dcb_ref[...] + jnp.sum(dxc, axis=0, keepdims=True)
        dx, dws = _conv_bwd(dxc, lx, cw_ref, 2)
        for j, dw in enumerate(dws):
            dcw_ref[j:j + 1, :] = dcw_ref[j:j + 1, :] + dw
        dlx_ref[...] = dx

    sp = _lru_specs(cfg, False)
    nct = cfg.LW // LANES
    return pl.pallas_call(
        body, name="lru_bwd", grid=(nct, cfg.BL),
        in_specs=[sp["lx"], sp["tok"], sp["cw"], sp["row"], sp["w"], sp["w"], sp["two"], sp["two"], sp["two"]],
        out_specs=[sp["tok"], sp["cw"], sp["row"], sp["w"], sp["w"], sp["two"], sp["two"], sp["two"]],
        out_shape=[jax.ShapeDtypeStruct((cfg.T, cfg.LW), f32), jax.ShapeDtypeStruct((4, cfg.LW), f32),
                   jax.ShapeDtypeStruct((1, cfg.LW), f32), jax.ShapeDtypeStruct((2, nct, LANES, LANES), f32),
                   jax.ShapeDtypeStruct((2, nct, LANES, LANES), f32), jax.ShapeDtypeStruct((2, cfg.LW), f32),
                   jax.ShapeDtypeStruct((2, cfg.LW), f32), jax.ShapeDtypeStruct((2, cfg.LW), f32)],
        compiler_params=_params(("arbitrary", "arbitrary")),
    )(proj, dhs, conv_w, conv_b, wa, wx, ba, bx, lam)


def _b_gdn(cfg, qkv, proj, do, alog_row, dt_row, saved):
    s, c, nc, hh = cfg.S, cfg.C, cfg.NC, cfg.H
    ba_blk = cfg.BAO // LANES
    scale = 128 ** -0.5
    per = 2 if nc % 2 == 0 else 1

    def body(q_ref, k_ref, v_ref, ba_ref, do_ref, al_ref, dt_ref, u_i, w_i, qd_i, kd_i, p_i, t_i, dec_i, cd_i,
             dqkv_ref, dba_ref, dal_ref, ddt_ref, vn_s, dvn_s, st_s, dst_s):
        h = pl.program_id(1)
        _zero_at_first((pl.program_id(0) == 0) & (h == 0), dal_ref, ddt_ref)
        _zero_at_first(h == 0, dba_ref)
        lane = lax.broadcasted_iota(jnp.int32, (c, LANES), 1)
        lane1 = lax.broadcasted_iota(jnp.int32, (1, LANES), 1)
        dirs = (0, 1)
        lane_b = [d * hh + h for d in dirs]
        lane_a = [2 * hh + d * hh + h for d in dirs]

        def fstep(i, sts):
            ns = [i, nc - 1 - i]
            rows = [pl.ds(pl.multiple_of(n * c, c), c) for n in ns]
            for d in dirs:
                st_s[d, ns[d]] = sts[d]
            vn = [u_i[d, rows[d], :] - _mm(w_i[d, rows[d], :], sts[d]) for d in dirs]
            for d in dirs:
                vn_s[d, rows[d], :] = vn[d]
            return tuple(sts[d] * cd_i[d, 0, ns[d]][0:1, :] + _mm_tn(kd_i[d, rows[d], :], vn[d]) for d in dirs)

        z0 = jnp.zeros((128, 128), f32)
        lax.fori_loop(0, nc, fstep, (z0, z0))

        def bseq(i, dst):
            ns = [nc - 1 - i, i]
            rows = [pl.ds(pl.multiple_of(n * c, c), c) for n in ns]
            for d in dirs:
                dst_s[d, ns[d]] = dst[d]
            dob = [do_ref[rows[d], :] for d in dirs]
            dvn = [_mm_tn(p_i[d, 0, ns[d]], dob[d]) + _mm(kd_i[d, rows[d], :], dst[d]) for d in dirs]
            for d in dirs:
                dvn_s[d, rows[d], :] = dvn[d]
            return tuple(_mm_tn(qd_i[d, rows[d], :], dob[d]) + cd_i[d, 0, ns[d]][0:1, 0:1] * dst[d]
                         - _mm_tn(w_i[d, rows[d], :], dvn[d]) for d in dirs)

        lax.fori_loop(0, nc, bseq, (z0, z0))

        def bpar(i, carry):
            ch = [(d, i * per + j) for j in range(per) for d in dirs]
            rows = [pl.ds(pl.multiple_of(n * c, c), c) for _, n in ch]
            masks = [_tri_masks(c, d == 1) for d, _ in ch]
            ld = lambda ref: [ref[d, r, :] for (d, _), r in zip(ch, rows)]
            ldm = lambda ref: [ref[d, 0, n] for d, n in ch]
            q, k, v, dob = ([ref[r, :] for r in rows] for ref in (q_ref, k_ref, v_ref, do_ref))
            gates = [_gate_cols(ba_ref[r, :], al_ref[...], dt_ref[...], lane_b[d], lane_a[d]) for (d, _), r in zip(ch, rows)]
            beta, g, aexp, xa = ([gt[x] for gt in gates] for x in range(4))
            st, dst = ([ref[d, n] for d, n in ch] for ref in (st_s, dst_s))
            vn, dvn, u, w = ld(vn_s), ld(dvn_s), ld(u_i), ld(w_i)
            p = [x.astype(f32) for x in ldm(p_i)]
            tm, dec = ldm(t_i), ldm(dec_i)
            cd = [cd_i[d, 0, n][0:1, 0:1] for d, n in ch]
            dp = _map(lambda m, a, b: jnp.where(m[0], _mm_nt(a, b), 0.0), masks, dob, vn)
            dqd = _map(_mm_nt, dob, st)
            dkd = _map(_mm_nt, vn, dst)
            dw = _map(lambda a, b: -_mm_nt(a, b), dvn, st)
            dcd = _map(lambda a, b: jnp.sum(jnp.sum(a * b, axis=1, keepdims=True), axis=0, keepdims=True), st, dst)
            gcum = _map(lambda m, x: _cum_mm(m[0], jnp.broadcast_to(x, (c, c)))[:, 0:1], masks, g)
            glast = [jnp.sum(x, axis=0, keepdims=True) for x in g]
            e = [jnp.exp(x) for x in gcum]
            el = _map(lambda a, b: jnp.exp(a - b), glast, gcum)
            qs = [x * scale for x in q]
            kb = _map(lambda a, b: a * b, k, beta)
            a = _map(lambda m, b, ki, dc: jnp.where(m[1], b * _mm_nt(ki, ki) * dc, 0.0), masks, beta, k, dec)
            dvb = _map(_hmm3_tn, tm, dvn)
            dkbe = _map(_hmm3_tn, tm, dw)
            da = _map(lambda m, x, ui, y, wi: -jnp.where(m[1], _mm_nt(x, ui) + _mm_nt(y, wi), 0.0), masks, dvb, u, dkbe, w)
            g1 = _map(lambda x, y: x * y, da, dec)
            g2 = _map(lambda x, y: x * y, dp, dec)
            dkb = _map(lambda x, ki, y, ei: _mm(x, ki) + y * ei, g1, k, dkbe, e)
            dk = _map(lambda x, kbi, y, qi, z, b, t, l: _mm_tn(x, kbi) + _mm_tn(y, qi) + z * b + t * l,
                      g1, kb, g2, qs, dkb, beta, dkd, el)
            dqs = _map(lambda y, ki, x, ei: _mm(y, ki) + x * ei, g2, k, dqd, e)
            ddd = _map(lambda x, ai, y, pi: x * ai + y * pi, da, a, dp, p)
            ones = jnp.ones((c, LANES), f32)
            dgcum = _map(lambda x: jnp.sum(x, axis=1, keepdims=True) - _hmm3_tn(x, ones)[:, 0:1], ddd)
            for x, (d, n) in enumerate(ch):
                dbeta = jnp.sum(dvb[x] * v[x], axis=1, keepdims=True) + jnp.sum(dkb[x] * k[x], axis=1, keepdims=True)
                de = jnp.sum(dkbe[x] * kb[x], axis=1, keepdims=True) + jnp.sum(dqd[x] * qs[x], axis=1, keepdims=True)
                del_ = jnp.sum(dkd[x] * k[x], axis=1, keepdims=True)
                dgc = dgcum[x] + de * e[x] - del_ * el[x]
                dglast = jnp.sum(del_ * el[x], axis=0, keepdims=True) + dcd[x] * cd[x]
                dg = _cum_mm(masks[x][2], jnp.broadcast_to(dgc, (c, LANES)))[:, 0:1] + dglast
                r = rows[x]
                if d == 0:
                    dqkv_ref[0, r, :] = dqs[x] * scale
                    dqkv_ref[1, r, :] = dk[x]
                    dqkv_ref[2, r, :] = dvb[x] * beta[x]
                else:
                    dqkv_ref[0, r, :] = dqkv_ref[0, r, :] + dqs[x] * scale
                    dqkv_ref[1, r, :] = dqkv_ref[1, r, :] + dk[x]
                    dqkv_ref[2, r, :] = dqkv_ref[2, r, :] + dvb[x] * beta[x]
                dlb = dbeta * beta[x] * (1.0 - beta[x])
                dalpha = -dg * aexp[x] * _sigmoid(xa[x])
                dba_ref[r, :] = dba_ref[r, :] + jnp.where(lane == lane_b[d], dlb, 0.0) + jnp.where(lane == lane_a[d], dalpha, 0.0)
                dal_ref[...] = dal_ref[...] + jnp.where(lane1 == lane_a[d], jnp.sum(dg * g[x], axis=0, keepdims=True), 0.0)
                ddt_ref[...] = ddt_ref[...] + jnp.where(lane1 == lane_a[d], jnp.sum(dalpha, axis=0, keepdims=True), 0.0)
            return carry

        lax.fori_loop(0, nc // per, bpar, 0)

    blk = lambda off: pl.BlockSpec((s, 128), functools.partial(lambda b, h, off: (b, off + h), off=off))
    row = pl.BlockSpec((1, LANES), lambda b, h: (0, 0))
    tok2, mat2, cd2, _ = _gdn_specs(cfg)
    return pl.pallas_call(
        body, name="gdn_bwd", grid=(cfg.BL, hh),
        in_specs=[blk(0), blk(hh), blk(2 * hh), pl.BlockSpec((s, LANES), lambda b, h: (b, ba_blk)),
                  pl.BlockSpec((s, 128), lambda b, h: (b, h)), row, row, tok2, tok2, tok2, tok2, mat2, mat2, mat2, cd2],
        out_specs=[pl.BlockSpec((3, s, 128), lambda b, h: (0, b, h)), pl.BlockSpec((s, LANES), lambda b, h: (b, 0)), row, row],
        out_shape=[jax.ShapeDtypeStruct((3, cfg.T, cfg.DN), f32), jax.ShapeDtypeStruct((cfg.T, LANES), f32),
                   jax.ShapeDtypeStruct((1, LANES), f32), jax.ShapeDtypeStruct((1, LANES), f32)],
        scratch_shapes=[pltpu.VMEM((2, s, 128), f32)] * 2 + [pltpu.VMEM((2, nc, 128, 128), f32)] * 2,
        compiler_params=_params(("arbitrary", "arbitrary")),
    )(qkv, qkv, qkv, proj, do, alog_row, dt_row, *saved)


def _b_prep(cfg, proj, dqkv, conv_w):
    dn, s = cfg.DN, cfg.S

    def body(x_ref, dy_ref, w_ref, dx_ref, dw_ref):
        _zero_at_first(pl.program_id(1) == 0, dw_ref)
        sec = pl.program_id(0)
        x = x_ref[...]
        c = _conv_fwd(x, w_ref, 2)
        y, dsilu = _silu(c)
        dy = dy_ref[0]
        parts = []
        for h in range(cfg.H):
            sl = slice(h * 128, (h + 1) * 128)
            yh, dyh = y[:, sl], dy[:, sl]
            inv = lax.rsqrt(jnp.sum(yh * yh, axis=1, keepdims=True) + EPS)
            dn_h = inv * dyh - yh * (inv * inv * inv) * jnp.sum(dyh * yh, axis=1, keepdims=True)
            parts.append(jnp.where(sec < 2, dn_h, dyh))
        ds = jnp.concatenate(parts, axis=1) if len(parts) > 1 else parts[0]
        dx, dws = _conv_bwd(ds * dsilu, x, w_ref, 2)
        for j, dw in enumerate(dws):
            dw_ref[j:j + 1, :] = dw_ref[j:j + 1, :] + dw
        dx_ref[...] = dx

    return pl.pallas_call(
        body, name="dn_prep_bwd", grid=(3, cfg.BL),
        in_specs=[pl.BlockSpec((s, dn), lambda j, b: (b, j)), pl.BlockSpec((1, s, dn), lambda j, b: (j, b, 0)),
                  pl.BlockSpec((4, dn), lambda j, b: (0, j))],
        out_specs=[pl.BlockSpec((s, dn), lambda j, b: (b, j)), pl.BlockSpec((4, dn), lambda j, b: (0, j))],
        out_shape=[jax.ShapeDtypeStruct((cfg.T, 3 * dn), f32), jax.ShapeDtypeStruct((4, 3 * dn), f32)],
        compiler_params=_params(("arbitrary", "arbitrary")),
    )(proj, dqkv, conv_w)


def _b_in(cfg, dqkv_pre, dz, dlx, dlg, dba, dr_mid, r_in, g1, wcat):
    d, dn, lw = cfg.D, cfg.DN, cfg.LW

    def body(dq_ref, dz_ref, dlx_ref, dlg_ref, dba_ref, dr_ref, r_ref, g_ref, w_ref, o_ref, dp_ref, dg_ref):
        _zero_at_first(pl.program_id(0) == 0, dg_ref)
        dp_ref[:, 0:cfg.ZO] = dq_ref[...].astype(bf16)
        dp_ref[:, cfg.ZO:cfg.LXO] = dz_ref[...].astype(bf16)
        dp_ref[:, cfg.LXO:cfg.LGO] = dlx_ref[...].astype(bf16)
        dp_ref[:, cfg.LGO:cfg.BAO] = dlg_ref[...].astype(bf16)
        dp_ref[:, cfg.BAO:] = dba_ref[...].astype(bf16)
        g = g_ref[...]
        _, xh, inv = _rms_fwd(r_ref[...], g)
        dx, dg = _rms_bwd(_mm(dp_ref[...], w_ref[...]), xh, inv, g)
        dg_ref[...] = dg_ref[...] + dg
        o_ref[...] = dr_ref[...] + dx

    return _row_call("in_proj_bwd", body, cfg.T, cfg.TM,
                     [(dqkv_pre, 3 * dn, 0), (dz, dn, 0), (dlx, lw, 0), (dlg, lw, 0), (dba, LANES, 0), (dr_mid, d, 0), (r_in, d, 0)],
                     [g1, wcat], [(d, f32), (cfg.PC, bf16)], [((1, d), f32)])


def _adam_call(name, w, g, m, v, tr):
    rows, cols = w.shape
    bc1 = 1.0 - ADAM_B1 ** ADAM_STEP
    bc2 = 1.0 - ADAM_B2 ** ADAM_STEP

    def body(w_ref, g_ref, m_ref, v_ref, d_ref, nm_ref, nv_ref):
        g = g_ref[...]
        m = ADAM_B1 * m_ref[...] + (1.0 - ADAM_B1) * g
        v = ADAM_B2 * v_ref[...] + (1.0 - ADAM_B2) * (g * g)
        nm_ref[...] = m
        nv_ref[...] = v
        d_ref[...] = -ADAM_LR * ((m / bc1) / (jnp.sqrt(v / bc2) + ADAM_EPS) + ADAM_WD * w_ref[...])

    spec = pl.BlockSpec((tr, cols), lambda i: (i, 0))
    return pl.pallas_call(
        body, name=name, grid=(rows // tr,), in_specs=[spec] * 4, out_specs=[spec] * 3,
        out_shape=[jax.ShapeDtypeStruct((rows, cols), f32)] * 3, compiler_params=_params(("arbitrary",)),
    )(w, g, m, v)


def _sum8_call(name, x, tr):
    _, rows, cols = x.shape

    def body(x_ref, o_ref):
        acc = x_ref[0].astype(f32)
        for j in range(1, N_DEV):
            acc = acc + x_ref[j].astype(f32)
        o_ref[...] = acc

    return pl.pallas_call(
        body, name=name, grid=(rows // tr,), in_specs=[pl.BlockSpec((N_DEV, tr, cols), lambda i: (0, i, 0))],
        out_specs=pl.BlockSpec((tr, cols), lambda i: (i, 0)), out_shape=jax.ShapeDtypeStruct((rows, cols), f32),
        compiler_params=_params(("arbitrary",)),
    )(x)


def _all_gather(name, shards):
    na = len(shards)

    def body(*refs):
        xs, outs = refs[:na], refs[na:2 * na]
        send_sems, recv_sems, local_sems = refs[2 * na:]
        x, y, c = lax.axis_index("x"), lax.axis_index("y"), lax.axis_index("c")
        me, sibling = (x, y, c), (x, y, 1 - c)
        chips = [(1 - x, y), (x, 1 - y), (1 - x, 1 - y)]

        def copy(a, k, block, to, src=None):
            px, py, pc = block
            slot = outs[a].at[4 * px + 2 * py + pc]
            return pltpu.make_async_remote_copy(
                src_ref=slot if src is None else src, dst_ref=slot,
                send_sem=send_sems.at[a, k], recv_sem=recv_sems.at[a, k], device_id=to, device_id_type=MESH)

        mine = [pltpu.make_async_copy(xs[a], outs[a].at[4 * x + 2 * y + c], local_sems.at[a]) for a in range(na)]
        for cp in mine:
            cp.start()
        first = []
        for a in range(na):
            first.append(copy(a, 0, me, sibling, src=xs[a]))
            first += [copy(a, 1 + j, me, (*chip, c), src=xs[a]) for j, chip in enumerate(chips)]
        for cp in first:
            cp.start()
        passed = []
        for j, chip in enumerate(chips):
            for a in range(na):
                copy(a, 1 + j, (*chip, c), me).wait_recv()
                cp = copy(a, 4 + j, (*chip, c), sibling)
                cp.start()
                passed.append(cp)
        for a in range(na):
            copy(a, 0, sibling, me).wait_recv()
            for j, chip in enumerate(chips):
                copy(a, 4 + j, (*chip, 1 - c), me).wait_recv()
        for cp in first + passed:
            cp.wait_send()
        for cp in mine:
            cp.wait()

    hbm = pl.BlockSpec(memory_space=pltpu.HBM)
    return pl.pallas_call(
        body, name=name, out_shape=[jax.ShapeDtypeStruct((N_DEV,) + s.shape, s.dtype) for s in shards],
        in_specs=[hbm] * na, out_specs=[hbm] * na,
        scratch_shapes=[pltpu.SemaphoreType.DMA((na, 7)), pltpu.SemaphoreType.DMA((na, 7)), pltpu.SemaphoreType.DMA((na,))],
    )(*shards)


def _all_to_all(name, blocks):
    na = len(blocks)

    def body(*refs):
        xs, outs = refs[:na], refs[na:2 * na]
        send_sems, recv_sems, local_sems = refs[2 * na:]
        x, y, c = lax.axis_index("x"), lax.axis_index("y"), lax.axis_index("c")
        me = 4 * x + 2 * y + c
        mine = [pltpu.make_async_copy(xs[a].at[me], outs[a].at[me], local_sems.at[a]) for a in range(na)]
        for cp in mine:
            cp.start()
        copies = []
        for k in range(1, N_DEV):
            px, py, pc = x ^ (k >> 2), y ^ ((k >> 1) & 1), c ^ (k & 1)
            for a in range(na):
                copies.append(pltpu.make_async_remote_copy(
                    src_ref=xs[a].at[4 * px + 2 * py + pc], dst_ref=outs[a].at[me],
                    send_sem=send_sems.at[a, k - 1], recv_sem=recv_sems.at[a, k - 1], device_id=(px, py, pc),
                    device_id_type=MESH))
        for cp in copies:
            cp.start()
        for cp in copies:
            cp.wait_recv()
        for cp in copies:
            cp.wait_send()
        for cp in mine:
            cp.wait()

    hbm = pl.BlockSpec(memory_space=pltpu.HBM)
    return pl.pallas_call(
        body, name=name, out_shape=[jax.ShapeDtypeStruct(b.shape, b.dtype) for b in blocks],
        in_specs=[hbm] * na, out_specs=[hbm] * na,
        scratch_shapes=[pltpu.SemaphoreType.DMA((na, 7)), pltpu.SemaphoreType.DMA((na, 7)), pltpu.SemaphoreType.DMA((na,))],
    )(*blocks)


def _sum8_layers(name, x):
    _, nl, rows, cols = x.shape

    def body(x_ref, o_ref):
        acc = x_ref[0, 0].astype(f32)
        for j in range(1, N_DEV):
            acc = acc + x_ref[j, 0].astype(f32)
        o_ref[0] = acc

    return pl.pallas_call(
        body, name=name, grid=(nl,), in_specs=[pl.BlockSpec((N_DEV, 1, rows, cols), lambda i: (0, i, 0, 0))],
        out_specs=pl.BlockSpec((1, rows, cols), lambda i: (i, 0, 0)), out_shape=jax.ShapeDtypeStruct((nl, rows, cols), f32),
        compiler_params=_params(("arbitrary",)),
    )(x)


def _layer_fwd(cfg, w, r, p):
    h1, proj = _k_in(cfg, r, w["norm1_g"], w["wcat_t"])
    qkv = _k_prep(cfg, proj, w["dn_conv_w"])
    o, *gdn_saved = _k_gdn_fwd(cfg, qkv, proj, w["alog_row"], w["dt_row"])
    hs = _k_lru_fwd(cfg, proj, w["lru_conv_w"], w["lru_conv_b"], w["wa"], w["wx"], w["lru_ba"], w["lru_bx"], w["lru_lambda"])
    mix, r_mid = _k_mix(cfg, o, proj, hs, r, w["dn_norm_g"], w["lru_norm_g"], w["w_out"])
    h2, gp, up = _k_ffn_a(cfg, r_mid, w["norm2_g"], w["ffn_wg_t"], w["ffn_wu_t"])
    act = _k_ffn_b(cfg, gp, up, w["ffn_conv_w"], w["ffn_conv_b"])
    r2 = _k_ffn_c(cfg, act, r_mid, w["ffn_wd"])
    pn, r3 = _k_ple(cfg, r2, p, w["ple_norm_g"], w["ple_wg"], w["ple_bg"], w["ple_wp_t"])
    saved = dict(r=r, h1=h1, proj=proj, qkv=qkv, o=o, gdn=gdn_saved, hs=hs, mix=mix, r_mid=r_mid, h2=h2, gp=gp, up=up,
                 r2=r2, pn=pn, p=p)
    return r3, saved


def _layer_bwd(cfg, w, sv, dr3):
    g = {}
    dt = min(512, cfg.D)
    dr2, dlog, dpp, g["ple_norm_g"], g["ple_bg"] = _b_ple(cfg, dr3, sv["r2"], sv["p"], w["ple_norm_g"], w["ple_wg"], w["ple_bg"], w["ple_wp_t"])
    g["ple_wg"] = _mm_tn_call(cfg, "d_ple_wg", sv["pn"], dlog, dt)
    g["ple_wp_t"] = _mm_tn_call(cfg, "d_ple_wp", dpp, sv["p"], cfg.PD)
    dgp, dup, g["ffn_wd"], g["ffn_conv_w"], g["ffn_conv_b"] = _b_ffn_bc(cfg, dr2, w["ffn_wd"], sv["gp"], sv["up"], w["ffn_conv_w"], w["ffn_conv_b"])
    dr_mid, g["norm2_g"] = _b_ffn_a(cfg, dgp, dup, dr2, sv["r_mid"], w["norm2_g"], w["ffn_wg_t"], w["ffn_wu_t"])
    g["ffn_wg_t"] = _mm_tn_call(cfg, "d_ffn_wg", dgp, sv["h2"], dt)
    g["ffn_wu_t"] = _mm_tn_call(cfg, "d_ffn_wu", dup, sv["h2"], dt)
    do, dz, dlg, dhs, g["dn_norm_g"], g["lru_norm_g"] = _b_mix(cfg, dr_mid, sv["o"], sv["proj"], sv["hs"], w["dn_norm_g"], w["lru_norm_g"], w["w_out"])
    g["w_out"] = _mm_tn_call(cfg, "d_w_out", sv["mix"], dr_mid, dt)
    dlx, g["lru_conv_w"], g["lru_conv_b"], g["wa"], g["wx"], g["lru_ba"], g["lru_bx"], g["lru_lambda"] = _b_lru(
        cfg, sv["proj"], dhs, w["lru_conv_w"], w["lru_conv_b"], w["wa"], w["wx"], w["lru_ba"], w["lru_bx"], w["lru_lambda"])
    dqkv, dba, g["alog_row"], g["dt_row"] = _b_gdn(cfg, sv["qkv"], sv["proj"], do, w["alog_row"], w["dt_row"], sv["gdn"])
    dqkv_pre, g["dn_conv_w"] = _b_prep(cfg, sv["proj"], dqkv, w["dn_conv_w"])
    dr, dproj, g["norm1_g"] = _b_in(cfg, dqkv_pre, dz, dlx, dlg, dba, dr_mid, sv["r"], w["norm1_g"], w["wcat_t"])
    g["wcat_t"] = _mm_tn_call(cfg, "d_w_in", dproj, sv["h1"], dt)
    return dr, g


BIG = ("w_in", "w_out", "ffn_wg", "ffn_wu", "ffn_wd", "ple_wg", "ple_wp")
BIG_T = {"w_in": True, "w_out": False, "ffn_wg": True, "ffn_wu": True, "ffn_wd": False, "ple_wg": False, "ple_wp": True}
BIG_OPERAND = {"w_in": "wcat_t", "w_out": "w_out", "ffn_wg": "ffn_wg_t", "ffn_wu": "ffn_wu_t", "ffn_wd": "ffn_wd",
               "ple_wg": "ple_wg", "ple_wp": "ple_wp_t"}
SMALL_SHARDED = ("dn_conv_w", "lru_conv_w", "lru_ba", "lru_bx", "lru_lambda", "ffn_conv_w")
SMALL_REPL = ("norm1_g", "dn_a_log", "dn_dt_bias", "dn_norm_g", "lru_conv_b", "lru_wa", "lru_wx", "lru_norm_g", "norm2_g",
              "ffn_conv_b", "ple_norm_g", "ple_bg", "final_g")
WEIGHTS = ("norm1_g", "w_in", "dn_conv_w", "dn_a_log", "dn_dt_bias", "dn_norm_g", "lru_conv_w", "lru_conv_b", "lru_wa",
           "lru_ba", "lru_wx", "lru_bx", "lru_lambda", "lru_norm_g", "w_out", "norm2_g", "ffn_wg", "ffn_wu", "ffn_conv_w",
           "ffn_conv_b", "ffn_wd", "ple_norm_g", "ple_wg", "ple_bg", "ple_wp", "final_g")


def _pad_rows(flat, cols, mult):
    n = flat.shape[0]
    rows = -(-n // cols)
    rows = -(-rows // mult) * mult
    return jnp.pad(flat, (0, rows * cols - n)).reshape(rows, cols)


def _pack(arrs, cols, mult, dtype):
    return _pad_rows(jnp.concatenate([a.reshape(-1).astype(dtype) for a in arrs]), cols, mult)


def _unpack(flat, shapes):
    out, off = [], 0
    for shp in shapes:
        n = math.prod(shp)
        piece = flat[off:off + n]
        if n < 4096:
            piece = lax.optimization_barrier(piece)
        out.append(piece.reshape(shp))
        off += n
    return out


def _unpack8(g8, shapes, axes):
    out, off = [], 0
    for shp, ax in zip(shapes, axes):
        n = math.prod(shp)
        a = g8[:, off:off + n].reshape((N_DEV,) + tuple(shp))
        a = jnp.moveaxis(a, 0, ax)
        out.append(a.reshape(shp[:ax] + (N_DEV * shp[ax],) + shp[ax + 1:]))
        off += n
    return out


def _wcat_t_from_w_in_t(cfg, wt):
    nba = 4 * cfg.H
    pad = jnp.zeros((LANES - nba, wt.shape[1]), wt.dtype)
    return jnp.concatenate([wt[:cfg.LXO], wt[cfg.LXO + nba:], wt[cfg.LXO:cfg.LXO + nba], pad], axis=0)


def _w_in_t_from_wcat_t(cfg, wc):
    nba = 4 * cfg.H
    return jnp.concatenate([wc[:cfg.LXO], wc[cfg.BAO:cfg.BAO + nba], wc[cfg.LXO:cfg.BAO]], axis=0)


def _gate_row(cfg, a):
    h2 = 2 * cfg.H
    return jnp.concatenate([jnp.zeros((1, h2), f32), a.reshape(1, h2), jnp.zeros((1, LANES - 2 * h2), f32)], axis=1)


def _blockdiag(cfg, w):
    w = w.reshape(2, cfg.NB // 2, 2, 64, 64)
    z = jnp.zeros_like(w[:, :, 0])
    top = jnp.concatenate([w[:, :, 0], z], axis=-1)
    bot = jnp.concatenate([z, w[:, :, 1]], axis=-1)
    return jnp.concatenate([top, bot], axis=-2).astype(bf16)


def _unblockdiag(cfg, g):
    a = g[:, :, :64, :64]
    b = g[:, :, 64:, 64:]
    return jnp.stack([a, b], axis=2).reshape(2, cfg.NB, 64, 64)


def _layer_operands(cfg, big, small, i):
    return dict(
        wcat_t=_wcat_t_from_w_in_t(cfg, big["w_in"][i]), w_out=big["w_out"][i], ffn_wg_t=big["ffn_wg"][i],
        ffn_wu_t=big["ffn_wu"][i], ffn_wd=big["ffn_wd"][i], ple_wg=big["ple_wg"][i], ple_wp_t=big["ple_wp"][i],
        norm1_g=small["norm1_g"][i][None], dn_conv_w=small["dn_conv_w"][i], alog_row=_gate_row(cfg, small["dn_a_log"][i]),
        dt_row=_gate_row(cfg, small["dn_dt_bias"][i]), dn_norm_g=small["dn_norm_g"][i][None],
        lru_conv_w=small["lru_conv_w"][i], lru_conv_b=small["lru_conv_b"][i][None],
        wa=_blockdiag(cfg, small["lru_wa"][i]), wx=_blockdiag(cfg, small["lru_wx"][i]),
        lru_ba=small["lru_ba"][i], lru_bx=small["lru_bx"][i], lru_lambda=small["lru_lambda"][i],
        lru_norm_g=small["lru_norm_g"][i][None], norm2_g=small["norm2_g"][i][None], ffn_conv_w=small["ffn_conv_w"][i],
        ffn_conv_b=small["ffn_conv_b"][i][None], ple_norm_g=small["ple_norm_g"][i][None], ple_bg=small["ple_bg"][i][None],
    )


def _small_grads_to_problem(cfg, g):
    h = cfg.H
    return dict(
        norm1_g=g["norm1_g"][0], dn_conv_w=g["dn_conv_w"],
        dn_a_log=g["alog_row"][0, 2 * h:4 * h].reshape(2, h), dn_dt_bias=g["dt_row"][0, 2 * h:4 * h].reshape(2, h),
        dn_norm_g=g["dn_norm_g"][0], lru_conv_w=g["lru_conv_w"], lru_conv_b=g["lru_conv_b"][0],
        lru_wa=_unblockdiag(cfg, g["wa"]), lru_wx=_unblockdiag(cfg, g["wx"]), lru_ba=g["lru_ba"], lru_bx=g["lru_bx"],
        lru_lambda=g["lru_lambda"], lru_norm_g=g["lru_norm_g"][0], norm2_g=g["norm2_g"][0], ffn_conv_w=g["ffn_conv_w"],
        ffn_conv_b=g["ffn_conv_b"][0], ple_norm_g=g["ple_norm_g"][0], ple_bg=g["ple_bg"][0],
    )


def _local_step(cfg, big, small, x, p, target):
    r = x.reshape(cfg.T, cfg.D)
    ops, saved = [], []
    for i in range(cfg.L):
        w = _layer_operands(cfg, big, small, i)
        r, sv = _layer_fwd(cfg, w, r, p[i].reshape(cfg.T, cfg.PD))
        ops.append(w)
        saved.append(sv)
    dr, loss, dgf = _k_loss(cfg, r, target.reshape(cfg.T, cfg.D), small["final_g"][None])
    gbig, gsmall = [None] * cfg.L, [None] * cfg.L
    for i in reversed(range(cfg.L)):
        dr, g = _layer_bwd(cfg, ops[i], saved[i], dr)
        gbig[i] = {n: (_w_in_t_from_wcat_t(cfg, g["wcat_t"]) if n == "w_in" else g[BIG_OPERAND[n]]) for n in BIG}
        gsmall[i] = _small_grads_to_problem(cfg, g)
    gs = {k: jnp.stack([gl[k] for gl in gsmall]) for k in gsmall[0]}
    gs["final_g"] = dgf[0]
    return loss, dr, gbig, gs


def _row_tile(rows, limit=512):
    best = rows
    for t in range(8, min(rows, limit) + 1, 8):
        if rows % t == 0:
            best = t
    return best if best <= limit or rows <= limit else rows


def _adam_group(name, ws, gs, ms, vs, cols, tr):
    shapes = [w.shape for w in ws]
    pk = lambda arrs: _pack(arrs, cols, tr, f32)
    w2 = pk(ws)
    d, nm, nv = _adam_call(name, w2, pk(gs), pk(ms), pk(vs), min(tr, w2.shape[0]))
    return [_unpack(a.reshape(-1), shapes) for a in (d, nm, nv)]


def kernel(x, p, norm1_g, w_in, dn_conv_w, dn_a_log, dn_dt_bias, dn_norm_g, lru_conv_w, lru_conv_b, lru_wa, lru_ba, lru_wx, lru_bx, lru_lambda, lru_norm_g, w_out, norm2_g, ffn_wg, ffn_wu, ffn_conv_w, ffn_conv_b, ffn_wd, ple_norm_g, ple_wg, ple_bg, ple_wp, final_g, loss_target, m_norm1_g, m_w_in, m_dn_conv_w, m_dn_a_log, m_dn_dt_bias, m_dn_norm_g, m_lru_conv_w, m_lru_conv_b, m_lru_wa, m_lru_ba, m_lru_wx, m_lru_bx, m_lru_lambda, m_lru_norm_g, m_w_out, m_norm2_g, m_ffn_wg, m_ffn_wu, m_ffn_conv_w, m_ffn_conv_b, m_ffn_wd, m_ple_norm_g, m_ple_wg, m_ple_bg, m_ple_wp, m_final_g, v_norm1_g, v_w_in, v_dn_conv_w, v_dn_a_log, v_dn_dt_bias, v_dn_norm_g, v_lru_conv_w, v_lru_conv_b, v_lru_wa, v_lru_ba, v_lru_wx, v_lru_bx, v_lru_lambda, v_lru_norm_g, v_w_out, v_norm2_g, v_ffn_wg, v_ffn_wu, v_ffn_conv_w, v_ffn_conv_b, v_ffn_wd, v_ple_norm_g, v_ple_wg, v_ple_bg, v_ple_wp, v_final_g):
    cfg = CFG
    a = dict(locals())
    wl = {n: a[n] for n in WEIGHTS}
    ml = {n: a["m_" + n] for n in WEIGHTS}
    vl = {n: a["v_" + n] for n in WEIGHTS}
    me = 4 * lax.axis_index("x") + 2 * lax.axis_index("y") + lax.axis_index("c")
    nl = cfg.L

    blocks = [(jnp.swapaxes(wl[n], 1, 2) if BIG_T[n] else wl[n]).astype(bf16) for n in BIG]
    ss_shapes = [wl[n].shape for n in SMALL_SHARDED]
    *g8, s8 = _all_gather("gather_weights", blocks + [_pack([wl[n] for n in SMALL_SHARDED], LANES, 8, f32)])
    big = {n: jnp.moveaxis(g, 0, 1).reshape(nl, N_DEV * g.shape[2], g.shape[3]) for n, g in zip(BIG, g8)}
    small = dict(zip(SMALL_SHARDED, _unpack8(s8.reshape(N_DEV, -1), ss_shapes, [2] * len(ss_shapes))))
    small.update({n: wl[n] for n in SMALL_REPL})

    loss_part, dr, gbig, gsmall = _local_step(cfg, big, small, x, p, loss_target)
    grad_x = dr.reshape(x.shape)

    send = []
    for n, blk in zip(BIG, blocks):
        g = jnp.stack([gbig[i][n].reshape((N_DEV,) + blk.shape[1:]) for i in range(nl)], axis=1)
        send.append(g.astype(bf16))
    recv = _all_to_all("exchange_grads", send)
    gl = {}
    for n, r8 in zip(BIG, recv):
        s = _sum8_layers("sum_" + n, r8)
        gl[n] = jnp.swapaxes(s, 1, 2) if BIG_T[n] else s

    small_names = SMALL_REPL + SMALL_SHARDED
    small_shapes = [gsmall[n].shape for n in small_names]
    sv = _pack([gsmall[n] for n in small_names] + [loss_part[0, 0:1]], LANES, 512, f32)
    small_sum = _sum8_call("sum_small", _all_gather("gather_small_grads", [sv])[0], 512).reshape(-1)
    gl.update(zip(small_names, _unpack(small_sum, small_shapes)))
    loss = small_sum[sum(math.prod(s) for s in small_shapes)]
    for n in SMALL_SHARDED:
        shard = wl[n].shape[2]
        gl[n] = lax.dynamic_slice_in_dim(gl[n], me * shard, shard, axis=2)

    outs = {}
    for n in BIG:
        shp = wl[n].shape
        two = lambda t: t.reshape(-1, shp[-1])
        d, nm, nv = _adam_call("adam_" + n, two(wl[n]), two(gl[n]), two(ml[n]), two(vl[n]), _row_tile(math.prod(shp[:-1])))
        outs[n] = (d.reshape(shp), nm.reshape(shp), nv.reshape(shp))
    d, nm, nv = _adam_group("adam_small", [wl[n] for n in small_names], [gl[n] for n in small_names],
                            [ml[n] for n in small_names], [vl[n] for n in small_names], LANES, 64)
    for j, n in enumerate(small_names):
        outs[n] = (d[j], nm[j], nv[j])
    return (loss, grad_x, *[gl[n] for n in WEIGHTS], *[outs[n][0] for n in WEIGHTS], *[outs[n][1] for n in WEIGHTS],
            *[outs[n][2] for n in WEIGHTS])
```

```python
import functools
import math

import jax
import jax.numpy as jnp
from jax import lax
from jax.experimental import pallas as pl
from jax.experimental.pallas import tpu as pltpu

f32 = jnp.float32
bf16 = jnp.bfloat16
MESH = pl.DeviceIdType.MESH

N_DEV = 8
LANES = 128
EPS = 1e-6
LRU_C = 8.0
ADAM_LR, ADAM_B1, ADAM_B2, ADAM_EPS, ADAM_WD, ADAM_STEP = 0.001, 0.9, 0.999, 1e-08, 0.01, 10
VMEM_LIMIT = 56 * 1024 * 1024


class Cfg:
    def __init__(self, d_model=1024, bl=4, seq=2048, depth=4, heads=4, lru_width=512, d_ff=2816, ple=256,
                 tm=512, tm_ffn=256, ff_tile=256):
        self.D, self.BL, self.S, self.L, self.H = d_model, bl, seq, depth, heads
        self.DH = 128
        self.DN = heads * self.DH
        self.LW = lru_width
        self.NB = lru_width // 64
        self.FF, self.PD = d_ff, ple
        self.C = 64
        self.NC = seq // self.C
        self.T = bl * seq
        self.TM = min(tm, self.T)
        self.TMF = min(tm_ffn, self.T)
        self.FT = ff_tile
        self.ZO = 3 * self.DN
        self.LXO = 4 * self.DN
        self.LGO = self.LXO + self.LW
        self.BAO = self.LGO + self.LW
        self.PC = self.BAO + LANES
        self.IN_COLS = 4 * self.DN + 4 * heads + 2 * self.LW


CFG = Cfg()


def _mm(a, b):
    return jnp.dot(a.astype(bf16), b.astype(bf16), preferred_element_type=f32)


def _mm_nt(a, b):
    return lax.dot_general(a.astype(bf16), b.astype(bf16), (((1,), (1,)), ((), ())), preferred_element_type=f32)


def _mm_tn(a, b):
    return lax.dot_general(a.astype(bf16), b.astype(bf16), (((0,), (0,)), ((), ())), preferred_element_type=f32)


def _split2(a):
    hi = a.astype(bf16)
    return hi, (a - hi.astype(f32)).astype(bf16)


def _hmm3(a, b, dims=(((1,), (0,)), ((), ()))):
    ah, al = _split2(a)
    bh, bl = _split2(b)
    dot = functools.partial(lax.dot_general, dimension_numbers=dims, preferred_element_type=f32)
    return dot(ah, bh) + dot(ah, bl) + dot(al, bh)


def _hmm3_tn(a, b):
    return _hmm3(a, b, (((0,), (0,)), ((), ())))


def _cum_mm(mask, x, dims=(((1,), (0,)), ((), ()))):
    m = mask.astype(bf16)
    x1 = x.astype(bf16)
    r = x - x1.astype(f32)
    x2 = r.astype(bf16)
    x3 = (r - x2.astype(f32)).astype(bf16)
    dot = functools.partial(lax.dot_general, dimension_numbers=dims, preferred_element_type=f32)
    return dot(m, x1) + dot(m, x2) + dot(m, x3)


def _rms_fwd(x, g):
    inv = lax.rsqrt(jnp.mean(x * x, axis=-1, keepdims=True) + EPS)
    xh = x * inv
    return xh * g, xh, inv


def _rms_bwd(dy, xh, inv, g):
    dxh = dy * g
    dx = inv * (dxh - xh * jnp.mean(dxh * xh, axis=-1, keepdims=True))
    dg = jnp.sum(dy * xh, axis=0, keepdims=True)
    return dx, dg


def _sigmoid(x):
    return 1.0 / (1.0 + jnp.exp(-x))


def _softplus(x):
    return jnp.maximum(x, 0.0) + jnp.log(1.0 + jnp.exp(-jnp.abs(x)))


def _silu(x):
    s = _sigmoid(x)
    return x * s, s * (1.0 + x * (1.0 - s))


_GC = math.sqrt(2.0 / math.pi)


def _gelu(x):
    t = jnp.tanh(_GC * (x + 0.044715 * x * x * x))
    y = 0.5 * x * (1.0 + t)
    dy = 0.5 * (1.0 + t) + 0.5 * x * (1.0 - t * t) * _GC * (1.0 + 3.0 * 0.044715 * x * x)
    return y, dy


def _nexpm1(x):
    ser = -x * (1.0 + x * 0.5 * (1.0 + x * (1.0 / 3.0) * (1.0 + x * 0.25 * (1.0 + x * 0.2))))
    return jnp.where(x > -0.1, ser, 1.0 - jnp.exp(x))


def _shift(x, s, fill=0.0):
    if s == 0:
        return x
    n = x.shape[0]
    t = lax.broadcasted_iota(jnp.int32, x.shape, 0)
    r = pltpu.roll(x, (-s) % n, 0)
    return jnp.where((t + s >= 0) & (t + s < n), r, fill)


def _conv_fwd(x, w_ref, left):
    k = w_ref.shape[0]
    out = _shift(x, -left) * w_ref[0:1, :]
    for j in range(1, k):
        out = out + _shift(x, j - left) * w_ref[j:j + 1, :]
    return out


def _conv_bwd(dout, x, w_ref, left):
    k = w_ref.shape[0]
    dx = None
    dws = []
    for j in range(k):
        term = _shift(dout, -(j - left)) * w_ref[j:j + 1, :]
        dx = term if dx is None else dx + term
        dws.append(jnp.sum(dout * _shift(x, j - left), axis=0, keepdims=True))
    return dx, dws


def _lin_scan(a, b, rev):
    n = a.shape[0]
    d = 1
    while d < n:
        s = d if rev else -d
        b = a * _shift(b, s, 0.0) + b
        a = a * _shift(a, s, 1.0)
        d *= 2
    return b


def _tri_masks(c, rev):
    i = lax.broadcasted_iota(jnp.int32, (c, c), 0)
    j = lax.broadcasted_iota(jnp.int32, (c, c), 1)
    incl = (i <= j) if rev else (i >= j)
    strict = (i < j) if rev else (i > j)
    incl_t = (i >= j) if rev else (i <= j)
    return incl, strict, incl_t


def _map(f, *lists):
    return [f(*a) for a in zip(*lists)]


def _tri_inv(mats):
    c = mats[0].shape[0]
    i = lax.broadcasted_iota(jnp.int32, (c, c), 0)
    j = lax.broadcasted_iota(jnp.int32, (c, c), 1)
    eye = jnp.where(i == j, 1.0, 0.0)
    t = [eye - a for a in mats]
    pw = _map(_hmm3, mats, mats)
    for it in range(5):
        t = _map(lambda ti, ui: ti + ui, t, _map(_hmm3, t, pw))
        if it < 4:
            pw = _map(_hmm3, pw, pw)
    return t


def _gdn_decay(gs, revs):
    c = gs[0].shape[0]
    masks = [_tri_masks(c, r) for r in revs]
    gb = [jnp.broadcast_to(g, (c, c)) for g in gs]
    mcol = _map(lambda m, x: _cum_mm(m[0], x), masks, gb)
    mrow = _map(lambda m, x: jnp.sum(jnp.where(m[2], x, 0.0), axis=0, keepdims=True), masks, gb)
    dec = _map(lambda m, a, b: jnp.exp(jnp.where(m[0], a - b, -1e30)), masks, mcol, mrow)
    return [m[:, 0:1] for m in mcol], [jnp.sum(g, axis=0, keepdims=True) for g in gs], dec


def _gdn_prep(qs, k, v, g, beta, kk, qk, revs):
    c = k[0].shape[0]
    masks = [_tri_masks(c, r) for r in revs]
    gcum, glast, dec = _gdn_decay(g, revs)
    e = [jnp.exp(x) for x in gcum]
    tm = _tri_inv(_map(lambda m, b, x, d: jnp.where(m[1], b * x * d, 0.0), masks, beta, kk, dec))
    u = _map(lambda t, vi, b: _hmm3(t, vi * b), tm, v, beta)
    w = _map(lambda t, ki, b, ei: _hmm3(t, ki * b * ei), tm, k, beta, e)
    p = _map(lambda m, x, d: jnp.where(m[0], x * d, 0.0), masks, qk, dec)
    return dict(dec=dec, cd=[jnp.exp(x) for x in glast], tm=tm, u=u, w=w, p=p, qd=_map(lambda a, b: a * b, qs, e),
                kd=_map(lambda ki, gl, gc: ki * jnp.exp(gl - gc), k, glast, gcum))


def _lane_pick(x, lane):
    l = lax.broadcasted_iota(jnp.int32, x.shape, 1)
    return jnp.sum(jnp.where(l == lane, x, 0.0), axis=1, keepdims=True)


def _params(sem, vmem=VMEM_LIMIT):
    return pltpu.CompilerParams(dimension_semantics=sem, vmem_limit_bytes=vmem)


def _row_call(name, body, t, tm, row_ins, full_ins, row_outs, acc_outs):
    in_specs = [pl.BlockSpec((tm, w), functools.partial(lambda i, c: (i, c), c=c)) for (_, w, c) in row_ins]
    for a in full_ins:
        in_specs.append(pl.BlockSpec(a.shape, functools.partial(lambda i, n: (0,) * n, n=a.ndim)))
    out_specs = [pl.BlockSpec((tm, w), lambda i: (i, 0)) for (w, _) in row_outs]
    out_shape = [jax.ShapeDtypeStruct((t, w), dt) for (w, dt) in row_outs]
    for shp, dt in acc_outs:
        out_specs.append(pl.BlockSpec(shp, functools.partial(lambda i, n: (0,) * n, n=len(shp))))
        out_shape.append(jax.ShapeDtypeStruct(shp, dt))
    return pl.pallas_call(
        body, name=name, grid=(t // tm,), in_specs=in_specs, out_specs=out_specs, out_shape=out_shape,
        compiler_params=_params(("arbitrary",)),
    )(*[a for (a, _, _) in row_ins], *full_ins)


def _k_in(cfg, r, g1, wcat):
    def body(r_ref, g_ref, w_ref, h_ref, proj_ref):
        y, _, _ = _rms_fwd(r_ref[...], g_ref[...])
        hb = y.astype(bf16)
        h_ref[...] = hb
        proj_ref[...] = _mm_nt(hb, w_ref[...])

    return _row_call("in_proj", body, cfg.T, cfg.TM, [(r, cfg.D, 0)], [g1, wcat],
                     [(cfg.D, bf16), (cfg.PC, f32)], [])


def _k_prep(cfg, proj, conv_w):
    dn, s = cfg.DN, cfg.S

    def body(x_ref, w_ref, o_ref):
        sec = pl.program_id(1)
        c = _conv_fwd(x_ref[...], w_ref, 2)
        y, _ = _silu(c)
        for h in range(cfg.H):
            yh = y[:, h * 128:(h + 1) * 128]
            nh = yh * lax.rsqrt(jnp.sum(yh * yh, axis=1, keepdims=True) + EPS)
            o_ref[:, h * 128:(h + 1) * 128] = jnp.where(sec < 2, nh, yh)

    return pl.pallas_call(
        body, name="dn_prep", grid=(cfg.BL, 3),
        in_specs=[pl.BlockSpec((s, dn), lambda b, j: (b, j)), pl.BlockSpec((4, dn), lambda b, j: (0, j))],
        out_specs=pl.BlockSpec((s, dn), lambda b, j: (b, j)),
        out_shape=jax.ShapeDtypeStruct((cfg.T, 3 * dn), f32),
        compiler_params=_params(("arbitrary", "arbitrary")),
    )(proj, conv_w)


def _gate_cols(ba, alog_row, dt_row, lane_b, lane_a):
    beta = _sigmoid(_lane_pick(ba, lane_b))
    alpha = _lane_pick(ba, lane_a)
    aexp = jnp.exp(_lane_pick(alog_row, lane_a))
    dtb = _lane_pick(dt_row, lane_a)
    xa = alpha + dtb
    g = -aexp * _softplus(xa)
    return beta, g, aexp, xa


def _gdn_specs(cfg):
    s, c, nc = cfg.S, cfg.C, cfg.NC
    tok2 = pl.BlockSpec((2, s, 128), lambda b, h: (0, b, h))
    mat2 = pl.BlockSpec((2, 1, nc, c, c), lambda b, h: (0, b, h, 0, 0))
    cd2 = pl.BlockSpec((2, 1, nc, 8, LANES), lambda b, h: (0, b, h, 0, 0))
    shapes = dict(
        tok32=jax.ShapeDtypeStruct((2, cfg.T, cfg.DN), f32), tok16=jax.ShapeDtypeStruct((2, cfg.T, cfg.DN), bf16),
        mat32=jax.ShapeDtypeStruct((2, cfg.BL, cfg.H * nc, c, c), f32), mat16=jax.ShapeDtypeStruct((2, cfg.BL, cfg.H * nc, c, c), bf16),
        cd=jax.ShapeDtypeStruct((2, cfg.BL, cfg.H * nc, 8, LANES), f32))
    return tok2, mat2, cd2, shapes


def _k_gdn_fwd(cfg, qkv, proj, alog_row, dt_row):
    s, c, nc, hh = cfg.S, cfg.C, cfg.NC, cfg.H
    ba_blk = cfg.BAO // LANES
    per = 4 if nc % 4 == 0 else 1

    def body(q_ref, k_ref, v_ref, ba_ref, al_ref, dt_ref, o_ref, u_o, w_o, qd_o, kd_o, p_o, t_o, dec_o, cd_o):
        h = pl.program_id(1)

        def prep(i, carry):
            chains, qs, k, v, kk, qk, g, beta = [], [], [], [], [], [], [], []
            for j in range(per):
                n = i * per + j
                rows = pl.ds(pl.multiple_of(n * c, c), c)
                kj, vj = k_ref[rows, :], v_ref[rows, :]
                qj = q_ref[rows, :] * (128 ** -0.5)
                kkj, qkj = _mm_nt(kj, kj), _mm_nt(qj, kj)
                ba = ba_ref[rows, :]
                for d in range(2):
                    bd, gd, _, _ = _gate_cols(ba, al_ref[...], dt_ref[...], d * hh + h, 2 * hh + d * hh + h)
                    chains.append((d, n, rows))
                    for lst, val in ((qs, qj), (k, kj), (v, vj), (kk, kkj), (qk, qkj), (g, gd), (beta, bd)):
                        lst.append(val)
            z = _gdn_prep(qs, k, v, g, beta, kk, qk, [d == 1 for d, _, _ in chains])
            for x, (d, n, rows) in enumerate(chains):
                u_o[d, rows, :] = z["u"][x]
                w_o[d, rows, :] = z["w"][x].astype(bf16)
                qd_o[d, rows, :] = z["qd"][x].astype(bf16)
                kd_o[d, rows, :] = z["kd"][x].astype(bf16)
                p_o[d, 0, n] = z["p"][x].astype(bf16)
                t_o[d, 0, n] = z["tm"][x]
                dec_o[d, 0, n] = z["dec"][x]
                cd_o[d, 0, n] = jnp.broadcast_to(z["cd"][x], (8, LANES))
            return carry

        lax.fori_loop(0, nc // per, prep, 0)
        o_ref[...] = jnp.zeros_like(o_ref)

        def step(i, sts):
            ns = [i, nc - 1 - i]
            rows = [pl.ds(pl.multiple_of(n * c, c), c) for n in ns]
            vn = [u_o[d, rows[d], :] - _mm(w_o[d, rows[d], :], sts[d]) for d in range(2)]
            o = [_mm(qd_o[d, rows[d], :], sts[d]) + _mm(p_o[d, 0, ns[d]], vn[d]) for d in range(2)]
            new = [sts[d] * cd_o[d, 0, ns[d]][0:1, :] + _mm_tn(kd_o[d, rows[d], :], vn[d]) for d in range(2)]
            for d in range(2):
                o_ref[rows[d], :] = o_ref[rows[d], :] + o[d]
            return tuple(new)

        z0 = jnp.zeros((128, 128), f32)
        lax.fori_loop(0, nc, step, (z0, z0))

    blk = lambda off: pl.BlockSpec((s, 128), functools.partial(lambda b, h, off: (b, off + h), off=off))
    row = pl.BlockSpec((1, LANES), lambda b, h: (0, 0))
    tok2, mat2, cd2, shp = _gdn_specs(cfg)
    return pl.pallas_call(
        body, name="gdn_fwd", grid=(cfg.BL, hh),
        in_specs=[blk(0), blk(hh), blk(2 * hh), pl.BlockSpec((s, LANES), lambda b, h: (b, ba_blk)), row, row],
        out_specs=[pl.BlockSpec((s, 128), lambda b, h: (b, h)), tok2, tok2, tok2, tok2, mat2, mat2, mat2, cd2],
        out_shape=[jax.ShapeDtypeStruct((cfg.T, cfg.DN), f32), shp["tok32"], shp["tok16"], shp["tok16"], shp["tok16"],
                   shp["mat16"], shp["mat32"], shp["mat32"], shp["cd"]],
        compiler_params=_params(("arbitrary", "arbitrary")),
    )(qkv, qkv, qkv, proj, alog_row, dt_row)


def _lru_dir_fwd(xc, wa, wx, ba, bx, lam, rev):
    ra = _sigmoid(_mm(xc, wa) + ba)
    ig = _sigmoid(_mm(xc, wx) + bx)
    sp = _softplus(-lam)
    la = -LRU_C * ra * sp
    a = jnp.exp(la)
    m = jnp.sqrt(_nexpm1(2.0 * la))
    gx = ig * xc
    h = _lin_scan(a, m * gx, rev)
    return dict(ra=ra, ig=ig, sp=sp, a=a, m=m, gx=gx, h=h)


def _lru_specs(cfg, outer_b):
    s = cfg.S
    lx_blk = cfg.LXO // LANES
    if outer_b:
        ix = lambda f: (lambda b, ct: f(b, ct))
    else:
        ix = lambda f: (lambda ct, b: f(b, ct))
    return dict(
        lx=pl.BlockSpec((s, LANES), ix(lambda b, ct: (b, lx_blk + ct))),
        tok=pl.BlockSpec((s, LANES), ix(lambda b, ct: (b, ct))),
        cw=pl.BlockSpec((4, LANES), ix(lambda b, ct: (0, ct))),
        row=pl.BlockSpec((1, LANES), ix(lambda b, ct: (0, ct))),
        w=pl.BlockSpec((2, 1, LANES, LANES), ix(lambda b, ct: (0, ct, 0, 0))),
        two=pl.BlockSpec((2, LANES), ix(lambda b, ct: (0, ct))),
    )


def _k_lru_fwd(cfg, proj, conv_w, conv_b, wa, wx, ba, bx, lam):
    def body(lx_ref, cw_ref, cb_ref, wa_ref, wx_ref, ba_ref, bx_ref, lam_ref, o_ref):
        xc = _conv_fwd(lx_ref[...], cw_ref, 2) + cb_ref[...]
        acc = None
        for d in range(2):
            z = _lru_dir_fwd(xc, wa_ref[d, 0], wx_ref[d, 0], ba_ref[d:d + 1, :], bx_ref[d:d + 1, :], lam_ref[d:d + 1, :], d == 1)
            acc = z["h"] if acc is None else acc + z["h"]
        o_ref[...] = acc

    sp = _lru_specs(cfg, True)
    return pl.pallas_call(
        body, name="lru_fwd", grid=(cfg.BL, cfg.LW // LANES),
        in_specs=[sp["lx"], sp["cw"], sp["row"], sp["w"], sp["w"], sp["two"], sp["two"], sp["two"]],
        out_specs=sp["tok"], out_shape=jax.ShapeDtypeStruct((cfg.T, cfg.LW), f32),
        compiler_params=_params(("arbitrary", "arbitrary")),
    )(proj, conv_w, conv_b, wa, wx, ba, bx, lam)


def _mix_parts(cfg, o, z, lg, hs, dng, lrg):
    heads = []
    for h in range(cfg.H):
        sl = slice(h * 128, (h + 1) * 128)
        y, xh, inv = _rms_fwd(o[:, sl], dng)
        sz, dsz = _silu(z[:, sl])
        heads.append((y, xh, inv, sz, dsz))
    gl, dgl = _gelu(lg)
    y2, xh2, inv2 = _rms_fwd(gl * hs, lrg)
    return heads, (y2, xh2, inv2, gl, dgl)


def _k_mix(cfg, o, proj, hs, r, dng, lrg, wout):
    dn = cfg.DN

    def body(o_ref, z_ref, lg_ref, hs_ref, r_ref, dng_ref, lrg_ref, w_ref, mix_ref, out_ref):
        heads, lru = _mix_parts(cfg, o_ref[...], z_ref[...], lg_ref[...], hs_ref[...], dng_ref[...], lrg_ref[...])
        for h, (y, _, _, sz, _) in enumerate(heads):
            mix_ref[:, h * 128:(h + 1) * 128] = (y * sz).astype(bf16)
        mix_ref[:, dn:] = lru[0].astype(bf16)
        out_ref[...] = r_ref[...] + jnp.dot(mix_ref[...], w_ref[...], preferred_element_type=f32)

    return _row_call("mix_out", body, cfg.T, cfg.TM,
                     [(o, dn, 0), (proj, dn, cfg.ZO // dn), (proj, cfg.LW, cfg.LGO // cfg.LW), (hs, cfg.LW, 0), (r, cfg.D, 0)],
                     [dng, lrg, wout], [(cfg.D, bf16), (cfg.D, f32)], [])


def _k_ffn_a(cfg, r, g2, wg, wu):
    def body(r_ref, g_ref, wg_ref, wu_ref, h_ref, gp_ref, up_ref):
        y, _, _ = _rms_fwd(r_ref[...], g_ref[...])
        hb = y.astype(bf16)
        h_ref[...] = hb
        gp_ref[...] = _mm_nt(hb, wg_ref[...])
        up_ref[...] = _mm_nt(hb, wu_ref[...])

    return _row_call("ffn_in", body, cfg.T, cfg.TMF, [(r, cfg.D, 0)], [g2, wg, wu],
                     [(cfg.D, bf16), (cfg.FF, f32), (cfg.FF, f32)], [])


def _k_ffn_b(cfg, gp, up, conv_w, conv_b):
    s, ft = cfg.S, cfg.FT

    def body(gp_ref, up_ref, w_ref, b_ref, o_ref):
        gate = _conv_fwd(gp_ref[...], w_ref, 1) + b_ref[...]
        gl, _ = _gelu(gate)
        o_ref[...] = (gl * up_ref[...]).astype(bf16)

    tok = pl.BlockSpec((s, ft), lambda b, j: (b, j))
    return pl.pallas_call(
        body, name="ffn_act", grid=(cfg.BL, cfg.FF // ft),
        in_specs=[tok, tok, pl.BlockSpec((3, ft), lambda b, j: (0, j)), pl.BlockSpec((1, ft), lambda b, j: (0, j))],
        out_specs=tok, out_shape=jax.ShapeDtypeStruct((cfg.T, cfg.FF), bf16),
        compiler_params=_params(("arbitrary", "arbitrary")),
    )(gp, up, conv_w, conv_b)


def _k_ffn_c(cfg, act, r, wd):
    def body(a_ref, r_ref, w_ref, o_ref):
        o_ref[...] = r_ref[...] + jnp.dot(a_ref[...], w_ref[...], preferred_element_type=f32)

    return _row_call("ffn_out", body, cfg.T, cfg.TM, [(act, cfg.FF, 0), (r, cfg.D, 0)], [wd], [(cfg.D, f32)], [])[0]


def _k_ple(cfg, r, p, gp, wpg, bg, wpp):
    def body(r_ref, p_ref, g_ref, wg_ref, bg_ref, wp_ref, pn_ref, o_ref):
        x = r_ref[...]
        y, _, _ = _rms_fwd(x, g_ref[...])
        pn = y.astype(bf16)
        pn_ref[...] = pn
        pg = _sigmoid(jnp.dot(pn, wg_ref[...], preferred_element_type=f32) + bg_ref[...])
        o_ref[...] = x + pg * _mm_nt(p_ref[...], wp_ref[...])

    return _row_call("ple", body, cfg.T, cfg.TM, [(r, cfg.D, 0), (p, cfg.PD, 0)], [gp, wpg, bg, wpp],
                     [(cfg.D, bf16), (cfg.D, f32)], [])


def _k_loss(cfg, r, tgt, gf):
    d = cfg.D

    def body(r_ref, t_ref, g_ref, dr_ref, loss_ref, dg_ref):
        @pl.when(pl.program_id(0) == 0)
        def _():
            loss_ref[...] = jnp.zeros_like(loss_ref)
            dg_ref[...] = jnp.zeros_like(dg_ref)

        g = g_ref[...]
        y, xh, inv = _rms_fwd(r_ref[...], g)
        err = y - t_ref[...]
        loss_ref[...] = loss_ref[...] + (0.5 / d) * jnp.sum(err * err)
        dx, dg = _rms_bwd(err * (1.0 / d), xh, inv, g)
        dr_ref[...] = dx
        dg_ref[...] = dg_ref[...] + dg

    return _row_call("loss_head", body, cfg.T, cfg.TM, [(r, d, 0), (tgt, d, 0)], [gf], [(d, f32)],
                     [((1, LANES), f32), ((1, d), f32)])


def _zero_at_first(cond, *refs):
    @pl.when(cond)
    def _():
        for r in refs:
            r[...] = jnp.zeros_like(r)


def _b_ple(cfg, dr3, r2, p, gp, wpg, bg, wpp):
    d = cfg.D

    def body(dr_ref, r_ref, p_ref, g_ref, wg_ref, bg_ref, wp_ref, dr2_ref, dlog_ref, dpp_ref, dgp_ref, dbg_ref):
        _zero_at_first(pl.program_id(0) == 0, dgp_ref, dbg_ref)
        g = g_ref[...]
        dr = dr_ref[...]
        y, xh, inv = _rms_fwd(r_ref[...], g)
        pg = _sigmoid(jnp.dot(y.astype(bf16), wg_ref[...], preferred_element_type=f32) + bg_ref[...])
        pp = _mm_nt(p_ref[...], wp_ref[...])
        dpp_ref[...] = (dr * pg).astype(bf16)
        dlog = dr * pp * pg * (1.0 - pg)
        dlog_ref[...] = dlog.astype(bf16)
        dbg_ref[...] = dbg_ref[...] + jnp.sum(dlog, axis=0, keepdims=True)
        dx, dg = _rms_bwd(_mm_nt(dlog, wg_ref[...]), xh, inv, g)
        dgp_ref[...] = dgp_ref[...] + dg
        dr2_ref[...] = dr + dx

    return _row_call("ple_bwd", body, cfg.T, cfg.TM, [(dr3, d, 0), (r2, d, 0), (p, cfg.PD, 0)], [gp, wpg, bg, wpp],
                     [(d, f32), (d, bf16), (d, bf16)], [((1, d), f32), ((1, d), f32)])


def _b_ffn_bc(cfg, dr2, wd, gp, up, conv_w, conv_b):
    s, ft, d = cfg.S, cfg.FT, cfg.D

    def body(dr_ref, wd_ref, gp_ref, up_ref, w_ref, b_ref, dgp_ref, dup_ref, dwd_ref, dcw_ref, dcb_ref):
        _zero_at_first(pl.program_id(1) == 0, dwd_ref, dcw_ref, dcb_ref)
        drb = dr_ref[...].astype(bf16)
        dact = _mm_nt(drb, wd_ref[...])
        gpre = gp_ref[...]
        up = up_ref[...]
        gl, dgl = _gelu(_conv_fwd(gpre, w_ref, 1) + b_ref[...])
        dup_ref[...] = (dact * gl).astype(bf16)
        dgate = dact * up * dgl
        dcb_ref[...] = dcb_ref[...] + jnp.sum(dgate, axis=0, keepdims=True)
        dx, dws = _conv_bwd(dgate, gpre, w_ref, 1)
        for j, dw in enumerate(dws):
            dcw_ref[j:j + 1, :] = dcw_ref[j:j + 1, :] + dw
        dgp_ref[...] = dx.astype(bf16)
        dwd_ref[...] = dwd_ref[...] + _mm_tn((gl * up).astype(bf16), drb)

    tok = pl.BlockSpec((s, ft), lambda j, b: (b, j))
    return pl.pallas_call(
        body, name="ffn_act_bwd", grid=(cfg.FF // ft, cfg.BL),
        in_specs=[pl.BlockSpec((s, d), lambda j, b: (b, 0)), pl.BlockSpec((ft, d), lambda j, b: (j, 0)), tok, tok,
                  pl.BlockSpec((3, ft), lambda j, b: (0, j)), pl.BlockSpec((1, ft), lambda j, b: (0, j))],
        out_specs=[tok, tok, pl.BlockSpec((ft, d), lambda j, b: (j, 0)), pl.BlockSpec((3, ft), lambda j, b: (0, j)),
                   pl.BlockSpec((1, ft), lambda j, b: (0, j))],
        out_shape=[jax.ShapeDtypeStruct((cfg.T, cfg.FF), bf16), jax.ShapeDtypeStruct((cfg.T, cfg.FF), bf16),
                   jax.ShapeDtypeStruct((cfg.FF, d), f32), jax.ShapeDtypeStruct((3, cfg.FF), f32),
                   jax.ShapeDtypeStruct((1, cfg.FF), f32)],
        compiler_params=_params(("arbitrary", "arbitrary")),
    )(dr2, wd, gp, up, conv_w, conv_b)


def _b_ffn_a(cfg, dgp, dup, dr2, r_mid, g2, wg, wu):
    d = cfg.D

    def body(dgp_ref, dup_ref, dr_ref, r_ref, g_ref, wg_ref, wu_ref, o_ref, dg_ref):
        _zero_at_first(pl.program_id(0) == 0, dg_ref)
        g = g_ref[...]
        _, xh, inv = _rms_fwd(r_ref[...], g)
        dh = _mm(dgp_ref[...], wg_ref[...]) + _mm(dup_ref[...], wu_ref[...])
        dx, dg = _rms_bwd(dh, xh, inv, g)
        dg_ref[...] = dg_ref[...] + dg
        o_ref[...] = dr_ref[...] + dx

    return _row_call("ffn_in_bwd", body, cfg.T, cfg.TMF, [(dgp, cfg.FF, 0), (dup, cfg.FF, 0), (dr2, d, 0), (r_mid, d, 0)],
                     [g2, wg, wu], [(d, f32)], [((1, d), f32)])


def _mm_tn_call(cfg, name, x, dy, tn):
    t, k = x.shape
    n = dy.shape[1]
    tm = cfg.TM

    def body(x_ref, dy_ref, o_ref):
        _zero_at_first(pl.program_id(1) == 0, o_ref)
        o_ref[...] = o_ref[...] + _mm_tn(x_ref[...], dy_ref[...])

    return pl.pallas_call(
        body, name=name, grid=(n // tn, t // tm),
        in_specs=[pl.BlockSpec((tm, k), lambda j, i: (i, 0)), pl.BlockSpec((tm, tn), lambda j, i: (i, j))],
        out_specs=pl.BlockSpec((k, tn), lambda j, i: (0, j)),
        out_shape=jax.ShapeDtypeStruct((k, n), f32),
        compiler_params=_params(("arbitrary", "arbitrary")),
    )(x, dy)


def _b_mix(cfg, dr, o, proj, hs, dng, lrg, wout):
    dn, lw, d = cfg.DN, cfg.LW, cfg.D

    def body(dr_ref, o_ref, z_ref, lg_ref, hs_ref, dng_ref, lrg_ref, w_ref, do_ref, dz_ref, dlg_ref, dhs_ref, ddn_ref, dlr_ref):
        _zero_at_first(pl.program_id(0) == 0, ddn_ref, dlr_ref)
        dng, lrg = dng_ref[...], lrg_ref[...]
        hs = hs_ref[...]
        o, z = o_ref[...], z_ref[...]
        heads, lru = _mix_parts(cfg, o, z, lg_ref[...], hs, dng, lrg)
        drb = dr_ref[...].astype(bf16)
        dmix_dn = _mm_nt(drb, w_ref[0:dn, :])
        dmix_lr = _mm_nt(drb, w_ref[dn:, :])
        dgn = jnp.zeros_like(dng)
        for h, (y, xh, inv, sz, dsz) in enumerate(heads):
            sl = slice(h * 128, (h + 1) * 128)
            dm = dmix_dn[:, sl]
            dz_ref[:, sl] = dm * y * dsz
            dx, dg = _rms_bwd(dm * sz, xh, inv, dng)
            do_ref[:, sl] = dx
            dgn = dgn + dg
        ddn_ref[...] = ddn_ref[...] + dgn
        _, xh2, inv2, gl, dgl = lru
        dx2, dg2 = _rms_bwd(dmix_lr, xh2, inv2, lrg)
        dlr_ref[...] = dlr_ref[...] + dg2
        dlg_ref[...] = dx2 * hs * dgl
        dhs_ref[...] = dx2 * gl

    return _row_call("mix_bwd", body, cfg.T, cfg.TM,
                     [(dr, d, 0), (o, dn, 0), (proj, dn, cfg.ZO // dn), (proj, lw, cfg.LGO // lw), (hs, lw, 0)],
                     [dng, lrg, wout], [(dn, f32), (dn, f32), (lw, f32), (lw, f32)], [((1, 128), f32), ((1, lw), f32)])


def _b_lru(cfg, proj, dhs, conv_w, conv_b, wa, wx, ba, bx, lam):
    def body(lx_ref, dh_ref, cw_ref, cb_ref, wa_ref, wx_ref, ba_ref, bx_ref, lam_ref,
             dlx_ref, dcw_ref, dcb_ref, dwa_ref, dwx_ref, dba_ref, dbx_ref, dlam_ref):
        _zero_at_first(pl.program_id(1) == 0, dcw_ref, dcb_ref, dwa_ref, dwx_ref, dba_ref, dbx_ref, dlam_ref)
        lx = lx_ref[...]
        dh = dh_ref[...]
        xc = _conv_fwd(lx, cw_ref, 2) + cb_ref[...]
        dxc = jnp.zeros_like(xc)
        for d in range(2):
            rev = d == 1
            lam = lam_ref[d:d + 1, :]
            z = _lru_dir_fwd(xc, wa_ref[d, 0], wx_ref[d, 0], ba_ref[d:d + 1, :], bx_ref[d:d + 1, :], lam, rev)
            a, m, ra, ig, sp = z["a"], z["m"], z["ra"], z["ig"], z["sp"]
            a_next = _shift(a, -1 if rev else 1, 0.0)
            lmb = _lin_scan(a_next, dh, not rev)
            h_prev = _shift(z["h"], 1 if rev else -1, 0.0)
            da = lmb * h_prev
            dm = lmb * z["gx"]
            dgx = lmb * m
            dla = da * a - dm * (a * a) / jnp.maximum(m, 1e-30)
            dra = dla * (-LRU_C) * sp
            dsp = jnp.sum(dla * (-LRU_C) * ra, axis=0, keepdims=True)
            dlam_ref[d:d + 1, :] = dlam_ref[d:d + 1, :] - dsp * _sigmoid(-lam)
            dpa = dra * ra * (1.0 - ra)
            dpx = dgx * xc * ig * (1.0 - ig)
            dba_ref[d:d + 1, :] = dba_ref[d:d + 1, :] + jnp.sum(dpa, axis=0, keepdims=True)
            dbx_ref[d:d + 1, :] = dbx_ref[d:d + 1, :] + jnp.sum(dpx, axis=0, keepdims=True)
            dwa_ref[d, 0] = dwa_ref[d, 0] + _mm_tn(xc, dpa)
            dwx_ref[d, 0] = dwx_ref[d, 0] + _mm_tn(xc, dpx)
            dxc = dxc + dgx * ig + _mm_nt(dpa, wa_ref[d, 0]) + _mm_nt(dpx, wx_ref[d, 0])
        dcb_ref[...] = dcb_ref[...] + jnp.sum(dxc, axis=0, keepdims=True)
        dx, dws = _conv_bwd(dxc, lx, cw_ref, 2)
        for j, dw in enumerate(dws):
            dcw_ref[j:j + 1, :] = dcw_ref[j:j + 1, :] + dw
        dlx_ref[...] = dx

    sp = _lru_specs(cfg, False)
    nct = cfg.LW // LANES
    return pl.pallas_call(
        body, name="lru_bwd", grid=(nct, cfg.BL),
        in_specs=[sp["lx"], sp["tok"], sp["cw"], sp["row"], sp["w"], sp["w"], sp["two"], sp["two"], sp["two"]],
        out_specs=[sp["tok"], sp["cw"], sp["row"], sp["w"], sp["w"], sp["two"], sp["two"], sp["two"]],
        out_shape=[jax.ShapeDtypeStruct((cfg.T, cfg.LW), f32), jax.ShapeDtypeStruct((4, cfg.LW), f32),
                   jax.ShapeDtypeStruct((1, cfg.LW), f32), jax.ShapeDtypeStruct((2, nct, LANES, LANES), f32),
                   jax.ShapeDtypeStruct((2, nct, LANES, LANES), f32), jax.ShapeDtypeStruct((2, cfg.LW), f32),
                   jax.ShapeDtypeStruct((2, cfg.LW), f32), jax.ShapeDtypeStruct((2, cfg.LW), f32)],
        compiler_params=_params(("arbitrary", "arbitrary")),
    )(proj, dhs, conv_w, conv_b, wa, wx, ba, bx, lam)


def _b_gdn(cfg, qkv, proj, do, alog_row, dt_row, saved):
    s, c, nc, hh = cfg.S, cfg.C, cfg.NC, cfg.H
    ba_blk = cfg.BAO // LANES
    scale = 128 ** -0.5
    per = 4 if nc % 4 == 0 else 1

    def body(q_ref, k_ref, v_ref, ba_ref, do_ref, al_ref, dt_ref, u_i, w_i, qd_i, kd_i, p_i, t_i, dec_i, cd_i,
             dqkv_ref, dba_ref, dal_ref, ddt_ref, vn_s, dvn_s, st_s, dst_s):
        h = pl.program_id(1)
        _zero_at_first((pl.program_id(0) == 0) & (h == 0), dal_ref, ddt_ref)
        _zero_at_first(h == 0, dba_ref)
        lane = lax.broadcasted_iota(jnp.int32, (c, LANES), 1)
        lane1 = lax.broadcasted_iota(jnp.int32, (1, LANES), 1)
        dirs = (0, 1)
        lane_b = [d * hh + h for d in dirs]
        lane_a = [2 * hh + d * hh + h for d in dirs]

        def fstep(i, sts):
            ns = [i, nc - 1 - i]
            rows = [pl.ds(pl.multiple_of(n * c, c), c) for n in ns]
            for d in dirs:
                st_s[d, ns[d]] = sts[d]
            vn = [u_i[d, rows[d], :] - _mm(w_i[d, rows[d], :], sts[d]) for d in dirs]
            for d in dirs:
                vn_s[d, rows[d], :] = vn[d]
            return tuple(sts[d] * cd_i[d, 0, ns[d]][0:1, :] + _mm_tn(kd_i[d, rows[d], :], vn[d]) for d in dirs)

        z0 = jnp.zeros((128, 128), f32)
        lax.fori_loop(0, nc, fstep, (z0, z0))

        def bseq(i, dst):
            ns = [nc - 1 - i, i]
            rows = [pl.ds(pl.multiple_of(n * c, c), c) for n in ns]
            for d in dirs:
                dst_s[d, ns[d]] = dst[d]
            dob = [do_ref[rows[d], :] for d in dirs]
            dvn = [_mm_tn(p_i[d, 0, ns[d]], dob[d]) + _mm(kd_i[d, rows[d], :], dst[d]) for d in dirs]
            for d in dirs:
                dvn_s[d, rows[d], :] = dvn[d]
            return tuple(_mm_tn(qd_i[d, rows[d], :], dob[d]) + cd_i[d, 0, ns[d]][0:1, 0:1] * dst[d]
                         - _mm_tn(w_i[d, rows[d], :], dvn[d]) for d in dirs)

        lax.fori_loop(0, nc, bseq, (z0, z0))

        def bpar(i, carry):
            ch = [(d, i * per + j) for j in range(per) for d in dirs]
            rows = [pl.ds(pl.multiple_of(n * c, c), c) for _, n in ch]
            masks = [_tri_masks(c, d == 1) for d, _ in ch]
            ld = lambda ref: [ref[d, r, :] for (d, _), r in zip(ch, rows)]
            ldm = lambda ref: [ref[d, 0, n] for d, n in ch]
            q, k, v, dob = ([ref[r, :] for r in rows] for ref in (q_ref, k_ref, v_ref, do_ref))
            gates = [_gate_cols(ba_ref[r, :], al_ref[...], dt_ref[...], lane_b[d], lane_a[d]) for (d, _), r in zip(ch, rows)]
            beta, g, aexp, xa = ([gt[x] for gt in gates] for x in range(4))
            st, dst = ([ref[d, n] for d, n in ch] for ref in (st_s, dst_s))
            vn, dvn, u, w = ld(vn_s), ld(dvn_s), ld(u_i), ld(w_i)
            p = [x.astype(f32) for x in ldm(p_i)]
            tm, dec = ldm(t_i), ldm(dec_i)
            cd = [cd_i[d, 0, n][0:1, 0:1] for d, n in ch]
            dp = _map(lambda m, a, b: jnp.where(m[0], _mm_nt(a, b), 0.0), masks, dob, vn)
            dqd = _map(_mm_nt, dob, st)
            dkd = _map(_mm_nt, vn, dst)
            dw = _map(lambda a, b: -_mm_nt(a, b), dvn, st)
            dcd = _map(lambda a, b: jnp.sum(jnp.sum(a * b, axis=1, keepdims=True), axis=0, keepdims=True), st, dst)
            gcum = _map(lambda m, x: _cum_mm(m[0], jnp.broadcast_to(x, (c, c)))[:, 0:1], masks, g)
            glast = [jnp.sum(x, axis=0, keepdims=True) for x in g]
            e = [jnp.exp(x) for x in gcum]
            el = _map(lambda a, b: jnp.exp(a - b), glast, gcum)
            qs = [x * scale for x in q]
            kb = _map(lambda a, b: a * b, k, beta)
            a = _map(lambda m, b, ki, dc: jnp.where(m[1], b * _mm_nt(ki, ki) * dc, 0.0), masks, beta, k, dec)
            dvb = _map(_hmm3_tn, tm, dvn)
            dkbe = _map(_hmm3_tn, tm, dw)
            da = _map(lambda m, x, ui, y, wi: -jnp.where(m[1], _mm_nt(x, ui) + _mm_nt(y, wi), 0.0), masks, dvb, u, dkbe, w)
            g1 = _map(lambda x, y: x * y, da, dec)
            g2 = _map(lambda x, y: x * y, dp, dec)
            dkb = _map(lambda x, ki, y, ei: _mm(x, ki) + y * ei, g1, k, dkbe, e)
            dk = _map(lambda x, kbi, y, qi, z, b, t, l: _mm_tn(x, kbi) + _mm_tn(y, qi) + z * b + t * l,
                      g1, kb, g2, qs, dkb, beta, dkd, el)
            dqs = _map(lambda y, ki, x, ei: _mm(y, ki) + x * ei, g2, k, dqd, e)
            ddd = _map(lambda x, ai, y, pi: x * ai + y * pi, da, a, dp, p)
            ones = jnp.ones((c, LANES), f32)
            dgcum = _map(lambda x: jnp.sum(x, axis=1, keepdims=True) - _hmm3_tn(x, ones)[:, 0:1], ddd)
            for x, (d, n) in enumerate(ch):
                dbeta = jnp.sum(dvb[x] * v[x], axis=1, keepdims=True) + jnp.sum(dkb[x] * k[x], axis=1, keepdims=True)
                de = jnp.sum(dkbe[x] * kb[x], axis=1, keepdims=True) + jnp.sum(dqd[x] * qs[x], axis=1, keepdims=True)
                del_ = jnp.sum(dkd[x] * k[x], axis=1, keepdims=True)
                dgc = dgcum[x] + de * e[x] - del_ * el[x]
                dglast = jnp.sum(del_ * el[x], axis=0, keepdims=True) + dcd[x] * cd[x]
                dg = _cum_mm(masks[x][2], jnp.broadcast_to(dgc, (c, LANES)))[:, 0:1] + dglast
                r = rows[x]
                if d == 0:
                    dqkv_ref[0, r, :] = dqs[x] * scale
                    dqkv_ref[1, r, :] = dk[x]
                    dqkv_ref[2, r, :] = dvb[x] * beta[x]
                else:
                    dqkv_ref[0, r, :] = dqkv_ref[0, r, :] + dqs[x] * scale
                    dqkv_ref[1, r, :] = dqkv_ref[1, r, :] + dk[x]
                    dqkv_ref[2, r, :] = dqkv_ref[2, r, :] + dvb[x] * beta[x]
                dlb = dbeta * beta[x] * (1.0 - beta[x])
                dalpha = -dg * aexp[x] * _sigmoid(xa[x])
                dba_ref[r, :] = dba_ref[r, :] + jnp.where(lane == lane_b[d], dlb, 0.0) + jnp.where(lane == lane_a[d], dalpha, 0.0)
                dal_ref[...] = dal_ref[...] + jnp.where(lane1 == lane_a[d], jnp.sum(dg * g[x], axis=0, keepdims=True), 0.0)
                ddt_ref[...] = ddt_ref[...] + jnp.where(lane1 == lane_a[d], jnp.sum(dalpha, axis=0, keepdims=True), 0.0)
            return carry

        lax.fori_loop(0, nc // per, bpar, 0)

    blk = lambda off: pl.BlockSpec((s, 128), functools.partial(lambda b, h, off: (b, off + h), off=off))
    row = pl.BlockSpec((1, LANES), lambda b, h: (0, 0))
    tok2, mat2, cd2, _ = _gdn_specs(cfg)
    return pl.pallas_call(
        body, name="gdn_bwd", grid=(cfg.BL, hh),
        in_specs=[blk(0), blk(hh), blk(2 * hh), pl.BlockSpec((s, LANES), lambda b, h: (b, ba_blk)),
                  pl.BlockSpec((s, 128), lambda b, h: (b, h)), row, row, tok2, tok2, tok2, tok2, mat2, mat2, mat2, cd2],
        out_specs=[pl.BlockSpec((3, s, 128), lambda b, h: (0, b, h)), pl.BlockSpec((s, LANES), lambda b, h: (b, 0)), row, row],
        out_shape=[jax.ShapeDtypeStruct((3, cfg.T, cfg.DN), f32), jax.ShapeDtypeStruct((cfg.T, LANES), f32),
                   jax.ShapeDtypeStruct((1, LANES), f32), jax.ShapeDtypeStruct((1, LANES), f32)],
        scratch_shapes=[pltpu.VMEM((2, s, 128), f32)] * 2 + [pltpu.VMEM((2, nc, 128, 128), f32)] * 2,
        compiler_params=_params(("arbitrary", "arbitrary")),
    )(qkv, qkv, qkv, proj, do, alog_row, dt_row, *saved)


def _b_prep(cfg, proj, dqkv, conv_w):
    dn, s = cfg.DN, cfg.S

    def body(x_ref, dy_ref, w_ref, dx_ref, dw_ref):
        _zero_at_first(pl.program_id(1) == 0, dw_ref)
        sec = pl.program_id(0)
        x = x_ref[...]
        c = _conv_fwd(x, w_ref, 2)
        y, dsilu = _silu(c)
        dy = dy_ref[0]
        parts = []
        for h in range(cfg.H):
            sl = slice(h * 128, (h + 1) * 128)
            yh, dyh = y[:, sl], dy[:, sl]
            inv = lax.rsqrt(jnp.sum(yh * yh, axis=1, keepdims=True) + EPS)
            dn_h = inv * dyh - yh * (inv * inv * inv) * jnp.sum(dyh * yh, axis=1, keepdims=True)
            parts.append(jnp.where(sec < 2, dn_h, dyh))
        ds = jnp.concatenate(parts, axis=1) if len(parts) > 1 else parts[0]
        dx, dws = _conv_bwd(ds * dsilu, x, w_ref, 2)
        for j, dw in enumerate(dws):
            dw_ref[j:j + 1, :] = dw_ref[j:j + 1, :] + dw
        dx_ref[...] = dx

    return pl.pallas_call(
        body, name="dn_prep_bwd", grid=(3, cfg.BL),
        in_specs=[pl.BlockSpec((s, dn), lambda j, b: (b, j)), pl.BlockSpec((1, s, dn), lambda j, b: (j, b, 0)),
                  pl.BlockSpec((4, dn), lambda j, b: (0, j))],
        out_specs=[pl.BlockSpec((s, dn), lambda j, b: (b, j)), pl.BlockSpec((4, dn), lambda j, b: (0, j))],
        out_shape=[jax.ShapeDtypeStruct((cfg.T, 3 * dn), f32), jax.ShapeDtypeStruct((4, 3 * dn), f32)],
        compiler_params=_params(("arbitrary", "arbitrary")),
    )(proj, dqkv, conv_w)


def _b_in(cfg, dqkv_pre, dz, dlx, dlg, dba, dr_mid, r_in, g1, wcat):
    d, dn, lw = cfg.D, cfg.DN, cfg.LW

    def body(dq_ref, dz_ref, dlx_ref, dlg_ref, dba_ref, dr_ref, r_ref, g_ref, w_ref, o_ref, dp_ref, dg_ref):
        _zero_at_first(pl.program_id(0) == 0, dg_ref)
        dp_ref[:, 0:cfg.ZO] = dq_ref[...].astype(bf16)
        dp_ref[:, cfg.ZO:cfg.LXO] = dz_ref[...].astype(bf16)
        dp_ref[:, cfg.LXO:cfg.LGO] = dlx_ref[...].astype(bf16)
        dp_ref[:, cfg.LGO:cfg.BAO] = dlg_ref[...].astype(bf16)
        dp_ref[:, cfg.BAO:] = dba_ref[...].astype(bf16)
        g = g_ref[...]
        _, xh, inv = _rms_fwd(r_ref[...], g)
        dx, dg = _rms_bwd(_mm(dp_ref[...], w_ref[...]), xh, inv, g)
        dg_ref[...] = dg_ref[...] + dg
        o_ref[...] = dr_ref[...] + dx

    return _row_call("in_proj_bwd", body, cfg.T, cfg.TM,
                     [(dqkv_pre, 3 * dn, 0), (dz, dn, 0), (dlx, lw, 0), (dlg, lw, 0), (dba, LANES, 0), (dr_mid, d, 0), (r_in, d, 0)],
                     [g1, wcat], [(d, f32), (cfg.PC, bf16)], [((1, d), f32)])


def _adam_call(name, w, g, m, v, tr):
    rows, cols = w.shape
    bc1 = 1.0 - ADAM_B1 ** ADAM_STEP
    bc2 = 1.0 - ADAM_B2 ** ADAM_STEP

    def body(w_ref, g_ref, m_ref, v_ref, d_ref, nm_ref, nv_ref):
        g = g_ref[...]
        m = ADAM_B1 * m_ref[...] + (1.0 - ADAM_B1) * g
        v = ADAM_B2 * v_ref[...] + (1.0 - ADAM_B2) * (g * g)
        nm_ref[...] = m
        nv_ref[...] = v
        d_ref[...] = -ADAM_LR * ((m / bc1) / (jnp.sqrt(v / bc2) + ADAM_EPS) + ADAM_WD * w_ref[...])

    spec = pl.BlockSpec((tr, cols), lambda i: (i, 0))
    return pl.pallas_call(
        body, name=name, grid=(rows // tr,), in_specs=[spec] * 4, out_specs=[spec] * 3,
        out_shape=[jax.ShapeDtypeStruct((rows, cols), f32)] * 3, compiler_params=_params(("arbitrary",)),
    )(w, g, m, v)


def _sum8_call(name, x, tr):
    _, rows, cols = x.shape

    def body(x_ref, o_ref):
        acc = x_ref[0].astype(f32)
        for j in range(1, N_DEV):
            acc = acc + x_ref[j].astype(f32)
        o_ref[...] = acc

    return pl.pallas_call(
        body, name=name, grid=(rows // tr,), in_specs=[pl.BlockSpec((N_DEV, tr, cols), lambda i: (0, i, 0))],
        out_specs=pl.BlockSpec((tr, cols), lambda i: (i, 0)), out_shape=jax.ShapeDtypeStruct((rows, cols), f32),
        compiler_params=_params(("arbitrary",)),
    )(x)


def _all_gather(name, shards):
    na = len(shards)

    def body(*refs):
        xs, outs = refs[:na], refs[na:2 * na]
        send_sems, recv_sems, local_sems = refs[2 * na:]
        x, y, c = lax.axis_index("x"), lax.axis_index("y"), lax.axis_index("c")
        me, sibling = (x, y, c), (x, y, 1 - c)
        chips = [(1 - x, y), (x, 1 - y), (1 - x, 1 - y)]

        def copy(a, k, block, to, src=None):
            px, py, pc = block
            slot = outs[a].at[4 * px + 2 * py + pc]
            return pltpu.make_async_remote_copy(
                src_ref=slot if src is None else src, dst_ref=slot,
                send_sem=send_sems.at[a, k], recv_sem=recv_sems.at[a, k], device_id=to, device_id_type=MESH)

        mine = [pltpu.make_async_copy(xs[a], outs[a].at[4 * x + 2 * y + c], local_sems.at[a]) for a in range(na)]
        for cp in mine:
            cp.start()
        first = []
        for a in range(na):
            first.append(copy(a, 0, me, sibling, src=xs[a]))
            first += [copy(a, 1 + j, me, (*chip, c), src=xs[a]) for j, chip in enumerate(chips)]
        for cp in first:
            cp.start()
        passed = []
        for j, chip in enumerate(chips):
            for a in range(na):
                copy(a, 1 + j, (*chip, c), me).wait_recv()
                cp = copy(a, 4 + j, (*chip, c), sibling)
                cp.start()
                passed.append(cp)
        for a in range(na):
            copy(a, 0, sibling, me).wait_recv()
            for j, chip in enumerate(chips):
                copy(a, 4 + j, (*chip, 1 - c), me).wait_recv()
        for cp in first + passed:
            cp.wait_send()
        for cp in mine:
            cp.wait()

    hbm = pl.BlockSpec(memory_space=pltpu.HBM)
    return pl.pallas_call(
        body, name=name, out_shape=[jax.ShapeDtypeStruct((N_DEV,) + s.shape, s.dtype) for s in shards],
        in_specs=[hbm] * na, out_specs=[hbm] * na,
        scratch_shapes=[pltpu.SemaphoreType.DMA((na, 7)), pltpu.SemaphoreType.DMA((na, 7)), pltpu.SemaphoreType.DMA((na,))],
    )(*shards)


def _all_to_all(name, blocks):
    na = len(blocks)

    def body(*refs):
        xs, outs = refs[:na], refs[na:2 * na]
        send_sems, recv_sems, local_sems = refs[2 * na:]
        x, y, c = lax.axis_index("x"), lax.axis_index("y"), lax.axis_index("c")
        me = 4 * x + 2 * y + c
        mine = [pltpu.make_async_copy(xs[a].at[me], outs[a].at[me], local_sems.at[a]) for a in range(na)]
        for cp in mine:
            cp.start()
        copies = []
        for k in range(1, N_DEV):
            px, py, pc = x ^ (k >> 2), y ^ ((k >> 1) & 1), c ^ (k & 1)
            for a in range(na):
                copies.append(pltpu.make_async_remote_copy(
                    src_ref=xs[a].at[4 * px + 2 * py + pc], dst_ref=outs[a].at[me],
                    send_sem=send_sems.at[a, k - 1], recv_sem=recv_sems.at[a, k - 1], device_id=(px, py, pc),
                    device_id_type=MESH))
        for cp in copies:
            cp.start()
        for cp in copies:
            cp.wait_recv()
        for cp in copies:
            cp.wait_send()
        for cp in mine:
            cp.wait()

    hbm = pl.BlockSpec(memory_space=pltpu.HBM)
    return pl.pallas_call(
        body, name=name, out_shape=[jax.ShapeDtypeStruct(b.shape, b.dtype) for b in blocks],
        in_specs=[hbm] * na, out_specs=[hbm] * na,
        scratch_shapes=[pltpu.SemaphoreType.DMA((na, 7)), pltpu.SemaphoreType.DMA((na, 7)), pltpu.SemaphoreType.DMA((na,))],
    )(*blocks)


def _sum8_layers(name, x):
    _, nl, rows, cols = x.shape

    def body(x_ref, o_ref):
        acc = x_ref[0, 0].astype(f32)
        for j in range(1, N_DEV):
            acc = acc + x_ref[j, 0].astype(f32)
        o_ref[0] = acc

    return pl.pallas_call(
        body, name=name, grid=(nl,), in_specs=[pl.BlockSpec((N_DEV, 1, rows, cols), lambda i: (0, i, 0, 0))],
        out_specs=pl.BlockSpec((1, rows, cols), lambda i: (i, 0, 0)), out_shape=jax.ShapeDtypeStruct((nl, rows, cols), f32),
        compiler_params=_params(("arbitrary",)),
    )(x)


def _layer_fwd(cfg, w, r, p):
    h1, proj = _k_in(cfg, r, w["norm1_g"], w["wcat_t"])
    qkv = _k_prep(cfg, proj, w["dn_conv_w"])
    o, *gdn_saved = _k_gdn_fwd(cfg, qkv, proj, w["alog_row"], w["dt_row"])
    hs = _k_lru_fwd(cfg, proj, w["lru_conv_w"], w["lru_conv_b"], w["wa"], w["wx"], w["lru_ba"], w["lru_bx"], w["lru_lambda"])
    mix, r_mid = _k_mix(cfg, o, proj, hs, r, w["dn_norm_g"], w["lru_norm_g"], w["w_out"])
    h2, gp, up = _k_ffn_a(cfg, r_mid, w["norm2_g"], w["ffn_wg_t"], w["ffn_wu_t"])
    act = _k_ffn_b(cfg, gp, up, w["ffn_conv_w"], w["ffn_conv_b"])
    r2 = _k_ffn_c(cfg, act, r_mid, w["ffn_wd"])
    pn, r3 = _k_ple(cfg, r2, p, w["ple_norm_g"], w["ple_wg"], w["ple_bg"], w["ple_wp_t"])
    saved = dict(r=r, h1=h1, proj=proj, qkv=qkv, o=o, gdn=gdn_saved, hs=hs, mix=mix, r_mid=r_mid, h2=h2, gp=gp, up=up,
                 r2=r2, pn=pn, p=p)
    return r3, saved


def _layer_bwd(cfg, w, sv, dr3):
    g = {}
    dt = min(512, cfg.D)
    dr2, dlog, dpp, g["ple_norm_g"], g["ple_bg"] = _b_ple(cfg, dr3, sv["r2"], sv["p"], w["ple_norm_g"], w["ple_wg"], w["ple_bg"], w["ple_wp_t"])
    g["ple_wg"] = _mm_tn_call(cfg, "d_ple_wg", sv["pn"], dlog, dt)
    g["ple_wp_t"] = _mm_tn_call(cfg, "d_ple_wp", dpp, sv["p"], cfg.PD)
    dgp, dup, g["ffn_wd"], g["ffn_conv_w"], g["ffn_conv_b"] = _b_ffn_bc(cfg, dr2, w["ffn_wd"], sv["gp"], sv["up"], w["ffn_conv_w"], w["ffn_conv_b"])
    dr_mid, g["norm2_g"] = _b_ffn_a(cfg, dgp, dup, dr2, sv["r_mid"], w["norm2_g"], w["ffn_wg_t"], w["ffn_wu_t"])
    g["ffn_wg_t"] = _mm_tn_call(cfg, "d_ffn_wg", dgp, sv["h2"], dt)
    g["ffn_wu_t"] = _mm_tn_call(cfg, "d_ffn_wu", dup, sv["h2"], dt)
    do, dz, dlg, dhs, g["dn_norm_g"], g["lru_norm_g"] = _b_mix(cfg, dr_mid, sv["o"], sv["proj"], sv["hs"], w["dn_norm_g"], w["lru_norm_g"], w["w_out"])
    g["w_out"] = _mm_tn_call(cfg, "d_w_out", sv["mix"], dr_mid, dt)
    dlx, g["lru_conv_w"], g["lru_conv_b"], g["wa"], g["wx"], g["lru_ba"], g["lru_bx"], g["lru_lambda"] = _b_lru(
        cfg, sv["proj"], dhs, w["lru_conv_w"], w["lru_conv_b"], w["wa"], w["wx"], w["lru_ba"], w["lru_bx"], w["lru_lambda"])
    dqkv, dba, g["alog_row"], g["dt_row"] = _b_gdn(cfg, sv["qkv"], sv["proj"], do, w["alog_row"], w["dt_row"], sv["gdn"])
    dqkv_pre, g["dn_conv_w"] = _b_prep(cfg, sv["proj"], dqkv, w["dn_conv_w"])
    dr, dproj, g["norm1_g"] = _b_in(cfg, dqkv_pre, dz, dlx, dlg, dba, dr_mid, sv["r"], w["norm1_g"], w["wcat_t"])
    g["wcat_t"] = _mm_tn_call(cfg, "d_w_in", dproj, sv["h1"], dt)
    return dr, g


BIG = ("w_in", "w_out", "ffn_wg", "ffn_wu", "ffn_wd", "ple_wg", "ple_wp")
BIG_T = {"w_in": True, "w_out": False, "ffn_wg": True, "ffn_wu": True, "ffn_wd": False, "ple_wg": False, "ple_wp": True}
BIG_OPERAND = {"w_in": "wcat_t", "w_out": "w_out", "ffn_wg": "ffn_wg_t", "ffn_wu": "ffn_wu_t", "ffn_wd": "ffn_wd",
               "ple_wg": "ple_wg", "ple_wp": "ple_wp_t"}
SMALL_SHARDED = ("dn_conv_w", "lru_conv_w", "lru_ba", "lru_bx", "lru_lambda", "ffn_conv_w")
SMALL_REPL = ("norm1_g", "dn_a_log", "dn_dt_bias", "dn_norm_g", "lru_conv_b", "lru_wa", "lru_wx", "lru_norm_g", "norm2_g",
              "ffn_conv_b", "ple_norm_g", "ple_bg", "final_g")
WEIGHTS = ("norm1_g", "w_in", "dn_conv_w", "dn_a_log", "dn_dt_bias", "dn_norm_g", "lru_conv_w", "lru_conv_b", "lru_wa",
           "lru_ba", "lru_wx", "lru_bx", "lru_lambda", "lru_norm_g", "w_out", "norm2_g", "ffn_wg", "ffn_wu", "ffn_conv_w",
           "ffn_conv_b", "ffn_wd", "ple_norm_g", "ple_wg", "ple_bg", "ple_wp", "final_g")


def _pad_rows(flat, cols, mult):
    n = flat.shape[0]
    rows = -(-n // cols)
    rows = -(-rows // mult) * mult
    return jnp.pad(flat, (0, rows * cols - n)).reshape(rows, cols)


def _pack(arrs, cols, mult, dtype):
    return _pad_rows(jnp.concatenate([a.reshape(-1).astype(dtype) for a in arrs]), cols, mult)


def _unpack(flat, shapes):
    out, off = [], 0
    for shp in shapes:
        n = math.prod(shp)
        piece = flat[off:off + n]
        if n < 4096:
            piece = lax.optimization_barrier(piece)
        out.append(piece.reshape(shp))
        off += n
    return out


def _unpack8(g8, shapes, axes):
    out, off = [], 0
    for shp, ax in zip(shapes, axes):
        n = math.prod(shp)
        a = g8[:, off:off + n].reshape((N_DEV,) + tuple(shp))
        a = jnp.moveaxis(a, 0, ax)
        out.append(a.reshape(shp[:ax] + (N_DEV * shp[ax],) + shp[ax + 1:]))
        off += n
    return out


def _wcat_t_from_w_in_t(cfg, wt):
    nba = 4 * cfg.H
    pad = jnp.zeros((LANES - nba, wt.shape[1]), wt.dtype)
    return jnp.concatenate([wt[:cfg.LXO], wt[cfg.LXO + nba:], wt[cfg.LXO:cfg.LXO + nba], pad], axis=0)


def _w_in_t_from_wcat_t(cfg, wc):
    nba = 4 * cfg.H
    return jnp.concatenate([wc[:cfg.LXO], wc[cfg.BAO:cfg.BAO + nba], wc[cfg.LXO:cfg.BAO]], axis=0)


def _gate_row(cfg, a):
    h2 = 2 * cfg.H
    return jnp.concatenate([jnp.zeros((1, h2), f32), a.reshape(1, h2), jnp.zeros((1, LANES - 2 * h2), f32)], axis=1)


def _blockdiag(cfg, w):
    w = w.reshape(2, cfg.NB // 2, 2, 64, 64)
    z = jnp.zeros_like(w[:, :, 0])
    top = jnp.concatenate([w[:, :, 0], z], axis=-1)
    bot = jnp.concatenate([z, w[:, :, 1]], axis=-1)
    return jnp.concatenate([top, bot], axis=-2).astype(bf16)


def _unblockdiag(cfg, g):
    a = g[:, :, :64, :64]
    b = g[:, :, 64:, 64:]
    return jnp.stack([a, b], axis=2).reshape(2, cfg.NB, 64, 64)


def _layer_operands(cfg, big, small, i):
    return dict(
        wcat_t=_wcat_t_from_w_in_t(cfg, big["w_in"][i]), w_out=big["w_out"][i], ffn_wg_t=big["ffn_wg"][i],
        ffn_wu_t=big["ffn_wu"][i], ffn_wd=big["ffn_wd"][i], ple_wg=big["ple_wg"][i], ple_wp_t=big["ple_wp"][i],
        norm1_g=small["norm1_g"][i][None], dn_conv_w=small["dn_conv_w"][i], alog_row=_gate_row(cfg, small["dn_a_log"][i]),
        dt_row=_gate_row(cfg, small["dn_dt_bias"][i]), dn_norm_g=small["dn_norm_g"][i][None],
        lru_conv_w=small["lru_conv_w"][i], lru_conv_b=small["lru_conv_b"][i][None],
        wa=_blockdiag(cfg, small["lru_wa"][i]), wx=_blockdiag(cfg, small["lru_wx"][i]),
        lru_ba=small["lru_ba"][i], lru_bx=small["lru_bx"][i], lru_lambda=small["lru_lambda"][i],
        lru_norm_g=small["lru_norm_g"][i][None], norm2_g=small["norm2_g"][i][None], ffn_conv_w=small["ffn_conv_w"][i],
        ffn_conv_b=small["ffn_conv_b"][i][None], ple_norm_g=small["ple_norm_g"][i][None], ple_bg=small["ple_bg"][i][None],
    )


def _small_grads_to_problem(cfg, g):
    h = cfg.H
    return dict(
        norm1_g=g["norm1_g"][0], dn_conv_w=g["dn_conv_w"],
        dn_a_log=g["alog_row"][0, 2 * h:4 * h].reshape(2, h), dn_dt_bias=g["dt_row"][0, 2 * h:4 * h].reshape(2, h),
        dn_norm_g=g["dn_norm_g"][0], lru_conv_w=g["lru_conv_w"], lru_conv_b=g["lru_conv_b"][0],
        lru_wa=_unblockdiag(cfg, g["wa"]), lru_wx=_unblockdiag(cfg, g["wx"]), lru_ba=g["lru_ba"], lru_bx=g["lru_bx"],
        lru_lambda=g["lru_lambda"], lru_norm_g=g["lru_norm_g"][0], norm2_g=g["norm2_g"][0], ffn_conv_w=g["ffn_conv_w"],
        ffn_conv_b=g["ffn_conv_b"][0], ple_norm_g=g["ple_norm_g"][0], ple_bg=g["ple_bg"][0],
    )


def _local_step(cfg, big, small, x, p, target):
    r = x.reshape(cfg.T, cfg.D)
    ops, saved = [], []
    for i in range(cfg.L):
        w = _layer_operands(cfg, big, small, i)
        r, sv = _layer_fwd(cfg, w, r, p[i].reshape(cfg.T, cfg.PD))
        ops.append(w)
        saved.append(sv)
    dr, loss, dgf = _k_loss(cfg, r, target.reshape(cfg.T, cfg.D), small["final_g"][None])
    gbig, gsmall = [None] * cfg.L, [None] * cfg.L
    for i in reversed(range(cfg.L)):
        dr, g = _layer_bwd(cfg, ops[i], saved[i], dr)
        gbig[i] = {n: (_w_in_t_from_wcat_t(cfg, g["wcat_t"]) if n == "w_in" else g[BIG_OPERAND[n]]) for n in BIG}
        gsmall[i] = _small_grads_to_problem(cfg, g)
    gs = {k: jnp.stack([gl[k] for gl in gsmall]) for k in gsmall[0]}
    gs["final_g"] = dgf[0]
    return loss, dr, gbig, gs


def _row_tile(rows, limit=512):
    best = rows
    for t in range(8, min(rows, limit) + 1, 8):
        if rows % t == 0:
            best = t
    return best if best <= limit or rows <= limit else rows


def _adam_group(name, ws, gs, ms, vs, cols, tr):
    shapes = [w.shape for w in ws]
    pk = lambda arrs: _pack(arrs, cols, tr, f32)
    w2 = pk(ws)
    d, nm, nv = _adam_call(name, w2, pk(gs), pk(ms), pk(vs), min(tr, w2.shape[0]))
    return [_unpack(a.reshape(-1), shapes) for a in (d, nm, nv)]


def kernel(x, p, norm1_g, w_in, dn_conv_w, dn_a_log, dn_dt_bias, dn_norm_g, lru_conv_w, lru_conv_b, lru_wa, lru_ba, lru_wx, lru_bx, lru_lambda, lru_norm_g, w_out, norm2_g, ffn_wg, ffn_wu, ffn_conv_w, ffn_conv_b, ffn_wd, ple_norm_g, ple_wg, ple_bg, ple_wp, final_g, loss_target, m_norm1_g, m_w_in, m_dn_conv_w, m_dn_a_log, m_dn_dt_bias, m_dn_norm_g, m_lru_conv_w, m_lru_conv_b, m_lru_wa, m_lru_ba, m_lru_wx, m_lru_bx, m_lru_lambda, m_lru_norm_g, m_w_out, m_norm2_g, m_ffn_wg, m_ffn_wu, m_ffn_conv_w, m_ffn_conv_b, m_ffn_wd, m_ple_norm_g, m_ple_wg, m_ple_bg, m_ple_wp, m_final_g, v_norm1_g, v_w_in, v_dn_conv_w, v_dn_a_log, v_dn_dt_bias, v_dn_norm_g, v_lru_conv_w, v_lru_conv_b, v_lru_wa, v_lru_ba, v_lru_wx, v_lru_bx, v_lru_lambda, v_lru_norm_g, v_w_out, v_norm2_g, v_ffn_wg, v_ffn_wu, v_ffn_conv_w, v_ffn_conv_b, v_ffn_wd, v_ple_norm_g, v_ple_wg, v_ple_bg, v_ple_wp, v_final_g):
    cfg = CFG
    a = dict(locals())
    wl = {n: a[n] for n in WEIGHTS}
    ml = {n: a["m_" + n] for n in WEIGHTS}
    vl = {n: a["v_" + n] for n in WEIGHTS}
    me = 4 * lax.axis_index("x") + 2 * lax.axis_index("y") + lax.axis_index("c")
    nl = cfg.L

    blocks = [(jnp.swapaxes(wl[n], 1, 2) if BIG_T[n] else wl[n]).astype(bf16) for n in BIG]
    ss_shapes = [wl[n].shape for n in SMALL_SHARDED]
    *g8, s8 = _all_gather("gather_weights", blocks + [_pack([wl[n] for n in SMALL_SHARDED], LANES, 8, f32)])
    big = {n: jnp.moveaxis(g, 0, 1).reshape(nl, N_DEV * g.shape[2], g.shape[3]) for n, g in zip(BIG, g8)}
    small = dict(zip(SMALL_SHARDED, _unpack8(s8.reshape(N_DEV, -1), ss_shapes, [2] * len(ss_shapes))))
    small.update({n: wl[n] for n in SMALL_REPL})

    loss_part, dr, gbig, gsmall = _local_step(cfg, big, small, x, p, loss_target)
    grad_x = dr.reshape(x.shape)

    send = []
    for n, blk in zip(BIG, blocks):
        g = jnp.stack([gbig[i][n].reshape((N_DEV,) + blk.shape[1:]) for i in range(nl)], axis=1)
        send.append(g.astype(bf16))
    recv = _all_to_all("exchange_grads", send)
    gl = {}
    for n, r8 in zip(BIG, recv):
        s = _sum8_layers("sum_" + n, r8)
        gl[n] = jnp.swapaxes(s, 1, 2) if BIG_T[n] else s

    small_names = SMALL_REPL + SMALL_SHARDED
    small_shapes = [gsmall[n].shape for n in small_names]
    sv = _pack([gsmall[n] for n in small_names] + [loss_part[0, 0:1]], LANES, 512, f32)
    small_sum = _sum8_call("sum_small", _all_gather("gather_small_grads", [sv])[0], 512).reshape(-1)
    gl.update(zip(small_names, _unpack(small_sum, small_shapes)))
    loss = small_sum[sum(math.prod(s) for s in small_shapes)]
    for n in SMALL_SHARDED:
        shard = wl[n].shape[2]
        gl[n] = lax.dynamic_slice_in_dim(gl[n], me * shard, shard, axis=2)

    outs = {}
    for n in BIG:
        shp = wl[n].shape
        two = lambda t: t.reshape(-1, shp[-1])
        d, nm, nv = _adam_call("adam_" + n, two(wl[n]), two(gl[n]), two(ml[n]), two(vl[n]), _row_tile(math.prod(shp[:-1])))
        outs[n] = (d.reshape(shp), nm.reshape(shp), nv.reshape(shp))
    d, nm, nv = _adam_group("adam_small", [wl[n] for n in small_names], [gl[n] for n in small_names],
                            [ml[n] for n in small_names], [vl[n] for n in small_names], LANES, 64)
    for j, n in enumerate(small_names):
        outs[n] = (d[j], nm[j], nv[j])
    return (loss, grad_x, *[gl[n] for n in WEIGHTS], *[outs[n][0] for n in WEIGHTS], *[outs[n][1] for n in WEIGHTS],
            *[outs[n][2] for n in WEIGHTS])
```

```python
import functools
import math

import jax
import jax.numpy as jnp
from jax import lax
from jax.experimental import pallas as pl
from jax.experimental.pallas import tpu as pltpu

f32 = jnp.float32
bf16 = jnp.bfloat16
MESH = pl.DeviceIdType.MESH

N_DEV = 8
LANES = 128
EPS = 1e-6
LRU_C = 8.0
ADAM_LR, ADAM_B1, ADAM_B2, ADAM_EPS, ADAM_WD, ADAM_STEP = 0.001, 0.9, 0.999, 1e-08, 0.01, 10
VMEM_LIMIT = 56 * 1024 * 1024


class Cfg:
    def __init__(self, d_model=1024, bl=4, seq=2048, depth=4, heads=4, lru_width=512, d_ff=2816, ple=256,
                 tm=512, tm_ffn=256, ff_tile=256):
        self.D, self.BL, self.S, self.L, self.H = d_model, bl, seq, depth, heads
        self.DH = 128
        self.DN = heads * self.DH
        self.LW = lru_width
        self.NB = lru_width // 64
        self.FF, self.PD = d_ff, ple
        self.C = 64
        self.NC = seq // self.C
        self.T = bl * seq
        self.TM = min(tm, self.T)
        self.TMF = min(tm_ffn, self.T)
        self.FT = ff_tile
        self.ZO = 3 * self.DN
        self.LXO = 4 * self.DN
        self.LGO = self.LXO + self.LW
        self.BAO = self.LGO + self.LW
        self.PC = self.BAO + LANES
        self.IN_COLS = 4 * self.DN + 4 * heads + 2 * self.LW


CFG = Cfg()


def _mm(a, b):
    return jnp.dot(a.astype(bf16), b.astype(bf16), preferred_element_type=f32)


def _mm_nt(a, b):
    return lax.dot_general(a.astype(bf16), b.astype(bf16), (((1,), (1,)), ((), ())), preferred_element_type=f32)


def _mm_tn(a, b):
    return lax.dot_general(a.astype(bf16), b.astype(bf16), (((0,), (0,)), ((), ())), preferred_element_type=f32)


def _split2(a):
    hi = a.astype(bf16)
    return hi, (a - hi.astype(f32)).astype(bf16)


def _hmm3(a, b, dims=(((1,), (0,)), ((), ()))):
    ah, al = _split2(a)
    bh, bl = _split2(b)
    dot = functools.partial(lax.dot_general, dimension_numbers=dims, preferred_element_type=f32)
    return dot(ah, bh) + dot(ah, bl) + dot(al, bh)


def _hmm3_tn(a, b):
    return _hmm3(a, b, (((0,), (0,)), ((), ())))


def _cum_mm(mask, x, dims=(((1,), (0,)), ((), ()))):
    m = mask.astype(bf16)
    x1 = x.astype(bf16)
    r = x - x1.astype(f32)
    x2 = r.astype(bf16)
    x3 = (r - x2.astype(f32)).astype(bf16)
    dot = functools.partial(lax.dot_general, dimension_numbers=dims, preferred_element_type=f32)
    return dot(m, x1) + dot(m, x2) + dot(m, x3)


def _rms_fwd(x, g):
    inv = lax.rsqrt(jnp.mean(x * x, axis=-1, keepdims=True) + EPS)
    xh = x * inv
    return xh * g, xh, inv


def _rms_bwd(dy, xh, inv, g):
    dxh = dy * g
    dx = inv * (dxh - xh * jnp.mean(dxh * xh, axis=-1, keepdims=True))
    dg = jnp.sum(dy * xh, axis=0, keepdims=True)
    return dx, dg


def _sigmoid(x):
    return 1.0 / (1.0 + jnp.exp(-x))


def _softplus(x):
    return jnp.maximum(x, 0.0) + jnp.log(1.0 + jnp.exp(-jnp.abs(x)))


def _silu(x):
    s = _sigmoid(x)
    return x * s, s * (1.0 + x * (1.0 - s))


_GC = math.sqrt(2.0 / math.pi)


def _gelu(x):
    t = jnp.tanh(_GC * (x + 0.044715 * x * x * x))
    y = 0.5 * x * (1.0 + t)
    dy = 0.5 * (1.0 + t) + 0.5 * x * (1.0 - t * t) * _GC * (1.0 + 3.0 * 0.044715 * x * x)
    return y, dy


def _nexpm1(x):
    ser = -x * (1.0 + x * 0.5 * (1.0 + x * (1.0 / 3.0) * (1.0 + x * 0.25 * (1.0 + x * 0.2))))
    return jnp.where(x > -0.1, ser, 1.0 - jnp.exp(x))


def _shift(x, s, fill=0.0):
    if s == 0:
        return x
    n = x.shape[0]
    t = lax.broadcasted_iota(jnp.int32, x.shape, 0)
    r = pltpu.roll(x, (-s) % n, 0)
    return jnp.where((t + s >= 0) & (t + s < n), r, fill)


def _conv_fwd(x, w_ref, left):
    k = w_ref.shape[0]
    out = _shift(x, -left) * w_ref[0:1, :]
    for j in range(1, k):
        out = out + _shift(x, j - left) * w_ref[j:j + 1, :]
    return out


def _conv_bwd(dout, x, w_ref, left):
    k = w_ref.shape[0]
    dx = None
    dws = []
    for j in range(k):
        term = _shift(dout, -(j - left)) * w_ref[j:j + 1, :]
        dx = term if dx is None else dx + term
        dws.append(jnp.sum(dout * _shift(x, j - left), axis=0, keepdims=True))
    return dx, dws


def _lin_scan(a, b, rev):
    n = a.shape[0]
    d = 1
    while d < n:
        s = d if rev else -d
        b = a * _shift(b, s, 0.0) + b
        a = a * _shift(a, s, 1.0)
        d *= 2
    return b


def _tri_masks(c, rev):
    i = lax.broadcasted_iota(jnp.int32, (c, c), 0)
    j = lax.broadcasted_iota(jnp.int32, (c, c), 1)
    incl = (i <= j) if rev else (i >= j)
    strict = (i < j) if rev else (i > j)
    incl_t = (i >= j) if rev else (i <= j)
    return incl, strict, incl_t


def _map(f, *lists):
    return [f(*a) for a in zip(*lists)]


def _tri_inv(mats):
    c = mats[0].shape[0]
    i = lax.broadcasted_iota(jnp.int32, (c, c), 0)
    j = lax.broadcasted_iota(jnp.int32, (c, c), 1)
    eye = jnp.where(i == j, 1.0, 0.0)
    t = [eye - a for a in mats]
    pw = _map(_hmm3, mats, mats)
    for it in range(5):
        t = _map(lambda ti, ui: ti + ui, t, _map(_hmm3, t, pw))
        if it < 4:
            pw = _map(_hmm3, pw, pw)
    return t


def _gdn_decay(gs, revs):
    c = gs[0].shape[0]
    masks = [_tri_masks(c, r) for r in revs]
    gb = [jnp.broadcast_to(g, (c, c)) for g in gs]
    mcol = _map(lambda m, x: _cum_mm(m[0], x), masks, gb)
    mrow = _map(lambda m, x: jnp.sum(jnp.where(m[2], x, 0.0), axis=0, keepdims=True), masks, gb)
    dec = _map(lambda m, a, b: jnp.exp(jnp.where(m[0], a - b, -1e30)), masks, mcol, mrow)
    return [m[:, 0:1] for m in mcol], [jnp.sum(g, axis=0, keepdims=True) for g in gs], dec


def _gdn_prep(qs, k, v, g, beta, kk, qk, revs):
    c = k[0].shape[0]
    masks = [_tri_masks(c, r) for r in revs]
    gcum, glast, dec = _gdn_decay(g, revs)
    e = [jnp.exp(x) for x in gcum]
    tm = _tri_inv(_map(lambda m, b, x, d: jnp.where(m[1], b * x * d, 0.0), masks, beta, kk, dec))
    u = _map(lambda t, vi, b: _hmm3(t, vi * b), tm, v, beta)
    w = _map(lambda t, ki, b, ei: _hmm3(t, ki * b * ei), tm, k, beta, e)
    p = _map(lambda m, x, d: jnp.where(m[0], x * d, 0.0), masks, qk, dec)
    return dict(dec=dec, cd=[jnp.exp(x) for x in glast], tm=tm, u=u, w=w, p=p, qd=_map(lambda a, b: a * b, qs, e),
                kd=_map(lambda ki, gl, gc: ki * jnp.exp(gl - gc), k, glast, gcum))


def _lane_pick(x, lane):
    l = lax.broadcasted_iota(jnp.int32, x.shape, 1)
    return jnp.sum(jnp.where(l == lane, x, 0.0), axis=1, keepdims=True)


def _params(sem, vmem=VMEM_LIMIT):
    return pltpu.CompilerParams(dimension_semantics=sem, vmem_limit_bytes=vmem)


def _row_call(name, body, t, tm, row_ins, full_ins, row_outs, acc_outs):
    in_specs = [pl.BlockSpec((tm, w), functools.partial(lambda i, c: (i, c), c=c)) for (_, w, c) in row_ins]
    for a in full_ins:
        in_specs.append(pl.BlockSpec(a.shape, functools.partial(lambda i, n: (0,) * n, n=a.ndim)))
    out_specs = [pl.BlockSpec((tm, w), lambda i: (i, 0)) for (w, _) in row_outs]
    out_shape = [jax.ShapeDtypeStruct((t, w), dt) for (w, dt) in row_outs]
    for shp, dt in acc_outs:
        out_specs.append(pl.BlockSpec(shp, functools.partial(lambda i, n: (0,) * n, n=len(shp))))
        out_shape.append(jax.ShapeDtypeStruct(shp, dt))
    return pl.pallas_call(
        body, name=name, grid=(t // tm,), in_specs=in_specs, out_specs=out_specs, out_shape=out_shape,
        compiler_params=_params(("arbitrary",)),
    )(*[a for (a, _, _) in row_ins], *full_ins)


def _k_in(cfg, r, g1, wcat):
    def body(r_ref, g_ref, w_ref, h_ref, proj_ref):
        y, _, _ = _rms_fwd(r_ref[...], g_ref[...])
        hb = y.astype(bf16)
        h_ref[...] = hb
        proj_ref[...] = _mm_nt(hb, w_ref[...])

    return _row_call("in_proj", body, cfg.T, cfg.TM, [(r, cfg.D, 0)], [g1, wcat],
                     [(cfg.D, bf16), (cfg.PC, f32)], [])


def _k_prep(cfg, proj, conv_w):
    dn, s = cfg.DN, cfg.S

    def body(x_ref, w_ref, o_ref):
        sec = pl.program_id(1)
        c = _conv_fwd(x_ref[...], w_ref, 2)
        y, _ = _silu(c)
        for h in range(cfg.H):
            yh = y[:, h * 128:(h + 1) * 128]
            nh = yh * lax.rsqrt(jnp.sum(yh * yh, axis=1, keepdims=True) + EPS)
            o_ref[:, h * 128:(h + 1) * 128] = jnp.where(sec < 2, nh, yh)

    return pl.pallas_call(
        body, name="dn_prep", grid=(cfg.BL, 3),
        in_specs=[pl.BlockSpec((s, dn), lambda b, j: (b, j)), pl.BlockSpec((4, dn), lambda b, j: (0, j))],
        out_specs=pl.BlockSpec((s, dn), lambda b, j: (b, j)),
        out_shape=jax.ShapeDtypeStruct((cfg.T, 3 * dn), f32),
        compiler_params=_params(("arbitrary", "arbitrary")),
    )(proj, conv_w)


def _gate_cols(ba, alog_row, dt_row, lane_b, lane_a):
    beta = _sigmoid(_lane_pick(ba, lane_b))
    alpha = _lane_pick(ba, lane_a)
    aexp = jnp.exp(_lane_pick(alog_row, lane_a))
    dtb = _lane_pick(dt_row, lane_a)
    xa = alpha + dtb
    g = -aexp * _softplus(xa)
    return beta, g, aexp, xa


def _gdn_specs(cfg):
    s, c, nc = cfg.S, cfg.C, cfg.NC
    tok2 = pl.BlockSpec((2, s, 128), lambda b, h: (0, b, h))
    mat2 = pl.BlockSpec((2, 1, nc, c, c), lambda b, h: (0, b, h, 0, 0))
    cd2 = pl.BlockSpec((2, 1, nc, 8, LANES), lambda b, h: (0, b, h, 0, 0))
    shapes = dict(
        tok32=jax.ShapeDtypeStruct((2, cfg.T, cfg.DN), f32), tok16=jax.ShapeDtypeStruct((2, cfg.T, cfg.DN), bf16),
        mat32=jax.ShapeDtypeStruct((2, cfg.BL, cfg.H * nc, c, c), f32), mat16=jax.ShapeDtypeStruct((2, cfg.BL, cfg.H * nc, c, c), bf16),
        cd=jax.ShapeDtypeStruct((2, cfg.BL, cfg.H * nc, 8, LANES), f32))
    return tok2, mat2, cd2, shapes


def _k_gdn_fwd(cfg, qkv, proj, alog_row, dt_row):
    s, c, nc, hh = cfg.S, cfg.C, cfg.NC, cfg.H
    ba_blk = cfg.BAO // LANES
    per = 4 if nc % 4 == 0 else 1

    def body(q_ref, k_ref, v_ref, ba_ref, al_ref, dt_ref, o_ref, u_o, w_o, qd_o, kd_o, p_o, t_o, dec_o, cd_o):
        h = pl.program_id(1)

        def prep(i, carry):
            chains, qs, k, v, kk, qk, g, beta = [], [], [], [], [], [], [], []
            for j in range(per):
                n = i * per + j
                rows = pl.ds(pl.multiple_of(n * c, c), c)
                kj, vj = k_ref[rows, :], v_ref[rows, :]
                qj = q_ref[rows, :] * (128 ** -0.5)
                kkj, qkj = _mm_nt(kj, kj), _mm_nt(qj, kj)
                ba = ba_ref[rows, :]
                for d in range(2):
                    bd, gd, _, _ = _gate_cols(ba, al_ref[...], dt_ref[...], d * hh + h, 2 * hh + d * hh + h)
                    chains.append((d, n, rows))
                    for lst, val in ((qs, qj), (k, kj), (v, vj), (kk, kkj), (qk, qkj), (g, gd), (beta, bd)):
                        lst.append(val)
            z = _gdn_prep(qs, k, v, g, beta, kk, qk, [d == 1 for d, _, _ in chains])
            for x, (d, n, rows) in enumerate(chains):
                u_o[d, rows, :] = z["u"][x]
                w_o[d, rows, :] = z["w"][x].astype(bf16)
                qd_o[d, rows, :] = z["qd"][x].astype(bf16)
                kd_o[d, rows, :] = z["kd"][x].astype(bf16)
                p_o[d, 0, n] = z["p"][x].astype(bf16)
                t_o[d, 0, n] = z["tm"][x]
                dec_o[d, 0, n] = z["dec"][x]
                cd_o[d, 0, n] = jnp.broadcast_to(z["cd"][x], (8, LANES))
            return carry

        lax.fori_loop(0, nc // per, prep, 0)
        o_ref[...] = jnp.zeros_like(o_ref)

        def step(i, sts):
            ns = [i, nc - 1 - i]
            rows = [pl.ds(pl.multiple_of(n * c, c), c) for n in ns]
            vn = [u_o[d, rows[d], :] - _mm(w_o[d, rows[d], :], sts[d]) for d in range(2)]
            o = [_mm(qd_o[d, rows[d], :], sts[d]) + _mm(p_o[d, 0, ns[d]], vn[d]) for d in range(2)]
            new = [sts[d] * cd_o[d, 0, ns[d]][0:1, :] + _mm_tn(kd_o[d, rows[d], :], vn[d]) for d in range(2)]
            for d in range(2):
                o_ref[rows[d], :] = o_ref[rows[d], :] + o[d]
            return tuple(new)

        z0 = jnp.zeros((128, 128), f32)
        lax.fori_loop(0, nc, step, (z0, z0))

    blk = lambda off: pl.BlockSpec((s, 128), functools.partial(lambda b, h, off: (b, off + h), off=off))
    row = pl.BlockSpec((1, LANES), lambda b, h: (0, 0))
    tok2, mat2, cd2, shp = _gdn_specs(cfg)
    return pl.pallas_call(
        body, name="gdn_fwd", grid=(cfg.BL, hh),
        in_specs=[blk(0), blk(hh), blk(2 * hh), pl.BlockSpec((s, LANES), lambda b, h: (b, ba_blk)), row, row],
        out_specs=[pl.BlockSpec((s, 128), lambda b, h: (b, h)), tok2, tok2, tok2, tok2, mat2, mat2, mat2, cd2],
        out_shape=[jax.ShapeDtypeStruct((cfg.T, cfg.DN), f32), shp["tok32"], shp["tok16"], shp["tok16"], shp["tok16"],
                   shp["mat16"], shp["mat32"], shp["mat32"], shp["cd"]],
        compiler_params=_params(("arbitrary", "arbitrary")),
    )(qkv, qkv, qkv, proj, alog_row, dt_row)


def _lru_dir_fwd(xc, wa, wx, ba, bx, lam, rev):
    ra = _sigmoid(_mm(xc, wa) + ba)
    ig = _sigmoid(_mm(xc, wx) + bx)
    sp = _softplus(-lam)
    la = -LRU_C * ra * sp
    a = jnp.exp(la)
    m = jnp.sqrt(_nexpm1(2.0 * la))
    gx = ig * xc
    h = _lin_scan(a, m * gx, rev)
    return dict(ra=ra, ig=ig, sp=sp, a=a, m=m, gx=gx, h=h)


def _lru_specs(cfg, outer_b):
    s = cfg.S
    lx_blk = cfg.LXO // LANES
    if outer_b:
        ix = lambda f: (lambda b, ct: f(b, ct))
    else:
        ix = lambda f: (lambda ct, b: f(b, ct))
    return dict(
        lx=pl.BlockSpec((s, LANES), ix(lambda b, ct: (b, lx_blk + ct))),
        tok=pl.BlockSpec((s, LANES), ix(lambda b, ct: (b, ct))),
        cw=pl.BlockSpec((4, LANES), ix(lambda b, ct: (0, ct))),
        row=pl.BlockSpec((1, LANES), ix(lambda b, ct: (0, ct))),
        w=pl.BlockSpec((2, 1, LANES, LANES), ix(lambda b, ct: (0, ct, 0, 0))),
        two=pl.BlockSpec((2, LANES), ix(lambda b, ct: (0, ct))),
    )


def _k_lru_fwd(cfg, proj, conv_w, conv_b, wa, wx, ba, bx, lam):
    def body(lx_ref, cw_ref, cb_ref, wa_ref, wx_ref, ba_ref, bx_ref, lam_ref, o_ref):
        xc = _conv_fwd(lx_ref[...], cw_ref, 2) + cb_ref[...]
        acc = None
        for d in range(2):
            z = _lru_dir_fwd(xc, wa_ref[d, 0], wx_ref[d, 0], ba_ref[d:d + 1, :], bx_ref[d:d + 1, :], lam_ref[d:d + 1, :], d == 1)
            acc = z["h"] if acc is None else acc + z["h"]
        o_ref[...] = acc

    sp = _lru_specs(cfg, True)
    return pl.pallas_call(
        body, name="lru_fwd", grid=(cfg.BL, cfg.LW // LANES),
        in_specs=[sp["lx"], sp["cw"], sp["row"], sp["w"], sp["w"], sp["two"], sp["two"], sp["two"]],
        out_specs=sp["tok"], out_shape=jax.ShapeDtypeStruct((cfg.T, cfg.LW), f32),
        compiler_params=_params(("arbitrary", "arbitrary")),
    )(proj, conv_w, conv_b, wa, wx, ba, bx, lam)


def _mix_parts(cfg, o, z, lg, hs, dng, lrg):
    heads = []
    for h in range(cfg.H):
        sl = slice(h * 128, (h + 1) * 128)
        y, xh, inv = _rms_fwd(o[:, sl], dng)
        sz, dsz = _silu(z[:, sl])
        heads.append((y, xh, inv, sz, dsz))
    gl, dgl = _gelu(lg)
    y2, xh2, inv2 = _rms_fwd(gl * hs, lrg)
    return heads, (y2, xh2, inv2, gl, dgl)


def _k_mix(cfg, o, proj, hs, r, dng, lrg, wout):
    dn = cfg.DN

    def body(o_ref, z_ref, lg_ref, hs_ref, r_ref, dng_ref, lrg_ref, w_ref, mix_ref, out_ref):
        heads, lru = _mix_parts(cfg, o_ref[...], z_ref[...], lg_ref[...], hs_ref[...], dng_ref[...], lrg_ref[...])
        for h, (y, _, _, sz, _) in enumerate(heads):
            mix_ref[:, h * 128:(h + 1) * 128] = (y * sz).astype(bf16)
        mix_ref[:, dn:] = lru[0].astype(bf16)
        out_ref[...] = r_ref[...] + jnp.dot(mix_ref[...], w_ref[...], preferred_element_type=f32)

    return _row_call("mix_out", body, cfg.T, cfg.TM,
                     [(o, dn, 0), (proj, dn, cfg.ZO // dn), (proj, cfg.LW, cfg.LGO // cfg.LW), (hs, cfg.LW, 0), (r, cfg.D, 0)],
                     [dng, lrg, wout], [(cfg.D, bf16), (cfg.D, f32)], [])


def _k_ffn_a(cfg, r, g2, wg, wu):
    def body(r_ref, g_ref, wg_ref, wu_ref, h_ref, gp_ref, up_ref):
        y, _, _ = _rms_fwd(r_ref[...], g_ref[...])
        hb = y.astype(bf16)
        h_ref[...] = hb
        gp_ref[...] = _mm_nt(hb, wg_ref[...])
        up_ref[...] = _mm_nt(hb, wu_ref[...])

    return _row_call("ffn_in", body, cfg.T, cfg.TMF, [(r, cfg.D, 0)], [g2, wg, wu],
                     [(cfg.D, bf16), (cfg.FF, f32), (cfg.FF, f32)], [])


def _k_ffn_b(cfg, gp, up, conv_w, conv_b):
    s, ft = cfg.S, cfg.FT

    def body(gp_ref, up_ref, w_ref, b_ref, o_ref):
        gate = _conv_fwd(gp_ref[...], w_ref, 1) + b_ref[...]
        gl, _ = _gelu(gate)
        o_ref[...] = (gl * up_ref[...]).astype(bf16)

    tok = pl.BlockSpec((s, ft), lambda b, j: (b, j))
    return pl.pallas_call(
        body, name="ffn_act", grid=(cfg.BL, cfg.FF // ft),
        in_specs=[tok, tok, pl.BlockSpec((3, ft), lambda b, j: (0, j)), pl.BlockSpec((1, ft), lambda b, j: (0, j))],
        out_specs=tok, out_shape=jax.ShapeDtypeStruct((cfg.T, cfg.FF), bf16),
        compiler_params=_params(("arbitrary", "arbitrary")),
    )(gp, up, conv_w, conv_b)


def _k_ffn_c(cfg, act, r, wd):
    def body(a_ref, r_ref, w_ref, o_ref):
        o_ref[...] = r_ref[...] + jnp.dot(a_ref[...], w_ref[...], preferred_element_type=f32)

    return _row_call("ffn_out", body, cfg.T, cfg.TM, [(act, cfg.FF, 0), (r, cfg.D, 0)], [wd], [(cfg.D, f32)], [])[0]


def _k_ple(cfg, r, p, gp, wpg, bg, wpp):
    def body(r_ref, p_ref, g_ref, wg_ref, bg_ref, wp_ref, pn_ref, o_ref):
        x = r_ref[...]
        y, _, _ = _rms_fwd(x, g_ref[...])
        pn = y.astype(bf16)
        pn_ref[...] = pn
        pg = _sigmoid(jnp.dot(pn, wg_ref[...], preferred_element_type=f32) + bg_ref[...])
        o_ref[...] = x + pg * _mm_nt(p_ref[...], wp_ref[...])

    return _row_call("ple", body, cfg.T, cfg.TM, [(r, cfg.D, 0), (p, cfg.PD, 0)], [gp, wpg, bg, wpp],
                     [(cfg.D, bf16), (cfg.D, f32)], [])


def _k_loss(cfg, r, tgt, gf):
    d = cfg.D

    def body(r_ref, t_ref, g_ref, dr_ref, loss_ref, dg_ref):
        @pl.when(pl.program_id(0) == 0)
        def _():
            loss_ref[...] = jnp.zeros_like(loss_ref)
            dg_ref[...] = jnp.zeros_like(dg_ref)

        g = g_ref[...]
        y, xh, inv = _rms_fwd(r_ref[...], g)
        err = y - t_ref[...]
        loss_ref[...] = loss_ref[...] + (0.5 / d) * jnp.sum(err * err)
        dx, dg = _rms_bwd(err * (1.0 / d), xh, inv, g)
        dr_ref[...] = dx
        dg_ref[...] = dg_ref[...] + dg

    return _row_call("loss_head", body, cfg.T, cfg.TM, [(r, d, 0), (tgt, d, 0)], [gf], [(d, f32)],
                     [((1, LANES), f32), ((1, d), f32)])


def _zero_at_first(cond, *refs):
    @pl.when(cond)
    def _():
        for r in refs:
            r[...] = jnp.zeros_like(r)


def _b_ple(cfg, dr3, r2, p, gp, wpg, bg, wpp):
    d = cfg.D

    def body(dr_ref, r_ref, p_ref, g_ref, wg_ref, bg_ref, wp_ref, dr2_ref, dlog_ref, dpp_ref, dgp_ref, dbg_ref):
        _zero_at_first(pl.program_id(0) == 0, dgp_ref, dbg_ref)
        g = g_ref[...]
        dr = dr_ref[...]
        y, xh, inv = _rms_fwd(r_ref[...], g)
        pg = _sigmoid(jnp.dot(y.astype(bf16), wg_ref[...], preferred_element_type=f32) + bg_ref[...])
        pp = _mm_nt(p_ref[...], wp_ref[...])
        dpp_ref[...] = (dr * pg).astype(bf16)
        dlog = dr * pp * pg * (1.0 - pg)
        dlog_ref[...] = dlog.astype(bf16)
        dbg_ref[...] = dbg_ref[...] + jnp.sum(dlog, axis=0, keepdims=True)
        dx, dg = _rms_bwd(_mm_nt(dlog, wg_ref[...]), xh, inv, g)
        dgp_ref[...] = dgp_ref[...] + dg
        dr2_ref[...] = dr + dx

    return _row_call("ple_bwd", body, cfg.T, cfg.TM, [(dr3, d, 0), (r2, d, 0), (p, cfg.PD, 0)], [gp, wpg, bg, wpp],
                     [(d, f32), (d, bf16), (d, bf16)], [((1, d), f32), ((1, d), f32)])


def _b_ffn_bc(cfg, dr2, wd, gp, up, conv_w, conv_b):
    s, ft, d = cfg.S, cfg.FT, cfg.D

    def body(dr_ref, wd_ref, gp_ref, up_ref, w_ref, b_ref, dgp_ref, dup_ref, dwd_ref, dcw_ref, dcb_ref):
        _zero_at_first(pl.program_id(1) == 0, dwd_ref, dcw_ref, dcb_ref)
        drb = dr_ref[...].astype(bf16)
        dact = _mm_nt(drb, wd_ref[...])
        gpre = gp_ref[...]
        up = up_ref[...]
        gl, dgl = _gelu(_conv_fwd(gpre, w_ref, 1) + b_ref[...])
        dup_ref[...] = (dact * gl).astype(bf16)
        dgate = dact * up * dgl
        dcb_ref[...] = dcb_ref[...] + jnp.sum(dgate, axis=0, keepdims=True)
        dx, dws = _conv_bwd(dgate, gpre, w_ref, 1)
        for j, dw in enumerate(dws):
            dcw_ref[j:j + 1, :] = dcw_ref[j:j + 1, :] + dw
        dgp_ref[...] = dx.astype(bf16)
        dwd_ref[...] = dwd_ref[...] + _mm_tn((gl * up).astype(bf16), drb)

    tok = pl.BlockSpec((s, ft), lambda j, b: (b, j))
    return pl.pallas_call(
        body, name="ffn_act_bwd", grid=(cfg.FF // ft, cfg.BL),
        in_specs=[pl.BlockSpec((s, d), lambda j, b: (b, 0)), pl.BlockSpec((ft, d), lambda j, b: (j, 0)), tok, tok,
                  pl.BlockSpec((3, ft), lambda j, b: (0, j)), pl.BlockSpec((1, ft), lambda j, b: (0, j))],
        out_specs=[tok, tok, pl.BlockSpec((ft, d), lambda j, b: (j, 0)), pl.BlockSpec((3, ft), lambda j, b: (0, j)),
                   pl.BlockSpec((1, ft), lambda j, b: (0, j))],
        out_shape=[jax.ShapeDtypeStruct((cfg.T, cfg.FF), bf16), jax.ShapeDtypeStruct((cfg.T, cfg.FF), bf16),
                   jax.ShapeDtypeStruct((cfg.FF, d), f32), jax.ShapeDtypeStruct((3, cfg.FF), f32),
                   jax.ShapeDtypeStruct((1, cfg.FF), f32)],
        compiler_params=_params(("arbitrary", "arbitrary")),
    )(dr2, wd, gp, up, conv_w, conv_b)


def _b_ffn_a(cfg, dgp, dup, dr2, r_mid, g2, wg, wu):
    d = cfg.D

    def body(dgp_ref, dup_ref, dr_ref, r_ref, g_ref, wg_ref, wu_ref, o_ref, dg_ref):
        _zero_at_first(pl.program_id(0) == 0, dg_ref)
        g = g_ref[...]
        _, xh, inv = _rms_fwd(r_ref[...], g)
        dh = _mm(dgp_ref[...], wg_ref[...]) + _mm(dup_ref[...], wu_ref[...])
        dx, dg = _rms_bwd(dh, xh, inv, g)
        dg_ref[...] = dg_ref[...] + dg
        o_ref[...] = dr_ref[...] + dx

    return _row_call("ffn_in_bwd", body, cfg.T, cfg.TMF, [(dgp, cfg.FF, 0), (dup, cfg.FF, 0), (dr2, d, 0), (r_mid, d, 0)],
                     [g2, wg, wu], [(d, f32)], [((1, d), f32)])


def _mm_tn_call(cfg, name, x, dy, tn):
    t, k = x.shape
    n = dy.shape[1]
    tm = cfg.TM

    def body(x_ref, dy_ref, o_ref):
        _zero_at_first(pl.program_id(1) == 0, o_ref)
        o_ref[...] = o_ref[...] + _mm_tn(x_ref[...], dy_ref[...])

    return pl.pallas_call(
        body, name=name, grid=(n // tn, t // tm),
        in_specs=[pl.BlockSpec((tm, k), lambda j, i: (i, 0)), pl.BlockSpec((tm, tn), lambda j, i: (i, j))],
        out_specs=pl.BlockSpec((k, tn), lambda j, i: (0, j)),
        out_shape=jax.ShapeDtypeStruct((k, n), f32),
        compiler_params=_params(("arbitrary", "arbitrary")),
    )(x, dy)


def _b_mix(cfg, dr, o, proj, hs, dng, lrg, wout):
    dn, lw, d = cfg.DN, cfg.LW, cfg.D

    def body(dr_ref, o_ref, z_ref, lg_ref, hs_ref, dng_ref, lrg_ref, w_ref, do_ref, dz_ref, dlg_ref, dhs_ref, ddn_ref, dlr_ref):
        _zero_at_first(pl.program_id(0) == 0, ddn_ref, dlr_ref)
        dng, lrg = dng_ref[...], lrg_ref[...]
        hs = hs_ref[...]
        o, z = o_ref[...], z_ref[...]
        heads, lru = _mix_parts(cfg, o, z, lg_ref[...], hs, dng, lrg)
        drb = dr_ref[...].astype(bf16)
        dmix_dn = _mm_nt(drb, w_ref[0:dn, :])
        dmix_lr = _mm_nt(drb, w_ref[dn:, :])
        dgn = jnp.zeros_like(dng)
        for h, (y, xh, inv, sz, dsz) in enumerate(heads):
            sl = slice(h * 128, (h + 1) * 128)
            dm = dmix_dn[:, sl]
            dz_ref[:, sl] = dm * y * dsz
            dx, dg = _rms_bwd(dm * sz, xh, inv, dng)
            do_ref[:, sl] = dx
            dgn = dgn + dg
        ddn_ref[...] = ddn_ref[...] + dgn
        _, xh2, inv2, gl, dgl = lru
        dx2, dg2 = _rms_bwd(dmix_lr, xh2, inv2, lrg)
        dlr_ref[...] = dlr_ref[...] + dg2
        dlg_ref[...] = dx2 * hs * dgl
        dhs_ref[...] = dx2 * gl

    return _row_call("mix_bwd", body, cfg.T, cfg.TM,
                     [(dr, d, 0), (o, dn, 0), (proj, dn, cfg.ZO // dn), (proj, lw, cfg.LGO // lw), (hs, lw, 0)],
                     [dng, lrg, wout], [(dn, f32), (dn, f32), (lw, f32), (lw, f32)], [((1, 128), f32), ((1, lw), f32)])


def _b_lru(cfg, proj, dhs, conv_w, conv_b, wa, wx, ba, bx, lam):
    def body(lx_ref, dh_ref, cw_ref, cb_ref, wa_ref, wx_ref, ba_ref, bx_ref, lam_ref,
             dlx_ref, dcw_ref, dcb_ref, dwa_ref, dwx_ref, dba_ref, dbx_ref, dlam_ref):
        _zero_at_first(pl.program_id(1) == 0, dcw_ref, dcb_ref, dwa_ref, dwx_ref, dba_ref, dbx_ref, dlam_ref)
        lx = lx_ref[...]
        dh = dh_ref[...]
        xc = _conv_fwd(lx, cw_ref, 2) + cb_ref[...]
        dxc = jnp.zeros_like(xc)
        for d in range(2):
            rev = d == 1
            lam = lam_ref[d:d + 1, :]
            z = _lru_dir_fwd(xc, wa_ref[d, 0], wx_ref[d, 0], ba_ref[d:d + 1, :], bx_ref[d:d + 1, :], lam, rev)
            a, m, ra, ig, sp = z["a"], z["m"], z["ra"], z["ig"], z["sp"]
            a_next = _shift(a, -1 if rev else 1, 0.0)
            lmb = _lin_scan(a_next, dh, not rev)
            h_prev = _shift(z["h"], 1 if rev else -1, 0.0)
            da = lmb * h_prev
            dm = lmb * z["gx"]
            dgx = lmb * m
            dla = da * a - dm * (a * a) / jnp.maximum(m, 1e-30)
            dra = dla * (-LRU_C) * sp
            dsp = jnp.sum(dla * (-LRU_C) * ra, axis=0, keepdims=True)
            dlam_ref[d:d + 1, :] = dlam_ref[d:d + 1, :] - dsp * _sigmoid(-lam)
            dpa = dra * ra * (1.0 - ra)
            dpx = dgx * xc * ig * (1.0 - ig)
            dba_ref[d:d + 1, :] = dba_ref[d:d + 1, :] + jnp.sum(dpa, axis=0, keepdims=True)
            dbx_ref[d:d + 1, :] = dbx_ref[d:d + 1, :] + jnp.sum(dpx, axis=0, keepdims=True)
            dwa_ref[d, 0] = dwa_ref[d, 0] + _mm_tn(xc, dpa)
            dwx_ref[d, 0] = dwx_ref[d, 0] + _mm_tn(xc, dpx)
            dxc = dxc + dgx * ig + _mm_nt(dpa, wa_ref[d, 0]) + _mm_nt(dpx, wx_ref[d, 0])
        dcb_ref[...] = dcb_ref[...] + jnp.sum(dxc, axis=0, keepdims=True)
        dx, dws = _conv_bwd(dxc, lx, cw_ref, 2)
        for j, dw in enumerate(dws):
            dcw_ref[j:j + 1, :] = dcw_ref[j:j + 1, :] + dw
        dlx_ref[...] = dx

    sp = _lru_specs(cfg, False)
    nct = cfg.LW // LANES
    return pl.pallas_call(
        body, name="lru_bwd", grid=(nct, cfg.BL),
        in_specs=[sp["lx"], sp["tok"], sp["cw"], sp["row"], sp["w"], sp["w"], sp["two"], sp["two"], sp["two"]],
        out_specs=[sp["tok"], sp["cw"], sp["row"], sp["w"], sp["w"], sp["two"], sp["two"], sp["two"]],
        out_shape=[jax.ShapeDtypeStruct((cfg.T, cfg.LW), f32), jax.ShapeDtypeStruct((4, cfg.LW), f32),
                   jax.ShapeDtypeStruct((1, cfg.LW), f32), jax.ShapeDtypeStruct((2, nct, LANES, LANES), f32),
                   jax.ShapeDtypeStruct((2, nct, LANES, LANES), f32), jax.ShapeDtypeStruct((2, cfg.LW), f32),
                   jax.ShapeDtypeStruct((2, cfg.LW), f32), jax.ShapeDtypeStruct((2, cfg.LW), f32)],
        compiler_params=_params(("arbitrary", "arbitrary")),
    )(proj, dhs, conv_w, conv_b, wa, wx, ba, bx, lam)


def _b_gdn(cfg, qkv, proj, do, alog_row, dt_row, saved):
    s, c, nc, hh = cfg.S, cfg.C, cfg.NC, cfg.H
    ba_blk = cfg.BAO // LANES
    scale = 128 ** -0.5
    per = 4 if nc % 4 == 0 else 1

    def body(q_ref, k_ref, v_ref, ba_ref, do_ref, al_ref, dt_ref, u_i, w_i, qd_i, kd_i, p_i, t_i, dec_i, cd_i,
             dqkv_ref, dba_ref, dal_ref, ddt_ref, vn_s, dvn_s, st_s, dst_s):
        h = pl.program_id(1)
        _zero_at_first((pl.program_id(0) == 0) & (h == 0), dal_ref, ddt_ref)
        _zero_at_first(h == 0, dba_ref)
        lane = lax.broadcasted_iota(jnp.int32, (c, LANES), 1)
        lane1 = lax.broadcasted_iota(jnp.int32, (1, LANES), 1)
        dirs = (0, 1)
        lane_b = [d * hh + h for d in dirs]
        lane_a = [2 * hh + d * hh + h for d in dirs]

        def fstep(i, sts):
            ns = [i, nc - 1 - i]
            rows = [pl.ds(pl.multiple_of(n * c, c), c) for n in ns]
            for d in dirs:
                st_s[d, ns[d]] = sts[d]
            vn = [u_i[d, rows[d], :] - _mm(w_i[d, rows[d], :], sts[d]) for d in dirs]
            for d in dirs:
                vn_s[d, rows[d], :] = vn[d]
            return tuple(sts[d] * cd_i[d, 0, ns[d]][0:1, :] + _mm_tn(kd_i[d, rows[d], :], vn[d]) for d in dirs)

        z0 = jnp.zeros((128, 128), f32)
        lax.fori_loop(0, nc, fstep, (z0, z0))

        def bseq(i, dst):
            ns = [nc - 1 - i, i]
            rows = [pl.ds(pl.multiple_of(n * c, c), c) for n in ns]
            for d in dirs:
                dst_s[d, ns[d]] = dst[d]
            dob = [do_ref[rows[d], :] for d in dirs]
            dvn = [_mm_tn(p_i[d, 0, ns[d]], dob[d]) + _mm(kd_i[d, rows[d], :], dst[d]) for d in dirs]
            for d in dirs:
                dvn_s[d, rows[d], :] = dvn[d]
            return tuple(_mm_tn(qd_i[d, rows[d], :], dob[d]) + cd_i[d, 0, ns[d]][0:1, 0:1] * dst[d]
                         - _mm_tn(w_i[d, rows[d], :], dvn[d]) for d in dirs)

        lax.fori_loop(0, nc, bseq, (z0, z0))

        def bpar(i, carry):
            ch = [(d, i * per + j) for j in range(per) for d in dirs]
            rows = [pl.ds(pl.multiple_of(n * c, c), c) for _, n in ch]
            masks = [_tri_masks(c, d == 1) for d, _ in ch]
            ld = lambda ref: [ref[d, r, :] for (d, _), r in zip(ch, rows)]
            ldm = lambda ref: [ref[d, 0, n] for d, n in ch]
            q, k, v, dob = ([ref[r, :] for r in rows] for ref in (q_ref, k_ref, v_ref, do_ref))
            gates = [_gate_cols(ba_ref[r, :], al_ref[...], dt_ref[...], lane_b[d], lane_a[d]) for (d, _), r in zip(ch, rows)]
            beta, g, aexp, xa = ([gt[x] for gt in gates] for x in range(4))
            st, dst = ([ref[d, n] for d, n in ch] for ref in (st_s, dst_s))
            vn, dvn, u, w = ld(vn_s), ld(dvn_s), ld(u_i), ld(w_i)
            p = [x.astype(f32) for x in ldm(p_i)]
            tm, dec = ldm(t_i), ldm(dec_i)
            cd = [cd_i[d, 0, n][0:1, 0:1] for d, n in ch]
            dp = _map(lambda m, a, b: jnp.where(m[0], _mm_nt(a, b), 0.0), masks, dob, vn)
            dqd = _map(_mm_nt, dob, st)
            dkd = _map(_mm_nt, vn, dst)
            dw = _map(lambda a, b: -_mm_nt(a, b), dvn, st)
            dcd = _map(lambda a, b: jnp.sum(jnp.sum(a * b, axis=1, keepdims=True), axis=0, keepdims=True), st, dst)
            gcum = _map(lambda m, x: _cum_mm(m[0], jnp.broadcast_to(x, (c, c)))[:, 0:1], masks, g)
            glast = [jnp.sum(x, axis=0, keepdims=True) for x in g]
            e = [jnp.exp(x) for x in gcum]
            el = _map(lambda a, b: jnp.exp(a - b), glast, gcum)
            qs = [x * scale for x in q]
            kb = _map(lambda a, b: a * b, k, beta)
            a = _map(lambda m, b, ki, dc: jnp.where(m[1], b * _mm_nt(ki, ki) * dc, 0.0), masks, beta, k, dec)
            dvb = _map(_hmm3_tn, tm, dvn)
            dkbe = _map(_hmm3_tn, tm, dw)
            da = _map(lambda m, x, ui, y, wi: -jnp.where(m[1], _mm_nt(x, ui) + _mm_nt(y, wi), 0.0), masks, dvb, u, dkbe, w)
            g1 = _map(lambda x, y: x * y, da, dec)
            g2 = _map(lambda x, y: x * y, dp, dec)
            dkb = _map(lambda x, ki, y, ei: _mm(x, ki) + y * ei, g1, k, dkbe, e)
            dk = _map(lambda x, kbi, y, qi, z, b, t, l: _mm_tn(x, kbi) + _mm_tn(y, qi) + z * b + t * l,
                      g1, kb, g2, qs, dkb, beta, dkd, el)
            dqs = _map(lambda y, ki, x, ei: _mm(y, ki) + x * ei, g2, k, dqd, e)
            ddd = _map(lambda x, ai, y, pi: x * ai + y * pi, da, a, dp, p)
            ones = jnp.ones((c, LANES), f32)
            dgcum = _map(lambda x: jnp.sum(x, axis=1, keepdims=True) - _hmm3_tn(x, ones)[:, 0:1], ddd)
            for x, (d, n) in enumerate(ch):
                dbeta = jnp.sum(dvb[x] * v[x], axis=1, keepdims=True) + jnp.sum(dkb[x] * k[x], axis=1, keepdims=True)
                de = jnp.sum(dkbe[x] * kb[x], axis=1, keepdims=True) + jnp.sum(dqd[x] * qs[x], axis=1, keepdims=True)
                del_ = jnp.sum(dkd[x] * k[x], axis=1, keepdims=True)
                dgc = dgcum[x] + de * e[x] - del_ * el[x]
                dglast = jnp.sum(del_ * el[x], axis=0, keepdims=True) + dcd[x] * cd[x]
                dg = _cum_mm(masks[x][2], jnp.broadcast_to(dgc, (c, LANES)))[:, 0:1] + dglast
                r = rows[x]
                if d == 0:
                    dqkv_ref[0, r, :] = dqs[x] * scale
                    dqkv_ref[1, r, :] = dk[x]
                    dqkv_ref[2, r, :] = dvb[x] * beta[x]
                else:
                    dqkv_ref[0, r, :] = dqkv_ref[0, r, :] + dqs[x] * scale
                    dqkv_ref[1, r, :] = dqkv_ref[1, r, :] + dk[x]
                    dqkv_ref[2, r, :] = dqkv_ref[2, r, :] + dvb[x] * beta[x]
                dlb = dbeta * beta[x] * (1.0 - beta[x])
                dalpha = -dg * aexp[x] * _sigmoid(xa[x])
                dba_ref[r, :] = dba_ref[r, :] + jnp.where(lane == lane_b[d], dlb, 0.0) + jnp.where(lane == lane_a[d], dalpha, 0.0)
                dal_ref[...] = dal_ref[...] + jnp.where(lane1 == lane_a[d], jnp.sum(dg * g[x], axis=0, keepdims=True), 0.0)
                ddt_ref[...] = ddt_ref[...] + jnp.where(lane1 == lane_a[d], jnp.sum(dalpha, axis=0, keepdims=True), 0.0)
            return carry

        lax.fori_loop(0, nc // per, bpar, 0)

    blk = lambda off: pl.BlockSpec((s, 128), functools.partial(lambda b, h, off: (b, off + h), off=off))
    row = pl.BlockSpec((1, LANES), lambda b, h: (0, 0))
    tok2, mat2, cd2, _ = _gdn_specs(cfg)
    return pl.pallas_call(
        body, name="gdn_bwd", grid=(cfg.BL, hh),
        in_specs=[blk(0), blk(hh), blk(2 * hh), pl.BlockSpec((s, LANES), lambda b, h: (b, ba_blk)),
                  pl.BlockSpec((s, 128), lambda b, h: (b, h)), row, row, tok2, tok2, tok2, tok2, mat2, mat2, mat2, cd2],
        out_specs=[pl.BlockSpec((3, s, 128), lambda b, h: (0, b, h)), pl.BlockSpec((s, LANES), lambda b, h: (b, 0)), row, row],
        out_shape=[jax.ShapeDtypeStruct((3, cfg.T, cfg.DN), f32), jax.ShapeDtypeStruct((cfg.T, LANES), f32),
                   jax.ShapeDtypeStruct((1, LANES), f32), jax.ShapeDtypeStruct((1, LANES), f32)],
        scratch_shapes=[pltpu.VMEM((2, s, 128), f32)] * 2 + [pltpu.VMEM((2, nc, 128, 128), f32)] * 2,
        compiler_params=_params(("arbitrary", "arbitrary")),
    )(qkv, qkv, qkv, proj, do, alog_row, dt_row, *saved)


def _b_prep(cfg, proj, dqkv, conv_w):
    dn, s = cfg.DN, cfg.S

    def body(x_ref, dy_ref, w_ref, dx_ref, dw_ref):
        _zero_at_first(pl.program_id(1) == 0, dw_ref)
        sec = pl.program_id(0)
        x = x_ref[...]
        c = _conv_fwd(x, w_ref, 2)
        y, dsilu = _silu(c)
        dy = dy_ref[0]
        parts = []
        for h in range(cfg.H):
            sl = slice(h * 128, (h + 1) * 128)
            yh, dyh = y[:, sl], dy[:, sl]
            inv = lax.rsqrt(jnp.sum(yh * yh, axis=1, keepdims=True) + EPS)
            dn_h = inv * dyh - yh * (inv * inv * inv) * jnp.sum(dyh * yh, axis=1, keepdims=True)
            parts.append(jnp.where(sec < 2, dn_h, dyh))
        ds = jnp.concatenate(parts, axis=1) if len(parts) > 1 else parts[0]
        dx, dws = _conv_bwd(ds * dsilu, x, w_ref, 2)
        for j, dw in enumerate(dws):
            dw_ref[j:j + 1, :] = dw_ref[j:j + 1, :] + dw
        dx_ref[...] = dx

    return pl.pallas_call(
        body, name="dn_prep_bwd", grid=(3, cfg.BL),
        in_specs=[pl.BlockSpec((s, dn), lambda j, b: (b, j)), pl.BlockSpec((1, s, dn), lambda j, b: (j, b, 0)),
                  pl.BlockSpec((4, dn), lambda j, b: (0, j))],
        out_specs=[pl.BlockSpec((s, dn), lambda j, b: (b, j)), pl.BlockSpec((4, dn), lambda j, b: (0, j))],
        out_shape=[jax.ShapeDtypeStruct((cfg.T, 3 * dn), f32), jax.ShapeDtypeStruct((4, 3 * dn), f32)],
        compiler_params=_params(("arbitrary", "arbitrary")),
    )(proj, dqkv, conv_w)


def _b_in(cfg, dqkv_pre, dz, dlx, dlg, dba, dr_mid, r_in, g1, wcat):
    d, dn, lw = cfg.D, cfg.DN, cfg.LW

    def body(dq_ref, dz_ref, dlx_ref, dlg_ref, dba_ref, dr_ref, r_ref, g_ref, w_ref, o_ref, dp_ref, dg_ref):
        _zero_at_first(pl.program_id(0) == 0, dg_ref)
        dp_ref[:, 0:cfg.ZO] = dq_ref[...].astype(bf16)
        dp_ref[:, cfg.ZO:cfg.LXO] = dz_ref[...].astype(bf16)
        dp_ref[:, cfg.LXO:cfg.LGO] = dlx_ref[...].astype(bf16)
        dp_ref[:, cfg.LGO:cfg.BAO] = dlg_ref[...].astype(bf16)
        dp_ref[:, cfg.BAO:] = dba_ref[...].astype(bf16)
        g = g_ref[...]
        _, xh, inv = _rms_fwd(r_ref[...], g)
        dx, dg = _rms_bwd(_mm(dp_ref[...], w_ref[...]), xh, inv, g)
        dg_ref[...] = dg_ref[...] + dg
        o_ref[...] = dr_ref[...] + dx

    return _row_call("in_proj_bwd", body, cfg.T, cfg.TM,
                     [(dqkv_pre, 3 * dn, 0), (dz, dn, 0), (dlx, lw, 0), (dlg, lw, 0), (dba, LANES, 0), (dr_mid, d, 0), (r_in, d, 0)],
                     [g1, wcat], [(d, f32), (cfg.PC, bf16)], [((1, d), f32)])


def _adam_call(name, w, g, m, v, tr):
    rows, cols = w.shape
    bc1 = 1.0 - ADAM_B1 ** ADAM_STEP
    bc2 = 1.0 - ADAM_B2 ** ADAM_STEP

    def body(w_ref, g_ref, m_ref, v_ref, d_ref, nm_ref, nv_ref):
        g = g_ref[...]
        m = ADAM_B1 * m_ref[...] + (1.0 - ADAM_B1) * g
        v = ADAM_B2 * v_ref[...] + (1.0 - ADAM_B2) * (g * g)
        nm_ref[...] = m
        nv_ref[...] = v
        d_ref[...] = -ADAM_LR * ((m / bc1) / (jnp.sqrt(v / bc2) + ADAM_EPS) + ADAM_WD * w_ref[...])

    spec = pl.BlockSpec((tr, cols), lambda i: (i, 0))
    return pl.pallas_call(
        body, name=name, grid=(rows // tr,), in_specs=[spec] * 4, out_specs=[spec] * 3,
        out_shape=[jax.ShapeDtypeStruct((rows, cols), f32)] * 3, compiler_params=_params(("arbitrary",)),
    )(w, g, m, v)


def _sum8_call(name, x, tr):
    _, rows, cols = x.shape

    def body(x_ref, o_ref):
        acc = x_ref[0].astype(f32)
        for j in range(1, N_DEV):
            acc = acc + x_ref[j].astype(f32)
        o_ref[...] = acc

    return pl.pallas_call(
        body, name=name, grid=(rows // tr,), in_specs=[pl.BlockSpec((N_DEV, tr, cols), lambda i: (0, i, 0))],
        out_specs=pl.BlockSpec((tr, cols), lambda i: (i, 0)), out_shape=jax.ShapeDtypeStruct((rows, cols), f32),
        compiler_params=_params(("arbitrary",)),
    )(x)


def _all_gather(name, shards):
    na = len(shards)

    def body(*refs):
        xs, outs = refs[:na], refs[na:2 * na]
        send_sems, recv_sems, local_sems = refs[2 * na:]
        x, y, c = lax.axis_index("x"), lax.axis_index("y"), lax.axis_index("c")
        me, sibling = (x, y, c), (x, y, 1 - c)
        chips = [(1 - x, y), (x, 1 - y), (1 - x, 1 - y)]

        def copy(a, k, block, to, src=None):
            px, py, pc = block
            slot = outs[a].at[4 * px + 2 * py + pc]
            return pltpu.make_async_remote_copy(
                src_ref=slot if src is None else src, dst_ref=slot,
                send_sem=send_sems.at[a, k], recv_sem=recv_sems.at[a, k], device_id=to, device_id_type=MESH)

        mine = [pltpu.make_async_copy(xs[a], outs[a].at[4 * x + 2 * y + c], local_sems.at[a]) for a in range(na)]
        for cp in mine:
            cp.start()
        first = []
        for a in range(na):
            first.append(copy(a, 0, me, sibling, src=xs[a]))
            first += [copy(a, 1 + j, me, (*chip, c), src=xs[a]) for j, chip in enumerate(chips)]
        for cp in first:
            cp.start()
        passed = []
        for j, chip in enumerate(chips):
            for a in range(na):
                copy(a, 1 + j, (*chip, c), me).wait_recv()
                cp = copy(a, 4 + j, (*chip, c), sibling)
                cp.start()
                passed.append(cp)
        for a in range(na):
            copy(a, 0, sibling, me).wait_recv()
            for j, chip in enumerate(chips):
                copy(a, 4 + j, (*chip, 1 - c), me).wait_recv()
        for cp in first + passed:
            cp.wait_send()
        for cp in mine:
            cp.wait()

    hbm = pl.BlockSpec(memory_space=pltpu.HBM)
    return pl.pallas_call(
        body, name=name, out_shape=[jax.ShapeDtypeStruct((N_DEV,) + s.shape, s.dtype) for s in shards],
        in_specs=[hbm] * na, out_specs=[hbm] * na,
        scratch_shapes=[pltpu.SemaphoreType.DMA((na, 7)), pltpu.SemaphoreType.DMA((na, 7)), pltpu.SemaphoreType.DMA((na,))],
    )(*shards)


def _peer_list():
    x, y, c = lax.axis_index("x"), lax.axis_index("y"), lax.axis_index("c")
    return 4 * x + 2 * y + c, [(x ^ (k >> 2), y ^ ((k >> 1) & 1), c ^ (k & 1)) for k in range(1, N_DEV)]


_HBM = pl.BlockSpec(memory_space=pltpu.HBM)
_SEM = pl.BlockSpec(memory_space=pltpu.SEMAPHORE)
_EFFECT = pltpu.SideEffectType.DATAFLOW_SIDE_EFFECTING


def _exchange_copies(xs, lands, send_sem, recv_sem):
    me, peers = _peer_list()
    return [pltpu.make_async_remote_copy(
        src_ref=xs[a].at[4 * px + 2 * py + pc], dst_ref=lands[a].at[me], send_sem=send_sem.at[7 * a + k],
        recv_sem=recv_sem.at[7 * a + k], device_id=(px, py, pc), device_id_type=MESH)
        for k, (px, py, pc) in enumerate(peers) for a in range(len(xs))]


def _exchange_start(name, blocks):
    na = len(blocks)

    def body(*refs):
        for cp in _exchange_copies(refs[:na], refs[na:2 * na], refs[2 * na], refs[2 * na + 1]):
            cp.start()
        refs[-1][...] = jnp.zeros_like(refs[-1])

    hbm = [pltpu.HBM(b.shape, b.dtype) for b in blocks]
    send_sem, recv_sem, *thru, token = pl.pallas_call(
        body, name=name,
        out_shape=(pltpu.SemaphoreType.DMA((7 * na,)), pltpu.SemaphoreType.DMA((7 * na,)), *hbm, *hbm,
                   jax.ShapeDtypeStruct((8, LANES), f32)),
        in_specs=[_HBM] * (2 * na), out_specs=(_SEM, _SEM, *([_HBM] * (2 * na)), pl.BlockSpec(memory_space=pltpu.VMEM)),
        input_output_aliases={i: 2 + i for i in range(2 * na)},
        compiler_params=pltpu.CompilerParams(has_side_effects=_EFFECT),
    )(*[pltpu.with_memory_space_constraint(b, pltpu.HBM) for b in blocks],
      *[pltpu.with_memory_space_constraint(lax.empty(b.shape, b.dtype), pltpu.HBM) for b in blocks])
    return send_sem, recv_sem, thru, token


def _exchange_wait(name, send_sem, recv_sem, thru, after):
    na = len(thru) // 2

    def body(*refs):
        for cp in _exchange_copies(refs[:na], refs[na:2 * na], refs[2 * na], refs[2 * na + 1]):
            cp.wait_send()
            cp.wait_recv()

    return pl.pallas_call(
        body, name=name, out_shape=tuple(pltpu.HBM(t.shape, t.dtype) for t in thru),
        in_specs=[_HBM] * (2 * na) + [_SEM, _SEM, pl.BlockSpec(memory_space=pl.ANY)], out_specs=tuple([_HBM] * (2 * na)),
        input_output_aliases={i: i for i in range(2 * na)},
        compiler_params=pltpu.CompilerParams(has_side_effects=_EFFECT),
    )(*thru, send_sem, recv_sem, after)[na:]


def _layer_fwd(cfg, w, r, p):
    h1, proj = _k_in(cfg, r, w["norm1_g"], w["wcat_t"])
    qkv = _k_prep(cfg, proj, w["dn_conv_w"])
    o, *gdn_saved = _k_gdn_fwd(cfg, qkv, proj, w["alog_row"], w["dt_row"])
    hs = _k_lru_fwd(cfg, proj, w["lru_conv_w"], w["lru_conv_b"], w["wa"], w["wx"], w["lru_ba"], w["lru_bx"], w["lru_lambda"])
    mix, r_mid = _k_mix(cfg, o, proj, hs, r, w["dn_norm_g"], w["lru_norm_g"], w["w_out"])
    h2, gp, up = _k_ffn_a(cfg, r_mid, w["norm2_g"], w["ffn_wg_t"], w["ffn_wu_t"])
    act = _k_ffn_b(cfg, gp, up, w["ffn_conv_w"], w["ffn_conv_b"])
    r2 = _k_ffn_c(cfg, act, r_mid, w["ffn_wd"])
    pn, r3 = _k_ple(cfg, r2, p, w["ple_norm_g"], w["ple_wg"], w["ple_bg"], w["ple_wp_t"])
    saved = dict(r=r, h1=h1, proj=proj, qkv=qkv, o=o, gdn=gdn_saved, hs=hs, mix=mix, r_mid=r_mid, h2=h2, gp=gp, up=up,
                 r2=r2, pn=pn, p=p)
    return r3, saved


def _layer_bwd(cfg, w, sv, dr3):
    g = {}
    dt = min(512, cfg.D)
    dr2, dlog, dpp, g["ple_norm_g"], g["ple_bg"] = _b_ple(cfg, dr3, sv["r2"], sv["p"], w["ple_norm_g"], w["ple_wg"], w["ple_bg"], w["ple_wp_t"])
    g["ple_wg"] = _mm_tn_call(cfg, "d_ple_wg", sv["pn"], dlog, dt)
    g["ple_wp_t"] = _mm_tn_call(cfg, "d_ple_wp", dpp, sv["p"], cfg.PD)
    dgp, dup, g["ffn_wd"], g["ffn_conv_w"], g["ffn_conv_b"] = _b_ffn_bc(cfg, dr2, w["ffn_wd"], sv["gp"], sv["up"], w["ffn_conv_w"], w["ffn_conv_b"])
    dr_mid, g["norm2_g"] = _b_ffn_a(cfg, dgp, dup, dr2, sv["r_mid"], w["norm2_g"], w["ffn_wg_t"], w["ffn_wu_t"])
    g["ffn_wg_t"] = _mm_tn_call(cfg, "d_ffn_wg", dgp, sv["h2"], dt)
    g["ffn_wu_t"] = _mm_tn_call(cfg, "d_ffn_wu", dup, sv["h2"], dt)
    do, dz, dlg, dhs, g["dn_norm_g"], g["lru_norm_g"] = _b_mix(cfg, dr_mid, sv["o"], sv["proj"], sv["hs"], w["dn_norm_g"], w["lru_norm_g"], w["w_out"])
    g["w_out"] = _mm_tn_call(cfg, "d_w_out", sv["mix"], dr_mid, dt)
    dlx, g["lru_conv_w"], g["lru_conv_b"], g["wa"], g["wx"], g["lru_ba"], g["lru_bx"], g["lru_lambda"] = _b_lru(
        cfg, sv["proj"], dhs, w["lru_conv_w"], w["lru_conv_b"], w["wa"], w["wx"], w["lru_ba"], w["lru_bx"], w["lru_lambda"])
    dqkv, dba, g["alog_row"], g["dt_row"] = _b_gdn(cfg, sv["qkv"], sv["proj"], do, w["alog_row"], w["dt_row"], sv["gdn"])
    dqkv_pre, g["dn_conv_w"] = _b_prep(cfg, sv["proj"], dqkv, w["dn_conv_w"])
    dr, dproj, g["norm1_g"] = _b_in(cfg, dqkv_pre, dz, dlx, dlg, dba, dr_mid, sv["r"], w["norm1_g"], w["wcat_t"])
    g["wcat_t"] = _mm_tn_call(cfg, "d_w_in", dproj, sv["h1"], dt)
    return dr, g


BIG = ("w_in", "w_out", "ffn_wg", "ffn_wu", "ffn_wd", "ple_wg", "ple_wp")
BIG_T = {"w_in": True, "w_out": False, "ffn_wg": True, "ffn_wu": True, "ffn_wd": False, "ple_wg": False, "ple_wp": True}
BIG_OPERAND = {"w_in": "wcat_t", "w_out": "w_out", "ffn_wg": "ffn_wg_t", "ffn_wu": "ffn_wu_t", "ffn_wd": "ffn_wd",
               "ple_wg": "ple_wg", "ple_wp": "ple_wp_t"}
SMALL_SHARDED = ("dn_conv_w", "lru_conv_w", "lru_ba", "lru_bx", "lru_lambda", "ffn_conv_w")
SMALL_REPL = ("norm1_g", "dn_a_log", "dn_dt_bias", "dn_norm_g", "lru_conv_b", "lru_wa", "lru_wx", "lru_norm_g", "norm2_g",
              "ffn_conv_b", "ple_norm_g", "ple_bg", "final_g")
WEIGHTS = ("norm1_g", "w_in", "dn_conv_w", "dn_a_log", "dn_dt_bias", "dn_norm_g", "lru_conv_w", "lru_conv_b", "lru_wa",
           "lru_ba", "lru_wx", "lru_bx", "lru_lambda", "lru_norm_g", "w_out", "norm2_g", "ffn_wg", "ffn_wu", "ffn_conv_w",
           "ffn_conv_b", "ffn_wd", "ple_norm_g", "ple_wg", "ple_bg", "ple_wp", "final_g")


def _pad_rows(flat, cols, mult):
    n = flat.shape[0]
    rows = -(-n // cols)
    rows = -(-rows // mult) * mult
    return jnp.pad(flat, (0, rows * cols - n)).reshape(rows, cols)


def _pack(arrs, cols, mult, dtype):
    return _pad_rows(jnp.concatenate([a.reshape(-1).astype(dtype) for a in arrs]), cols, mult)


def _unpack(flat, shapes):
    out, off = [], 0
    for shp in shapes:
        n = math.prod(shp)
        piece = flat[off:off + n]
        if n < 4096:
            piece = lax.optimization_barrier(piece)
        out.append(piece.reshape(shp))
        off += n
    return out


def _unpack8(g8, shapes, axes):
    out, off = [], 0
    for shp, ax in zip(shapes, axes):
        n = math.prod(shp)
        a = g8[:, off:off + n].reshape((N_DEV,) + tuple(shp))
        a = jnp.moveaxis(a, 0, ax)
        out.append(a.reshape(shp[:ax] + (N_DEV * shp[ax],) + shp[ax + 1:]))
        off += n
    return out


def _wcat_t_from_w_in_t(cfg, wt):
    nba = 4 * cfg.H
    pad = jnp.zeros((LANES - nba, wt.shape[1]), wt.dtype)
    return jnp.concatenate([wt[:cfg.LXO], wt[cfg.LXO + nba:], wt[cfg.LXO:cfg.LXO + nba], pad], axis=0)


def _w_in_t_from_wcat_t(cfg, wc):
    nba = 4 * cfg.H
    return jnp.concatenate([wc[:cfg.LXO], wc[cfg.BAO:cfg.BAO + nba], wc[cfg.LXO:cfg.BAO]], axis=0)


def _gate_row(cfg, a):
    h2 = 2 * cfg.H
    return jnp.concatenate([jnp.zeros((1, h2), f32), a.reshape(1, h2), jnp.zeros((1, LANES - 2 * h2), f32)], axis=1)


def _blockdiag(cfg, w):
    w = w.reshape(2, cfg.NB // 2, 2, 64, 64)
    z = jnp.zeros_like(w[:, :, 0])
    top = jnp.concatenate([w[:, :, 0], z], axis=-1)
    bot = jnp.concatenate([z, w[:, :, 1]], axis=-1)
    return jnp.concatenate([top, bot], axis=-2).astype(bf16)


def _unblockdiag(cfg, g):
    a = g[:, :, :64, :64]
    b = g[:, :, 64:, 64:]
    return jnp.stack([a, b], axis=2).reshape(2, cfg.NB, 64, 64)


def _layer_operands(cfg, big, small, i):
    return dict(
        wcat_t=_wcat_t_from_w_in_t(cfg, big["w_in"][i]), w_out=big["w_out"][i], ffn_wg_t=big["ffn_wg"][i],
        ffn_wu_t=big["ffn_wu"][i], ffn_wd=big["ffn_wd"][i], ple_wg=big["ple_wg"][i], ple_wp_t=big["ple_wp"][i],
        norm1_g=small["norm1_g"][i][None], dn_conv_w=small["dn_conv_w"][i], alog_row=_gate_row(cfg, small["dn_a_log"][i]),
        dt_row=_gate_row(cfg, small["dn_dt_bias"][i]), dn_norm_g=small["dn_norm_g"][i][None],
        lru_conv_w=small["lru_conv_w"][i], lru_conv_b=small["lru_conv_b"][i][None],
        wa=_blockdiag(cfg, small["lru_wa"][i]), wx=_blockdiag(cfg, small["lru_wx"][i]),
        lru_ba=small["lru_ba"][i], lru_bx=small["lru_bx"][i], lru_lambda=small["lru_lambda"][i],
        lru_norm_g=small["lru_norm_g"][i][None], norm2_g=small["norm2_g"][i][None], ffn_conv_w=small["ffn_conv_w"][i],
        ffn_conv_b=small["ffn_conv_b"][i][None], ple_norm_g=small["ple_norm_g"][i][None], ple_bg=small["ple_bg"][i][None],
    )


def _small_grads_to_problem(cfg, g):
    h = cfg.H
    return dict(
        norm1_g=g["norm1_g"][0], dn_conv_w=g["dn_conv_w"],
        dn_a_log=g["alog_row"][0, 2 * h:4 * h].reshape(2, h), dn_dt_bias=g["dt_row"][0, 2 * h:4 * h].reshape(2, h),
        dn_norm_g=g["dn_norm_g"][0], lru_conv_w=g["lru_conv_w"], lru_conv_b=g["lru_conv_b"][0],
        lru_wa=_unblockdiag(cfg, g["wa"]), lru_wx=_unblockdiag(cfg, g["wx"]), lru_ba=g["lru_ba"], lru_bx=g["lru_bx"],
        lru_lambda=g["lru_lambda"], lru_norm_g=g["lru_norm_g"][0], norm2_g=g["norm2_g"][0], ffn_conv_w=g["ffn_conv_w"],
        ffn_conv_b=g["ffn_conv_b"][0], ple_norm_g=g["ple_norm_g"][0], ple_bg=g["ple_bg"][0],
    )


def _local_step(cfg, big, small, x, p, target, on_big_grads):
    r = x.reshape(cfg.T, cfg.D)
    ops, saved = [], []
    for i in range(cfg.L):
        w = _layer_operands(cfg, big, small, i)
        r, sv = _layer_fwd(cfg, w, r, p[i].reshape(cfg.T, cfg.PD))
        ops.append(w)
        saved.append(sv)
    dr, loss, dgf = _k_loss(cfg, r, target.reshape(cfg.T, cfg.D), small["final_g"][None])
    gsmall = [None] * cfg.L
    for i in reversed(range(cfg.L)):
        dr, g = _layer_bwd(cfg, ops[i], saved[i], dr)
        token = on_big_grads(i, {n: (_w_in_t_from_wcat_t(cfg, g["wcat_t"]) if n == "w_in" else g[BIG_OPERAND[n]]) for n in BIG})
        if i > 0:
            ops[i - 1]["ple_bg"] = ops[i - 1]["ple_bg"] + token[0, 0]
        gsmall[i] = _small_grads_to_problem(cfg, g)
    gs = {k: jnp.stack([gl[k] for gl in gsmall]) for k in gsmall[0]}
    gs["final_g"] = dgf[0]
    return loss, dr, gs


def _row_tile(rows, limit=512):
    best = rows
    for t in range(8, min(rows, limit) + 1, 8):
        if rows % t == 0:
            best = t
    return best if best <= limit or rows <= limit else rows


def _adam_group(name, ws, gs, ms, vs, cols, tr):
    shapes = [w.shape for w in ws]
    pk = lambda arrs: _pack(arrs, cols, tr, f32)
    w2 = pk(ws)
    d, nm, nv = _adam_call(name, w2, pk(gs), pk(ms), pk(vs), min(tr, w2.shape[0]))
    return [_unpack(a.reshape(-1), shapes) for a in (d, nm, nv)]


def kernel(x, p, norm1_g, w_in, dn_conv_w, dn_a_log, dn_dt_bias, dn_norm_g, lru_conv_w, lru_conv_b, lru_wa, lru_ba, lru_wx, lru_bx, lru_lambda, lru_norm_g, w_out, norm2_g, ffn_wg, ffn_wu, ffn_conv_w, ffn_conv_b, ffn_wd, ple_norm_g, ple_wg, ple_bg, ple_wp, final_g, loss_target, m_norm1_g, m_w_in, m_dn_conv_w, m_dn_a_log, m_dn_dt_bias, m_dn_norm_g, m_lru_conv_w, m_lru_conv_b, m_lru_wa, m_lru_ba, m_lru_wx, m_lru_bx, m_lru_lambda, m_lru_norm_g, m_w_out, m_norm2_g, m_ffn_wg, m_ffn_wu, m_ffn_conv_w, m_ffn_conv_b, m_ffn_wd, m_ple_norm_g, m_ple_wg, m_ple_bg, m_ple_wp, m_final_g, v_norm1_g, v_w_in, v_dn_conv_w, v_dn_a_log, v_dn_dt_bias, v_dn_norm_g, v_lru_conv_w, v_lru_conv_b, v_lru_wa, v_lru_ba, v_lru_wx, v_lru_bx, v_lru_lambda, v_lru_norm_g, v_w_out, v_norm2_g, v_ffn_wg, v_ffn_wu, v_ffn_conv_w, v_ffn_conv_b, v_ffn_wd, v_ple_norm_g, v_ple_wg, v_ple_bg, v_ple_wp, v_final_g):
    cfg = CFG
    a = dict(locals())
    wl = {n: a[n] for n in WEIGHTS}
    ml = {n: a["m_" + n] for n in WEIGHTS}
    vl = {n: a["v_" + n] for n in WEIGHTS}
    me = 4 * lax.axis_index("x") + 2 * lax.axis_index("y") + lax.axis_index("c")
    nl = cfg.L

    blocks = [(jnp.swapaxes(wl[n], 1, 2) if BIG_T[n] else wl[n]).astype(bf16) for n in BIG]
    ss_shapes = [wl[n].shape for n in SMALL_SHARDED]
    *g8, s8 = _all_gather("gather_weights", blocks + [_pack([wl[n] for n in SMALL_SHARDED], LANES, 8, f32)])
    big = {n: jnp.moveaxis(g, 0, 1).reshape(nl, N_DEV * g.shape[2], g.shape[3]) for n, g in zip(BIG, g8)}
    small = dict(zip(SMALL_SHARDED, _unpack8(s8.reshape(N_DEV, -1), ss_shapes, [2] * len(ss_shapes))))
    small.update({n: wl[n] for n in SMALL_REPL})

    pending = {}

    def on_big_grads(i, g):
        send = [g[n].reshape((N_DEV,) + blk.shape[1:]).astype(bf16) for n, blk in zip(BIG, blocks)]
        own = [lax.dynamic_index_in_dim(sd, me, 0, keepdims=True) for sd in send]
        send_sem, recv_sem, thru, token = _exchange_start("exchange_start_%d" % i, send)
        pending[i] = (send_sem, recv_sem, thru, own)
        return token

    loss_part, dr, gsmall = _local_step(cfg, big, small, x, p, loss_target, on_big_grads)
    grad_x = dr.reshape(x.shape)

    sums = {n: [] for n in BIG}
    for i in range(nl):
        send_sem, recv_sem, thru, own = pending[i]
        lands = _exchange_wait("exchange_wait_%d" % i, send_sem, recv_sem, thru, dr)
        for n, land, o in zip(BIG, lands, own):
            slots = lax.dynamic_update_slice_in_dim(land, o, me, 0)
            sums[n].append(_sum8_call("sum_%s_%d" % (n, i), slots, slots.shape[1]))
    gl = {}
    for n in BIG:
        s = jnp.stack(sums[n])
        gl[n] = jnp.swapaxes(s, 1, 2) if BIG_T[n] else s

    small_names = SMALL_REPL + SMALL_SHARDED
    small_shapes = [gsmall[n].shape for n in small_names]
    sv = _pack([gsmall[n] for n in small_names] + [loss_part[0, 0:1]], LANES, 512, f32)
    small_sum = _sum8_call("sum_small", _all_gather("gather_small_grads", [sv])[0], 512).reshape(-1)
    gl.update(zip(small_names, _unpack(small_sum, small_shapes)))
    loss = small_sum[sum(math.prod(s) for s in small_shapes)]
    for n in SMALL_SHARDED:
        shard = wl[n].shape[2]
        gl[n] = lax.dynamic_slice_in_dim(gl[n], me * shard, shard, axis=2)

    outs = {}
    for n in BIG:
        shp = wl[n].shape
        two = lambda t: t.reshape(-1, shp[-1])
        d, nm, nv = _adam_call("adam_" + n, two(wl[n]), two(gl[n]), two(ml[n]), two(vl[n]), _row_tile(math.prod(shp[:-1])))
        outs[n] = (d.reshape(shp), nm.reshape(shp), nv.reshape(shp))
    d, nm, nv = _adam_group("adam_small", [wl[n] for n in small_names], [gl[n] for n in small_names],
                            [ml[n] for n in small_names], [vl[n] for n in small_names], LANES, 64)
    for j, n in enumerate(small_names):
        outs[n] = (d[j], nm[j], nv[j])
    return (loss, grad_x, *[gl[n] for n in WEIGHTS], *[outs[n][0] for n in WEIGHTS], *[outs[n][1] for n in WEIGHTS],
            *[outs[n][2] for n in WEIGHTS])
```

```python
import functools
import math

import jax
import jax.numpy as jnp
from jax import lax
from jax.experimental import pallas as pl
from jax.experimental.pallas import tpu as pltpu

f32 = jnp.float32
bf16 = jnp.bfloat16
MESH = pl.DeviceIdType.MESH

N_DEV = 8
LANES = 128
EPS = 1e-6
LRU_C = 8.0
ADAM_LR, ADAM_B1, ADAM_B2, ADAM_EPS, ADAM_WD, ADAM_STEP = 0.001, 0.9, 0.999, 1e-08, 0.01, 10
VMEM_LIMIT = 56 * 1024 * 1024


class Cfg:
    def __init__(self, d_model=1024, bl=4, seq=2048, depth=4, heads=4, lru_width=512, d_ff=2816, ple=256,
                 tm=512, tm_ffn=256, ff_tile=256):
        self.D, self.BL, self.S, self.L, self.H = d_model, bl, seq, depth, heads
        self.DH = 128
        self.DN = heads * self.DH
        self.LW = lru_width
        self.NB = lru_width // 64
        self.FF, self.PD = d_ff, ple
        self.C = 64
        self.NC = seq // self.C
        self.T = bl * seq
        self.TM = min(tm, self.T)
        self.TMF = min(tm_ffn, self.T)
        self.FT = ff_tile
        self.ZO = 3 * self.DN
        self.LXO = 4 * self.DN
        self.LGO = self.LXO + self.LW
        self.BAO = self.LGO + self.LW
        self.PC = self.BAO + LANES
        self.IN_COLS = 4 * self.DN + 4 * heads + 2 * self.LW


CFG = Cfg()


def _mm(a, b):
    return jnp.dot(a.astype(bf16), b.astype(bf16), preferred_element_type=f32)


def _mm_nt(a, b):
    return lax.dot_general(a.astype(bf16), b.astype(bf16), (((1,), (1,)), ((), ())), preferred_element_type=f32)


def _mm_tn(a, b):
    return lax.dot_general(a.astype(bf16), b.astype(bf16), (((0,), (0,)), ((), ())), preferred_element_type=f32)


def _split2(a):
    hi = a.astype(bf16)
    return hi, (a - hi.astype(f32)).astype(bf16)


def _hmm3(a, b, dims=(((1,), (0,)), ((), ()))):
    ah, al = _split2(a)
    bh, bl = _split2(b)
    dot = functools.partial(lax.dot_general, dimension_numbers=dims, preferred_element_type=f32)
    return dot(ah, bh) + dot(ah, bl) + dot(al, bh)


def _hmm3_tn(a, b):
    return _hmm3(a, b, (((0,), (0,)), ((), ())))


def _cum_mm(mask, x, dims=(((1,), (0,)), ((), ()))):
    m = mask.astype(bf16)
    x1 = x.astype(bf16)
    r = x - x1.astype(f32)
    x2 = r.astype(bf16)
    x3 = (r - x2.astype(f32)).astype(bf16)
    dot = functools.partial(lax.dot_general, dimension_numbers=dims, preferred_element_type=f32)
    return dot(m, x1) + dot(m, x2) + dot(m, x3)


def _rms_fwd(x, g):
    inv = lax.rsqrt(jnp.mean(x * x, axis=-1, keepdims=True) + EPS)
    xh = x * inv
    return xh * g, xh, inv


def _rms_bwd(dy, xh, inv, g):
    dxh = dy * g
    dx = inv * (dxh - xh * jnp.mean(dxh * xh, axis=-1, keepdims=True))
    dg = jnp.sum(dy * xh, axis=0, keepdims=True)
    return dx, dg


def _sigmoid(x):
    return 1.0 / (1.0 + jnp.exp(-x))


def _softplus(x):
    return jnp.maximum(x, 0.0) + jnp.log(1.0 + jnp.exp(-jnp.abs(x)))


def _silu(x):
    s = _sigmoid(x)
    return x * s, s * (1.0 + x * (1.0 - s))


_GC = math.sqrt(2.0 / math.pi)


def _gelu(x):
    t = jnp.tanh(_GC * (x + 0.044715 * x * x * x))
    y = 0.5 * x * (1.0 + t)
    dy = 0.5 * (1.0 + t) + 0.5 * x * (1.0 - t * t) * _GC * (1.0 + 3.0 * 0.044715 * x * x)
    return y, dy


def _nexpm1(x):
    ser = -x * (1.0 + x * 0.5 * (1.0 + x * (1.0 / 3.0) * (1.0 + x * 0.25 * (1.0 + x * 0.2))))
    return jnp.where(x > -0.1, ser, 1.0 - jnp.exp(x))


def _shift(x, s, fill=0.0):
    if s == 0:
        return x
    n = x.shape[0]
    t = lax.broadcasted_iota(jnp.int32, x.shape, 0)
    r = pltpu.roll(x, (-s) % n, 0)
    return jnp.where((t + s >= 0) & (t + s < n), r, fill)


def _conv_fwd(x, w_ref, left):
    k = w_ref.shape[0]
    out = _shift(x, -left) * w_ref[0:1, :]
    for j in range(1, k):
        out = out + _shift(x, j - left) * w_ref[j:j + 1, :]
    return out


def _conv_bwd(dout, x, w_ref, left):
    k = w_ref.shape[0]
    dx = None
    dws = []
    for j in range(k):
        term = _shift(dout, -(j - left)) * w_ref[j:j + 1, :]
        dx = term if dx is None else dx + term
        dws.append(jnp.sum(dout * _shift(x, j - left), axis=0, keepdims=True))
    return dx, dws


def _lin_scan(a, b, rev):
    n = a.shape[0]
    d = 1
    while d < n:
        s = d if rev else -d
        b = a * _shift(b, s, 0.0) + b
        a = a * _shift(a, s, 1.0)
        d *= 2
    return b


def _tri_masks(c, rev):
    i = lax.broadcasted_iota(jnp.int32, (c, c), 0)
    j = lax.broadcasted_iota(jnp.int32, (c, c), 1)
    incl = (i <= j) if rev else (i >= j)
    strict = (i < j) if rev else (i > j)
    incl_t = (i >= j) if rev else (i <= j)
    return incl, strict, incl_t


def _map(f, *lists):
    return [f(*a) for a in zip(*lists)]


def _tri_inv(mats):
    c = mats[0].shape[0]
    i = lax.broadcasted_iota(jnp.int32, (c, c), 0)
    j = lax.broadcasted_iota(jnp.int32, (c, c), 1)
    eye = jnp.where(i == j, 1.0, 0.0)
    t = [eye - a for a in mats]
    pw = _map(_hmm3, mats, mats)
    for it in range(5):
        t = _map(lambda ti, ui: ti + ui, t, _map(_hmm3, t, pw))
        if it < 4:
            pw = _map(_hmm3, pw, pw)
    return t


def _gdn_decay(gs, revs):
    c = gs[0].shape[0]
    masks = [_tri_masks(c, r) for r in revs]
    gb = [jnp.broadcast_to(g, (c, c)) for g in gs]
    mcol = _map(lambda m, x: _cum_mm(m[0], x), masks, gb)
    mrow = _map(lambda m, x: jnp.sum(jnp.where(m[2], x, 0.0), axis=0, keepdims=True), masks, gb)
    dec = _map(lambda m, a, b: jnp.exp(jnp.where(m[0], a - b, -1e30)), masks, mcol, mrow)
    return [m[:, 0:1] for m in mcol], [jnp.sum(g, axis=0, keepdims=True) for g in gs], dec


def _gdn_prep(qs, k, v, g, beta, kk, qk, revs):
    c = k[0].shape[0]
    masks = [_tri_masks(c, r) for r in revs]
    gcum, glast, dec = _gdn_decay(g, revs)
    e = [jnp.exp(x) for x in gcum]
    tm = _tri_inv(_map(lambda m, b, x, d: jnp.where(m[1], b * x * d, 0.0), masks, beta, kk, dec))
    u = _map(lambda t, vi, b: _hmm3(t, vi * b), tm, v, beta)
    w = _map(lambda t, ki, b, ei: _hmm3(t, ki * b * ei), tm, k, beta, e)
    p = _map(lambda m, x, d: jnp.where(m[0], x * d, 0.0), masks, qk, dec)
    return dict(dec=dec, cd=[jnp.exp(x) for x in glast], tm=tm, u=u, w=w, p=p, qd=_map(lambda a, b: a * b, qs, e),
                kd=_map(lambda ki, gl, gc: ki * jnp.exp(gl - gc), k, glast, gcum))


def _lane_pick(x, lane):
    l = lax.broadcasted_iota(jnp.int32, x.shape, 1)
    return jnp.sum(jnp.where(l == lane, x, 0.0), axis=1, keepdims=True)


def _params(sem, vmem=VMEM_LIMIT):
    return pltpu.CompilerParams(dimension_semantics=sem, vmem_limit_bytes=vmem)


def _row_call(name, body, t, tm, row_ins, full_ins, row_outs, acc_outs):
    in_specs = [pl.BlockSpec((tm, w), functools.partial(lambda i, c: (i, c), c=c)) for (_, w, c) in row_ins]
    for a in full_ins:
        in_specs.append(pl.BlockSpec(a.shape, functools.partial(lambda i, n: (0,) * n, n=a.ndim)))
    out_specs = [pl.BlockSpec((tm, w), lambda i: (i, 0)) for (w, _) in row_outs]
    out_shape = [jax.ShapeDtypeStruct((t, w), dt) for (w, dt) in row_outs]
    for shp, dt in acc_outs:
        out_specs.append(pl.BlockSpec(shp, functools.partial(lambda i, n: (0,) * n, n=len(shp))))
        out_shape.append(jax.ShapeDtypeStruct(shp, dt))
    return pl.pallas_call(
        body, name=name, grid=(t // tm,), in_specs=in_specs, out_specs=out_specs, out_shape=out_shape,
        compiler_params=_params(("arbitrary",)),
    )(*[a for (a, _, _) in row_ins], *full_ins)


def _k_in(cfg, r, g1, wcat):
    def body(r_ref, g_ref, w_ref, h_ref, proj_ref):
        y, _, _ = _rms_fwd(r_ref[...], g_ref[...])
        hb = y.astype(bf16)
        h_ref[...] = hb
        proj_ref[...] = _mm_nt(hb, w_ref[...])

    return _row_call("in_proj", body, cfg.T, cfg.TM, [(r, cfg.D, 0)], [g1, wcat],
                     [(cfg.D, bf16), (cfg.PC, f32)], [])


def _k_prep(cfg, proj, conv_w):
    dn, s = cfg.DN, cfg.S

    def body(x_ref, w_ref, o_ref):
        sec = pl.program_id(1)
        c = _conv_fwd(x_ref[...], w_ref, 2)
        y, _ = _silu(c)
        for h in range(cfg.H):
            yh = y[:, h * 128:(h + 1) * 128]
            nh = yh * lax.rsqrt(jnp.sum(yh * yh, axis=1, keepdims=True) + EPS)
            o_ref[:, h * 128:(h + 1) * 128] = jnp.where(sec < 2, nh, yh)

    return pl.pallas_call(
        body, name="dn_prep", grid=(cfg.BL, 3),
        in_specs=[pl.BlockSpec((s, dn), lambda b, j: (b, j)), pl.BlockSpec((4, dn), lambda b, j: (0, j))],
        out_specs=pl.BlockSpec((s, dn), lambda b, j: (b, j)),
        out_shape=jax.ShapeDtypeStruct((cfg.T, 3 * dn), f32),
        compiler_params=_params(("arbitrary", "arbitrary")),
    )(proj, conv_w)


def _gate_cols(ba, alog_row, dt_row, lane_b, lane_a):
    beta = _sigmoid(_lane_pick(ba, lane_b))
    alpha = _lane_pick(ba, lane_a)
    aexp = jnp.exp(_lane_pick(alog_row, lane_a))
    dtb = _lane_pick(dt_row, lane_a)
    xa = alpha + dtb
    g = -aexp * _softplus(xa)
    return beta, g, aexp, xa


def _gdn_specs(cfg):
    s, c, nc = cfg.S, cfg.C, cfg.NC
    tok2 = pl.BlockSpec((2, s, 128), lambda b, h: (0, b, h))
    mat2 = pl.BlockSpec((2, 1, nc, c, c), lambda b, h: (0, b, h, 0, 0))
    cd2 = pl.BlockSpec((2, 1, nc, 8, LANES), lambda b, h: (0, b, h, 0, 0))
    shapes = dict(
        tok32=jax.ShapeDtypeStruct((2, cfg.T, cfg.DN), f32), tok16=jax.ShapeDtypeStruct((2, cfg.T, cfg.DN), bf16),
        mat32=jax.ShapeDtypeStruct((2, cfg.BL, cfg.H * nc, c, c), f32), mat16=jax.ShapeDtypeStruct((2, cfg.BL, cfg.H * nc, c, c), bf16),
        cd=jax.ShapeDtypeStruct((2, cfg.BL, cfg.H * nc, 8, LANES), f32))
    return tok2, mat2, cd2, shapes


def _k_gdn_fwd(cfg, qkv, proj, alog_row, dt_row):
    s, c, nc, hh = cfg.S, cfg.C, cfg.NC, cfg.H
    ba_blk = cfg.BAO // LANES
    per = 4 if nc % 4 == 0 else 1

    def body(q_ref, k_ref, v_ref, ba_ref, al_ref, dt_ref, o_ref, u_o, w_o, qd_o, kd_o, p_o, t_o, dec_o, cd_o):
        h = pl.program_id(1)

        def prep(i, carry):
            chains, qs, k, v, kk, qk, g, beta = [], [], [], [], [], [], [], []
            for j in range(per):
                n = i * per + j
                rows = pl.ds(pl.multiple_of(n * c, c), c)
                kj, vj = k_ref[rows, :], v_ref[rows, :]
                qj = q_ref[rows, :] * (128 ** -0.5)
                kkj, qkj = _mm_nt(kj, kj), _mm_nt(qj, kj)
                ba = ba_ref[rows, :]
                for d in range(2):
                    bd, gd, _, _ = _gate_cols(ba, al_ref[...], dt_ref[...], d * hh + h, 2 * hh + d * hh + h)
                    chains.append((d, n, rows))
                    for lst, val in ((qs, qj), (k, kj), (v, vj), (kk, kkj), (qk, qkj), (g, gd), (beta, bd)):
                        lst.append(val)
            z = _gdn_prep(qs, k, v, g, beta, kk, qk, [d == 1 for d, _, _ in chains])
            for x, (d, n, rows) in enumerate(chains):
                u_o[d, rows, :] = z["u"][x]
                w_o[d, rows, :] = z["w"][x].astype(bf16)
                qd_o[d, rows, :] = z["qd"][x].astype(bf16)
                kd_o[d, rows, :] = z["kd"][x].astype(bf16)
                p_o[d, 0, n] = z["p"][x].astype(bf16)
                t_o[d, 0, n] = z["tm"][x]
                dec_o[d, 0, n] = z["dec"][x]
                cd_o[d, 0, n] = jnp.broadcast_to(z["cd"][x], (8, LANES))
            return carry

        lax.fori_loop(0, nc // per, prep, 0)
        o_ref[...] = jnp.zeros_like(o_ref)

        def step(i, sts):
            ns = [i, nc - 1 - i]
            rows = [pl.ds(pl.multiple_of(n * c, c), c) for n in ns]
            vn = [u_o[d, rows[d], :] - _mm(w_o[d, rows[d], :], sts[d]) for d in range(2)]
            o = [_mm(qd_o[d, rows[d], :], sts[d]) + _mm(p_o[d, 0, ns[d]], vn[d]) for d in range(2)]
            new = [sts[d] * cd_o[d, 0, ns[d]][0:1, :] + _mm_tn(kd_o[d, rows[d], :], vn[d]) for d in range(2)]
            for d in range(2):
                o_ref[rows[d], :] = o_ref[rows[d], :] + o[d]
            return tuple(new)

        z0 = jnp.zeros((128, 128), f32)
        lax.fori_loop(0, nc, step, (z0, z0))

    blk = lambda off: pl.BlockSpec((s, 128), functools.partial(lambda b, h, off: (b, off + h), off=off))
    row = pl.BlockSpec((1, LANES), lambda b, h: (0, 0))
    tok2, mat2, cd2, shp = _gdn_specs(cfg)
    return pl.pallas_call(
        body, name="gdn_fwd", grid=(cfg.BL, hh),
        in_specs=[blk(0), blk(hh), blk(2 * hh), pl.BlockSpec((s, LANES), lambda b, h: (b, ba_blk)), row, row],
        out_specs=[pl.BlockSpec((s, 128), lambda b, h: (b, h)), tok2, tok2, tok2, tok2, mat2, mat2, mat2, cd2],
        out_shape=[jax.ShapeDtypeStruct((cfg.T, cfg.DN), f32), shp["tok32"], shp["tok16"], shp["tok16"], shp["tok16"],
                   shp["mat16"], shp["mat32"], shp["mat32"], shp["cd"]],
        compiler_params=_params(("arbitrary", "arbitrary")),
    )(qkv, qkv, qkv, proj, alog_row, dt_row)


def _lru_dir_fwd(xc, wa, wx, ba, bx, lam, rev):
    ra = _sigmoid(_mm(xc, wa) + ba)
    ig = _sigmoid(_mm(xc, wx) + bx)
    sp = _softplus(-lam)
    la = -LRU_C * ra * sp
    a = jnp.exp(la)
    m = jnp.sqrt(_nexpm1(2.0 * la))
    gx = ig * xc
    h = _lin_scan(a, m * gx, rev)
    return dict(ra=ra, ig=ig, sp=sp, a=a, m=m, gx=gx, h=h)


def _lru_specs(cfg, outer_b):
    s = cfg.S
    lx_blk = cfg.LXO // LANES
    if outer_b:
        ix = lambda f: (lambda b, ct: f(b, ct))
    else:
        ix = lambda f: (lambda ct, b: f(b, ct))
    return dict(
        lx=pl.BlockSpec((s, LANES), ix(lambda b, ct: (b, lx_blk + ct))),
        tok=pl.BlockSpec((s, LANES), ix(lambda b, ct: (b, ct))),
        cw=pl.BlockSpec((4, LANES), ix(lambda b, ct: (0, ct))),
        row=pl.BlockSpec((1, LANES), ix(lambda b, ct: (0, ct))),
        w=pl.BlockSpec((2, 1, LANES, LANES), ix(lambda b, ct: (0, ct, 0, 0))),
        two=pl.BlockSpec((2, LANES), ix(lambda b, ct: (0, ct))),
    )


def _k_lru_fwd(cfg, proj, conv_w, conv_b, wa, wx, ba, bx, lam):
    def body(lx_ref, cw_ref, cb_ref, wa_ref, wx_ref, ba_ref, bx_ref, lam_ref, o_ref):
        xc = _conv_fwd(lx_ref[...], cw_ref, 2) + cb_ref[...]
        acc = None
        for d in range(2):
            z = _lru_dir_fwd(xc, wa_ref[d, 0], wx_ref[d, 0], ba_ref[d:d + 1, :], bx_ref[d:d + 1, :], lam_ref[d:d + 1, :], d == 1)
            acc = z["h"] if acc is None else acc + z["h"]
        o_ref[...] = acc

    sp = _lru_specs(cfg, True)
    return pl.pallas_call(
        body, name="lru_fwd", grid=(cfg.BL, cfg.LW // LANES),
        in_specs=[sp["lx"], sp["cw"], sp["row"], sp["w"], sp["w"], sp["two"], sp["two"], sp["two"]],
        out_specs=sp["tok"], out_shape=jax.ShapeDtypeStruct((cfg.T, cfg.LW), f32),
        compiler_params=_params(("arbitrary", "arbitrary")),
    )(proj, conv_w, conv_b, wa, wx, ba, bx, lam)


def _mix_parts(cfg, o, z, lg, hs, dng, lrg):
    heads = []
    for h in range(cfg.H):
        sl = slice(h * 128, (h + 1) * 128)
        y, xh, inv = _rms_fwd(o[:, sl], dng)
        sz, dsz = _silu(z[:, sl])
        heads.append((y, xh, inv, sz, dsz))
    gl, dgl = _gelu(lg)
    y2, xh2, inv2 = _rms_fwd(gl * hs, lrg)
    return heads, (y2, xh2, inv2, gl, dgl)


def _k_mix(cfg, o, proj, hs, r, dng, lrg, wout):
    dn = cfg.DN

    def body(o_ref, z_ref, lg_ref, hs_ref, r_ref, dng_ref, lrg_ref, w_ref, mix_ref, out_ref):
        heads, lru = _mix_parts(cfg, o_ref[...], z_ref[...], lg_ref[...], hs_ref[...], dng_ref[...], lrg_ref[...])
        for h, (y, _, _, sz, _) in enumerate(heads):
            mix_ref[:, h * 128:(h + 1) * 128] = (y * sz).astype(bf16)
        mix_ref[:, dn:] = lru[0].astype(bf16)
        out_ref[...] = r_ref[...] + jnp.dot(mix_ref[...], w_ref[...], preferred_element_type=f32)

    return _row_call("mix_out", body, cfg.T, cfg.TM,
                     [(o, dn, 0), (proj, dn, cfg.ZO // dn), (proj, cfg.LW, cfg.LGO // cfg.LW), (hs, cfg.LW, 0), (r, cfg.D, 0)],
                     [dng, lrg, wout], [(cfg.D, bf16), (cfg.D, f32)], [])


def _k_ffn_a(cfg, r, g2, wg, wu):
    def body(r_ref, g_ref, wg_ref, wu_ref, h_ref, gp_ref, up_ref):
        y, _, _ = _rms_fwd(r_ref[...], g_ref[...])
        hb = y.astype(bf16)
        h_ref[...] = hb
        gp_ref[...] = _mm_nt(hb, wg_ref[...])
        up_ref[...] = _mm_nt(hb, wu_ref[...])

    return _row_call("ffn_in", body, cfg.T, cfg.TMF, [(r, cfg.D, 0)], [g2, wg, wu],
                     [(cfg.D, bf16), (cfg.FF, f32), (cfg.FF, f32)], [])


def _k_ffn_b(cfg, gp, up, conv_w, conv_b):
    s, ft = cfg.S, cfg.FT

    def body(gp_ref, up_ref, w_ref, b_ref, o_ref):
        gate = _conv_fwd(gp_ref[...], w_ref, 1) + b_ref[...]
        gl, _ = _gelu(gate)
        o_ref[...] = (gl * up_ref[...]).astype(bf16)

    tok = pl.BlockSpec((s, ft), lambda b, j: (b, j))
    return pl.pallas_call(
        body, name="ffn_act", grid=(cfg.BL, cfg.FF // ft),
        in_specs=[tok, tok, pl.BlockSpec((3, ft), lambda b, j: (0, j)), pl.BlockSpec((1, ft), lambda b, j: (0, j))],
        out_specs=tok, out_shape=jax.ShapeDtypeStruct((cfg.T, cfg.FF), bf16),
        compiler_params=_params(("arbitrary", "arbitrary")),
    )(gp, up, conv_w, conv_b)


def _k_ffn_c(cfg, act, r, wd):
    def body(a_ref, r_ref, w_ref, o_ref):
        o_ref[...] = r_ref[...] + jnp.dot(a_ref[...], w_ref[...], preferred_element_type=f32)

    return _row_call("ffn_out", body, cfg.T, cfg.TM, [(act, cfg.FF, 0), (r, cfg.D, 0)], [wd], [(cfg.D, f32)], [])[0]


def _k_ple(cfg, r, p, gp, wpg, bg, wpp):
    def body(r_ref, p_ref, g_ref, wg_ref, bg_ref, wp_ref, pn_ref, o_ref):
        x = r_ref[...]
        y, _, _ = _rms_fwd(x, g_ref[...])
        pn = y.astype(bf16)
        pn_ref[...] = pn
        pg = _sigmoid(jnp.dot(pn, wg_ref[...], preferred_element_type=f32) + bg_ref[...])
        o_ref[...] = x + pg * _mm_nt(p_ref[...], wp_ref[...])

    return _row_call("ple", body, cfg.T, cfg.TM, [(r, cfg.D, 0), (p, cfg.PD, 0)], [gp, wpg, bg, wpp],
                     [(cfg.D, bf16), (cfg.D, f32)], [])


def _k_loss(cfg, r, tgt, gf):
    d = cfg.D

    def body(r_ref, t_ref, g_ref, dr_ref, loss_ref, dg_ref):
        @pl.when(pl.program_id(0) == 0)
        def _():
            loss_ref[...] = jnp.zeros_like(loss_ref)
            dg_ref[...] = jnp.zeros_like(dg_ref)

        g = g_ref[...]
        y, xh, inv = _rms_fwd(r_ref[...], g)
        err = y - t_ref[...]
        loss_ref[...] = loss_ref[...] + (0.5 / d) * jnp.sum(err * err)
        dx, dg = _rms_bwd(err * (1.0 / d), xh, inv, g)
        dr_ref[...] = dx
        dg_ref[...] = dg_ref[...] + dg

    return _row_call("loss_head", body, cfg.T, cfg.TM, [(r, d, 0), (tgt, d, 0)], [gf], [(d, f32)],
                     [((1, LANES), f32), ((1, d), f32)])


def _zero_at_first(cond, *refs):
    @pl.when(cond)
    def _():
        for r in refs:
            r[...] = jnp.zeros_like(r)


def _b_ple(cfg, dr3, r2, p, gp, wpg, bg, wpp):
    d = cfg.D

    def body(dr_ref, r_ref, p_ref, g_ref, wg_ref, bg_ref, wp_ref, dr2_ref, dlog_ref, dpp_ref, dgp_ref, dbg_ref):
        _zero_at_first(pl.program_id(0) == 0, dgp_ref, dbg_ref)
        g = g_ref[...]
        dr = dr_ref[...]
        y, xh, inv = _rms_fwd(r_ref[...], g)
        pg = _sigmoid(jnp.dot(y.astype(bf16), wg_ref[...], preferred_element_type=f32) + bg_ref[...])
        pp = _mm_nt(p_ref[...], wp_ref[...])
        dpp_ref[...] = (dr * pg).astype(bf16)
        dlog = dr * pp * pg * (1.0 - pg)
        dlog_ref[...] = dlog.astype(bf16)
        dbg_ref[...] = dbg_ref[...] + jnp.sum(dlog, axis=0, keepdims=True)
        dx, dg = _rms_bwd(_mm_nt(dlog, wg_ref[...]), xh, inv, g)
        dgp_ref[...] = dgp_ref[...] + dg
        dr2_ref[...] = dr + dx

    return _row_call("ple_bwd", body, cfg.T, cfg.TM, [(dr3, d, 0), (r2, d, 0), (p, cfg.PD, 0)], [gp, wpg, bg, wpp],
                     [(d, f32), (d, bf16), (d, bf16)], [((1, d), f32), ((1, d), f32)])


def _b_ffn_bc(cfg, dr2, wd, gp, up, conv_w, conv_b):
    s, ft, d = cfg.S, cfg.FT, cfg.D

    def body(dr_ref, wd_ref, gp_ref, up_ref, w_ref, b_ref, dgp_ref, dup_ref, dwd_ref, dcw_ref, dcb_ref):
        _zero_at_first(pl.program_id(1) == 0, dwd_ref, dcw_ref, dcb_ref)
        drb = dr_ref[...].astype(bf16)
        dact = _mm_nt(drb, wd_ref[...])
        gpre = gp_ref[...]
        up = up_ref[...]
        gl, dgl = _gelu(_conv_fwd(gpre, w_ref, 1) + b_ref[...])
        dup_ref[...] = (dact * gl).astype(bf16)
        dgate = dact * up * dgl
        dcb_ref[...] = dcb_ref[...] + jnp.sum(dgate, axis=0, keepdims=True)
        dx, dws = _conv_bwd(dgate, gpre, w_ref, 1)
        for j, dw in enumerate(dws):
            dcw_ref[j:j + 1, :] = dcw_ref[j:j + 1, :] + dw
        dgp_ref[...] = dx.astype(bf16)
        dwd_ref[...] = dwd_ref[...] + _mm_tn((gl * up).astype(bf16), drb)

    tok = pl.BlockSpec((s, ft), lambda j, b: (b, j))
    return pl.pallas_call(
        body, name="ffn_act_bwd", grid=(cfg.FF // ft, cfg.BL),
        in_specs=[pl.BlockSpec((s, d), lambda j, b: (b, 0)), pl.BlockSpec((ft, d), lambda j, b: (j, 0)), tok, tok,
                  pl.BlockSpec((3, ft), lambda j, b: (0, j)), pl.BlockSpec((1, ft), lambda j, b: (0, j))],
        out_specs=[tok, tok, pl.BlockSpec((ft, d), lambda j, b: (j, 0)), pl.BlockSpec((3, ft), lambda j, b: (0, j)),
                   pl.BlockSpec((1, ft), lambda j, b: (0, j))],
        out_shape=[jax.ShapeDtypeStruct((cfg.T, cfg.FF), bf16), jax.ShapeDtypeStruct((cfg.T, cfg.FF), bf16),
                   jax.ShapeDtypeStruct((cfg.FF, d), f32), jax.ShapeDtypeStruct((3, cfg.FF), f32),
                   jax.ShapeDtypeStruct((1, cfg.FF), f32)],
        compiler_params=_params(("arbitrary", "arbitrary")),
    )(dr2, wd, gp, up, conv_w, conv_b)


def _b_ffn_a(cfg, dgp, dup, dr2, r_mid, g2, wg, wu):
    d = cfg.D

    def body(dgp_ref, dup_ref, dr_ref, r_ref, g_ref, wg_ref, wu_ref, o_ref, dg_ref):
        _zero_at_first(pl.program_id(0) == 0, dg_ref)
        g = g_ref[...]
        _, xh, inv = _rms_fwd(r_ref[...], g)
        dh = _mm(dgp_ref[...], wg_ref[...]) + _mm(dup_ref[...], wu_ref[...])
        dx, dg = _rms_bwd(dh, xh, inv, g)
        dg_ref[...] = dg_ref[...] + dg
        o_ref[...] = dr_ref[...] + dx

    return _row_call("ffn_in_bwd", body, cfg.T, cfg.TMF, [(dgp, cfg.FF, 0), (dup, cfg.FF, 0), (dr2, d, 0), (r_mid, d, 0)],
                     [g2, wg, wu], [(d, f32)], [((1, d), f32)])


def _mm_tn_call(cfg, name, x, dy, tn):
    t, k = x.shape
    n = dy.shape[1]
    tm = cfg.TM

    def body(x_ref, dy_ref, o_ref):
        _zero_at_first(pl.program_id(1) == 0, o_ref)
        o_ref[...] = o_ref[...] + _mm_tn(x_ref[...], dy_ref[...])

    return pl.pallas_call(
        body, name=name, grid=(n // tn, t // tm),
        in_specs=[pl.BlockSpec((tm, k), lambda j, i: (i, 0)), pl.BlockSpec((tm, tn), lambda j, i: (i, j))],
        out_specs=pl.BlockSpec((k, tn), lambda j, i: (0, j)),
        out_shape=jax.ShapeDtypeStruct((k, n), f32),
        compiler_params=_params(("arbitrary", "arbitrary")),
    )(x, dy)


def _b_mix(cfg, dr, o, proj, hs, dng, lrg, wout):
    dn, lw, d = cfg.DN, cfg.LW, cfg.D

    def body(dr_ref, o_ref, z_ref, lg_ref, hs_ref, dng_ref, lrg_ref, w_ref, do_ref, dz_ref, dlg_ref, dhs_ref, ddn_ref, dlr_ref):
        _zero_at_first(pl.program_id(0) == 0, ddn_ref, dlr_ref)
        dng, lrg = dng_ref[...], lrg_ref[...]
        hs = hs_ref[...]
        o, z = o_ref[...], z_ref[...]
        heads, lru = _mix_parts(cfg, o, z, lg_ref[...], hs, dng, lrg)
        drb = dr_ref[...].astype(bf16)
        dmix_dn = _mm_nt(drb, w_ref[0:dn, :])
        dmix_lr = _mm_nt(drb, w_ref[dn:, :])
        dgn = jnp.zeros_like(dng)
        for h, (y, xh, inv, sz, dsz) in enumerate(heads):
            sl = slice(h * 128, (h + 1) * 128)
            dm = dmix_dn[:, sl]
            dz_ref[:, sl] = dm * y * dsz
            dx, dg = _rms_bwd(dm * sz, xh, inv, dng)
            do_ref[:, sl] = dx
            dgn = dgn + dg
        ddn_ref[...] = ddn_ref[...] + dgn
        _, xh2, inv2, gl, dgl = lru
        dx2, dg2 = _rms_bwd(dmix_lr, xh2, inv2, lrg)
        dlr_ref[...] = dlr_ref[...] + dg2
        dlg_ref[...] = dx2 * hs * dgl
        dhs_ref[...] = dx2 * gl

    return _row_call("mix_bwd", body, cfg.T, cfg.TM,
                     [(dr, d, 0), (o, dn, 0), (proj, dn, cfg.ZO // dn), (proj, lw, cfg.LGO // lw), (hs, lw, 0)],
                     [dng, lrg, wout], [(dn, f32), (dn, f32), (lw, f32), (lw, f32)], [((1, 128), f32), ((1, lw), f32)])


def _b_lru(cfg, proj, dhs, conv_w, conv_b, wa, wx, ba, bx, lam):
    def body(lx_ref, dh_ref, cw_ref, cb_ref, wa_ref, wx_ref, ba_ref, bx_ref, lam_ref,
             dlx_ref, dcw_ref, dcb_ref, dwa_ref, dwx_ref, dba_ref, dbx_ref, dlam_ref):
        _zero_at_first(pl.program_id(1) == 0, dcw_ref, dcb_ref, dwa_ref, dwx_ref, dba_ref, dbx_ref, dlam_ref)
        lx = lx_ref[...]
        dh = dh_ref[...]
        xc = _conv_fwd(lx, cw_ref, 2) + cb_ref[...]
        dxc = jnp.zeros_like(xc)
        for d in range(2):
            rev = d == 1
            lam = lam_ref[d:d + 1, :]
            z = _lru_dir_fwd(xc, wa_ref[d, 0], wx_ref[d, 0], ba_ref[d:d + 1, :], bx_ref[d:d + 1, :], lam, rev)
            a, m, ra, ig, sp = z["a"], z["m"], z["ra"], z["ig"], z["sp"]
            a_next = _shift(a, -1 if rev else 1, 0.0)
            lmb = _lin_scan(a_next, dh, not rev)
            h_prev = _shift(z["h"], 1 if rev else -1, 0.0)
            da = lmb * h_prev
            dm = lmb * z["gx"]
            dgx = lmb * m
            dla = da * a - dm * (a * a) / jnp.maximum(m, 1e-30)
            dra = dla * (-LRU_C) * sp
            dsp = jnp.sum(dla * (-LRU_C) * ra, axis=0, keepdims=True)
            dlam_ref[d:d + 1, :] = dlam_ref[d:d + 1, :] - dsp * _sigmoid(-lam)
            dpa = dra * ra * (1.0 - ra)
            dpx = dgx * xc * ig * (1.0 - ig)
            dba_ref[d:d + 1, :] = dba_ref[d:d + 1, :] + jnp.sum(dpa, axis=0, keepdims=True)
            dbx_ref[d:d + 1, :] = dbx_ref[d:d + 1, :] + jnp.sum(dpx, axis=0, keepdims=True)
            dwa_ref[d, 0] = dwa_ref[d, 0] + _mm_tn(xc, dpa)
            dwx_ref[d, 0] = dwx_ref[d, 0] + _mm_tn(xc, dpx)
            dxc = dxc + dgx * ig + _mm_nt(dpa, wa_ref[d, 0]) + _mm_nt(dpx, wx_ref[d, 0])
        dcb_ref[...] = dcb_ref[...] + jnp.sum(dxc, axis=0, keepdims=True)
        dx, dws = _conv_bwd(dxc, lx, cw_ref, 2)
        for j, dw in enumerate(dws):
            dcw_ref[j:j + 1, :] = dcw_ref[j:j + 1, :] + dw
        dlx_ref[...] = dx

    sp = _lru_specs(cfg, False)
    nct = cfg.LW // LANES
    return pl.pallas_call(
        body, name="lru_bwd", grid=(nct, cfg.BL),
        in_specs=[sp["lx"], sp["tok"], sp["cw"], sp["row"], sp["w"], sp["w"], sp["two"], sp["two"], sp["two"]],
        out_specs=[sp["tok"], sp["cw"], sp["row"], sp["w"], sp["w"], sp["two"], sp["two"], sp["two"]],
        out_shape=[jax.ShapeDtypeStruct((cfg.T, cfg.LW), f32), jax.ShapeDtypeStruct((4, cfg.LW), f32),
                   jax.ShapeDtypeStruct((1, cfg.LW), f32), jax.ShapeDtypeStruct((2, nct, LANES, LANES), f32),
                   jax.ShapeDtypeStruct((2, nct, LANES, LANES), f32), jax.ShapeDtypeStruct((2, cfg.LW), f32),
                   jax.ShapeDtypeStruct((2, cfg.LW), f32), jax.ShapeDtypeStruct((2, cfg.LW), f32)],
        compiler_params=_params(("arbitrary", "arbitrary")),
    )(proj, dhs, conv_w, conv_b, wa, wx, ba, bx, lam)


def _b_gdn(cfg, qkv, proj, do, alog_row, dt_row, saved):
    s, c, nc, hh = cfg.S, cfg.C, cfg.NC, cfg.H
    ba_blk = cfg.BAO // LANES
    scale = 128 ** -0.5
    per = 4 if nc % 4 == 0 else 1

    def body(q_ref, k_ref, v_ref, ba_ref, do_ref, al_ref, dt_ref, u_i, w_i, qd_i, kd_i, p_i, t_i, dec_i, cd_i,
             dqkv_ref, dba_ref, dal_ref, ddt_ref, vn_s, dvn_s, st_s, dst_s):
        h = pl.program_id(1)
        _zero_at_first((pl.program_id(0) == 0) & (h == 0), dal_ref, ddt_ref)
        _zero_at_first(h == 0, dba_ref)
        lane = lax.broadcasted_iota(jnp.int32, (c, LANES), 1)
        lane1 = lax.broadcasted_iota(jnp.int32, (1, LANES), 1)
        dirs = (0, 1)
        lane_b = [d * hh + h for d in dirs]
        lane_a = [2 * hh + d * hh + h for d in dirs]

        def fstep(i, sts):
            ns = [i, nc - 1 - i]
            rows = [pl.ds(pl.multiple_of(n * c, c), c) for n in ns]
            for d in dirs:
                st_s[d, ns[d]] = sts[d]
            vn = [u_i[d, rows[d], :] - _mm(w_i[d, rows[d], :], sts[d]) for d in dirs]
            for d in dirs:
                vn_s[d, rows[d], :] = vn[d]
            return tuple(sts[d] * cd_i[d, 0, ns[d]][0:1, :] + _mm_tn(kd_i[d, rows[d], :], vn[d]) for d in dirs)

        z0 = jnp.zeros((128, 128), f32)
        lax.fori_loop(0, nc, fstep, (z0, z0))

        def bseq(i, dst):
            ns = [nc - 1 - i, i]
            rows = [pl.ds(pl.multiple_of(n * c, c), c) for n in ns]
            for d in dirs:
                dst_s[d, ns[d]] = dst[d]
            dob = [do_ref[rows[d], :] for d in dirs]
            dvn = [_mm_tn(p_i[d, 0, ns[d]], dob[d]) + _mm(kd_i[d, rows[d], :], dst[d]) for d in dirs]
            for d in dirs:
                dvn_s[d, rows[d], :] = dvn[d]
            return tuple(_mm_tn(qd_i[d, rows[d], :], dob[d]) + cd_i[d, 0, ns[d]][0:1, 0:1] * dst[d]
                         - _mm_tn(w_i[d, rows[d], :], dvn[d]) for d in dirs)

        lax.fori_loop(0, nc, bseq, (z0, z0))

        def bpar(i, carry):
            ch = [(d, i * per + j) for j in range(per) for d in dirs]
            rows = [pl.ds(pl.multiple_of(n * c, c), c) for _, n in ch]
            masks = [_tri_masks(c, d == 1) for d, _ in ch]
            ld = lambda ref: [ref[d, r, :] for (d, _), r in zip(ch, rows)]
            ldm = lambda ref: [ref[d, 0, n] for d, n in ch]
            q, k, v, dob = ([ref[r, :] for r in rows] for ref in (q_ref, k_ref, v_ref, do_ref))
            gates = [_gate_cols(ba_ref[r, :], al_ref[...], dt_ref[...], lane_b[d], lane_a[d]) for (d, _), r in zip(ch, rows)]
            beta, g, aexp, xa = ([gt[x] for gt in gates] for x in range(4))
            st, dst = ([ref[d, n] for d, n in ch] for ref in (st_s, dst_s))
            vn, dvn, u, w = ld(vn_s), ld(dvn_s), ld(u_i), ld(w_i)
            p = [x.astype(f32) for x in ldm(p_i)]
            tm, dec = ldm(t_i), ldm(dec_i)
            cd = [cd_i[d, 0, n][0:1, 0:1] for d, n in ch]
            dp = _map(lambda m, a, b: jnp.where(m[0], _mm_nt(a, b), 0.0), masks, dob, vn)
            dqd = _map(_mm_nt, dob, st)
            dkd = _map(_mm_nt, vn, dst)
            dw = _map(lambda a, b: -_mm_nt(a, b), dvn, st)
            dcd = _map(lambda a, b: jnp.sum(jnp.sum(a * b, axis=1, keepdims=True), axis=0, keepdims=True), st, dst)
            gcum = _map(lambda m, x: _cum_mm(m[0], jnp.broadcast_to(x, (c, c)))[:, 0:1], masks, g)
            glast = [jnp.sum(x, axis=0, keepdims=True) for x in g]
            e = [jnp.exp(x) for x in gcum]
            el = _map(lambda a, b: jnp.exp(a - b), glast, gcum)
            qs = [x * scale for x in q]
            kb = _map(lambda a, b: a * b, k, beta)
            a = _map(lambda m, b, ki, dc: jnp.where(m[1], b * _mm_nt(ki, ki) * dc, 0.0), masks, beta, k, dec)
            dvb = _map(_hmm3_tn, tm, dvn)
            dkbe = _map(_hmm3_tn, tm, dw)
            da = _map(lambda m, x, ui, y, wi: -jnp.where(m[1], _mm_nt(x, ui) + _mm_nt(y, wi), 0.0), masks, dvb, u, dkbe, w)
            g1 = _map(lambda x, y: x * y, da, dec)
            g2 = _map(lambda x, y: x * y, dp, dec)
            dkb = _map(lambda x, ki, y, ei: _mm(x, ki) + y * ei, g1, k, dkbe, e)
            dk = _map(lambda x, kbi, y, qi, z, b, t, l: _mm_tn(x, kbi) + _mm_tn(y, qi) + z * b + t * l,
                      g1, kb, g2, qs, dkb, beta, dkd, el)
            dqs = _map(lambda y, ki, x, ei: _mm(y, ki) + x * ei, g2, k, dqd, e)
            ddd = _map(lambda x, ai, y, pi: x * ai + y * pi, da, a, dp, p)
            ones = jnp.ones((c, LANES), f32)
            dgcum = _map(lambda x: jnp.sum(x, axis=1, keepdims=True) - _hmm3_tn(x, ones)[:, 0:1], ddd)
            for x, (d, n) in enumerate(ch):
                dbeta = jnp.sum(dvb[x] * v[x], axis=1, keepdims=True) + jnp.sum(dkb[x] * k[x], axis=1, keepdims=True)
                de = jnp.sum(dkbe[x] * kb[x], axis=1, keepdims=True) + jnp.sum(dqd[x] * qs[x], axis=1, keepdims=True)
                del_ = jnp.sum(dkd[x] * k[x], axis=1, keepdims=True)
                dgc = dgcum[x] + de * e[x] - del_ * el[x]
                dglast = jnp.sum(del_ * el[x], axis=0, keepdims=True) + dcd[x] * cd[x]
                dg = _cum_mm(masks[x][2], jnp.broadcast_to(dgc, (c, LANES)))[:, 0:1] + dglast
                r = rows[x]
                if d == 0:
                    dqkv_ref[0, r, :] = dqs[x] * scale
                    dqkv_ref[1, r, :] = dk[x]
                    dqkv_ref[2, r, :] = dvb[x] * beta[x]
                else:
                    dqkv_ref[0, r, :] = dqkv_ref[0, r, :] + dqs[x] * scale
                    dqkv_ref[1, r, :] = dqkv_ref[1, r, :] + dk[x]
                    dqkv_ref[2, r, :] = dqkv_ref[2, r, :] + dvb[x] * beta[x]
                dlb = dbeta * beta[x] * (1.0 - beta[x])
                dalpha = -dg * aexp[x] * _sigmoid(xa[x])
                dba_ref[r, :] = dba_ref[r, :] + jnp.where(lane == lane_b[d], dlb, 0.0) + jnp.where(lane == lane_a[d], dalpha, 0.0)
                dal_ref[...] = dal_ref[...] + jnp.where(lane1 == lane_a[d], jnp.sum(dg * g[x], axis=0, keepdims=True), 0.0)
                ddt_ref[...] = ddt_ref[...] + jnp.where(lane1 == lane_a[d], jnp.sum(dalpha, axis=0, keepdims=True), 0.0)
            return carry

        lax.fori_loop(0, nc // per, bpar, 0)

    blk = lambda off: pl.BlockSpec((s, 128), functools.partial(lambda b, h, off: (b, off + h), off=off))
    row = pl.BlockSpec((1, LANES), lambda b, h: (0, 0))
    tok2, mat2, cd2, _ = _gdn_specs(cfg)
    return pl.pallas_call(
        body, name="gdn_bwd", grid=(cfg.BL, hh),
        in_specs=[blk(0), blk(hh), blk(2 * hh), pl.BlockSpec((s, LANES), lambda b, h: (b, ba_blk)),
                  pl.BlockSpec((s, 128), lambda b, h: (b, h)), row, row, tok2, tok2, tok2, tok2, mat2, mat2, mat2, cd2],
        out_specs=[pl.BlockSpec((3, s, 128), lambda b, h: (0, b, h)), pl.BlockSpec((s, LANES), lambda b, h: (b, 0)), row, row],
        out_shape=[jax.ShapeDtypeStruct((3, cfg.T, cfg.DN), f32), jax.ShapeDtypeStruct((cfg.T, LANES), f32),
                   jax.ShapeDtypeStruct((1, LANES), f32), jax.ShapeDtypeStruct((1, LANES), f32)],
        scratch_shapes=[pltpu.VMEM((2, s, 128), f32)] * 2 + [pltpu.VMEM((2, nc, 128, 128), f32)] * 2,
        compiler_params=_params(("arbitrary", "arbitrary")),
    )(qkv, qkv, qkv, proj, do, alog_row, dt_row, *saved)


def _b_prep(cfg, proj, dqkv, conv_w):
    dn, s = cfg.DN, cfg.S

    def body(x_ref, dy_ref, w_ref, dx_ref, dw_ref):
        _zero_at_first(pl.program_id(1) == 0, dw_ref)
        sec = pl.program_id(0)
        x = x_ref[...]
        c = _conv_fwd(x, w_ref, 2)
        y, dsilu = _silu(c)
        dy = dy_ref[0]
        parts = []
        for h in range(cfg.H):
            sl = slice(h * 128, (h + 1) * 128)
            yh, dyh = y[:, sl], dy[:, sl]
            inv = lax.rsqrt(jnp.sum(yh * yh, axis=1, keepdims=True) + EPS)
            dn_h = inv * dyh - yh * (inv * inv * inv) * jnp.sum(dyh * yh, axis=1, keepdims=True)
            parts.append(jnp.where(sec < 2, dn_h, dyh))
        ds = jnp.concatenate(parts, axis=1) if len(parts) > 1 else parts[0]
        dx, dws = _conv_bwd(ds * dsilu, x, w_ref, 2)
        for j, dw in enumerate(dws):
            dw_ref[j:j + 1, :] = dw_ref[j:j + 1, :] + dw
        dx_ref[...] = dx

    return pl.pallas_call(
        body, name="dn_prep_bwd", grid=(3, cfg.BL),
        in_specs=[pl.BlockSpec((s, dn), lambda j, b: (b, j)), pl.BlockSpec((1, s, dn), lambda j, b: (j, b, 0)),
                  pl.BlockSpec((4, dn), lambda j, b: (0, j))],
        out_specs=[pl.BlockSpec((s, dn), lambda j, b: (b, j)), pl.BlockSpec((4, dn), lambda j, b: (0, j))],
        out_shape=[jax.ShapeDtypeStruct((cfg.T, 3 * dn), f32), jax.ShapeDtypeStruct((4, 3 * dn), f32)],
        compiler_params=_params(("arbitrary", "arbitrary")),
    )(proj, dqkv, conv_w)


def _b_in(cfg, dqkv_pre, dz, dlx, dlg, dba, dr_mid, r_in, g1, wcat):
    d, dn, lw = cfg.D, cfg.DN, cfg.LW

    def body(dq_ref, dz_ref, dlx_ref, dlg_ref, dba_ref, dr_ref, r_ref, g_ref, w_ref, o_ref, dp_ref, dg_ref):
        _zero_at_first(pl.program_id(0) == 0, dg_ref)
        dp_ref[:, 0:cfg.ZO] = dq_ref[...].astype(bf16)
        dp_ref[:, cfg.ZO:cfg.LXO] = dz_ref[...].astype(bf16)
        dp_ref[:, cfg.LXO:cfg.LGO] = dlx_ref[...].astype(bf16)
        dp_ref[:, cfg.LGO:cfg.BAO] = dlg_ref[...].astype(bf16)
        dp_ref[:, cfg.BAO:] = dba_ref[...].astype(bf16)
        g = g_ref[...]
        _, xh, inv = _rms_fwd(r_ref[...], g)
        dx, dg = _rms_bwd(_mm(dp_ref[...], w_ref[...]), xh, inv, g)
        dg_ref[...] = dg_ref[...] + dg
        o_ref[...] = dr_ref[...] + dx

    return _row_call("in_proj_bwd", body, cfg.T, cfg.TM,
                     [(dqkv_pre, 3 * dn, 0), (dz, dn, 0), (dlx, lw, 0), (dlg, lw, 0), (dba, LANES, 0), (dr_mid, d, 0), (r_in, d, 0)],
                     [g1, wcat], [(d, f32), (cfg.PC, bf16)], [((1, d), f32)])


def _adam_call(name, w, g, m, v, tr):
    rows, cols = w.shape
    bc1 = 1.0 - ADAM_B1 ** ADAM_STEP
    bc2 = 1.0 - ADAM_B2 ** ADAM_STEP

    def body(w_ref, g_ref, m_ref, v_ref, d_ref, nm_ref, nv_ref):
        g = g_ref[...]
        m = ADAM_B1 * m_ref[...] + (1.0 - ADAM_B1) * g
        v = ADAM_B2 * v_ref[...] + (1.0 - ADAM_B2) * (g * g)
        nm_ref[...] = m
        nv_ref[...] = v
        d_ref[...] = -ADAM_LR * ((m / bc1) / (jnp.sqrt(v / bc2) + ADAM_EPS) + ADAM_WD * w_ref[...])

    spec = pl.BlockSpec((tr, cols), lambda i: (i, 0))
    return pl.pallas_call(
        body, name=name, grid=(rows // tr,), in_specs=[spec] * 4, out_specs=[spec] * 3,
        out_shape=[jax.ShapeDtypeStruct((rows, cols), f32)] * 3, compiler_params=_params(("arbitrary",)),
    )(w, g, m, v)


def _sum8_call(name, x, tr):
    _, rows, cols = x.shape

    def body(x_ref, o_ref):
        acc = x_ref[0].astype(f32)
        for j in range(1, N_DEV):
            acc = acc + x_ref[j].astype(f32)
        o_ref[...] = acc

    return pl.pallas_call(
        body, name=name, grid=(rows // tr,), in_specs=[pl.BlockSpec((N_DEV, tr, cols), lambda i: (0, i, 0))],
        out_specs=pl.BlockSpec((tr, cols), lambda i: (i, 0)), out_shape=jax.ShapeDtypeStruct((rows, cols), f32),
        compiler_params=_params(("arbitrary",)),
    )(x)


def _all_gather(name, shards):
    na = len(shards)

    def body(*refs):
        xs, outs = refs[:na], refs[na:2 * na]
        send_sems, recv_sems, local_sems = refs[2 * na:]
        x, y, c = lax.axis_index("x"), lax.axis_index("y"), lax.axis_index("c")
        me, sibling = (x, y, c), (x, y, 1 - c)
        chips = [(1 - x, y), (x, 1 - y), (1 - x, 1 - y)]

        def copy(a, k, block, to, src=None):
            px, py, pc = block
            slot = outs[a].at[4 * px + 2 * py + pc]
            return pltpu.make_async_remote_copy(
                src_ref=slot if src is None else src, dst_ref=slot,
                send_sem=send_sems.at[a, k], recv_sem=recv_sems.at[a, k], device_id=to, device_id_type=MESH)

        mine = [pltpu.make_async_copy(xs[a], outs[a].at[4 * x + 2 * y + c], local_sems.at[a]) for a in range(na)]
        for cp in mine:
            cp.start()
        first = []
        for a in range(na):
            first.append(copy(a, 0, me, sibling, src=xs[a]))
            first += [copy(a, 1 + j, me, (*chip, c), src=xs[a]) for j, chip in enumerate(chips)]
        for cp in first:
            cp.start()
        passed = []
        for j, chip in enumerate(chips):
            for a in range(na):
                copy(a, 1 + j, (*chip, c), me).wait_recv()
                cp = copy(a, 4 + j, (*chip, c), sibling)
                cp.start()
                passed.append(cp)
        for a in range(na):
            copy(a, 0, sibling, me).wait_recv()
            for j, chip in enumerate(chips):
                copy(a, 4 + j, (*chip, 1 - c), me).wait_recv()
        for cp in first + passed:
            cp.wait_send()
        for cp in mine:
            cp.wait()

    hbm = pl.BlockSpec(memory_space=pltpu.HBM)
    return pl.pallas_call(
        body, name=name, out_shape=[jax.ShapeDtypeStruct((N_DEV,) + s.shape, s.dtype) for s in shards],
        in_specs=[hbm] * na, out_specs=[hbm] * na,
        scratch_shapes=[pltpu.SemaphoreType.DMA((na, 7)), pltpu.SemaphoreType.DMA((na, 7)), pltpu.SemaphoreType.DMA((na,))],
    )(*shards)


def _peer_list():
    x, y, c = lax.axis_index("x"), lax.axis_index("y"), lax.axis_index("c")
    return 4 * x + 2 * y + c, [(x ^ (k >> 2), y ^ ((k >> 1) & 1), c ^ (k & 1)) for k in range(1, N_DEV)]


_HBM = pl.BlockSpec(memory_space=pltpu.HBM)
_SEM = pl.BlockSpec(memory_space=pltpu.SEMAPHORE)
_EFFECT = pltpu.SideEffectType.DATAFLOW_SIDE_EFFECTING


def _exchange_copies(xs, lands, send_sem, recv_sem, gather):
    me, peers = _peer_list()
    return [pltpu.make_async_remote_copy(
        src_ref=xs[a] if gather else xs[a].at[4 * px + 2 * py + pc], dst_ref=lands[a].at[me],
        send_sem=send_sem.at[7 * a + k], recv_sem=recv_sem.at[7 * a + k], device_id=(px, py, pc), device_id_type=MESH)
        for k, (px, py, pc) in enumerate(peers) for a in range(len(xs))]


def _exchange_start(name, blocks, gather=False):
    na = len(blocks)

    def body(*refs):
        for cp in _exchange_copies(refs[:na], refs[na:2 * na], refs[2 * na], refs[2 * na + 1], gather):
            cp.start()
        refs[-1][...] = jnp.zeros_like(refs[-1])

    lands = [jax.ShapeDtypeStruct((N_DEV,) + b.shape if gather else b.shape, b.dtype) for b in blocks]
    hbm = [pltpu.HBM(b.shape, b.dtype) for b in blocks] + [pltpu.HBM(b.shape, b.dtype) for b in lands]
    send_sem, recv_sem, *thru, token = pl.pallas_call(
        body, name=name,
        out_shape=(pltpu.SemaphoreType.DMA((7 * na,)), pltpu.SemaphoreType.DMA((7 * na,)), *hbm,
                   jax.ShapeDtypeStruct((8, LANES), f32)),
        in_specs=[_HBM] * (2 * na), out_specs=(_SEM, _SEM, *([_HBM] * (2 * na)), pl.BlockSpec(memory_space=pltpu.VMEM)),
        input_output_aliases={i: 2 + i for i in range(2 * na)},
        compiler_params=pltpu.CompilerParams(has_side_effects=_EFFECT),
    )(*[pltpu.with_memory_space_constraint(b, pltpu.HBM) for b in blocks],
      *[pltpu.with_memory_space_constraint(lax.empty(b.shape, b.dtype), pltpu.HBM) for b in lands])
    return send_sem, recv_sem, thru, token


def _exchange_wait(name, send_sem, recv_sem, thru, after, gather=False):
    na = len(thru) // 2

    def body(*refs):
        for cp in _exchange_copies(refs[:na], refs[na:2 * na], refs[2 * na], refs[2 * na + 1], gather):
            cp.wait_send()
            cp.wait_recv()

    return pl.pallas_call(
        body, name=name, out_shape=tuple(pltpu.HBM(t.shape, t.dtype) for t in thru),
        in_specs=[_HBM] * (2 * na) + [_SEM, _SEM, pl.BlockSpec(memory_space=pl.ANY)], out_specs=tuple([_HBM] * (2 * na)),
        input_output_aliases={i: i for i in range(2 * na)},
        compiler_params=pltpu.CompilerParams(has_side_effects=_EFFECT),
    )(*thru, send_sem, recv_sem, after)[na:]


def _layer_fwd(cfg, w, r, p):
    h1, proj = _k_in(cfg, r, w["norm1_g"], w["wcat_t"])
    qkv = _k_prep(cfg, proj, w["dn_conv_w"])
    o, *gdn_saved = _k_gdn_fwd(cfg, qkv, proj, w["alog_row"], w["dt_row"])
    hs = _k_lru_fwd(cfg, proj, w["lru_conv_w"], w["lru_conv_b"], w["wa"], w["wx"], w["lru_ba"], w["lru_bx"], w["lru_lambda"])
    mix, r_mid = _k_mix(cfg, o, proj, hs, r, w["dn_norm_g"], w["lru_norm_g"], w["w_out"])
    h2, gp, up = _k_ffn_a(cfg, r_mid, w["norm2_g"], w["ffn_wg_t"], w["ffn_wu_t"])
    act = _k_ffn_b(cfg, gp, up, w["ffn_conv_w"], w["ffn_conv_b"])
    r2 = _k_ffn_c(cfg, act, r_mid, w["ffn_wd"])
    pn, r3 = _k_ple(cfg, r2, p, w["ple_norm_g"], w["ple_wg"], w["ple_bg"], w["ple_wp_t"])
    saved = dict(r=r, h1=h1, proj=proj, qkv=qkv, o=o, gdn=gdn_saved, hs=hs, mix=mix, r_mid=r_mid, h2=h2, gp=gp, up=up,
                 r2=r2, pn=pn, p=p)
    return r3, saved


def _layer_bwd(cfg, w, sv, dr3):
    g = {}
    dt = min(512, cfg.D)
    dr2, dlog, dpp, g["ple_norm_g"], g["ple_bg"] = _b_ple(cfg, dr3, sv["r2"], sv["p"], w["ple_norm_g"], w["ple_wg"], w["ple_bg"], w["ple_wp_t"])
    g["ple_wg"] = _mm_tn_call(cfg, "d_ple_wg", sv["pn"], dlog, dt)
    g["ple_wp_t"] = _mm_tn_call(cfg, "d_ple_wp", dpp, sv["p"], cfg.PD)
    dgp, dup, g["ffn_wd"], g["ffn_conv_w"], g["ffn_conv_b"] = _b_ffn_bc(cfg, dr2, w["ffn_wd"], sv["gp"], sv["up"], w["ffn_conv_w"], w["ffn_conv_b"])
    dr_mid, g["norm2_g"] = _b_ffn_a(cfg, dgp, dup, dr2, sv["r_mid"], w["norm2_g"], w["ffn_wg_t"], w["ffn_wu_t"])
    g["ffn_wg_t"] = _mm_tn_call(cfg, "d_ffn_wg", dgp, sv["h2"], dt)
    g["ffn_wu_t"] = _mm_tn_call(cfg, "d_ffn_wu", dup, sv["h2"], dt)
    do, dz, dlg, dhs, g["dn_norm_g"], g["lru_norm_g"] = _b_mix(cfg, dr_mid, sv["o"], sv["proj"], sv["hs"], w["dn_norm_g"], w["lru_norm_g"], w["w_out"])
    g["w_out"] = _mm_tn_call(cfg, "d_w_out", sv["mix"], dr_mid, dt)
    dlx, g["lru_conv_w"], g["lru_conv_b"], g["wa"], g["wx"], g["lru_ba"], g["lru_bx"], g["lru_lambda"] = _b_lru(
        cfg, sv["proj"], dhs, w["lru_conv_w"], w["lru_conv_b"], w["wa"], w["wx"], w["lru_ba"], w["lru_bx"], w["lru_lambda"])
    dqkv, dba, g["alog_row"], g["dt_row"] = _b_gdn(cfg, sv["qkv"], sv["proj"], do, w["alog_row"], w["dt_row"], sv["gdn"])
    dqkv_pre, g["dn_conv_w"] = _b_prep(cfg, sv["proj"], dqkv, w["dn_conv_w"])
    dr, dproj, g["norm1_g"] = _b_in(cfg, dqkv_pre, dz, dlx, dlg, dba, dr_mid, sv["r"], w["norm1_g"], w["wcat_t"])
    g["wcat_t"] = _mm_tn_call(cfg, "d_w_in", dproj, sv["h1"], dt)
    return dr, g


BIG = ("w_in", "w_out", "ffn_wg", "ffn_wu", "ffn_wd", "ple_wg", "ple_wp")
BIG_T = {"w_in": True, "w_out": False, "ffn_wg": True, "ffn_wu": True, "ffn_wd": False, "ple_wg": False, "ple_wp": True}
BIG_OPERAND = {"w_in": "wcat_t", "w_out": "w_out", "ffn_wg": "ffn_wg_t", "ffn_wu": "ffn_wu_t", "ffn_wd": "ffn_wd",
               "ple_wg": "ple_wg", "ple_wp": "ple_wp_t"}
SMALL_SHARDED = ("dn_conv_w", "lru_conv_w", "lru_ba", "lru_bx", "lru_lambda", "ffn_conv_w")
SMALL_REPL = ("norm1_g", "dn_a_log", "dn_dt_bias", "dn_norm_g", "lru_conv_b", "lru_wa", "lru_wx", "lru_norm_g", "norm2_g",
              "ffn_conv_b", "ple_norm_g", "ple_bg", "final_g")
WEIGHTS = ("norm1_g", "w_in", "dn_conv_w", "dn_a_log", "dn_dt_bias", "dn_norm_g", "lru_conv_w", "lru_conv_b", "lru_wa",
           "lru_ba", "lru_wx", "lru_bx", "lru_lambda", "lru_norm_g", "w_out", "norm2_g", "ffn_wg", "ffn_wu", "ffn_conv_w",
           "ffn_conv_b", "ffn_wd", "ple_norm_g", "ple_wg", "ple_bg", "ple_wp", "final_g")


def _pad_rows(flat, cols, mult):
    n = flat.shape[0]
    rows = -(-n // cols)
    rows = -(-rows // mult) * mult
    return jnp.pad(flat, (0, rows * cols - n)).reshape(rows, cols)


def _pack(arrs, cols, mult, dtype):
    return _pad_rows(jnp.concatenate([a.reshape(-1).astype(dtype) for a in arrs]), cols, mult)


def _unpack(flat, shapes):
    out, off = [], 0
    for shp in shapes:
        n = math.prod(shp)
        piece = flat[off:off + n]
        if n < 4096:
            piece = lax.optimization_barrier(piece)
        out.append(piece.reshape(shp))
        off += n
    return out


def _unpack8(g8, shapes, axes):
    out, off = [], 0
    for shp, ax in zip(shapes, axes):
        n = math.prod(shp)
        a = g8[:, off:off + n].reshape((N_DEV,) + tuple(shp))
        a = jnp.moveaxis(a, 0, ax)
        out.append(a.reshape(shp[:ax] + (N_DEV * shp[ax],) + shp[ax + 1:]))
        off += n
    return out


def _wcat_t_from_w_in_t(cfg, wt):
    nba = 4 * cfg.H
    pad = jnp.zeros((LANES - nba, wt.shape[1]), wt.dtype)
    return jnp.concatenate([wt[:cfg.LXO], wt[cfg.LXO + nba:], wt[cfg.LXO:cfg.LXO + nba], pad], axis=0)


def _w_in_t_from_wcat_t(cfg, wc):
    nba = 4 * cfg.H
    return jnp.concatenate([wc[:cfg.LXO], wc[cfg.BAO:cfg.BAO + nba], wc[cfg.LXO:cfg.BAO]], axis=0)


def _gate_row(cfg, a):
    h2 = 2 * cfg.H
    return jnp.concatenate([jnp.zeros((1, h2), f32), a.reshape(1, h2), jnp.zeros((1, LANES - 2 * h2), f32)], axis=1)


def _blockdiag(cfg, w):
    w = w.reshape(2, cfg.NB // 2, 2, 64, 64)
    z = jnp.zeros_like(w[:, :, 0])
    top = jnp.concatenate([w[:, :, 0], z], axis=-1)
    bot = jnp.concatenate([z, w[:, :, 1]], axis=-1)
    return jnp.concatenate([top, bot], axis=-2).astype(bf16)


def _unblockdiag(cfg, g):
    a = g[:, :, :64, :64]
    b = g[:, :, 64:, 64:]
    return jnp.stack([a, b], axis=2).reshape(2, cfg.NB, 64, 64)


def _layer_operands(cfg, big, small, i):
    return dict(
        wcat_t=_wcat_t_from_w_in_t(cfg, big["w_in"]), w_out=big["w_out"], ffn_wg_t=big["ffn_wg"],
        ffn_wu_t=big["ffn_wu"], ffn_wd=big["ffn_wd"], ple_wg=big["ple_wg"], ple_wp_t=big["ple_wp"],
        norm1_g=small["norm1_g"][i][None], dn_conv_w=small["dn_conv_w"][i], alog_row=_gate_row(cfg, small["dn_a_log"][i]),
        dt_row=_gate_row(cfg, small["dn_dt_bias"][i]), dn_norm_g=small["dn_norm_g"][i][None],
        lru_conv_w=small["lru_conv_w"][i], lru_conv_b=small["lru_conv_b"][i][None],
        wa=_blockdiag(cfg, small["lru_wa"][i]), wx=_blockdiag(cfg, small["lru_wx"][i]),
        lru_ba=small["lru_ba"][i], lru_bx=small["lru_bx"][i], lru_lambda=small["lru_lambda"][i],
        lru_norm_g=small["lru_norm_g"][i][None], norm2_g=small["norm2_g"][i][None], ffn_conv_w=small["ffn_conv_w"][i],
        ffn_conv_b=small["ffn_conv_b"][i][None], ple_norm_g=small["ple_norm_g"][i][None], ple_bg=small["ple_bg"][i][None],
    )


def _small_grads_to_problem(cfg, g):
    h = cfg.H
    return dict(
        norm1_g=g["norm1_g"][0], dn_conv_w=g["dn_conv_w"],
        dn_a_log=g["alog_row"][0, 2 * h:4 * h].reshape(2, h), dn_dt_bias=g["dt_row"][0, 2 * h:4 * h].reshape(2, h),
        dn_norm_g=g["dn_norm_g"][0], lru_conv_w=g["lru_conv_w"], lru_conv_b=g["lru_conv_b"][0],
        lru_wa=_unblockdiag(cfg, g["wa"]), lru_wx=_unblockdiag(cfg, g["wx"]), lru_ba=g["lru_ba"], lru_bx=g["lru_bx"],
        lru_lambda=g["lru_lambda"], lru_norm_g=g["lru_norm_g"][0], norm2_g=g["norm2_g"][0], ffn_conv_w=g["ffn_conv_w"],
        ffn_conv_b=g["ffn_conv_b"][0], ple_norm_g=g["ple_norm_g"][0], ple_bg=g["ple_bg"][0],
    )


def _local_step(cfg, get_big, small, x, p, target, on_big_grads):
    r = x.reshape(cfg.T, cfg.D)
    ops, saved = [], []
    for i in range(cfg.L):
        w = _layer_operands(cfg, get_big(i, r), small, i)
        r, sv = _layer_fwd(cfg, w, r, p[i].reshape(cfg.T, cfg.PD))
        ops.append(w)
        saved.append(sv)
    dr, loss, dgf = _k_loss(cfg, r, target.reshape(cfg.T, cfg.D), small["final_g"][None])
    gsmall = [None] * cfg.L
    for i in reversed(range(cfg.L)):
        dr, g = _layer_bwd(cfg, ops[i], saved[i], dr)
        token = on_big_grads(i, {n: (_w_in_t_from_wcat_t(cfg, g["wcat_t"]) if n == "w_in" else g[BIG_OPERAND[n]]) for n in BIG})
        if i > 0:
            ops[i - 1]["ple_bg"] = ops[i - 1]["ple_bg"] + token[0, 0]
        gsmall[i] = _small_grads_to_problem(cfg, g)
    gs = {k: jnp.stack([gl[k] for gl in gsmall]) for k in gsmall[0]}
    gs["final_g"] = dgf[0]
    return loss, dr, gs


def _row_tile(rows, limit=512):
    best = rows
    for t in range(8, min(rows, limit) + 1, 8):
        if rows % t == 0:
            best = t
    return best if best <= limit or rows <= limit else rows


def _adam_group(name, ws, gs, ms, vs, cols, tr):
    shapes = [w.shape for w in ws]
    pk = lambda arrs: _pack(arrs, cols, tr, f32)
    w2 = pk(ws)
    d, nm, nv = _adam_call(name, w2, pk(gs), pk(ms), pk(vs), min(tr, w2.shape[0]))
    return [_unpack(a.reshape(-1), shapes) for a in (d, nm, nv)]


def kernel(x, p, norm1_g, w_in, dn_conv_w, dn_a_log, dn_dt_bias, dn_norm_g, lru_conv_w, lru_conv_b, lru_wa, lru_ba, lru_wx, lru_bx, lru_lambda, lru_norm_g, w_out, norm2_g, ffn_wg, ffn_wu, ffn_conv_w, ffn_conv_b, ffn_wd, ple_norm_g, ple_wg, ple_bg, ple_wp, final_g, loss_target, m_norm1_g, m_w_in, m_dn_conv_w, m_dn_a_log, m_dn_dt_bias, m_dn_norm_g, m_lru_conv_w, m_lru_conv_b, m_lru_wa, m_lru_ba, m_lru_wx, m_lru_bx, m_lru_lambda, m_lru_norm_g, m_w_out, m_norm2_g, m_ffn_wg, m_ffn_wu, m_ffn_conv_w, m_ffn_conv_b, m_ffn_wd, m_ple_norm_g, m_ple_wg, m_ple_bg, m_ple_wp, m_final_g, v_norm1_g, v_w_in, v_dn_conv_w, v_dn_a_log, v_dn_dt_bias, v_dn_norm_g, v_lru_conv_w, v_lru_conv_b, v_lru_wa, v_lru_ba, v_lru_wx, v_lru_bx, v_lru_lambda, v_lru_norm_g, v_w_out, v_norm2_g, v_ffn_wg, v_ffn_wu, v_ffn_conv_w, v_ffn_conv_b, v_ffn_wd, v_ple_norm_g, v_ple_wg, v_ple_bg, v_ple_wp, v_final_g):
    cfg = CFG
    a = dict(locals())
    wl = {n: a[n] for n in WEIGHTS}
    ml = {n: a["m_" + n] for n in WEIGHTS}
    vl = {n: a["v_" + n] for n in WEIGHTS}
    me = 4 * lax.axis_index("x") + 2 * lax.axis_index("y") + lax.axis_index("c")
    nl = cfg.L

    blocks = [(jnp.swapaxes(wl[n], 1, 2) if BIG_T[n] else wl[n]).astype(bf16) for n in BIG]
    gathers = [_exchange_start("gather_start_%d" % i, [blk[i] for blk in blocks], gather=True) for i in range(nl)]
    all_started = sum(g[3][0:1, 0:1] for g in gathers)
    ss_shapes = [wl[n].shape for n in SMALL_SHARDED]
    s8, = _all_gather("gather_small", [_pack([wl[n] for n in SMALL_SHARDED], LANES, 8, f32)])
    small = dict(zip(SMALL_SHARDED, _unpack8(s8.reshape(N_DEV, -1), ss_shapes, [2] * len(ss_shapes))))
    small.update({n: wl[n] for n in SMALL_REPL})

    def get_big(i, r):
        send_sem, recv_sem, thru, _ = gathers[i]
        lands = _exchange_wait("gather_wait_%d" % i, send_sem, recv_sem, thru, all_started if i == 0 else r, gather=True)
        out = {}
        for n, land, blk in zip(BIG, lands, blocks):
            full = lax.dynamic_update_slice_in_dim(land, blk[i][None], me, 0)
            out[n] = full.reshape(N_DEV * full.shape[1], full.shape[2])
        return out

    pending = {}

    def on_big_grads(i, g):
        send = [g[n].reshape((N_DEV,) + blk.shape[1:]).astype(bf16) for n, blk in zip(BIG, blocks)]
        own = [lax.dynamic_index_in_dim(sd, me, 0, keepdims=True) for sd in send]
        send_sem, recv_sem, thru, token = _exchange_start("exchange_start_%d" % i, send)
        pending[i] = (send_sem, recv_sem, thru, own)
        return token

    loss_part, dr, gsmall = _local_step(cfg, get_big, small, x, p, loss_target, on_big_grads)
    grad_x = dr.reshape(x.shape)

    sums = {n: [] for n in BIG}
    for i in range(nl):
        send_sem, recv_sem, thru, own = pending[i]
        lands = _exchange_wait("exchange_wait_%d" % i, send_sem, recv_sem, thru, dr)
        for n, land, o in zip(BIG, lands, own):
            slots = lax.dynamic_update_slice_in_dim(land, o, me, 0)
            sums[n].append(_sum8_call("sum_%s_%d" % (n, i), slots, slots.shape[1]))
    gl = {}
    for n in BIG:
        s = jnp.stack(sums[n])
        gl[n] = jnp.swapaxes(s, 1, 2) if BIG_T[n] else s

    small_names = SMALL_REPL + SMALL_SHARDED
    small_shapes = [gsmall[n].shape for n in small_names]
    sv = _pack([gsmall[n] for n in small_names] + [loss_part[0, 0:1]], LANES, 512, f32)
    small_sum = _sum8_call("sum_small", _all_gather("gather_small_grads", [sv])[0], 512).reshape(-1)
    gl.update(zip(small_names, _unpack(small_sum, small_shapes)))
    loss = small_sum[sum(math.prod(s) for s in small_shapes)]
    for n in SMALL_SHARDED:
        shard = wl[n].shape[2]
        gl[n] = lax.dynamic_slice_in_dim(gl[n], me * shard, shard, axis=2)

    outs = {}
    for n in BIG:
        shp = wl[n].shape
        two = lambda t: t.reshape(-1, shp[-1])
        d, nm, nv = _adam_call("adam_" + n, two(wl[n]), two(gl[n]), two(ml[n]), two(vl[n]), _row_tile(math.prod(shp[:-1])))
        outs[n] = (d.reshape(shp), nm.reshape(shp), nv.reshape(shp))
    d, nm, nv = _adam_group("adam_small", [wl[n] for n in small_names], [gl[n] for n in small_names],
                            [ml[n] for n in small_names], [vl[n] for n in small_names], LANES, 64)
    for j, n in enumerate(small_names):
        outs[n] = (d[j], nm[j], nv[j])
    return (loss, grad_x, *[gl[n] for n in WEIGHTS], *[outs[n][0] for n in WEIGHTS], *[outs[n][1] for n in WEIGHTS],
            *[outs[n][2] for n in WEIGHTS])
```

```python
import functools
import math

import jax
import jax.numpy as jnp
from jax import lax
from jax.experimental import pallas as pl
from jax.experimental.pallas import tpu as pltpu

f32 = jnp.float32
bf16 = jnp.bfloat16
MESH = pl.DeviceIdType.MESH

N_DEV = 8
LANES = 128
EPS = 1e-6
LRU_C = 8.0
ADAM_LR, ADAM_B1, ADAM_B2, ADAM_EPS, ADAM_WD, ADAM_STEP = 0.001, 0.9, 0.999, 1e-08, 0.01, 10
VMEM_LIMIT = 56 * 1024 * 1024


class Cfg:
    def __init__(self, d_model=1024, bl=4, seq=2048, depth=4, heads=4, lru_width=512, d_ff=2816, ple=256,
                 tm=512, tm_ffn=256, ff_tile=256):
        self.D, self.BL, self.S, self.L, self.H = d_model, bl, seq, depth, heads
        self.DH = 128
        self.DN = heads * self.DH
        self.LW = lru_width
        self.NB = lru_width // 64
        self.FF, self.PD = d_ff, ple
        self.C = 64
        self.NC = seq // self.C
        self.T = bl * seq
        self.TM = min(tm, self.T)
        self.TMF = min(tm_ffn, self.T)
        self.FT = ff_tile
        self.ZO = 3 * self.DN
        self.LXO = 4 * self.DN
        self.LGO = self.LXO + self.LW
        self.BAO = self.LGO + self.LW
        self.PC = self.BAO + LANES
        self.IN_COLS = 4 * self.DN + 4 * heads + 2 * self.LW


CFG = Cfg()


def _mm(a, b):
    return jnp.dot(a.astype(bf16), b.astype(bf16), preferred_element_type=f32)


def _mm_nt(a, b):
    return lax.dot_general(a.astype(bf16), b.astype(bf16), (((1,), (1,)), ((), ())), preferred_element_type=f32)


def _mm_tn(a, b):
    return lax.dot_general(a.astype(bf16), b.astype(bf16), (((0,), (0,)), ((), ())), preferred_element_type=f32)


def _split2(a):
    hi = a.astype(bf16)
    return hi, (a - hi.astype(f32)).astype(bf16)


def _hmm3(a, b, dims=(((1,), (0,)), ((), ()))):
    ah, al = _split2(a)
    bh, bl = _split2(b)
    dot = functools.partial(lax.dot_general, dimension_numbers=dims, preferred_element_type=f32)
    return dot(ah, bh) + dot(ah, bl) + dot(al, bh)


def _hmm3_tn(a, b):
    return _hmm3(a, b, (((0,), (0,)), ((), ())))


def _cum_mm(mask, x, dims=(((1,), (0,)), ((), ()))):
    m = mask.astype(bf16)
    x1 = x.astype(bf16)
    r = x - x1.astype(f32)
    x2 = r.astype(bf16)
    x3 = (r - x2.astype(f32)).astype(bf16)
    dot = functools.partial(lax.dot_general, dimension_numbers=dims, preferred_element_type=f32)
    return dot(m, x1) + dot(m, x2) + dot(m, x3)


def _rms_fwd(x, g):
    inv = lax.rsqrt(jnp.mean(x * x, axis=-1, keepdims=True) + EPS)
    xh = x * inv
    return xh * g, xh, inv


def _rms_bwd(dy, xh, inv, g):
    dxh = dy * g
    dx = inv * (dxh - xh * jnp.mean(dxh * xh, axis=-1, keepdims=True))
    dg = jnp.sum(dy * xh, axis=0, keepdims=True)
    return dx, dg


def _sigmoid(x):
    return 1.0 / (1.0 + jnp.exp(-x))


def _softplus(x):
    return jnp.maximum(x, 0.0) + jnp.log(1.0 + jnp.exp(-jnp.abs(x)))


def _silu(x):
    s = _sigmoid(x)
    return x * s, s * (1.0 + x * (1.0 - s))


_GC = math.sqrt(2.0 / math.pi)


def _gelu(x):
    t = jnp.tanh(_GC * (x + 0.044715 * x * x * x))
    y = 0.5 * x * (1.0 + t)
    dy = 0.5 * (1.0 + t) + 0.5 * x * (1.0 - t * t) * _GC * (1.0 + 3.0 * 0.044715 * x * x)
    return y, dy


def _nexpm1(x):
    ser = -x * (1.0 + x * 0.5 * (1.0 + x * (1.0 / 3.0) * (1.0 + x * 0.25 * (1.0 + x * 0.2))))
    return jnp.where(x > -0.1, ser, 1.0 - jnp.exp(x))


def _shift(x, s, fill=0.0):
    if s == 0:
        return x
    n = x.shape[0]
    t = lax.broadcasted_iota(jnp.int32, x.shape, 0)
    r = pltpu.roll(x, (-s) % n, 0)
    return jnp.where((t + s >= 0) & (t + s < n), r, fill)


def _conv_fwd(x, w_ref, left):
    k = w_ref.shape[0]
    out = _shift(x, -left) * w_ref[0:1, :]
    for j in range(1, k):
        out = out + _shift(x, j - left) * w_ref[j:j + 1, :]
    return out


def _conv_bwd(dout, x, w_ref, left):
    k = w_ref.shape[0]
    dx = None
    dws = []
    for j in range(k):
        term = _shift(dout, -(j - left)) * w_ref[j:j + 1, :]
        dx = term if dx is None else dx + term
        dws.append(jnp.sum(dout * _shift(x, j - left), axis=0, keepdims=True))
    return dx, dws


def _lin_scan(a, b, rev):
    n = a.shape[0]
    d = 1
    while d < n:
        s = d if rev else -d
        b = a * _shift(b, s, 0.0) + b
        a = a * _shift(a, s, 1.0)
        d *= 2
    return b


def _tri_masks(c, rev):
    i = lax.broadcasted_iota(jnp.int32, (c, c), 0)
    j = lax.broadcasted_iota(jnp.int32, (c, c), 1)
    incl = (i <= j) if rev else (i >= j)
    strict = (i < j) if rev else (i > j)
    incl_t = (i >= j) if rev else (i <= j)
    return incl, strict, incl_t


def _map(f, *lists):
    return [f(*a) for a in zip(*lists)]


def _tri_inv(mats):
    c = mats[0].shape[0]
    i = lax.broadcasted_iota(jnp.int32, (c, c), 0)
    j = lax.broadcasted_iota(jnp.int32, (c, c), 1)
    eye = jnp.where(i == j, 1.0, 0.0)
    t = [eye - a for a in mats]
    pw = _map(_hmm3, mats, mats)
    for it in range(5):
        t = _map(lambda ti, ui: ti + ui, t, _map(_hmm3, t, pw))
        if it < 4:
            pw = _map(_hmm3, pw, pw)
    return t


def _gdn_decay(gs, revs):
    c = gs[0].shape[0]
    masks = [_tri_masks(c, r) for r in revs]
    gb = [jnp.broadcast_to(g, (c, c)) for g in gs]
    mcol = _map(lambda m, x: _cum_mm(m[0], x), masks, gb)
    mrow = _map(lambda m, x: jnp.sum(jnp.where(m[2], x, 0.0), axis=0, keepdims=True), masks, gb)
    dec = _map(lambda m, a, b: jnp.exp(jnp.where(m[0], a - b, -1e30)), masks, mcol, mrow)
    return [m[:, 0:1] for m in mcol], [jnp.sum(g, axis=0, keepdims=True) for g in gs], dec


def _gdn_prep(qs, k, v, g, beta, kk, qk, revs):
    c = k[0].shape[0]
    masks = [_tri_masks(c, r) for r in revs]
    gcum, glast, dec = _gdn_decay(g, revs)
    e = [jnp.exp(x) for x in gcum]
    tm = _tri_inv(_map(lambda m, b, x, d: jnp.where(m[1], b * x * d, 0.0), masks, beta, kk, dec))
    u = _map(lambda t, vi, b: _hmm3(t, vi * b), tm, v, beta)
    w = _map(lambda t, ki, b, ei: _hmm3(t, ki * b * ei), tm, k, beta, e)
    p = _map(lambda m, x, d: jnp.where(m[0], x * d, 0.0), masks, qk, dec)
    return dict(dec=dec, cd=[jnp.exp(x) for x in glast], tm=tm, u=u, w=w, p=p, qd=_map(lambda a, b: a * b, qs, e),
                kd=_map(lambda ki, gl, gc: ki * jnp.exp(gl - gc), k, glast, gcum))


def _lane_pick(x, lane):
    l = lax.broadcasted_iota(jnp.int32, x.shape, 1)
    return jnp.sum(jnp.where(l == lane, x, 0.0), axis=1, keepdims=True)


def _params(sem, vmem=VMEM_LIMIT):
    return pltpu.CompilerParams(dimension_semantics=sem, vmem_limit_bytes=vmem)


def _row_call(name, body, t, tm, row_ins, full_ins, row_outs, acc_outs):
    in_specs = [pl.BlockSpec((tm, w), functools.partial(lambda i, c: (i, c), c=c)) for (_, w, c) in row_ins]
    for a in full_ins:
        in_specs.append(pl.BlockSpec(a.shape, functools.partial(lambda i, n: (0,) * n, n=a.ndim)))
    out_specs = [pl.BlockSpec((tm, w), lambda i: (i, 0)) for (w, _) in row_outs]
    out_shape = [jax.ShapeDtypeStruct((t, w), dt) for (w, dt) in row_outs]
    for shp, dt in acc_outs:
        out_specs.append(pl.BlockSpec(shp, functools.partial(lambda i, n: (0,) * n, n=len(shp))))
        out_shape.append(jax.ShapeDtypeStruct(shp, dt))
    return pl.pallas_call(
        body, name=name, grid=(t // tm,), in_specs=in_specs, out_specs=out_specs, out_shape=out_shape,
        compiler_params=_params(("arbitrary",)),
    )(*[a for (a, _, _) in row_ins], *full_ins)


def _k_in(cfg, r, g1, wcat):
    def body(r_ref, g_ref, w_ref, h_ref, proj_ref):
        y, _, _ = _rms_fwd(r_ref[...], g_ref[...])
        hb = y.astype(bf16)
        h_ref[...] = hb
        proj_ref[...] = _mm_nt(hb, w_ref[...])

    return _row_call("in_proj", body, cfg.T, cfg.TM, [(r, cfg.D, 0)], [g1, wcat],
                     [(cfg.D, bf16), (cfg.PC, f32)], [])


def _k_prep(cfg, proj, conv_w):
    dn, s = cfg.DN, cfg.S

    def body(x_ref, w_ref, o_ref):
        sec = pl.program_id(1)
        c = _conv_fwd(x_ref[...], w_ref, 2)
        y, _ = _silu(c)
        for h in range(cfg.H):
            yh = y[:, h * 128:(h + 1) * 128]
            nh = yh * lax.rsqrt(jnp.sum(yh * yh, axis=1, keepdims=True) + EPS)
            o_ref[:, h * 128:(h + 1) * 128] = jnp.where(sec < 2, nh, yh)

    return pl.pallas_call(
        body, name="dn_prep", grid=(cfg.BL, 3),
        in_specs=[pl.BlockSpec((s, dn), lambda b, j: (b, j)), pl.BlockSpec((4, dn), lambda b, j: (0, j))],
        out_specs=pl.BlockSpec((s, dn), lambda b, j: (b, j)),
        out_shape=jax.ShapeDtypeStruct((cfg.T, 3 * dn), f32),
        compiler_params=_params(("arbitrary", "arbitrary")),
    )(proj, conv_w)


def _gate_cols(ba, alog_row, dt_row, lane_b, lane_a):
    beta = _sigmoid(_lane_pick(ba, lane_b))
    alpha = _lane_pick(ba, lane_a)
    aexp = jnp.exp(_lane_pick(alog_row, lane_a))
    dtb = _lane_pick(dt_row, lane_a)
    xa = alpha + dtb
    g = -aexp * _softplus(xa)
    return beta, g, aexp, xa


def _gdn_specs(cfg):
    s, c, nc = cfg.S, cfg.C, cfg.NC
    tok2 = pl.BlockSpec((2, s, 128), lambda b, h: (0, b, h))
    mat2 = pl.BlockSpec((2, 1, nc, c, c), lambda b, h: (0, b, h, 0, 0))
    cd2 = pl.BlockSpec((2, 1, nc, 8, LANES), lambda b, h: (0, b, h, 0, 0))
    shapes = dict(
        tok32=jax.ShapeDtypeStruct((2, cfg.T, cfg.DN), f32), tok16=jax.ShapeDtypeStruct((2, cfg.T, cfg.DN), bf16),
        mat32=jax.ShapeDtypeStruct((2, cfg.BL, cfg.H * nc, c, c), f32), mat16=jax.ShapeDtypeStruct((2, cfg.BL, cfg.H * nc, c, c), bf16),
        cd=jax.ShapeDtypeStruct((2, cfg.BL, cfg.H * nc, 8, LANES), f32))
    return tok2, mat2, cd2, shapes


def _k_gdn_fwd(cfg, qkv, proj, alog_row, dt_row):
    s, c, nc, hh = cfg.S, cfg.C, cfg.NC, cfg.H
    ba_blk = cfg.BAO // LANES
    per = 4 if nc % 4 == 0 else 1

    def body(q_ref, k_ref, v_ref, ba_ref, al_ref, dt_ref, o_ref, u_o, w_o, qd_o, kd_o, p_o, t_o, dec_o, cd_o):
        h = pl.program_id(1)

        def prep(i, carry):
            chains, qs, k, v, kk, qk, g, beta = [], [], [], [], [], [], [], []
            for j in range(per):
                n = i * per + j
                rows = pl.ds(pl.multiple_of(n * c, c), c)
                kj, vj = k_ref[rows, :], v_ref[rows, :]
                qj = q_ref[rows, :] * (128 ** -0.5)
                kkj, qkj = _mm_nt(kj, kj), _mm_nt(qj, kj)
                ba = ba_ref[rows, :]
                for d in range(2):
                    bd, gd, _, _ = _gate_cols(ba, al_ref[...], dt_ref[...], d * hh + h, 2 * hh + d * hh + h)
                    chains.append((d, n, rows))
                    for lst, val in ((qs, qj), (k, kj), (v, vj), (kk, kkj), (qk, qkj), (g, gd), (beta, bd)):
                        lst.append(val)
            z = _gdn_prep(qs, k, v, g, beta, kk, qk, [d == 1 for d, _, _ in chains])
            for x, (d, n, rows) in enumerate(chains):
                u_o[d, rows, :] = z["u"][x]
                w_o[d, rows, :] = z["w"][x].astype(bf16)
                qd_o[d, rows, :] = z["qd"][x].astype(bf16)
                kd_o[d, rows, :] = z["kd"][x].astype(bf16)
                p_o[d, 0, n] = z["p"][x].astype(bf16)
                t_o[d, 0, n] = z["tm"][x]
                dec_o[d, 0, n] = z["dec"][x]
                cd_o[d, 0, n] = jnp.broadcast_to(z["cd"][x], (8, LANES))
            return carry

        lax.fori_loop(0, nc // per, prep, 0)
        o_ref[...] = jnp.zeros_like(o_ref)

        def step(i, sts):
            ns = [i, nc - 1 - i]
            rows = [pl.ds(pl.multiple_of(n * c, c), c) for n in ns]
            vn = [u_o[d, rows[d], :] - _mm(w_o[d, rows[d], :], sts[d]) for d in range(2)]
            o = [_mm(qd_o[d, rows[d], :], sts[d]) + _mm(p_o[d, 0, ns[d]], vn[d]) for d in range(2)]
            new = [sts[d] * cd_o[d, 0, ns[d]][0:1, :] + _mm_tn(kd_o[d, rows[d], :], vn[d]) for d in range(2)]
            for d in range(2):
                o_ref[rows[d], :] = o_ref[rows[d], :] + o[d]
            return tuple(new)

        z0 = jnp.zeros((128, 128), f32)
        lax.fori_loop(0, nc, step, (z0, z0))

    blk = lambda off: pl.BlockSpec((s, 128), functools.partial(lambda b, h, off: (b, off + h), off=off))
    row = pl.BlockSpec((1, LANES), lambda b, h: (0, 0))
    tok2, mat2, cd2, shp = _gdn_specs(cfg)
    return pl.pallas_call(
        body, name="gdn_fwd", grid=(cfg.BL, hh),
        in_specs=[blk(0), blk(hh), blk(2 * hh), pl.BlockSpec((s, LANES), lambda b, h: (b, ba_blk)), row, row],
        out_specs=[pl.BlockSpec((s, 128), lambda b, h: (b, h)), tok2, tok2, tok2, tok2, mat2, mat2, mat2, cd2],
        out_shape=[jax.ShapeDtypeStruct((cfg.T, cfg.DN), f32), shp["tok32"], shp["tok16"], shp["tok16"], shp["tok16"],
                   shp["mat16"], shp["mat32"], shp["mat32"], shp["cd"]],
        compiler_params=_params(("arbitrary", "arbitrary")),
    )(qkv, qkv, qkv, proj, alog_row, dt_row)


def _lru_dir_fwd(xc, wa, wx, ba, bx, lam, rev):
    ra = _sigmoid(_mm(xc, wa) + ba)
    ig = _sigmoid(_mm(xc, wx) + bx)
    sp = _softplus(-lam)
    la = -LRU_C * ra * sp
    a = jnp.exp(la)
    m = jnp.sqrt(_nexpm1(2.0 * la))
    gx = ig * xc
    h = _lin_scan(a, m * gx, rev)
    return dict(ra=ra, ig=ig, sp=sp, a=a, m=m, gx=gx, h=h)


def _lru_specs(cfg, outer_b):
    s = cfg.S
    lx_blk = cfg.LXO // LANES
    if outer_b:
        ix = lambda f: (lambda b, ct: f(b, ct))
    else:
        ix = lambda f: (lambda ct, b: f(b, ct))
    return dict(
        lx=pl.BlockSpec((s, LANES), ix(lambda b, ct: (b, lx_blk + ct))),
        tok=pl.BlockSpec((s, LANES), ix(lambda b, ct: (b, ct))),
        cw=pl.BlockSpec((4, LANES), ix(lambda b, ct: (0, ct))),
        row=pl.BlockSpec((1, LANES), ix(lambda b, ct: (0, ct))),
        w=pl.BlockSpec((2, 1, LANES, LANES), ix(lambda b, ct: (0, ct, 0, 0))),
        two=pl.BlockSpec((2, LANES), ix(lambda b, ct: (0, ct))),
    )


def _k_lru_fwd(cfg, proj, conv_w, conv_b, wa, wx, ba, bx, lam):
    def body(lx_ref, cw_ref, cb_ref, wa_ref, wx_ref, ba_ref, bx_ref, lam_ref, o_ref):
        xc = _conv_fwd(lx_ref[...], cw_ref, 2) + cb_ref[...]
        acc = None
        for d in range(2):
            z = _lru_dir_fwd(xc, wa_ref[d, 0], wx_ref[d, 0], ba_ref[d:d + 1, :], bx_ref[d:d + 1, :], lam_ref[d:d + 1, :], d == 1)
            acc = z["h"] if acc is None else acc + z["h"]
        o_ref[...] = acc

    sp = _lru_specs(cfg, True)
    return pl.pallas_call(
        body, name="lru_fwd", grid=(cfg.BL, cfg.LW // LANES),
        in_specs=[sp["lx"], sp["cw"], sp["row"], sp["w"], sp["w"], sp["two"], sp["two"], sp["two"]],
        out_specs=sp["tok"], out_shape=jax.ShapeDtypeStruct((cfg.T, cfg.LW), f32),
        compiler_params=_params(("arbitrary", "arbitrary")),
    )(proj, conv_w, conv_b, wa, wx, ba, bx, lam)


def _mix_parts(cfg, o, z, lg, hs, dng, lrg):
    heads = []
    for h in range(cfg.H):
        sl = slice(h * 128, (h + 1) * 128)
        y, xh, inv = _rms_fwd(o[:, sl], dng)
        sz, dsz = _silu(z[:, sl])
        heads.append((y, xh, inv, sz, dsz))
    gl, dgl = _gelu(lg)
    y2, xh2, inv2 = _rms_fwd(gl * hs, lrg)
    return heads, (y2, xh2, inv2, gl, dgl)


def _k_mix(cfg, o, proj, hs, r, dng, lrg, wout):
    dn = cfg.DN

    def body(o_ref, z_ref, lg_ref, hs_ref, r_ref, dng_ref, lrg_ref, w_ref, mix_ref, out_ref):
        heads, lru = _mix_parts(cfg, o_ref[...], z_ref[...], lg_ref[...], hs_ref[...], dng_ref[...], lrg_ref[...])
        for h, (y, _, _, sz, _) in enumerate(heads):
            mix_ref[:, h * 128:(h + 1) * 128] = (y * sz).astype(bf16)
        mix_ref[:, dn:] = lru[0].astype(bf16)
        out_ref[...] = r_ref[...] + jnp.dot(mix_ref[...], w_ref[...], preferred_element_type=f32)

    return _row_call("mix_out", body, cfg.T, cfg.TM,
                     [(o, dn, 0), (proj, dn, cfg.ZO // dn), (proj, cfg.LW, cfg.LGO // cfg.LW), (hs, cfg.LW, 0), (r, cfg.D, 0)],
                     [dng, lrg, wout], [(cfg.D, bf16), (cfg.D, f32)], [])


def _k_ffn_a(cfg, r, g2, wg, wu):
    def body(r_ref, g_ref, wg_ref, wu_ref, h_ref, gp_ref, up_ref):
        y, _, _ = _rms_fwd(r_ref[...], g_ref[...])
        hb = y.astype(bf16)
        h_ref[...] = hb
        gp_ref[...] = _mm_nt(hb, wg_ref[...])
        up_ref[...] = _mm_nt(hb, wu_ref[...])

    return _row_call("ffn_in", body, cfg.T, cfg.TMF, [(r, cfg.D, 0)], [g2, wg, wu],
                     [(cfg.D, bf16), (cfg.FF, f32), (cfg.FF, f32)], [])


def _k_ffn_b(cfg, gp, up, conv_w, conv_b):
    s, ft = cfg.S, cfg.FT

    def body(gp_ref, up_ref, w_ref, b_ref, o_ref):
        gate = _conv_fwd(gp_ref[...], w_ref, 1) + b_ref[...]
        gl, _ = _gelu(gate)
        o_ref[...] = (gl * up_ref[...]).astype(bf16)

    tok = pl.BlockSpec((s, ft), lambda b, j: (b, j))
    return pl.pallas_call(
        body, name="ffn_act", grid=(cfg.BL, cfg.FF // ft),
        in_specs=[tok, tok, pl.BlockSpec((3, ft), lambda b, j: (0, j)), pl.BlockSpec((1, ft), lambda b, j: (0, j))],
        out_specs=tok, out_shape=jax.ShapeDtypeStruct((cfg.T, cfg.FF), bf16),
        compiler_params=_params(("arbitrary", "arbitrary")),
    )(gp, up, conv_w, conv_b)


def _k_ffn_c(cfg, act, r, wd):
    def body(a_ref, r_ref, w_ref, o_ref):
        o_ref[...] = r_ref[...] + jnp.dot(a_ref[...], w_ref[...], preferred_element_type=f32)

    return _row_call("ffn_out", body, cfg.T, cfg.TM, [(act, cfg.FF, 0), (r, cfg.D, 0)], [wd], [(cfg.D, f32)], [])[0]


def _k_ple(cfg, r, p, gp, wpg, bg, wpp):
    def body(r_ref, p_ref, g_ref, wg_ref, bg_ref, wp_ref, pn_ref, o_ref):
        x = r_ref[...]
        y, _, _ = _rms_fwd(x, g_ref[...])
        pn = y.astype(bf16)
        pn_ref[...] = pn
        pg = _sigmoid(jnp.dot(pn, wg_ref[...], preferred_element_type=f32) + bg_ref[...])
        o_ref[...] = x + pg * _mm_nt(p_ref[...], wp_ref[...])

    return _row_call("ple", body, cfg.T, cfg.TM, [(r, cfg.D, 0), (p, cfg.PD, 0)], [gp, wpg, bg, wpp],
                     [(cfg.D, bf16), (cfg.D, f32)], [])


def _k_loss(cfg, r, tgt, gf):
    d = cfg.D

    def body(r_ref, t_ref, g_ref, dr_ref, loss_ref, dg_ref):
        @pl.when(pl.program_id(0) == 0)
        def _():
            loss_ref[...] = jnp.zeros_like(loss_ref)
            dg_ref[...] = jnp.zeros_like(dg_ref)

        g = g_ref[...]
        y, xh, inv = _rms_fwd(r_ref[...], g)
        err = y - t_ref[...]
        loss_ref[...] = loss_ref[...] + (0.5 / d) * jnp.sum(err * err)
        dx, dg = _rms_bwd(err * (1.0 / d), xh, inv, g)
        dr_ref[...] = dx
        dg_ref[...] = dg_ref[...] + dg

    return _row_call("loss_head", body, cfg.T, cfg.TM, [(r, d, 0), (tgt, d, 0)], [gf], [(d, f32)],
                     [((1, LANES), f32), ((1, d), f32)])


def _zero_at_first(cond, *refs):
    @pl.when(cond)
    def _():
        for r in refs:
            r[...] = jnp.zeros_like(r)


def _b_ple(cfg, dr3, r2, p, gp, wpg, bg, wpp):
    d = cfg.D

    def body(dr_ref, r_ref, p_ref, g_ref, wg_ref, bg_ref, wp_ref, dr2_ref, dlog_ref, dpp_ref, dgp_ref, dbg_ref):
        _zero_at_first(pl.program_id(0) == 0, dgp_ref, dbg_ref)
        g = g_ref[...]
        dr = dr_ref[...]
        y, xh, inv = _rms_fwd(r_ref[...], g)
        pg = _sigmoid(jnp.dot(y.astype(bf16), wg_ref[...], preferred_element_type=f32) + bg_ref[...])
        pp = _mm_nt(p_ref[...], wp_ref[...])
        dpp_ref[...] = (dr * pg).astype(bf16)
        dlog = dr * pp * pg * (1.0 - pg)
        dlog_ref[...] = dlog.astype(bf16)
        dbg_ref[...] = dbg_ref[...] + jnp.sum(dlog, axis=0, keepdims=True)
        dx, dg = _rms_bwd(_mm_nt(dlog, wg_ref[...]), xh, inv, g)
        dgp_ref[...] = dgp_ref[...] + dg
        dr2_ref[...] = dr + dx

    return _row_call("ple_bwd", body, cfg.T, cfg.TM, [(dr3, d, 0), (r2, d, 0), (p, cfg.PD, 0)], [gp, wpg, bg, wpp],
                     [(d, f32), (d, bf16), (d, bf16)], [((1, d), f32), ((1, d), f32)])


def _b_ffn_bc(cfg, dr2, wd, gp, up, conv_w, conv_b):
    s, ft, d = cfg.S, cfg.FT, cfg.D

    def body(dr_ref, wd_ref, gp_ref, up_ref, w_ref, b_ref, dgp_ref, dup_ref, dwd_ref, dcw_ref, dcb_ref):
        _zero_at_first(pl.program_id(1) == 0, dwd_ref, dcw_ref, dcb_ref)
        drb = dr_ref[...].astype(bf16)
        dact = _mm_nt(drb, wd_ref[...])
        gpre = gp_ref[...]
        up = up_ref[...]
        gl, dgl = _gelu(_conv_fwd(gpre, w_ref, 1) + b_ref[...])
        dup_ref[...] = (dact * gl).astype(bf16)
        dgate = dact * up * dgl
        dcb_ref[...] = dcb_ref[...] + jnp.sum(dgate, axis=0, keepdims=True)
        dx, dws = _conv_bwd(dgate, gpre, w_ref, 1)
        for j, dw in enumerate(dws):
            dcw_ref[j:j + 1, :] = dcw_ref[j:j + 1, :] + dw
        dgp_ref[...] = dx.astype(bf16)
        dwd_ref[...] = dwd_ref[...] + _mm_tn((gl * up).astype(bf16), drb)

    tok = pl.BlockSpec((s, ft), lambda j, b: (b, j))
    return pl.pallas_call(
        body, name="ffn_act_bwd", grid=(cfg.FF // ft, cfg.BL),
        in_specs=[pl.BlockSpec((s, d), lambda j, b: (b, 0)), pl.BlockSpec((ft, d), lambda j, b: (j, 0)), tok, tok,
                  pl.BlockSpec((3, ft), lambda j, b: (0, j)), pl.BlockSpec((1, ft), lambda j, b: (0, j))],
        out_specs=[tok, tok, pl.BlockSpec((ft, d), lambda j, b: (j, 0)), pl.BlockSpec((3, ft), lambda j, b: (0, j)),
                   pl.BlockSpec((1, ft), lambda j, b: (0, j))],
        out_shape=[jax.ShapeDtypeStruct((cfg.T, cfg.FF), bf16), jax.ShapeDtypeStruct((cfg.T, cfg.FF), bf16),
                   jax.ShapeDtypeStruct((cfg.FF, d), f32), jax.ShapeDtypeStruct((3, cfg.FF), f32),
                   jax.ShapeDtypeStruct((1, cfg.FF), f32)],
        compiler_params=_params(("arbitrary", "arbitrary")),
    )(dr2, wd, gp, up, conv_w, conv_b)


def _b_ffn_a(cfg, dgp, dup, dr2, r_mid, g2, wg, wu):
    d = cfg.D

    def body(dgp_ref, dup_ref, dr_ref, r_ref, g_ref, wg_ref, wu_ref, o_ref, dg_ref):
        _zero_at_first(pl.program_id(0) == 0, dg_ref)
        g = g_ref[...]
        _, xh, inv = _rms_fwd(r_ref[...], g)
        dh = _mm(dgp_ref[...], wg_ref[...]) + _mm(dup_ref[...], wu_ref[...])
        dx, dg = _rms_bwd(dh, xh, inv, g)
        dg_ref[...] = dg_ref[...] + dg
        o_ref[...] = dr_ref[...] + dx

    return _row_call("ffn_in_bwd", body, cfg.T, cfg.TMF, [(dgp, cfg.FF, 0), (dup, cfg.FF, 0), (dr2, d, 0), (r_mid, d, 0)],
                     [g2, wg, wu], [(d, f32)], [((1, d), f32)])


def _mm_tn_call(cfg, name, x, dy, tn):
    t, k = x.shape
    n = dy.shape[1]
    tm = cfg.TM

    def body(x_ref, dy_ref, o_ref):
        _zero_at_first(pl.program_id(1) == 0, o_ref)
        o_ref[...] = o_ref[...] + _mm_tn(x_ref[...], dy_ref[...])

    return pl.pallas_call(
        body, name=name, grid=(n // tn, t // tm),
        in_specs=[pl.BlockSpec((tm, k), lambda j, i: (i, 0)), pl.BlockSpec((tm, tn), lambda j, i: (i, j))],
        out_specs=pl.BlockSpec((k, tn), lambda j, i: (0, j)),
        out_shape=jax.ShapeDtypeStruct((k, n), f32),
        compiler_params=_params(("arbitrary", "arbitrary")),
    )(x, dy)


def _b_mix(cfg, dr, o, proj, hs, dng, lrg, wout):
    dn, lw, d = cfg.DN, cfg.LW, cfg.D

    def body(dr_ref, o_ref, z_ref, lg_ref, hs_ref, dng_ref, lrg_ref, w_ref, do_ref, dz_ref, dlg_ref, dhs_ref, ddn_ref, dlr_ref):
        _zero_at_first(pl.program_id(0) == 0, ddn_ref, dlr_ref)
        dng, lrg = dng_ref[...], lrg_ref[...]
        hs = hs_ref[...]
        o, z = o_ref[...], z_ref[...]
        heads, lru = _mix_parts(cfg, o, z, lg_ref[...], hs, dng, lrg)
        drb = dr_ref[...].astype(bf16)
        dmix_dn = _mm_nt(drb, w_ref[0:dn, :])
        dmix_lr = _mm_nt(drb, w_ref[dn:, :])
        dgn = jnp.zeros_like(dng)
        for h, (y, xh, inv, sz, dsz) in enumerate(heads):
            sl = slice(h * 128, (h + 1) * 128)
            dm = dmix_dn[:, sl]
            dz_ref[:, sl] = dm * y * dsz
            dx, dg = _rms_bwd(dm * sz, xh, inv, dng)
            do_ref[:, sl] = dx
            dgn = dgn + dg
        ddn_ref[...] = ddn_ref[...] + dgn
        _, xh2, inv2, gl, dgl = lru
        dx2, dg2 = _rms_bwd(dmix_lr, xh2, inv2, lrg)
        dlr_ref[...] = dlr_ref[...] + dg2
        dlg_ref[...] = dx2 * hs * dgl
        dhs_ref[...] = dx2 * gl

    return _row_call("mix_bwd", body, cfg.T, cfg.TM,
                     [(dr, d, 0), (o, dn, 0), (proj, dn, cfg.ZO // dn), (proj, lw, cfg.LGO // lw), (hs, lw, 0)],
                     [dng, lrg, wout], [(dn, f32), (dn, f32), (lw, f32), (lw, f32)], [((1, 128), f32), ((1, lw), f32)])


def _b_lru(cfg, proj, dhs, conv_w, conv_b, wa, wx, ba, bx, lam):
    def body(lx_ref, dh_ref, cw_ref, cb_ref, wa_ref, wx_ref, ba_ref, bx_ref, lam_ref,
             dlx_ref, dcw_ref, dcb_ref, dwa_ref, dwx_ref, dba_ref, dbx_ref, dlam_ref):
        _zero_at_first(pl.program_id(1) == 0, dcw_ref, dcb_ref, dwa_ref, dwx_ref, dba_ref, dbx_ref, dlam_ref)
        lx = lx_ref[...]
        dh = dh_ref[...]
        xc = _conv_fwd(lx, cw_ref, 2) + cb_ref[...]
        dxc = jnp.zeros_like(xc)
        for d in range(2):
            rev = d == 1
            lam = lam_ref[d:d + 1, :]
            z = _lru_dir_fwd(xc, wa_ref[d, 0], wx_ref[d, 0], ba_ref[d:d + 1, :], bx_ref[d:d + 1, :], lam, rev)
            a, m, ra, ig, sp = z["a"], z["m"], z["ra"], z["ig"], z["sp"]
            a_next = _shift(a, -1 if rev else 1, 0.0)
            lmb = _lin_scan(a_next, dh, not rev)
            h_prev = _shift(z["h"], 1 if rev else -1, 0.0)
            da = lmb * h_prev
            dm = lmb * z["gx"]
            dgx = lmb * m
            dla = da * a - dm * (a * a) / jnp.maximum(m, 1e-30)
            dra = dla * (-LRU_C) * sp
            dsp = jnp.sum(dla * (-LRU_C) * ra, axis=0, keepdims=True)
            dlam_ref[d:d + 1, :] = dlam_ref[d:d + 1, :] - dsp * _sigmoid(-lam)
            dpa = dra * ra * (1.0 - ra)
            dpx = dgx * xc * ig * (1.0 - ig)
            dba_ref[d:d + 1, :] = dba_ref[d:d + 1, :] + jnp.sum(dpa, axis=0, keepdims=True)
            dbx_ref[d:d + 1, :] = dbx_ref[d:d + 1, :] + jnp.sum(dpx, axis=0, keepdims=True)
            dwa_ref[d, 0] = dwa_ref[d, 0] + _mm_tn(xc, dpa)
            dwx_ref[d, 0] = dwx_ref[d, 0] + _mm_tn(xc, dpx)
            dxc = dxc + dgx * ig + _mm_nt(dpa, wa_ref[d, 0]) + _mm_nt(dpx, wx_ref[d, 0])
        dcb_ref[...] = dcb_ref[...] + jnp.sum(dxc, axis=0, keepdims=True)
        dx, dws = _conv_bwd(dxc, lx, cw_ref, 2)
        for j, dw in enumerate(dws):
            dcw_ref[j:j + 1, :] = dcw_ref[j:j + 1, :] + dw
        dlx_ref[...] = dx

    sp = _lru_specs(cfg, False)
    nct = cfg.LW // LANES
    return pl.pallas_call(
        body, name="lru_bwd", grid=(nct, cfg.BL),
        in_specs=[sp["lx"], sp["tok"], sp["cw"], sp["row"], sp["w"], sp["w"], sp["two"], sp["two"], sp["two"]],
        out_specs=[sp["tok"], sp["cw"], sp["row"], sp["w"], sp["w"], sp["two"], sp["two"], sp["two"]],
        out_shape=[jax.ShapeDtypeStruct((cfg.T, cfg.LW), f32), jax.ShapeDtypeStruct((4, cfg.LW), f32),
                   jax.ShapeDtypeStruct((1, cfg.LW), f32), jax.ShapeDtypeStruct((2, nct, LANES, LANES), f32),
                   jax.ShapeDtypeStruct((2, nct, LANES, LANES), f32), jax.ShapeDtypeStruct((2, cfg.LW), f32),
                   jax.ShapeDtypeStruct((2, cfg.LW), f32), jax.ShapeDtypeStruct((2, cfg.LW), f32)],
        compiler_params=_params(("arbitrary", "arbitrary")),
    )(proj, dhs, conv_w, conv_b, wa, wx, ba, bx, lam)


def _b_gdn(cfg, qkv, proj, do, alog_row, dt_row, saved):
    s, c, nc, hh = cfg.S, cfg.C, cfg.NC, cfg.H
    ba_blk = cfg.BAO // LANES
    scale = 128 ** -0.5
    per = 4 if nc % 4 == 0 else 1

    def body(q_ref, k_ref, v_ref, ba_ref, do_ref, al_ref, dt_ref, u_i, w_i, qd_i, kd_i, p_i, t_i, dec_i, cd_i,
             dqkv_ref, dba_ref, dal_ref, ddt_ref, vn_s, dvn_s, st_s, dst_s):
        h = pl.program_id(1)
        _zero_at_first((pl.program_id(0) == 0) & (h == 0), dal_ref, ddt_ref)
        _zero_at_first(h == 0, dba_ref)
        lane = lax.broadcasted_iota(jnp.int32, (c, LANES), 1)
        lane1 = lax.broadcasted_iota(jnp.int32, (1, LANES), 1)
        dirs = (0, 1)
        lane_b = [d * hh + h for d in dirs]
        lane_a = [2 * hh + d * hh + h for d in dirs]

        def fstep(i, sts):
            ns = [i, nc - 1 - i]
            rows = [pl.ds(pl.multiple_of(n * c, c), c) for n in ns]
            for d in dirs:
                st_s[d, ns[d]] = sts[d]
            vn = [u_i[d, rows[d], :] - _mm(w_i[d, rows[d], :], sts[d]) for d in dirs]
            for d in dirs:
                vn_s[d, rows[d], :] = vn[d]
            return tuple(sts[d] * cd_i[d, 0, ns[d]][0:1, :] + _mm_tn(kd_i[d, rows[d], :], vn[d]) for d in dirs)

        z0 = jnp.zeros((128, 128), f32)
        lax.fori_loop(0, nc, fstep, (z0, z0))

        def bseq(i, dst):
            ns = [nc - 1 - i, i]
            rows = [pl.ds(pl.multiple_of(n * c, c), c) for n in ns]
            for d in dirs:
                dst_s[d, ns[d]] = dst[d]
            dob = [do_ref[rows[d], :] for d in dirs]
            dvn = [_mm_tn(p_i[d, 0, ns[d]], dob[d]) + _mm(kd_i[d, rows[d], :], dst[d]) for d in dirs]
            for d in dirs:
                dvn_s[d, rows[d], :] = dvn[d]
            return tuple(_mm_tn(qd_i[d, rows[d], :], dob[d]) + cd_i[d, 0, ns[d]][0:1, 0:1] * dst[d]
                         - _mm_tn(w_i[d, rows[d], :], dvn[d]) for d in dirs)

        lax.fori_loop(0, nc, bseq, (z0, z0))

        def bpar(i, carry):
            ch = [(d, i * per + j) for j in range(per) for d in dirs]
            rows = [pl.ds(pl.multiple_of(n * c, c), c) for _, n in ch]
            masks = [_tri_masks(c, d == 1) for d, _ in ch]
            ld = lambda ref: [ref[d, r, :] for (d, _), r in zip(ch, rows)]
            ldm = lambda ref: [ref[d, 0, n] for d, n in ch]
            q, k, v, dob = ([ref[r, :] for r in rows] for ref in (q_ref, k_ref, v_ref, do_ref))
            gates = [_gate_cols(ba_ref[r, :], al_ref[...], dt_ref[...], lane_b[d], lane_a[d]) for (d, _), r in zip(ch, rows)]
            beta, g, aexp, xa = ([gt[x] for gt in gates] for x in range(4))
            st, dst = ([ref[d, n] for d, n in ch] for ref in (st_s, dst_s))
            vn, dvn, u, w = ld(vn_s), ld(dvn_s), ld(u_i), ld(w_i)
            p = [x.astype(f32) for x in ldm(p_i)]
            tm, dec = ldm(t_i), ldm(dec_i)
            cd = [cd_i[d, 0, n][0:1, 0:1] for d, n in ch]
            dp = _map(lambda m, a, b: jnp.where(m[0], _mm_nt(a, b), 0.0), masks, dob, vn)
            dqd = _map(_mm_nt, dob, st)
            dkd = _map(_mm_nt, vn, dst)
            dw = _map(lambda a, b: -_mm_nt(a, b), dvn, st)
            dcd = _map(lambda a, b: jnp.sum(jnp.sum(a * b, axis=1, keepdims=True), axis=0, keepdims=True), st, dst)
            gcum = _map(lambda m, x: _cum_mm(m[0], jnp.broadcast_to(x, (c, c)))[:, 0:1], masks, g)
            glast = [jnp.sum(x, axis=0, keepdims=True) for x in g]
            e = [jnp.exp(x) for x in gcum]
            el = _map(lambda a, b: jnp.exp(a - b), glast, gcum)
            qs = [x * scale for x in q]
            kb = _map(lambda a, b: a * b, k, beta)
            a = _map(lambda m, b, ki, dc: jnp.where(m[1], b * _mm_nt(ki, ki) * dc, 0.0), masks, beta, k, dec)
            dvb = _map(_hmm3_tn, tm, dvn)
            dkbe = _map(_hmm3_tn, tm, dw)
            da = _map(lambda m, x, ui, y, wi: -jnp.where(m[1], _mm_nt(x, ui) + _mm_nt(y, wi), 0.0), masks, dvb, u, dkbe, w)
            g1 = _map(lambda x, y: x * y, da, dec)
            g2 = _map(lambda x, y: x * y, dp, dec)
            dkb = _map(lambda x, ki, y, ei: _mm(x, ki) + y * ei, g1, k, dkbe, e)
            dk = _map(lambda x, kbi, y, qi, z, b, t, l: _mm_tn(x, kbi) + _mm_tn(y, qi) + z * b + t * l,
                      g1, kb, g2, qs, dkb, beta, dkd, el)
            dqs = _map(lambda y, ki, x, ei: _mm(y, ki) + x * ei, g2, k, dqd, e)
            ddd = _map(lambda x, ai, y, pi: x * ai + y * pi, da, a, dp, p)
            ones = jnp.ones((c, LANES), f32)
            dgcum = _map(lambda x: jnp.sum(x, axis=1, keepdims=True) - _hmm3_tn(x, ones)[:, 0:1], ddd)
            for x, (d, n) in enumerate(ch):
                dbeta = jnp.sum(dvb[x] * v[x], axis=1, keepdims=True) + jnp.sum(dkb[x] * k[x], axis=1, keepdims=True)
                de = jnp.sum(dkbe[x] * kb[x], axis=1, keepdims=True) + jnp.sum(dqd[x] * qs[x], axis=1, keepdims=True)
                del_ = jnp.sum(dkd[x] * k[x], axis=1, keepdims=True)
                dgc = dgcum[x] + de * e[x] - del_ * el[x]
                dglast = jnp.sum(del_ * el[x], axis=0, keepdims=True) + dcd[x] * cd[x]
                dg = _cum_mm(masks[x][2], jnp.broadcast_to(dgc, (c, LANES)))[:, 0:1] + dglast
                r = rows[x]
                if d == 0:
                    dqkv_ref[0, r, :] = dqs[x] * scale
                    dqkv_ref[1, r, :] = dk[x]
                    dqkv_ref[2, r, :] = dvb[x] * beta[x]
                else:
                    dqkv_ref[0, r, :] = dqkv_ref[0, r, :] + dqs[x] * scale
                    dqkv_ref[1, r, :] = dqkv_ref[1, r, :] + dk[x]
                    dqkv_ref[2, r, :] = dqkv_ref[2, r, :] + dvb[x] * beta[x]
                dlb = dbeta * beta[x] * (1.0 - beta[x])
                dalpha = -dg * aexp[x] * _sigmoid(xa[x])
                dba_ref[r, :] = dba_ref[r, :] + jnp.where(lane == lane_b[d], dlb, 0.0) + jnp.where(lane == lane_a[d], dalpha, 0.0)
                dal_ref[...] = dal_ref[...] + jnp.where(lane1 == lane_a[d], jnp.sum(dg * g[x], axis=0, keepdims=True), 0.0)
                ddt_ref[...] = ddt_ref[...] + jnp.where(lane1 == lane_a[d], jnp.sum(dalpha, axis=0, keepdims=True), 0.0)
            return carry

        lax.fori_loop(0, nc // per, bpar, 0)

    blk = lambda off: pl.BlockSpec((s, 128), functools.partial(lambda b, h, off: (b, off + h), off=off))
    row = pl.BlockSpec((1, LANES), lambda b, h: (0, 0))
    tok2, mat2, cd2, _ = _gdn_specs(cfg)
    return pl.pallas_call(
        body, name="gdn_bwd", grid=(cfg.BL, hh),
        in_specs=[blk(0), blk(hh), blk(2 * hh), pl.BlockSpec((s, LANES), lambda b, h: (b, ba_blk)),
                  pl.BlockSpec((s, 128), lambda b, h: (b, h)), row, row, tok2, tok2, tok2, tok2, mat2, mat2, mat2, cd2],
        out_specs=[pl.BlockSpec((3, s, 128), lambda b, h: (0, b, h)), pl.BlockSpec((s, LANES), lambda b, h: (b, 0)), row, row],
        out_shape=[jax.ShapeDtypeStruct((3, cfg.T, cfg.DN), f32), jax.ShapeDtypeStruct((cfg.T, LANES), f32),
                   jax.ShapeDtypeStruct((1, LANES), f32), jax.ShapeDtypeStruct((1, LANES), f32)],
        scratch_shapes=[pltpu.VMEM((2, s, 128), f32)] * 2 + [pltpu.VMEM((2, nc, 128, 128), f32)] * 2,
        compiler_params=_params(("arbitrary", "arbitrary")),
    )(qkv, qkv, qkv, proj, do, alog_row, dt_row, *saved)


def _b_prep(cfg, proj, dqkv, conv_w):
    dn, s = cfg.DN, cfg.S

    def body(x_ref, dy_ref, w_ref, dx_ref, dw_ref):
        _zero_at_first(pl.program_id(1) == 0, dw_ref)
        sec = pl.program_id(0)
        x = x_ref[...]
        c = _conv_fwd(x, w_ref, 2)
        y, dsilu = _silu(c)
        dy = dy_ref[0]
        parts = []
        for h in range(cfg.H):
            sl = slice(h * 128, (h + 1) * 128)
            yh, dyh = y[:, sl], dy[:, sl]
            inv = lax.rsqrt(jnp.sum(yh * yh, axis=1, keepdims=True) + EPS)
            dn_h = inv * dyh - yh * (inv * inv * inv) * jnp.sum(dyh * yh, axis=1, keepdims=True)
            parts.append(jnp.where(sec < 2, dn_h, dyh))
        ds = jnp.concatenate(parts, axis=1) if len(parts) > 1 else parts[0]
        dx, dws = _conv_bwd(ds * dsilu, x, w_ref, 2)
        for j, dw in enumerate(dws):
            dw_ref[j:j + 1, :] = dw_ref[j:j + 1, :] + dw
        dx_ref[...] = dx

    return pl.pallas_call(
        body, name="dn_prep_bwd", grid=(3, cfg.BL),
        in_specs=[pl.BlockSpec((s, dn), lambda j, b: (b, j)), pl.BlockSpec((1, s, dn), lambda j, b: (j, b, 0)),
                  pl.BlockSpec((4, dn), lambda j, b: (0, j))],
        out_specs=[pl.BlockSpec((s, dn), lambda j, b: (b, j)), pl.BlockSpec((4, dn), lambda j, b: (0, j))],
        out_shape=[jax.ShapeDtypeStruct((cfg.T, 3 * dn), f32), jax.ShapeDtypeStruct((4, 3 * dn), f32)],
        compiler_params=_params(("arbitrary", "arbitrary")),
    )(proj, dqkv, conv_w)


def _b_in(cfg, dqkv_pre, dz, dlx, dlg, dba, dr_mid, r_in, g1, wcat):
    d, dn, lw = cfg.D, cfg.DN, cfg.LW

    def body(dq_ref, dz_ref, dlx_ref, dlg_ref, dba_ref, dr_ref, r_ref, g_ref, w_ref, o_ref, dp_ref, dg_ref):
        _zero_at_first(pl.program_id(0) == 0, dg_ref)
        dp_ref[:, 0:cfg.ZO] = dq_ref[...].astype(bf16)
        dp_ref[:, cfg.ZO:cfg.LXO] = dz_ref[...].astype(bf16)
        dp_ref[:, cfg.LXO:cfg.LGO] = dlx_ref[...].astype(bf16)
        dp_ref[:, cfg.LGO:cfg.BAO] = dlg_ref[...].astype(bf16)
        dp_ref[:, cfg.BAO:] = dba_ref[...].astype(bf16)
        g = g_ref[...]
        _, xh, inv = _rms_fwd(r_ref[...], g)
        dx, dg = _rms_bwd(_mm(dp_ref[...], w_ref[...]), xh, inv, g)
        dg_ref[...] = dg_ref[...] + dg
        o_ref[...] = dr_ref[...] + dx

    return _row_call("in_proj_bwd", body, cfg.T, cfg.TM,
                     [(dqkv_pre, 3 * dn, 0), (dz, dn, 0), (dlx, lw, 0), (dlg, lw, 0), (dba, LANES, 0), (dr_mid, d, 0), (r_in, d, 0)],
                     [g1, wcat], [(d, f32), (cfg.PC, bf16)], [((1, d), f32)])


def _adam_call(name, w, g, m, v, tr):
    rows, cols = w.shape
    bc1 = 1.0 - ADAM_B1 ** ADAM_STEP
    bc2 = 1.0 - ADAM_B2 ** ADAM_STEP

    def body(w_ref, g_ref, m_ref, v_ref, d_ref, nm_ref, nv_ref):
        g = g_ref[...]
        m = ADAM_B1 * m_ref[...] + (1.0 - ADAM_B1) * g
        v = ADAM_B2 * v_ref[...] + (1.0 - ADAM_B2) * (g * g)
        nm_ref[...] = m
        nv_ref[...] = v
        d_ref[...] = -ADAM_LR * ((m / bc1) / (jnp.sqrt(v / bc2) + ADAM_EPS) + ADAM_WD * w_ref[...])

    spec = pl.BlockSpec((tr, cols), lambda i: (i, 0))
    return pl.pallas_call(
        body, name=name, grid=(rows // tr,), in_specs=[spec] * 4, out_specs=[spec] * 3,
        out_shape=[jax.ShapeDtypeStruct((rows, cols), f32)] * 3, compiler_params=_params(("arbitrary",)),
    )(w, g, m, v)


def _sum8_call(name, x, tr):
    _, rows, cols = x.shape

    def body(x_ref, o_ref):
        acc = x_ref[0].astype(f32)
        for j in range(1, N_DEV):
            acc = acc + x_ref[j].astype(f32)
        o_ref[...] = acc

    return pl.pallas_call(
        body, name=name, grid=(rows // tr,), in_specs=[pl.BlockSpec((N_DEV, tr, cols), lambda i: (0, i, 0))],
        out_specs=pl.BlockSpec((tr, cols), lambda i: (i, 0)), out_shape=jax.ShapeDtypeStruct((rows, cols), f32),
        compiler_params=_params(("arbitrary",)),
    )(x)


def _all_gather(name, shards):
    na = len(shards)

    def body(*refs):
        xs, outs = refs[:na], refs[na:2 * na]
        send_sems, recv_sems, local_sems = refs[2 * na:]
        x, y, c = lax.axis_index("x"), lax.axis_index("y"), lax.axis_index("c")
        me, sibling = (x, y, c), (x, y, 1 - c)
        chips = [(1 - x, y), (x, 1 - y), (1 - x, 1 - y)]

        def copy(a, k, block, to, src=None):
            px, py, pc = block
            slot = outs[a].at[4 * px + 2 * py + pc]
            return pltpu.make_async_remote_copy(
                src_ref=slot if src is None else src, dst_ref=slot,
                send_sem=send_sems.at[a, k], recv_sem=recv_sems.at[a, k], device_id=to, device_id_type=MESH)

        mine = [pltpu.make_async_copy(xs[a], outs[a].at[4 * x + 2 * y + c], local_sems.at[a]) for a in range(na)]
        for cp in mine:
            cp.start()
        first = []
        for a in range(na):
            first.append(copy(a, 0, me, sibling, src=xs[a]))
            first += [copy(a, 1 + j, me, (*chip, c), src=xs[a]) for j, chip in enumerate(chips)]
        for cp in first:
            cp.start()
        passed = []
        for j, chip in enumerate(chips):
            for a in range(na):
                copy(a, 1 + j, (*chip, c), me).wait_recv()
                cp = copy(a, 4 + j, (*chip, c), sibling)
                cp.start()
                passed.append(cp)
        for a in range(na):
            copy(a, 0, sibling, me).wait_recv()
            for j, chip in enumerate(chips):
                copy(a, 4 + j, (*chip, 1 - c), me).wait_recv()
        for cp in first + passed:
            cp.wait_send()
        for cp in mine:
            cp.wait()

    hbm = pl.BlockSpec(memory_space=pltpu.HBM)
    return pl.pallas_call(
        body, name=name, out_shape=[jax.ShapeDtypeStruct((N_DEV,) + s.shape, s.dtype) for s in shards],
        in_specs=[hbm] * na, out_specs=[hbm] * na,
        scratch_shapes=[pltpu.SemaphoreType.DMA((na, 7)), pltpu.SemaphoreType.DMA((na, 7)), pltpu.SemaphoreType.DMA((na,))],
    )(*shards)


def _peer_list():
    x, y, c = lax.axis_index("x"), lax.axis_index("y"), lax.axis_index("c")
    return 4 * x + 2 * y + c, [(x ^ (k >> 2), y ^ ((k >> 1) & 1), c ^ (k & 1)) for k in range(1, N_DEV)]


_HBM = pl.BlockSpec(memory_space=pltpu.HBM)
_SEM = pl.BlockSpec(memory_space=pltpu.SEMAPHORE)
_EFFECT = pltpu.SideEffectType.DATAFLOW_SIDE_EFFECTING


def _exchange_copies(xs, lands, send_sem, recv_sem, gather):
    me, peers = _peer_list()
    return [pltpu.make_async_remote_copy(
        src_ref=xs[a] if gather else xs[a].at[4 * px + 2 * py + pc], dst_ref=lands[a].at[me],
        send_sem=send_sem.at[7 * a + k], recv_sem=recv_sem.at[7 * a + k], device_id=(px, py, pc), device_id_type=MESH)
        for k, (px, py, pc) in enumerate(peers) for a in range(len(xs))]


def _exchange_start(name, blocks, gather=False):
    na = len(blocks)

    def body(*refs):
        for cp in _exchange_copies(refs[:na], refs[na:2 * na], refs[2 * na], refs[2 * na + 1], gather):
            cp.start()
        refs[-1][...] = jnp.zeros_like(refs[-1])

    lands = [jax.ShapeDtypeStruct((N_DEV,) + b.shape if gather else b.shape, b.dtype) for b in blocks]
    hbm = [pltpu.HBM(b.shape, b.dtype) for b in blocks] + [pltpu.HBM(b.shape, b.dtype) for b in lands]
    send_sem, recv_sem, *thru, token = pl.pallas_call(
        body, name=name,
        out_shape=(pltpu.SemaphoreType.DMA((7 * na,)), pltpu.SemaphoreType.DMA((7 * na,)), *hbm,
                   jax.ShapeDtypeStruct((8, LANES), f32)),
        in_specs=[_HBM] * (2 * na), out_specs=(_SEM, _SEM, *([_HBM] * (2 * na)), pl.BlockSpec(memory_space=pltpu.VMEM)),
        input_output_aliases={i: 2 + i for i in range(2 * na)},
        compiler_params=pltpu.CompilerParams(has_side_effects=_EFFECT),
    )(*[pltpu.with_memory_space_constraint(b, pltpu.HBM) for b in blocks],
      *[pltpu.with_memory_space_constraint(lax.empty(b.shape, b.dtype), pltpu.HBM) for b in lands])
    return send_sem, recv_sem, thru, token


def _exchange_wait(name, send_sem, recv_sem, thru, after, gather=False):
    na = len(thru) // 2

    def body(*refs):
        for cp in _exchange_copies(refs[:na], refs[na:2 * na], refs[2 * na], refs[2 * na + 1], gather):
            cp.wait_send()
            cp.wait_recv()

    return pl.pallas_call(
        body, name=name, out_shape=tuple(pltpu.HBM(t.shape, t.dtype) for t in thru),
        in_specs=[_HBM] * (2 * na) + [_SEM, _SEM, pl.BlockSpec(memory_space=pl.ANY)], out_specs=tuple([_HBM] * (2 * na)),
        input_output_aliases={i: i for i in range(2 * na)},
        compiler_params=pltpu.CompilerParams(has_side_effects=_EFFECT),
    )(*thru, send_sem, recv_sem, after)[na:]


def _layer_fwd(cfg, w, r, p):
    h1, proj = _k_in(cfg, r, w["norm1_g"], w["wcat_t"])
    qkv = _k_prep(cfg, proj, w["dn_conv_w"])
    o, *gdn_saved = _k_gdn_fwd(cfg, qkv, proj, w["alog_row"], w["dt_row"])
    hs = _k_lru_fwd(cfg, proj, w["lru_conv_w"], w["lru_conv_b"], w["wa"], w["wx"], w["lru_ba"], w["lru_bx"], w["lru_lambda"])
    mix, r_mid = _k_mix(cfg, o, proj, hs, r, w["dn_norm_g"], w["lru_norm_g"], w["w_out"])
    h2, gp, up = _k_ffn_a(cfg, r_mid, w["norm2_g"], w["ffn_wg_t"], w["ffn_wu_t"])
    act = _k_ffn_b(cfg, gp, up, w["ffn_conv_w"], w["ffn_conv_b"])
    r2 = _k_ffn_c(cfg, act, r_mid, w["ffn_wd"])
    pn, r3 = _k_ple(cfg, r2, p, w["ple_norm_g"], w["ple_wg"], w["ple_bg"], w["ple_wp_t"])
    saved = dict(r=r, h1=h1, proj=proj, qkv=qkv, o=o, gdn=gdn_saved, hs=hs, mix=mix, r_mid=r_mid, h2=h2, gp=gp, up=up,
                 r2=r2, pn=pn, p=p)
    return r3, saved


def _layer_bwd(cfg, w, sv, dr3):
    g = {}
    dt = min(512, cfg.D)
    dr2, dlog, dpp, g["ple_norm_g"], g["ple_bg"] = _b_ple(cfg, dr3, sv["r2"], sv["p"], w["ple_norm_g"], w["ple_wg"], w["ple_bg"], w["ple_wp_t"])
    g["ple_wg"] = _mm_tn_call(cfg, "d_ple_wg", sv["pn"], dlog, dt)
    g["ple_wp_t"] = _mm_tn_call(cfg, "d_ple_wp", dpp, sv["p"], cfg.PD)
    dgp, dup, g["ffn_wd"], g["ffn_conv_w"], g["ffn_conv_b"] = _b_ffn_bc(cfg, dr2, w["ffn_wd"], sv["gp"], sv["up"], w["ffn_conv_w"], w["ffn_conv_b"])
    dr_mid, g["norm2_g"] = _b_ffn_a(cfg, dgp, dup, dr2, sv["r_mid"], w["norm2_g"], w["ffn_wg_t"], w["ffn_wu_t"])
    g["ffn_wg_t"] = _mm_tn_call(cfg, "d_ffn_wg", dgp, sv["h2"], dt)
    g["ffn_wu_t"] = _mm_tn_call(cfg, "d_ffn_wu", dup, sv["h2"], dt)
    do, dz, dlg, dhs, g["dn_norm_g"], g["lru_norm_g"] = _b_mix(cfg, dr_mid, sv["o"], sv["proj"], sv["hs"], w["dn_norm_g"], w["lru_norm_g"], w["w_out"])
    g["w_out"] = _mm_tn_call(cfg, "d_w_out", sv["mix"], dr_mid, dt)
    dlx, g["lru_conv_w"], g["lru_conv_b"], g["wa"], g["wx"], g["lru_ba"], g["lru_bx"], g["lru_lambda"] = _b_lru(
        cfg, sv["proj"], dhs, w["lru_conv_w"], w["lru_conv_b"], w["wa"], w["wx"], w["lru_ba"], w["lru_bx"], w["lru_lambda"])
    dqkv, dba, g["alog_row"], g["dt_row"] = _b_gdn(cfg, sv["qkv"], sv["proj"], do, w["alog_row"], w["dt_row"], sv["gdn"])
    dqkv_pre, g["dn_conv_w"] = _b_prep(cfg, sv["proj"], dqkv, w["dn_conv_w"])
    dr, dproj, g["norm1_g"] = _b_in(cfg, dqkv_pre, dz, dlx, dlg, dba, dr_mid, sv["r"], w["norm1_g"], w["wcat_t"])
    g["wcat_t"] = _mm_tn_call(cfg, "d_w_in", dproj, sv["h1"], dt)
    return dr, g


BIG = ("w_in", "w_out", "ffn_wg", "ffn_wu", "ffn_wd", "ple_wg", "ple_wp")
BIG_T = {"w_in": True, "w_out": False, "ffn_wg": True, "ffn_wu": True, "ffn_wd": False, "ple_wg": False, "ple_wp": True}
BIG_OPERAND = {"w_in": "wcat_t", "w_out": "w_out", "ffn_wg": "ffn_wg_t", "ffn_wu": "ffn_wu_t", "ffn_wd": "ffn_wd",
               "ple_wg": "ple_wg", "ple_wp": "ple_wp_t"}
SMALL_SHARDED = ("dn_conv_w", "lru_conv_w", "lru_ba", "lru_bx", "lru_lambda", "ffn_conv_w")
SMALL_REPL = ("norm1_g", "dn_a_log", "dn_dt_bias", "dn_norm_g", "lru_conv_b", "lru_wa", "lru_wx", "lru_norm_g", "norm2_g",
              "ffn_conv_b", "ple_norm_g", "ple_bg", "final_g")
WEIGHTS = ("norm1_g", "w_in", "dn_conv_w", "dn_a_log", "dn_dt_bias", "dn_norm_g", "lru_conv_w", "lru_conv_b", "lru_wa",
           "lru_ba", "lru_wx", "lru_bx", "lru_lambda", "lru_norm_g", "w_out", "norm2_g", "ffn_wg", "ffn_wu", "ffn_conv_w",
           "ffn_conv_b", "ffn_wd", "ple_norm_g", "ple_wg", "ple_bg", "ple_wp", "final_g")


def _pad_rows(flat, cols, mult):
    n = flat.shape[0]
    rows = -(-n // cols)
    rows = -(-rows // mult) * mult
    return jnp.pad(flat, (0, rows * cols - n)).reshape(rows, cols)


def _pack(arrs, cols, mult, dtype):
    return _pad_rows(jnp.concatenate([a.reshape(-1).astype(dtype) for a in arrs]), cols, mult)


def _unpack(flat, shapes):
    out, off = [], 0
    for shp in shapes:
        n = math.prod(shp)
        piece = flat[off:off + n]
        if n < 4096:
            piece = lax.optimization_barrier(piece)
        out.append(piece.reshape(shp))
        off += n
    return out


def _unpack8(g8, shapes, axes):
    out, off = [], 0
    for shp, ax in zip(shapes, axes):
        n = math.prod(shp)
        a = g8[:, off:off + n].reshape((N_DEV,) + tuple(shp))
        a = jnp.moveaxis(a, 0, ax)
        out.append(a.reshape(shp[:ax] + (N_DEV * shp[ax],) + shp[ax + 1:]))
        off += n
    return out


def _wcat_t_from_w_in_t(cfg, wt):
    nba = 4 * cfg.H
    pad = jnp.zeros((LANES - nba, wt.shape[1]), wt.dtype)
    return jnp.concatenate([wt[:cfg.LXO], wt[cfg.LXO + nba:], wt[cfg.LXO:cfg.LXO + nba], pad], axis=0)


def _w_in_t_from_wcat_t(cfg, wc):
    nba = 4 * cfg.H
    return jnp.concatenate([wc[:cfg.LXO], wc[cfg.BAO:cfg.BAO + nba], wc[cfg.LXO:cfg.BAO]], axis=0)


def _gate_row(cfg, a):
    h2 = 2 * cfg.H
    return jnp.concatenate([jnp.zeros((1, h2), f32), a.reshape(1, h2), jnp.zeros((1, LANES - 2 * h2), f32)], axis=1)


def _blockdiag(cfg, w):
    w = w.reshape(2, cfg.NB // 2, 2, 64, 64)
    z = jnp.zeros_like(w[:, :, 0])
    top = jnp.concatenate([w[:, :, 0], z], axis=-1)
    bot = jnp.concatenate([z, w[:, :, 1]], axis=-1)
    return jnp.concatenate([top, bot], axis=-2).astype(bf16)


def _unblockdiag(cfg, g):
    a = g[:, :, :64, :64]
    b = g[:, :, 64:, 64:]
    return jnp.stack([a, b], axis=2).reshape(2, cfg.NB, 64, 64)


def _layer_operands(cfg, big, small, i):
    return dict(
        wcat_t=_wcat_t_from_w_in_t(cfg, big["w_in"]), w_out=big["w_out"], ffn_wg_t=big["ffn_wg"],
        ffn_wu_t=big["ffn_wu"], ffn_wd=big["ffn_wd"], ple_wg=big["ple_wg"], ple_wp_t=big["ple_wp"],
        norm1_g=small["norm1_g"][i][None], dn_conv_w=small["dn_conv_w"][i], alog_row=_gate_row(cfg, small["dn_a_log"][i]),
        dt_row=_gate_row(cfg, small["dn_dt_bias"][i]), dn_norm_g=small["dn_norm_g"][i][None],
        lru_conv_w=small["lru_conv_w"][i], lru_conv_b=small["lru_conv_b"][i][None],
        wa=_blockdiag(cfg, small["lru_wa"][i]), wx=_blockdiag(cfg, small["lru_wx"][i]),
        lru_ba=small["lru_ba"][i], lru_bx=small["lru_bx"][i], lru_lambda=small["lru_lambda"][i],
        lru_norm_g=small["lru_norm_g"][i][None], norm2_g=small["norm2_g"][i][None], ffn_conv_w=small["ffn_conv_w"][i],
        ffn_conv_b=small["ffn_conv_b"][i][None], ple_norm_g=small["ple_norm_g"][i][None], ple_bg=small["ple_bg"][i][None],
    )


def _small_grads_to_problem(cfg, g):
    h = cfg.H
    return dict(
        norm1_g=g["norm1_g"][0], dn_conv_w=g["dn_conv_w"],
        dn_a_log=g["alog_row"][0, 2 * h:4 * h].reshape(2, h), dn_dt_bias=g["dt_row"][0, 2 * h:4 * h].reshape(2, h),
        dn_norm_g=g["dn_norm_g"][0], lru_conv_w=g["lru_conv_w"], lru_conv_b=g["lru_conv_b"][0],
        lru_wa=_unblockdiag(cfg, g["wa"]), lru_wx=_unblockdiag(cfg, g["wx"]), lru_ba=g["lru_ba"], lru_bx=g["lru_bx"],
        lru_lambda=g["lru_lambda"], lru_norm_g=g["lru_norm_g"][0], norm2_g=g["norm2_g"][0], ffn_conv_w=g["ffn_conv_w"],
        ffn_conv_b=g["ffn_conv_b"][0], ple_norm_g=g["ple_norm_g"][0], ple_bg=g["ple_bg"][0],
    )


def _local_step(cfg, get_big, small, x, p, target, on_big_grads):
    r = x.reshape(cfg.T, cfg.D)
    ops, saved = [], []
    for i in range(cfg.L):
        big, token = get_big(i, r)
        w = _layer_operands(cfg, big, small, i)
        if token is not None:
            w["norm1_g"] = w["norm1_g"] + token[0, 0]
        r, sv = _layer_fwd(cfg, w, r, p[i].reshape(cfg.T, cfg.PD))
        ops.append(w)
        saved.append(sv)
    dr, loss, dgf = _k_loss(cfg, r, target.reshape(cfg.T, cfg.D), small["final_g"][None])
    gsmall = [None] * cfg.L
    for i in reversed(range(cfg.L)):
        dr, g = _layer_bwd(cfg, ops[i], saved[i], dr)
        token = on_big_grads(i, {n: (_w_in_t_from_wcat_t(cfg, g["wcat_t"]) if n == "w_in" else g[BIG_OPERAND[n]]) for n in BIG})
        if i > 0:
            ops[i - 1]["ple_bg"] = ops[i - 1]["ple_bg"] + token[0, 0]
        gsmall[i] = _small_grads_to_problem(cfg, g)
    gs = {k: jnp.stack([gl[k] for gl in gsmall]) for k in gsmall[0]}
    gs["final_g"] = dgf[0]
    return loss, dr, gs


def _row_tile(rows, limit=512):
    best = rows
    for t in range(8, min(rows, limit) + 1, 8):
        if rows % t == 0:
            best = t
    return best if best <= limit or rows <= limit else rows


def _adam_group(name, ws, gs, ms, vs, cols, tr):
    shapes = [w.shape for w in ws]
    pk = lambda arrs: _pack(arrs, cols, tr, f32)
    w2 = pk(ws)
    d, nm, nv = _adam_call(name, w2, pk(gs), pk(ms), pk(vs), min(tr, w2.shape[0]))
    return [_unpack(a.reshape(-1), shapes) for a in (d, nm, nv)]


def kernel(x, p, norm1_g, w_in, dn_conv_w, dn_a_log, dn_dt_bias, dn_norm_g, lru_conv_w, lru_conv_b, lru_wa, lru_ba, lru_wx, lru_bx, lru_lambda, lru_norm_g, w_out, norm2_g, ffn_wg, ffn_wu, ffn_conv_w, ffn_conv_b, ffn_wd, ple_norm_g, ple_wg, ple_bg, ple_wp, final_g, loss_target, m_norm1_g, m_w_in, m_dn_conv_w, m_dn_a_log, m_dn_dt_bias, m_dn_norm_g, m_lru_conv_w, m_lru_conv_b, m_lru_wa, m_lru_ba, m_lru_wx, m_lru_bx, m_lru_lambda, m_lru_norm_g, m_w_out, m_norm2_g, m_ffn_wg, m_ffn_wu, m_ffn_conv_w, m_ffn_conv_b, m_ffn_wd, m_ple_norm_g, m_ple_wg, m_ple_bg, m_ple_wp, m_final_g, v_norm1_g, v_w_in, v_dn_conv_w, v_dn_a_log, v_dn_dt_bias, v_dn_norm_g, v_lru_conv_w, v_lru_conv_b, v_lru_wa, v_lru_ba, v_lru_wx, v_lru_bx, v_lru_lambda, v_lru_norm_g, v_w_out, v_norm2_g, v_ffn_wg, v_ffn_wu, v_ffn_conv_w, v_ffn_conv_b, v_ffn_wd, v_ple_norm_g, v_ple_wg, v_ple_bg, v_ple_wp, v_final_g):
    cfg = CFG
    a = dict(locals())
    wl = {n: a[n] for n in WEIGHTS}
    ml = {n: a["m_" + n] for n in WEIGHTS}
    vl = {n: a["v_" + n] for n in WEIGHTS}
    me = 4 * lax.axis_index("x") + 2 * lax.axis_index("y") + lax.axis_index("c")
    nl = cfg.L

    blocks = [(jnp.swapaxes(wl[n], 1, 2) if BIG_T[n] else wl[n]).astype(bf16) for n in BIG]
    ss_shapes = [wl[n].shape for n in SMALL_SHARDED]
    *first, s8 = _all_gather("gather_weights_0", [blk[0] for blk in blocks] + [_pack([wl[n] for n in SMALL_SHARDED], LANES, 8, f32)])
    small = dict(zip(SMALL_SHARDED, _unpack8(s8.reshape(N_DEV, -1), ss_shapes, [2] * len(ss_shapes))))
    small.update({n: wl[n] for n in SMALL_REPL})
    gathers = {}

    def start_gather(i, behind):
        shards, _ = lax.optimization_barrier(([blk[i] for blk in blocks], behind))
        gathers[i] = _exchange_start("gather_start_%d" % i, shards, gather=True)
        return gathers[i][3]

    def get_big(i, r):
        if i == 0:
            full = first
        else:
            send_sem, recv_sem, thru, _ = gathers[i]
            lands = _exchange_wait("gather_wait_%d" % i, send_sem, recv_sem, thru, r, gather=True)
            full = [lax.dynamic_update_slice_in_dim(land, blk[i][None], me, 0) for land, blk in zip(lands, blocks)]
        token = start_gather(i + 1, full[0]) if i + 1 < nl else None
        return {n: f.reshape(N_DEV * f.shape[1], f.shape[2]) for n, f in zip(BIG, full)}, token

    pending = {}

    def on_big_grads(i, g):
        send = [g[n].reshape((N_DEV,) + blk.shape[1:]).astype(bf16) for n, blk in zip(BIG, blocks)]
        own = [lax.dynamic_index_in_dim(sd, me, 0, keepdims=True) for sd in send]
        send_sem, recv_sem, thru, token = _exchange_start("exchange_start_%d" % i, send)
        pending[i] = (send_sem, recv_sem, thru, own)
        return token

    loss_part, dr, gsmall = _local_step(cfg, get_big, small, x, p, loss_target, on_big_grads)
    grad_x = dr.reshape(x.shape)

    sums = {n: [] for n in BIG}
    for i in range(nl):
        send_sem, recv_sem, thru, own = pending[i]
        lands = _exchange_wait("exchange_wait_%d" % i, send_sem, recv_sem, thru, dr)
        for n, land, o in zip(BIG, lands, own):
            slots = lax.dynamic_update_slice_in_dim(land, o, me, 0)
            sums[n].append(_sum8_call("sum_%s_%d" % (n, i), slots, slots.shape[1]))
    gl = {}
    for n in BIG:
        s = jnp.stack(sums[n])
        gl[n] = jnp.swapaxes(s, 1, 2) if BIG_T[n] else s

    small_names = SMALL_REPL + SMALL_SHARDED
    small_shapes = [gsmall[n].shape for n in small_names]
    sv = _pack([gsmall[n] for n in small_names] + [loss_part[0, 0:1]], LANES, 512, f32)
    small_sum = _sum8_call("sum_small", _all_gather("gather_small_grads", [sv])[0], 512).reshape(-1)
    gl.update(zip(small_names, _unpack(small_sum, small_shapes)))
    loss = small_sum[sum(math.prod(s) for s in small_shapes)]
    for n in SMALL_SHARDED:
        shard = wl[n].shape[2]
        gl[n] = lax.dynamic_slice_in_dim(gl[n], me * shard, shard, axis=2)

    outs = {}
    for n in BIG:
        shp = wl[n].shape
        two = lambda t: t.reshape(-1, shp[-1])
        d, nm, nv = _adam_call("adam_" + n, two(wl[n]), two(gl[n]), two(ml[n]), two(vl[n]), _row_tile(math.prod(shp[:-1])))
        outs[n] = (d.reshape(shp), nm.reshape(shp), nv.reshape(shp))
    d, nm, nv = _adam_group("adam_small", [wl[n] for n in small_names], [gl[n] for n in small_names],
                            [ml[n] for n in small_names], [vl[n] for n in small_names], LANES, 64)
    for j, n in enumerate(small_names):
        outs[n] = (d[j], nm[j], nv[j])
    return (loss, grad_x, *[gl[n] for n in WEIGHTS], *[outs[n][0] for n in WEIGHTS], *[outs[n][1] for n in WEIGHTS],
            *[outs[n][2] for n in WEIGHTS])
```

```python
import functools
import math

import jax
import jax.numpy as jnp
from jax import lax
from jax.experimental import pallas as pl
from jax.experimental.pallas import tpu as pltpu

f32 = jnp.float32
bf16 = jnp.bfloat16
MESH = pl.DeviceIdType.MESH

N_DEV = 8
LANES = 128
EPS = 1e-6
LRU_C = 8.0
ADAM_LR, ADAM_B1, ADAM_B2, ADAM_EPS, ADAM_WD, ADAM_STEP = 0.001, 0.9, 0.999, 1e-08, 0.01, 10
VMEM_LIMIT = 56 * 1024 * 1024


class Cfg:
    def __init__(self, d_model=1024, bl=4, seq=2048, depth=4, heads=4, lru_width=512, d_ff=2816, ple=256,
                 tm=512, tm_ffn=256, ff_tile=256):
        self.D, self.BL, self.S, self.L, self.H = d_model, bl, seq, depth, heads
        self.DH = 128
        self.DN = heads * self.DH
        self.LW = lru_width
        self.NB = lru_width // 64
        self.FF, self.PD = d_ff, ple
        self.C = 64
        self.NC = seq // self.C
        self.T = bl * seq
        self.TM = min(tm, self.T)
        self.TMF = min(tm_ffn, self.T)
        self.FT = ff_tile
        self.ZO = 3 * self.DN
        self.LXO = 4 * self.DN
        self.LGO = self.LXO + self.LW
        self.BAO = self.LGO + self.LW
        self.PC = self.BAO + LANES
        self.IN_COLS = 4 * self.DN + 4 * heads + 2 * self.LW


CFG = Cfg()


def _mm(a, b):
    return jnp.dot(a.astype(bf16), b.astype(bf16), preferred_element_type=f32)


def _mm_nt(a, b):
    return lax.dot_general(a.astype(bf16), b.astype(bf16), (((1,), (1,)), ((), ())), preferred_element_type=f32)


def _mm_tn(a, b):
    return lax.dot_general(a.astype(bf16), b.astype(bf16), (((0,), (0,)), ((), ())), preferred_element_type=f32)


def _split2(a):
    hi = a.astype(bf16)
    return hi, (a - hi.astype(f32)).astype(bf16)


def _hmm3(a, b, dims=(((1,), (0,)), ((), ()))):
    ah, al = _split2(a)
    bh, bl = _split2(b)
    dot = functools.partial(lax.dot_general, dimension_numbers=dims, preferred_element_type=f32)
    return dot(ah, bh) + dot(ah, bl) + dot(al, bh)


def _hmm3_tn(a, b):
    return _hmm3(a, b, (((0,), (0,)), ((), ())))


def _cum_mm(mask, x, dims=(((1,), (0,)), ((), ()))):
    m = mask.astype(bf16)
    x1 = x.astype(bf16)
    r = x - x1.astype(f32)
    x2 = r.astype(bf16)
    x3 = (r - x2.astype(f32)).astype(bf16)
    dot = functools.partial(lax.dot_general, dimension_numbers=dims, preferred_element_type=f32)
    return dot(m, x1) + dot(m, x2) + dot(m, x3)


def _rms_fwd(x, g):
    inv = lax.rsqrt(jnp.mean(x * x, axis=-1, keepdims=True) + EPS)
    xh = x * inv
    return xh * g, xh, inv


def _rms_bwd(dy, xh, inv, g):
    dxh = dy * g
    dx = inv * (dxh - xh * jnp.mean(dxh * xh, axis=-1, keepdims=True))
    dg = jnp.sum(dy * xh, axis=0, keepdims=True)
    return dx, dg


def _sigmoid(x):
    return 1.0 / (1.0 + jnp.exp(-x))


def _softplus(x):
    return jnp.maximum(x, 0.0) + jnp.log(1.0 + jnp.exp(-jnp.abs(x)))


def _silu(x):
    s = _sigmoid(x)
    return x * s, s * (1.0 + x * (1.0 - s))


_GC = math.sqrt(2.0 / math.pi)


def _gelu(x):
    t = jnp.tanh(_GC * (x + 0.044715 * x * x * x))
    y = 0.5 * x * (1.0 + t)
    dy = 0.5 * (1.0 + t) + 0.5 * x * (1.0 - t * t) * _GC * (1.0 + 3.0 * 0.044715 * x * x)
    return y, dy


def _nexpm1(x):
    ser = -x * (1.0 + x * 0.5 * (1.0 + x * (1.0 / 3.0) * (1.0 + x * 0.25 * (1.0 + x * 0.2))))
    return jnp.where(x > -0.1, ser, 1.0 - jnp.exp(x))


def _shift(x, s, fill=0.0):
    if s == 0:
        return x
    n = x.shape[0]
    t = lax.broadcasted_iota(jnp.int32, x.shape, 0)
    r = pltpu.roll(x, (-s) % n, 0)
    return jnp.where((t + s >= 0) & (t + s < n), r, fill)


def _conv_fwd(x, w_ref, left):
    k = w_ref.shape[0]
    out = _shift(x, -left) * w_ref[0:1, :]
    for j in range(1, k):
        out = out + _shift(x, j - left) * w_ref[j:j + 1, :]
    return out


def _conv_bwd(dout, x, w_ref, left):
    k = w_ref.shape[0]
    dx = None
    dws = []
    for j in range(k):
        term = _shift(dout, -(j - left)) * w_ref[j:j + 1, :]
        dx = term if dx is None else dx + term
        dws.append(jnp.sum(dout * _shift(x, j - left), axis=0, keepdims=True))
    return dx, dws


def _scan_refs(scans, n):
    blk = 64
    nb = n // blk
    sub = lax.broadcasted_iota(jnp.int32, (blk, LANES), 0) & 7

    def local(a, b, rev):
        for d in (1, 2, 4):
            ok = (sub < 8 - d) if rev else (sub >= d)
            sh = (blk - d) if rev else d
            b = a * jnp.where(ok, pltpu.roll(b, sh, 0), 0.0) + b
            a = a * jnp.where(ok, pltpu.roll(a, sh, 0), 1.0)
        return a, b

    def body(i, carries):
        new = []
        for (a_ref, b_ref, h_ref, rev), carry in zip(scans, carries):
            j = (nb - 1 - i) if rev else i
            base = pl.multiple_of(j * blk, blk)
            a, b = local(a_ref[pl.ds(base, blk), :], b_ref[pl.ds(base, blk), :], rev)
            order = range(blk // 8 - 1, -1, -1) if rev else range(blk // 8)
            for v in order:
                h = b[8 * v:8 * v + 8, :] + a[8 * v:8 * v + 8, :] * carry
                h_ref[pl.ds(base + 8 * v, 8), :] = h
                carry = h[0:1, :] if rev else h[7:8, :]
            new.append(carry)
        return tuple(new)

    lax.fori_loop(0, nb, body, tuple(jnp.zeros((1, LANES), f32) for _ in scans))


def _tri_masks(c, rev):
    i = lax.broadcasted_iota(jnp.int32, (c, c), 0)
    j = lax.broadcasted_iota(jnp.int32, (c, c), 1)
    incl = (i <= j) if rev else (i >= j)
    strict = (i < j) if rev else (i > j)
    incl_t = (i >= j) if rev else (i <= j)
    return incl, strict, incl_t


def _map(f, *lists):
    return [f(*a) for a in zip(*lists)]


def _tri_inv(mats):
    c = mats[0].shape[0]
    i = lax.broadcasted_iota(jnp.int32, (c, c), 0)
    j = lax.broadcasted_iota(jnp.int32, (c, c), 1)
    eye = jnp.where(i == j, 1.0, 0.0)
    t = [eye - a for a in mats]
    pw = _map(_hmm3, mats, mats)
    for it in range(5):
        t = _map(lambda ti, ui: ti + ui, t, _map(_hmm3, t, pw))
        if it < 4:
            pw = _map(_hmm3, pw, pw)
    return t


def _gdn_decay(gs, revs):
    c = gs[0].shape[0]
    masks = [_tri_masks(c, r) for r in revs]
    gb = [jnp.broadcast_to(g, (c, c)) for g in gs]
    mcol = _map(lambda m, x: _cum_mm(m[0], x), masks, gb)
    mrow = _map(lambda m, x: jnp.sum(jnp.where(m[2], x, 0.0), axis=0, keepdims=True), masks, gb)
    dec = _map(lambda m, a, b: jnp.exp(jnp.where(m[0], a - b, -1e30)), masks, mcol, mrow)
    return [m[:, 0:1] for m in mcol], [jnp.sum(g, axis=0, keepdims=True) for g in gs], dec


def _gdn_prep(qs, k, v, g, beta, kk, qk, revs):
    c = k[0].shape[0]
    masks = [_tri_masks(c, r) for r in revs]
    gcum, glast, dec = _gdn_decay(g, revs)
    e = [jnp.exp(x) for x in gcum]
    tm = _tri_inv(_map(lambda m, b, x, d: jnp.where(m[1], b * x * d, 0.0), masks, beta, kk, dec))
    u = _map(lambda t, vi, b: _hmm3(t, vi * b), tm, v, beta)
    w = _map(lambda t, ki, b, ei: _hmm3(t, ki * b * ei), tm, k, beta, e)
    p = _map(lambda m, x, d: jnp.where(m[0], x * d, 0.0), masks, qk, dec)
    return dict(dec=dec, cd=[jnp.exp(x) for x in glast], tm=tm, u=u, w=w, p=p, qd=_map(lambda a, b: a * b, qs, e),
                kd=_map(lambda ki, gl, gc: ki * jnp.exp(gl - gc), k, glast, gcum))


def _lane_pick(x, lane):
    l = lax.broadcasted_iota(jnp.int32, x.shape, 1)
    return jnp.sum(jnp.where(l == lane, x, 0.0), axis=1, keepdims=True)


def _params(sem, vmem=VMEM_LIMIT):
    return pltpu.CompilerParams(dimension_semantics=sem, vmem_limit_bytes=vmem)


def _row_call(name, body, t, tm, row_ins, full_ins, row_outs, acc_outs):
    in_specs = [pl.BlockSpec((tm, w), functools.partial(lambda i, c: (i, c), c=c)) for (_, w, c) in row_ins]
    for a in full_ins:
        in_specs.append(pl.BlockSpec(a.shape, functools.partial(lambda i, n: (0,) * n, n=a.ndim)))
    out_specs = [pl.BlockSpec((tm, w), lambda i: (i, 0)) for (w, _) in row_outs]
    out_shape = [jax.ShapeDtypeStruct((t, w), dt) for (w, dt) in row_outs]
    for shp, dt in acc_outs:
        out_specs.append(pl.BlockSpec(shp, functools.partial(lambda i, n: (0,) * n, n=len(shp))))
        out_shape.append(jax.ShapeDtypeStruct(shp, dt))
    return pl.pallas_call(
        body, name=name, grid=(t // tm,), in_specs=in_specs, out_specs=out_specs, out_shape=out_shape,
        compiler_params=_params(("arbitrary",)),
    )(*[a for (a, _, _) in row_ins], *full_ins)


def _k_in(cfg, r, g1, wcat):
    def body(r_ref, g_ref, w_ref, h_ref, proj_ref):
        y, _, _ = _rms_fwd(r_ref[...], g_ref[...])
        hb = y.astype(bf16)
        h_ref[...] = hb
        proj_ref[...] = _mm_nt(hb, w_ref[...])

    return _row_call("in_proj", body, cfg.T, cfg.TM, [(r, cfg.D, 0)], [g1, wcat],
                     [(cfg.D, bf16), (cfg.PC, f32)], [])


def _k_prep(cfg, proj, conv_w):
    dn, s = cfg.DN, cfg.S

    def body(x_ref, w_ref, o_ref):
        sec = pl.program_id(1)
        c = _conv_fwd(x_ref[...], w_ref, 2)
        y, _ = _silu(c)
        for h in range(cfg.H):
            yh = y[:, h * 128:(h + 1) * 128]
            nh = yh * lax.rsqrt(jnp.sum(yh * yh, axis=1, keepdims=True) + EPS)
            o_ref[:, h * 128:(h + 1) * 128] = jnp.where(sec < 2, nh, yh)

    return pl.pallas_call(
        body, name="dn_prep", grid=(cfg.BL, 3),
        in_specs=[pl.BlockSpec((s, dn), lambda b, j: (b, j)), pl.BlockSpec((4, dn), lambda b, j: (0, j))],
        out_specs=pl.BlockSpec((s, dn), lambda b, j: (b, j)),
        out_shape=jax.ShapeDtypeStruct((cfg.T, 3 * dn), f32),
        compiler_params=_params(("arbitrary", "arbitrary")),
    )(proj, conv_w)


def _gate_cols(ba, alog_row, dt_row, lane_b, lane_a):
    beta = _sigmoid(_lane_pick(ba, lane_b))
    alpha = _lane_pick(ba, lane_a)
    aexp = jnp.exp(_lane_pick(alog_row, lane_a))
    dtb = _lane_pick(dt_row, lane_a)
    xa = alpha + dtb
    g = -aexp * _softplus(xa)
    return beta, g, aexp, xa


def _gdn_specs(cfg):
    s, c, nc = cfg.S, cfg.C, cfg.NC
    tok2 = pl.BlockSpec((2, s, 128), lambda b, h: (0, b, h))
    mat2 = pl.BlockSpec((2, 1, nc, c, c), lambda b, h: (0, b, h, 0, 0))
    cd2 = pl.BlockSpec((2, 1, nc, 8, LANES), lambda b, h: (0, b, h, 0, 0))
    shapes = dict(
        tok32=jax.ShapeDtypeStruct((2, cfg.T, cfg.DN), f32), tok16=jax.ShapeDtypeStruct((2, cfg.T, cfg.DN), bf16),
        mat32=jax.ShapeDtypeStruct((2, cfg.BL, cfg.H * nc, c, c), f32), mat16=jax.ShapeDtypeStruct((2, cfg.BL, cfg.H * nc, c, c), bf16),
        cd=jax.ShapeDtypeStruct((2, cfg.BL, cfg.H * nc, 8, LANES), f32))
    return tok2, mat2, cd2, shapes


def _k_gdn_fwd(cfg, qkv, proj, alog_row, dt_row):
    s, c, nc, hh = cfg.S, cfg.C, cfg.NC, cfg.H
    ba_blk = cfg.BAO // LANES
    per = 4 if nc % 4 == 0 else 1

    def body(q_ref, k_ref, v_ref, ba_ref, al_ref, dt_ref, o_ref, u_o, w_o, qd_o, kd_o, p_o, t_o, dec_o, cd_o):
        h = pl.program_id(1)

        def prep(i, carry):
            chains, qs, k, v, kk, qk, g, beta = [], [], [], [], [], [], [], []
            for j in range(per):
                n = i * per + j
                rows = pl.ds(pl.multiple_of(n * c, c), c)
                kj, vj = k_ref[rows, :], v_ref[rows, :]
                qj = q_ref[rows, :] * (128 ** -0.5)
                kkj, qkj = _mm_nt(kj, kj), _mm_nt(qj, kj)
                ba = ba_ref[rows, :]
                for d in range(2):
                    bd, gd, _, _ = _gate_cols(ba, al_ref[...], dt_ref[...], d * hh + h, 2 * hh + d * hh + h)
                    chains.append((d, n, rows))
                    for lst, val in ((qs, qj), (k, kj), (v, vj), (kk, kkj), (qk, qkj), (g, gd), (beta, bd)):
                        lst.append(val)
            z = _gdn_prep(qs, k, v, g, beta, kk, qk, [d == 1 for d, _, _ in chains])
            for x, (d, n, rows) in enumerate(chains):
                u_o[d, rows, :] = z["u"][x]
                w_o[d, rows, :] = z["w"][x].astype(bf16)
                qd_o[d, rows, :] = z["qd"][x].astype(bf16)
                kd_o[d, rows, :] = z["kd"][x].astype(bf16)
                p_o[d, 0, n] = z["p"][x].astype(bf16)
                t_o[d, 0, n] = z["tm"][x]
                dec_o[d, 0, n] = z["dec"][x]
                cd_o[d, 0, n] = jnp.broadcast_to(z["cd"][x], (8, LANES))
            return carry

        lax.fori_loop(0, nc // per, prep, 0)
        o_ref[...] = jnp.zeros_like(o_ref)

        def step(i, sts):
            ns = [i, nc - 1 - i]
            rows = [pl.ds(pl.multiple_of(n * c, c), c) for n in ns]
            vn = [u_o[d, rows[d], :] - _mm(w_o[d, rows[d], :], sts[d]) for d in range(2)]
            o = [_mm(qd_o[d, rows[d], :], sts[d]) + _mm(p_o[d, 0, ns[d]], vn[d]) for d in range(2)]
            new = [sts[d] * cd_o[d, 0, ns[d]][0:1, :] + _mm_tn(kd_o[d, rows[d], :], vn[d]) for d in range(2)]
            for d in range(2):
                o_ref[rows[d], :] = o_ref[rows[d], :] + o[d]
            return tuple(new)

        z0 = jnp.zeros((128, 128), f32)
        lax.fori_loop(0, nc, step, (z0, z0))

    blk = lambda off: pl.BlockSpec((s, 128), functools.partial(lambda b, h, off: (b, off + h), off=off))
    row = pl.BlockSpec((1, LANES), lambda b, h: (0, 0))
    tok2, mat2, cd2, shp = _gdn_specs(cfg)
    return pl.pallas_call(
        body, name="gdn_fwd", grid=(cfg.BL, hh),
        in_specs=[blk(0), blk(hh), blk(2 * hh), pl.BlockSpec((s, LANES), lambda b, h: (b, ba_blk)), row, row],
        out_specs=[pl.BlockSpec((s, 128), lambda b, h: (b, h)), tok2, tok2, tok2, tok2, mat2, mat2, mat2, cd2],
        out_shape=[jax.ShapeDtypeStruct((cfg.T, cfg.DN), f32), shp["tok32"], shp["tok16"], shp["tok16"], shp["tok16"],
                   shp["mat16"], shp["mat32"], shp["mat32"], shp["cd"]],
        compiler_params=_params(("arbitrary", "arbitrary")),
    )(qkv, qkv, qkv, proj, alog_row, dt_row)


def _lru_gates(xc, wa, wx, ba, bx, lam):
    ra = _sigmoid(_mm(xc, wa) + ba)
    ig = _sigmoid(_mm(xc, wx) + bx)
    sp = _softplus(-lam)
    la = -LRU_C * ra * sp
    a = jnp.exp(la)
    m = jnp.sqrt(_nexpm1(2.0 * la))
    return dict(ra=ra, ig=ig, sp=sp, a=a, m=m, gx=ig * xc)


def _lru_specs(cfg, outer_b):
    s = cfg.S
    lx_blk = cfg.LXO // LANES
    if outer_b:
        ix = lambda f: (lambda b, ct: f(b, ct))
    else:
        ix = lambda f: (lambda ct, b: f(b, ct))
    return dict(
        lx=pl.BlockSpec((s, LANES), ix(lambda b, ct: (b, lx_blk + ct))),
        tok=pl.BlockSpec((s, LANES), ix(lambda b, ct: (b, ct))),
        cw=pl.BlockSpec((4, LANES), ix(lambda b, ct: (0, ct))),
        row=pl.BlockSpec((1, LANES), ix(lambda b, ct: (0, ct))),
        w=pl.BlockSpec((2, 1, LANES, LANES), ix(lambda b, ct: (0, ct, 0, 0))),
        two=pl.BlockSpec((2, LANES), ix(lambda b, ct: (0, ct))),
    )


def _k_lru_fwd(cfg, proj, conv_w, conv_b, wa, wx, ba, bx, lam):
    def body(lx_ref, cw_ref, cb_ref, wa_ref, wx_ref, ba_ref, bx_ref, lam_ref, o_ref, a_s, b_s, h_s):
        xc = _conv_fwd(lx_ref[...], cw_ref, 2) + cb_ref[...]
        for d in range(2):
            z = _lru_gates(xc, wa_ref[d, 0], wx_ref[d, 0], ba_ref[d:d + 1, :], bx_ref[d:d + 1, :], lam_ref[d:d + 1, :])
            a_s[d] = z["a"]
            b_s[d] = z["m"] * z["gx"]
        _scan_refs([(a_s.at[d], b_s.at[d], h_s.at[d], d == 1) for d in range(2)], cfg.S)
        o_ref[...] = h_s[0] + h_s[1]

    sp = _lru_specs(cfg, True)
    return pl.pallas_call(
        body, name="lru_fwd", grid=(cfg.BL, cfg.LW // LANES),
        in_specs=[sp["lx"], sp["cw"], sp["row"], sp["w"], sp["w"], sp["two"], sp["two"], sp["two"]],
        out_specs=sp["tok"], out_shape=jax.ShapeDtypeStruct((cfg.T, cfg.LW), f32),
        scratch_shapes=[pltpu.VMEM((2, cfg.S, LANES), f32)] * 3,
        compiler_params=_params(("arbitrary", "arbitrary")),
    )(proj, conv_w, conv_b, wa, wx, ba, bx, lam)


def _mix_parts(cfg, o, z, lg, hs, dng, lrg):
    heads = []
    for h in range(cfg.H):
        sl = slice(h * 128, (h + 1) * 128)
        y, xh, inv = _rms_fwd(o[:, sl], dng)
        sz, dsz = _silu(z[:, sl])
        heads.append((y, xh, inv, sz, dsz))
    gl, dgl = _gelu(lg)
    y2, xh2, inv2 = _rms_fwd(gl * hs, lrg)
    return heads, (y2, xh2, inv2, gl, dgl)


def _k_mix(cfg, o, proj, hs, r, dng, lrg, wout):
    dn = cfg.DN

    def body(o_ref, z_ref, lg_ref, hs_ref, r_ref, dng_ref, lrg_ref, w_ref, mix_ref, out_ref):
        heads, lru = _mix_parts(cfg, o_ref[...], z_ref[...], lg_ref[...], hs_ref[...], dng_ref[...], lrg_ref[...])
        for h, (y, _, _, sz, _) in enumerate(heads):
            mix_ref[:, h * 128:(h + 1) * 128] = (y * sz).astype(bf16)
        mix_ref[:, dn:] = lru[0].astype(bf16)
        out_ref[...] = r_ref[...] + jnp.dot(mix_ref[...], w_ref[...], preferred_element_type=f32)

    return _row_call("mix_out", body, cfg.T, cfg.TM,
                     [(o, dn, 0), (proj, dn, cfg.ZO // dn), (proj, cfg.LW, cfg.LGO // cfg.LW), (hs, cfg.LW, 0), (r, cfg.D, 0)],
                     [dng, lrg, wout], [(cfg.D, bf16), (cfg.D, f32)], [])


def _k_ffn_a(cfg, r, g2, wg, wu):
    def body(r_ref, g_ref, wg_ref, wu_ref, h_ref, gp_ref, up_ref):
        y, _, _ = _rms_fwd(r_ref[...], g_ref[...])
        hb = y.astype(bf16)
        h_ref[...] = hb
        gp_ref[...] = _mm_nt(hb, wg_ref[...])
        up_ref[...] = _mm_nt(hb, wu_ref[...])

    return _row_call("ffn_in", body, cfg.T, cfg.TMF, [(r, cfg.D, 0)], [g2, wg, wu],
                     [(cfg.D, bf16), (cfg.FF, f32), (cfg.FF, f32)], [])


def _k_ffn_b(cfg, gp, up, conv_w, conv_b):
    s, ft = cfg.S, cfg.FT

    def body(gp_ref, up_ref, w_ref, b_ref, o_ref):
        gate = _conv_fwd(gp_ref[...], w_ref, 1) + b_ref[...]
        gl, _ = _gelu(gate)
        o_ref[...] = (gl * up_ref[...]).astype(bf16)

    tok = pl.BlockSpec((s, ft), lambda b, j: (b, j))
    return pl.pallas_call(
        body, name="ffn_act", grid=(cfg.BL, cfg.FF // ft),
        in_specs=[tok, tok, pl.BlockSpec((3, ft), lambda b, j: (0, j)), pl.BlockSpec((1, ft), lambda b, j: (0, j))],
        out_specs=tok, out_shape=jax.ShapeDtypeStruct((cfg.T, cfg.FF), bf16),
        compiler_params=_params(("arbitrary", "arbitrary")),
    )(gp, up, conv_w, conv_b)


def _k_ffn_c(cfg, act, r, wd):
    def body(a_ref, r_ref, w_ref, o_ref):
        o_ref[...] = r_ref[...] + jnp.dot(a_ref[...], w_ref[...], preferred_element_type=f32)

    return _row_call("ffn_out", body, cfg.T, cfg.TM, [(act, cfg.FF, 0), (r, cfg.D, 0)], [wd], [(cfg.D, f32)], [])[0]


def _k_ple(cfg, r, p, gp, wpg, bg, wpp):
    def body(r_ref, p_ref, g_ref, wg_ref, bg_ref, wp_ref, pn_ref, o_ref):
        x = r_ref[...]
        y, _, _ = _rms_fwd(x, g_ref[...])
        pn = y.astype(bf16)
        pn_ref[...] = pn
        pg = _sigmoid(jnp.dot(pn, wg_ref[...], preferred_element_type=f32) + bg_ref[...])
        o_ref[...] = x + pg * _mm_nt(p_ref[...], wp_ref[...])

    return _row_call("ple", body, cfg.T, cfg.TM, [(r, cfg.D, 0), (p, cfg.PD, 0)], [gp, wpg, bg, wpp],
                     [(cfg.D, bf16), (cfg.D, f32)], [])


def _k_loss(cfg, r, tgt, gf):
    d = cfg.D

    def body(r_ref, t_ref, g_ref, dr_ref, loss_ref, dg_ref):
        @pl.when(pl.program_id(0) == 0)
        def _():
            loss_ref[...] = jnp.zeros_like(loss_ref)
            dg_ref[...] = jnp.zeros_like(dg_ref)

        g = g_ref[...]
        y, xh, inv = _rms_fwd(r_ref[...], g)
        err = y - t_ref[...]
        loss_ref[...] = loss_ref[...] + (0.5 / d) * jnp.sum(err * err)
        dx, dg = _rms_bwd(err * (1.0 / d), xh, inv, g)
        dr_ref[...] = dx
        dg_ref[...] = dg_ref[...] + dg

    return _row_call("loss_head", body, cfg.T, cfg.TM, [(r, d, 0), (tgt, d, 0)], [gf], [(d, f32)],
                     [((1, LANES), f32), ((1, d), f32)])


def _zero_at_first(cond, *refs):
    @pl.when(cond)
    def _():
        for r in refs:
            r[...] = jnp.zeros_like(r)


def _b_ple(cfg, dr3, r2, p, gp, wpg, bg, wpp):
    d = cfg.D

    def body(dr_ref, r_ref, p_ref, g_ref, wg_ref, bg_ref, wp_ref, dr2_ref, dlog_ref, dpp_ref, dgp_ref, dbg_ref):
        _zero_at_first(pl.program_id(0) == 0, dgp_ref, dbg_ref)
        g = g_ref[...]
        dr = dr_ref[...]
        y, xh, inv = _rms_fwd(r_ref[...], g)
        pg = _sigmoid(jnp.dot(y.astype(bf16), wg_ref[...], preferred_element_type=f32) + bg_ref[...])
        pp = _mm_nt(p_ref[...], wp_ref[...])
        dpp_ref[...] = (dr * pg).astype(bf16)
        dlog = dr * pp * pg * (1.0 - pg)
        dlog_ref[...] = dlog.astype(bf16)
        dbg_ref[...] = dbg_ref[...] + jnp.sum(dlog, axis=0, keepdims=True)
        dx, dg = _rms_bwd(_mm_nt(dlog, wg_ref[...]), xh, inv, g)
        dgp_ref[...] = dgp_ref[...] + dg
        dr2_ref[...] = dr + dx

    return _row_call("ple_bwd", body, cfg.T, cfg.TM, [(dr3, d, 0), (r2, d, 0), (p, cfg.PD, 0)], [gp, wpg, bg, wpp],
                     [(d, f32), (d, bf16), (d, bf16)], [((1, d), f32), ((1, d), f32)])


def _b_ffn_bc(cfg, dr2, wd, gp, up, conv_w, conv_b):
    s, ft, d = cfg.S, cfg.FT, cfg.D

    def body(dr_ref, wd_ref, gp_ref, up_ref, w_ref, b_ref, dgp_ref, dup_ref, dwd_ref, dcw_ref, dcb_ref):
        _zero_at_first(pl.program_id(1) == 0, dwd_ref, dcw_ref, dcb_ref)
        drb = dr_ref[...].astype(bf16)
        dact = _mm_nt(drb, wd_ref[...])
        gpre = gp_ref[...]
        up = up_ref[...]
        gl, dgl = _gelu(_conv_fwd(gpre, w_ref, 1) + b_ref[...])
        dup_ref[...] = (dact * gl).astype(bf16)
        dgate = dact * up * dgl
        dcb_ref[...] = dcb_ref[...] + jnp.sum(dgate, axis=0, keepdims=True)
        dx, dws = _conv_bwd(dgate, gpre, w_ref, 1)
        for j, dw in enumerate(dws):
            dcw_ref[j:j + 1, :] = dcw_ref[j:j + 1, :] + dw
        dgp_ref[...] = dx.astype(bf16)
        dwd_ref[...] = dwd_ref[...] + _mm_tn((gl * up).astype(bf16), drb)

    tok = pl.BlockSpec((s, ft), lambda j, b: (b, j))
    return pl.pallas_call(
        body, name="ffn_act_bwd", grid=(cfg.FF // ft, cfg.BL),
        in_specs=[pl.BlockSpec((s, d), lambda j, b: (b, 0)), pl.BlockSpec((ft, d), lambda j, b: (j, 0)), tok, tok,
                  pl.BlockSpec((3, ft), lambda j, b: (0, j)), pl.BlockSpec((1, ft), lambda j, b: (0, j))],
        out_specs=[tok, tok, pl.BlockSpec((ft, d), lambda j, b: (j, 0)), pl.BlockSpec((3, ft), lambda j, b: (0, j)),
                   pl.BlockSpec((1, ft), lambda j, b: (0, j))],
        out_shape=[jax.ShapeDtypeStruct((cfg.T, cfg.FF), bf16), jax.ShapeDtypeStruct((cfg.T, cfg.FF), bf16),
                   jax.ShapeDtypeStruct((cfg.FF, d), f32), jax.ShapeDtypeStruct((3, cfg.FF), f32),
                   jax.ShapeDtypeStruct((1, cfg.FF), f32)],
        compiler_params=_params(("arbitrary", "arbitrary")),
    )(dr2, wd, gp, up, conv_w, conv_b)


def _b_ffn_a(cfg, dgp, dup, dr2, r_mid, g2, wg, wu):
    d = cfg.D

    def body(dgp_ref, dup_ref, dr_ref, r_ref, g_ref, wg_ref, wu_ref, o_ref, dg_ref):
        _zero_at_first(pl.program_id(0) == 0, dg_ref)
        g = g_ref[...]
        _, xh, inv = _rms_fwd(r_ref[...], g)
        dh = _mm(dgp_ref[...], wg_ref[...]) + _mm(dup_ref[...], wu_ref[...])
        dx, dg = _rms_bwd(dh, xh, inv, g)
        dg_ref[...] = dg_ref[...] + dg
        o_ref[...] = dr_ref[...] + dx

    return _row_call("ffn_in_bwd", body, cfg.T, cfg.TMF, [(dgp, cfg.FF, 0), (dup, cfg.FF, 0), (dr2, d, 0), (r_mid, d, 0)],
                     [g2, wg, wu], [(d, f32)], [((1, d), f32)])


def _mm_tn_call(cfg, name, x, dy, tn):
    t, k = x.shape
    n = dy.shape[1]
    tm = cfg.TM

    def body(x_ref, dy_ref, o_ref):
        _zero_at_first(pl.program_id(1) == 0, o_ref)
        o_ref[...] = o_ref[...] + _mm_tn(x_ref[...], dy_ref[...])

    return pl.pallas_call(
        body, name=name, grid=(n // tn, t // tm),
        in_specs=[pl.BlockSpec((tm, k), lambda j, i: (i, 0)), pl.BlockSpec((tm, tn), lambda j, i: (i, j))],
        out_specs=pl.BlockSpec((k, tn), lambda j, i: (0, j)),
        out_shape=jax.ShapeDtypeStruct((k, n), f32),
        compiler_params=_params(("arbitrary", "arbitrary")),
    )(x, dy)


def _b_mix(cfg, dr, o, proj, hs, dng, lrg, wout):
    dn, lw, d = cfg.DN, cfg.LW, cfg.D

    def body(dr_ref, o_ref, z_ref, lg_ref, hs_ref, dng_ref, lrg_ref, w_ref, do_ref, dz_ref, dlg_ref, dhs_ref, ddn_ref, dlr_ref):
        _zero_at_first(pl.program_id(0) == 0, ddn_ref, dlr_ref)
        dng, lrg = dng_ref[...], lrg_ref[...]
        hs = hs_ref[...]
        o, z = o_ref[...], z_ref[...]
        heads, lru = _mix_parts(cfg, o, z, lg_ref[...], hs, dng, lrg)
        drb = dr_ref[...].astype(bf16)
        dmix_dn = _mm_nt(drb, w_ref[0:dn, :])
        dmix_lr = _mm_nt(drb, w_ref[dn:, :])
        dgn = jnp.zeros_like(dng)
        for h, (y, xh, inv, sz, dsz) in enumerate(heads):
            sl = slice(h * 128, (h + 1) * 128)
            dm = dmix_dn[:, sl]
            dz_ref[:, sl] = dm * y * dsz
            dx, dg = _rms_bwd(dm * sz, xh, inv, dng)
            do_ref[:, sl] = dx
            dgn = dgn + dg
        ddn_ref[...] = ddn_ref[...] + dgn
        _, xh2, inv2, gl, dgl = lru
        dx2, dg2 = _rms_bwd(dmix_lr, xh2, inv2, lrg)
        dlr_ref[...] = dlr_ref[...] + dg2
        dlg_ref[...] = dx2 * hs * dgl
        dhs_ref[...] = dx2 * gl

    return _row_call("mix_bwd", body, cfg.T, cfg.TM,
                     [(dr, d, 0), (o, dn, 0), (proj, dn, cfg.ZO // dn), (proj, lw, cfg.LGO // lw), (hs, lw, 0)],
                     [dng, lrg, wout], [(dn, f32), (dn, f32), (lw, f32), (lw, f32)], [((1, 128), f32), ((1, lw), f32)])


def _b_lru(cfg, proj, dhs, conv_w, conv_b, wa, wx, ba, bx, lam):
    def body(lx_ref, dh_ref, cw_ref, cb_ref, wa_ref, wx_ref, ba_ref, bx_ref, lam_ref,
             dlx_ref, dcw_ref, dcb_ref, dwa_ref, dwx_ref, dba_ref, dbx_ref, dlam_ref, a_s, b_s, h_s, an_s, l_s):
        _zero_at_first(pl.program_id(1) == 0, dcw_ref, dcb_ref, dwa_ref, dwx_ref, dba_ref, dbx_ref, dlam_ref)
        lx = lx_ref[...]
        xc = _conv_fwd(lx, cw_ref, 2) + cb_ref[...]
        dxc = jnp.zeros_like(xc)
        gates = []
        for d in range(2):
            z = _lru_gates(xc, wa_ref[d, 0], wx_ref[d, 0], ba_ref[d:d + 1, :], bx_ref[d:d + 1, :], lam_ref[d:d + 1, :])
            a_s[d] = z["a"]
            b_s[d] = z["m"] * z["gx"]
            an_s[d] = _shift(z["a"], -1 if d == 1 else 1, 0.0)
            gates.append(z)
        _scan_refs([(a_s.at[d], b_s.at[d], h_s.at[d], d == 1) for d in range(2)]
                   + [(an_s.at[d], dh_ref, l_s.at[d], d == 0) for d in range(2)], cfg.S)
        for d in range(2):
            rev = d == 1
            lam = lam_ref[d:d + 1, :]
            z = gates[d]
            a, m, ra, ig, sp = z["a"], z["m"], z["ra"], z["ig"], z["sp"]
            lmb = l_s[d]
            h_prev = _shift(h_s[d], 1 if rev else -1, 0.0)
            da = lmb * h_prev
            dm = lmb * z["gx"]
            dgx = lmb * m
            dla = da * a - dm * (a * a) / jnp.maximum(m, 1e-30)
            dra = dla * (-LRU_C) * sp
            dsp = jnp.sum(dla * (-LRU_C) * ra, axis=0, keepdims=True)
            dlam_ref[d:d + 1, :] = dlam_ref[d:d + 1, :] - dsp * _sigmoid(-lam)
            dpa = dra * ra * (1.0 - ra)
            dpx = dgx * xc * ig * (1.0 - ig)
            dba_ref[d:d + 1, :] = dba_ref[d:d + 1, :] + jnp.sum(dpa, axis=0, keepdims=True)
            dbx_ref[d:d + 1, :] = dbx_ref[d:d + 1, :] + jnp.sum(dpx, axis=0, keepdims=True)
            dwa_ref[d, 0] = dwa_ref[d, 0] + _mm_tn(xc, dpa)
            dwx_ref[d, 0] = dwx_ref[d, 0] + _mm_tn(xc, dpx)
            dxc = dxc + dgx * ig + _mm_nt(dpa, wa_ref[d, 0]) + _mm_nt(dpx, wx_ref[d, 0])
        dcb_ref[...] = dcb_ref[...] + jnp.sum(dxc, axis=0, keepdims=True)
        dx, dws = _conv_bwd(dxc, lx, cw_ref, 2)
        for j, dw in enumerate(dws):
            dcw_ref[j:j + 1, :] = dcw_ref[j:j + 1, :] + dw
        dlx_ref[...] = dx

    sp = _lru_specs(cfg, False)
    nct = cfg.LW // LANES
    return pl.pallas_call(
        body, name="lru_bwd", grid=(nct, cfg.BL),
        in_specs=[sp["lx"], sp["tok"], sp["cw"], sp["row"], sp["w"], sp["w"], sp["two"], sp["two"], sp["two"]],
        out_specs=[sp["tok"], sp["cw"], sp["row"], sp["w"], sp["w"], sp["two"], sp["two"], sp["two"]],
        out_shape=[jax.ShapeDtypeStruct((cfg.T, cfg.LW), f32), jax.ShapeDtypeStruct((4, cfg.LW), f32),
                   jax.ShapeDtypeStruct((1, cfg.LW), f32), jax.ShapeDtypeStruct((2, nct, LANES, LANES), f32),
                   jax.ShapeDtypeStruct((2, nct, LANES, LANES), f32), jax.ShapeDtypeStruct((2, cfg.LW), f32),
                   jax.ShapeDtypeStruct((2, cfg.LW), f32), jax.ShapeDtypeStruct((2, cfg.LW), f32)],
        scratch_shapes=[pltpu.VMEM((2, cfg.S, LANES), f32)] * 5,
        compiler_params=_params(("arbitrary", "arbitrary")),
    )(proj, dhs, conv_w, conv_b, wa, wx, ba, bx, lam)


def _b_gdn(cfg, qkv, proj, do, alog_row, dt_row, saved):
    s, c, nc, hh = cfg.S, cfg.C, cfg.NC, cfg.H
    ba_blk = cfg.BAO // LANES
    scale = 128 ** -0.5
    per = 4 if nc % 4 == 0 else 1

    def body(q_ref, k_ref, v_ref, ba_ref, do_ref, al_ref, dt_ref, u_i, w_i, qd_i, kd_i, p_i, t_i, dec_i, cd_i,
             dqkv_ref, dba_ref, dal_ref, ddt_ref, vn_s, dvn_s, st_s, dst_s):
        h = pl.program_id(1)
        _zero_at_first((pl.program_id(0) == 0) & (h == 0), dal_ref, ddt_ref)
        _zero_at_first(h == 0, dba_ref)
        lane = lax.broadcasted_iota(jnp.int32, (c, LANES), 1)
        lane1 = lax.broadcasted_iota(jnp.int32, (1, LANES), 1)
        dirs = (0, 1)
        lane_b = [d * hh + h for d in dirs]
        lane_a = [2 * hh + d * hh + h for d in dirs]

        def fstep(i, sts):
            ns = [i, nc - 1 - i]
            rows = [pl.ds(pl.multiple_of(n * c, c), c) for n in ns]
            for d in dirs:
                st_s[d, ns[d]] = sts[d]
            vn = [u_i[d, rows[d], :] - _mm(w_i[d, rows[d], :], sts[d]) for d in dirs]
            for d in dirs:
                vn_s[d, rows[d], :] = vn[d]
            return tuple(sts[d] * cd_i[d, 0, ns[d]][0:1, :] + _mm_tn(kd_i[d, rows[d], :], vn[d]) for d in dirs)

        z0 = jnp.zeros((128, 128), f32)
        lax.fori_loop(0, nc, fstep, (z0, z0))

        def bseq(i, dst):
            ns = [nc - 1 - i, i]
            rows = [pl.ds(pl.multiple_of(n * c, c), c) for n in ns]
            for d in dirs:
                dst_s[d, ns[d]] = dst[d]
            dob = [do_ref[rows[d], :] for d in dirs]
            dvn = [_mm_tn(p_i[d, 0, ns[d]], dob[d]) + _mm(kd_i[d, rows[d], :], dst[d]) for d in dirs]
            for d in dirs:
                dvn_s[d, rows[d], :] = dvn[d]
            return tuple(_mm_tn(qd_i[d, rows[d], :], dob[d]) + cd_i[d, 0, ns[d]][0:1, 0:1] * dst[d]
                         - _mm_tn(w_i[d, rows[d], :], dvn[d]) for d in dirs)

        lax.fori_loop(0, nc, bseq, (z0, z0))

        def bpar(i, carry):
            ch = [(d, i * per + j) for j in range(per) for d in dirs]
            rows = [pl.ds(pl.multiple_of(n * c, c), c) for _, n in ch]
            masks = [_tri_masks(c, d == 1) for d, _ in ch]
            ld = lambda ref: [ref[d, r, :] for (d, _), r in zip(ch, rows)]
            ldm = lambda ref: [ref[d, 0, n] for d, n in ch]
            q, k, v, dob = ([ref[r, :] for r in rows] for ref in (q_ref, k_ref, v_ref, do_ref))
            gates = [_gate_cols(ba_ref[r, :], al_ref[...], dt_ref[...], lane_b[d], lane_a[d]) for (d, _), r in zip(ch, rows)]
            beta, g, aexp, xa = ([gt[x] for gt in gates] for x in range(4))
            st, dst = ([ref[d, n] for d, n in ch] for ref in (st_s, dst_s))
            vn, dvn, u, w = ld(vn_s), ld(dvn_s), ld(u_i), ld(w_i)
            p = [x.astype(f32) for x in ldm(p_i)]
            tm, dec = ldm(t_i), ldm(dec_i)
            cd = [cd_i[d, 0, n][0:1, 0:1] for d, n in ch]
            dp = _map(lambda m, a, b: jnp.where(m[0], _mm_nt(a, b), 0.0), masks, dob, vn)
            dqd = _map(_mm_nt, dob, st)
            dkd = _map(_mm_nt, vn, dst)
            dw = _map(lambda a, b: -_mm_nt(a, b), dvn, st)
            dcd = _map(lambda a, b: jnp.sum(jnp.sum(a * b, axis=1, keepdims=True), axis=0, keepdims=True), st, dst)
            gcum = _map(lambda m, x: _cum_mm(m[0], jnp.broadcast_to(x, (c, c)))[:, 0:1], masks, g)
            glast = [jnp.sum(x, axis=0, keepdims=True) for x in g]
            e = [jnp.exp(x) for x in gcum]
            el = _map(lambda a, b: jnp.exp(a - b), glast, gcum)
            qs = [x * scale for x in q]
            kb = _map(lambda a, b: a * b, k, beta)
            a = _map(lambda m, b, ki, dc: jnp.where(m[1], b * _mm_nt(ki, ki) * dc, 0.0), masks, beta, k, dec)
            dvb = _map(_hmm3_tn, tm, dvn)
            dkbe = _map(_hmm3_tn, tm, dw)
            da = _map(lambda m, x, ui, y, wi: -jnp.where(m[1], _mm_nt(x, ui) + _mm_nt(y, wi), 0.0), masks, dvb, u, dkbe, w)
            g1 = _map(lambda x, y: x * y, da, dec)
            g2 = _map(lambda x, y: x * y, dp, dec)
            dkb = _map(lambda x, ki, y, ei: _mm(x, ki) + y * ei, g1, k, dkbe, e)
            dk = _map(lambda x, kbi, y, qi, z, b, t, l: _mm_tn(x, kbi) + _mm_tn(y, qi) + z * b + t * l,
                      g1, kb, g2, qs, dkb, beta, dkd, el)
            dqs = _map(lambda y, ki, x, ei: _mm(y, ki) + x * ei, g2, k, dqd, e)
            ddd = _map(lambda x, ai, y, pi: x * ai + y * pi, da, a, dp, p)
            ones = jnp.ones((c, LANES), f32)
            dgcum = _map(lambda x: jnp.sum(x, axis=1, keepdims=True) - _hmm3_tn(x, ones)[:, 0:1], ddd)
            for x, (d, n) in enumerate(ch):
                dbeta = jnp.sum(dvb[x] * v[x], axis=1, keepdims=True) + jnp.sum(dkb[x] * k[x], axis=1, keepdims=True)
                de = jnp.sum(dkbe[x] * kb[x], axis=1, keepdims=True) + jnp.sum(dqd[x] * qs[x], axis=1, keepdims=True)
                del_ = jnp.sum(dkd[x] * k[x], axis=1, keepdims=True)
                dgc = dgcum[x] + de * e[x] - del_ * el[x]
                dglast = jnp.sum(del_ * el[x], axis=0, keepdims=True) + dcd[x] * cd[x]
                dg = _cum_mm(masks[x][2], jnp.broadcast_to(dgc, (c, LANES)))[:, 0:1] + dglast
                r = rows[x]
                if d == 0:
                    dqkv_ref[0, r, :] = dqs[x] * scale
                    dqkv_ref[1, r, :] = dk[x]
                    dqkv_ref[2, r, :] = dvb[x] * beta[x]
                else:
                    dqkv_ref[0, r, :] = dqkv_ref[0, r, :] + dqs[x] * scale
                    dqkv_ref[1, r, :] = dqkv_ref[1, r, :] + dk[x]
                    dqkv_ref[2, r, :] = dqkv_ref[2, r, :] + dvb[x] * beta[x]
                dlb = dbeta * beta[x] * (1.0 - beta[x])
                dalpha = -dg * aexp[x] * _sigmoid(xa[x])
                dba_ref[r, :] = dba_ref[r, :] + jnp.where(lane == lane_b[d], dlb, 0.0) + jnp.where(lane == lane_a[d], dalpha, 0.0)
                dal_ref[...] = dal_ref[...] + jnp.where(lane1 == lane_a[d], jnp.sum(dg * g[x], axis=0, keepdims=True), 0.0)
                ddt_ref[...] = ddt_ref[...] + jnp.where(lane1 == lane_a[d], jnp.sum(dalpha, axis=0, keepdims=True), 0.0)
            return carry

        lax.fori_loop(0, nc // per, bpar, 0)

    blk = lambda off: pl.BlockSpec((s, 128), functools.partial(lambda b, h, off: (b, off + h), off=off))
    row = pl.BlockSpec((1, LANES), lambda b, h: (0, 0))
    tok2, mat2, cd2, _ = _gdn_specs(cfg)
    return pl.pallas_call(
        body, name="gdn_bwd", grid=(cfg.BL, hh),
        in_specs=[blk(0), blk(hh), blk(2 * hh), pl.BlockSpec((s, LANES), lambda b, h: (b, ba_blk)),
                  pl.BlockSpec((s, 128), lambda b, h: (b, h)), row, row, tok2, tok2, tok2, tok2, mat2, mat2, mat2, cd2],
        out_specs=[pl.BlockSpec((3, s, 128), lambda b, h: (0, b, h)), pl.BlockSpec((s, LANES), lambda b, h: (b, 0)), row, row],
        out_shape=[jax.ShapeDtypeStruct((3, cfg.T, cfg.DN), f32), jax.ShapeDtypeStruct((cfg.T, LANES), f32),
                   jax.ShapeDtypeStruct((1, LANES), f32), jax.ShapeDtypeStruct((1, LANES), f32)],
        scratch_shapes=[pltpu.VMEM((2, s, 128), f32)] * 2 + [pltpu.VMEM((2, nc, 128, 128), f32)] * 2,
        compiler_params=_params(("arbitrary", "arbitrary")),
    )(qkv, qkv, qkv, proj, do, alog_row, dt_row, *saved)


def _b_prep(cfg, proj, dqkv, conv_w):
    dn, s = cfg.DN, cfg.S

    def body(x_ref, dy_ref, w_ref, dx_ref, dw_ref):
        _zero_at_first(pl.program_id(1) == 0, dw_ref)
        sec = pl.program_id(0)
        x = x_ref[...]
        c = _conv_fwd(x, w_ref, 2)
        y, dsilu = _silu(c)
        dy = dy_ref[0]
        parts = []
        for h in range(cfg.H):
            sl = slice(h * 128, (h + 1) * 128)
            yh, dyh = y[:, sl], dy[:, sl]
            inv = lax.rsqrt(jnp.sum(yh * yh, axis=1, keepdims=True) + EPS)
            dn_h = inv * dyh - yh * (inv * inv * inv) * jnp.sum(dyh * yh, axis=1, keepdims=True)
            parts.append(jnp.where(sec < 2, dn_h, dyh))
        ds = jnp.concatenate(parts, axis=1) if len(parts) > 1 else parts[0]
        dx, dws = _conv_bwd(ds * dsilu, x, w_ref, 2)
        for j, dw in enumerate(dws):
            dw_ref[j:j + 1, :] = dw_ref[j:j + 1, :] + dw
        dx_ref[...] = dx

    return pl.pallas_call(
        body, name="dn_prep_bwd", grid=(3, cfg.BL),
        in_specs=[pl.BlockSpec((s, dn), lambda j, b: (b, j)), pl.BlockSpec((1, s, dn), lambda j, b: (j, b, 0)),
                  pl.BlockSpec((4, dn), lambda j, b: (0, j))],
        out_specs=[pl.BlockSpec((s, dn), lambda j, b: (b, j)), pl.BlockSpec((4, dn), lambda j, b: (0, j))],
        out_shape=[jax.ShapeDtypeStruct((cfg.T, 3 * dn), f32), jax.ShapeDtypeStruct((4, 3 * dn), f32)],
        compiler_params=_params(("arbitrary", "arbitrary")),
    )(proj, dqkv, conv_w)


def _b_in(cfg, dqkv_pre, dz, dlx, dlg, dba, dr_mid, r_in, g1, wcat):
    d, dn, lw = cfg.D, cfg.DN, cfg.LW

    def body(dq_ref, dz_ref, dlx_ref, dlg_ref, dba_ref, dr_ref, r_ref, g_ref, w_ref, o_ref, dp_ref, dg_ref):
        _zero_at_first(pl.program_id(0) == 0, dg_ref)
        dp_ref[:, 0:cfg.ZO] = dq_ref[...].astype(bf16)
        dp_ref[:, cfg.ZO:cfg.LXO] = dz_ref[...].astype(bf16)
        dp_ref[:, cfg.LXO:cfg.LGO] = dlx_ref[...].astype(bf16)
        dp_ref[:, cfg.LGO:cfg.BAO] = dlg_ref[...].astype(bf16)
        dp_ref[:, cfg.BAO:] = dba_ref[...].astype(bf16)
        g = g_ref[...]
        _, xh, inv = _rms_fwd(r_ref[...], g)
        dx, dg = _rms_bwd(_mm(dp_ref[...], w_ref[...]), xh, inv, g)
        dg_ref[...] = dg_ref[...] + dg
        o_ref[...] = dr_ref[...] + dx

    return _row_call("in_proj_bwd", body, cfg.T, cfg.TM,
                     [(dqkv_pre, 3 * dn, 0), (dz, dn, 0), (dlx, lw, 0), (dlg, lw, 0), (dba, LANES, 0), (dr_mid, d, 0), (r_in, d, 0)],
                     [g1, wcat], [(d, f32), (cfg.PC, bf16)], [((1, d), f32)])


def _adam_call(name, w, g, m, v, tr):
    rows, cols = w.shape
    bc1 = 1.0 - ADAM_B1 ** ADAM_STEP
    bc2 = 1.0 - ADAM_B2 ** ADAM_STEP

    def body(w_ref, g_ref, m_ref, v_ref, d_ref, nm_ref, nv_ref):
        g = g_ref[...]
        m = ADAM_B1 * m_ref[...] + (1.0 - ADAM_B1) * g
        v = ADAM_B2 * v_ref[...] + (1.0 - ADAM_B2) * (g * g)
        nm_ref[...] = m
        nv_ref[...] = v
        d_ref[...] = -ADAM_LR * ((m / bc1) / (jnp.sqrt(v / bc2) + ADAM_EPS) + ADAM_WD * w_ref[...])

    spec = pl.BlockSpec((tr, cols), lambda i: (i, 0))
    return pl.pallas_call(
        body, name=name, grid=(rows // tr,), in_specs=[spec] * 4, out_specs=[spec] * 3,
        out_shape=[jax.ShapeDtypeStruct((rows, cols), f32)] * 3, compiler_params=_params(("arbitrary",)),
    )(w, g, m, v)


def _sum8_call(name, x, tr):
    _, rows, cols = x.shape

    def body(x_ref, o_ref):
        acc = x_ref[0].astype(f32)
        for j in range(1, N_DEV):
            acc = acc + x_ref[j].astype(f32)
        o_ref[...] = acc

    return pl.pallas_call(
        body, name=name, grid=(rows // tr,), in_specs=[pl.BlockSpec((N_DEV, tr, cols), lambda i: (0, i, 0))],
        out_specs=pl.BlockSpec((tr, cols), lambda i: (i, 0)), out_shape=jax.ShapeDtypeStruct((rows, cols), f32),
        compiler_params=_params(("arbitrary",)),
    )(x)


def _all_gather(name, shards):
    na = len(shards)

    def body(*refs):
        xs, outs = refs[:na], refs[na:2 * na]
        send_sems, recv_sems, local_sems = refs[2 * na:]
        x, y, c = lax.axis_index("x"), lax.axis_index("y"), lax.axis_index("c")
        me, sibling = (x, y, c), (x, y, 1 - c)
        chips = [(1 - x, y), (x, 1 - y), (1 - x, 1 - y)]

        def copy(a, k, block, to, src=None):
            px, py, pc = block
            slot = outs[a].at[4 * px + 2 * py + pc]
            return pltpu.make_async_remote_copy(
                src_ref=slot if src is None else src, dst_ref=slot,
                send_sem=send_sems.at[a, k], recv_sem=recv_sems.at[a, k], device_id=to, device_id_type=MESH)

        mine = [pltpu.make_async_copy(xs[a], outs[a].at[4 * x + 2 * y + c], local_sems.at[a]) for a in range(na)]
        for cp in mine:
            cp.start()
        first = []
        for a in range(na):
            first.append(copy(a, 0, me, sibling, src=xs[a]))
            first += [copy(a, 1 + j, me, (*chip, c), src=xs[a]) for j, chip in enumerate(chips)]
        for cp in first:
            cp.start()
        passed = []
        for j, chip in enumerate(chips):
            for a in range(na):
                copy(a, 1 + j, (*chip, c), me).wait_recv()
                cp = copy(a, 4 + j, (*chip, c), sibling)
                cp.start()
                passed.append(cp)
        for a in range(na):
            copy(a, 0, sibling, me).wait_recv()
            for j, chip in enumerate(chips):
                copy(a, 4 + j, (*chip, 1 - c), me).wait_recv()
        for cp in first + passed:
            cp.wait_send()
        for cp in mine:
            cp.wait()

    hbm = pl.BlockSpec(memory_space=pltpu.HBM)
    return pl.pallas_call(
        body, name=name, out_shape=[jax.ShapeDtypeStruct((N_DEV,) + s.shape, s.dtype) for s in shards],
        in_specs=[hbm] * na, out_specs=[hbm] * na,
        scratch_shapes=[pltpu.SemaphoreType.DMA((na, 7)), pltpu.SemaphoreType.DMA((na, 7)), pltpu.SemaphoreType.DMA((na,))],
    )(*shards)


def _peer_list():
    x, y, c = lax.axis_index("x"), lax.axis_index("y"), lax.axis_index("c")
    return 4 * x + 2 * y + c, [(x ^ (k >> 2), y ^ ((k >> 1) & 1), c ^ (k & 1)) for k in range(1, N_DEV)]


_HBM = pl.BlockSpec(memory_space=pltpu.HBM)
_SEM = pl.BlockSpec(memory_space=pltpu.SEMAPHORE)
_EFFECT = pltpu.SideEffectType.DATAFLOW_SIDE_EFFECTING


def _exchange_copies(xs, lands, send_sem, recv_sem, gather):
    me, peers = _peer_list()
    return [pltpu.make_async_remote_copy(
        src_ref=xs[a] if gather else xs[a].at[4 * px + 2 * py + pc], dst_ref=lands[a].at[me],
        send_sem=send_sem.at[7 * a + k], recv_sem=recv_sem.at[7 * a + k], device_id=(px, py, pc), device_id_type=MESH)
        for k, (px, py, pc) in enumerate(peers) for a in range(len(xs))]


def _exchange_start(name, blocks, gather=False):
    na = len(blocks)

    def body(*refs):
        for cp in _exchange_copies(refs[:na], refs[na:2 * na], refs[2 * na], refs[2 * na + 1], gather):
            cp.start()
        refs[-1][...] = jnp.zeros_like(refs[-1])

    lands = [jax.ShapeDtypeStruct((N_DEV,) + b.shape if gather else b.shape, b.dtype) for b in blocks]
    hbm = [pltpu.HBM(b.shape, b.dtype) for b in blocks] + [pltpu.HBM(b.shape, b.dtype) for b in lands]
    send_sem, recv_sem, *thru, token = pl.pallas_call(
        body, name=name,
        out_shape=(pltpu.SemaphoreType.DMA((7 * na,)), pltpu.SemaphoreType.DMA((7 * na,)), *hbm,
                   jax.ShapeDtypeStruct((8, LANES), f32)),
        in_specs=[_HBM] * (2 * na), out_specs=(_SEM, _SEM, *([_HBM] * (2 * na)), pl.BlockSpec(memory_space=pltpu.VMEM)),
        input_output_aliases={i: 2 + i for i in range(2 * na)},
        compiler_params=pltpu.CompilerParams(has_side_effects=_EFFECT),
    )(*[pltpu.with_memory_space_constraint(b, pltpu.HBM) for b in blocks],
      *[pltpu.with_memory_space_constraint(lax.empty(b.shape, b.dtype), pltpu.HBM) for b in lands])
    return send_sem, recv_sem, thru, token


def _exchange_wait(name, send_sem, recv_sem, thru, after, gather=False):
    na = len(thru) // 2

    def body(*refs):
        for cp in _exchange_copies(refs[:na], refs[na:2 * na], refs[2 * na], refs[2 * na + 1], gather):
            cp.wait_send()
            cp.wait_recv()

    return pl.pallas_call(
        body, name=name, out_shape=tuple(pltpu.HBM(t.shape, t.dtype) for t in thru),
        in_specs=[_HBM] * (2 * na) + [_SEM, _SEM, pl.BlockSpec(memory_space=pl.ANY)], out_specs=tuple([_HBM] * (2 * na)),
        input_output_aliases={i: i for i in range(2 * na)},
        compiler_params=pltpu.CompilerParams(has_side_effects=_EFFECT),
    )(*thru, send_sem, recv_sem, after)[na:]


def _layer_fwd(cfg, w, r, p):
    h1, proj = _k_in(cfg, r, w["norm1_g"], w["wcat_t"])
    qkv = _k_prep(cfg, proj, w["dn_conv_w"])
    o, *gdn_saved = _k_gdn_fwd(cfg, qkv, proj, w["alog_row"], w["dt_row"])
    hs = _k_lru_fwd(cfg, proj, w["lru_conv_w"], w["lru_conv_b"], w["wa"], w["wx"], w["lru_ba"], w["lru_bx"], w["lru_lambda"])
    mix, r_mid = _k_mix(cfg, o, proj, hs, r, w["dn_norm_g"], w["lru_norm_g"], w["w_out"])
    h2, gp, up = _k_ffn_a(cfg, r_mid, w["norm2_g"], w["ffn_wg_t"], w["ffn_wu_t"])
    act = _k_ffn_b(cfg, gp, up, w["ffn_conv_w"], w["ffn_conv_b"])
    r2 = _k_ffn_c(cfg, act, r_mid, w["ffn_wd"])
    pn, r3 = _k_ple(cfg, r2, p, w["ple_norm_g"], w["ple_wg"], w["ple_bg"], w["ple_wp_t"])
    saved = dict(r=r, h1=h1, proj=proj, qkv=qkv, o=o, gdn=gdn_saved, hs=hs, mix=mix, r_mid=r_mid, h2=h2, gp=gp, up=up,
                 r2=r2, pn=pn, p=p)
    return r3, saved


def _layer_bwd(cfg, w, sv, dr3, early=None):
    g = {}
    dt = min(512, cfg.D)
    dr2, dlog, dpp, g["ple_norm_g"], g["ple_bg"] = _b_ple(cfg, dr3, sv["r2"], sv["p"], w["ple_norm_g"], w["ple_wg"], w["ple_bg"], w["ple_wp_t"])
    g["ple_wg"] = _mm_tn_call(cfg, "d_ple_wg", sv["pn"], dlog, dt)
    g["ple_wp_t"] = _mm_tn_call(cfg, "d_ple_wp", dpp, sv["p"], cfg.PD)
    dgp, dup, g["ffn_wd"], g["ffn_conv_w"], g["ffn_conv_b"] = _b_ffn_bc(cfg, dr2, w["ffn_wd"], sv["gp"], sv["up"], w["ffn_conv_w"], w["ffn_conv_b"])
    dr_mid, g["norm2_g"] = _b_ffn_a(cfg, dgp, dup, dr2, sv["r_mid"], w["norm2_g"], w["ffn_wg_t"], w["ffn_wu_t"])
    g["ffn_wg_t"] = _mm_tn_call(cfg, "d_ffn_wg", dgp, sv["h2"], dt)
    g["ffn_wu_t"] = _mm_tn_call(cfg, "d_ffn_wu", dup, sv["h2"], dt)
    if early is not None:
        w = dict(w, dn_norm_g=w["dn_norm_g"] + early(g)[0, 0])
    do, dz, dlg, dhs, g["dn_norm_g"], g["lru_norm_g"] = _b_mix(cfg, dr_mid, sv["o"], sv["proj"], sv["hs"], w["dn_norm_g"], w["lru_norm_g"], w["w_out"])
    g["w_out"] = _mm_tn_call(cfg, "d_w_out", sv["mix"], dr_mid, dt)
    dlx, g["lru_conv_w"], g["lru_conv_b"], g["wa"], g["wx"], g["lru_ba"], g["lru_bx"], g["lru_lambda"] = _b_lru(
        cfg, sv["proj"], dhs, w["lru_conv_w"], w["lru_conv_b"], w["wa"], w["wx"], w["lru_ba"], w["lru_bx"], w["lru_lambda"])
    dqkv, dba, g["alog_row"], g["dt_row"] = _b_gdn(cfg, sv["qkv"], sv["proj"], do, w["alog_row"], w["dt_row"], sv["gdn"])
    dqkv_pre, g["dn_conv_w"] = _b_prep(cfg, sv["proj"], dqkv, w["dn_conv_w"])
    dr, dproj, g["norm1_g"] = _b_in(cfg, dqkv_pre, dz, dlx, dlg, dba, dr_mid, sv["r"], w["norm1_g"], w["wcat_t"])
    g["wcat_t"] = _mm_tn_call(cfg, "d_w_in", dproj, sv["h1"], dt)
    return dr, g


BIG = ("w_in", "w_out", "ffn_wg", "ffn_wu", "ffn_wd", "ple_wg", "ple_wp")
BIG_T = {"w_in": True, "w_out": False, "ffn_wg": True, "ffn_wu": True, "ffn_wd": False, "ple_wg": False, "ple_wp": True}
BIG_OPERAND = {"w_in": "wcat_t", "w_out": "w_out", "ffn_wg": "ffn_wg_t", "ffn_wu": "ffn_wu_t", "ffn_wd": "ffn_wd",
               "ple_wg": "ple_wg", "ple_wp": "ple_wp_t"}
EARLY = ("ffn_wg", "ffn_wu", "ffn_wd", "ple_wg", "ple_wp")
SMALL_SHARDED = ("dn_conv_w", "lru_conv_w", "lru_ba", "lru_bx", "lru_lambda", "ffn_conv_w")
SMALL_REPL = ("norm1_g", "dn_a_log", "dn_dt_bias", "dn_norm_g", "lru_conv_b", "lru_wa", "lru_wx", "lru_norm_g", "norm2_g",
              "ffn_conv_b", "ple_norm_g", "ple_bg", "final_g")
WEIGHTS = ("norm1_g", "w_in", "dn_conv_w", "dn_a_log", "dn_dt_bias", "dn_norm_g", "lru_conv_w", "lru_conv_b", "lru_wa",
           "lru_ba", "lru_wx", "lru_bx", "lru_lambda", "lru_norm_g", "w_out", "norm2_g", "ffn_wg", "ffn_wu", "ffn_conv_w",
           "ffn_conv_b", "ffn_wd", "ple_norm_g", "ple_wg", "ple_bg", "ple_wp", "final_g")


def _pad_rows(flat, cols, mult):
    n = flat.shape[0]
    rows = -(-n // cols)
    rows = -(-rows // mult) * mult
    return jnp.pad(flat, (0, rows * cols - n)).reshape(rows, cols)


def _pack(arrs, cols, mult, dtype):
    return _pad_rows(jnp.concatenate([a.reshape(-1).astype(dtype) for a in arrs]), cols, mult)


def _unpack(flat, shapes):
    out, off = [], 0
    for shp in shapes:
        n = math.prod(shp)
        piece = flat[off:off + n]
        if n < 4096:
            piece = lax.optimization_barrier(piece)
        out.append(piece.reshape(shp))
        off += n
    return out


def _unpack8(g8, shapes, axes):
    out, off = [], 0
    for shp, ax in zip(shapes, axes):
        n = math.prod(shp)
        a = g8[:, off:off + n].reshape((N_DEV,) + tuple(shp))
        a = jnp.moveaxis(a, 0, ax)
        out.append(a.reshape(shp[:ax] + (N_DEV * shp[ax],) + shp[ax + 1:]))
        off += n
    return out


def _wcat_t_from_w_in_t(cfg, wt):
    nba = 4 * cfg.H
    pad = jnp.zeros((LANES - nba, wt.shape[1]), wt.dtype)
    return jnp.concatenate([wt[:cfg.LXO], wt[cfg.LXO + nba:], wt[cfg.LXO:cfg.LXO + nba], pad], axis=0)


def _w_in_t_from_wcat_t(cfg, wc):
    nba = 4 * cfg.H
    return jnp.concatenate([wc[:cfg.LXO], wc[cfg.BAO:cfg.BAO + nba], wc[cfg.LXO:cfg.BAO]], axis=0)


def _gate_row(cfg, a):
    h2 = 2 * cfg.H
    return jnp.concatenate([jnp.zeros((1, h2), f32), a.reshape(1, h2), jnp.zeros((1, LANES - 2 * h2), f32)], axis=1)


def _blockdiag(cfg, w):
    w = w.reshape(2, cfg.NB // 2, 2, 64, 64)
    z = jnp.zeros_like(w[:, :, 0])
    top = jnp.concatenate([w[:, :, 0], z], axis=-1)
    bot = jnp.concatenate([z, w[:, :, 1]], axis=-1)
    return jnp.concatenate([top, bot], axis=-2).astype(bf16)


def _unblockdiag(cfg, g):
    a = g[:, :, :64, :64]
    b = g[:, :, 64:, 64:]
    return jnp.stack([a, b], axis=2).reshape(2, cfg.NB, 64, 64)


def _layer_operands(cfg, big, small, i):
    return dict(
        wcat_t=_wcat_t_from_w_in_t(cfg, big["w_in"]), w_out=big["w_out"], ffn_wg_t=big["ffn_wg"],
        ffn_wu_t=big["ffn_wu"], ffn_wd=big["ffn_wd"], ple_wg=big["ple_wg"], ple_wp_t=big["ple_wp"],
        norm1_g=small["norm1_g"][i][None], dn_conv_w=small["dn_conv_w"][i], alog_row=_gate_row(cfg, small["dn_a_log"][i]),
        dt_row=_gate_row(cfg, small["dn_dt_bias"][i]), dn_norm_g=small["dn_norm_g"][i][None],
        lru_conv_w=small["lru_conv_w"][i], lru_conv_b=small["lru_conv_b"][i][None],
        wa=_blockdiag(cfg, small["lru_wa"][i]), wx=_blockdiag(cfg, small["lru_wx"][i]),
        lru_ba=small["lru_ba"][i], lru_bx=small["lru_bx"][i], lru_lambda=small["lru_lambda"][i],
        lru_norm_g=small["lru_norm_g"][i][None], norm2_g=small["norm2_g"][i][None], ffn_conv_w=small["ffn_conv_w"][i],
        ffn_conv_b=small["ffn_conv_b"][i][None], ple_norm_g=small["ple_norm_g"][i][None], ple_bg=small["ple_bg"][i][None],
    )


def _small_grads_to_problem(cfg, g):
    h = cfg.H
    return dict(
        norm1_g=g["norm1_g"][0], dn_conv_w=g["dn_conv_w"],
        dn_a_log=g["alog_row"][0, 2 * h:4 * h].reshape(2, h), dn_dt_bias=g["dt_row"][0, 2 * h:4 * h].reshape(2, h),
        dn_norm_g=g["dn_norm_g"][0], lru_conv_w=g["lru_conv_w"], lru_conv_b=g["lru_conv_b"][0],
        lru_wa=_unblockdiag(cfg, g["wa"]), lru_wx=_unblockdiag(cfg, g["wx"]), lru_ba=g["lru_ba"], lru_bx=g["lru_bx"],
        lru_lambda=g["lru_lambda"], lru_norm_g=g["lru_norm_g"][0], norm2_g=g["norm2_g"][0], ffn_conv_w=g["ffn_conv_w"],
        ffn_conv_b=g["ffn_conv_b"][0], ple_norm_g=g["ple_norm_g"][0], ple_bg=g["ple_bg"][0],
    )


def _local_step(cfg, get_big, small, x, p, target, on_big_grads):
    r = x.reshape(cfg.T, cfg.D)
    ops, saved = [], []
    for i in range(cfg.L):
        big, token = get_big(i, r)
        w = _layer_operands(cfg, big, small, i)
        if token is not None:
            w["norm1_g"] = w["norm1_g"] + token[0, 0]
        r, sv = _layer_fwd(cfg, w, r, p[i].reshape(cfg.T, cfg.PD))
        ops.append(w)
        saved.append(sv)
    dr, loss, dgf = _k_loss(cfg, r, target.reshape(cfg.T, cfg.D), small["final_g"][None])
    gsmall = [None] * cfg.L
    for i in reversed(range(cfg.L)):
        first = EARLY if i == 0 else ()
        early = (lambda g: on_big_grads("%da" % i, {n: g[BIG_OPERAND[n]] for n in first})) if first else None
        dr, g = _layer_bwd(cfg, ops[i], saved[i], dr, early)
        token = on_big_grads("%d" % i, {n: (_w_in_t_from_wcat_t(cfg, g["wcat_t"]) if n == "w_in" else g[BIG_OPERAND[n]])
                                        for n in BIG if n not in first})
        if i > 0:
            ops[i - 1]["ple_bg"] = ops[i - 1]["ple_bg"] + token[0, 0]
        gsmall[i] = _small_grads_to_problem(cfg, g)
    gs = {k: jnp.stack([gl[k] for gl in gsmall]) for k in gsmall[0]}
    gs["final_g"] = dgf[0]
    return loss, dr, gs


def _row_tile(rows, limit=512):
    best = rows
    for t in range(8, min(rows, limit) + 1, 8):
        if rows % t == 0:
            best = t
    return best if best <= limit or rows <= limit else rows


def _adam_group(name, ws, gs, ms, vs, cols, tr):
    shapes = [w.shape for w in ws]
    pk = lambda arrs: _pack(arrs, cols, tr, f32)
    w2 = pk(ws)
    d, nm, nv = _adam_call(name, w2, pk(gs), pk(ms), pk(vs), min(tr, w2.shape[0]))
    return [_unpack(a.reshape(-1), shapes) for a in (d, nm, nv)]


def kernel(x, p, norm1_g, w_in, dn_conv_w, dn_a_log, dn_dt_bias, dn_norm_g, lru_conv_w, lru_conv_b, lru_wa, lru_ba, lru_wx, lru_bx, lru_lambda, lru_norm_g, w_out, norm2_g, ffn_wg, ffn_wu, ffn_conv_w, ffn_conv_b, ffn_wd, ple_norm_g, ple_wg, ple_bg, ple_wp, final_g, loss_target, m_norm1_g, m_w_in, m_dn_conv_w, m_dn_a_log, m_dn_dt_bias, m_dn_norm_g, m_lru_conv_w, m_lru_conv_b, m_lru_wa, m_lru_ba, m_lru_wx, m_lru_bx, m_lru_lambda, m_lru_norm_g, m_w_out, m_norm2_g, m_ffn_wg, m_ffn_wu, m_ffn_conv_w, m_ffn_conv_b, m_ffn_wd, m_ple_norm_g, m_ple_wg, m_ple_bg, m_ple_wp, m_final_g, v_norm1_g, v_w_in, v_dn_conv_w, v_dn_a_log, v_dn_dt_bias, v_dn_norm_g, v_lru_conv_w, v_lru_conv_b, v_lru_wa, v_lru_ba, v_lru_wx, v_lru_bx, v_lru_lambda, v_lru_norm_g, v_w_out, v_norm2_g, v_ffn_wg, v_ffn_wu, v_ffn_conv_w, v_ffn_conv_b, v_ffn_wd, v_ple_norm_g, v_ple_wg, v_ple_bg, v_ple_wp, v_final_g):
    cfg = CFG
    a = dict(locals())
    wl = {n: a[n] for n in WEIGHTS}
    ml = {n: a["m_" + n] for n in WEIGHTS}
    vl = {n: a["v_" + n] for n in WEIGHTS}
    me = 4 * lax.axis_index("x") + 2 * lax.axis_index("y") + lax.axis_index("c")
    nl = cfg.L

    blocks = [(jnp.swapaxes(wl[n], 1, 2) if BIG_T[n] else wl[n]).astype(bf16) for n in BIG]
    ss_shapes = [wl[n].shape for n in SMALL_SHARDED]
    *first, s8 = _all_gather("gather_weights_0", [blk[0] for blk in blocks] + [_pack([wl[n] for n in SMALL_SHARDED], LANES, 8, f32)])
    small = dict(zip(SMALL_SHARDED, _unpack8(s8.reshape(N_DEV, -1), ss_shapes, [2] * len(ss_shapes))))
    small.update({n: wl[n] for n in SMALL_REPL})
    gathers = {}

    def start_gather(i, behind):
        shards, _ = lax.optimization_barrier(([blk[i] for blk in blocks], behind))
        gathers[i] = _exchange_start("gather_start_%d" % i, shards, gather=True)
        return gathers[i][3]

    def get_big(i, r):
        if i == 0:
            full = first
        else:
            send_sem, recv_sem, thru, _ = gathers[i]
            lands = _exchange_wait("gather_wait_%d" % i, send_sem, recv_sem, thru, r, gather=True)
            full = [lax.dynamic_update_slice_in_dim(land, blk[i][None], me, 0) for land, blk in zip(lands, blocks)]
        token = start_gather(i + 1, full[0]) if i + 1 < nl else None
        return {n: f.reshape(N_DEV * f.shape[1], f.shape[2]) for n, f in zip(BIG, full)}, token

    pending = {}

    def on_big_grads(key, g):
        send = [g[n].reshape((N_DEV,) + blk.shape[1:]).astype(bf16) for n, blk in zip(BIG, blocks) if n in g]
        own = [lax.dynamic_index_in_dim(sd, me, 0, keepdims=True) for sd in send]
        send_sem, recv_sem, thru, token = _exchange_start("exchange_start_" + key, send)
        pending[key] = (send_sem, recv_sem, thru, own, [n for n in BIG if n in g])
        return token

    loss_part, dr, gsmall = _local_step(cfg, get_big, small, x, p, loss_target, on_big_grads)
    grad_x = dr.reshape(x.shape)

    sums = {n: [None] * nl for n in BIG}
    for key in sorted(pending):
        send_sem, recv_sem, thru, own, names = pending[key]
        lands = _exchange_wait("exchange_wait_" + key, send_sem, recv_sem, thru, dr)
        for n, land, o in zip(names, lands, own):
            slots = lax.dynamic_update_slice_in_dim(land, o, me, 0)
            sums[n][int(key[0])] = _sum8_call("sum_%s_%s" % (n, key), slots, slots.shape[1])
    gl = {}
    for n in BIG:
        s = jnp.stack(sums[n])
        gl[n] = jnp.swapaxes(s, 1, 2) if BIG_T[n] else s

    small_names = SMALL_REPL + SMALL_SHARDED
    small_shapes = [gsmall[n].shape for n in small_names]
    sv = _pack([gsmall[n] for n in small_names] + [loss_part[0, 0:1]], LANES, 512, f32)
    small_sum = _sum8_call("sum_small", _all_gather("gather_small_grads", [sv])[0], 512).reshape(-1)
    gl.update(zip(small_names, _unpack(small_sum, small_shapes)))
    loss = small_sum[sum(math.prod(s) for s in small_shapes)]
    for n in SMALL_SHARDED:
        shard = wl[n].shape[2]
        gl[n] = lax.dynamic_slice_in_dim(gl[n], me * shard, shard, axis=2)

    outs = {}
    for n in BIG:
        shp = wl[n].shape
        two = lambda t: t.reshape(-1, shp[-1])
        d, nm, nv = _adam_call("adam_" + n, two(wl[n]), two(gl[n]), two(ml[n]), two(vl[n]), _row_tile(math.prod(shp[:-1])))
        outs[n] = (d.reshape(shp), nm.reshape(shp), nv.reshape(shp))
    d, nm, nv = _adam_group("adam_small", [wl[n] for n in small_names], [gl[n] for n in small_names],
                            [ml[n] for n in small_names], [vl[n] for n in small_names], LANES, 64)
    for j, n in enumerate(small_names):
        outs[n] = (d[j], nm[j], nv[j])
    return (loss, grad_x, *[gl[n] for n in WEIGHTS], *[outs[n][0] for n in WEIGHTS], *[outs[n][1] for n in WEIGHTS],
            *[outs[n][2] for n in WEIGHTS])
```

```python
import functools
import math

import jax
import jax.numpy as jnp
from jax import lax
from jax.experimental import pallas as pl
from jax.experimental.pallas import tpu as pltpu

f32 = jnp.float32
bf16 = jnp.bfloat16
MESH = pl.DeviceIdType.MESH

N_DEV = 8
LANES = 128
EPS = 1e-6
LRU_C = 8.0
ADAM_LR, ADAM_B1, ADAM_B2, ADAM_EPS, ADAM_WD, ADAM_STEP = 0.001, 0.9, 0.999, 1e-08, 0.01, 10
VMEM_LIMIT = 56 * 1024 * 1024


class Cfg:
    def __init__(self, d_model=1024, bl=4, seq=2048, depth=4, heads=4, lru_width=512, d_ff=2816, ple=256,
                 tm=512, tm_ffn=256, ff_tile=256):
        self.D, self.BL, self.S, self.L, self.H = d_model, bl, seq, depth, heads
        self.DH = 128
        self.DN = heads * self.DH
        self.LW = lru_width
        self.NB = lru_width // 64
        self.FF, self.PD = d_ff, ple
        self.C = 64
        self.NC = seq // self.C
        self.T = bl * seq
        self.TM = min(tm, self.T)
        self.TMF = min(tm_ffn, self.T)
        self.FT = ff_tile
        self.ZO = 3 * self.DN
        self.LXO = 4 * self.DN
        self.LGO = self.LXO + self.LW
        self.BAO = self.LGO + self.LW
        self.PC = self.BAO + LANES
        self.IN_COLS = 4 * self.DN + 4 * heads + 2 * self.LW


CFG = Cfg()


def _mm(a, b):
    return jnp.dot(a.astype(bf16), b.astype(bf16), preferred_element_type=f32)


def _mm_nt(a, b):
    return lax.dot_general(a.astype(bf16), b.astype(bf16), (((1,), (1,)), ((), ())), preferred_element_type=f32)


def _mm_tn(a, b):
    return lax.dot_general(a.astype(bf16), b.astype(bf16), (((0,), (0,)), ((), ())), preferred_element_type=f32)


def _split2(a):
    hi = a.astype(bf16)
    return hi, (a - hi.astype(f32)).astype(bf16)


def _hmm3(a, b, dims=(((1,), (0,)), ((), ()))):
    ah, al = _split2(a)
    bh, bl = _split2(b)
    dot = functools.partial(lax.dot_general, dimension_numbers=dims, preferred_element_type=f32)
    return dot(ah, bh) + dot(ah, bl) + dot(al, bh)


def _hmm3_tn(a, b):
    return _hmm3(a, b, (((0,), (0,)), ((), ())))


def _cum_mm(mask, x, dims=(((1,), (0,)), ((), ()))):
    m = mask.astype(bf16)
    x1 = x.astype(bf16)
    r = x - x1.astype(f32)
    x2 = r.astype(bf16)
    x3 = (r - x2.astype(f32)).astype(bf16)
    dot = functools.partial(lax.dot_general, dimension_numbers=dims, preferred_element_type=f32)
    return dot(m, x1) + dot(m, x2) + dot(m, x3)


def _rms_fwd(x, g):
    inv = lax.rsqrt(jnp.mean(x * x, axis=-1, keepdims=True) + EPS)
    xh = x * inv
    return xh * g, xh, inv


def _rms_bwd(dy, xh, inv, g):
    dxh = dy * g
    dx = inv * (dxh - xh * jnp.mean(dxh * xh, axis=-1, keepdims=True))
    dg = jnp.sum(dy * xh, axis=0, keepdims=True)
    return dx, dg


def _sigmoid(x):
    return 1.0 / (1.0 + jnp.exp(-x))


def _softplus(x):
    return jnp.maximum(x, 0.0) + jnp.log(1.0 + jnp.exp(-jnp.abs(x)))


def _silu(x):
    s = _sigmoid(x)
    return x * s, s * (1.0 + x * (1.0 - s))


_GC = math.sqrt(2.0 / math.pi)


def _gelu(x):
    t = jnp.tanh(_GC * (x + 0.044715 * x * x * x))
    y = 0.5 * x * (1.0 + t)
    dy = 0.5 * (1.0 + t) + 0.5 * x * (1.0 - t * t) * _GC * (1.0 + 3.0 * 0.044715 * x * x)
    return y, dy


def _nexpm1(x):
    ser = -x * (1.0 + x * 0.5 * (1.0 + x * (1.0 / 3.0) * (1.0 + x * 0.25 * (1.0 + x * 0.2))))
    return jnp.where(x > -0.1, ser, 1.0 - jnp.exp(x))


def _shift(x, s, fill=0.0):
    if s == 0:
        return x
    n = x.shape[0]
    t = lax.broadcasted_iota(jnp.int32, x.shape, 0)
    r = pltpu.roll(x, (-s) % n, 0)
    return jnp.where((t + s >= 0) & (t + s < n), r, fill)


def _conv_fwd(x, w_ref, left):
    k = w_ref.shape[0]
    out = _shift(x, -left) * w_ref[0:1, :]
    for j in range(1, k):
        out = out + _shift(x, j - left) * w_ref[j:j + 1, :]
    return out


def _conv_bwd(dout, x, w_ref, left):
    k = w_ref.shape[0]
    dx = None
    dws = []
    for j in range(k):
        term = _shift(dout, -(j - left)) * w_ref[j:j + 1, :]
        dx = term if dx is None else dx + term
        dws.append(jnp.sum(dout * _shift(x, j - left), axis=0, keepdims=True))
    return dx, dws


def _scan_refs(scans, n):
    blk = 64
    nb = n // blk
    sub = lax.broadcasted_iota(jnp.int32, (blk, LANES), 0) & 7

    def local(a, b, rev):
        for d in (1, 2, 4):
            ok = (sub < 8 - d) if rev else (sub >= d)
            sh = (blk - d) if rev else d
            b = a * jnp.where(ok, pltpu.roll(b, sh, 0), 0.0) + b
            a = a * jnp.where(ok, pltpu.roll(a, sh, 0), 1.0)
        return a, b

    def body(i, carries):
        new = []
        for (a_ref, b_ref, h_ref, rev), carry in zip(scans, carries):
            j = (nb - 1 - i) if rev else i
            base = pl.multiple_of(j * blk, blk)
            a, b = local(a_ref[pl.ds(base, blk), :], b_ref[pl.ds(base, blk), :], rev)
            order = range(blk // 8 - 1, -1, -1) if rev else range(blk // 8)
            for v in order:
                h = b[8 * v:8 * v + 8, :] + a[8 * v:8 * v + 8, :] * carry
                h_ref[pl.ds(base + 8 * v, 8), :] = h
                carry = h[0:1, :] if rev else h[7:8, :]
            new.append(carry)
        return tuple(new)

    lax.fori_loop(0, nb, body, tuple(jnp.zeros((1, LANES), f32) for _ in scans))


def _tri_masks(c, rev):
    i = lax.broadcasted_iota(jnp.int32, (c, c), 0)
    j = lax.broadcasted_iota(jnp.int32, (c, c), 1)
    incl = (i <= j) if rev else (i >= j)
    strict = (i < j) if rev else (i > j)
    incl_t = (i >= j) if rev else (i <= j)
    return incl, strict, incl_t


def _map(f, *lists):
    return [f(*a) for a in zip(*lists)]


def _tri_inv(mats, tick=lambda: None):
    c = mats[0].shape[0]
    i = lax.broadcasted_iota(jnp.int32, (c, c), 0)
    j = lax.broadcasted_iota(jnp.int32, (c, c), 1)
    eye = jnp.where(i == j, 1.0, 0.0)
    t = [eye - a for a in mats]
    pw = _map(_hmm3, mats, mats)
    for it in range(5):
        tick()
        t = _map(lambda ti, ui: ti + ui, t, _map(_hmm3, t, pw))
        if it < 4:
            pw = _map(_hmm3, pw, pw)
    return t


def _gdn_decay(gs, revs):
    c = gs[0].shape[0]
    masks = [_tri_masks(c, r) for r in revs]
    gb = [jnp.broadcast_to(g, (c, c)) for g in gs]
    mcol = _map(lambda m, x: _cum_mm(m[0], x), masks, gb)
    mrow = _map(lambda m, x: jnp.sum(jnp.where(m[2], x, 0.0), axis=0, keepdims=True), masks, gb)
    dec = _map(lambda m, a, b: jnp.exp(jnp.where(m[0], a - b, -1e30)), masks, mcol, mrow)
    return [m[:, 0:1] for m in mcol], [jnp.sum(g, axis=0, keepdims=True) for g in gs], dec


def _gdn_prep(qs, k, v, g, beta, kk, qk, revs, tick=lambda: None):
    c = k[0].shape[0]
    masks = [_tri_masks(c, r) for r in revs]
    gcum, glast, dec = _gdn_decay(g, revs)
    e = [jnp.exp(x) for x in gcum]
    tick()
    tm = _tri_inv(_map(lambda m, b, x, d: jnp.where(m[1], b * x * d, 0.0), masks, beta, kk, dec), tick)
    tick()
    u = _map(lambda t, vi, b: _hmm3(t, vi * b), tm, v, beta)
    tick()
    w = _map(lambda t, ki, b, ei: _hmm3(t, ki * b * ei), tm, k, beta, e)
    p = _map(lambda m, x, d: jnp.where(m[0], x * d, 0.0), masks, qk, dec)
    return dict(dec=dec, cd=[jnp.exp(x) for x in glast], tm=tm, u=u, w=w, p=p, qd=_map(lambda a, b: a * b, qs, e),
                kd=_map(lambda ki, gl, gc: ki * jnp.exp(gl - gc), k, glast, gcum))


def _lane_pick(x, lane):
    l = lax.broadcasted_iota(jnp.int32, x.shape, 1)
    return jnp.sum(jnp.where(l == lane, x, 0.0), axis=1, keepdims=True)


def _params(sem, vmem=VMEM_LIMIT):
    return pltpu.CompilerParams(dimension_semantics=sem, vmem_limit_bytes=vmem)


def _row_call(name, body, t, tm, row_ins, full_ins, row_outs, acc_outs):
    in_specs = [pl.BlockSpec((tm, w), functools.partial(lambda i, c: (i, c), c=c)) for (_, w, c) in row_ins]
    for a in full_ins:
        in_specs.append(pl.BlockSpec(a.shape, functools.partial(lambda i, n: (0,) * n, n=a.ndim)))
    out_specs = [pl.BlockSpec((tm, w), lambda i: (i, 0)) for (w, _) in row_outs]
    out_shape = [jax.ShapeDtypeStruct((t, w), dt) for (w, dt) in row_outs]
    for shp, dt in acc_outs:
        out_specs.append(pl.BlockSpec(shp, functools.partial(lambda i, n: (0,) * n, n=len(shp))))
        out_shape.append(jax.ShapeDtypeStruct(shp, dt))
    return pl.pallas_call(
        body, name=name, grid=(t // tm,), in_specs=in_specs, out_specs=out_specs, out_shape=out_shape,
        compiler_params=_params(("arbitrary",)),
    )(*[a for (a, _, _) in row_ins], *full_ins)


def _k_in(cfg, r, g1, wcat):
    def body(r_ref, g_ref, w_ref, h_ref, proj_ref):
        y, _, _ = _rms_fwd(r_ref[...], g_ref[...])
        hb = y.astype(bf16)
        h_ref[...] = hb
        proj_ref[...] = _mm_nt(hb, w_ref[...])

    return _row_call("in_proj", body, cfg.T, cfg.TM, [(r, cfg.D, 0)], [g1, wcat],
                     [(cfg.D, bf16), (cfg.PC, f32)], [])


def _k_prep(cfg, proj, conv_w):
    dn, s = cfg.DN, cfg.S

    def body(x_ref, w_ref, o_ref):
        sec = pl.program_id(1)
        c = _conv_fwd(x_ref[...], w_ref, 2)
        y, _ = _silu(c)
        for h in range(cfg.H):
            yh = y[:, h * 128:(h + 1) * 128]
            nh = yh * lax.rsqrt(jnp.sum(yh * yh, axis=1, keepdims=True) + EPS)
            o_ref[:, h * 128:(h + 1) * 128] = jnp.where(sec < 2, nh, yh)

    return pl.pallas_call(
        body, name="dn_prep", grid=(cfg.BL, 3),
        in_specs=[pl.BlockSpec((s, dn), lambda b, j: (b, j)), pl.BlockSpec((4, dn), lambda b, j: (0, j))],
        out_specs=pl.BlockSpec((s, dn), lambda b, j: (b, j)),
        out_shape=jax.ShapeDtypeStruct((cfg.T, 3 * dn), f32),
        compiler_params=_params(("arbitrary", "arbitrary")),
    )(proj, conv_w)


def _gate_cols(ba, alog_row, dt_row, lane_b, lane_a):
    beta = _sigmoid(_lane_pick(ba, lane_b))
    alpha = _lane_pick(ba, lane_a)
    aexp = jnp.exp(_lane_pick(alog_row, lane_a))
    dtb = _lane_pick(dt_row, lane_a)
    xa = alpha + dtb
    g = -aexp * _softplus(xa)
    return beta, g, aexp, xa


def _gdn_specs(cfg):
    s, c, nc = cfg.S, cfg.C, cfg.NC
    tok2 = pl.BlockSpec((2, s, 128), lambda b, h: (0, b, h))
    mat2 = pl.BlockSpec((2, 1, nc, c, c), lambda b, h: (0, b, h, 0, 0))
    cd2 = pl.BlockSpec((2, 1, nc, 8, LANES), lambda b, h: (0, b, h, 0, 0))
    shapes = dict(
        tok32=jax.ShapeDtypeStruct((2, cfg.T, cfg.DN), f32), tok16=jax.ShapeDtypeStruct((2, cfg.T, cfg.DN), bf16),
        mat32=jax.ShapeDtypeStruct((2, cfg.BL, cfg.H * nc, c, c), f32), mat16=jax.ShapeDtypeStruct((2, cfg.BL, cfg.H * nc, c, c), bf16),
        cd=jax.ShapeDtypeStruct((2, cfg.BL, cfg.H * nc, 8, LANES), f32))
    return tok2, mat2, cd2, shapes


def _k_gdn_fwd(cfg, qkv, proj, alog_row, dt_row):
    s, c, nc, hh = cfg.S, cfg.C, cfg.NC, cfg.H
    ba_blk = cfg.BAO // LANES
    per = 4 if nc % 4 == 0 else 1

    def body(q_ref, k_ref, v_ref, ba_ref, al_ref, dt_ref, o_ref, u_o, w_o, qd_o, kd_o, p_o, t_o, dec_o, cd_o):
        h = pl.program_id(1)
        groups = nc // per

        def chunk_of(d, g, j):
            return (nc - 1 - (g * per + j)) if d == 1 else (g * per + j)

        def prep(g, tick=lambda: None):
            chains, qs, k, v, kk, qk, gt, beta = [], [], [], [], [], [], [], []
            for j in range(per):
                for d in range(2):
                    n = chunk_of(d, g, j)
                    rows = pl.ds(pl.multiple_of(n * c, c), c)
                    kj, vj = k_ref[rows, :], v_ref[rows, :]
                    qj = q_ref[rows, :] * (128 ** -0.5)
                    bd, gd, _, _ = _gate_cols(ba_ref[rows, :], al_ref[...], dt_ref[...], d * hh + h, 2 * hh + d * hh + h)
                    chains.append((d, n, rows))
                    for lst, val in ((qs, qj), (k, kj), (v, vj), (kk, _mm_nt(kj, kj)), (qk, _mm_nt(qj, kj)), (gt, gd), (beta, bd)):
                        lst.append(val)
            z = _gdn_prep(qs, k, v, gt, beta, kk, qk, [d == 1 for d, _, _ in chains], tick)
            for x, (d, n, rows) in enumerate(chains):
                u_o[d, rows, :] = z["u"][x]
                w_o[d, rows, :] = z["w"][x].astype(bf16)
                qd_o[d, rows, :] = z["qd"][x].astype(bf16)
                kd_o[d, rows, :] = z["kd"][x].astype(bf16)
                p_o[d, 0, n] = z["p"][x].astype(bf16)
                t_o[d, 0, n] = z["tm"][x]
                dec_o[d, 0, n] = z["dec"][x]
                cd_o[d, 0, n] = jnp.broadcast_to(z["cd"][x], (8, LANES))

        def steps(g, box):
            for j in range(per):
                sts = box[0]
                ns = [chunk_of(d, g, j) for d in range(2)]
                rows = [pl.ds(pl.multiple_of(n * c, c), c) for n in ns]
                ws = [_mm(w_o[d, rows[d], :], sts[d]) for d in range(2)]
                qs_ = [_mm(qd_o[d, rows[d], :], sts[d]) for d in range(2)]
                yield
                vn = [u_o[d, rows[d], :] - ws[d] for d in range(2)]
                box[0] = tuple(sts[d] * cd_o[d, 0, ns[d]][0:1, :] + _mm_tn(kd_o[d, rows[d], :], vn[d]) for d in range(2))
                for d in range(2):
                    o_ref[rows[d], :] = o_ref[rows[d], :] + qs_[d] + _mm(p_o[d, 0, ns[d]], vn[d])
                yield

        o_ref[...] = jnp.zeros_like(o_ref)
        prep(0)

        def fused(g, sts):
            box = [sts]
            chain = steps(g - 1, box)
            prep(g, lambda: next(chain, None))
            for _ in chain:
                pass
            return box[0]

        z0 = jnp.zeros((128, 128), f32)
        box = [lax.fori_loop(1, groups, fused, (z0, z0))]
        for _ in steps(groups - 1, box):
            pass

    blk = lambda off: pl.BlockSpec((s, 128), functools.partial(lambda b, h, off: (b, off + h), off=off))
    row = pl.BlockSpec((1, LANES), lambda b, h: (0, 0))
    tok2, mat2, cd2, shp = _gdn_specs(cfg)
    return pl.pallas_call(
        body, name="gdn_fwd", grid=(cfg.BL, hh),
        in_specs=[blk(0), blk(hh), blk(2 * hh), pl.BlockSpec((s, LANES), lambda b, h: (b, ba_blk)), row, row],
        out_specs=[pl.BlockSpec((s, 128), lambda b, h: (b, h)), tok2, tok2, tok2, tok2, mat2, mat2, mat2, cd2],
        out_shape=[jax.ShapeDtypeStruct((cfg.T, cfg.DN), f32), shp["tok32"], shp["tok16"], shp["tok16"], shp["tok16"],
                   shp["mat16"], shp["mat32"], shp["mat32"], shp["cd"]],
        compiler_params=_params(("arbitrary", "arbitrary")),
    )(qkv, qkv, qkv, proj, alog_row, dt_row)


def _lru_gates(xc, wa, wx, ba, bx, lam):
    ra = _sigmoid(_mm(xc, wa) + ba)
    ig = _sigmoid(_mm(xc, wx) + bx)
    sp = _softplus(-lam)
    la = -LRU_C * ra * sp
    a = jnp.exp(la)
    m = jnp.sqrt(_nexpm1(2.0 * la))
    return dict(ra=ra, ig=ig, sp=sp, a=a, m=m, gx=ig * xc)


def _lru_specs(cfg, outer_b):
    s = cfg.S
    lx_blk = cfg.LXO // LANES
    if outer_b:
        ix = lambda f: (lambda b, ct: f(b, ct))
    else:
        ix = lambda f: (lambda ct, b: f(b, ct))
    return dict(
        lx=pl.BlockSpec((s, LANES), ix(lambda b, ct: (b, lx_blk + ct))),
        tok=pl.BlockSpec((s, LANES), ix(lambda b, ct: (b, ct))),
        cw=pl.BlockSpec((4, LANES), ix(lambda b, ct: (0, ct))),
        row=pl.BlockSpec((1, LANES), ix(lambda b, ct: (0, ct))),
        w=pl.BlockSpec((2, 1, LANES, LANES), ix(lambda b, ct: (0, ct, 0, 0))),
        two=pl.BlockSpec((2, LANES), ix(lambda b, ct: (0, ct))),
    )


def _k_lru_fwd(cfg, proj, conv_w, conv_b, wa, wx, ba, bx, lam):
    def body(lx_ref, cw_ref, cb_ref, wa_ref, wx_ref, ba_ref, bx_ref, lam_ref, o_ref, a_s, b_s, h_s):
        xc = _conv_fwd(lx_ref[...], cw_ref, 2) + cb_ref[...]
        for d in range(2):
            z = _lru_gates(xc, wa_ref[d, 0], wx_ref[d, 0], ba_ref[d:d + 1, :], bx_ref[d:d + 1, :], lam_ref[d:d + 1, :])
            a_s[d] = z["a"]
            b_s[d] = z["m"] * z["gx"]
        _scan_refs([(a_s.at[d], b_s.at[d], h_s.at[d], d == 1) for d in range(2)], cfg.S)
        o_ref[...] = h_s[0] + h_s[1]

    sp = _lru_specs(cfg, True)
    return pl.pallas_call(
        body, name="lru_fwd", grid=(cfg.BL, cfg.LW // LANES),
        in_specs=[sp["lx"], sp["cw"], sp["row"], sp["w"], sp["w"], sp["two"], sp["two"], sp["two"]],
        out_specs=sp["tok"], out_shape=jax.ShapeDtypeStruct((cfg.T, cfg.LW), f32),
        scratch_shapes=[pltpu.VMEM((2, cfg.S, LANES), f32)] * 3,
        compiler_params=_params(("arbitrary", "arbitrary")),
    )(proj, conv_w, conv_b, wa, wx, ba, bx, lam)


def _mix_parts(cfg, o, z, lg, hs, dng, lrg):
    heads = []
    for h in range(cfg.H):
        sl = slice(h * 128, (h + 1) * 128)
        y, xh, inv = _rms_fwd(o[:, sl], dng)
        sz, dsz = _silu(z[:, sl])
        heads.append((y, xh, inv, sz, dsz))
    gl, dgl = _gelu(lg)
    y2, xh2, inv2 = _rms_fwd(gl * hs, lrg)
    return heads, (y2, xh2, inv2, gl, dgl)


def _k_mix(cfg, o, proj, hs, r, dng, lrg, wout):
    dn = cfg.DN

    def body(o_ref, z_ref, lg_ref, hs_ref, r_ref, dng_ref, lrg_ref, w_ref, mix_ref, out_ref):
        heads, lru = _mix_parts(cfg, o_ref[...], z_ref[...], lg_ref[...], hs_ref[...], dng_ref[...], lrg_ref[...])
        for h, (y, _, _, sz, _) in enumerate(heads):
            mix_ref[:, h * 128:(h + 1) * 128] = (y * sz).astype(bf16)
        mix_ref[:, dn:] = lru[0].astype(bf16)
        out_ref[...] = r_ref[...] + jnp.dot(mix_ref[...], w_ref[...], preferred_element_type=f32)

    return _row_call("mix_out", body, cfg.T, cfg.TM,
                     [(o, dn, 0), (proj, dn, cfg.ZO // dn), (proj, cfg.LW, cfg.LGO // cfg.LW), (hs, cfg.LW, 0), (r, cfg.D, 0)],
                     [dng, lrg, wout], [(cfg.D, bf16), (cfg.D, f32)], [])


def _k_ffn_a(cfg, r, g2, wg, wu):
    def body(r_ref, g_ref, wg_ref, wu_ref, h_ref, gp_ref, up_ref):
        y, _, _ = _rms_fwd(r_ref[...], g_ref[...])
        hb = y.astype(bf16)
        h_ref[...] = hb
        gp_ref[...] = _mm_nt(hb, wg_ref[...])
        up_ref[...] = _mm_nt(hb, wu_ref[...])

    return _row_call("ffn_in", body, cfg.T, cfg.TMF, [(r, cfg.D, 0)], [g2, wg, wu],
                     [(cfg.D, bf16), (cfg.FF, f32), (cfg.FF, f32)], [])


def _k_ffn_b(cfg, gp, up, conv_w, conv_b):
    s, ft = cfg.S, cfg.FT

    def body(gp_ref, up_ref, w_ref, b_ref, o_ref):
        gate = _conv_fwd(gp_ref[...], w_ref, 1) + b_ref[...]
        gl, _ = _gelu(gate)
        o_ref[...] = (gl * up_ref[...]).astype(bf16)

    tok = pl.BlockSpec((s, ft), lambda b, j: (b, j))
    return pl.pallas_call(
        body, name="ffn_act", grid=(cfg.BL, cfg.FF // ft),
        in_specs=[tok, tok, pl.BlockSpec((3, ft), lambda b, j: (0, j)), pl.BlockSpec((1, ft), lambda b, j: (0, j))],
        out_specs=tok, out_shape=jax.ShapeDtypeStruct((cfg.T, cfg.FF), bf16),
        compiler_params=_params(("arbitrary", "arbitrary")),
    )(gp, up, conv_w, conv_b)


def _k_ffn_c(cfg, act, r, wd):
    def body(a_ref, r_ref, w_ref, o_ref):
        o_ref[...] = r_ref[...] + jnp.dot(a_ref[...], w_ref[...], preferred_element_type=f32)

    return _row_call("ffn_out", body, cfg.T, cfg.TM, [(act, cfg.FF, 0), (r, cfg.D, 0)], [wd], [(cfg.D, f32)], [])[0]


def _k_ple(cfg, r, p, gp, wpg, bg, wpp):
    def body(r_ref, p_ref, g_ref, wg_ref, bg_ref, wp_ref, pn_ref, o_ref):
        x = r_ref[...]
        y, _, _ = _rms_fwd(x, g_ref[...])
        pn = y.astype(bf16)
        pn_ref[...] = pn
        pg = _sigmoid(jnp.dot(pn, wg_ref[...], preferred_element_type=f32) + bg_ref[...])
        o_ref[...] = x + pg * _mm_nt(p_ref[...], wp_ref[...])

    return _row_call("ple", body, cfg.T, cfg.TM, [(r, cfg.D, 0), (p, cfg.PD, 0)], [gp, wpg, bg, wpp],
                     [(cfg.D, bf16), (cfg.D, f32)], [])


def _k_loss(cfg, r, tgt, gf):
    d = cfg.D

    def body(r_ref, t_ref, g_ref, dr_ref, loss_ref, dg_ref):
        @pl.when(pl.program_id(0) == 0)
        def _():
            loss_ref[...] = jnp.zeros_like(loss_ref)
            dg_ref[...] = jnp.zeros_like(dg_ref)

        g = g_ref[...]
        y, xh, inv = _rms_fwd(r_ref[...], g)
        err = y - t_ref[...]
        loss_ref[...] = loss_ref[...] + (0.5 / d) * jnp.sum(err * err)
        dx, dg = _rms_bwd(err * (1.0 / d), xh, inv, g)
        dr_ref[...] = dx
        dg_ref[...] = dg_ref[...] + dg

    return _row_call("loss_head", body, cfg.T, cfg.TM, [(r, d, 0), (tgt, d, 0)], [gf], [(d, f32)],
                     [((1, LANES), f32), ((1, d), f32)])


def _zero_at_first(cond, *refs):
    @pl.when(cond)
    def _():
        for r in refs:
            r[...] = jnp.zeros_like(r)


def _b_ple(cfg, dr3, r2, p, gp, wpg, bg, wpp):
    d = cfg.D

    def body(dr_ref, r_ref, p_ref, g_ref, wg_ref, bg_ref, wp_ref, dr2_ref, dlog_ref, dpp_ref, dgp_ref, dbg_ref):
        _zero_at_first(pl.program_id(0) == 0, dgp_ref, dbg_ref)
        g = g_ref[...]
        dr = dr_ref[...]
        y, xh, inv = _rms_fwd(r_ref[...], g)
        pg = _sigmoid(jnp.dot(y.astype(bf16), wg_ref[...], preferred_element_type=f32) + bg_ref[...])
        pp = _mm_nt(p_ref[...], wp_ref[...])
        dpp_ref[...] = (dr * pg).astype(bf16)
        dlog = dr * pp * pg * (1.0 - pg)
        dlog_ref[...] = dlog.astype(bf16)
        dbg_ref[...] = dbg_ref[...] + jnp.sum(dlog, axis=0, keepdims=True)
        dx, dg = _rms_bwd(_mm_nt(dlog, wg_ref[...]), xh, inv, g)
        dgp_ref[...] = dgp_ref[...] + dg
        dr2_ref[...] = dr + dx

    return _row_call("ple_bwd", body, cfg.T, cfg.TM, [(dr3, d, 0), (r2, d, 0), (p, cfg.PD, 0)], [gp, wpg, bg, wpp],
                     [(d, f32), (d, bf16), (d, bf16)], [((1, d), f32), ((1, d), f32)])


def _b_ffn_bc(cfg, dr2, wd, gp, up, conv_w, conv_b):
    s, ft, d = cfg.S, cfg.FT, cfg.D

    def body(dr_ref, wd_ref, gp_ref, up_ref, w_ref, b_ref, dgp_ref, dup_ref, dwd_ref, dcw_ref, dcb_ref):
        _zero_at_first(pl.program_id(1) == 0, dwd_ref, dcw_ref, dcb_ref)
        drb = dr_ref[...].astype(bf16)
        dact = _mm_nt(drb, wd_ref[...])
        gpre = gp_ref[...]
        up = up_ref[...]
        gl, dgl = _gelu(_conv_fwd(gpre, w_ref, 1) + b_ref[...])
        dup_ref[...] = (dact * gl).astype(bf16)
        dgate = dact * up * dgl
        dcb_ref[...] = dcb_ref[...] + jnp.sum(dgate, axis=0, keepdims=True)
        dx, dws = _conv_bwd(dgate, gpre, w_ref, 1)
        for j, dw in enumerate(dws):
            dcw_ref[j:j + 1, :] = dcw_ref[j:j + 1, :] + dw
        dgp_ref[...] = dx.astype(bf16)
        dwd_ref[...] = dwd_ref[...] + _mm_tn((gl * up).astype(bf16), drb)

    tok = pl.BlockSpec((s, ft), lambda j, b: (b, j))
    return pl.pallas_call(
        body, name="ffn_act_bwd", grid=(cfg.FF // ft, cfg.BL),
        in_specs=[pl.BlockSpec((s, d), lambda j, b: (b, 0)), pl.BlockSpec((ft, d), lambda j, b: (j, 0)), tok, tok,
                  pl.BlockSpec((3, ft), lambda j, b: (0, j)), pl.BlockSpec((1, ft), lambda j, b: (0, j))],
        out_specs=[tok, tok, pl.BlockSpec((ft, d), lambda j, b: (j, 0)), pl.BlockSpec((3, ft), lambda j, b: (0, j)),
                   pl.BlockSpec((1, ft), lambda j, b: (0, j))],
        out_shape=[jax.ShapeDtypeStruct((cfg.T, cfg.FF), bf16), jax.ShapeDtypeStruct((cfg.T, cfg.FF), bf16),
                   jax.ShapeDtypeStruct((cfg.FF, d), f32), jax.ShapeDtypeStruct((3, cfg.FF), f32),
                   jax.ShapeDtypeStruct((1, cfg.FF), f32)],
        compiler_params=_params(("arbitrary", "arbitrary")),
    )(dr2, wd, gp, up, conv_w, conv_b)


def _b_ffn_a(cfg, dgp, dup, dr2, r_mid, g2, wg, wu):
    d = cfg.D

    def body(dgp_ref, dup_ref, dr_ref, r_ref, g_ref, wg_ref, wu_ref, o_ref, dg_ref):
        _zero_at_first(pl.program_id(0) == 0, dg_ref)
        g = g_ref[...]
        _, xh, inv = _rms_fwd(r_ref[...], g)
        dh = _mm(dgp_ref[...], wg_ref[...]) + _mm(dup_ref[...], wu_ref[...])
        dx, dg = _rms_bwd(dh, xh, inv, g)
        dg_ref[...] = dg_ref[...] + dg
        o_ref[...] = dr_ref[...] + dx

    return _row_call("ffn_in_bwd", body, cfg.T, cfg.TMF, [(dgp, cfg.FF, 0), (dup, cfg.FF, 0), (dr2, d, 0), (r_mid, d, 0)],
                     [g2, wg, wu], [(d, f32)], [((1, d), f32)])


def _mm_tn_call(cfg, name, x, dy, tn):
    t, k = x.shape
    n = dy.shape[1]
    tm = cfg.TM

    def body(x_ref, dy_ref, o_ref):
        _zero_at_first(pl.program_id(1) == 0, o_ref)
        o_ref[...] = o_ref[...] + _mm_tn(x_ref[...], dy_ref[...])

    return pl.pallas_call(
        body, name=name, grid=(n // tn, t // tm),
        in_specs=[pl.BlockSpec((tm, k), lambda j, i: (i, 0)), pl.BlockSpec((tm, tn), lambda j, i: (i, j))],
        out_specs=pl.BlockSpec((k, tn), lambda j, i: (0, j)),
        out_shape=jax.ShapeDtypeStruct((k, n), f32),
        compiler_params=_params(("arbitrary", "arbitrary")),
    )(x, dy)


def _b_mix(cfg, dr, o, proj, hs, dng, lrg, wout):
    dn, lw, d = cfg.DN, cfg.LW, cfg.D

    def body(dr_ref, o_ref, z_ref, lg_ref, hs_ref, dng_ref, lrg_ref, w_ref, do_ref, dz_ref, dlg_ref, dhs_ref, ddn_ref, dlr_ref):
        _zero_at_first(pl.program_id(0) == 0, ddn_ref, dlr_ref)
        dng, lrg = dng_ref[...], lrg_ref[...]
        hs = hs_ref[...]
        o, z = o_ref[...], z_ref[...]
        heads, lru = _mix_parts(cfg, o, z, lg_ref[...], hs, dng, lrg)
        drb = dr_ref[...].astype(bf16)
        dmix_dn = _mm_nt(drb, w_ref[0:dn, :])
        dmix_lr = _mm_nt(drb, w_ref[dn:, :])
        dgn = jnp.zeros_like(dng)
        for h, (y, xh, inv, sz, dsz) in enumerate(heads):
            sl = slice(h * 128, (h + 1) * 128)
            dm = dmix_dn[:, sl]
            dz_ref[:, sl] = dm * y * dsz
            dx, dg = _rms_bwd(dm * sz, xh, inv, dng)
            do_ref[:, sl] = dx
            dgn = dgn + dg
        ddn_ref[...] = ddn_ref[...] + dgn
        _, xh2, inv2, gl, dgl = lru
        dx2, dg2 = _rms_bwd(dmix_lr, xh2, inv2, lrg)
        dlr_ref[...] = dlr_ref[...] + dg2
        dlg_ref[...] = dx2 * hs * dgl
        dhs_ref[...] = dx2 * gl

    return _row_call("mix_bwd", body, cfg.T, cfg.TM,
                     [(dr, d, 0), (o, dn, 0), (proj, dn, cfg.ZO // dn), (proj, lw, cfg.LGO // lw), (hs, lw, 0)],
                     [dng, lrg, wout], [(dn, f32), (dn, f32), (lw, f32), (lw, f32)], [((1, 128), f32), ((1, lw), f32)])


def _b_lru(cfg, proj, dhs, conv_w, conv_b, wa, wx, ba, bx, lam):
    def body(lx_ref, dh_ref, cw_ref, cb_ref, wa_ref, wx_ref, ba_ref, bx_ref, lam_ref,
             dlx_ref, dcw_ref, dcb_ref, dwa_ref, dwx_ref, dba_ref, dbx_ref, dlam_ref, a_s, b_s, h_s, an_s, l_s):
        _zero_at_first(pl.program_id(1) == 0, dcw_ref, dcb_ref, dwa_ref, dwx_ref, dba_ref, dbx_ref, dlam_ref)
        lx = lx_ref[...]
        xc = _conv_fwd(lx, cw_ref, 2) + cb_ref[...]
        dxc = jnp.zeros_like(xc)
        gates = []
        for d in range(2):
            z = _lru_gates(xc, wa_ref[d, 0], wx_ref[d, 0], ba_ref[d:d + 1, :], bx_ref[d:d + 1, :], lam_ref[d:d + 1, :])
            a_s[d] = z["a"]
            b_s[d] = z["m"] * z["gx"]
            an_s[d] = _shift(z["a"], -1 if d == 1 else 1, 0.0)
            gates.append(z)
        _scan_refs([(a_s.at[d], b_s.at[d], h_s.at[d], d == 1) for d in range(2)]
                   + [(an_s.at[d], dh_ref, l_s.at[d], d == 0) for d in range(2)], cfg.S)
        for d in range(2):
            rev = d == 1
            lam = lam_ref[d:d + 1, :]
            z = gates[d]
            a, m, ra, ig, sp = z["a"], z["m"], z["ra"], z["ig"], z["sp"]
            lmb = l_s[d]
            h_prev = _shift(h_s[d], 1 if rev else -1, 0.0)
            da = lmb * h_prev
            dm = lmb * z["gx"]
            dgx = lmb * m
            dla = da * a - dm * (a * a) / jnp.maximum(m, 1e-30)
            dra = dla * (-LRU_C) * sp
            dsp = jnp.sum(dla * (-LRU_C) * ra, axis=0, keepdims=True)
            dlam_ref[d:d + 1, :] = dlam_ref[d:d + 1, :] - dsp * _sigmoid(-lam)
            dpa = dra * ra * (1.0 - ra)
            dpx = dgx * xc * ig * (1.0 - ig)
            dba_ref[d:d + 1, :] = dba_ref[d:d + 1, :] + jnp.sum(dpa, axis=0, keepdims=True)
            dbx_ref[d:d + 1, :] = dbx_ref[d:d + 1, :] + jnp.sum(dpx, axis=0, keepdims=True)
            dwa_ref[d, 0] = dwa_ref[d, 0] + _mm_tn(xc, dpa)
            dwx_ref[d, 0] = dwx_ref[d, 0] + _mm_tn(xc, dpx)
            dxc = dxc + dgx * ig + _mm_nt(dpa, wa_ref[d, 0]) + _mm_nt(dpx, wx_ref[d, 0])
        dcb_ref[...] = dcb_ref[...] + jnp.sum(dxc, axis=0, keepdims=True)
        dx, dws = _conv_bwd(dxc, lx, cw_ref, 2)
        for j, dw in enumerate(dws):
            dcw_ref[j:j + 1, :] = dcw_ref[j:j + 1, :] + dw
        dlx_ref[...] = dx

    sp = _lru_specs(cfg, False)
    nct = cfg.LW // LANES
    return pl.pallas_call(
        body, name="lru_bwd", grid=(nct, cfg.BL),
        in_specs=[sp["lx"], sp["tok"], sp["cw"], sp["row"], sp["w"], sp["w"], sp["two"], sp["two"], sp["two"]],
        out_specs=[sp["tok"], sp["cw"], sp["row"], sp["w"], sp["w"], sp["two"], sp["two"], sp["two"]],
        out_shape=[jax.ShapeDtypeStruct((cfg.T, cfg.LW), f32), jax.ShapeDtypeStruct((4, cfg.LW), f32),
                   jax.ShapeDtypeStruct((1, cfg.LW), f32), jax.ShapeDtypeStruct((2, nct, LANES, LANES), f32),
                   jax.ShapeDtypeStruct((2, nct, LANES, LANES), f32), jax.ShapeDtypeStruct((2, cfg.LW), f32),
                   jax.ShapeDtypeStruct((2, cfg.LW), f32), jax.ShapeDtypeStruct((2, cfg.LW), f32)],
        scratch_shapes=[pltpu.VMEM((2, cfg.S, LANES), f32)] * 5,
        compiler_params=_params(("arbitrary", "arbitrary")),
    )(proj, dhs, conv_w, conv_b, wa, wx, ba, bx, lam)


def _b_gdn(cfg, qkv, proj, do, alog_row, dt_row, saved):
    s, c, nc, hh = cfg.S, cfg.C, cfg.NC, cfg.H
    ba_blk = cfg.BAO // LANES
    scale = 128 ** -0.5
    per = 4 if nc % 4 == 0 else 1

    def body(q_ref, k_ref, v_ref, ba_ref, do_ref, al_ref, dt_ref, u_i, w_i, qd_i, kd_i, p_i, t_i, dec_i, cd_i,
             dqkv_ref, dba_ref, dal_ref, ddt_ref, vn_s, dvn_s, st_s, dst_s):
        h = pl.program_id(1)
        _zero_at_first((pl.program_id(0) == 0) & (h == 0), dal_ref, ddt_ref)
        _zero_at_first(h == 0, dba_ref)
        lane = lax.broadcasted_iota(jnp.int32, (c, LANES), 1)
        lane1 = lax.broadcasted_iota(jnp.int32, (1, LANES), 1)
        dirs = (0, 1)
        lane_b = [d * hh + h for d in dirs]
        lane_a = [2 * hh + d * hh + h for d in dirs]

        def seq(i, carry):
            sts, dst = carry
            nf = [i, nc - 1 - i]
            nb = [nc - 1 - i, i]
            rf = [pl.ds(pl.multiple_of(n * c, c), c) for n in nf]
            rb = [pl.ds(pl.multiple_of(n * c, c), c) for n in nb]
            for d in dirs:
                st_s[d, nf[d]] = sts[d]
                dst_s[d, nb[d]] = dst[d]
            dob = [do_ref[rb[d], :] for d in dirs]
            vn = [u_i[d, rf[d], :] - _mm(w_i[d, rf[d], :], sts[d]) for d in dirs]
            dvn = [_mm_tn(p_i[d, 0, nb[d]], dob[d]) + _mm(kd_i[d, rb[d], :], dst[d]) for d in dirs]
            for d in dirs:
                vn_s[d, rf[d], :] = vn[d]
                dvn_s[d, rb[d], :] = dvn[d]
            sts = tuple(sts[d] * cd_i[d, 0, nf[d]][0:1, :] + _mm_tn(kd_i[d, rf[d], :], vn[d]) for d in dirs)
            dst = tuple(_mm_tn(qd_i[d, rb[d], :], dob[d]) + cd_i[d, 0, nb[d]][0:1, 0:1] * dst[d]
                        - _mm_tn(w_i[d, rb[d], :], dvn[d]) for d in dirs)
            return sts, dst

        z0 = jnp.zeros((128, 128), f32)
        lax.fori_loop(0, nc, seq, ((z0, z0), (z0, z0)))

        def bpar(i, carry):
            ch = [(d, i * per + j) for j in range(per) for d in dirs]
            rows = [pl.ds(pl.multiple_of(n * c, c), c) for _, n in ch]
            masks = [_tri_masks(c, d == 1) for d, _ in ch]
            ld = lambda ref: [ref[d, r, :] for (d, _), r in zip(ch, rows)]
            ldm = lambda ref: [ref[d, 0, n] for d, n in ch]
            q, k, v, dob = ([ref[r, :] for r in rows] for ref in (q_ref, k_ref, v_ref, do_ref))
            gates = [_gate_cols(ba_ref[r, :], al_ref[...], dt_ref[...], lane_b[d], lane_a[d]) for (d, _), r in zip(ch, rows)]
            beta, g, aexp, xa = ([gt[x] for gt in gates] for x in range(4))
            st, dst = ([ref[d, n] for d, n in ch] for ref in (st_s, dst_s))
            vn, dvn, u, w = ld(vn_s), ld(dvn_s), ld(u_i), ld(w_i)
            p = [x.astype(f32) for x in ldm(p_i)]
            tm, dec = ldm(t_i), ldm(dec_i)
            cd = [cd_i[d, 0, n][0:1, 0:1] for d, n in ch]
            dp = _map(lambda m, a, b: jnp.where(m[0], _mm_nt(a, b), 0.0), masks, dob, vn)
            dqd = _map(_mm_nt, dob, st)
            dkd = _map(_mm_nt, vn, dst)
            dw = _map(lambda a, b: -_mm_nt(a, b), dvn, st)
            dcd = _map(lambda a, b: jnp.sum(jnp.sum(a * b, axis=1, keepdims=True), axis=0, keepdims=True), st, dst)
            gcum = _map(lambda m, x: _cum_mm(m[0], jnp.broadcast_to(x, (c, c)))[:, 0:1], masks, g)
            glast = [jnp.sum(x, axis=0, keepdims=True) for x in g]
            e = [jnp.exp(x) for x in gcum]
            el = _map(lambda a, b: jnp.exp(a - b), glast, gcum)
            qs = [x * scale for x in q]
            kb = _map(lambda a, b: a * b, k, beta)
            a = _map(lambda m, b, ki, dc: jnp.where(m[1], b * _mm_nt(ki, ki) * dc, 0.0), masks, beta, k, dec)
            dvb = _map(_hmm3_tn, tm, dvn)
            dkbe = _map(_hmm3_tn, tm, dw)
            da = _map(lambda m, x, ui, y, wi: -jnp.where(m[1], _mm_nt(x, ui) + _mm_nt(y, wi), 0.0), masks, dvb, u, dkbe, w)
            g1 = _map(lambda x, y: x * y, da, dec)
            g2 = _map(lambda x, y: x * y, dp, dec)
            dkb = _map(lambda x, ki, y, ei: _mm(x, ki) + y * ei, g1, k, dkbe, e)
            dk = _map(lambda x, kbi, y, qi, z, b, t, l: _mm_tn(x, kbi) + _mm_tn(y, qi) + z * b + t * l,
                      g1, kb, g2, qs, dkb, beta, dkd, el)
            dqs = _map(lambda y, ki, x, ei: _mm(y, ki) + x * ei, g2, k, dqd, e)
            ddd = _map(lambda x, ai, y, pi: x * ai + y * pi, da, a, dp, p)
            ones = jnp.ones((c, LANES), f32)
            dgcum = _map(lambda x: jnp.sum(x, axis=1, keepdims=True) - _hmm3_tn(x, ones)[:, 0:1], ddd)
            for x, (d, n) in enumerate(ch):
                dbeta = jnp.sum(dvb[x] * v[x], axis=1, keepdims=True) + jnp.sum(dkb[x] * k[x], axis=1, keepdims=True)
                de = jnp.sum(dkbe[x] * kb[x], axis=1, keepdims=True) + jnp.sum(dqd[x] * qs[x], axis=1, keepdims=True)
                del_ = jnp.sum(dkd[x] * k[x], axis=1, keepdims=True)
                dgc = dgcum[x] + de * e[x] - del_ * el[x]
                dglast = jnp.sum(del_ * el[x], axis=0, keepdims=True) + dcd[x] * cd[x]
                dg = _cum_mm(masks[x][2], jnp.broadcast_to(dgc, (c, LANES)))[:, 0:1] + dglast
                r = rows[x]
                if d == 0:
                    dqkv_ref[0, r, :] = dqs[x] * scale
                    dqkv_ref[1, r, :] = dk[x]
                    dqkv_ref[2, r, :] = dvb[x] * beta[x]
                else:
                    dqkv_ref[0, r, :] = dqkv_ref[0, r, :] + dqs[x] * scale
                    dqkv_ref[1, r, :] = dqkv_ref[1, r, :] + dk[x]
                    dqkv_ref[2, r, :] = dqkv_ref[2, r, :] + dvb[x] * beta[x]
                dlb = dbeta * beta[x] * (1.0 - beta[x])
                dalpha = -dg * aexp[x] * _sigmoid(xa[x])
                dba_ref[r, :] = dba_ref[r, :] + jnp.where(lane == lane_b[d], dlb, 0.0) + jnp.where(lane == lane_a[d], dalpha, 0.0)
                dal_ref[...] = dal_ref[...] + jnp.where(lane1 == lane_a[d], jnp.sum(dg * g[x], axis=0, keepdims=True), 0.0)
                ddt_ref[...] = ddt_ref[...] + jnp.where(lane1 == lane_a[d], jnp.sum(dalpha, axis=0, keepdims=True), 0.0)
            return carry

        lax.fori_loop(0, nc // per, bpar, 0)

    blk = lambda off: pl.BlockSpec((s, 128), functools.partial(lambda b, h, off: (b, off + h), off=off))
    row = pl.BlockSpec((1, LANES), lambda b, h: (0, 0))
    tok2, mat2, cd2, _ = _gdn_specs(cfg)
    return pl.pallas_call(
        body, name="gdn_bwd", grid=(cfg.BL, hh),
        in_specs=[blk(0), blk(hh), blk(2 * hh), pl.BlockSpec((s, LANES), lambda b, h: (b, ba_blk)),
                  pl.BlockSpec((s, 128), lambda b, h: (b, h)), row, row, tok2, tok2, tok2, tok2, mat2, mat2, mat2, cd2],
        out_specs=[pl.BlockSpec((3, s, 128), lambda b, h: (0, b, h)), pl.BlockSpec((s, LANES), lambda b, h: (b, 0)), row, row],
        out_shape=[jax.ShapeDtypeStruct((3, cfg.T, cfg.DN), f32), jax.ShapeDtypeStruct((cfg.T, LANES), f32),
                   jax.ShapeDtypeStruct((1, LANES), f32), jax.ShapeDtypeStruct((1, LANES), f32)],
        scratch_shapes=[pltpu.VMEM((2, s, 128), f32)] * 2 + [pltpu.VMEM((2, nc, 128, 128), f32)] * 2,
        compiler_params=_params(("arbitrary", "arbitrary")),
    )(qkv, qkv, qkv, proj, do, alog_row, dt_row, *saved)


def _b_prep(cfg, proj, dqkv, conv_w):
    dn, s = cfg.DN, cfg.S

    def body(x_ref, dy_ref, w_ref, dx_ref, dw_ref):
        _zero_at_first(pl.program_id(1) == 0, dw_ref)
        sec = pl.program_id(0)
        x = x_ref[...]
        c = _conv_fwd(x, w_ref, 2)
        y, dsilu = _silu(c)
        dy = dy_ref[0]
        parts = []
        for h in range(cfg.H):
            sl = slice(h * 128, (h + 1) * 128)
            yh, dyh = y[:, sl], dy[:, sl]
            inv = lax.rsqrt(jnp.sum(yh * yh, axis=1, keepdims=True) + EPS)
            dn_h = inv * dyh - yh * (inv * inv * inv) * jnp.sum(dyh * yh, axis=1, keepdims=True)
            parts.append(jnp.where(sec < 2, dn_h, dyh))
        ds = jnp.concatenate(parts, axis=1) if len(parts) > 1 else parts[0]
        dx, dws = _conv_bwd(ds * dsilu, x, w_ref, 2)
        for j, dw in enumerate(dws):
            dw_ref[j:j + 1, :] = dw_ref[j:j + 1, :] + dw
        dx_ref[...] = dx

    return pl.pallas_call(
        body, name="dn_prep_bwd", grid=(3, cfg.BL),
        in_specs=[pl.BlockSpec((s, dn), lambda j, b: (b, j)), pl.BlockSpec((1, s, dn), lambda j, b: (j, b, 0)),
                  pl.BlockSpec((4, dn), lambda j, b: (0, j))],
        out_specs=[pl.BlockSpec((s, dn), lambda j, b: (b, j)), pl.BlockSpec((4, dn), lambda j, b: (0, j))],
        out_shape=[jax.ShapeDtypeStruct((cfg.T, 3 * dn), f32), jax.ShapeDtypeStruct((4, 3 * dn), f32)],
        compiler_params=_params(("arbitrary", "arbitrary")),
    )(proj, dqkv, conv_w)


def _b_in(cfg, dqkv_pre, dz, dlx, dlg, dba, dr_mid, r_in, g1, wcat):
    d, dn, lw = cfg.D, cfg.DN, cfg.LW

    def body(dq_ref, dz_ref, dlx_ref, dlg_ref, dba_ref, dr_ref, r_ref, g_ref, w_ref, o_ref, dp_ref, dg_ref):
        _zero_at_first(pl.program_id(0) == 0, dg_ref)
        dp_ref[:, 0:cfg.ZO] = dq_ref[...].astype(bf16)
        dp_ref[:, cfg.ZO:cfg.LXO] = dz_ref[...].astype(bf16)
        dp_ref[:, cfg.LXO:cfg.LGO] = dlx_ref[...].astype(bf16)
        dp_ref[:, cfg.LGO:cfg.BAO] = dlg_ref[...].astype(bf16)
        dp_ref[:, cfg.BAO:] = dba_ref[...].astype(bf16)
        g = g_ref[...]
        _, xh, inv = _rms_fwd(r_ref[...], g)
        dx, dg = _rms_bwd(_mm(dp_ref[...], w_ref[...]), xh, inv, g)
        dg_ref[...] = dg_ref[...] + dg
        o_ref[...] = dr_ref[...] + dx

    return _row_call("in_proj_bwd", body, cfg.T, cfg.TM,
                     [(dqkv_pre, 3 * dn, 0), (dz, dn, 0), (dlx, lw, 0), (dlg, lw, 0), (dba, LANES, 0), (dr_mid, d, 0), (r_in, d, 0)],
                     [g1, wcat], [(d, f32), (cfg.PC, bf16)], [((1, d), f32)])


def _adam_call(name, w, g, m, v, tr):
    rows, cols = w.shape
    bc1 = 1.0 - ADAM_B1 ** ADAM_STEP
    bc2 = 1.0 - ADAM_B2 ** ADAM_STEP

    def body(w_ref, g_ref, m_ref, v_ref, d_ref, nm_ref, nv_ref):
        g = g_ref[...]
        m = ADAM_B1 * m_ref[...] + (1.0 - ADAM_B1) * g
        v = ADAM_B2 * v_ref[...] + (1.0 - ADAM_B2) * (g * g)
        nm_ref[...] = m
        nv_ref[...] = v
        d_ref[...] = -ADAM_LR * ((m / bc1) / (jnp.sqrt(v / bc2) + ADAM_EPS) + ADAM_WD * w_ref[...])

    spec = pl.BlockSpec((tr, cols), lambda i: (i, 0))
    return pl.pallas_call(
        body, name=name, grid=(rows // tr,), in_specs=[spec] * 4, out_specs=[spec] * 3,
        out_shape=[jax.ShapeDtypeStruct((rows, cols), f32)] * 3, compiler_params=_params(("arbitrary",)),
    )(w, g, m, v)


def _sum8_call(name, x, tr):
    _, rows, cols = x.shape

    def body(x_ref, o_ref):
        acc = x_ref[0].astype(f32)
        for j in range(1, N_DEV):
            acc = acc + x_ref[j].astype(f32)
        o_ref[...] = acc

    return pl.pallas_call(
        body, name=name, grid=(rows // tr,), in_specs=[pl.BlockSpec((N_DEV, tr, cols), lambda i: (0, i, 0))],
        out_specs=pl.BlockSpec((tr, cols), lambda i: (i, 0)), out_shape=jax.ShapeDtypeStruct((rows, cols), f32),
        compiler_params=_params(("arbitrary",)),
    )(x)


def _all_gather(name, shards):
    na = len(shards)

    def body(*refs):
        xs, outs = refs[:na], refs[na:2 * na]
        send_sems, recv_sems, local_sems = refs[2 * na:]
        x, y, c = lax.axis_index("x"), lax.axis_index("y"), lax.axis_index("c")
        me, sibling = (x, y, c), (x, y, 1 - c)
        chips = [(1 - x, y), (x, 1 - y), (1 - x, 1 - y)]

        def copy(a, k, block, to, src=None):
            px, py, pc = block
            slot = outs[a].at[4 * px + 2 * py + pc]
            return pltpu.make_async_remote_copy(
                src_ref=slot if src is None else src, dst_ref=slot,
                send_sem=send_sems.at[a, k], recv_sem=recv_sems.at[a, k], device_id=to, device_id_type=MESH)

        mine = [pltpu.make_async_copy(xs[a], outs[a].at[4 * x + 2 * y + c], local_sems.at[a]) for a in range(na)]
        for cp in mine:
            cp.start()
        first = []
        for a in range(na):
            first.append(copy(a, 0, me, sibling, src=xs[a]))
            first += [copy(a, 1 + j, me, (*chip, c), src=xs[a]) for j, chip in enumerate(chips)]
        for cp in first:
            cp.start()
        passed = []
        for j, chip in enumerate(chips):
            for a in range(na):
                copy(a, 1 + j, (*chip, c), me).wait_recv()
                cp = copy(a, 4 + j, (*chip, c), sibling)
                cp.start()
                passed.append(cp)
        for a in range(na):
            copy(a, 0, sibling, me).wait_recv()
            for j, chip in enumerate(chips):
                copy(a, 4 + j, (*chip, 1 - c), me).wait_recv()
        for cp in first + passed:
            cp.wait_send()
        for cp in mine:
            cp.wait()

    hbm = pl.BlockSpec(memory_space=pltpu.HBM)
    return pl.pallas_call(
        body, name=name, out_shape=[jax.ShapeDtypeStruct((N_DEV,) + s.shape, s.dtype) for s in shards],
        in_specs=[hbm] * na, out_specs=[hbm] * na,
        scratch_shapes=[pltpu.SemaphoreType.DMA((na, 7)), pltpu.SemaphoreType.DMA((na, 7)), pltpu.SemaphoreType.DMA((na,))],
    )(*shards)


def _peer_list():
    x, y, c = lax.axis_index("x"), lax.axis_index("y"), lax.axis_index("c")
    return 4 * x + 2 * y + c, [(x ^ (k >> 2), y ^ ((k >> 1) & 1), c ^ (k & 1)) for k in range(1, N_DEV)]


_HBM = pl.BlockSpec(memory_space=pltpu.HBM)
_SEM = pl.BlockSpec(memory_space=pltpu.SEMAPHORE)
_EFFECT = pltpu.SideEffectType.DATAFLOW_SIDE_EFFECTING


def _exchange_copies(xs, lands, send_sem, recv_sem, gather):
    me, peers = _peer_list()
    return [pltpu.make_async_remote_copy(
        src_ref=xs[a] if gather else xs[a].at[4 * px + 2 * py + pc], dst_ref=lands[a].at[me],
        send_sem=send_sem.at[7 * a + k], recv_sem=recv_sem.at[7 * a + k], device_id=(px, py, pc), device_id_type=MESH)
        for k, (px, py, pc) in enumerate(peers) for a in range(len(xs))]


def _exchange_start(name, blocks, gather=False):
    na = len(blocks)

    def body(*refs):
        for cp in _exchange_copies(refs[:na], refs[na:2 * na], refs[2 * na], refs[2 * na + 1], gather):
            cp.start()
        refs[-1][...] = jnp.zeros_like(refs[-1])

    lands = [jax.ShapeDtypeStruct((N_DEV,) + b.shape if gather else b.shape, b.dtype) for b in blocks]
    hbm = [pltpu.HBM(b.shape, b.dtype) for b in blocks] + [pltpu.HBM(b.shape, b.dtype) for b in lands]
    send_sem, recv_sem, *thru, token = pl.pallas_call(
        body, name=name,
        out_shape=(pltpu.SemaphoreType.DMA((7 * na,)), pltpu.SemaphoreType.DMA((7 * na,)), *hbm,
                   jax.ShapeDtypeStruct((8, LANES), f32)),
        in_specs=[_HBM] * (2 * na), out_specs=(_SEM, _SEM, *([_HBM] * (2 * na)), pl.BlockSpec(memory_space=pltpu.VMEM)),
        input_output_aliases={i: 2 + i for i in range(2 * na)},
        compiler_params=pltpu.CompilerParams(has_side_effects=_EFFECT),
    )(*[pltpu.with_memory_space_constraint(b, pltpu.HBM) for b in blocks],
      *[pltpu.with_memory_space_constraint(lax.empty(b.shape, b.dtype), pltpu.HBM) for b in lands])
    return send_sem, recv_sem, thru, token


def _exchange_wait(name, send_sem, recv_sem, thru, after, gather=False):
    na = len(thru) // 2

    def body(*refs):
        for cp in _exchange_copies(refs[:na], refs[na:2 * na], refs[2 * na], refs[2 * na + 1], gather):
            cp.wait_send()
            cp.wait_recv()

    return pl.pallas_call(
        body, name=name, out_shape=tuple(pltpu.HBM(t.shape, t.dtype) for t in thru),
        in_specs=[_HBM] * (2 * na) + [_SEM, _SEM, pl.BlockSpec(memory_space=pl.ANY)], out_specs=tuple([_HBM] * (2 * na)),
        input_output_aliases={i: i for i in range(2 * na)},
        compiler_params=pltpu.CompilerParams(has_side_effects=_EFFECT),
    )(*thru, send_sem, recv_sem, after)[na:]


def _layer_fwd(cfg, w, r, p):
    h1, proj = _k_in(cfg, r, w["norm1_g"], w["wcat_t"])
    qkv = _k_prep(cfg, proj, w["dn_conv_w"])
    o, *gdn_saved = _k_gdn_fwd(cfg, qkv, proj, w["alog_row"], w["dt_row"])
    hs = _k_lru_fwd(cfg, proj, w["lru_conv_w"], w["lru_conv_b"], w["wa"], w["wx"], w["lru_ba"], w["lru_bx"], w["lru_lambda"])
    mix, r_mid = _k_mix(cfg, o, proj, hs, r, w["dn_norm_g"], w["lru_norm_g"], w["w_out"])
    h2, gp, up = _k_ffn_a(cfg, r_mid, w["norm2_g"], w["ffn_wg_t"], w["ffn_wu_t"])
    act = _k_ffn_b(cfg, gp, up, w["ffn_conv_w"], w["ffn_conv_b"])
    r2 = _k_ffn_c(cfg, act, r_mid, w["ffn_wd"])
    pn, r3 = _k_ple(cfg, r2, p, w["ple_norm_g"], w["ple_wg"], w["ple_bg"], w["ple_wp_t"])
    saved = dict(r=r, h1=h1, proj=proj, qkv=qkv, o=o, gdn=gdn_saved, hs=hs, mix=mix, r_mid=r_mid, h2=h2, gp=gp, up=up,
                 r2=r2, pn=pn, p=p)
    return r3, saved


def _layer_bwd(cfg, w, sv, dr3, early=None):
    g = {}
    dt = min(512, cfg.D)
    dr2, dlog, dpp, g["ple_norm_g"], g["ple_bg"] = _b_ple(cfg, dr3, sv["r2"], sv["p"], w["ple_norm_g"], w["ple_wg"], w["ple_bg"], w["ple_wp_t"])
    g["ple_wg"] = _mm_tn_call(cfg, "d_ple_wg", sv["pn"], dlog, dt)
    g["ple_wp_t"] = _mm_tn_call(cfg, "d_ple_wp", dpp, sv["p"], cfg.PD)
    dgp, dup, g["ffn_wd"], g["ffn_conv_w"], g["ffn_conv_b"] = _b_ffn_bc(cfg, dr2, w["ffn_wd"], sv["gp"], sv["up"], w["ffn_conv_w"], w["ffn_conv_b"])
    dr_mid, g["norm2_g"] = _b_ffn_a(cfg, dgp, dup, dr2, sv["r_mid"], w["norm2_g"], w["ffn_wg_t"], w["ffn_wu_t"])
    g["ffn_wg_t"] = _mm_tn_call(cfg, "d_ffn_wg", dgp, sv["h2"], dt)
    g["ffn_wu_t"] = _mm_tn_call(cfg, "d_ffn_wu", dup, sv["h2"], dt)
    if early is not None:
        w = dict(w, dn_norm_g=w["dn_norm_g"] + early(g)[0, 0])
    do, dz, dlg, dhs, g["dn_norm_g"], g["lru_norm_g"] = _b_mix(cfg, dr_mid, sv["o"], sv["proj"], sv["hs"], w["dn_norm_g"], w["lru_norm_g"], w["w_out"])
    g["w_out"] = _mm_tn_call(cfg, "d_w_out", sv["mix"], dr_mid, dt)
    dlx, g["lru_conv_w"], g["lru_conv_b"], g["wa"], g["wx"], g["lru_ba"], g["lru_bx"], g["lru_lambda"] = _b_lru(
        cfg, sv["proj"], dhs, w["lru_conv_w"], w["lru_conv_b"], w["wa"], w["wx"], w["lru_ba"], w["lru_bx"], w["lru_lambda"])
    dqkv, dba, g["alog_row"], g["dt_row"] = _b_gdn(cfg, sv["qkv"], sv["proj"], do, w["alog_row"], w["dt_row"], sv["gdn"])
    dqkv_pre, g["dn_conv_w"] = _b_prep(cfg, sv["proj"], dqkv, w["dn_conv_w"])
    dr, dproj, g["norm1_g"] = _b_in(cfg, dqkv_pre, dz, dlx, dlg, dba, dr_mid, sv["r"], w["norm1_g"], w["wcat_t"])
    g["wcat_t"] = _mm_tn_call(cfg, "d_w_in", dproj, sv["h1"], dt)
    return dr, g


BIG = ("w_in", "w_out", "ffn_wg", "ffn_wu", "ffn_wd", "ple_wg", "ple_wp")
BIG_T = {"w_in": True, "w_out": False, "ffn_wg": True, "ffn_wu": True, "ffn_wd": False, "ple_wg": False, "ple_wp": True}
BIG_OPERAND = {"w_in": "wcat_t", "w_out": "w_out", "ffn_wg": "ffn_wg_t", "ffn_wu": "ffn_wu_t", "ffn_wd": "ffn_wd",
               "ple_wg": "ple_wg", "ple_wp": "ple_wp_t"}
EARLY = ("ffn_wg", "ffn_wu", "ffn_wd", "ple_wg", "ple_wp")
SMALL_SHARDED = ("dn_conv_w", "lru_conv_w", "lru_ba", "lru_bx", "lru_lambda", "ffn_conv_w")
SMALL_REPL = ("norm1_g", "dn_a_log", "dn_dt_bias", "dn_norm_g", "lru_conv_b", "lru_wa", "lru_wx", "lru_norm_g", "norm2_g",
              "ffn_conv_b", "ple_norm_g", "ple_bg", "final_g")
WEIGHTS = ("norm1_g", "w_in", "dn_conv_w", "dn_a_log", "dn_dt_bias", "dn_norm_g", "lru_conv_w", "lru_conv_b", "lru_wa",
           "lru_ba", "lru_wx", "lru_bx", "lru_lambda", "lru_norm_g", "w_out", "norm2_g", "ffn_wg", "ffn_wu", "ffn_conv_w",
           "ffn_conv_b", "ffn_wd", "ple_norm_g", "ple_wg", "ple_bg", "ple_wp", "final_g")


def _pad_rows(flat, cols, mult):
    n = flat.shape[0]
    rows = -(-n // cols)
    rows = -(-rows // mult) * mult
    return jnp.pad(flat, (0, rows * cols - n)).reshape(rows, cols)


def _pack(arrs, cols, mult, dtype):
    return _pad_rows(jnp.concatenate([a.reshape(-1).astype(dtype) for a in arrs]), cols, mult)


def _unpack(flat, shapes):
    out, off = [], 0
    for shp in shapes:
        n = math.prod(shp)
        piece = flat[off:off + n]
        if n < 4096:
            piece = lax.optimization_barrier(piece)
        out.append(piece.reshape(shp))
        off += n
    return out


def _unpack8(g8, shapes, axes):
    out, off = [], 0
    for shp, ax in zip(shapes, axes):
        n = math.prod(shp)
        a = g8[:, off:off + n].reshape((N_DEV,) + tuple(shp))
        a = jnp.moveaxis(a, 0, ax)
        out.append(a.reshape(shp[:ax] + (N_DEV * shp[ax],) + shp[ax + 1:]))
        off += n
    return out


def _wcat_t_from_w_in_t(cfg, wt):
    nba = 4 * cfg.H
    pad = jnp.zeros((LANES - nba, wt.shape[1]), wt.dtype)
    return jnp.concatenate([wt[:cfg.LXO], wt[cfg.LXO + nba:], wt[cfg.LXO:cfg.LXO + nba], pad], axis=0)


def _w_in_t_from_wcat_t(cfg, wc):
    nba = 4 * cfg.H
    return jnp.concatenate([wc[:cfg.LXO], wc[cfg.BAO:cfg.BAO + nba], wc[cfg.LXO:cfg.BAO]], axis=0)


def _gate_row(cfg, a):
    h2 = 2 * cfg.H
    return jnp.concatenate([jnp.zeros((1, h2), f32), a.reshape(1, h2), jnp.zeros((1, LANES - 2 * h2), f32)], axis=1)


def _blockdiag(cfg, w):
    w = w.reshape(2, cfg.NB // 2, 2, 64, 64)
    z = jnp.zeros_like(w[:, :, 0])
    top = jnp.concatenate([w[:, :, 0], z], axis=-1)
    bot = jnp.concatenate([z, w[:, :, 1]], axis=-1)
    return jnp.concatenate([top, bot], axis=-2).astype(bf16)


def _unblockdiag(cfg, g):
    a = g[:, :, :64, :64]
    b = g[:, :, 64:, 64:]
    return jnp.stack([a, b], axis=2).reshape(2, cfg.NB, 64, 64)


def _layer_operands(cfg, big, small, i):
    return dict(
        wcat_t=_wcat_t_from_w_in_t(cfg, big["w_in"]), w_out=big["w_out"], ffn_wg_t=big["ffn_wg"],
        ffn_wu_t=big["ffn_wu"], ffn_wd=big["ffn_wd"], ple_wg=big["ple_wg"], ple_wp_t=big["ple_wp"],
        norm1_g=small["norm1_g"][i][None], dn_conv_w=small["dn_conv_w"][i], alog_row=_gate_row(cfg, small["dn_a_log"][i]),
        dt_row=_gate_row(cfg, small["dn_dt_bias"][i]), dn_norm_g=small["dn_norm_g"][i][None],
        lru_conv_w=small["lru_conv_w"][i], lru_conv_b=small["lru_conv_b"][i][None],
        wa=_blockdiag(cfg, small["lru_wa"][i]), wx=_blockdiag(cfg, small["lru_wx"][i]),
        lru_ba=small["lru_ba"][i], lru_bx=small["lru_bx"][i], lru_lambda=small["lru_lambda"][i],
        lru_norm_g=small["lru_norm_g"][i][None], norm2_g=small["norm2_g"][i][None], ffn_conv_w=small["ffn_conv_w"][i],
        ffn_conv_b=small["ffn_conv_b"][i][None], ple_norm_g=small["ple_norm_g"][i][None], ple_bg=small["ple_bg"][i][None],
    )


def _small_grads_to_problem(cfg, g):
    h = cfg.H
    return dict(
        norm1_g=g["norm1_g"][0], dn_conv_w=g["dn_conv_w"],
        dn_a_log=g["alog_row"][0, 2 * h:4 * h].reshape(2, h), dn_dt_bias=g["dt_row"][0, 2 * h:4 * h].reshape(2, h),
        dn_norm_g=g["dn_norm_g"][0], lru_conv_w=g["lru_conv_w"], lru_conv_b=g["lru_conv_b"][0],
        lru_wa=_unblockdiag(cfg, g["wa"]), lru_wx=_unblockdiag(cfg, g["wx"]), lru_ba=g["lru_ba"], lru_bx=g["lru_bx"],
        lru_lambda=g["lru_lambda"], lru_norm_g=g["lru_norm_g"][0], norm2_g=g["norm2_g"][0], ffn_conv_w=g["ffn_conv_w"],
        ffn_conv_b=g["ffn_conv_b"][0], ple_norm_g=g["ple_norm_g"][0], ple_bg=g["ple_bg"][0],
    )


def _local_step(cfg, get_big, small, x, p, target, on_big_grads):
    r = x.reshape(cfg.T, cfg.D)
    ops, saved = [], []
    for i in range(cfg.L):
        big, token = get_big(i, r)
        w = _layer_operands(cfg, big, small, i)
        if token is not None:
            w["norm1_g"] = w["norm1_g"] + token[0, 0]
        r, sv = _layer_fwd(cfg, w, r, p[i].reshape(cfg.T, cfg.PD))
        ops.append(w)
        saved.append(sv)
    dr, loss, dgf = _k_loss(cfg, r, target.reshape(cfg.T, cfg.D), small["final_g"][None])
    gsmall = [None] * cfg.L
    for i in reversed(range(cfg.L)):
        first = EARLY if i == 0 else ()
        early = (lambda g: on_big_grads("%da" % i, {n: g[BIG_OPERAND[n]] for n in first})) if first else None
        dr, g = _layer_bwd(cfg, ops[i], saved[i], dr, early)
        token = on_big_grads("%d" % i, {n: (_w_in_t_from_wcat_t(cfg, g["wcat_t"]) if n == "w_in" else g[BIG_OPERAND[n]])
                                        for n in BIG if n not in first})
        if i > 0:
            ops[i - 1]["ple_bg"] = ops[i - 1]["ple_bg"] + token[0, 0]
        gsmall[i] = _small_grads_to_problem(cfg, g)
    gs = {k: jnp.stack([gl[k] for gl in gsmall]) for k in gsmall[0]}
    gs["final_g"] = dgf[0]
    return loss, dr, gs


def _row_tile(rows, limit=512):
    best = rows
    for t in range(8, min(rows, limit) + 1, 8):
        if rows % t == 0:
            best = t
    return best if best <= limit or rows <= limit else rows


def _adam_group(name, ws, gs, ms, vs, cols, tr):
    shapes = [w.shape for w in ws]
    pk = lambda arrs: _pack(arrs, cols, tr, f32)
    w2 = pk(ws)
    d, nm, nv = _adam_call(name, w2, pk(gs), pk(ms), pk(vs), min(tr, w2.shape[0]))
    return [_unpack(a.reshape(-1), shapes) for a in (d, nm, nv)]


def kernel(x, p, norm1_g, w_in, dn_conv_w, dn_a_log, dn_dt_bias, dn_norm_g, lru_conv_w, lru_conv_b, lru_wa, lru_ba, lru_wx, lru_bx, lru_lambda, lru_norm_g, w_out, norm2_g, ffn_wg, ffn_wu, ffn_conv_w, ffn_conv_b, ffn_wd, ple_norm_g, ple_wg, ple_bg, ple_wp, final_g, loss_target, m_norm1_g, m_w_in, m_dn_conv_w, m_dn_a_log, m_dn_dt_bias, m_dn_norm_g, m_lru_conv_w, m_lru_conv_b, m_lru_wa, m_lru_ba, m_lru_wx, m_lru_bx, m_lru_lambda, m_lru_norm_g, m_w_out, m_norm2_g, m_ffn_wg, m_ffn_wu, m_ffn_conv_w, m_ffn_conv_b, m_ffn_wd, m_ple_norm_g, m_ple_wg, m_ple_bg, m_ple_wp, m_final_g, v_norm1_g, v_w_in, v_dn_conv_w, v_dn_a_log, v_dn_dt_bias, v_dn_norm_g, v_lru_conv_w, v_lru_conv_b, v_lru_wa, v_lru_ba, v_lru_wx, v_lru_bx, v_lru_lambda, v_lru_norm_g, v_w_out, v_norm2_g, v_ffn_wg, v_ffn_wu, v_ffn_conv_w, v_ffn_conv_b, v_ffn_wd, v_ple_norm_g, v_ple_wg, v_ple_bg, v_ple_wp, v_final_g):
    cfg = CFG
    a = dict(locals())
    wl = {n: a[n] for n in WEIGHTS}
    ml = {n: a["m_" + n] for n in WEIGHTS}
    vl = {n: a["v_" + n] for n in WEIGHTS}
    me = 4 * lax.axis_index("x") + 2 * lax.axis_index("y") + lax.axis_index("c")
    nl = cfg.L

    blocks = [(jnp.swapaxes(wl[n], 1, 2) if BIG_T[n] else wl[n]).astype(bf16) for n in BIG]
    ss_shapes = [wl[n].shape for n in SMALL_SHARDED]
    *first, s8 = _all_gather("gather_weights_0", [blk[0] for blk in blocks] + [_pack([wl[n] for n in SMALL_SHARDED], LANES, 8, f32)])
    small = dict(zip(SMALL_SHARDED, _unpack8(s8.reshape(N_DEV, -1), ss_shapes, [2] * len(ss_shapes))))
    small.update({n: wl[n] for n in SMALL_REPL})
    gathers = {}

    def start_gather(i, behind):
        shards, _ = lax.optimization_barrier(([blk[i] for blk in blocks], behind))
        gathers[i] = _exchange_start("gather_start_%d" % i, shards, gather=True)
        return gathers[i][3]

    def get_big(i, r):
        if i == 0:
            full = first
        else:
            send_sem, recv_sem, thru, _ = gathers[i]
            lands = _exchange_wait("gather_wait_%d" % i, send_sem, recv_sem, thru, r, gather=True)
            full = [lax.dynamic_update_slice_in_dim(land, blk[i][None], me, 0) for land, blk in zip(lands, blocks)]
        token = start_gather(i + 1, full[0]) if i + 1 < nl else None
        return {n: f.reshape(N_DEV * f.shape[1], f.shape[2]) for n, f in zip(BIG, full)}, token

    pending = {}

    def on_big_grads(key, g):
        send = [g[n].reshape((N_DEV,) + blk.shape[1:]).astype(bf16) for n, blk in zip(BIG, blocks) if n in g]
        own = [lax.dynamic_index_in_dim(sd, me, 0, keepdims=True) for sd in send]
        send_sem, recv_sem, thru, token = _exchange_start("exchange_start_" + key, send)
        pending[key] = (send_sem, recv_sem, thru, own, [n for n in BIG if n in g])
        return token

    loss_part, dr, gsmall = _local_step(cfg, get_big, small, x, p, loss_target, on_big_grads)
    grad_x = dr.reshape(x.shape)

    sums = {n: [None] * nl for n in BIG}
    for key in sorted(pending):
        send_sem, recv_sem, thru, own, names = pending[key]
        lands = _exchange_wait("exchange_wait_" + key, send_sem, recv_sem, thru, dr)
        for n, land, o in zip(names, lands, own):
            slots = lax.dynamic_update_slice_in_dim(land, o, me, 0)
            sums[n][int(key[0])] = _sum8_call("sum_%s_%s" % (n, key), slots, slots.shape[1])
    gl = {}
    for n in BIG:
        s = jnp.stack(sums[n])
        gl[n] = jnp.swapaxes(s, 1, 2) if BIG_T[n] else s

    small_names = SMALL_REPL + SMALL_SHARDED
    small_shapes = [gsmall[n].shape for n in small_names]
    sv = _pack([gsmall[n] for n in small_names] + [loss_part[0, 0:1]], LANES, 512, f32)
    small_sum = _sum8_call("sum_small", _all_gather("gather_small_grads", [sv])[0], 512).reshape(-1)
    gl.update(zip(small_names, _unpack(small_sum, small_shapes)))
    loss = small_sum[sum(math.prod(s) for s in small_shapes)]
    for n in SMALL_SHARDED:
        shard = wl[n].shape[2]
        gl[n] = lax.dynamic_slice_in_dim(gl[n], me * shard, shard, axis=2)

    outs = {}
    for n in BIG:
        shp = wl[n].shape
        two = lambda t: t.reshape(-1, shp[-1])
        d, nm, nv = _adam_call("adam_" + n, two(wl[n]), two(gl[n]), two(ml[n]), two(vl[n]), _row_tile(math.prod(shp[:-1])))
        outs[n] = (d.reshape(shp), nm.reshape(shp), nv.reshape(shp))
    d, nm, nv = _adam_group("adam_small", [wl[n] for n in small_names], [gl[n] for n in small_names],
                            [ml[n] for n in small_names], [vl[n] for n in small_names], LANES, 64)
    for j, n in enumerate(small_names):
        outs[n] = (d[j], nm[j], nv[j])
    return (loss, grad_x, *[gl[n] for n in WEIGHTS], *[outs[n][0] for n in WEIGHTS], *[outs[n][1] for n in WEIGHTS],
            *[outs[n][2] for n in WEIGHTS])
```

```python
import functools
import math

import jax
import jax.numpy as jnp
from jax import lax
from jax.experimental import pallas as pl
from jax.experimental.pallas import tpu as pltpu

f32 = jnp.float32
bf16 = jnp.bfloat16
MESH = pl.DeviceIdType.MESH

N_DEV = 8
LANES = 128
EPS = 1e-6
LRU_C = 8.0
ADAM_LR, ADAM_B1, ADAM_B2, ADAM_EPS, ADAM_WD, ADAM_STEP = 0.001, 0.9, 0.999, 1e-08, 0.01, 10
VMEM_LIMIT = 56 * 1024 * 1024


class Cfg:
    def __init__(self, d_model=1024, bl=4, seq=2048, depth=4, heads=4, lru_width=512, d_ff=2816, ple=256,
                 tm=512, tm_ffn=256, ff_tile=256):
        self.D, self.BL, self.S, self.L, self.H = d_model, bl, seq, depth, heads
        self.DH = 128
        self.DN = heads * self.DH
        self.LW = lru_width
        self.NB = lru_width // 64
        self.FF, self.PD = d_ff, ple
        self.C = 64
        self.NC = seq // self.C
        self.T = bl * seq
        self.TM = min(tm, self.T)
        self.TMF = min(tm_ffn, self.T)
        self.FT = ff_tile
        self.ZO = 3 * self.DN
        self.LXO = 4 * self.DN
        self.LGO = self.LXO + self.LW
        self.BAO = self.LGO + self.LW
        self.PC = self.BAO + LANES
        self.IN_COLS = 4 * self.DN + 4 * heads + 2 * self.LW


CFG = Cfg()


def _mm(a, b):
    return jnp.dot(a.astype(bf16), b.astype(bf16), preferred_element_type=f32)


def _mm_nt(a, b):
    return lax.dot_general(a.astype(bf16), b.astype(bf16), (((1,), (1,)), ((), ())), preferred_element_type=f32)


def _mm_tn(a, b):
    return lax.dot_general(a.astype(bf16), b.astype(bf16), (((0,), (0,)), ((), ())), preferred_element_type=f32)


def _split2(a):
    hi = a.astype(bf16)
    return hi, (a - hi.astype(f32)).astype(bf16)


def _hmm3(a, b, dims=(((1,), (0,)), ((), ()))):
    ah, al = _split2(a)
    bh, bl = _split2(b)
    dot = functools.partial(lax.dot_general, dimension_numbers=dims, preferred_element_type=f32)
    return dot(ah, bh) + dot(ah, bl) + dot(al, bh)


def _hmm3_tn(a, b):
    return _hmm3(a, b, (((0,), (0,)), ((), ())))


def _cum_mm(mask, x, dims=(((1,), (0,)), ((), ()))):
    m = mask.astype(bf16)
    x1 = x.astype(bf16)
    r = x - x1.astype(f32)
    x2 = r.astype(bf16)
    x3 = (r - x2.astype(f32)).astype(bf16)
    dot = functools.partial(lax.dot_general, dimension_numbers=dims, preferred_element_type=f32)
    return dot(m, x1) + dot(m, x2) + dot(m, x3)


def _rms_fwd(x, g):
    inv = lax.rsqrt(jnp.mean(x * x, axis=-1, keepdims=True) + EPS)
    xh = x * inv
    return xh * g, xh, inv


def _rms_bwd(dy, xh, inv, g):
    dxh = dy * g
    dx = inv * (dxh - xh * jnp.mean(dxh * xh, axis=-1, keepdims=True))
    dg = jnp.sum(dy * xh, axis=0, keepdims=True)
    return dx, dg


def _sigmoid(x):
    return 1.0 / (1.0 + jnp.exp(-x))


def _softplus(x):
    return jnp.maximum(x, 0.0) + jnp.log(1.0 + jnp.exp(-jnp.abs(x)))


def _silu(x):
    s = _sigmoid(x)
    return x * s, s * (1.0 + x * (1.0 - s))


_GC = math.sqrt(2.0 / math.pi)


def _gelu(x):
    t = jnp.tanh(_GC * (x + 0.044715 * x * x * x))
    y = 0.5 * x * (1.0 + t)
    dy = 0.5 * (1.0 + t) + 0.5 * x * (1.0 - t * t) * _GC * (1.0 + 3.0 * 0.044715 * x * x)
    return y, dy


def _nexpm1(x):
    ser = -x * (1.0 + x * 0.5 * (1.0 + x * (1.0 / 3.0) * (1.0 + x * 0.25 * (1.0 + x * 0.2))))
    return jnp.where(x > -0.1, ser, 1.0 - jnp.exp(x))


def _shift(x, s, fill=0.0):
    if s == 0:
        return x
    n = x.shape[0]
    t = lax.broadcasted_iota(jnp.int32, x.shape, 0)
    r = pltpu.roll(x, (-s) % n, 0)
    return jnp.where((t + s >= 0) & (t + s < n), r, fill)


def _conv_fwd(x, w_ref, left):
    k = w_ref.shape[0]
    out = _shift(x, -left) * w_ref[0:1, :]
    for j in range(1, k):
        out = out + _shift(x, j - left) * w_ref[j:j + 1, :]
    return out


def _conv_bwd(dout, x, w_ref, left):
    k = w_ref.shape[0]
    dx = None
    dws = []
    for j in range(k):
        term = _shift(dout, -(j - left)) * w_ref[j:j + 1, :]
        dx = term if dx is None else dx + term
        dws.append(jnp.sum(dout * _shift(x, j - left), axis=0, keepdims=True))
    return dx, dws


def _scan_refs(scans, n):
    blk = 64
    nb = n // blk
    sub = lax.broadcasted_iota(jnp.int32, (blk, LANES), 0) & 7

    def local(a, b, rev):
        for d in (1, 2, 4):
            ok = (sub < 8 - d) if rev else (sub >= d)
            sh = (blk - d) if rev else d
            b = a * jnp.where(ok, pltpu.roll(b, sh, 0), 0.0) + b
            a = a * jnp.where(ok, pltpu.roll(a, sh, 0), 1.0)
        return a, b

    def body(i, carries):
        new = []
        for (a_ref, b_ref, h_ref, rev), carry in zip(scans, carries):
            j = (nb - 1 - i) if rev else i
            base = pl.multiple_of(j * blk, blk)
            a, b = local(a_ref[pl.ds(base, blk), :], b_ref[pl.ds(base, blk), :], rev)
            order = range(blk // 8 - 1, -1, -1) if rev else range(blk // 8)
            for v in order:
                h = b[8 * v:8 * v + 8, :] + a[8 * v:8 * v + 8, :] * carry
                h_ref[pl.ds(base + 8 * v, 8), :] = h
                carry = h[0:1, :] if rev else h[7:8, :]
            new.append(carry)
        return tuple(new)

    lax.fori_loop(0, nb, body, tuple(jnp.zeros((1, LANES), f32) for _ in scans))


def _tri_masks(c, rev):
    i = lax.broadcasted_iota(jnp.int32, (c, c), 0)
    j = lax.broadcasted_iota(jnp.int32, (c, c), 1)
    incl = (i <= j) if rev else (i >= j)
    strict = (i < j) if rev else (i > j)
    incl_t = (i >= j) if rev else (i <= j)
    return incl, strict, incl_t


def _map(f, *lists):
    return [f(*a) for a in zip(*lists)]


def _tri_inv(mats, tick=lambda: None):
    c = mats[0].shape[0]
    i = lax.broadcasted_iota(jnp.int32, (c, c), 0)
    j = lax.broadcasted_iota(jnp.int32, (c, c), 1)
    eye = jnp.where(i == j, 1.0, 0.0)
    t = [eye - a for a in mats]
    pw = _map(_hmm3, mats, mats)
    for it in range(5):
        tick()
        t = _map(lambda ti, ui: ti + ui, t, _map(_hmm3, t, pw))
        if it < 4:
            pw = _map(_hmm3, pw, pw)
    return t


def _gdn_decay(gs, revs):
    c = gs[0].shape[0]
    masks = [_tri_masks(c, r) for r in revs]
    gb = [jnp.broadcast_to(g, (c, c)) for g in gs]
    mcol = _map(lambda m, x: _cum_mm(m[0], x), masks, gb)
    mrow = _map(lambda m, x: jnp.sum(jnp.where(m[2], x, 0.0), axis=0, keepdims=True), masks, gb)
    dec = _map(lambda m, a, b: jnp.exp(jnp.where(m[0], a - b, -1e30)), masks, mcol, mrow)
    return [m[:, 0:1] for m in mcol], [jnp.sum(g, axis=0, keepdims=True) for g in gs], dec


def _gdn_prep(qs, k, v, g, beta, kk, qk, revs, tick=lambda: None):
    c = k[0].shape[0]
    masks = [_tri_masks(c, r) for r in revs]
    gcum, glast, dec = _gdn_decay(g, revs)
    e = [jnp.exp(x) for x in gcum]
    tick()
    tm = _tri_inv(_map(lambda m, b, x, d: jnp.where(m[1], b * x * d, 0.0), masks, beta, kk, dec), tick)
    tick()
    u = _map(lambda t, vi, b: _hmm3(t, vi * b), tm, v, beta)
    tick()
    w = _map(lambda t, ki, b, ei: _hmm3(t, ki * b * ei), tm, k, beta, e)
    p = _map(lambda m, x, d: jnp.where(m[0], x * d, 0.0), masks, qk, dec)
    return dict(dec=dec, cd=[jnp.exp(x) for x in glast], tm=tm, u=u, w=w, p=p, qd=_map(lambda a, b: a * b, qs, e),
                kd=_map(lambda ki, gl, gc: ki * jnp.exp(gl - gc), k, glast, gcum))


def _lane_pick(x, lane):
    l = lax.broadcasted_iota(jnp.int32, x.shape, 1)
    return jnp.sum(jnp.where(l == lane, x, 0.0), axis=1, keepdims=True)


def _params(sem, vmem=VMEM_LIMIT):
    return pltpu.CompilerParams(dimension_semantics=sem, vmem_limit_bytes=vmem)


def _row_call(name, body, t, tm, row_ins, full_ins, row_outs, acc_outs):
    in_specs = [pl.BlockSpec((tm, w), functools.partial(lambda i, c: (i, c), c=c)) for (_, w, c) in row_ins]
    for a in full_ins:
        in_specs.append(pl.BlockSpec(a.shape, functools.partial(lambda i, n: (0,) * n, n=a.ndim)))
    out_specs = [pl.BlockSpec((tm, w), lambda i: (i, 0)) for (w, _) in row_outs]
    out_shape = [jax.ShapeDtypeStruct((t, w), dt) for (w, dt) in row_outs]
    for shp, dt in acc_outs:
        out_specs.append(pl.BlockSpec(shp, functools.partial(lambda i, n: (0,) * n, n=len(shp))))
        out_shape.append(jax.ShapeDtypeStruct(shp, dt))
    return pl.pallas_call(
        body, name=name, grid=(t // tm,), in_specs=in_specs, out_specs=out_specs, out_shape=out_shape,
        compiler_params=_params(("arbitrary",)),
    )(*[a for (a, _, _) in row_ins], *full_ins)


def _k_in(cfg, r, g1, wcat):
    def body(r_ref, g_ref, w_ref, h_ref, proj_ref):
        y, _, _ = _rms_fwd(r_ref[...], g_ref[...])
        hb = y.astype(bf16)
        h_ref[...] = hb
        proj_ref[...] = _mm_nt(hb, w_ref[...])

    return _row_call("in_proj", body, cfg.T, cfg.TM, [(r, cfg.D, 0)], [g1, wcat],
                     [(cfg.D, bf16), (cfg.PC, f32)], [])


def _k_prep(cfg, proj, conv_w):
    dn, s = cfg.DN, cfg.S

    def body(x_ref, w_ref, o_ref):
        sec = pl.program_id(1)
        c = _conv_fwd(x_ref[...], w_ref, 2)
        y, _ = _silu(c)
        for h in range(cfg.H):
            yh = y[:, h * 128:(h + 1) * 128]
            nh = yh * lax.rsqrt(jnp.sum(yh * yh, axis=1, keepdims=True) + EPS)
            o_ref[:, h * 128:(h + 1) * 128] = jnp.where(sec < 2, nh, yh)

    return pl.pallas_call(
        body, name="dn_prep", grid=(cfg.BL, 3),
        in_specs=[pl.BlockSpec((s, dn), lambda b, j: (b, j)), pl.BlockSpec((4, dn), lambda b, j: (0, j))],
        out_specs=pl.BlockSpec((s, dn), lambda b, j: (b, j)),
        out_shape=jax.ShapeDtypeStruct((cfg.T, 3 * dn), f32),
        compiler_params=_params(("arbitrary", "arbitrary")),
    )(proj, conv_w)


def _gate_cols(ba, alog_row, dt_row, lane_b, lane_a):
    beta = _sigmoid(_lane_pick(ba, lane_b))
    alpha = _lane_pick(ba, lane_a)
    aexp = jnp.exp(_lane_pick(alog_row, lane_a))
    dtb = _lane_pick(dt_row, lane_a)
    xa = alpha + dtb
    g = -aexp * _softplus(xa)
    return beta, g, aexp, xa


def _gdn_specs(cfg):
    s, c, nc = cfg.S, cfg.C, cfg.NC
    tok2 = pl.BlockSpec((2, s, 128), lambda b, h: (0, b, h))
    mat2 = pl.BlockSpec((2, 1, nc, c, c), lambda b, h: (0, b, h, 0, 0))
    cd2 = pl.BlockSpec((2, 1, nc, 8, LANES), lambda b, h: (0, b, h, 0, 0))
    shapes = dict(
        tok32=jax.ShapeDtypeStruct((2, cfg.T, cfg.DN), f32), tok16=jax.ShapeDtypeStruct((2, cfg.T, cfg.DN), bf16),
        mat32=jax.ShapeDtypeStruct((2, cfg.BL, cfg.H * nc, c, c), f32), mat16=jax.ShapeDtypeStruct((2, cfg.BL, cfg.H * nc, c, c), bf16),
        cd=jax.ShapeDtypeStruct((2, cfg.BL, cfg.H * nc, 8, LANES), f32))
    return tok2, mat2, cd2, shapes


def _k_gdn_fwd(cfg, qkv, proj, alog_row, dt_row):
    s, c, nc, hh = cfg.S, cfg.C, cfg.NC, cfg.H
    ba_blk = cfg.BAO // LANES
    per = 4 if nc % 4 == 0 else 1

    def body(q_ref, k_ref, v_ref, ba_ref, al_ref, dt_ref, o_ref, u_o, w_o, qd_o, kd_o, p_o, t_o, dec_o, cd_o):
        h = pl.program_id(1)
        groups = nc // per

        def chunk_of(d, g, j):
            return (nc - 1 - (g * per + j)) if d == 1 else (g * per + j)

        def prep(g, tick=lambda: None):
            chains, qs, k, v, kk, qk, gt, beta = [], [], [], [], [], [], [], []
            for j in range(per):
                for d in range(2):
                    n = chunk_of(d, g, j)
                    rows = pl.ds(pl.multiple_of(n * c, c), c)
                    kj, vj = k_ref[rows, :], v_ref[rows, :]
                    qj = q_ref[rows, :] * (128 ** -0.5)
                    bd, gd, _, _ = _gate_cols(ba_ref[rows, :], al_ref[...], dt_ref[...], d * hh + h, 2 * hh + d * hh + h)
                    chains.append((d, n, rows))
                    for lst, val in ((qs, qj), (k, kj), (v, vj), (kk, _mm_nt(kj, kj)), (qk, _mm_nt(qj, kj)), (gt, gd), (beta, bd)):
                        lst.append(val)
            z = _gdn_prep(qs, k, v, gt, beta, kk, qk, [d == 1 for d, _, _ in chains], tick)
            for x, (d, n, rows) in enumerate(chains):
                u_o[d, rows, :] = z["u"][x]
                w_o[d, rows, :] = z["w"][x].astype(bf16)
                qd_o[d, rows, :] = z["qd"][x].astype(bf16)
                kd_o[d, rows, :] = z["kd"][x].astype(bf16)
                p_o[d, 0, n] = z["p"][x].astype(bf16)
                t_o[d, 0, n] = z["tm"][x]
                dec_o[d, 0, n] = z["dec"][x]
                cd_o[d, 0, n] = jnp.broadcast_to(z["cd"][x], (8, LANES))

        def steps(g, box):
            for j in range(per):
                sts = box[0]
                ns = [chunk_of(d, g, j) for d in range(2)]
                rows = [pl.ds(pl.multiple_of(n * c, c), c) for n in ns]
                ws = [_mm(w_o[d, rows[d], :], sts[d]) for d in range(2)]
                qs_ = [_mm(qd_o[d, rows[d], :], sts[d]) for d in range(2)]
                yield
                vn = [u_o[d, rows[d], :] - ws[d] for d in range(2)]
                box[0] = tuple(sts[d] * cd_o[d, 0, ns[d]][0:1, :] + _mm_tn(kd_o[d, rows[d], :], vn[d]) for d in range(2))
                for d in range(2):
                    o_ref[rows[d], :] = o_ref[rows[d], :] + qs_[d] + _mm(p_o[d, 0, ns[d]], vn[d])
                yield

        o_ref[...] = jnp.zeros_like(o_ref)
        prep(0)

        def fused(g, sts):
            box = [sts]
            chain = steps(g - 1, box)
            prep(g, lambda: next(chain, None))
            for _ in chain:
                pass
            return box[0]

        z0 = jnp.zeros((128, 128), f32)
        box = [lax.fori_loop(1, groups, fused, (z0, z0))]
        for _ in steps(groups - 1, box):
            pass

    blk = lambda off: pl.BlockSpec((s, 128), functools.partial(lambda b, h, off: (b, off + h), off=off))
    row = pl.BlockSpec((1, LANES), lambda b, h: (0, 0))
    tok2, mat2, cd2, shp = _gdn_specs(cfg)
    return pl.pallas_call(
        body, name="gdn_fwd", grid=(cfg.BL, hh),
        in_specs=[blk(0), blk(hh), blk(2 * hh), pl.BlockSpec((s, LANES), lambda b, h: (b, ba_blk)), row, row],
        out_specs=[pl.BlockSpec((s, 128), lambda b, h: (b, h)), tok2, tok2, tok2, tok2, mat2, mat2, mat2, cd2],
        out_shape=[jax.ShapeDtypeStruct((cfg.T, cfg.DN), f32), shp["tok32"], shp["tok16"], shp["tok16"], shp["tok16"],
                   shp["mat16"], shp["mat32"], shp["mat32"], shp["cd"]],
        compiler_params=_params(("arbitrary", "arbitrary")),
    )(qkv, qkv, qkv, proj, alog_row, dt_row)


def _lru_gates(xc, wa, wx, ba, bx, lam):
    ra = _sigmoid(_mm(xc, wa) + ba)
    ig = _sigmoid(_mm(xc, wx) + bx)
    sp = _softplus(-lam)
    la = -LRU_C * ra * sp
    a = jnp.exp(la)
    m = jnp.sqrt(_nexpm1(2.0 * la))
    return dict(ra=ra, ig=ig, sp=sp, a=a, m=m, gx=ig * xc)


def _lru_specs(cfg, outer_b):
    s = cfg.S
    lx_blk = cfg.LXO // LANES
    if outer_b:
        ix = lambda f: (lambda b, ct: f(b, ct))
    else:
        ix = lambda f: (lambda ct, b: f(b, ct))
    return dict(
        lx=pl.BlockSpec((s, LANES), ix(lambda b, ct: (b, lx_blk + ct))),
        tok=pl.BlockSpec((s, LANES), ix(lambda b, ct: (b, ct))),
        cw=pl.BlockSpec((4, LANES), ix(lambda b, ct: (0, ct))),
        row=pl.BlockSpec((1, LANES), ix(lambda b, ct: (0, ct))),
        w=pl.BlockSpec((2, 1, LANES, LANES), ix(lambda b, ct: (0, ct, 0, 0))),
        two=pl.BlockSpec((2, LANES), ix(lambda b, ct: (0, ct))),
    )


def _k_lru_fwd(cfg, proj, conv_w, conv_b, wa, wx, ba, bx, lam):
    def body(lx_ref, cw_ref, cb_ref, wa_ref, wx_ref, ba_ref, bx_ref, lam_ref, o_ref, a_s, b_s, h_s):
        xc = _conv_fwd(lx_ref[...], cw_ref, 2) + cb_ref[...]
        for d in range(2):
            z = _lru_gates(xc, wa_ref[d, 0], wx_ref[d, 0], ba_ref[d:d + 1, :], bx_ref[d:d + 1, :], lam_ref[d:d + 1, :])
            a_s[d] = z["a"]
            b_s[d] = z["m"] * z["gx"]
        _scan_refs([(a_s.at[d], b_s.at[d], h_s.at[d], d == 1) for d in range(2)], cfg.S)
        o_ref[...] = h_s[0] + h_s[1]

    sp = _lru_specs(cfg, True)
    return pl.pallas_call(
        body, name="lru_fwd", grid=(cfg.BL, cfg.LW // LANES),
        in_specs=[sp["lx"], sp["cw"], sp["row"], sp["w"], sp["w"], sp["two"], sp["two"], sp["two"]],
        out_specs=sp["tok"], out_shape=jax.ShapeDtypeStruct((cfg.T, cfg.LW), f32),
        scratch_shapes=[pltpu.VMEM((2, cfg.S, LANES), f32)] * 3,
        compiler_params=_params(("arbitrary", "arbitrary")),
    )(proj, conv_w, conv_b, wa, wx, ba, bx, lam)


def _mix_parts(cfg, o, z, lg, hs, dng, lrg):
    heads = []
    for h in range(cfg.H):
        sl = slice(h * 128, (h + 1) * 128)
        y, xh, inv = _rms_fwd(o[:, sl], dng)
        sz, dsz = _silu(z[:, sl])
        heads.append((y, xh, inv, sz, dsz))
    gl, dgl = _gelu(lg)
    y2, xh2, inv2 = _rms_fwd(gl * hs, lrg)
    return heads, (y2, xh2, inv2, gl, dgl)


def _k_mix(cfg, o, proj, hs, r, dng, lrg, wout):
    dn = cfg.DN

    def body(o_ref, z_ref, lg_ref, hs_ref, r_ref, dng_ref, lrg_ref, w_ref, mix_ref, out_ref):
        heads, lru = _mix_parts(cfg, o_ref[...], z_ref[...], lg_ref[...], hs_ref[...], dng_ref[...], lrg_ref[...])
        for h, (y, _, _, sz, _) in enumerate(heads):
            mix_ref[:, h * 128:(h + 1) * 128] = (y * sz).astype(bf16)
        mix_ref[:, dn:] = lru[0].astype(bf16)
        out_ref[...] = r_ref[...] + jnp.dot(mix_ref[...], w_ref[...], preferred_element_type=f32)

    return _row_call("mix_out", body, cfg.T, cfg.TM,
                     [(o, dn, 0), (proj, dn, cfg.ZO // dn), (proj, cfg.LW, cfg.LGO // cfg.LW), (hs, cfg.LW, 0), (r, cfg.D, 0)],
                     [dng, lrg, wout], [(cfg.D, bf16), (cfg.D, f32)], [])


def _k_ffn_a(cfg, r, g2, wg, wu):
    def body(r_ref, g_ref, wg_ref, wu_ref, h_ref, gp_ref, up_ref):
        y, _, _ = _rms_fwd(r_ref[...], g_ref[...])
        hb = y.astype(bf16)
        h_ref[...] = hb
        gp_ref[...] = _mm_nt(hb, wg_ref[...])
        up_ref[...] = _mm_nt(hb, wu_ref[...])

    return _row_call("ffn_in", body, cfg.T, cfg.TMF, [(r, cfg.D, 0)], [g2, wg, wu],
                     [(cfg.D, bf16), (cfg.FF, f32), (cfg.FF, f32)], [])


def _k_ffn_b(cfg, gp, up, conv_w, conv_b):
    s, ft = cfg.S, cfg.FT

    def body(gp_ref, up_ref, w_ref, b_ref, o_ref):
        gate = _conv_fwd(gp_ref[...], w_ref, 1) + b_ref[...]
        gl, _ = _gelu(gate)
        o_ref[...] = (gl * up_ref[...]).astype(bf16)

    tok = pl.BlockSpec((s, ft), lambda b, j: (b, j))
    return pl.pallas_call(
        body, name="ffn_act", grid=(cfg.BL, cfg.FF // ft),
        in_specs=[tok, tok, pl.BlockSpec((3, ft), lambda b, j: (0, j)), pl.BlockSpec((1, ft), lambda b, j: (0, j))],
        out_specs=tok, out_shape=jax.ShapeDtypeStruct((cfg.T, cfg.FF), bf16),
        compiler_params=_params(("arbitrary", "arbitrary")),
    )(gp, up, conv_w, conv_b)


def _k_ffn_c(cfg, act, r, wd):
    def body(a_ref, r_ref, w_ref, o_ref):
        o_ref[...] = r_ref[...] + jnp.dot(a_ref[...], w_ref[...], preferred_element_type=f32)

    return _row_call("ffn_out", body, cfg.T, cfg.TM, [(act, cfg.FF, 0), (r, cfg.D, 0)], [wd], [(cfg.D, f32)], [])[0]


def _k_ple(cfg, r, p, gp, wpg, bg, wpp):
    def body(r_ref, p_ref, g_ref, wg_ref, bg_ref, wp_ref, pn_ref, o_ref):
        x = r_ref[...]
        y, _, _ = _rms_fwd(x, g_ref[...])
        pn = y.astype(bf16)
        pn_ref[...] = pn
        pg = _sigmoid(jnp.dot(pn, wg_ref[...], preferred_element_type=f32) + bg_ref[...])
        o_ref[...] = x + pg * _mm_nt(p_ref[...], wp_ref[...])

    return _row_call("ple", body, cfg.T, cfg.TM, [(r, cfg.D, 0), (p, cfg.PD, 0)], [gp, wpg, bg, wpp],
                     [(cfg.D, bf16), (cfg.D, f32)], [])


def _k_loss(cfg, r, tgt, gf):
    d = cfg.D

    def body(r_ref, t_ref, g_ref, dr_ref, loss_ref, dg_ref):
        @pl.when(pl.program_id(0) == 0)
        def _():
            loss_ref[...] = jnp.zeros_like(loss_ref)
            dg_ref[...] = jnp.zeros_like(dg_ref)

        g = g_ref[...]
        y, xh, inv = _rms_fwd(r_ref[...], g)
        err = y - t_ref[...]
        loss_ref[...] = loss_ref[...] + (0.5 / d) * jnp.sum(err * err)
        dx, dg = _rms_bwd(err * (1.0 / d), xh, inv, g)
        dr_ref[...] = dx
        dg_ref[...] = dg_ref[...] + dg

    return _row_call("loss_head", body, cfg.T, cfg.TM, [(r, d, 0), (tgt, d, 0)], [gf], [(d, f32)],
                     [((1, LANES), f32), ((1, d), f32)])


def _zero_at_first(cond, *refs):
    @pl.when(cond)
    def _():
        for r in refs:
            r[...] = jnp.zeros_like(r)


def _b_ple(cfg, dr3, r2, p, gp, wpg, bg, wpp):
    d = cfg.D

    def body(dr_ref, r_ref, p_ref, g_ref, wg_ref, bg_ref, wp_ref, dr2_ref, dlog_ref, dpp_ref, dgp_ref, dbg_ref):
        _zero_at_first(pl.program_id(0) == 0, dgp_ref, dbg_ref)
        g = g_ref[...]
        dr = dr_ref[...]
        y, xh, inv = _rms_fwd(r_ref[...], g)
        pg = _sigmoid(jnp.dot(y.astype(bf16), wg_ref[...], preferred_element_type=f32) + bg_ref[...])
        pp = _mm_nt(p_ref[...], wp_ref[...])
        dpp_ref[...] = (dr * pg).astype(bf16)
        dlog = dr * pp * pg * (1.0 - pg)
        dlog_ref[...] = dlog.astype(bf16)
        dbg_ref[...] = dbg_ref[...] + jnp.sum(dlog, axis=0, keepdims=True)
        dx, dg = _rms_bwd(_mm_nt(dlog, wg_ref[...]), xh, inv, g)
        dgp_ref[...] = dgp_ref[...] + dg
        dr2_ref[...] = dr + dx

    return _row_call("ple_bwd", body, cfg.T, cfg.TM, [(dr3, d, 0), (r2, d, 0), (p, cfg.PD, 0)], [gp, wpg, bg, wpp],
                     [(d, f32), (d, bf16), (d, bf16)], [((1, d), f32), ((1, d), f32)])


def _b_ffn_bc(cfg, dr2, wd, gp, up, conv_w, conv_b):
    s, ft, d = cfg.S, cfg.FT, cfg.D

    def body(dr_ref, wd_ref, gp_ref, up_ref, w_ref, b_ref, dgp_ref, dup_ref, dwd_ref, dcw_ref, dcb_ref):
        _zero_at_first(pl.program_id(1) == 0, dwd_ref, dcw_ref, dcb_ref)
        drb = dr_ref[...].astype(bf16)
        dact = _mm_nt(drb, wd_ref[...])
        gpre = gp_ref[...]
        up = up_ref[...]
        gl, dgl = _gelu(_conv_fwd(gpre, w_ref, 1) + b_ref[...])
        dup_ref[...] = (dact * gl).astype(bf16)
        dgate = dact * up * dgl
        dcb_ref[...] = dcb_ref[...] + jnp.sum(dgate, axis=0, keepdims=True)
        dx, dws = _conv_bwd(dgate, gpre, w_ref, 1)
        for j, dw in enumerate(dws):
            dcw_ref[j:j + 1, :] = dcw_ref[j:j + 1, :] + dw
        dgp_ref[...] = dx.astype(bf16)
        dwd_ref[...] = dwd_ref[...] + _mm_tn((gl * up).astype(bf16), drb)

    tok = pl.BlockSpec((s, ft), lambda j, b: (b, j))
    return pl.pallas_call(
        body, name="ffn_act_bwd", grid=(cfg.FF // ft, cfg.BL),
        in_specs=[pl.BlockSpec((s, d), lambda j, b: (b, 0)), pl.BlockSpec((ft, d), lambda j, b: (j, 0)), tok, tok,
                  pl.BlockSpec((3, ft), lambda j, b: (0, j)), pl.BlockSpec((1, ft), lambda j, b: (0, j))],
        out_specs=[tok, tok, pl.BlockSpec((ft, d), lambda j, b: (j, 0)), pl.BlockSpec((3, ft), lambda j, b: (0, j)),
                   pl.BlockSpec((1, ft), lambda j, b: (0, j))],
        out_shape=[jax.ShapeDtypeStruct((cfg.T, cfg.FF), bf16), jax.ShapeDtypeStruct((cfg.T, cfg.FF), bf16),
                   jax.ShapeDtypeStruct((cfg.FF, d), f32), jax.ShapeDtypeStruct((3, cfg.FF), f32),
                   jax.ShapeDtypeStruct((1, cfg.FF), f32)],
        compiler_params=_params(("arbitrary", "arbitrary")),
    )(dr2, wd, gp, up, conv_w, conv_b)


def _b_ffn_a(cfg, dgp, dup, dr2, r_mid, g2, wg, wu):
    d = cfg.D

    def body(dgp_ref, dup_ref, dr_ref, r_ref, g_ref, wg_ref, wu_ref, o_ref, dg_ref):
        _zero_at_first(pl.program_id(0) == 0, dg_ref)
        g = g_ref[...]
        _, xh, inv = _rms_fwd(r_ref[...], g)
        dh = _mm(dgp_ref[...], wg_ref[...]) + _mm(dup_ref[...], wu_ref[...])
        dx, dg = _rms_bwd(dh, xh, inv, g)
        dg_ref[...] = dg_ref[...] + dg
        o_ref[...] = dr_ref[...] + dx

    return _row_call("ffn_in_bwd", body, cfg.T, cfg.TMF, [(dgp, cfg.FF, 0), (dup, cfg.FF, 0), (dr2, d, 0), (r_mid, d, 0)],
                     [g2, wg, wu], [(d, f32)], [((1, d), f32)])


def _mm_tn_call(cfg, name, x, dy, tn):
    t, k = x.shape
    n = dy.shape[1]
    tm = cfg.TM

    def body(x_ref, dy_ref, o_ref):
        _zero_at_first(pl.program_id(1) == 0, o_ref)
        o_ref[...] = o_ref[...] + _mm_tn(x_ref[...], dy_ref[...])

    return pl.pallas_call(
        body, name=name, grid=(n // tn, t // tm),
        in_specs=[pl.BlockSpec((tm, k), lambda j, i: (i, 0)), pl.BlockSpec((tm, tn), lambda j, i: (i, j))],
        out_specs=pl.BlockSpec((k, tn), lambda j, i: (0, j)),
        out_shape=jax.ShapeDtypeStruct((k, n), f32),
        compiler_params=_params(("arbitrary", "arbitrary")),
    )(x, dy)


def _b_mix(cfg, dr, o, proj, hs, dng, lrg, wout):
    dn, lw, d = cfg.DN, cfg.LW, cfg.D

    def body(dr_ref, o_ref, z_ref, lg_ref, hs_ref, dng_ref, lrg_ref, w_ref, do_ref, dz_ref, dlg_ref, dhs_ref, ddn_ref, dlr_ref):
        _zero_at_first(pl.program_id(0) == 0, ddn_ref, dlr_ref)
        dng, lrg = dng_ref[...], lrg_ref[...]
        hs = hs_ref[...]
        o, z = o_ref[...], z_ref[...]
        heads, lru = _mix_parts(cfg, o, z, lg_ref[...], hs, dng, lrg)
        drb = dr_ref[...].astype(bf16)
        dmix_dn = _mm_nt(drb, w_ref[0:dn, :])
        dmix_lr = _mm_nt(drb, w_ref[dn:, :])
        dgn = jnp.zeros_like(dng)
        for h, (y, xh, inv, sz, dsz) in enumerate(heads):
            sl = slice(h * 128, (h + 1) * 128)
            dm = dmix_dn[:, sl]
            dz_ref[:, sl] = dm * y * dsz
            dx, dg = _rms_bwd(dm * sz, xh, inv, dng)
            do_ref[:, sl] = dx
            dgn = dgn + dg
        ddn_ref[...] = ddn_ref[...] + dgn
        _, xh2, inv2, gl, dgl = lru
        dx2, dg2 = _rms_bwd(dmix_lr, xh2, inv2, lrg)
        dlr_ref[...] = dlr_ref[...] + dg2
        dlg_ref[...] = dx2 * hs * dgl
        dhs_ref[...] = dx2 * gl

    return _row_call("mix_bwd", body, cfg.T, cfg.TM,
                     [(dr, d, 0), (o, dn, 0), (proj, dn, cfg.ZO // dn), (proj, lw, cfg.LGO // lw), (hs, lw, 0)],
                     [dng, lrg, wout], [(dn, f32), (dn, f32), (lw, f32), (lw, f32)], [((1, 128), f32), ((1, lw), f32)])


def _b_lru(cfg, proj, dhs, conv_w, conv_b, wa, wx, ba, bx, lam):
    def body(lx_ref, dh_ref, cw_ref, cb_ref, wa_ref, wx_ref, ba_ref, bx_ref, lam_ref,
             dlx_ref, dcw_ref, dcb_ref, dwa_ref, dwx_ref, dba_ref, dbx_ref, dlam_ref, a_s, b_s, h_s, an_s, l_s):
        _zero_at_first(pl.program_id(1) == 0, dcw_ref, dcb_ref, dwa_ref, dwx_ref, dba_ref, dbx_ref, dlam_ref)
        lx = lx_ref[...]
        xc = _conv_fwd(lx, cw_ref, 2) + cb_ref[...]
        dxc = jnp.zeros_like(xc)
        gates = []
        for d in range(2):
            z = _lru_gates(xc, wa_ref[d, 0], wx_ref[d, 0], ba_ref[d:d + 1, :], bx_ref[d:d + 1, :], lam_ref[d:d + 1, :])
            a_s[d] = z["a"]
            b_s[d] = z["m"] * z["gx"]
            an_s[d] = _shift(z["a"], -1 if d == 1 else 1, 0.0)
            gates.append(z)
        _scan_refs([(a_s.at[d], b_s.at[d], h_s.at[d], d == 1) for d in range(2)]
                   + [(an_s.at[d], dh_ref, l_s.at[d], d == 0) for d in range(2)], cfg.S)
        for d in range(2):
            rev = d == 1
            lam = lam_ref[d:d + 1, :]
            z = gates[d]
            a, m, ra, ig, sp = z["a"], z["m"], z["ra"], z["ig"], z["sp"]
            lmb = l_s[d]
            h_prev = _shift(h_s[d], 1 if rev else -1, 0.0)
            da = lmb * h_prev
            dm = lmb * z["gx"]
            dgx = lmb * m
            dla = da * a - dm * (a * a) / jnp.maximum(m, 1e-30)
            dra = dla * (-LRU_C) * sp
            dsp = jnp.sum(dla * (-LRU_C) * ra, axis=0, keepdims=True)
            dlam_ref[d:d + 1, :] = dlam_ref[d:d + 1, :] - dsp * _sigmoid(-lam)
            dpa = dra * ra * (1.0 - ra)
            dpx = dgx * xc * ig * (1.0 - ig)
            dba_ref[d:d + 1, :] = dba_ref[d:d + 1, :] + jnp.sum(dpa, axis=0, keepdims=True)
            dbx_ref[d:d + 1, :] = dbx_ref[d:d + 1, :] + jnp.sum(dpx, axis=0, keepdims=True)
            dwa_ref[d, 0] = dwa_ref[d, 0] + _mm_tn(xc, dpa)
            dwx_ref[d, 0] = dwx_ref[d, 0] + _mm_tn(xc, dpx)
            dxc = dxc + dgx * ig + _mm_nt(dpa, wa_ref[d, 0]) + _mm_nt(dpx, wx_ref[d, 0])
        dcb_ref[...] = dcb_ref[...] + jnp.sum(dxc, axis=0, keepdims=True)
        dx, dws = _conv_bwd(dxc, lx, cw_ref, 2)
        for j, dw in enumerate(dws):
            dcw_ref[j:j + 1, :] = dcw_ref[j:j + 1, :] + dw
        dlx_ref[...] = dx

    sp = _lru_specs(cfg, False)
    nct = cfg.LW // LANES
    return pl.pallas_call(
        body, name="lru_bwd", grid=(nct, cfg.BL),
        in_specs=[sp["lx"], sp["tok"], sp["cw"], sp["row"], sp["w"], sp["w"], sp["two"], sp["two"], sp["two"]],
        out_specs=[sp["tok"], sp["cw"], sp["row"], sp["w"], sp["w"], sp["two"], sp["two"], sp["two"]],
        out_shape=[jax.ShapeDtypeStruct((cfg.T, cfg.LW), f32), jax.ShapeDtypeStruct((4, cfg.LW), f32),
                   jax.ShapeDtypeStruct((1, cfg.LW), f32), jax.ShapeDtypeStruct((2, nct, LANES, LANES), f32),
                   jax.ShapeDtypeStruct((2, nct, LANES, LANES), f32), jax.ShapeDtypeStruct((2, cfg.LW), f32),
                   jax.ShapeDtypeStruct((2, cfg.LW), f32), jax.ShapeDtypeStruct((2, cfg.LW), f32)],
        scratch_shapes=[pltpu.VMEM((2, cfg.S, LANES), f32)] * 5,
        compiler_params=_params(("arbitrary", "arbitrary")),
    )(proj, dhs, conv_w, conv_b, wa, wx, ba, bx, lam)


def _b_gdn(cfg, qkv, proj, do, alog_row, dt_row, saved):
    s, c, nc, hh = cfg.S, cfg.C, cfg.NC, cfg.H
    ba_blk = cfg.BAO // LANES
    scale = 128 ** -0.5
    per = 4
    assert nc % per == 0

    def body(q_ref, k_ref, v_ref, ba_ref, do_ref, al_ref, dt_ref, u_i, w_i, qd_i, kd_i, p_i, t_i, dec_i, cd_i,
             dqkv_ref, dba_ref, dal_ref, ddt_ref, vn_s, dvn_s, st_s, dst_s):
        h = pl.program_id(1)
        _zero_at_first((pl.program_id(0) == 0) & (h == 0), dal_ref, ddt_ref)
        _zero_at_first(h == 0, dba_ref)
        lane = lax.broadcasted_iota(jnp.int32, (c, LANES), 1)
        lane1 = lax.broadcasted_iota(jnp.int32, (1, LANES), 1)
        dirs = (0, 1)
        lane_b = [d * hh + h for d in dirs]
        lane_a = [2 * hh + d * hh + h for d in dirs]

        def seq(i, box):
            sts, dst = box[0]
            live = i < nc
            ii = jnp.where(live, i, 0)
            nf = [ii, nc - 1 - ii]
            nb = [nc - 1 - ii, ii]
            sf = [jnp.where(live, n, nc) for n in nf]
            sb = [jnp.where(live, n, nc) for n in nb]
            rf = [pl.ds(pl.multiple_of(n * c, c), c) for n in nf]
            rb = [pl.ds(pl.multiple_of(n * c, c), c) for n in nb]
            for d in dirs:
                st_s[d, sf[d]] = sts[d]
                dst_s[d, sb[d]] = dst[d]
            dob = [do_ref[rb[d], :] for d in dirs]
            vn = [u_i[d, rf[d], :] - _mm(w_i[d, rf[d], :], sts[d]) for d in dirs]
            dvn = [_mm_tn(p_i[d, 0, nb[d]], dob[d]) + _mm(kd_i[d, rb[d], :], dst[d]) for d in dirs]
            yield
            for d in dirs:
                vn_s[d, pl.ds(pl.multiple_of(sf[d] * c, c), c), :] = vn[d]
                dvn_s[d, pl.ds(pl.multiple_of(sb[d] * c, c), c), :] = dvn[d]
            sts = tuple(sts[d] * cd_i[d, 0, nf[d]][0:1, :] + _mm_tn(kd_i[d, rf[d], :], vn[d]) for d in dirs)
            dst = tuple(_mm_tn(qd_i[d, rb[d], :], dob[d]) + cd_i[d, 0, nb[d]][0:1, 0:1] * dst[d]
                        - _mm_tn(w_i[d, rb[d], :], dvn[d]) for d in dirs)
            box[0] = (sts, dst)
            yield

        def bpar(chunks, tick=lambda: None):
            ch = [(d, n) for n in chunks for d in dirs]
            rows = [pl.ds(pl.multiple_of(n * c, c), c) for _, n in ch]
            masks = [_tri_masks(c, d == 1) for d, _ in ch]
            ld = lambda ref: [ref[d, r, :] for (d, _), r in zip(ch, rows)]
            ldm = lambda ref: [ref[d, 0, n] for d, n in ch]
            q, k, v, dob = ([ref[r, :] for r in rows] for ref in (q_ref, k_ref, v_ref, do_ref))
            gates = [_gate_cols(ba_ref[r, :], al_ref[...], dt_ref[...], lane_b[d], lane_a[d]) for (d, _), r in zip(ch, rows)]
            beta, g, aexp, xa = ([gt[x] for gt in gates] for x in range(4))
            st, dst = ([ref[d, n] for d, n in ch] for ref in (st_s, dst_s))
            vn, dvn, u, w = ld(vn_s), ld(dvn_s), ld(u_i), ld(w_i)
            p = [x.astype(f32) for x in ldm(p_i)]
            tm, dec = ldm(t_i), ldm(dec_i)
            cd = [cd_i[d, 0, n][0:1, 0:1] for d, n in ch]
            dp = _map(lambda m, a, b: jnp.where(m[0], _mm_nt(a, b), 0.0), masks, dob, vn)
            dqd = _map(_mm_nt, dob, st)
            dkd = _map(_mm_nt, vn, dst)
            dw = _map(lambda a, b: -_mm_nt(a, b), dvn, st)
            dcd = _map(lambda a, b: jnp.sum(jnp.sum(a * b, axis=1, keepdims=True), axis=0, keepdims=True), st, dst)
            tick()
            gcum = _map(lambda m, x: _cum_mm(m[0], jnp.broadcast_to(x, (c, c)))[:, 0:1], masks, g)
            glast = [jnp.sum(x, axis=0, keepdims=True) for x in g]
            e = [jnp.exp(x) for x in gcum]
            el = _map(lambda a, b: jnp.exp(a - b), glast, gcum)
            qs = [x * scale for x in q]
            kb = _map(lambda a, b: a * b, k, beta)
            a = _map(lambda m, b, ki, dc: jnp.where(m[1], b * _mm_nt(ki, ki) * dc, 0.0), masks, beta, k, dec)
            tick()
            dvb = _map(_hmm3_tn, tm, dvn)
            dkbe = _map(_hmm3_tn, tm, dw)
            tick()
            da = _map(lambda m, x, ui, y, wi: -jnp.where(m[1], _mm_nt(x, ui) + _mm_nt(y, wi), 0.0), masks, dvb, u, dkbe, w)
            g1 = _map(lambda x, y: x * y, da, dec)
            g2 = _map(lambda x, y: x * y, dp, dec)
            dkb = _map(lambda x, ki, y, ei: _mm(x, ki) + y * ei, g1, k, dkbe, e)
            tick()
            dk = _map(lambda x, kbi, y, qi, z, b, t, l: _mm_tn(x, kbi) + _mm_tn(y, qi) + z * b + t * l,
                      g1, kb, g2, qs, dkb, beta, dkd, el)
            dqs = _map(lambda y, ki, x, ei: _mm(y, ki) + x * ei, g2, k, dqd, e)
            tick()
            ddd = _map(lambda x, ai, y, pi: x * ai + y * pi, da, a, dp, p)
            ones = jnp.ones((c, LANES), f32)
            dgcum = _map(lambda x: jnp.sum(x, axis=1, keepdims=True) - _hmm3_tn(x, ones)[:, 0:1], ddd)
            for x, (d, n) in enumerate(ch):
                dbeta = jnp.sum(dvb[x] * v[x], axis=1, keepdims=True) + jnp.sum(dkb[x] * k[x], axis=1, keepdims=True)
                de = jnp.sum(dkbe[x] * kb[x], axis=1, keepdims=True) + jnp.sum(dqd[x] * qs[x], axis=1, keepdims=True)
                del_ = jnp.sum(dkd[x] * k[x], axis=1, keepdims=True)
                dgc = dgcum[x] + de * e[x] - del_ * el[x]
                dglast = jnp.sum(del_ * el[x], axis=0, keepdims=True) + dcd[x] * cd[x]
                dg = _cum_mm(masks[x][2], jnp.broadcast_to(dgc, (c, LANES)))[:, 0:1] + dglast
                r = rows[x]
                if d == 0:
                    dqkv_ref[0, r, :] = dqs[x] * scale
                    dqkv_ref[1, r, :] = dk[x]
                    dqkv_ref[2, r, :] = dvb[x] * beta[x]
                else:
                    dqkv_ref[0, r, :] = dqkv_ref[0, r, :] + dqs[x] * scale
                    dqkv_ref[1, r, :] = dqkv_ref[1, r, :] + dk[x]
                    dqkv_ref[2, r, :] = dqkv_ref[2, r, :] + dvb[x] * beta[x]
                dlb = dbeta * beta[x] * (1.0 - beta[x])
                dalpha = -dg * aexp[x] * _sigmoid(xa[x])
                dba_ref[r, :] = dba_ref[r, :] + jnp.where(lane == lane_b[d], dlb, 0.0) + jnp.where(lane == lane_a[d], dalpha, 0.0)
                dal_ref[...] = dal_ref[...] + jnp.where(lane1 == lane_a[d], jnp.sum(dg * g[x], axis=0, keepdims=True), 0.0)
                ddt_ref[...] = ddt_ref[...] + jnp.where(lane1 == lane_a[d], jnp.sum(dalpha, axis=0, keepdims=True), 0.0)

        half, groups = nc // 2, nc // per

        def group(k):
            return [half - 2 * k - 2, half - 2 * k - 1, half + 2 * k, half + 2 * k + 1]

        def run(gen):
            for _ in gen:
                pass

        def plain(i, carry):
            box = [carry]
            run(seq(i, box))
            return box[0]

        def woven(k, carry):
            box = [carry]

            def two_steps():
                yield from seq(half + 2 + 2 * k, box)
                yield from seq(half + 3 + 2 * k, box)

            chain = two_steps()
            bpar(group(k), lambda: next(chain, None))
            run(chain)
            return box[0]

        z0 = jnp.zeros((128, 128), f32)
        lax.fori_loop(0, groups, woven, lax.fori_loop(0, half + 2, plain, ((z0, z0), (z0, z0))))

    blk = lambda off: pl.BlockSpec((s, 128), functools.partial(lambda b, h, off: (b, off + h), off=off))
    row = pl.BlockSpec((1, LANES), lambda b, h: (0, 0))
    tok2, mat2, cd2, _ = _gdn_specs(cfg)
    return pl.pallas_call(
        body, name="gdn_bwd", grid=(cfg.BL, hh),
        in_specs=[blk(0), blk(hh), blk(2 * hh), pl.BlockSpec((s, LANES), lambda b, h: (b, ba_blk)),
                  pl.BlockSpec((s, 128), lambda b, h: (b, h)), row, row, tok2, tok2, tok2, tok2, mat2, mat2, mat2, cd2],
        out_specs=[pl.BlockSpec((3, s, 128), lambda b, h: (0, b, h)), pl.BlockSpec((s, LANES), lambda b, h: (b, 0)), row, row],
        out_shape=[jax.ShapeDtypeStruct((3, cfg.T, cfg.DN), f32), jax.ShapeDtypeStruct((cfg.T, LANES), f32),
                   jax.ShapeDtypeStruct((1, LANES), f32), jax.ShapeDtypeStruct((1, LANES), f32)],
        scratch_shapes=[pltpu.VMEM((2, s + c, 128), f32)] * 2 + [pltpu.VMEM((2, nc + 1, 128, 128), f32)] * 2,
        compiler_params=_params(("arbitrary", "arbitrary")),
    )(qkv, qkv, qkv, proj, do, alog_row, dt_row, *saved)


def _b_prep(cfg, proj, dqkv, conv_w):
    dn, s = cfg.DN, cfg.S

    def body(x_ref, dy_ref, w_ref, dx_ref, dw_ref):
        _zero_at_first(pl.program_id(1) == 0, dw_ref)
        sec = pl.program_id(0)
        x = x_ref[...]
        c = _conv_fwd(x, w_ref, 2)
        y, dsilu = _silu(c)
        dy = dy_ref[0]
        parts = []
        for h in range(cfg.H):
            sl = slice(h * 128, (h + 1) * 128)
            yh, dyh = y[:, sl], dy[:, sl]
            inv = lax.rsqrt(jnp.sum(yh * yh, axis=1, keepdims=True) + EPS)
            dn_h = inv * dyh - yh * (inv * inv * inv) * jnp.sum(dyh * yh, axis=1, keepdims=True)
            parts.append(jnp.where(sec < 2, dn_h, dyh))
        ds = jnp.concatenate(parts, axis=1) if len(parts) > 1 else parts[0]
        dx, dws = _conv_bwd(ds * dsilu, x, w_ref, 2)
        for j, dw in enumerate(dws):
            dw_ref[j:j + 1, :] = dw_ref[j:j + 1, :] + dw
        dx_ref[...] = dx

    return pl.pallas_call(
        body, name="dn_prep_bwd", grid=(3, cfg.BL),
        in_specs=[pl.BlockSpec((s, dn), lambda j, b: (b, j)), pl.BlockSpec((1, s, dn), lambda j, b: (j, b, 0)),
                  pl.BlockSpec((4, dn), lambda j, b: (0, j))],
        out_specs=[pl.BlockSpec((s, dn), lambda j, b: (b, j)), pl.BlockSpec((4, dn), lambda j, b: (0, j))],
        out_shape=[jax.ShapeDtypeStruct((cfg.T, 3 * dn), f32), jax.ShapeDtypeStruct((4, 3 * dn), f32)],
        compiler_params=_params(("arbitrary", "arbitrary")),
    )(proj, dqkv, conv_w)


def _b_in(cfg, dqkv_pre, dz, dlx, dlg, dba, dr_mid, r_in, g1, wcat):
    d, dn, lw = cfg.D, cfg.DN, cfg.LW

    def body(dq_ref, dz_ref, dlx_ref, dlg_ref, dba_ref, dr_ref, r_ref, g_ref, w_ref, o_ref, dp_ref, dg_ref):
        _zero_at_first(pl.program_id(0) == 0, dg_ref)
        dp_ref[:, 0:cfg.ZO] = dq_ref[...].astype(bf16)
        dp_ref[:, cfg.ZO:cfg.LXO] = dz_ref[...].astype(bf16)
        dp_ref[:, cfg.LXO:cfg.LGO] = dlx_ref[...].astype(bf16)
        dp_ref[:, cfg.LGO:cfg.BAO] = dlg_ref[...].astype(bf16)
        dp_ref[:, cfg.BAO:] = dba_ref[...].astype(bf16)
        g = g_ref[...]
        _, xh, inv = _rms_fwd(r_ref[...], g)
        dx, dg = _rms_bwd(_mm(dp_ref[...], w_ref[...]), xh, inv, g)
        dg_ref[...] = dg_ref[...] + dg
        o_ref[...] = dr_ref[...] + dx

    return _row_call("in_proj_bwd", body, cfg.T, cfg.TM,
                     [(dqkv_pre, 3 * dn, 0), (dz, dn, 0), (dlx, lw, 0), (dlg, lw, 0), (dba, LANES, 0), (dr_mid, d, 0), (r_in, d, 0)],
                     [g1, wcat], [(d, f32), (cfg.PC, bf16)], [((1, d), f32)])


def _adam_call(name, w, g, m, v, tr):
    rows, cols = w.shape
    bc1 = 1.0 - ADAM_B1 ** ADAM_STEP
    bc2 = 1.0 - ADAM_B2 ** ADAM_STEP

    def body(w_ref, g_ref, m_ref, v_ref, d_ref, nm_ref, nv_ref):
        g = g_ref[...]
        m = ADAM_B1 * m_ref[...] + (1.0 - ADAM_B1) * g
        v = ADAM_B2 * v_ref[...] + (1.0 - ADAM_B2) * (g * g)
        nm_ref[...] = m
        nv_ref[...] = v
        d_ref[...] = -ADAM_LR * ((m / bc1) / (jnp.sqrt(v / bc2) + ADAM_EPS) + ADAM_WD * w_ref[...])

    spec = pl.BlockSpec((tr, cols), lambda i: (i, 0))
    return pl.pallas_call(
        body, name=name, grid=(rows // tr,), in_specs=[spec] * 4, out_specs=[spec] * 3,
        out_shape=[jax.ShapeDtypeStruct((rows, cols), f32)] * 3, compiler_params=_params(("arbitrary",)),
    )(w, g, m, v)


def _sum8_call(name, x, tr):
    _, rows, cols = x.shape

    def body(x_ref, o_ref):
        acc = x_ref[0].astype(f32)
        for j in range(1, N_DEV):
            acc = acc + x_ref[j].astype(f32)
        o_ref[...] = acc

    return pl.pallas_call(
        body, name=name, grid=(rows // tr,), in_specs=[pl.BlockSpec((N_DEV, tr, cols), lambda i: (0, i, 0))],
        out_specs=pl.BlockSpec((tr, cols), lambda i: (i, 0)), out_shape=jax.ShapeDtypeStruct((rows, cols), f32),
        compiler_params=_params(("arbitrary",)),
    )(x)


def _all_gather(name, shards):
    na = len(shards)

    def body(*refs):
        xs, outs = refs[:na], refs[na:2 * na]
        send_sems, recv_sems, local_sems = refs[2 * na:]
        x, y, c = lax.axis_index("x"), lax.axis_index("y"), lax.axis_index("c")
        me, sibling = (x, y, c), (x, y, 1 - c)
        chips = [(1 - x, y), (x, 1 - y), (1 - x, 1 - y)]

        def copy(a, k, block, to, src=None):
            px, py, pc = block
            slot = outs[a].at[4 * px + 2 * py + pc]
            return pltpu.make_async_remote_copy(
                src_ref=slot if src is None else src, dst_ref=slot,
                send_sem=send_sems.at[a, k], recv_sem=recv_sems.at[a, k], device_id=to, device_id_type=MESH)

        mine = [pltpu.make_async_copy(xs[a], outs[a].at[4 * x + 2 * y + c], local_sems.at[a]) for a in range(na)]
        for cp in mine:
            cp.start()
        first = []
        for a in range(na):
            first.append(copy(a, 0, me, sibling, src=xs[a]))
            first += [copy(a, 1 + j, me, (*chip, c), src=xs[a]) for j, chip in enumerate(chips)]
        for cp in first:
            cp.start()
        passed = []
        for j, chip in enumerate(chips):
            for a in range(na):
                copy(a, 1 + j, (*chip, c), me).wait_recv()
                cp = copy(a, 4 + j, (*chip, c), sibling)
                cp.start()
                passed.append(cp)
        for a in range(na):
            copy(a, 0, sibling, me).wait_recv()
            for j, chip in enumerate(chips):
                copy(a, 4 + j, (*chip, 1 - c), me).wait_recv()
        for cp in first + passed:
            cp.wait_send()
        for cp in mine:
            cp.wait()

    hbm = pl.BlockSpec(memory_space=pltpu.HBM)
    return pl.pallas_call(
        body, name=name, out_shape=[jax.ShapeDtypeStruct((N_DEV,) + s.shape, s.dtype) for s in shards],
        in_specs=[hbm] * na, out_specs=[hbm] * na,
        scratch_shapes=[pltpu.SemaphoreType.DMA((na, 7)), pltpu.SemaphoreType.DMA((na, 7)), pltpu.SemaphoreType.DMA((na,))],
    )(*shards)


def _peer_list():
    x, y, c = lax.axis_index("x"), lax.axis_index("y"), lax.axis_index("c")
    return 4 * x + 2 * y + c, [(x ^ (k >> 2), y ^ ((k >> 1) & 1), c ^ (k & 1)) for k in range(1, N_DEV)]


_HBM = pl.BlockSpec(memory_space=pltpu.HBM)
_SEM = pl.BlockSpec(memory_space=pltpu.SEMAPHORE)
_EFFECT = pltpu.SideEffectType.DATAFLOW_SIDE_EFFECTING


def _exchange_copies(xs, lands, send_sem, recv_sem, gather):
    me, peers = _peer_list()
    return [pltpu.make_async_remote_copy(
        src_ref=xs[a] if gather else xs[a].at[4 * px + 2 * py + pc], dst_ref=lands[a].at[me],
        send_sem=send_sem.at[7 * a + k], recv_sem=recv_sem.at[7 * a + k], device_id=(px, py, pc), device_id_type=MESH)
        for k, (px, py, pc) in enumerate(peers) for a in range(len(xs))]


def _exchange_start(name, blocks, gather=False):
    na = len(blocks)

    def body(*refs):
        for cp in _exchange_copies(refs[:na], refs[na:2 * na], refs[2 * na], refs[2 * na + 1], gather):
            cp.start()
        refs[-1][...] = jnp.zeros_like(refs[-1])

    lands = [jax.ShapeDtypeStruct((N_DEV,) + b.shape if gather else b.shape, b.dtype) for b in blocks]
    hbm = [pltpu.HBM(b.shape, b.dtype) for b in blocks] + [pltpu.HBM(b.shape, b.dtype) for b in lands]
    send_sem, recv_sem, *thru, token = pl.pallas_call(
        body, name=name,
        out_shape=(pltpu.SemaphoreType.DMA((7 * na,)), pltpu.SemaphoreType.DMA((7 * na,)), *hbm,
                   jax.ShapeDtypeStruct((8, LANES), f32)),
        in_specs=[_HBM] * (2 * na), out_specs=(_SEM, _SEM, *([_HBM] * (2 * na)), pl.BlockSpec(memory_space=pltpu.VMEM)),
        input_output_aliases={i: 2 + i for i in range(2 * na)},
        compiler_params=pltpu.CompilerParams(has_side_effects=_EFFECT),
    )(*[pltpu.with_memory_space_constraint(b, pltpu.HBM) for b in blocks],
      *[pltpu.with_memory_space_constraint(lax.empty(b.shape, b.dtype), pltpu.HBM) for b in lands])
    return send_sem, recv_sem, thru, token


def _exchange_wait(name, send_sem, recv_sem, thru, after, gather=False):
    na = len(thru) // 2

    def body(*refs):
        for cp in _exchange_copies(refs[:na], refs[na:2 * na], refs[2 * na], refs[2 * na + 1], gather):
            cp.wait_send()
            cp.wait_recv()

    return pl.pallas_call(
        body, name=name, out_shape=tuple(pltpu.HBM(t.shape, t.dtype) for t in thru),
        in_specs=[_HBM] * (2 * na) + [_SEM, _SEM, pl.BlockSpec(memory_space=pl.ANY)], out_specs=tuple([_HBM] * (2 * na)),
        input_output_aliases={i: i for i in range(2 * na)},
        compiler_params=pltpu.CompilerParams(has_side_effects=_EFFECT),
    )(*thru, send_sem, recv_sem, after)[na:]


def _layer_fwd(cfg, w, r, p):
    h1, proj = _k_in(cfg, r, w["norm1_g"], w["wcat_t"])
    qkv = _k_prep(cfg, proj, w["dn_conv_w"])
    o, *gdn_saved = _k_gdn_fwd(cfg, qkv, proj, w["alog_row"], w["dt_row"])
    hs = _k_lru_fwd(cfg, proj, w["lru_conv_w"], w["lru_conv_b"], w["wa"], w["wx"], w["lru_ba"], w["lru_bx"], w["lru_lambda"])
    mix, r_mid = _k_mix(cfg, o, proj, hs, r, w["dn_norm_g"], w["lru_norm_g"], w["w_out"])
    h2, gp, up = _k_ffn_a(cfg, r_mid, w["norm2_g"], w["ffn_wg_t"], w["ffn_wu_t"])
    act = _k_ffn_b(cfg, gp, up, w["ffn_conv_w"], w["ffn_conv_b"])
    r2 = _k_ffn_c(cfg, act, r_mid, w["ffn_wd"])
    pn, r3 = _k_ple(cfg, r2, p, w["ple_norm_g"], w["ple_wg"], w["ple_bg"], w["ple_wp_t"])
    saved = dict(r=r, h1=h1, proj=proj, qkv=qkv, o=o, gdn=gdn_saved, hs=hs, mix=mix, r_mid=r_mid, h2=h2, gp=gp, up=up,
                 r2=r2, pn=pn, p=p)
    return r3, saved


def _layer_bwd(cfg, w, sv, dr3, early=None):
    g = {}
    dt = min(512, cfg.D)
    dr2, dlog, dpp, g["ple_norm_g"], g["ple_bg"] = _b_ple(cfg, dr3, sv["r2"], sv["p"], w["ple_norm_g"], w["ple_wg"], w["ple_bg"], w["ple_wp_t"])
    g["ple_wg"] = _mm_tn_call(cfg, "d_ple_wg", sv["pn"], dlog, dt)
    g["ple_wp_t"] = _mm_tn_call(cfg, "d_ple_wp", dpp, sv["p"], cfg.PD)
    dgp, dup, g["ffn_wd"], g["ffn_conv_w"], g["ffn_conv_b"] = _b_ffn_bc(cfg, dr2, w["ffn_wd"], sv["gp"], sv["up"], w["ffn_conv_w"], w["ffn_conv_b"])
    dr_mid, g["norm2_g"] = _b_ffn_a(cfg, dgp, dup, dr2, sv["r_mid"], w["norm2_g"], w["ffn_wg_t"], w["ffn_wu_t"])
    g["ffn_wg_t"] = _mm_tn_call(cfg, "d_ffn_wg", dgp, sv["h2"], dt)
    g["ffn_wu_t"] = _mm_tn_call(cfg, "d_ffn_wu", dup, sv["h2"], dt)
    if early is not None:
        w = dict(w, dn_norm_g=w["dn_norm_g"] + early(g)[0, 0])
    do, dz, dlg, dhs, g["dn_norm_g"], g["lru_norm_g"] = _b_mix(cfg, dr_mid, sv["o"], sv["proj"], sv["hs"], w["dn_norm_g"], w["lru_norm_g"], w["w_out"])
    g["w_out"] = _mm_tn_call(cfg, "d_w_out", sv["mix"], dr_mid, dt)
    dlx, g["lru_conv_w"], g["lru_conv_b"], g["wa"], g["wx"], g["lru_ba"], g["lru_bx"], g["lru_lambda"] = _b_lru(
        cfg, sv["proj"], dhs, w["lru_conv_w"], w["lru_conv_b"], w["wa"], w["wx"], w["lru_ba"], w["lru_bx"], w["lru_lambda"])
    dqkv, dba, g["alog_row"], g["dt_row"] = _b_gdn(cfg, sv["qkv"], sv["proj"], do, w["alog_row"], w["dt_row"], sv["gdn"])
    dqkv_pre, g["dn_conv_w"] = _b_prep(cfg, sv["proj"], dqkv, w["dn_conv_w"])
    dr, dproj, g["norm1_g"] = _b_in(cfg, dqkv_pre, dz, dlx, dlg, dba, dr_mid, sv["r"], w["norm1_g"], w["wcat_t"])
    g["wcat_t"] = _mm_tn_call(cfg, "d_w_in", dproj, sv["h1"], dt)
    return dr, g


BIG = ("w_in", "w_out", "ffn_wg", "ffn_wu", "ffn_wd", "ple_wg", "ple_wp")
BIG_T = {"w_in": True, "w_out": False, "ffn_wg": True, "ffn_wu": True, "ffn_wd": False, "ple_wg": False, "ple_wp": True}
BIG_OPERAND = {"w_in": "wcat_t", "w_out": "w_out", "ffn_wg": "ffn_wg_t", "ffn_wu": "ffn_wu_t", "ffn_wd": "ffn_wd",
               "ple_wg": "ple_wg", "ple_wp": "ple_wp_t"}
EARLY = ("ffn_wg", "ffn_wu", "ffn_wd", "ple_wg", "ple_wp")
SMALL_SHARDED = ("dn_conv_w", "lru_conv_w", "lru_ba", "lru_bx", "lru_lambda", "ffn_conv_w")
SMALL_REPL = ("norm1_g", "dn_a_log", "dn_dt_bias", "dn_norm_g", "lru_conv_b", "lru_wa", "lru_wx", "lru_norm_g", "norm2_g",
              "ffn_conv_b", "ple_norm_g", "ple_bg", "final_g")
WEIGHTS = ("norm1_g", "w_in", "dn_conv_w", "dn_a_log", "dn_dt_bias", "dn_norm_g", "lru_conv_w", "lru_conv_b", "lru_wa",
           "lru_ba", "lru_wx", "lru_bx", "lru_lambda", "lru_norm_g", "w_out", "norm2_g", "ffn_wg", "ffn_wu", "ffn_conv_w",
           "ffn_conv_b", "ffn_wd", "ple_norm_g", "ple_wg", "ple_bg", "ple_wp", "final_g")


def _pad_rows(flat, cols, mult):
    n = flat.shape[0]
    rows = -(-n // cols)
    rows = -(-rows // mult) * mult
    return jnp.pad(flat, (0, rows * cols - n)).reshape(rows, cols)


def _pack(arrs, cols, mult, dtype):
    return _pad_rows(jnp.concatenate([a.reshape(-1).astype(dtype) for a in arrs]), cols, mult)


def _unpack(flat, shapes):
    out, off = [], 0
    for shp in shapes:
        n = math.prod(shp)
        piece = flat[off:off + n]
        if n < 4096:
            piece = lax.optimization_barrier(piece)
        out.append(piece.reshape(shp))
        off += n
    return out


def _unpack8(g8, shapes, axes):
    out, off = [], 0
    for shp, ax in zip(shapes, axes):
        n = math.prod(shp)
        a = g8[:, off:off + n].reshape((N_DEV,) + tuple(shp))
        a = jnp.moveaxis(a, 0, ax)
        out.append(a.reshape(shp[:ax] + (N_DEV * shp[ax],) + shp[ax + 1:]))
        off += n
    return out


def _wcat_t_from_w_in_t(cfg, wt):
    nba = 4 * cfg.H
    pad = jnp.zeros((LANES - nba, wt.shape[1]), wt.dtype)
    return jnp.concatenate([wt[:cfg.LXO], wt[cfg.LXO + nba:], wt[cfg.LXO:cfg.LXO + nba], pad], axis=0)


def _w_in_t_from_wcat_t(cfg, wc):
    nba = 4 * cfg.H
    return jnp.concatenate([wc[:cfg.LXO], wc[cfg.BAO:cfg.BAO + nba], wc[cfg.LXO:cfg.BAO]], axis=0)


def _gate_row(cfg, a):
    h2 = 2 * cfg.H
    return jnp.concatenate([jnp.zeros((1, h2), f32), a.reshape(1, h2), jnp.zeros((1, LANES - 2 * h2), f32)], axis=1)


def _blockdiag(cfg, w):
    w = w.reshape(2, cfg.NB // 2, 2, 64, 64)
    z = jnp.zeros_like(w[:, :, 0])
    top = jnp.concatenate([w[:, :, 0], z], axis=-1)
    bot = jnp.concatenate([z, w[:, :, 1]], axis=-1)
    return jnp.concatenate([top, bot], axis=-2).astype(bf16)


def _unblockdiag(cfg, g):
    a = g[:, :, :64, :64]
    b = g[:, :, 64:, 64:]
    return jnp.stack([a, b], axis=2).reshape(2, cfg.NB, 64, 64)


def _layer_operands(cfg, big, small, i):
    return dict(
        wcat_t=_wcat_t_from_w_in_t(cfg, big["w_in"]), w_out=big["w_out"], ffn_wg_t=big["ffn_wg"],
        ffn_wu_t=big["ffn_wu"], ffn_wd=big["ffn_wd"], ple_wg=big["ple_wg"], ple_wp_t=big["ple_wp"],
        norm1_g=small["norm1_g"][i][None], dn_conv_w=small["dn_conv_w"][i], alog_row=_gate_row(cfg, small["dn_a_log"][i]),
        dt_row=_gate_row(cfg, small["dn_dt_bias"][i]), dn_norm_g=small["dn_norm_g"][i][None],
        lru_conv_w=small["lru_conv_w"][i], lru_conv_b=small["lru_conv_b"][i][None],
        wa=_blockdiag(cfg, small["lru_wa"][i]), wx=_blockdiag(cfg, small["lru_wx"][i]),
        lru_ba=small["lru_ba"][i], lru_bx=small["lru_bx"][i], lru_lambda=small["lru_lambda"][i],
        lru_norm_g=small["lru_norm_g"][i][None], norm2_g=small["norm2_g"][i][None], ffn_conv_w=small["ffn_conv_w"][i],
        ffn_conv_b=small["ffn_conv_b"][i][None], ple_norm_g=small["ple_norm_g"][i][None], ple_bg=small["ple_bg"][i][None],
    )


def _small_grads_to_problem(cfg, g):
    h = cfg.H
    return dict(
        norm1_g=g["norm1_g"][0], dn_conv_w=g["dn_conv_w"],
        dn_a_log=g["alog_row"][0, 2 * h:4 * h].reshape(2, h), dn_dt_bias=g["dt_row"][0, 2 * h:4 * h].reshape(2, h),
        dn_norm_g=g["dn_norm_g"][0], lru_conv_w=g["lru_conv_w"], lru_conv_b=g["lru_conv_b"][0],
        lru_wa=_unblockdiag(cfg, g["wa"]), lru_wx=_unblockdiag(cfg, g["wx"]), lru_ba=g["lru_ba"], lru_bx=g["lru_bx"],
        lru_lambda=g["lru_lambda"], lru_norm_g=g["lru_norm_g"][0], norm2_g=g["norm2_g"][0], ffn_conv_w=g["ffn_conv_w"],
        ffn_conv_b=g["ffn_conv_b"][0], ple_norm_g=g["ple_norm_g"][0], ple_bg=g["ple_bg"][0],
    )


def _local_step(cfg, get_big, small, x, p, target, on_big_grads):
    r = x.reshape(cfg.T, cfg.D)
    ops, saved = [], []
    for i in range(cfg.L):
        big, token = get_big(i, r)
        w = _layer_operands(cfg, big, small, i)
        if token is not None:
            w["norm1_g"] = w["norm1_g"] + token[0, 0]
        r, sv = _layer_fwd(cfg, w, r, p[i].reshape(cfg.T, cfg.PD))
        ops.append(w)
        saved.append(sv)
    dr, loss, dgf = _k_loss(cfg, r, target.reshape(cfg.T, cfg.D), small["final_g"][None])
    gsmall = [None] * cfg.L
    for i in reversed(range(cfg.L)):
        first = EARLY if i == 0 else ()
        early = (lambda g: on_big_grads("%da" % i, {n: g[BIG_OPERAND[n]] for n in first})) if first else None
        dr, g = _layer_bwd(cfg, ops[i], saved[i], dr, early)
        token = on_big_grads("%d" % i, {n: (_w_in_t_from_wcat_t(cfg, g["wcat_t"]) if n == "w_in" else g[BIG_OPERAND[n]])
                                        for n in BIG if n not in first})
        if i > 0:
            ops[i - 1]["ple_bg"] = ops[i - 1]["ple_bg"] + token[0, 0]
        gsmall[i] = _small_grads_to_problem(cfg, g)
    gs = {k: jnp.stack([gl[k] for gl in gsmall]) for k in gsmall[0]}
    gs["final_g"] = dgf[0]
    return loss, dr, gs


def _row_tile(rows, limit=512):
    best = rows
    for t in range(8, min(rows, limit) + 1, 8):
        if rows % t == 0:
            best = t
    return best if best <= limit or rows <= limit else rows


def _adam_group(name, ws, gs, ms, vs, cols, tr):
    shapes = [w.shape for w in ws]
    pk = lambda arrs: _pack(arrs, cols, tr, f32)
    w2 = pk(ws)
    d, nm, nv = _adam_call(name, w2, pk(gs), pk(ms), pk(vs), min(tr, w2.shape[0]))
    return [_unpack(a.reshape(-1), shapes) for a in (d, nm, nv)]


def kernel(x, p, norm1_g, w_in, dn_conv_w, dn_a_log, dn_dt_bias, dn_norm_g, lru_conv_w, lru_conv_b, lru_wa, lru_ba, lru_wx, lru_bx, lru_lambda, lru_norm_g, w_out, norm2_g, ffn_wg, ffn_wu, ffn_conv_w, ffn_conv_b, ffn_wd, ple_norm_g, ple_wg, ple_bg, ple_wp, final_g, loss_target, m_norm1_g, m_w_in, m_dn_conv_w, m_dn_a_log, m_dn_dt_bias, m_dn_norm_g, m_lru_conv_w, m_lru_conv_b, m_lru_wa, m_lru_ba, m_lru_wx, m_lru_bx, m_lru_lambda, m_lru_norm_g, m_w_out, m_norm2_g, m_ffn_wg, m_ffn_wu, m_ffn_conv_w, m_ffn_conv_b, m_ffn_wd, m_ple_norm_g, m_ple_wg, m_ple_bg, m_ple_wp, m_final_g, v_norm1_g, v_w_in, v_dn_conv_w, v_dn_a_log, v_dn_dt_bias, v_dn_norm_g, v_lru_conv_w, v_lru_conv_b, v_lru_wa, v_lru_ba, v_lru_wx, v_lru_bx, v_lru_lambda, v_lru_norm_g, v_w_out, v_norm2_g, v_ffn_wg, v_ffn_wu, v_ffn_conv_w, v_ffn_conv_b, v_ffn_wd, v_ple_norm_g, v_ple_wg, v_ple_bg, v_ple_wp, v_final_g):
    cfg = CFG
    a = dict(locals())
    wl = {n: a[n] for n in WEIGHTS}
    ml = {n: a["m_" + n] for n in WEIGHTS}
    vl = {n: a["v_" + n] for n in WEIGHTS}
    me = 4 * lax.axis_index("x") + 2 * lax.axis_index("y") + lax.axis_index("c")
    nl = cfg.L

    blocks = [(jnp.swapaxes(wl[n], 1, 2) if BIG_T[n] else wl[n]).astype(bf16) for n in BIG]
    ss_shapes = [wl[n].shape for n in SMALL_SHARDED]
    *first, s8 = _all_gather("gather_weights_0", [blk[0] for blk in blocks] + [_pack([wl[n] for n in SMALL_SHARDED], LANES, 8, f32)])
    small = dict(zip(SMALL_SHARDED, _unpack8(s8.reshape(N_DEV, -1), ss_shapes, [2] * len(ss_shapes))))
    small.update({n: wl[n] for n in SMALL_REPL})
    gathers = {}

    def start_gather(i, behind):
        shards, _ = lax.optimization_barrier(([blk[i] for blk in blocks], behind))
        gathers[i] = _exchange_start("gather_start_%d" % i, shards, gather=True)
        return gathers[i][3]

    def get_big(i, r):
        if i == 0:
            full = first
        else:
            send_sem, recv_sem, thru, _ = gathers[i]
            lands = _exchange_wait("gather_wait_%d" % i, send_sem, recv_sem, thru, r, gather=True)
            full = [lax.dynamic_update_slice_in_dim(land, blk[i][None], me, 0) for land, blk in zip(lands, blocks)]
        token = start_gather(i + 1, full[0]) if i + 1 < nl else None
        return {n: f.reshape(N_DEV * f.shape[1], f.shape[2]) for n, f in zip(BIG, full)}, token

    pending = {}

    def on_big_grads(key, g):
        send = [g[n].reshape((N_DEV,) + blk.shape[1:]).astype(bf16) for n, blk in zip(BIG, blocks) if n in g]
        own = [lax.dynamic_index_in_dim(sd, me, 0, keepdims=True) for sd in send]
        send_sem, recv_sem, thru, token = _exchange_start("exchange_start_" + key, send)
        pending[key] = (send_sem, recv_sem, thru, own, [n for n in BIG if n in g])
        return token

    loss_part, dr, gsmall = _local_step(cfg, get_big, small, x, p, loss_target, on_big_grads)
    grad_x = dr.reshape(x.shape)

    sums = {n: [None] * nl for n in BIG}
    for key in sorted(pending):
        send_sem, recv_sem, thru, own, names = pending[key]
        lands = _exchange_wait("exchange_wait_" + key, send_sem, recv_sem, thru, dr)
        for n, land, o in zip(names, lands, own):
            slots = lax.dynamic_update_slice_in_dim(land, o, me, 0)
            sums[n][int(key[0])] = _sum8_call("sum_%s_%s" % (n, key), slots, slots.shape[1])
    gl = {}
    for n in BIG:
        s = jnp.stack(sums[n])
        gl[n] = jnp.swapaxes(s, 1, 2) if BIG_T[n] else s

    small_names = SMALL_REPL + SMALL_SHARDED
    small_shapes = [gsmall[n].shape for n in small_names]
    sv = _pack([gsmall[n] for n in small_names] + [loss_part[0, 0:1]], LANES, 512, f32)
    small_sum = _sum8_call("sum_small", _all_gather("gather_small_grads", [sv])[0], 512).reshape(-1)
    gl.update(zip(small_names, _unpack(small_sum, small_shapes)))
    loss = small_sum[sum(math.prod(s) for s in small_shapes)]
    for n in SMALL_SHARDED:
        shard = wl[n].shape[2]
        gl[n] = lax.dynamic_slice_in_dim(gl[n], me * shard, shard, axis=2)

    outs = {}
    for n in BIG:
        shp = wl[n].shape
        two = lambda t: t.reshape(-1, shp[-1])
        d, nm, nv = _adam_call("adam_" + n, two(wl[n]), two(gl[n]), two(ml[n]), two(vl[n]), _row_tile(math.prod(shp[:-1])))
        outs[n] = (d.reshape(shp), nm.reshape(shp), nv.reshape(shp))
    d, nm, nv = _adam_group("adam_small", [wl[n] for n in small_names], [gl[n] for n in small_names],
                            [ml[n] for n in small_names], [vl[n] for n in small_names], LANES, 64)
    for j, n in enumerate(small_names):
        outs[n] = (d[j], nm[j], nv[j])
    return (loss, grad_x, *[gl[n] for n in WEIGHTS], *[outs[n][0] for n in WEIGHTS], *[outs[n][1] for n in WEIGHTS],
            *[outs[n][2] for n in WEIGHTS])
```

```python
import functools
import math

import jax
import jax.numpy as jnp
from jax import lax
from jax.experimental import pallas as pl
from jax.experimental.pallas import tpu as pltpu

f32 = jnp.float32
bf16 = jnp.bfloat16
MESH = pl.DeviceIdType.MESH

N_DEV = 8
LANES = 128
EPS = 1e-6
LRU_C = 8.0
ADAM_LR, ADAM_B1, ADAM_B2, ADAM_EPS, ADAM_WD, ADAM_STEP = 0.001, 0.9, 0.999, 1e-08, 0.01, 10
VMEM_LIMIT = 56 * 1024 * 1024


class Cfg:
    def __init__(self, d_model=1024, bl=4, seq=2048, depth=4, heads=4, lru_width=512, d_ff=2816, ple=256,
                 tm=512, tm_ffn=256, ff_tile=256):
        self.D, self.BL, self.S, self.L, self.H = d_model, bl, seq, depth, heads
        self.DH = 128
        self.DN = heads * self.DH
        self.LW = lru_width
        self.NB = lru_width // 64
        self.FF, self.PD = d_ff, ple
        self.C = 64
        self.NC = seq // self.C
        self.T = bl * seq
        self.TM = min(tm, self.T)
        self.TMF = min(tm_ffn, self.T)
        self.FT = ff_tile
        self.ZO = 3 * self.DN
        self.LXO = 4 * self.DN
        self.LGO = self.LXO + self.LW
        self.BAO = self.LGO + self.LW
        self.PC = self.BAO + LANES
        self.IN_COLS = 4 * self.DN + 4 * heads + 2 * self.LW


CFG = Cfg()


def _mm(a, b):
    return jnp.dot(a.astype(bf16), b.astype(bf16), preferred_element_type=f32)


def _mm_nt(a, b):
    return lax.dot_general(a.astype(bf16), b.astype(bf16), (((1,), (1,)), ((), ())), preferred_element_type=f32)


def _mm_tn(a, b):
    return lax.dot_general(a.astype(bf16), b.astype(bf16), (((0,), (0,)), ((), ())), preferred_element_type=f32)


def _split2(a):
    hi = a.astype(bf16)
    return hi, (a - hi.astype(f32)).astype(bf16)


def _hmm3(a, b, dims=(((1,), (0,)), ((), ()))):
    ah, al = _split2(a)
    bh, bl = _split2(b)
    dot = functools.partial(lax.dot_general, dimension_numbers=dims, preferred_element_type=f32)
    return dot(ah, bh) + dot(ah, bl) + dot(al, bh)


def _hmm3_tn(a, b):
    return _hmm3(a, b, (((0,), (0,)), ((), ())))


def _cum_mm(mask, x, dims=(((1,), (0,)), ((), ()))):
    m = mask.astype(bf16)
    x1 = x.astype(bf16)
    r = x - x1.astype(f32)
    x2 = r.astype(bf16)
    x3 = (r - x2.astype(f32)).astype(bf16)
    dot = functools.partial(lax.dot_general, dimension_numbers=dims, preferred_element_type=f32)
    return dot(m, x1) + dot(m, x2) + dot(m, x3)


def _rms_fwd(x, g):
    inv = lax.rsqrt(jnp.mean(x * x, axis=-1, keepdims=True) + EPS)
    xh = x * inv
    return xh * g, xh, inv


def _rms_bwd(dy, xh, inv, g):
    dxh = dy * g
    dx = inv * (dxh - xh * jnp.mean(dxh * xh, axis=-1, keepdims=True))
    dg = jnp.sum(dy * xh, axis=0, keepdims=True)
    return dx, dg


def _sigmoid(x):
    return 1.0 / (1.0 + jnp.exp(-x))


def _softplus(x):
    return jnp.maximum(x, 0.0) + jnp.log(1.0 + jnp.exp(-jnp.abs(x)))


def _silu(x):
    s = _sigmoid(x)
    return x * s, s * (1.0 + x * (1.0 - s))


_GC = math.sqrt(2.0 / math.pi)


def _gelu(x):
    t = jnp.tanh(_GC * (x + 0.044715 * x * x * x))
    y = 0.5 * x * (1.0 + t)
    dy = 0.5 * (1.0 + t) + 0.5 * x * (1.0 - t * t) * _GC * (1.0 + 3.0 * 0.044715 * x * x)
    return y, dy


def _nexpm1(x):
    ser = -x * (1.0 + x * 0.5 * (1.0 + x * (1.0 / 3.0) * (1.0 + x * 0.25 * (1.0 + x * 0.2))))
    return jnp.where(x > -0.1, ser, 1.0 - jnp.exp(x))


def _shift(x, s, fill=0.0):
    if s == 0:
        return x
    n = x.shape[0]
    t = lax.broadcasted_iota(jnp.int32, x.shape, 0)
    r = pltpu.roll(x, (-s) % n, 0)
    return jnp.where((t + s >= 0) & (t + s < n), r, fill)


def _conv_fwd(x, w_ref, left):
    k = w_ref.shape[0]
    out = _shift(x, -left) * w_ref[0:1, :]
    for j in range(1, k):
        out = out + _shift(x, j - left) * w_ref[j:j + 1, :]
    return out


def _conv_bwd(dout, x, w_ref, left):
    k = w_ref.shape[0]
    dx = None
    dws = []
    for j in range(k):
        term = _shift(dout, -(j - left)) * w_ref[j:j + 1, :]
        dx = term if dx is None else dx + term
        dws.append(jnp.sum(dout * _shift(x, j - left), axis=0, keepdims=True))
    return dx, dws


def _scan_refs(scans, n):
    blk = 64
    nb = n // blk
    sub = lax.broadcasted_iota(jnp.int32, (blk, LANES), 0) & 7

    def local(a, b, rev):
        for d in (1, 2, 4):
            ok = (sub < 8 - d) if rev else (sub >= d)
            sh = (blk - d) if rev else d
            b = a * jnp.where(ok, pltpu.roll(b, sh, 0), 0.0) + b
            a = a * jnp.where(ok, pltpu.roll(a, sh, 0), 1.0)
        return a, b

    def body(i, carries):
        new = []
        for (a_ref, b_ref, h_ref, rev), carry in zip(scans, carries):
            j = (nb - 1 - i) if rev else i
            base = pl.multiple_of(j * blk, blk)
            a, b = local(a_ref[pl.ds(base, blk), :], b_ref[pl.ds(base, blk), :], rev)
            order = range(blk // 8 - 1, -1, -1) if rev else range(blk // 8)
            for v in order:
                h = b[8 * v:8 * v + 8, :] + a[8 * v:8 * v + 8, :] * carry
                h_ref[pl.ds(base + 8 * v, 8), :] = h
                carry = h[0:1, :] if rev else h[7:8, :]
            new.append(carry)
        return tuple(new)

    lax.fori_loop(0, nb, body, tuple(jnp.zeros((1, LANES), f32) for _ in scans))


def _tri_masks(c, rev):
    i = lax.broadcasted_iota(jnp.int32, (c, c), 0)
    j = lax.broadcasted_iota(jnp.int32, (c, c), 1)
    incl = (i <= j) if rev else (i >= j)
    strict = (i < j) if rev else (i > j)
    incl_t = (i >= j) if rev else (i <= j)
    return incl, strict, incl_t


def _map(f, *lists):
    return [f(*a) for a in zip(*lists)]


def _tri_inv(mats, tick=lambda: None):
    c = mats[0].shape[0]
    i = lax.broadcasted_iota(jnp.int32, (c, c), 0)
    j = lax.broadcasted_iota(jnp.int32, (c, c), 1)
    eye = jnp.where(i == j, 1.0, 0.0)
    t = [eye - a for a in mats]
    pw = _map(_hmm3, mats, mats)
    for it in range(5):
        tick()
        t = _map(lambda ti, ui: ti + ui, t, _map(_hmm3, t, pw))
        if it < 4:
            pw = _map(_hmm3, pw, pw)
    return t


def _gdn_decay(gs, revs):
    c = gs[0].shape[0]
    masks = [_tri_masks(c, r) for r in revs]
    gb = [jnp.broadcast_to(g, (c, c)) for g in gs]
    mcol = _map(lambda m, x: _cum_mm(m[0], x), masks, gb)
    mrow = _map(lambda m, x: jnp.sum(jnp.where(m[2], x, 0.0), axis=0, keepdims=True), masks, gb)
    dec = _map(lambda m, a, b: jnp.exp(jnp.where(m[0], a - b, -1e30)), masks, mcol, mrow)
    return [m[:, 0:1] for m in mcol], [jnp.sum(g, axis=0, keepdims=True) for g in gs], dec


def _gdn_prep(qs, k, v, g, beta, kk, qk, revs, tick=lambda: None):
    c = k[0].shape[0]
    masks = [_tri_masks(c, r) for r in revs]
    gcum, glast, dec = _gdn_decay(g, revs)
    e = [jnp.exp(x) for x in gcum]
    tick()
    tm = _tri_inv(_map(lambda m, b, x, d: jnp.where(m[1], b * x * d, 0.0), masks, beta, kk, dec), tick)
    tick()
    u = _map(lambda t, vi, b: _hmm3(t, vi * b), tm, v, beta)
    tick()
    w = _map(lambda t, ki, b, ei: _hmm3(t, ki * b * ei), tm, k, beta, e)
    p = _map(lambda m, x, d: jnp.where(m[0], x * d, 0.0), masks, qk, dec)
    return dict(dec=dec, cd=[jnp.exp(x) for x in glast], tm=tm, u=u, w=w, p=p, qd=_map(lambda a, b: a * b, qs, e),
                kd=_map(lambda ki, gl, gc: ki * jnp.exp(gl - gc), k, glast, gcum))


def _lane_pick(x, lane):
    l = lax.broadcasted_iota(jnp.int32, x.shape, 1)
    return jnp.sum(jnp.where(l == lane, x, 0.0), axis=1, keepdims=True)


def _params(sem, vmem=VMEM_LIMIT):
    return pltpu.CompilerParams(dimension_semantics=sem, vmem_limit_bytes=vmem)


def _row_call(name, body, t, tm, row_ins, full_ins, row_outs, acc_outs):
    in_specs = [pl.BlockSpec((tm, w), functools.partial(lambda i, c: (i, c), c=c)) for (_, w, c) in row_ins]
    for a in full_ins:
        in_specs.append(pl.BlockSpec(a.shape, functools.partial(lambda i, n: (0,) * n, n=a.ndim)))
    out_specs = [pl.BlockSpec((tm, w), lambda i: (i, 0)) for (w, _) in row_outs]
    out_shape = [jax.ShapeDtypeStruct((t, w), dt) for (w, dt) in row_outs]
    for shp, dt in acc_outs:
        out_specs.append(pl.BlockSpec(shp, functools.partial(lambda i, n: (0,) * n, n=len(shp))))
        out_shape.append(jax.ShapeDtypeStruct(shp, dt))
    return pl.pallas_call(
        body, name=name, grid=(t // tm,), in_specs=in_specs, out_specs=out_specs, out_shape=out_shape,
        compiler_params=_params(("arbitrary",)),
    )(*[a for (a, _, _) in row_ins], *full_ins)


def _k_in(cfg, r, g1, wcat):
    def body(r_ref, g_ref, w_ref, h_ref, proj_ref):
        y, _, _ = _rms_fwd(r_ref[...], g_ref[...])
        hb = y.astype(bf16)
        h_ref[...] = hb
        proj_ref[...] = _mm_nt(hb, w_ref[...])

    return _row_call("in_proj", body, cfg.T, cfg.TM, [(r, cfg.D, 0)], [g1, wcat],
                     [(cfg.D, bf16), (cfg.PC, f32)], [])


def _k_prep(cfg, proj, conv_w):
    dn, s = cfg.DN, cfg.S

    def body(x_ref, w_ref, o_ref):
        sec = pl.program_id(1)
        c = _conv_fwd(x_ref[...], w_ref, 2)
        y, _ = _silu(c)
        for h in range(cfg.H):
            yh = y[:, h * 128:(h + 1) * 128]
            nh = yh * lax.rsqrt(jnp.sum(yh * yh, axis=1, keepdims=True) + EPS)
            o_ref[:, h * 128:(h + 1) * 128] = jnp.where(sec < 2, nh, yh)

    return pl.pallas_call(
        body, name="dn_prep", grid=(cfg.BL, 3),
        in_specs=[pl.BlockSpec((s, dn), lambda b, j: (b, j)), pl.BlockSpec((4, dn), lambda b, j: (0, j))],
        out_specs=pl.BlockSpec((s, dn), lambda b, j: (b, j)),
        out_shape=jax.ShapeDtypeStruct((cfg.T, 3 * dn), f32),
        compiler_params=_params(("arbitrary", "arbitrary")),
    )(proj, conv_w)


def _gate_cols(ba, alog_row, dt_row, lane_b, lane_a):
    beta = _sigmoid(_lane_pick(ba, lane_b))
    alpha = _lane_pick(ba, lane_a)
    aexp = jnp.exp(_lane_pick(alog_row, lane_a))
    dtb = _lane_pick(dt_row, lane_a)
    xa = alpha + dtb
    g = -aexp * _softplus(xa)
    return beta, g, aexp, xa


def _gdn_specs(cfg):
    s, c, nc = cfg.S, cfg.C, cfg.NC
    tok2 = pl.BlockSpec((2, s, 128), lambda b, h: (0, b, h))
    mat2 = pl.BlockSpec((2, 1, nc, c, c), lambda b, h: (0, b, h, 0, 0))
    cd2 = pl.BlockSpec((2, 1, nc, 8, LANES), lambda b, h: (0, b, h, 0, 0))
    shapes = dict(
        tok32=jax.ShapeDtypeStruct((2, cfg.T, cfg.DN), f32), tok16=jax.ShapeDtypeStruct((2, cfg.T, cfg.DN), bf16),
        mat32=jax.ShapeDtypeStruct((2, cfg.BL, cfg.H * nc, c, c), f32), mat16=jax.ShapeDtypeStruct((2, cfg.BL, cfg.H * nc, c, c), bf16),
        cd=jax.ShapeDtypeStruct((2, cfg.BL, cfg.H * nc, 8, LANES), f32))
    return tok2, mat2, cd2, shapes


def _k_gdn_fwd(cfg, qkv, proj, alog_row, dt_row):
    s, c, nc, hh = cfg.S, cfg.C, cfg.NC, cfg.H
    ba_blk = cfg.BAO // LANES
    per = 4 if nc % 4 == 0 else 1

    def body(q_ref, k_ref, v_ref, ba_ref, al_ref, dt_ref, o_ref, u_o, w_o, qd_o, kd_o, p_o, t_o, dec_o, cd_o):
        h = pl.program_id(1)
        groups = nc // per

        def chunk_of(d, g, j):
            return (nc - 1 - (g * per + j)) if d == 1 else (g * per + j)

        def prep(g, tick=lambda: None):
            chains, qs, k, v, kk, qk, gt, beta = [], [], [], [], [], [], [], []
            for j in range(per):
                for d in range(2):
                    n = chunk_of(d, g, j)
                    rows = pl.ds(pl.multiple_of(n * c, c), c)
                    kj, vj = k_ref[rows, :], v_ref[rows, :]
                    qj = q_ref[rows, :] * (128 ** -0.5)
                    bd, gd, _, _ = _gate_cols(ba_ref[rows, :], al_ref[...], dt_ref[...], d * hh + h, 2 * hh + d * hh + h)
                    chains.append((d, n, rows))
                    for lst, val in ((qs, qj), (k, kj), (v, vj), (kk, _mm_nt(kj, kj)), (qk, _mm_nt(qj, kj)), (gt, gd), (beta, bd)):
                        lst.append(val)
            z = _gdn_prep(qs, k, v, gt, beta, kk, qk, [d == 1 for d, _, _ in chains], tick)
            for x, (d, n, rows) in enumerate(chains):
                u_o[d, rows, :] = z["u"][x]
                w_o[d, rows, :] = z["w"][x].astype(bf16)
                qd_o[d, rows, :] = z["qd"][x].astype(bf16)
                kd_o[d, rows, :] = z["kd"][x].astype(bf16)
                p_o[d, 0, n] = z["p"][x].astype(bf16)
                t_o[d, 0, n] = z["tm"][x]
                dec_o[d, 0, n] = z["dec"][x]
                cd_o[d, 0, n] = jnp.broadcast_to(z["cd"][x], (8, LANES))

        def steps(g, box):
            for j in range(per):
                sts = box[0]
                ns = [chunk_of(d, g, j) for d in range(2)]
                rows = [pl.ds(pl.multiple_of(n * c, c), c) for n in ns]
                ws = [_mm(w_o[d, rows[d], :], sts[d]) for d in range(2)]
                qs_ = [_mm(qd_o[d, rows[d], :], sts[d]) for d in range(2)]
                yield
                vn = [u_o[d, rows[d], :] - ws[d] for d in range(2)]
                box[0] = tuple(sts[d] * cd_o[d, 0, ns[d]][0:1, :] + _mm_tn(kd_o[d, rows[d], :], vn[d]) for d in range(2))
                for d in range(2):
                    o_ref[rows[d], :] = o_ref[rows[d], :] + qs_[d] + _mm(p_o[d, 0, ns[d]], vn[d])
                yield

        o_ref[...] = jnp.zeros_like(o_ref)
        prep(0)

        def fused(g, sts):
            box = [sts]
            chain = steps(g - 1, box)
            prep(g, lambda: next(chain, None))
            for _ in chain:
                pass
            return box[0]

        z0 = jnp.zeros((128, 128), f32)
        box = [lax.fori_loop(1, groups, fused, (z0, z0))]
        for _ in steps(groups - 1, box):
            pass

    blk = lambda off: pl.BlockSpec((s, 128), functools.partial(lambda b, h, off: (b, off + h), off=off))
    row = pl.BlockSpec((1, LANES), lambda b, h: (0, 0))
    tok2, mat2, cd2, shp = _gdn_specs(cfg)
    return pl.pallas_call(
        body, name="gdn_fwd", grid=(cfg.BL, hh),
        in_specs=[blk(0), blk(hh), blk(2 * hh), pl.BlockSpec((s, LANES), lambda b, h: (b, ba_blk)), row, row],
        out_specs=[pl.BlockSpec((s, 128), lambda b, h: (b, h)), tok2, tok2, tok2, tok2, mat2, mat2, mat2, cd2],
        out_shape=[jax.ShapeDtypeStruct((cfg.T, cfg.DN), f32), shp["tok32"], shp["tok16"], shp["tok16"], shp["tok16"],
                   shp["mat16"], shp["mat32"], shp["mat32"], shp["cd"]],
        compiler_params=_params(("arbitrary", "arbitrary")),
    )(qkv, qkv, qkv, proj, alog_row, dt_row)


def _lru_gates(xc, wa, wx, ba, bx, lam):
    ra = _sigmoid(_mm(xc, wa) + ba)
    ig = _sigmoid(_mm(xc, wx) + bx)
    sp = _softplus(-lam)
    la = -LRU_C * ra * sp
    a = jnp.exp(la)
    m = jnp.sqrt(_nexpm1(2.0 * la))
    return dict(ra=ra, ig=ig, sp=sp, a=a, m=m, gx=ig * xc)


def _lru_specs(cfg, outer_b):
    s = cfg.S
    lx_blk = cfg.LXO // LANES
    if outer_b:
        ix = lambda f: (lambda b, ct: f(b, ct))
    else:
        ix = lambda f: (lambda ct, b: f(b, ct))
    return dict(
        lx=pl.BlockSpec((s, LANES), ix(lambda b, ct: (b, lx_blk + ct))),
        tok=pl.BlockSpec((s, LANES), ix(lambda b, ct: (b, ct))),
        cw=pl.BlockSpec((4, LANES), ix(lambda b, ct: (0, ct))),
        row=pl.BlockSpec((1, LANES), ix(lambda b, ct: (0, ct))),
        w=pl.BlockSpec((2, 1, LANES, LANES), ix(lambda b, ct: (0, ct, 0, 0))),
        two=pl.BlockSpec((2, LANES), ix(lambda b, ct: (0, ct))),
    )


def _k_lru_fwd(cfg, proj, conv_w, conv_b, wa, wx, ba, bx, lam):
    def body(lx_ref, cw_ref, cb_ref, wa_ref, wx_ref, ba_ref, bx_ref, lam_ref, o_ref, a_s, b_s, h_s):
        xc = _conv_fwd(lx_ref[...], cw_ref, 2) + cb_ref[...]
        for d in range(2):
            z = _lru_gates(xc, wa_ref[d, 0], wx_ref[d, 0], ba_ref[d:d + 1, :], bx_ref[d:d + 1, :], lam_ref[d:d + 1, :])
            a_s[d] = z["a"]
            b_s[d] = z["m"] * z["gx"]
        _scan_refs([(a_s.at[d], b_s.at[d], h_s.at[d], d == 1) for d in range(2)], cfg.S)
        o_ref[...] = h_s[0] + h_s[1]

    sp = _lru_specs(cfg, True)
    return pl.pallas_call(
        body, name="lru_fwd", grid=(cfg.BL, cfg.LW // LANES),
        in_specs=[sp["lx"], sp["cw"], sp["row"], sp["w"], sp["w"], sp["two"], sp["two"], sp["two"]],
        out_specs=sp["tok"], out_shape=jax.ShapeDtypeStruct((cfg.T, cfg.LW), f32),
        scratch_shapes=[pltpu.VMEM((2, cfg.S, LANES), f32)] * 3,
        compiler_params=_params(("arbitrary", "arbitrary")),
    )(proj, conv_w, conv_b, wa, wx, ba, bx, lam)


def _mix_parts(cfg, o, z, lg, hs, dng, lrg):
    heads = []
    for h in range(cfg.H):
        sl = slice(h * 128, (h + 1) * 128)
        y, xh, inv = _rms_fwd(o[:, sl], dng)
        sz, dsz = _silu(z[:, sl])
        heads.append((y, xh, inv, sz, dsz))
    gl, dgl = _gelu(lg)
    y2, xh2, inv2 = _rms_fwd(gl * hs, lrg)
    return heads, (y2, xh2, inv2, gl, dgl)


def _k_mix(cfg, o, proj, hs, r, dng, lrg, wout):
    dn = cfg.DN

    def body(o_ref, z_ref, lg_ref, hs_ref, r_ref, dng_ref, lrg_ref, w_ref, mix_ref, out_ref):
        heads, lru = _mix_parts(cfg, o_ref[...], z_ref[...], lg_ref[...], hs_ref[...], dng_ref[...], lrg_ref[...])
        for h, (y, _, _, sz, _) in enumerate(heads):
            mix_ref[:, h * 128:(h + 1) * 128] = (y * sz).astype(bf16)
        mix_ref[:, dn:] = lru[0].astype(bf16)
        out_ref[...] = r_ref[...] + jnp.dot(mix_ref[...], w_ref[...], preferred_element_type=f32)

    return _row_call("mix_out", body, cfg.T, cfg.TM,
                     [(o, dn, 0), (proj, dn, cfg.ZO // dn), (proj, cfg.LW, cfg.LGO // cfg.LW), (hs, cfg.LW, 0), (r, cfg.D, 0)],
                     [dng, lrg, wout], [(cfg.D, bf16), (cfg.D, f32)], [])


def _k_ffn_a(cfg, r, g2, wg, wu):
    def body(r_ref, g_ref, wg_ref, wu_ref, h_ref, gp_ref, up_ref):
        y, _, _ = _rms_fwd(r_ref[...], g_ref[...])
        hb = y.astype(bf16)
        h_ref[...] = hb
        gp_ref[...] = _mm_nt(hb, wg_ref[...])
        up_ref[...] = _mm_nt(hb, wu_ref[...])

    return _row_call("ffn_in", body, cfg.T, cfg.TMF, [(r, cfg.D, 0)], [g2, wg, wu],
                     [(cfg.D, bf16), (cfg.FF, f32), (cfg.FF, f32)], [])


def _k_ffn_b(cfg, gp, up, conv_w, conv_b):
    s, ft = cfg.S, cfg.FT

    def body(gp_ref, up_ref, w_ref, b_ref, o_ref):
        gate = _conv_fwd(gp_ref[...], w_ref, 1) + b_ref[...]
        gl, _ = _gelu(gate)
        o_ref[...] = (gl * up_ref[...]).astype(bf16)

    tok = pl.BlockSpec((s, ft), lambda b, j: (b, j))
    return pl.pallas_call(
        body, name="ffn_act", grid=(cfg.BL, cfg.FF // ft),
        in_specs=[tok, tok, pl.BlockSpec((3, ft), lambda b, j: (0, j)), pl.BlockSpec((1, ft), lambda b, j: (0, j))],
        out_specs=tok, out_shape=jax.ShapeDtypeStruct((cfg.T, cfg.FF), bf16),
        compiler_params=_params(("arbitrary", "arbitrary")),
    )(gp, up, conv_w, conv_b)


def _k_ffn_c(cfg, act, r, wd):
    def body(a_ref, r_ref, w_ref, o_ref):
        o_ref[...] = r_ref[...] + jnp.dot(a_ref[...], w_ref[...], preferred_element_type=f32)

    return _row_call("ffn_out", body, cfg.T, cfg.TM, [(act, cfg.FF, 0), (r, cfg.D, 0)], [wd], [(cfg.D, f32)], [])[0]


def _k_ple(cfg, r, p, gp, wpg, bg, wpp):
    def body(r_ref, p_ref, g_ref, wg_ref, bg_ref, wp_ref, pn_ref, o_ref):
        x = r_ref[...]
        y, _, _ = _rms_fwd(x, g_ref[...])
        pn = y.astype(bf16)
        pn_ref[...] = pn
        pg = _sigmoid(jnp.dot(pn, wg_ref[...], preferred_element_type=f32) + bg_ref[...])
        o_ref[...] = x + pg * _mm_nt(p_ref[...], wp_ref[...])

    return _row_call("ple", body, cfg.T, cfg.TM, [(r, cfg.D, 0), (p, cfg.PD, 0)], [gp, wpg, bg, wpp],
                     [(cfg.D, bf16), (cfg.D, f32)], [])


def _k_loss(cfg, r, tgt, gf):
    d = cfg.D

    def body(r_ref, t_ref, g_ref, dr_ref, loss_ref, dg_ref):
        @pl.when(pl.program_id(0) == 0)
        def _():
            loss_ref[...] = jnp.zeros_like(loss_ref)
            dg_ref[...] = jnp.zeros_like(dg_ref)

        g = g_ref[...]
        y, xh, inv = _rms_fwd(r_ref[...], g)
        err = y - t_ref[...]
        loss_ref[...] = loss_ref[...] + (0.5 / d) * jnp.sum(err * err)
        dx, dg = _rms_bwd(err * (1.0 / d), xh, inv, g)
        dr_ref[...] = dx
        dg_ref[...] = dg_ref[...] + dg

    return _row_call("loss_head", body, cfg.T, cfg.TM, [(r, d, 0), (tgt, d, 0)], [gf], [(d, f32)],
                     [((1, LANES), f32), ((1, d), f32)])


def _zero_at_first(cond, *refs):
    @pl.when(cond)
    def _():
        for r in refs:
            r[...] = jnp.zeros_like(r)


def _b_ple(cfg, dr3, r2, p, gp, wpg, bg, wpp):
    d = cfg.D

    def body(dr_ref, r_ref, p_ref, g_ref, wg_ref, bg_ref, wp_ref, dr2_ref, dr2b_ref, dlog_ref, dpp_ref, dgp_ref, dbg_ref):
        _zero_at_first(pl.program_id(0) == 0, dgp_ref, dbg_ref)
        g = g_ref[...]
        dr = dr_ref[...]
        y, xh, inv = _rms_fwd(r_ref[...], g)
        pg = _sigmoid(jnp.dot(y.astype(bf16), wg_ref[...], preferred_element_type=f32) + bg_ref[...])
        pp = _mm_nt(p_ref[...], wp_ref[...])
        dpp_ref[...] = (dr * pg).astype(bf16)
        dlog = dr * pp * pg * (1.0 - pg)
        dlog_ref[...] = dlog.astype(bf16)
        dbg_ref[...] = dbg_ref[...] + jnp.sum(dlog, axis=0, keepdims=True)
        dx, dg = _rms_bwd(_mm_nt(dlog, wg_ref[...]), xh, inv, g)
        dgp_ref[...] = dgp_ref[...] + dg
        dr2_ref[...] = dr + dx
        dr2b_ref[...] = (dr + dx).astype(bf16)

    return _row_call("ple_bwd", body, cfg.T, cfg.TM, [(dr3, d, 0), (r2, d, 0), (p, cfg.PD, 0)], [gp, wpg, bg, wpp],
                     [(d, f32), (d, bf16), (d, bf16), (d, bf16)], [((1, d), f32), ((1, d), f32)])


def _b_ffn_bc(cfg, dr2, wd, gp, up, conv_w, conv_b):
    s, ft, d = cfg.S, cfg.FT, cfg.D

    def body(dr_ref, wd_ref, gp_ref, up_ref, w_ref, b_ref, dgp_ref, dup_ref, dwd_ref, dcw_ref, dcb_ref):
        _zero_at_first(pl.program_id(1) == 0, dwd_ref, dcw_ref, dcb_ref)
        drb = dr_ref[...]
        dact = _mm_nt(drb, wd_ref[...])
        gpre = gp_ref[...]
        up = up_ref[...]
        gl, dgl = _gelu(_conv_fwd(gpre, w_ref, 1) + b_ref[...])
        dup_ref[...] = (dact * gl).astype(bf16)
        dgate = dact * up * dgl
        dcb_ref[...] = dcb_ref[...] + jnp.sum(dgate, axis=0, keepdims=True)
        dx, dws = _conv_bwd(dgate, gpre, w_ref, 1)
        for j, dw in enumerate(dws):
            dcw_ref[j:j + 1, :] = dcw_ref[j:j + 1, :] + dw
        dgp_ref[...] = dx.astype(bf16)
        dwd_ref[...] = dwd_ref[...] + _mm_tn((gl * up).astype(bf16), drb)

    tok = pl.BlockSpec((s, ft), lambda j, b: (b, j))
    return pl.pallas_call(
        body, name="ffn_act_bwd", grid=(cfg.FF // ft, cfg.BL),
        in_specs=[pl.BlockSpec((s, d), lambda j, b: (b, 0)), pl.BlockSpec((ft, d), lambda j, b: (j, 0)), tok, tok,
                  pl.BlockSpec((3, ft), lambda j, b: (0, j)), pl.BlockSpec((1, ft), lambda j, b: (0, j))],
        out_specs=[tok, tok, pl.BlockSpec((ft, d), lambda j, b: (j, 0)), pl.BlockSpec((3, ft), lambda j, b: (0, j)),
                   pl.BlockSpec((1, ft), lambda j, b: (0, j))],
        out_shape=[jax.ShapeDtypeStruct((cfg.T, cfg.FF), bf16), jax.ShapeDtypeStruct((cfg.T, cfg.FF), bf16),
                   jax.ShapeDtypeStruct((cfg.FF, d), f32), jax.ShapeDtypeStruct((3, cfg.FF), f32),
                   jax.ShapeDtypeStruct((1, cfg.FF), f32)],
        compiler_params=_params(("arbitrary", "arbitrary")),
    )(dr2, wd, gp, up, conv_w, conv_b)


def _b_ffn_a(cfg, dgp, dup, dr2, r_mid, g2, wg, wu):
    d = cfg.D

    def body(dgp_ref, dup_ref, dr_ref, r_ref, g_ref, wg_ref, wu_ref, o_ref, dg_ref):
        _zero_at_first(pl.program_id(0) == 0, dg_ref)
        g = g_ref[...]
        _, xh, inv = _rms_fwd(r_ref[...], g)
        dh = _mm(dgp_ref[...], wg_ref[...]) + _mm(dup_ref[...], wu_ref[...])
        dx, dg = _rms_bwd(dh, xh, inv, g)
        dg_ref[...] = dg_ref[...] + dg
        o_ref[...] = dr_ref[...] + dx

    return _row_call("ffn_in_bwd", body, cfg.T, cfg.TMF, [(dgp, cfg.FF, 0), (dup, cfg.FF, 0), (dr2, d, 0), (r_mid, d, 0)],
                     [g2, wg, wu], [(d, f32)], [((1, d), f32)])


def _mm_tn_call(cfg, name, x, dy, tn):
    t, k = x.shape
    n = dy.shape[1]
    tm = cfg.TM
    last = t // tm - 1

    def body(x_ref, dy_ref, o_ref, acc):
        _zero_at_first(pl.program_id(1) == 0, acc)
        acc[...] = acc[...] + _mm_tn(x_ref[...], dy_ref[...])

        @pl.when(pl.program_id(1) == last)
        def _():
            o_ref[...] = acc[...].astype(bf16)

    return pl.pallas_call(
        body, name=name, grid=(n // tn, t // tm),
        in_specs=[pl.BlockSpec((tm, k), lambda j, i: (i, 0)), pl.BlockSpec((tm, tn), lambda j, i: (i, j))],
        out_specs=pl.BlockSpec((k, tn), lambda j, i: (0, j)),
        out_shape=jax.ShapeDtypeStruct((k, n), bf16), scratch_shapes=[pltpu.VMEM((k, tn), f32)],
        compiler_params=_params(("arbitrary", "arbitrary")),
    )(x, dy)


def _b_mix(cfg, dr, o, proj, hs, dng, lrg, wout):
    dn, lw, d = cfg.DN, cfg.LW, cfg.D

    def body(dr_ref, o_ref, z_ref, lg_ref, hs_ref, dng_ref, lrg_ref, w_ref, do_ref, dz_ref, dlg_ref, dhs_ref, ddn_ref, dlr_ref):
        _zero_at_first(pl.program_id(0) == 0, ddn_ref, dlr_ref)
        dng, lrg = dng_ref[...], lrg_ref[...]
        hs = hs_ref[...]
        o, z = o_ref[...], z_ref[...]
        heads, lru = _mix_parts(cfg, o, z, lg_ref[...], hs, dng, lrg)
        drb = dr_ref[...].astype(bf16)
        dmix_dn = _mm_nt(drb, w_ref[0:dn, :])
        dmix_lr = _mm_nt(drb, w_ref[dn:, :])
        dgn = jnp.zeros_like(dng)
        for h, (y, xh, inv, sz, dsz) in enumerate(heads):
            sl = slice(h * 128, (h + 1) * 128)
            dm = dmix_dn[:, sl]
            dz_ref[:, sl] = dm * y * dsz
            dx, dg = _rms_bwd(dm * sz, xh, inv, dng)
            do_ref[:, sl] = dx
            dgn = dgn + dg
        ddn_ref[...] = ddn_ref[...] + dgn
        _, xh2, inv2, gl, dgl = lru
        dx2, dg2 = _rms_bwd(dmix_lr, xh2, inv2, lrg)
        dlr_ref[...] = dlr_ref[...] + dg2
        dlg_ref[...] = dx2 * hs * dgl
        dhs_ref[...] = dx2 * gl

    return _row_call("mix_bwd", body, cfg.T, cfg.TM,
                     [(dr, d, 0), (o, dn, 0), (proj, dn, cfg.ZO // dn), (proj, lw, cfg.LGO // lw), (hs, lw, 0)],
                     [dng, lrg, wout], [(dn, f32), (dn, f32), (lw, f32), (lw, f32)], [((1, 128), f32), ((1, lw), f32)])


def _b_lru(cfg, proj, dhs, conv_w, conv_b, wa, wx, ba, bx, lam):
    def body(lx_ref, dh_ref, cw_ref, cb_ref, wa_ref, wx_ref, ba_ref, bx_ref, lam_ref,
             dlx_ref, dcw_ref, dcb_ref, dwa_ref, dwx_ref, dba_ref, dbx_ref, dlam_ref, a_s, b_s, h_s, an_s, l_s):
        _zero_at_first(pl.program_id(1) == 0, dcw_ref, dcb_ref, dwa_ref, dwx_ref, dba_ref, dbx_ref, dlam_ref)
        lx = lx_ref[...]
        xc = _conv_fwd(lx, cw_ref, 2) + cb_ref[...]
        dxc = jnp.zeros_like(xc)
        gates = []
        for d in range(2):
            z = _lru_gates(xc, wa_ref[d, 0], wx_ref[d, 0], ba_ref[d:d + 1, :], bx_ref[d:d + 1, :], lam_ref[d:d + 1, :])
            a_s[d] = z["a"]
            b_s[d] = z["m"] * z["gx"]
            an_s[d] = _shift(z["a"], -1 if d == 1 else 1, 0.0)
            gates.append(z)
        _scan_refs([(a_s.at[d], b_s.at[d], h_s.at[d], d == 1) for d in range(2)]
                   + [(an_s.at[d], dh_ref, l_s.at[d], d == 0) for d in range(2)], cfg.S)
        for d in range(2):
            rev = d == 1
            lam = lam_ref[d:d + 1, :]
            z = gates[d]
            a, m, ra, ig, sp = z["a"], z["m"], z["ra"], z["ig"], z["sp"]
            lmb = l_s[d]
            h_prev = _shift(h_s[d], 1 if rev else -1, 0.0)
            da = lmb * h_prev
            dm = lmb * z["gx"]
            dgx = lmb * m
            dla = da * a - dm * (a * a) / jnp.maximum(m, 1e-30)
            dra = dla * (-LRU_C) * sp
            dsp = jnp.sum(dla * (-LRU_C) * ra, axis=0, keepdims=True)
            dlam_ref[d:d + 1, :] = dlam_ref[d:d + 1, :] - dsp * _sigmoid(-lam)
            dpa = dra * ra * (1.0 - ra)
            dpx = dgx * xc * ig * (1.0 - ig)
            dba_ref[d:d + 1, :] = dba_ref[d:d + 1, :] + jnp.sum(dpa, axis=0, keepdims=True)
            dbx_ref[d:d + 1, :] = dbx_ref[d:d + 1, :] + jnp.sum(dpx, axis=0, keepdims=True)
            dwa_ref[d, 0] = dwa_ref[d, 0] + _mm_tn(xc, dpa)
            dwx_ref[d, 0] = dwx_ref[d, 0] + _mm_tn(xc, dpx)
            dxc = dxc + dgx * ig + _mm_nt(dpa, wa_ref[d, 0]) + _mm_nt(dpx, wx_ref[d, 0])
        dcb_ref[...] = dcb_ref[...] + jnp.sum(dxc, axis=0, keepdims=True)
        dx, dws = _conv_bwd(dxc, lx, cw_ref, 2)
        for j, dw in enumerate(dws):
            dcw_ref[j:j + 1, :] = dcw_ref[j:j + 1, :] + dw
        dlx_ref[...] = dx

    sp = _lru_specs(cfg, False)
    nct = cfg.LW // LANES
    return pl.pallas_call(
        body, name="lru_bwd", grid=(nct, cfg.BL),
        in_specs=[sp["lx"], sp["tok"], sp["cw"], sp["row"], sp["w"], sp["w"], sp["two"], sp["two"], sp["two"]],
        out_specs=[sp["tok"], sp["cw"], sp["row"], sp["w"], sp["w"], sp["two"], sp["two"], sp["two"]],
        out_shape=[jax.ShapeDtypeStruct((cfg.T, cfg.LW), f32), jax.ShapeDtypeStruct((4, cfg.LW), f32),
                   jax.ShapeDtypeStruct((1, cfg.LW), f32), jax.ShapeDtypeStruct((2, nct, LANES, LANES), f32),
                   jax.ShapeDtypeStruct((2, nct, LANES, LANES), f32), jax.ShapeDtypeStruct((2, cfg.LW), f32),
                   jax.ShapeDtypeStruct((2, cfg.LW), f32), jax.ShapeDtypeStruct((2, cfg.LW), f32)],
        scratch_shapes=[pltpu.VMEM((2, cfg.S, LANES), f32)] * 5,
        compiler_params=_params(("arbitrary", "arbitrary")),
    )(proj, dhs, conv_w, conv_b, wa, wx, ba, bx, lam)


def _b_gdn(cfg, qkv, proj, do, alog_row, dt_row, saved):
    s, c, nc, hh = cfg.S, cfg.C, cfg.NC, cfg.H
    ba_blk = cfg.BAO // LANES
    scale = 128 ** -0.5
    per = 4
    assert nc % per == 0

    def body(q_ref, k_ref, v_ref, ba_ref, do_ref, al_ref, dt_ref, u_i, w_i, qd_i, kd_i, p_i, t_i, dec_i, cd_i,
             dqkv_ref, dba_ref, dal_ref, ddt_ref, vn_s, dvn_s, st_s, dst_s):
        h = pl.program_id(1)
        _zero_at_first((pl.program_id(0) == 0) & (h == 0), dal_ref, ddt_ref)
        _zero_at_first(h == 0, dba_ref)
        lane = lax.broadcasted_iota(jnp.int32, (c, LANES), 1)
        lane1 = lax.broadcasted_iota(jnp.int32, (1, LANES), 1)
        dirs = (0, 1)
        lane_b = [d * hh + h for d in dirs]
        lane_a = [2 * hh + d * hh + h for d in dirs]

        def seq(i, box):
            sts, dst = box[0]
            live = i < nc
            ii = jnp.where(live, i, 0)
            nf = [ii, nc - 1 - ii]
            nb = [nc - 1 - ii, ii]
            sf = [jnp.where(live, n, nc) for n in nf]
            sb = [jnp.where(live, n, nc) for n in nb]
            rf = [pl.ds(pl.multiple_of(n * c, c), c) for n in nf]
            rb = [pl.ds(pl.multiple_of(n * c, c), c) for n in nb]
            for d in dirs:
                st_s[d, sf[d]] = sts[d]
                dst_s[d, sb[d]] = dst[d]
            dob = [do_ref[rb[d], :] for d in dirs]
            vn = [u_i[d, rf[d], :] - _mm(w_i[d, rf[d], :], sts[d]) for d in dirs]
            dvn = [_mm_tn(p_i[d, 0, nb[d]], dob[d]) + _mm(kd_i[d, rb[d], :], dst[d]) for d in dirs]
            yield
            for d in dirs:
                vn_s[d, pl.ds(pl.multiple_of(sf[d] * c, c), c), :] = vn[d]
                dvn_s[d, pl.ds(pl.multiple_of(sb[d] * c, c), c), :] = dvn[d]
            sts = tuple(sts[d] * cd_i[d, 0, nf[d]][0:1, :] + _mm_tn(kd_i[d, rf[d], :], vn[d]) for d in dirs)
            dst = tuple(_mm_tn(qd_i[d, rb[d], :], dob[d]) + cd_i[d, 0, nb[d]][0:1, 0:1] * dst[d]
                        - _mm_tn(w_i[d, rb[d], :], dvn[d]) for d in dirs)
            box[0] = (sts, dst)
            yield

        def bpar(chunks, tick=lambda: None):
            ch = [(d, n) for n in chunks for d in dirs]
            rows = [pl.ds(pl.multiple_of(n * c, c), c) for _, n in ch]
            masks = [_tri_masks(c, d == 1) for d, _ in ch]
            ld = lambda ref: [ref[d, r, :] for (d, _), r in zip(ch, rows)]
            ldm = lambda ref: [ref[d, 0, n] for d, n in ch]
            q, k, v, dob = ([ref[r, :] for r in rows] for ref in (q_ref, k_ref, v_ref, do_ref))
            gates = [_gate_cols(ba_ref[r, :], al_ref[...], dt_ref[...], lane_b[d], lane_a[d]) for (d, _), r in zip(ch, rows)]
            beta, g, aexp, xa = ([gt[x] for gt in gates] for x in range(4))
            st, dst = ([ref[d, n] for d, n in ch] for ref in (st_s, dst_s))
            vn, dvn, u, w = ld(vn_s), ld(dvn_s), ld(u_i), ld(w_i)
            p = [x.astype(f32) for x in ldm(p_i)]
            tm, dec = ldm(t_i), ldm(dec_i)
            cd = [cd_i[d, 0, n][0:1, 0:1] for d, n in ch]
            dp = _map(lambda m, a, b: jnp.where(m[0], _mm_nt(a, b), 0.0), masks, dob, vn)
            dqd = _map(_mm_nt, dob, st)
            dkd = _map(_mm_nt, vn, dst)
            dw = _map(lambda a, b: -_mm_nt(a, b), dvn, st)
            dcd = _map(lambda a, b: jnp.sum(jnp.sum(a * b, axis=1, keepdims=True), axis=0, keepdims=True), st, dst)
            tick()
            gcum = _map(lambda m, x: _cum_mm(m[0], jnp.broadcast_to(x, (c, c)))[:, 0:1], masks, g)
            glast = [jnp.sum(x, axis=0, keepdims=True) for x in g]
            e = [jnp.exp(x) for x in gcum]
            el = _map(lambda a, b: jnp.exp(a - b), glast, gcum)
            qs = [x * scale for x in q]
            kb = _map(lambda a, b: a * b, k, beta)
            a = _map(lambda m, b, ki, dc: jnp.where(m[1], b * _mm_nt(ki, ki) * dc, 0.0), masks, beta, k, dec)
            tick()
            dvb = _map(_hmm3_tn, tm, dvn)
            dkbe = _map(_hmm3_tn, tm, dw)
            tick()
            da = _map(lambda m, x, ui, y, wi: -jnp.where(m[1], _mm_nt(x, ui) + _mm_nt(y, wi), 0.0), masks, dvb, u, dkbe, w)
            g1 = _map(lambda x, y: x * y, da, dec)
            g2 = _map(lambda x, y: x * y, dp, dec)
            dkb = _map(lambda x, ki, y, ei: _mm(x, ki) + y * ei, g1, k, dkbe, e)
            tick()
            dk = _map(lambda x, kbi, y, qi, z, b, t, l: _mm_tn(x, kbi) + _mm_tn(y, qi) + z * b + t * l,
                      g1, kb, g2, qs, dkb, beta, dkd, el)
            dqs = _map(lambda y, ki, x, ei: _mm(y, ki) + x * ei, g2, k, dqd, e)
            tick()
            ddd = _map(lambda x, ai, y, pi: x * ai + y * pi, da, a, dp, p)
            ones = jnp.ones((c, LANES), f32)
            dgcum = _map(lambda x: jnp.sum(x, axis=1, keepdims=True) - _hmm3_tn(x, ones)[:, 0:1], ddd)
            for x, (d, n) in enumerate(ch):
                dbeta = jnp.sum(dvb[x] * v[x], axis=1, keepdims=True) + jnp.sum(dkb[x] * k[x], axis=1, keepdims=True)
                de = jnp.sum(dkbe[x] * kb[x], axis=1, keepdims=True) + jnp.sum(dqd[x] * qs[x], axis=1, keepdims=True)
                del_ = jnp.sum(dkd[x] * k[x], axis=1, keepdims=True)
                dgc = dgcum[x] + de * e[x] - del_ * el[x]
                dglast = jnp.sum(del_ * el[x], axis=0, keepdims=True) + dcd[x] * cd[x]
                dg = _cum_mm(masks[x][2], jnp.broadcast_to(dgc, (c, LANES)))[:, 0:1] + dglast
                r = rows[x]
                if d == 0:
                    dqkv_ref[0, r, :] = dqs[x] * scale
                    dqkv_ref[1, r, :] = dk[x]
                    dqkv_ref[2, r, :] = dvb[x] * beta[x]
                else:
                    dqkv_ref[0, r, :] = dqkv_ref[0, r, :] + dqs[x] * scale
                    dqkv_ref[1, r, :] = dqkv_ref[1, r, :] + dk[x]
                    dqkv_ref[2, r, :] = dqkv_ref[2, r, :] + dvb[x] * beta[x]
                dlb = dbeta * beta[x] * (1.0 - beta[x])
                dalpha = -dg * aexp[x] * _sigmoid(xa[x])
                dba_ref[r, :] = dba_ref[r, :] + jnp.where(lane == lane_b[d], dlb, 0.0) + jnp.where(lane == lane_a[d], dalpha, 0.0)
                dal_ref[...] = dal_ref[...] + jnp.where(lane1 == lane_a[d], jnp.sum(dg * g[x], axis=0, keepdims=True), 0.0)
                ddt_ref[...] = ddt_ref[...] + jnp.where(lane1 == lane_a[d], jnp.sum(dalpha, axis=0, keepdims=True), 0.0)

        half, groups = nc // 2, nc // per

        def group(k):
            return [half - 2 * k - 2, half - 2 * k - 1, half + 2 * k, half + 2 * k + 1]

        def run(gen):
            for _ in gen:
                pass

        def plain(i, carry):
            box = [carry]
            run(seq(i, box))
            return box[0]

        def woven(k, carry):
            box = [carry]

            def two_steps():
                yield from seq(half + 2 + 2 * k, box)
                yield from seq(half + 3 + 2 * k, box)

            chain = two_steps()
            bpar(group(k), lambda: next(chain, None))
            run(chain)
            return box[0]

        z0 = jnp.zeros((128, 128), f32)
        lax.fori_loop(0, groups, woven, lax.fori_loop(0, half + 2, plain, ((z0, z0), (z0, z0))))

    blk = lambda off: pl.BlockSpec((s, 128), functools.partial(lambda b, h, off: (b, off + h), off=off))
    row = pl.BlockSpec((1, LANES), lambda b, h: (0, 0))
    tok2, mat2, cd2, _ = _gdn_specs(cfg)
    return pl.pallas_call(
        body, name="gdn_bwd", grid=(cfg.BL, hh),
        in_specs=[blk(0), blk(hh), blk(2 * hh), pl.BlockSpec((s, LANES), lambda b, h: (b, ba_blk)),
                  pl.BlockSpec((s, 128), lambda b, h: (b, h)), row, row, tok2, tok2, tok2, tok2, mat2, mat2, mat2, cd2],
        out_specs=[pl.BlockSpec((3, s, 128), lambda b, h: (0, b, h)), pl.BlockSpec((s, LANES), lambda b, h: (b, 0)), row, row],
        out_shape=[jax.ShapeDtypeStruct((3, cfg.T, cfg.DN), f32), jax.ShapeDtypeStruct((cfg.T, LANES), f32),
                   jax.ShapeDtypeStruct((1, LANES), f32), jax.ShapeDtypeStruct((1, LANES), f32)],
        scratch_shapes=[pltpu.VMEM((2, s + c, 128), f32)] * 2 + [pltpu.VMEM((2, nc + 1, 128, 128), f32)] * 2,
        compiler_params=_params(("arbitrary", "arbitrary")),
    )(qkv, qkv, qkv, proj, do, alog_row, dt_row, *saved)


def _b_prep(cfg, proj, dqkv, conv_w):
    dn, s = cfg.DN, cfg.S

    def body(x_ref, dy_ref, w_ref, dx_ref, dw_ref):
        _zero_at_first(pl.program_id(1) == 0, dw_ref)
        sec = pl.program_id(0)
        x = x_ref[...]
        c = _conv_fwd(x, w_ref, 2)
        y, dsilu = _silu(c)
        dy = dy_ref[0]
        parts = []
        for h in range(cfg.H):
            sl = slice(h * 128, (h + 1) * 128)
            yh, dyh = y[:, sl], dy[:, sl]
            inv = lax.rsqrt(jnp.sum(yh * yh, axis=1, keepdims=True) + EPS)
            dn_h = inv * dyh - yh * (inv * inv * inv) * jnp.sum(dyh * yh, axis=1, keepdims=True)
            parts.append(jnp.where(sec < 2, dn_h, dyh))
        ds = jnp.concatenate(parts, axis=1) if len(parts) > 1 else parts[0]
        dx, dws = _conv_bwd(ds * dsilu, x, w_ref, 2)
        for j, dw in enumerate(dws):
            dw_ref[j:j + 1, :] = dw_ref[j:j + 1, :] + dw
        dx_ref[...] = dx

    return pl.pallas_call(
        body, name="dn_prep_bwd", grid=(3, cfg.BL),
        in_specs=[pl.BlockSpec((s, dn), lambda j, b: (b, j)), pl.BlockSpec((1, s, dn), lambda j, b: (j, b, 0)),
                  pl.BlockSpec((4, dn), lambda j, b: (0, j))],
        out_specs=[pl.BlockSpec((s, dn), lambda j, b: (b, j)), pl.BlockSpec((4, dn), lambda j, b: (0, j))],
        out_shape=[jax.ShapeDtypeStruct((cfg.T, 3 * dn), f32), jax.ShapeDtypeStruct((4, 3 * dn), f32)],
        compiler_params=_params(("arbitrary", "arbitrary")),
    )(proj, dqkv, conv_w)


def _b_in(cfg, dqkv_pre, dz, dlx, dlg, dba, dr_mid, r_in, g1, wcat):
    d, dn, lw = cfg.D, cfg.DN, cfg.LW

    def body(dq_ref, dz_ref, dlx_ref, dlg_ref, dba_ref, dr_ref, r_ref, g_ref, w_ref, o_ref, dp_ref, dg_ref):
        _zero_at_first(pl.program_id(0) == 0, dg_ref)
        dp_ref[:, 0:cfg.ZO] = dq_ref[...].astype(bf16)
        dp_ref[:, cfg.ZO:cfg.LXO] = dz_ref[...].astype(bf16)
        dp_ref[:, cfg.LXO:cfg.LGO] = dlx_ref[...].astype(bf16)
        dp_ref[:, cfg.LGO:cfg.BAO] = dlg_ref[...].astype(bf16)
        dp_ref[:, cfg.BAO:] = dba_ref[...].astype(bf16)
        g = g_ref[...]
        _, xh, inv = _rms_fwd(r_ref[...], g)
        dx, dg = _rms_bwd(_mm(dp_ref[...], w_ref[...]), xh, inv, g)
        dg_ref[...] = dg_ref[...] + dg
        o_ref[...] = dr_ref[...] + dx

    return _row_call("in_proj_bwd", body, cfg.T, cfg.TM,
                     [(dqkv_pre, 3 * dn, 0), (dz, dn, 0), (dlx, lw, 0), (dlg, lw, 0), (dba, LANES, 0), (dr_mid, d, 0), (r_in, d, 0)],
                     [g1, wcat], [(d, f32), (cfg.PC, bf16)], [((1, d), f32)])


def _adam_call(name, w, g, m, v, tr):
    rows, cols = w.shape
    bc1 = 1.0 - ADAM_B1 ** ADAM_STEP
    bc2 = 1.0 - ADAM_B2 ** ADAM_STEP

    def body(w_ref, g_ref, m_ref, v_ref, d_ref, nm_ref, nv_ref):
        g = g_ref[...]
        m = ADAM_B1 * m_ref[...] + (1.0 - ADAM_B1) * g
        v = ADAM_B2 * v_ref[...] + (1.0 - ADAM_B2) * (g * g)
        nm_ref[...] = m
        nv_ref[...] = v
        d_ref[...] = -ADAM_LR * ((m / bc1) / (jnp.sqrt(v / bc2) + ADAM_EPS) + ADAM_WD * w_ref[...])

    spec = pl.BlockSpec((tr, cols), lambda i: (i, 0))
    return pl.pallas_call(
        body, name=name, grid=(rows // tr,), in_specs=[spec] * 4, out_specs=[spec] * 3,
        out_shape=[jax.ShapeDtypeStruct((rows, cols), f32)] * 3, compiler_params=_params(("arbitrary",)),
    )(w, g, m, v)


def _sum8_call(name, x, tr):
    _, rows, cols = x.shape

    def body(x_ref, o_ref):
        acc = x_ref[0].astype(f32)
        for j in range(1, N_DEV):
            acc = acc + x_ref[j].astype(f32)
        o_ref[...] = acc

    return pl.pallas_call(
        body, name=name, grid=(rows // tr,), in_specs=[pl.BlockSpec((N_DEV, tr, cols), lambda i: (0, i, 0))],
        out_specs=pl.BlockSpec((tr, cols), lambda i: (i, 0)), out_shape=jax.ShapeDtypeStruct((rows, cols), f32),
        compiler_params=_params(("arbitrary",)),
    )(x)


def _all_gather(name, shards):
    na = len(shards)

    def body(*refs):
        xs, outs = refs[:na], refs[na:2 * na]
        send_sems, recv_sems, local_sems = refs[2 * na:]
        x, y, c = lax.axis_index("x"), lax.axis_index("y"), lax.axis_index("c")
        me, sibling = (x, y, c), (x, y, 1 - c)
        chips = [(1 - x, y), (x, 1 - y), (1 - x, 1 - y)]

        def copy(a, k, block, to, src=None):
            px, py, pc = block
            slot = outs[a].at[4 * px + 2 * py + pc]
            return pltpu.make_async_remote_copy(
                src_ref=slot if src is None else src, dst_ref=slot,
                send_sem=send_sems.at[a, k], recv_sem=recv_sems.at[a, k], device_id=to, device_id_type=MESH)

        mine = [pltpu.make_async_copy(xs[a], outs[a].at[4 * x + 2 * y + c], local_sems.at[a]) for a in range(na)]
        for cp in mine:
            cp.start()
        first = []
        for a in range(na):
            first.append(copy(a, 0, me, sibling, src=xs[a]))
            first += [copy(a, 1 + j, me, (*chip, c), src=xs[a]) for j, chip in enumerate(chips)]
        for cp in first:
            cp.start()
        passed = []
        for j, chip in enumerate(chips):
            for a in range(na):
                copy(a, 1 + j, (*chip, c), me).wait_recv()
                cp = copy(a, 4 + j, (*chip, c), sibling)
                cp.start()
                passed.append(cp)
        for a in range(na):
            copy(a, 0, sibling, me).wait_recv()
            for j, chip in enumerate(chips):
                copy(a, 4 + j, (*chip, 1 - c), me).wait_recv()
        for cp in first + passed:
            cp.wait_send()
        for cp in mine:
            cp.wait()

    hbm = pl.BlockSpec(memory_space=pltpu.HBM)
    return pl.pallas_call(
        body, name=name, out_shape=[jax.ShapeDtypeStruct((N_DEV,) + s.shape, s.dtype) for s in shards],
        in_specs=[hbm] * na, out_specs=[hbm] * na,
        scratch_shapes=[pltpu.SemaphoreType.DMA((na, 7)), pltpu.SemaphoreType.DMA((na, 7)), pltpu.SemaphoreType.DMA((na,))],
    )(*shards)


def _peer_list():
    x, y, c = lax.axis_index("x"), lax.axis_index("y"), lax.axis_index("c")
    return 4 * x + 2 * y + c, [(x ^ (k >> 2), y ^ ((k >> 1) & 1), c ^ (k & 1)) for k in range(1, N_DEV)]


_HBM = pl.BlockSpec(memory_space=pltpu.HBM)
_SEM = pl.BlockSpec(memory_space=pltpu.SEMAPHORE)
_EFFECT = pltpu.SideEffectType.DATAFLOW_SIDE_EFFECTING


def _exchange_copies(xs, lands, send_sem, recv_sem, gather):
    me, peers = _peer_list()
    return [pltpu.make_async_remote_copy(
        src_ref=xs[a] if gather else xs[a].at[4 * px + 2 * py + pc], dst_ref=lands[a].at[me],
        send_sem=send_sem.at[7 * a + k], recv_sem=recv_sem.at[7 * a + k], device_id=(px, py, pc), device_id_type=MESH)
        for k, (px, py, pc) in enumerate(peers) for a in range(len(xs))]


def _exchange_start(name, blocks, gather=False):
    na = len(blocks)

    def body(*refs):
        for cp in _exchange_copies(refs[:na], refs[na:2 * na], refs[2 * na], refs[2 * na + 1], gather):
            cp.start()
        refs[-1][...] = jnp.zeros_like(refs[-1])

    lands = [jax.ShapeDtypeStruct((N_DEV,) + b.shape if gather else b.shape, b.dtype) for b in blocks]
    hbm = [pltpu.HBM(b.shape, b.dtype) for b in blocks] + [pltpu.HBM(b.shape, b.dtype) for b in lands]
    send_sem, recv_sem, *thru, token = pl.pallas_call(
        body, name=name,
        out_shape=(pltpu.SemaphoreType.DMA((7 * na,)), pltpu.SemaphoreType.DMA((7 * na,)), *hbm,
                   jax.ShapeDtypeStruct((8, LANES), f32)),
        in_specs=[_HBM] * (2 * na), out_specs=(_SEM, _SEM, *([_HBM] * (2 * na)), pl.BlockSpec(memory_space=pltpu.VMEM)),
        input_output_aliases={i: 2 + i for i in range(2 * na)},
        compiler_params=pltpu.CompilerParams(has_side_effects=_EFFECT),
    )(*[pltpu.with_memory_space_constraint(b, pltpu.HBM) for b in blocks],
      *[pltpu.with_memory_space_constraint(lax.empty(b.shape, b.dtype), pltpu.HBM) for b in lands])
    return send_sem, recv_sem, thru, token


def _exchange_wait(name, send_sem, recv_sem, thru, after, gather=False):
    na = len(thru) // 2

    def body(*refs):
        for cp in _exchange_copies(refs[:na], refs[na:2 * na], refs[2 * na], refs[2 * na + 1], gather):
            cp.wait_send()
            cp.wait_recv()

    return pl.pallas_call(
        body, name=name, out_shape=tuple(pltpu.HBM(t.shape, t.dtype) for t in thru),
        in_specs=[_HBM] * (2 * na) + [_SEM, _SEM, pl.BlockSpec(memory_space=pl.ANY)], out_specs=tuple([_HBM] * (2 * na)),
        input_output_aliases={i: i for i in range(2 * na)},
        compiler_params=pltpu.CompilerParams(has_side_effects=_EFFECT),
    )(*thru, send_sem, recv_sem, after)[na:]


def _layer_fwd(cfg, w, r, p):
    h1, proj = _k_in(cfg, r, w["norm1_g"], w["wcat_t"])
    qkv = _k_prep(cfg, proj, w["dn_conv_w"])
    o, *gdn_saved = _k_gdn_fwd(cfg, qkv, proj, w["alog_row"], w["dt_row"])
    hs = _k_lru_fwd(cfg, proj, w["lru_conv_w"], w["lru_conv_b"], w["wa"], w["wx"], w["lru_ba"], w["lru_bx"], w["lru_lambda"])
    mix, r_mid = _k_mix(cfg, o, proj, hs, r, w["dn_norm_g"], w["lru_norm_g"], w["w_out"])
    h2, gp, up = _k_ffn_a(cfg, r_mid, w["norm2_g"], w["ffn_wg_t"], w["ffn_wu_t"])
    act = _k_ffn_b(cfg, gp, up, w["ffn_conv_w"], w["ffn_conv_b"])
    r2 = _k_ffn_c(cfg, act, r_mid, w["ffn_wd"])
    pn, r3 = _k_ple(cfg, r2, p, w["ple_norm_g"], w["ple_wg"], w["ple_bg"], w["ple_wp_t"])
    saved = dict(r=r, h1=h1, proj=proj, qkv=qkv, o=o, gdn=gdn_saved, hs=hs, mix=mix, r_mid=r_mid, h2=h2, gp=gp, up=up,
                 r2=r2, pn=pn, p=p)
    return r3, saved


def _layer_bwd(cfg, w, sv, dr3, early=None):
    g = {}
    dt = min(512, cfg.D)
    dr2, dr2b, dlog, dpp, g["ple_norm_g"], g["ple_bg"] = _b_ple(cfg, dr3, sv["r2"], sv["p"], w["ple_norm_g"], w["ple_wg"], w["ple_bg"], w["ple_wp_t"])
    g["ple_wg"] = _mm_tn_call(cfg, "d_ple_wg", sv["pn"], dlog, dt)
    g["ple_wp_t"] = _mm_tn_call(cfg, "d_ple_wp", dpp, sv["p"], cfg.PD)
    dgp, dup, g["ffn_wd"], g["ffn_conv_w"], g["ffn_conv_b"] = _b_ffn_bc(cfg, dr2b, w["ffn_wd"], sv["gp"], sv["up"], w["ffn_conv_w"], w["ffn_conv_b"])
    dr_mid, g["norm2_g"] = _b_ffn_a(cfg, dgp, dup, dr2, sv["r_mid"], w["norm2_g"], w["ffn_wg_t"], w["ffn_wu_t"])
    g["ffn_wg_t"] = _mm_tn_call(cfg, "d_ffn_wg", dgp, sv["h2"], dt)
    g["ffn_wu_t"] = _mm_tn_call(cfg, "d_ffn_wu", dup, sv["h2"], dt)
    if early is not None:
        w = dict(w, dn_norm_g=w["dn_norm_g"] + early(g)[0, 0])
    do, dz, dlg, dhs, g["dn_norm_g"], g["lru_norm_g"] = _b_mix(cfg, dr_mid, sv["o"], sv["proj"], sv["hs"], w["dn_norm_g"], w["lru_norm_g"], w["w_out"])
    g["w_out"] = _mm_tn_call(cfg, "d_w_out", sv["mix"], dr_mid, dt)
    dlx, g["lru_conv_w"], g["lru_conv_b"], g["wa"], g["wx"], g["lru_ba"], g["lru_bx"], g["lru_lambda"] = _b_lru(
        cfg, sv["proj"], dhs, w["lru_conv_w"], w["lru_conv_b"], w["wa"], w["wx"], w["lru_ba"], w["lru_bx"], w["lru_lambda"])
    dqkv, dba, g["alog_row"], g["dt_row"] = _b_gdn(cfg, sv["qkv"], sv["proj"], do, w["alog_row"], w["dt_row"], sv["gdn"])
    dqkv_pre, g["dn_conv_w"] = _b_prep(cfg, sv["proj"], dqkv, w["dn_conv_w"])
    dr, dproj, g["norm1_g"] = _b_in(cfg, dqkv_pre, dz, dlx, dlg, dba, dr_mid, sv["r"], w["norm1_g"], w["wcat_t"])
    g["wcat_t"] = _mm_tn_call(cfg, "d_w_in", dproj, sv["h1"], dt)
    return dr, g


BIG = ("w_in", "w_out", "ffn_wg", "ffn_wu", "ffn_wd", "ple_wg", "ple_wp")
BIG_T = {"w_in": True, "w_out": False, "ffn_wg": True, "ffn_wu": True, "ffn_wd": False, "ple_wg": False, "ple_wp": True}
BIG_OPERAND = {"w_in": "wcat_t", "w_out": "w_out", "ffn_wg": "ffn_wg_t", "ffn_wu": "ffn_wu_t", "ffn_wd": "ffn_wd",
               "ple_wg": "ple_wg", "ple_wp": "ple_wp_t"}
EARLY = ("ffn_wg", "ffn_wu", "ffn_wd", "ple_wg", "ple_wp")
SMALL_SHARDED = ("dn_conv_w", "lru_conv_w", "lru_ba", "lru_bx", "lru_lambda", "ffn_conv_w")
SMALL_REPL = ("norm1_g", "dn_a_log", "dn_dt_bias", "dn_norm_g", "lru_conv_b", "lru_wa", "lru_wx", "lru_norm_g", "norm2_g",
              "ffn_conv_b", "ple_norm_g", "ple_bg", "final_g")
WEIGHTS = ("norm1_g", "w_in", "dn_conv_w", "dn_a_log", "dn_dt_bias", "dn_norm_g", "lru_conv_w", "lru_conv_b", "lru_wa",
           "lru_ba", "lru_wx", "lru_bx", "lru_lambda", "lru_norm_g", "w_out", "norm2_g", "ffn_wg", "ffn_wu", "ffn_conv_w",
           "ffn_conv_b", "ffn_wd", "ple_norm_g", "ple_wg", "ple_bg", "ple_wp", "final_g")


def _pad_rows(flat, cols, mult):
    n = flat.shape[0]
    rows = -(-n // cols)
    rows = -(-rows // mult) * mult
    return jnp.pad(flat, (0, rows * cols - n)).reshape(rows, cols)


def _pack(arrs, cols, mult, dtype):
    return _pad_rows(jnp.concatenate([a.reshape(-1).astype(dtype) for a in arrs]), cols, mult)


def _unpack(flat, shapes):
    out, off = [], 0
    for shp in shapes:
        n = math.prod(shp)
        piece = flat[off:off + n]
        if n < 4096:
            piece = lax.optimization_barrier(piece)
        out.append(piece.reshape(shp))
        off += n
    return out


def _unpack8(g8, shapes, axes):
    out, off = [], 0
    for shp, ax in zip(shapes, axes):
        n = math.prod(shp)
        a = g8[:, off:off + n].reshape((N_DEV,) + tuple(shp))
        a = jnp.moveaxis(a, 0, ax)
        out.append(a.reshape(shp[:ax] + (N_DEV * shp[ax],) + shp[ax + 1:]))
        off += n
    return out


def _wcat_t_from_w_in_t(cfg, wt):
    nba = 4 * cfg.H
    pad = jnp.zeros((LANES - nba, wt.shape[1]), wt.dtype)
    return jnp.concatenate([wt[:cfg.LXO], wt[cfg.LXO + nba:], wt[cfg.LXO:cfg.LXO + nba], pad], axis=0)


def _w_in_t_from_wcat_t(cfg, wc):
    nba = 4 * cfg.H
    return jnp.concatenate([wc[:cfg.LXO], wc[cfg.BAO:cfg.BAO + nba], wc[cfg.LXO:cfg.BAO]], axis=0)


def _gate_row(cfg, a):
    h2 = 2 * cfg.H
    return jnp.concatenate([jnp.zeros((1, h2), f32), a.reshape(1, h2), jnp.zeros((1, LANES - 2 * h2), f32)], axis=1)


def _blockdiag(cfg, w):
    w = w.reshape(2, cfg.NB // 2, 2, 64, 64)
    z = jnp.zeros_like(w[:, :, 0])
    top = jnp.concatenate([w[:, :, 0], z], axis=-1)
    bot = jnp.concatenate([z, w[:, :, 1]], axis=-1)
    return jnp.concatenate([top, bot], axis=-2).astype(bf16)


def _unblockdiag(cfg, g):
    a = g[:, :, :64, :64]
    b = g[:, :, 64:, 64:]
    return jnp.stack([a, b], axis=2).reshape(2, cfg.NB, 64, 64)


def _layer_operands(cfg, big, small, i):
    return dict(
        wcat_t=_wcat_t_from_w_in_t(cfg, big["w_in"]), w_out=big["w_out"], ffn_wg_t=big["ffn_wg"],
        ffn_wu_t=big["ffn_wu"], ffn_wd=big["ffn_wd"], ple_wg=big["ple_wg"], ple_wp_t=big["ple_wp"],
        norm1_g=small["norm1_g"][i][None], dn_conv_w=small["dn_conv_w"][i], alog_row=_gate_row(cfg, small["dn_a_log"][i]),
        dt_row=_gate_row(cfg, small["dn_dt_bias"][i]), dn_norm_g=small["dn_norm_g"][i][None],
        lru_conv_w=small["lru_conv_w"][i], lru_conv_b=small["lru_conv_b"][i][None],
        wa=_blockdiag(cfg, small["lru_wa"][i]), wx=_blockdiag(cfg, small["lru_wx"][i]),
        lru_ba=small["lru_ba"][i], lru_bx=small["lru_bx"][i], lru_lambda=small["lru_lambda"][i],
        lru_norm_g=small["lru_norm_g"][i][None], norm2_g=small["norm2_g"][i][None], ffn_conv_w=small["ffn_conv_w"][i],
        ffn_conv_b=small["ffn_conv_b"][i][None], ple_norm_g=small["ple_norm_g"][i][None], ple_bg=small["ple_bg"][i][None],
    )


def _small_grads_to_problem(cfg, g):
    h = cfg.H
    return dict(
        norm1_g=g["norm1_g"][0], dn_conv_w=g["dn_conv_w"],
        dn_a_log=g["alog_row"][0, 2 * h:4 * h].reshape(2, h), dn_dt_bias=g["dt_row"][0, 2 * h:4 * h].reshape(2, h),
        dn_norm_g=g["dn_norm_g"][0], lru_conv_w=g["lru_conv_w"], lru_conv_b=g["lru_conv_b"][0],
        lru_wa=_unblockdiag(cfg, g["wa"]), lru_wx=_unblockdiag(cfg, g["wx"]), lru_ba=g["lru_ba"], lru_bx=g["lru_bx"],
        lru_lambda=g["lru_lambda"], lru_norm_g=g["lru_norm_g"][0], norm2_g=g["norm2_g"][0], ffn_conv_w=g["ffn_conv_w"],
        ffn_conv_b=g["ffn_conv_b"][0], ple_norm_g=g["ple_norm_g"][0], ple_bg=g["ple_bg"][0],
    )


def _local_step(cfg, get_big, small, x, p, target, on_big_grads):
    r = x.reshape(cfg.T, cfg.D)
    ops, saved = [], []
    for i in range(cfg.L):
        big, token = get_big(i, r)
        w = _layer_operands(cfg, big, small, i)
        if token is not None:
            w["norm1_g"] = w["norm1_g"] + token[0, 0]
        r, sv = _layer_fwd(cfg, w, r, p[i].reshape(cfg.T, cfg.PD))
        ops.append(w)
        saved.append(sv)
    dr, loss, dgf = _k_loss(cfg, r, target.reshape(cfg.T, cfg.D), small["final_g"][None])
    gsmall = [None] * cfg.L
    for i in reversed(range(cfg.L)):
        first = EARLY if i == 0 else ()
        early = (lambda g: on_big_grads("%da" % i, {n: g[BIG_OPERAND[n]] for n in first})) if first else None
        dr, g = _layer_bwd(cfg, ops[i], saved[i], dr, early)
        token = on_big_grads("%d" % i, {n: (_w_in_t_from_wcat_t(cfg, g["wcat_t"]) if n == "w_in" else g[BIG_OPERAND[n]])
                                        for n in BIG if n not in first})
        if i > 0:
            ops[i - 1]["ple_bg"] = ops[i - 1]["ple_bg"] + token[0, 0]
        gsmall[i] = _small_grads_to_problem(cfg, g)
    gs = {k: jnp.stack([gl[k] for gl in gsmall]) for k in gsmall[0]}
    gs["final_g"] = dgf[0]
    return loss, dr, gs


def _row_tile(rows, limit=512):
    best = rows
    for t in range(8, min(rows, limit) + 1, 8):
        if rows % t == 0:
            best = t
    return best if best <= limit or rows <= limit else rows


def _adam_group(name, ws, gs, ms, vs, cols, tr):
    shapes = [w.shape for w in ws]
    pk = lambda arrs: _pack(arrs, cols, tr, f32)
    w2 = pk(ws)
    d, nm, nv = _adam_call(name, w2, pk(gs), pk(ms), pk(vs), min(tr, w2.shape[0]))
    return [_unpack(a.reshape(-1), shapes) for a in (d, nm, nv)]


def kernel(x, p, norm1_g, w_in, dn_conv_w, dn_a_log, dn_dt_bias, dn_norm_g, lru_conv_w, lru_conv_b, lru_wa, lru_ba, lru_wx, lru_bx, lru_lambda, lru_norm_g, w_out, norm2_g, ffn_wg, ffn_wu, ffn_conv_w, ffn_conv_b, ffn_wd, ple_norm_g, ple_wg, ple_bg, ple_wp, final_g, loss_target, m_norm1_g, m_w_in, m_dn_conv_w, m_dn_a_log, m_dn_dt_bias, m_dn_norm_g, m_lru_conv_w, m_lru_conv_b, m_lru_wa, m_lru_ba, m_lru_wx, m_lru_bx, m_lru_lambda, m_lru_norm_g, m_w_out, m_norm2_g, m_ffn_wg, m_ffn_wu, m_ffn_conv_w, m_ffn_conv_b, m_ffn_wd, m_ple_norm_g, m_ple_wg, m_ple_bg, m_ple_wp, m_final_g, v_norm1_g, v_w_in, v_dn_conv_w, v_dn_a_log, v_dn_dt_bias, v_dn_norm_g, v_lru_conv_w, v_lru_conv_b, v_lru_wa, v_lru_ba, v_lru_wx, v_lru_bx, v_lru_lambda, v_lru_norm_g, v_w_out, v_norm2_g, v_ffn_wg, v_ffn_wu, v_ffn_conv_w, v_ffn_conv_b, v_ffn_wd, v_ple_norm_g, v_ple_wg, v_ple_bg, v_ple_wp, v_final_g):
    cfg = CFG
    a = dict(locals())
    wl = {n: a[n] for n in WEIGHTS}
    ml = {n: a["m_" + n] for n in WEIGHTS}
    vl = {n: a["v_" + n] for n in WEIGHTS}
    me = 4 * lax.axis_index("x") + 2 * lax.axis_index("y") + lax.axis_index("c")
    nl = cfg.L

    blocks = [(jnp.swapaxes(wl[n], 1, 2) if BIG_T[n] else wl[n]).astype(bf16) for n in BIG]
    ss_shapes = [wl[n].shape for n in SMALL_SHARDED]
    *first, s8 = _all_gather("gather_weights_0", [blk[0] for blk in blocks] + [_pack([wl[n] for n in SMALL_SHARDED], LANES, 8, f32)])
    small = dict(zip(SMALL_SHARDED, _unpack8(s8.reshape(N_DEV, -1), ss_shapes, [2] * len(ss_shapes))))
    small.update({n: wl[n] for n in SMALL_REPL})
    gathers = {}

    def start_gather(i, behind):
        shards, _ = lax.optimization_barrier(([blk[i] for blk in blocks], behind))
        gathers[i] = _exchange_start("gather_start_%d" % i, shards, gather=True)
        return gathers[i][3]

    def get_big(i, r):
        if i == 0:
            full = first
        else:
            send_sem, recv_sem, thru, _ = gathers[i]
            lands = _exchange_wait("gather_wait_%d" % i, send_sem, recv_sem, thru, r, gather=True)
            full = [lax.dynamic_update_slice_in_dim(land, blk[i][None], me, 0) for land, blk in zip(lands, blocks)]
        token = start_gather(i + 1, full[0]) if i + 1 < nl else None
        return {n: f.reshape(N_DEV * f.shape[1], f.shape[2]) for n, f in zip(BIG, full)}, token

    pending = {}

    def on_big_grads(key, g):
        send = [g[n].reshape((N_DEV,) + blk.shape[1:]).astype(bf16) for n, blk in zip(BIG, blocks) if n in g]
        own = [lax.dynamic_index_in_dim(sd, me, 0, keepdims=True) for sd in send]
        send_sem, recv_sem, thru, token = _exchange_start("exchange_start_" + key, send)
        pending[key] = (send_sem, recv_sem, thru, own, [n for n in BIG if n in g])
        return token

    loss_part, dr, gsmall = _local_step(cfg, get_big, small, x, p, loss_target, on_big_grads)
    grad_x = dr.reshape(x.shape)

    sums = {n: [None] * nl for n in BIG}
    for key in sorted(pending):
        send_sem, recv_sem, thru, own, names = pending[key]
        lands = _exchange_wait("exchange_wait_" + key, send_sem, recv_sem, thru, dr)
        for n, land, o in zip(names, lands, own):
            slots = lax.dynamic_update_slice_in_dim(land, o, me, 0)
            sums[n][int(key[0])] = _sum8_call("sum_%s_%s" % (n, key), slots, slots.shape[1])
    gl = {}
    for n in BIG:
        s = jnp.stack(sums[n])
        gl[n] = jnp.swapaxes(s, 1, 2) if BIG_T[n] else s

    small_names = SMALL_REPL + SMALL_SHARDED
    small_shapes = [gsmall[n].shape for n in small_names]
    sv = _pack([gsmall[n] for n in small_names] + [loss_part[0, 0:1]], LANES, 512, f32)
    small_sum = _sum8_call("sum_small", _all_gather("gather_small_grads", [sv])[0], 512).reshape(-1)
    gl.update(zip(small_names, _unpack(small_sum, small_shapes)))
    loss = small_sum[sum(math.prod(s) for s in small_shapes)]
    for n in SMALL_SHARDED:
        shard = wl[n].shape[2]
        gl[n] = lax.dynamic_slice_in_dim(gl[n], me * shard, shard, axis=2)

    outs = {}
    for n in BIG:
        shp = wl[n].shape
        two = lambda t: t.reshape(-1, shp[-1])
        d, nm, nv = _adam_call("adam_" + n, two(wl[n]), two(gl[n]), two(ml[n]), two(vl[n]), _row_tile(math.prod(shp[:-1])))
        outs[n] = (d.reshape(shp), nm.reshape(shp), nv.reshape(shp))
    d, nm, nv = _adam_group("adam_small", [wl[n] for n in small_names], [gl[n] for n in small_names],
                            [ml[n] for n in small_names], [vl[n] for n in small_names], LANES, 64)
    for j, n in enumerate(small_names):
        outs[n] = (d[j], nm[j], nv[j])
    return (loss, grad_x, *[gl[n] for n in WEIGHTS], *[outs[n][0] for n in WEIGHTS], *[outs[n][1] for n in WEIGHTS],
            *[outs[n][2] for n in WEIGHTS])
```

```python
import functools
import math

import jax
import jax.numpy as jnp
from jax import lax
from jax.experimental import pallas as pl
from jax.experimental.pallas import tpu as pltpu

f32 = jnp.float32
bf16 = jnp.bfloat16
MESH = pl.DeviceIdType.MESH

N_DEV = 8
LANES = 128
EPS = 1e-6
LRU_C = 8.0
ADAM_LR, ADAM_B1, ADAM_B2, ADAM_EPS, ADAM_WD, ADAM_STEP = 0.001, 0.9, 0.999, 1e-08, 0.01, 10
VMEM_LIMIT = 56 * 1024 * 1024


class Cfg:
    def __init__(self, d_model=1024, bl=4, seq=2048, depth=4, heads=4, lru_width=512, d_ff=2816, ple=256,
                 tm=512, tm_ffn=256, ff_tile=256):
        self.D, self.BL, self.S, self.L, self.H = d_model, bl, seq, depth, heads
        self.DH = 128
        self.DN = heads * self.DH
        self.LW = lru_width
        self.NB = lru_width // 64
        self.FF, self.PD = d_ff, ple
        self.C = 64
        self.NC = seq // self.C
        self.T = bl * seq
        self.TM = min(tm, self.T)
        self.TMF = min(tm_ffn, self.T)
        self.FT = ff_tile
        self.ZO = 3 * self.DN
        self.LXO = 4 * self.DN
        self.LGO = self.LXO + self.LW
        self.BAO = self.LGO + self.LW
        self.PC = self.BAO + LANES
        self.IN_COLS = 4 * self.DN + 4 * heads + 2 * self.LW


CFG = Cfg()


def _mm(a, b):
    return jnp.dot(a.astype(bf16), b.astype(bf16), preferred_element_type=f32)


def _mm_nt(a, b):
    return lax.dot_general(a.astype(bf16), b.astype(bf16), (((1,), (1,)), ((), ())), preferred_element_type=f32)


def _mm_tn(a, b):
    return lax.dot_general(a.astype(bf16), b.astype(bf16), (((0,), (0,)), ((), ())), preferred_element_type=f32)


def _split2(a):
    hi = a.astype(bf16)
    return hi, (a - hi.astype(f32)).astype(bf16)


def _hmm3(a, b, dims=(((1,), (0,)), ((), ()))):
    ah, al = _split2(a)
    bh, bl = _split2(b)
    dot = functools.partial(lax.dot_general, dimension_numbers=dims, preferred_element_type=f32)
    return dot(ah, bh) + dot(ah, bl) + dot(al, bh)


def _hmm3_tn(a, b):
    return _hmm3(a, b, (((0,), (0,)), ((), ())))


def _cum_mm(mask, x, dims=(((1,), (0,)), ((), ()))):
    m = mask.astype(bf16)
    x1 = x.astype(bf16)
    r = x - x1.astype(f32)
    x2 = r.astype(bf16)
    x3 = (r - x2.astype(f32)).astype(bf16)
    dot = functools.partial(lax.dot_general, dimension_numbers=dims, preferred_element_type=f32)
    return dot(m, x1) + dot(m, x2) + dot(m, x3)


def _rms_fwd(x, g):
    inv = lax.rsqrt(jnp.mean(x * x, axis=-1, keepdims=True) + EPS)
    xh = x * inv
    return xh * g, xh, inv


def _rms_bwd(dy, xh, inv, g):
    dxh = dy * g
    dx = inv * (dxh - xh * jnp.mean(dxh * xh, axis=-1, keepdims=True))
    dg = jnp.sum(dy * xh, axis=0, keepdims=True)
    return dx, dg


def _sigmoid(x):
    return 1.0 / (1.0 + jnp.exp(-x))


def _softplus(x):
    return jnp.maximum(x, 0.0) + jnp.log(1.0 + jnp.exp(-jnp.abs(x)))


def _silu(x):
    s = _sigmoid(x)
    return x * s, s * (1.0 + x * (1.0 - s))


_GC = math.sqrt(2.0 / math.pi)


def _gelu(x):
    t = jnp.tanh(_GC * (x + 0.044715 * x * x * x))
    y = 0.5 * x * (1.0 + t)
    dy = 0.5 * (1.0 + t) + 0.5 * x * (1.0 - t * t) * _GC * (1.0 + 3.0 * 0.044715 * x * x)
    return y, dy


def _nexpm1(x):
    ser = -x * (1.0 + x * 0.5 * (1.0 + x * (1.0 / 3.0) * (1.0 + x * 0.25 * (1.0 + x * 0.2))))
    return jnp.where(x > -0.1, ser, 1.0 - jnp.exp(x))


def _shift(x, s, fill=0.0):
    if s == 0:
        return x
    n = x.shape[0]
    t = lax.broadcasted_iota(jnp.int32, x.shape, 0)
    r = pltpu.roll(x, (-s) % n, 0)
    return jnp.where((t + s >= 0) & (t + s < n), r, fill)


def _conv_fwd(x, w_ref, left):
    k = w_ref.shape[0]
    out = _shift(x, -left) * w_ref[0:1, :]
    for j in range(1, k):
        out = out + _shift(x, j - left) * w_ref[j:j + 1, :]
    return out


def _conv_bwd(dout, x, w_ref, left):
    k = w_ref.shape[0]
    dx = None
    dws = []
    for j in range(k):
        term = _shift(dout, -(j - left)) * w_ref[j:j + 1, :]
        dx = term if dx is None else dx + term
        dws.append(jnp.sum(dout * _shift(x, j - left), axis=0, keepdims=True))
    return dx, dws


def _scan_refs(scans, n):
    blk = 64
    nb = n // blk
    sub = lax.broadcasted_iota(jnp.int32, (blk, LANES), 0) & 7

    def local(a, b, rev):
        for d in (1, 2, 4):
            ok = (sub < 8 - d) if rev else (sub >= d)
            sh = (blk - d) if rev else d
            b = a * jnp.where(ok, pltpu.roll(b, sh, 0), 0.0) + b
            a = a * jnp.where(ok, pltpu.roll(a, sh, 0), 1.0)
        return a, b

    def body(i, carries):
        new = []
        for (a_ref, b_ref, h_ref, rev), carry in zip(scans, carries):
            j = (nb - 1 - i) if rev else i
            base = pl.multiple_of(j * blk, blk)
            a, b = local(a_ref[pl.ds(base, blk), :], b_ref[pl.ds(base, blk), :], rev)
            order = range(blk // 8 - 1, -1, -1) if rev else range(blk // 8)
            for v in order:
                h = b[8 * v:8 * v + 8, :] + a[8 * v:8 * v + 8, :] * carry
                h_ref[pl.ds(base + 8 * v, 8), :] = h
                carry = h[0:1, :] if rev else h[7:8, :]
            new.append(carry)
        return tuple(new)

    lax.fori_loop(0, nb, body, tuple(jnp.zeros((1, LANES), f32) for _ in scans))


def _tri_masks(c, rev):
    i = lax.broadcasted_iota(jnp.int32, (c, c), 0)
    j = lax.broadcasted_iota(jnp.int32, (c, c), 1)
    incl = (i <= j) if rev else (i >= j)
    strict = (i < j) if rev else (i > j)
    incl_t = (i >= j) if rev else (i <= j)
    return incl, strict, incl_t


def _map(f, *lists):
    return [f(*a) for a in zip(*lists)]


def _tri_inv(mats, tick=lambda: None):
    c = mats[0].shape[0]
    i = lax.broadcasted_iota(jnp.int32, (c, c), 0)
    j = lax.broadcasted_iota(jnp.int32, (c, c), 1)
    eye = jnp.where(i == j, 1.0, 0.0)
    t = [eye - a for a in mats]
    pw = _map(_hmm3, mats, mats)
    for it in range(5):
        tick()
        t = _map(lambda ti, ui: ti + ui, t, _map(_hmm3, t, pw))
        if it < 4:
            pw = _map(_hmm3, pw, pw)
    return t


def _gdn_decay(gs, revs):
    c = gs[0].shape[0]
    masks = [_tri_masks(c, r) for r in revs]
    gb = [jnp.broadcast_to(g, (c, c)) for g in gs]
    mcol = _map(lambda m, x: _cum_mm(m[0], x), masks, gb)
    mrow = _map(lambda m, x: jnp.sum(jnp.where(m[2], x, 0.0), axis=0, keepdims=True), masks, gb)
    dec = _map(lambda m, a, b: jnp.exp(jnp.where(m[0], a - b, -1e30)), masks, mcol, mrow)
    return [m[:, 0:1] for m in mcol], [jnp.sum(g, axis=0, keepdims=True) for g in gs], dec


def _gdn_prep(qs, k, v, g, beta, kk, qk, revs, tick=lambda: None):
    c = k[0].shape[0]
    masks = [_tri_masks(c, r) for r in revs]
    gcum, glast, dec = _gdn_decay(g, revs)
    e = [jnp.exp(x) for x in gcum]
    tick()
    tm = _tri_inv(_map(lambda m, b, x, d: jnp.where(m[1], b * x * d, 0.0), masks, beta, kk, dec), tick)
    tick()
    u = _map(lambda t, vi, b: _hmm3(t, vi * b), tm, v, beta)
    tick()
    w = _map(lambda t, ki, b, ei: _hmm3(t, ki * b * ei), tm, k, beta, e)
    p = _map(lambda m, x, d: jnp.where(m[0], x * d, 0.0), masks, qk, dec)
    return dict(dec=dec, cd=[jnp.exp(x) for x in glast], tm=tm, u=u, w=w, p=p, qd=_map(lambda a, b: a * b, qs, e),
                kd=_map(lambda ki, gl, gc: ki * jnp.exp(gl - gc), k, glast, gcum))


def _lane_pick(x, lane):
    l = lax.broadcasted_iota(jnp.int32, x.shape, 1)
    return jnp.sum(jnp.where(l == lane, x, 0.0), axis=1, keepdims=True)


def _params(sem, vmem=VMEM_LIMIT):
    return pltpu.CompilerParams(dimension_semantics=sem, vmem_limit_bytes=vmem)


def _row_call(name, body, t, tm, row_ins, full_ins, row_outs, acc_outs):
    in_specs = [pl.BlockSpec((tm, w), functools.partial(lambda i, c: (i, c), c=c)) for (_, w, c) in row_ins]
    for a in full_ins:
        in_specs.append(pl.BlockSpec(a.shape, functools.partial(lambda i, n: (0,) * n, n=a.ndim)))
    out_specs = [pl.BlockSpec((tm, w), lambda i: (i, 0)) for (w, _) in row_outs]
    out_shape = [jax.ShapeDtypeStruct((t, w), dt) for (w, dt) in row_outs]
    for shp, dt in acc_outs:
        out_specs.append(pl.BlockSpec(shp, functools.partial(lambda i, n: (0,) * n, n=len(shp))))
        out_shape.append(jax.ShapeDtypeStruct(shp, dt))
    return pl.pallas_call(
        body, name=name, grid=(t // tm,), in_specs=in_specs, out_specs=out_specs, out_shape=out_shape,
        compiler_params=_params(("arbitrary",)),
    )(*[a for (a, _, _) in row_ins], *full_ins)


def _k_in(cfg, r, g1, wcat):
    def body(r_ref, g_ref, w_ref, h_ref, proj_ref):
        y, _, _ = _rms_fwd(r_ref[...], g_ref[...])
        hb = y.astype(bf16)
        h_ref[...] = hb
        proj_ref[...] = _mm_nt(hb, w_ref[...])

    return _row_call("in_proj", body, cfg.T, cfg.TM, [(r, cfg.D, 0)], [g1, wcat],
                     [(cfg.D, bf16), (cfg.PC, f32)], [])


def _k_prep(cfg, proj, conv_w):
    dn, s = cfg.DN, cfg.S

    def body(x_ref, w_ref, o_ref):
        sec = pl.program_id(1)
        c = _conv_fwd(x_ref[...], w_ref, 2)
        y, _ = _silu(c)
        for h in range(cfg.H):
            yh = y[:, h * 128:(h + 1) * 128]
            nh = yh * lax.rsqrt(jnp.sum(yh * yh, axis=1, keepdims=True) + EPS)
            o_ref[:, h * 128:(h + 1) * 128] = jnp.where(sec < 2, nh, yh)

    return pl.pallas_call(
        body, name="dn_prep", grid=(cfg.BL, 3),
        in_specs=[pl.BlockSpec((s, dn), lambda b, j: (b, j)), pl.BlockSpec((4, dn), lambda b, j: (0, j))],
        out_specs=pl.BlockSpec((s, dn), lambda b, j: (b, j)),
        out_shape=jax.ShapeDtypeStruct((cfg.T, 3 * dn), f32),
        compiler_params=_params(("arbitrary", "arbitrary")),
    )(proj, conv_w)


def _gate_cols(ba, alog_row, dt_row, lane_b, lane_a):
    beta = _sigmoid(_lane_pick(ba, lane_b))
    alpha = _lane_pick(ba, lane_a)
    aexp = jnp.exp(_lane_pick(alog_row, lane_a))
    dtb = _lane_pick(dt_row, lane_a)
    xa = alpha + dtb
    g = -aexp * _softplus(xa)
    return beta, g, aexp, xa


def _gdn_specs(cfg):
    s, c, nc = cfg.S, cfg.C, cfg.NC
    tok2 = pl.BlockSpec((2, s, 128), lambda b, h: (0, b, h))
    mat2 = pl.BlockSpec((2, 1, nc, c, c), lambda b, h: (0, b, h, 0, 0))
    cd2 = pl.BlockSpec((2, 1, nc, 8, LANES), lambda b, h: (0, b, h, 0, 0))
    shapes = dict(
        tok32=jax.ShapeDtypeStruct((2, cfg.T, cfg.DN), f32), tok16=jax.ShapeDtypeStruct((2, cfg.T, cfg.DN), bf16),
        mat32=jax.ShapeDtypeStruct((2, cfg.BL, cfg.H * nc, c, c), f32), mat16=jax.ShapeDtypeStruct((2, cfg.BL, cfg.H * nc, c, c), bf16),
        cd=jax.ShapeDtypeStruct((2, cfg.BL, cfg.H * nc, 8, LANES), f32))
    return tok2, mat2, cd2, shapes


def _k_gdn_fwd(cfg, qkv, proj, alog_row, dt_row):
    s, c, nc, hh = cfg.S, cfg.C, cfg.NC, cfg.H
    ba_blk = cfg.BAO // LANES
    per = 4 if nc % 4 == 0 else 1

    def body(q_ref, k_ref, v_ref, ba_ref, al_ref, dt_ref, o_ref, u_o, w_o, qd_o, kd_o, p_o, t_o, dec_o, cd_o):
        h = pl.program_id(1)
        groups = nc // per

        def chunk_of(d, g, j):
            return (nc - 1 - (g * per + j)) if d == 1 else (g * per + j)

        def prep(g, tick=lambda: None):
            chains, qs, k, v, kk, qk, gt, beta = [], [], [], [], [], [], [], []
            for j in range(per):
                for d in range(2):
                    n = chunk_of(d, g, j)
                    rows = pl.ds(pl.multiple_of(n * c, c), c)
                    kj, vj = k_ref[rows, :], v_ref[rows, :]
                    qj = q_ref[rows, :] * (128 ** -0.5)
                    bd, gd, _, _ = _gate_cols(ba_ref[rows, :], al_ref[...], dt_ref[...], d * hh + h, 2 * hh + d * hh + h)
                    chains.append((d, n, rows))
                    for lst, val in ((qs, qj), (k, kj), (v, vj), (kk, _mm_nt(kj, kj)), (qk, _mm_nt(qj, kj)), (gt, gd), (beta, bd)):
                        lst.append(val)
            z = _gdn_prep(qs, k, v, gt, beta, kk, qk, [d == 1 for d, _, _ in chains], tick)
            for x, (d, n, rows) in enumerate(chains):
                u_o[d, rows, :] = z["u"][x]
                w_o[d, rows, :] = z["w"][x].astype(bf16)
                qd_o[d, rows, :] = z["qd"][x].astype(bf16)
                kd_o[d, rows, :] = z["kd"][x].astype(bf16)
                p_o[d, 0, n] = z["p"][x].astype(bf16)
                t_o[d, 0, n] = z["tm"][x]
                dec_o[d, 0, n] = z["dec"][x]
                cd_o[d, 0, n] = jnp.broadcast_to(z["cd"][x], (8, LANES))

        def steps(g, box):
            for j in range(per):
                sts = box[0]
                ns = [chunk_of(d, g, j) for d in range(2)]
                rows = [pl.ds(pl.multiple_of(n * c, c), c) for n in ns]
                ws = [_mm(w_o[d, rows[d], :], sts[d]) for d in range(2)]
                qs_ = [_mm(qd_o[d, rows[d], :], sts[d]) for d in range(2)]
                yield
                vn = [u_o[d, rows[d], :] - ws[d] for d in range(2)]
                box[0] = tuple(sts[d] * cd_o[d, 0, ns[d]][0:1, :] + _mm_tn(kd_o[d, rows[d], :], vn[d]) for d in range(2))
                for d in range(2):
                    o_ref[rows[d], :] = o_ref[rows[d], :] + qs_[d] + _mm(p_o[d, 0, ns[d]], vn[d])
                yield

        o_ref[...] = jnp.zeros_like(o_ref)
        prep(0)

        def fused(g, sts):
            box = [sts]
            chain = steps(g - 1, box)
            prep(g, lambda: next(chain, None))
            for _ in chain:
                pass
            return box[0]

        z0 = jnp.zeros((128, 128), f32)
        box = [lax.fori_loop(1, groups, fused, (z0, z0))]
        for _ in steps(groups - 1, box):
            pass

    blk = lambda off: pl.BlockSpec((s, 128), functools.partial(lambda b, h, off: (b, off + h), off=off))
    row = pl.BlockSpec((1, LANES), lambda b, h: (0, 0))
    tok2, mat2, cd2, shp = _gdn_specs(cfg)
    return pl.pallas_call(
        body, name="gdn_fwd", grid=(cfg.BL, hh),
        in_specs=[blk(0), blk(hh), blk(2 * hh), pl.BlockSpec((s, LANES), lambda b, h: (b, ba_blk)), row, row],
        out_specs=[pl.BlockSpec((s, 128), lambda b, h: (b, h)), tok2, tok2, tok2, tok2, mat2, mat2, mat2, cd2],
        out_shape=[jax.ShapeDtypeStruct((cfg.T, cfg.DN), f32), shp["tok32"], shp["tok16"], shp["tok16"], shp["tok16"],
                   shp["mat16"], shp["mat32"], shp["mat32"], shp["cd"]],
        compiler_params=_params(("arbitrary", "arbitrary")),
    )(qkv, qkv, qkv, proj, alog_row, dt_row)


def _lru_gates(xc, wa, wx, ba, bx, lam):
    ra = _sigmoid(_mm(xc, wa) + ba)
    ig = _sigmoid(_mm(xc, wx) + bx)
    sp = _softplus(-lam)
    la = -LRU_C * ra * sp
    a = jnp.exp(la)
    m = jnp.sqrt(_nexpm1(2.0 * la))
    return dict(ra=ra, ig=ig, sp=sp, a=a, m=m, gx=ig * xc)


def _lru_specs(cfg, outer_b):
    s = cfg.S
    lx_blk = cfg.LXO // LANES
    if outer_b:
        ix = lambda f: (lambda b, ct: f(b, ct))
    else:
        ix = lambda f: (lambda ct, b: f(b, ct))
    return dict(
        lx=pl.BlockSpec((s, LANES), ix(lambda b, ct: (b, lx_blk + ct))),
        tok=pl.BlockSpec((s, LANES), ix(lambda b, ct: (b, ct))),
        cw=pl.BlockSpec((4, LANES), ix(lambda b, ct: (0, ct))),
        row=pl.BlockSpec((1, LANES), ix(lambda b, ct: (0, ct))),
        w=pl.BlockSpec((2, 1, LANES, LANES), ix(lambda b, ct: (0, ct, 0, 0))),
        two=pl.BlockSpec((2, LANES), ix(lambda b, ct: (0, ct))),
    )


def _k_lru_fwd(cfg, proj, conv_w, conv_b, wa, wx, ba, bx, lam):
    def body(lx_ref, cw_ref, cb_ref, wa_ref, wx_ref, ba_ref, bx_ref, lam_ref, o_ref, a_s, b_s, h_s):
        xc = _conv_fwd(lx_ref[...], cw_ref, 2) + cb_ref[...]
        for d in range(2):
            z = _lru_gates(xc, wa_ref[d, 0], wx_ref[d, 0], ba_ref[d:d + 1, :], bx_ref[d:d + 1, :], lam_ref[d:d + 1, :])
            a_s[d] = z["a"]
            b_s[d] = z["m"] * z["gx"]
        _scan_refs([(a_s.at[d], b_s.at[d], h_s.at[d], d == 1) for d in range(2)], cfg.S)
        o_ref[...] = h_s[0] + h_s[1]

    sp = _lru_specs(cfg, True)
    return pl.pallas_call(
        body, name="lru_fwd", grid=(cfg.BL, cfg.LW // LANES),
        in_specs=[sp["lx"], sp["cw"], sp["row"], sp["w"], sp["w"], sp["two"], sp["two"], sp["two"]],
        out_specs=sp["tok"], out_shape=jax.ShapeDtypeStruct((cfg.T, cfg.LW), f32),
        scratch_shapes=[pltpu.VMEM((2, cfg.S, LANES), f32)] * 3,
        compiler_params=_params(("arbitrary", "arbitrary")),
    )(proj, conv_w, conv_b, wa, wx, ba, bx, lam)


def _mix_parts(cfg, o, z, lg, hs, dng, lrg):
    heads = []
    for h in range(cfg.H):
        sl = slice(h * 128, (h + 1) * 128)
        y, xh, inv = _rms_fwd(o[:, sl], dng)
        sz, dsz = _silu(z[:, sl])
        heads.append((y, xh, inv, sz, dsz))
    gl, dgl = _gelu(lg)
    y2, xh2, inv2 = _rms_fwd(gl * hs, lrg)
    return heads, (y2, xh2, inv2, gl, dgl)


def _k_mix(cfg, o, proj, hs, r, dng, lrg, wout):
    dn = cfg.DN

    def body(o_ref, z_ref, lg_ref, hs_ref, r_ref, dng_ref, lrg_ref, w_ref, mix_ref, out_ref):
        heads, lru = _mix_parts(cfg, o_ref[...], z_ref[...], lg_ref[...], hs_ref[...], dng_ref[...], lrg_ref[...])
        for h, (y, _, _, sz, _) in enumerate(heads):
            mix_ref[:, h * 128:(h + 1) * 128] = (y * sz).astype(bf16)
        mix_ref[:, dn:] = lru[0].astype(bf16)
        out_ref[...] = r_ref[...] + jnp.dot(mix_ref[...], w_ref[...], preferred_element_type=f32)

    return _row_call("mix_out", body, cfg.T, cfg.TM,
                     [(o, dn, 0), (proj, dn, cfg.ZO // dn), (proj, cfg.LW, cfg.LGO // cfg.LW), (hs, cfg.LW, 0), (r, cfg.D, 0)],
                     [dng, lrg, wout], [(cfg.D, bf16), (cfg.D, f32)], [])


def _k_ffn_a(cfg, r, g2, wg, wu):
    def body(r_ref, g_ref, wg_ref, wu_ref, h_ref, gp_ref, up_ref):
        y, _, _ = _rms_fwd(r_ref[...], g_ref[...])
        hb = y.astype(bf16)
        h_ref[...] = hb
        gp_ref[...] = _mm_nt(hb, wg_ref[...])
        up_ref[...] = _mm_nt(hb, wu_ref[...])

    return _row_call("ffn_in", body, cfg.T, cfg.TMF, [(r, cfg.D, 0)], [g2, wg, wu],
                     [(cfg.D, bf16), (cfg.FF, f32), (cfg.FF, f32)], [])


def _k_ffn_b(cfg, gp, up, conv_w, conv_b):
    s, ft = cfg.S, cfg.FT

    def body(gp_ref, up_ref, w_ref, b_ref, o_ref):
        gate = _conv_fwd(gp_ref[...], w_ref, 1) + b_ref[...]
        gl, _ = _gelu(gate)
        o_ref[...] = (gl * up_ref[...]).astype(bf16)

    tok = pl.BlockSpec((s, ft), lambda b, j: (b, j))
    return pl.pallas_call(
        body, name="ffn_act", grid=(cfg.BL, cfg.FF // ft),
        in_specs=[tok, tok, pl.BlockSpec((3, ft), lambda b, j: (0, j)), pl.BlockSpec((1, ft), lambda b, j: (0, j))],
        out_specs=tok, out_shape=jax.ShapeDtypeStruct((cfg.T, cfg.FF), bf16),
        compiler_params=_params(("arbitrary", "arbitrary")),
    )(gp, up, conv_w, conv_b)


def _k_ffn_c(cfg, act, r, wd):
    def body(a_ref, r_ref, w_ref, o_ref):
        o_ref[...] = r_ref[...] + jnp.dot(a_ref[...], w_ref[...], preferred_element_type=f32)

    return _row_call("ffn_out", body, cfg.T, cfg.TM, [(act, cfg.FF, 0), (r, cfg.D, 0)], [wd], [(cfg.D, f32)], [])[0]


def _k_ple(cfg, r, p, gp, wpg, bg, wpp):
    def body(r_ref, p_ref, g_ref, wg_ref, bg_ref, wp_ref, pn_ref, o_ref):
        x = r_ref[...]
        y, _, _ = _rms_fwd(x, g_ref[...])
        pn = y.astype(bf16)
        pn_ref[...] = pn
        pg = _sigmoid(jnp.dot(pn, wg_ref[...], preferred_element_type=f32) + bg_ref[...])
        o_ref[...] = x + pg * _mm_nt(p_ref[...], wp_ref[...])

    return _row_call("ple", body, cfg.T, cfg.TM, [(r, cfg.D, 0), (p, cfg.PD, 0)], [gp, wpg, bg, wpp],
                     [(cfg.D, bf16), (cfg.D, f32)], [])


def _k_loss(cfg, r, tgt, gf):
    d = cfg.D

    def body(r_ref, t_ref, g_ref, dr_ref, loss_ref, dg_ref):
        @pl.when(pl.program_id(0) == 0)
        def _():
            loss_ref[...] = jnp.zeros_like(loss_ref)
            dg_ref[...] = jnp.zeros_like(dg_ref)

        g = g_ref[...]
        y, xh, inv = _rms_fwd(r_ref[...], g)
        err = y - t_ref[...]
        loss_ref[...] = loss_ref[...] + (0.5 / d) * jnp.sum(err * err)
        dx, dg = _rms_bwd(err * (1.0 / d), xh, inv, g)
        dr_ref[...] = dx
        dg_ref[...] = dg_ref[...] + dg

    return _row_call("loss_head", body, cfg.T, cfg.TM, [(r, d, 0), (tgt, d, 0)], [gf], [(d, f32)],
                     [((1, LANES), f32), ((1, d), f32)])


def _zero_at_first(cond, *refs):
    @pl.when(cond)
    def _():
        for r in refs:
            r[...] = jnp.zeros_like(r)


def _b_ple(cfg, dr3, r2, p, gp, wpg, bg, wpp):
    d = cfg.D

    def body(dr_ref, r_ref, p_ref, g_ref, wg_ref, bg_ref, wp_ref, dr2_ref, dr2b_ref, dlog_ref, dpp_ref, dgp_ref, dbg_ref):
        _zero_at_first(pl.program_id(0) == 0, dgp_ref, dbg_ref)
        g = g_ref[...]
        dr = dr_ref[...]
        y, xh, inv = _rms_fwd(r_ref[...], g)
        pg = _sigmoid(jnp.dot(y.astype(bf16), wg_ref[...], preferred_element_type=f32) + bg_ref[...])
        pp = _mm_nt(p_ref[...], wp_ref[...])
        dpp_ref[...] = (dr * pg).astype(bf16)
        dlog = dr * pp * pg * (1.0 - pg)
        dlog_ref[...] = dlog.astype(bf16)
        dbg_ref[...] = dbg_ref[...] + jnp.sum(dlog, axis=0, keepdims=True)
        dx, dg = _rms_bwd(_mm_nt(dlog, wg_ref[...]), xh, inv, g)
        dgp_ref[...] = dgp_ref[...] + dg
        dr2_ref[...] = dr + dx
        dr2b_ref[...] = (dr + dx).astype(bf16)

    return _row_call("ple_bwd", body, cfg.T, cfg.TM, [(dr3, d, 0), (r2, d, 0), (p, cfg.PD, 0)], [gp, wpg, bg, wpp],
                     [(d, f32), (d, bf16), (d, bf16), (d, bf16)], [((1, d), f32), ((1, d), f32)])


def _b_ffn_bc(cfg, dr2, wd, gp, up, conv_w, conv_b):
    s, ft, d = cfg.S, cfg.FT, cfg.D

    def body(dr_ref, wd_ref, gp_ref, up_ref, w_ref, b_ref, dgp_ref, dup_ref, dwd_ref, dcw_ref, dcb_ref):
        _zero_at_first(pl.program_id(1) == 0, dwd_ref, dcw_ref, dcb_ref)
        drb = dr_ref[...]
        dact = _mm_nt(drb, wd_ref[...])
        gpre = gp_ref[...]
        up = up_ref[...]
        gl, dgl = _gelu(_conv_fwd(gpre, w_ref, 1) + b_ref[...])
        dup_ref[...] = (dact * gl).astype(bf16)
        dgate = dact * up * dgl
        dcb_ref[...] = dcb_ref[...] + jnp.sum(dgate, axis=0, keepdims=True)
        dx, dws = _conv_bwd(dgate, gpre, w_ref, 1)
        for j, dw in enumerate(dws):
            dcw_ref[j:j + 1, :] = dcw_ref[j:j + 1, :] + dw
        dgp_ref[...] = dx.astype(bf16)
        dwd_ref[...] = dwd_ref[...] + _mm_tn((gl * up).astype(bf16), drb)

    tok = pl.BlockSpec((s, ft), lambda j, b: (b, j))
    return pl.pallas_call(
        body, name="ffn_act_bwd", grid=(cfg.FF // ft, cfg.BL),
        in_specs=[pl.BlockSpec((s, d), lambda j, b: (b, 0)), pl.BlockSpec((ft, d), lambda j, b: (j, 0)), tok, tok,
                  pl.BlockSpec((3, ft), lambda j, b: (0, j)), pl.BlockSpec((1, ft), lambda j, b: (0, j))],
        out_specs=[tok, tok, pl.BlockSpec((ft, d), lambda j, b: (j, 0)), pl.BlockSpec((3, ft), lambda j, b: (0, j)),
                   pl.BlockSpec((1, ft), lambda j, b: (0, j))],
        out_shape=[jax.ShapeDtypeStruct((cfg.T, cfg.FF), bf16), jax.ShapeDtypeStruct((cfg.T, cfg.FF), bf16),
                   jax.ShapeDtypeStruct((cfg.FF, d), f32), jax.ShapeDtypeStruct((3, cfg.FF), f32),
                   jax.ShapeDtypeStruct((1, cfg.FF), f32)],
        compiler_params=_params(("arbitrary", "arbitrary")),
    )(dr2, wd, gp, up, conv_w, conv_b)


def _b_ffn_a(cfg, dgp, dup, dr2, r_mid, g2, wg, wu):
    d = cfg.D

    def body(dgp_ref, dup_ref, dr_ref, r_ref, g_ref, wg_ref, wu_ref, o_ref, dg_ref):
        _zero_at_first(pl.program_id(0) == 0, dg_ref)
        g = g_ref[...]
        _, xh, inv = _rms_fwd(r_ref[...], g)
        dh = _mm(dgp_ref[...], wg_ref[...]) + _mm(dup_ref[...], wu_ref[...])
        dx, dg = _rms_bwd(dh, xh, inv, g)
        dg_ref[...] = dg_ref[...] + dg
        o_ref[...] = dr_ref[...] + dx

    return _row_call("ffn_in_bwd", body, cfg.T, cfg.TMF, [(dgp, cfg.FF, 0), (dup, cfg.FF, 0), (dr2, d, 0), (r_mid, d, 0)],
                     [g2, wg, wu], [(d, f32)], [((1, d), f32)])


def _mm_tn_call(cfg, name, x, dy, tn):
    t, k = x.shape
    n = dy.shape[1]
    tm = cfg.TM
    last = t // tm - 1

    def body(x_ref, dy_ref, o_ref, acc):
        _zero_at_first(pl.program_id(1) == 0, acc)
        acc[...] = acc[...] + _mm_tn(x_ref[...], dy_ref[...])

        @pl.when(pl.program_id(1) == last)
        def _():
            o_ref[...] = acc[...].astype(bf16)

    return pl.pallas_call(
        body, name=name, grid=(n // tn, t // tm),
        in_specs=[pl.BlockSpec((tm, k), lambda j, i: (i, 0)), pl.BlockSpec((tm, tn), lambda j, i: (i, j))],
        out_specs=pl.BlockSpec((k, tn), lambda j, i: (0, j)),
        out_shape=jax.ShapeDtypeStruct((k, n), bf16), scratch_shapes=[pltpu.VMEM((k, tn), f32)],
        compiler_params=_params(("arbitrary", "arbitrary")),
    )(x, dy)


def _b_mix(cfg, dr, o, proj, hs, dng, lrg, wout):
    dn, lw, d = cfg.DN, cfg.LW, cfg.D

    def body(dr_ref, o_ref, z_ref, lg_ref, hs_ref, dng_ref, lrg_ref, w_ref, do_ref, dz_ref, dlg_ref, dhs_ref, ddn_ref, dlr_ref):
        _zero_at_first(pl.program_id(0) == 0, ddn_ref, dlr_ref)
        dng, lrg = dng_ref[...], lrg_ref[...]
        hs = hs_ref[...]
        o, z = o_ref[...], z_ref[...]
        heads, lru = _mix_parts(cfg, o, z, lg_ref[...], hs, dng, lrg)
        drb = dr_ref[...].astype(bf16)
        dmix_dn = _mm_nt(drb, w_ref[0:dn, :])
        dmix_lr = _mm_nt(drb, w_ref[dn:, :])
        dgn = jnp.zeros_like(dng)
        for h, (y, xh, inv, sz, dsz) in enumerate(heads):
            sl = slice(h * 128, (h + 1) * 128)
            dm = dmix_dn[:, sl]
            dz_ref[:, sl] = dm * y * dsz
            dx, dg = _rms_bwd(dm * sz, xh, inv, dng)
            do_ref[:, sl] = dx
            dgn = dgn + dg
        ddn_ref[...] = ddn_ref[...] + dgn
        _, xh2, inv2, gl, dgl = lru
        dx2, dg2 = _rms_bwd(dmix_lr, xh2, inv2, lrg)
        dlr_ref[...] = dlr_ref[...] + dg2
        dlg_ref[...] = dx2 * hs * dgl
        dhs_ref[...] = dx2 * gl

    return _row_call("mix_bwd", body, cfg.T, cfg.TM,
                     [(dr, d, 0), (o, dn, 0), (proj, dn, cfg.ZO // dn), (proj, lw, cfg.LGO // lw), (hs, lw, 0)],
                     [dng, lrg, wout], [(dn, f32), (dn, f32), (lw, f32), (lw, f32)], [((1, 128), f32), ((1, lw), f32)])


def _b_lru(cfg, proj, dhs, conv_w, conv_b, wa, wx, ba, bx, lam):
    def body(lx_ref, dh_ref, cw_ref, cb_ref, wa_ref, wx_ref, ba_ref, bx_ref, lam_ref,
             dlx_ref, dcw_ref, dcb_ref, dwa_ref, dwx_ref, dba_ref, dbx_ref, dlam_ref, a_s, b_s, h_s, an_s, l_s):
        _zero_at_first(pl.program_id(1) == 0, dcw_ref, dcb_ref, dwa_ref, dwx_ref, dba_ref, dbx_ref, dlam_ref)
        lx = lx_ref[...]
        xc = _conv_fwd(lx, cw_ref, 2) + cb_ref[...]
        dxc = jnp.zeros_like(xc)
        gates = []
        for d in range(2):
            z = _lru_gates(xc, wa_ref[d, 0], wx_ref[d, 0], ba_ref[d:d + 1, :], bx_ref[d:d + 1, :], lam_ref[d:d + 1, :])
            a_s[d] = z["a"]
            b_s[d] = z["m"] * z["gx"]
            an_s[d] = _shift(z["a"], -1 if d == 1 else 1, 0.0)
            gates.append(z)
        _scan_refs([(a_s.at[d], b_s.at[d], h_s.at[d], d == 1) for d in range(2)]
                   + [(an_s.at[d], dh_ref, l_s.at[d], d == 0) for d in range(2)], cfg.S)
        for d in range(2):
            rev = d == 1
            lam = lam_ref[d:d + 1, :]
            z = gates[d]
            a, m, ra, ig, sp = z["a"], z["m"], z["ra"], z["ig"], z["sp"]
            lmb = l_s[d]
            h_prev = _shift(h_s[d], 1 if rev else -1, 0.0)
            da = lmb * h_prev
            dm = lmb * z["gx"]
            dgx = lmb * m
            dla = da * a - dm * (a * a) / jnp.maximum(m, 1e-30)
            dra = dla * (-LRU_C) * sp
            dsp = jnp.sum(dla * (-LRU_C) * ra, axis=0, keepdims=True)
            dlam_ref[d:d + 1, :] = dlam_ref[d:d + 1, :] - dsp * _sigmoid(-lam)
            dpa = dra * ra * (1.0 - ra)
            dpx = dgx * xc * ig * (1.0 - ig)
            dba_ref[d:d + 1, :] = dba_ref[d:d + 1, :] + jnp.sum(dpa, axis=0, keepdims=True)
            dbx_ref[d:d + 1, :] = dbx_ref[d:d + 1, :] + jnp.sum(dpx, axis=0, keepdims=True)
            dwa_ref[d, 0] = dwa_ref[d, 0] + _mm_tn(xc, dpa)
            dwx_ref[d, 0] = dwx_ref[d, 0] + _mm_tn(xc, dpx)
            dxc = dxc + dgx * ig + _mm_nt(dpa, wa_ref[d, 0]) + _mm_nt(dpx, wx_ref[d, 0])
        dcb_ref[...] = dcb_ref[...] + jnp.sum(dxc, axis=0, keepdims=True)
        dx, dws = _conv_bwd(dxc, lx, cw_ref, 2)
        for j, dw in enumerate(dws):
            dcw_ref[j:j + 1, :] = dcw_ref[j:j + 1, :] + dw
        dlx_ref[...] = dx

    sp = _lru_specs(cfg, False)
    nct = cfg.LW // LANES
    return pl.pallas_call(
        body, name="lru_bwd", grid=(nct, cfg.BL),
        in_specs=[sp["lx"], sp["tok"], sp["cw"], sp["row"], sp["w"], sp["w"], sp["two"], sp["two"], sp["two"]],
        out_specs=[sp["tok"], sp["cw"], sp["row"], sp["w"], sp["w"], sp["two"], sp["two"], sp["two"]],
        out_shape=[jax.ShapeDtypeStruct((cfg.T, cfg.LW), f32), jax.ShapeDtypeStruct((4, cfg.LW), f32),
                   jax.ShapeDtypeStruct((1, cfg.LW), f32), jax.ShapeDtypeStruct((2, nct, LANES, LANES), f32),
                   jax.ShapeDtypeStruct((2, nct, LANES, LANES), f32), jax.ShapeDtypeStruct((2, cfg.LW), f32),
                   jax.ShapeDtypeStruct((2, cfg.LW), f32), jax.ShapeDtypeStruct((2, cfg.LW), f32)],
        scratch_shapes=[pltpu.VMEM((2, cfg.S, LANES), f32)] * 5,
        compiler_params=_params(("arbitrary", "arbitrary")),
    )(proj, dhs, conv_w, conv_b, wa, wx, ba, bx, lam)


def _b_gdn(cfg, qkv, proj, do, alog_row, dt_row, saved):
    s, c, nc, hh = cfg.S, cfg.C, cfg.NC, cfg.H
    ba_blk = cfg.BAO // LANES
    scale = 128 ** -0.5
    per = 4
    assert nc % per == 0

    def body(q_ref, k_ref, v_ref, ba_ref, do_ref, al_ref, dt_ref, u_i, w_i, qd_i, kd_i, p_i, t_i, dec_i, cd_i,
             dqkv_ref, dba_ref, dal_ref, ddt_ref, vn_s, dvn_s, st_s, dst_s):
        h = pl.program_id(1)
        _zero_at_first((pl.program_id(0) == 0) & (h == 0), dal_ref, ddt_ref)
        _zero_at_first(h == 0, dba_ref)
        lane = lax.broadcasted_iota(jnp.int32, (c, LANES), 1)
        lane1 = lax.broadcasted_iota(jnp.int32, (1, LANES), 1)
        dirs = (0, 1)
        lane_b = [d * hh + h for d in dirs]
        lane_a = [2 * hh + d * hh + h for d in dirs]

        def seq(i, box):
            sts, dst = box[0]
            live = i < nc
            ii = jnp.where(live, i, 0)
            nf = [ii, nc - 1 - ii]
            nb = [nc - 1 - ii, ii]
            sf = [jnp.where(live, n, nc) for n in nf]
            sb = [jnp.where(live, n, nc) for n in nb]
            rf = [pl.ds(pl.multiple_of(n * c, c), c) for n in nf]
            rb = [pl.ds(pl.multiple_of(n * c, c), c) for n in nb]
            for d in dirs:
                st_s[d, sf[d]] = sts[d]
                dst_s[d, sb[d]] = dst[d]
            dob = [do_ref[rb[d], :] for d in dirs]
            vn = [u_i[d, rf[d], :] - _mm(w_i[d, rf[d], :], sts[d]) for d in dirs]
            dvn = [_mm_tn(p_i[d, 0, nb[d]], dob[d]) + _mm(kd_i[d, rb[d], :], dst[d]) for d in dirs]
            yield
            for d in dirs:
                vn_s[d, pl.ds(pl.multiple_of(sf[d] * c, c), c), :] = vn[d]
                dvn_s[d, pl.ds(pl.multiple_of(sb[d] * c, c), c), :] = dvn[d]
            sts = tuple(sts[d] * cd_i[d, 0, nf[d]][0:1, :] + _mm_tn(kd_i[d, rf[d], :], vn[d]) for d in dirs)
            dst = tuple(_mm_tn(qd_i[d, rb[d], :], dob[d]) + cd_i[d, 0, nb[d]][0:1, 0:1] * dst[d]
                        - _mm_tn(w_i[d, rb[d], :], dvn[d]) for d in dirs)
            box[0] = (sts, dst)
            yield

        def bpar(chunks, tick=lambda: None):
            ch = [(d, n) for n in chunks for d in dirs]
            rows = [pl.ds(pl.multiple_of(n * c, c), c) for _, n in ch]
            masks = [_tri_masks(c, d == 1) for d, _ in ch]
            ld = lambda ref: [ref[d, r, :] for (d, _), r in zip(ch, rows)]
            ldm = lambda ref: [ref[d, 0, n] for d, n in ch]
            q, k, v, dob = ([ref[r, :] for r in rows] for ref in (q_ref, k_ref, v_ref, do_ref))
            gates = [_gate_cols(ba_ref[r, :], al_ref[...], dt_ref[...], lane_b[d], lane_a[d]) for (d, _), r in zip(ch, rows)]
            beta, g, aexp, xa = ([gt[x] for gt in gates] for x in range(4))
            st, dst = ([ref[d, n] for d, n in ch] for ref in (st_s, dst_s))
            vn, dvn, u, w = ld(vn_s), ld(dvn_s), ld(u_i), ld(w_i)
            p = [x.astype(f32) for x in ldm(p_i)]
            tm, dec = ldm(t_i), ldm(dec_i)
            cd = [cd_i[d, 0, n][0:1, 0:1] for d, n in ch]
            dp = _map(lambda m, a, b: jnp.where(m[0], _mm_nt(a, b), 0.0), masks, dob, vn)
            dqd = _map(_mm_nt, dob, st)
            dkd = _map(_mm_nt, vn, dst)
            dw = _map(lambda a, b: -_mm_nt(a, b), dvn, st)
            dcd = _map(lambda a, b: jnp.sum(jnp.sum(a * b, axis=1, keepdims=True), axis=0, keepdims=True), st, dst)
            tick()
            gcum = _map(lambda m, x: _cum_mm(m[0], jnp.broadcast_to(x, (c, c)))[:, 0:1], masks, g)
            glast = [jnp.sum(x, axis=0, keepdims=True) for x in g]
            e = [jnp.exp(x) for x in gcum]
            el = _map(lambda a, b: jnp.exp(a - b), glast, gcum)
            qs = [x * scale for x in q]
            kb = _map(lambda a, b: a * b, k, beta)
            a = _map(lambda m, b, ki, dc: jnp.where(m[1], b * _mm_nt(ki, ki) * dc, 0.0), masks, beta, k, dec)
            tick()
            dvb = _map(_hmm3_tn, tm, dvn)
            dkbe = _map(_hmm3_tn, tm, dw)
            tick()
            da = _map(lambda m, x, ui, y, wi: -jnp.where(m[1], _mm_nt(x, ui) + _mm_nt(y, wi), 0.0), masks, dvb, u, dkbe, w)
            g1 = _map(lambda x, y: x * y, da, dec)
            g2 = _map(lambda x, y: x * y, dp, dec)
            dkb = _map(lambda x, ki, y, ei: _mm(x, ki) + y * ei, g1, k, dkbe, e)
            tick()
            dk = _map(lambda x, kbi, y, qi, z, b, t, l: _mm_tn(x, kbi) + _mm_tn(y, qi) + z * b + t * l,
                      g1, kb, g2, qs, dkb, beta, dkd, el)
            dqs = _map(lambda y, ki, x, ei: _mm(y, ki) + x * ei, g2, k, dqd, e)
            tick()
            ddd = _map(lambda x, ai, y, pi: x * ai + y * pi, da, a, dp, p)
            ones = jnp.ones((c, LANES), f32)
            dgcum = _map(lambda x: jnp.sum(x, axis=1, keepdims=True) - _hmm3_tn(x, ones)[:, 0:1], ddd)
            for x, (d, n) in enumerate(ch):
                dbeta = jnp.sum(dvb[x] * v[x], axis=1, keepdims=True) + jnp.sum(dkb[x] * k[x], axis=1, keepdims=True)
                de = jnp.sum(dkbe[x] * kb[x], axis=1, keepdims=True) + jnp.sum(dqd[x] * qs[x], axis=1, keepdims=True)
                del_ = jnp.sum(dkd[x] * k[x], axis=1, keepdims=True)
                dgc = dgcum[x] + de * e[x] - del_ * el[x]
                dglast = jnp.sum(del_ * el[x], axis=0, keepdims=True) + dcd[x] * cd[x]
                dg = _cum_mm(masks[x][2], jnp.broadcast_to(dgc, (c, LANES)))[:, 0:1] + dglast
                r = rows[x]
                if d == 0:
                    dqkv_ref[0, r, :] = dqs[x] * scale
                    dqkv_ref[1, r, :] = dk[x]
                    dqkv_ref[2, r, :] = dvb[x] * beta[x]
                else:
                    dqkv_ref[0, r, :] = dqkv_ref[0, r, :] + dqs[x] * scale
                    dqkv_ref[1, r, :] = dqkv_ref[1, r, :] + dk[x]
                    dqkv_ref[2, r, :] = dqkv_ref[2, r, :] + dvb[x] * beta[x]
                dlb = dbeta * beta[x] * (1.0 - beta[x])
                dalpha = -dg * aexp[x] * _sigmoid(xa[x])
                dba_ref[r, :] = dba_ref[r, :] + jnp.where(lane == lane_b[d], dlb, 0.0) + jnp.where(lane == lane_a[d], dalpha, 0.0)
                dal_ref[...] = dal_ref[...] + jnp.where(lane1 == lane_a[d], jnp.sum(dg * g[x], axis=0, keepdims=True), 0.0)
                ddt_ref[...] = ddt_ref[...] + jnp.where(lane1 == lane_a[d], jnp.sum(dalpha, axis=0, keepdims=True), 0.0)

        half, groups = nc // 2, nc // per

        def group(k):
            return [half - 2 * k - 2, half - 2 * k - 1, half + 2 * k, half + 2 * k + 1]

        def run(gen):
            for _ in gen:
                pass

        def plain(i, carry):
            box = [carry]
            run(seq(i, box))
            return box[0]

        def woven(k, carry):
            box = [carry]

            def two_steps():
                yield from seq(half + 2 + 2 * k, box)
                yield from seq(half + 3 + 2 * k, box)

            chain = two_steps()
            bpar(group(k), lambda: next(chain, None))
            run(chain)
            return box[0]

        z0 = jnp.zeros((128, 128), f32)
        lax.fori_loop(0, groups, woven, lax.fori_loop(0, half + 2, plain, ((z0, z0), (z0, z0))))

    blk = lambda off: pl.BlockSpec((s, 128), functools.partial(lambda b, h, off: (b, off + h), off=off))
    row = pl.BlockSpec((1, LANES), lambda b, h: (0, 0))
    tok2, mat2, cd2, _ = _gdn_specs(cfg)
    return pl.pallas_call(
        body, name="gdn_bwd", grid=(cfg.BL, hh),
        in_specs=[blk(0), blk(hh), blk(2 * hh), pl.BlockSpec((s, LANES), lambda b, h: (b, ba_blk)),
                  pl.BlockSpec((s, 128), lambda b, h: (b, h)), row, row, tok2, tok2, tok2, tok2, mat2, mat2, mat2, cd2],
        out_specs=[pl.BlockSpec((3, s, 128), lambda b, h: (0, b, h)), pl.BlockSpec((s, LANES), lambda b, h: (b, 0)), row, row],
        out_shape=[jax.ShapeDtypeStruct((3, cfg.T, cfg.DN), f32), jax.ShapeDtypeStruct((cfg.T, LANES), f32),
                   jax.ShapeDtypeStruct((1, LANES), f32), jax.ShapeDtypeStruct((1, LANES), f32)],
        scratch_shapes=[pltpu.VMEM((2, s + c, 128), f32)] * 2 + [pltpu.VMEM((2, nc + 1, 128, 128), f32)] * 2,
        compiler_params=_params(("arbitrary", "arbitrary")),
    )(qkv, qkv, qkv, proj, do, alog_row, dt_row, *saved)


def _b_prep(cfg, proj, dqkv, conv_w):
    dn, s = cfg.DN, cfg.S

    def body(x_ref, dy_ref, w_ref, dx_ref, dw_ref):
        _zero_at_first(pl.program_id(1) == 0, dw_ref)
        sec = pl.program_id(0)
        x = x_ref[...]
        c = _conv_fwd(x, w_ref, 2)
        y, dsilu = _silu(c)
        dy = dy_ref[0]
        parts = []
        for h in range(cfg.H):
            sl = slice(h * 128, (h + 1) * 128)
            yh, dyh = y[:, sl], dy[:, sl]
            inv = lax.rsqrt(jnp.sum(yh * yh, axis=1, keepdims=True) + EPS)
            dn_h = inv * dyh - yh * (inv * inv * inv) * jnp.sum(dyh * yh, axis=1, keepdims=True)
            parts.append(jnp.where(sec < 2, dn_h, dyh))
        ds = jnp.concatenate(parts, axis=1) if len(parts) > 1 else parts[0]
        dx, dws = _conv_bwd(ds * dsilu, x, w_ref, 2)
        for j, dw in enumerate(dws):
            dw_ref[j:j + 1, :] = dw_ref[j:j + 1, :] + dw
        dx_ref[...] = dx

    return pl.pallas_call(
        body, name="dn_prep_bwd", grid=(3, cfg.BL),
        in_specs=[pl.BlockSpec((s, dn), lambda j, b: (b, j)), pl.BlockSpec((1, s, dn), lambda j, b: (j, b, 0)),
                  pl.BlockSpec((4, dn), lambda j, b: (0, j))],
        out_specs=[pl.BlockSpec((s, dn), lambda j, b: (b, j)), pl.BlockSpec((4, dn), lambda j, b: (0, j))],
        out_shape=[jax.ShapeDtypeStruct((cfg.T, 3 * dn), f32), jax.ShapeDtypeStruct((4, 3 * dn), f32)],
        compiler_params=_params(("arbitrary", "arbitrary")),
    )(proj, dqkv, conv_w)


def _b_in(cfg, dqkv_pre, dz, dlx, dlg, dba, dr_mid, r_in, g1, wcat):
    d, dn, lw = cfg.D, cfg.DN, cfg.LW

    def body(dq_ref, dz_ref, dlx_ref, dlg_ref, dba_ref, dr_ref, r_ref, g_ref, w_ref, o_ref, dp_ref, dg_ref):
        _zero_at_first(pl.program_id(0) == 0, dg_ref)
        dp_ref[:, 0:cfg.ZO] = dq_ref[...].astype(bf16)
        dp_ref[:, cfg.ZO:cfg.LXO] = dz_ref[...].astype(bf16)
        dp_ref[:, cfg.LXO:cfg.LGO] = dlx_ref[...].astype(bf16)
        dp_ref[:, cfg.LGO:cfg.BAO] = dlg_ref[...].astype(bf16)
        dp_ref[:, cfg.BAO:] = dba_ref[...].astype(bf16)
        g = g_ref[...]
        _, xh, inv = _rms_fwd(r_ref[...], g)
        dx, dg = _rms_bwd(_mm(dp_ref[...], w_ref[...]), xh, inv, g)
        dg_ref[...] = dg_ref[...] + dg
        o_ref[...] = dr_ref[...] + dx

    return _row_call("in_proj_bwd", body, cfg.T, cfg.TM,
                     [(dqkv_pre, 3 * dn, 0), (dz, dn, 0), (dlx, lw, 0), (dlg, lw, 0), (dba, LANES, 0), (dr_mid, d, 0), (r_in, d, 0)],
                     [g1, wcat], [(d, f32), (cfg.PC, bf16)], [((1, d), f32)])


def _adam_call(name, w, g, m, v, tr):
    rows, cols = w.shape
    bc1 = 1.0 - ADAM_B1 ** ADAM_STEP
    bc2 = 1.0 - ADAM_B2 ** ADAM_STEP

    def body(w_ref, g_ref, m_ref, v_ref, d_ref, nm_ref, nv_ref):
        g = g_ref[...]
        m = ADAM_B1 * m_ref[...] + (1.0 - ADAM_B1) * g
        v = ADAM_B2 * v_ref[...] + (1.0 - ADAM_B2) * (g * g)
        nm_ref[...] = m
        nv_ref[...] = v
        d_ref[...] = -ADAM_LR * ((m / bc1) / (jnp.sqrt(v / bc2) + ADAM_EPS) + ADAM_WD * w_ref[...])

    spec = pl.BlockSpec((tr, cols), lambda i: (i, 0))
    return pl.pallas_call(
        body, name=name, grid=(rows // tr,), in_specs=[spec] * 4, out_specs=[spec] * 3,
        out_shape=[jax.ShapeDtypeStruct((rows, cols), f32)] * 3, compiler_params=_params(("arbitrary",)),
    )(w, g, m, v)


def _sum8_call(name, x, tr):
    _, rows, cols = x.shape

    def body(x_ref, o_ref):
        acc = x_ref[0].astype(f32)
        for j in range(1, N_DEV):
            acc = acc + x_ref[j].astype(f32)
        o_ref[...] = acc

    return pl.pallas_call(
        body, name=name, grid=(rows // tr,), in_specs=[pl.BlockSpec((N_DEV, tr, cols), lambda i: (0, i, 0))],
        out_specs=pl.BlockSpec((tr, cols), lambda i: (i, 0)), out_shape=jax.ShapeDtypeStruct((rows, cols), f32),
        compiler_params=_params(("arbitrary",)),
    )(x)


def _all_gather(name, shards):
    na = len(shards)

    def body(*refs):
        xs, outs = refs[:na], refs[na:2 * na]
        send_sems, recv_sems, local_sems = refs[2 * na:]
        x, y, c = lax.axis_index("x"), lax.axis_index("y"), lax.axis_index("c")
        me, sibling = (x, y, c), (x, y, 1 - c)
        chips = [(1 - x, y), (x, 1 - y), (1 - x, 1 - y)]

        def copy(a, k, block, to, src=None):
            px, py, pc = block
            slot = outs[a].at[4 * px + 2 * py + pc]
            return pltpu.make_async_remote_copy(
                src_ref=slot if src is None else src, dst_ref=slot,
                send_sem=send_sems.at[a, k], recv_sem=recv_sems.at[a, k], device_id=to, device_id_type=MESH)

        mine = [pltpu.make_async_copy(xs[a], outs[a].at[4 * x + 2 * y + c], local_sems.at[a]) for a in range(na)]
        for cp in mine:
            cp.start()
        first = []
        for a in range(na):
            first.append(copy(a, 0, me, sibling, src=xs[a]))
            first += [copy(a, 1 + j, me, (*chip, c), src=xs[a]) for j, chip in enumerate(chips)]
        for cp in first:
            cp.start()
        passed = []
        for j, chip in enumerate(chips):
            for a in range(na):
                copy(a, 1 + j, (*chip, c), me).wait_recv()
                cp = copy(a, 4 + j, (*chip, c), sibling)
                cp.start()
                passed.append(cp)
        for a in range(na):
            copy(a, 0, sibling, me).wait_recv()
            for j, chip in enumerate(chips):
                copy(a, 4 + j, (*chip, 1 - c), me).wait_recv()
        for cp in first + passed:
            cp.wait_send()
        for cp in mine:
            cp.wait()

    hbm = pl.BlockSpec(memory_space=pltpu.HBM)
    return pl.pallas_call(
        body, name=name, out_shape=[jax.ShapeDtypeStruct((N_DEV,) + s.shape, s.dtype) for s in shards],
        in_specs=[hbm] * na, out_specs=[hbm] * na,
        scratch_shapes=[pltpu.SemaphoreType.DMA((na, 7)), pltpu.SemaphoreType.DMA((na, 7)), pltpu.SemaphoreType.DMA((na,))],
    )(*shards)


def _peer_list():
    x, y, c = lax.axis_index("x"), lax.axis_index("y"), lax.axis_index("c")
    return 4 * x + 2 * y + c, [(x ^ (k >> 2), y ^ ((k >> 1) & 1), c ^ (k & 1)) for k in range(1, N_DEV)]


_HBM = pl.BlockSpec(memory_space=pltpu.HBM)
_SEM = pl.BlockSpec(memory_space=pltpu.SEMAPHORE)
_EFFECT = pltpu.SideEffectType.DATAFLOW_SIDE_EFFECTING


def _exchange_copies(xs, lands, send_sem, recv_sem, gather):
    me, peers = _peer_list()
    return [pltpu.make_async_remote_copy(
        src_ref=xs[a] if gather else xs[a].at[4 * px + 2 * py + pc], dst_ref=lands[a].at[me],
        send_sem=send_sem.at[7 * a + k], recv_sem=recv_sem.at[7 * a + k], device_id=(px, py, pc), device_id_type=MESH)
        for k, (px, py, pc) in enumerate(peers) for a in range(len(xs))]


def _exchange_start(name, blocks, gather=False):
    na = len(blocks)

    def body(*refs):
        for cp in _exchange_copies(refs[:na], refs[na:2 * na], refs[2 * na], refs[2 * na + 1], gather):
            cp.start()
        refs[-1][...] = jnp.zeros_like(refs[-1])

    lands = [jax.ShapeDtypeStruct((N_DEV,) + b.shape if gather else b.shape, b.dtype) for b in blocks]
    hbm = [pltpu.HBM(b.shape, b.dtype) for b in blocks] + [pltpu.HBM(b.shape, b.dtype) for b in lands]
    send_sem, recv_sem, *thru, token = pl.pallas_call(
        body, name=name,
        out_shape=(pltpu.SemaphoreType.DMA((7 * na,)), pltpu.SemaphoreType.DMA((7 * na,)), *hbm,
                   jax.ShapeDtypeStruct((8, LANES), f32)),
        in_specs=[_HBM] * (2 * na), out_specs=(_SEM, _SEM, *([_HBM] * (2 * na)), pl.BlockSpec(memory_space=pltpu.VMEM)),
        input_output_aliases={i: 2 + i for i in range(2 * na)},
        compiler_params=pltpu.CompilerParams(has_side_effects=_EFFECT),
    )(*[pltpu.with_memory_space_constraint(b, pltpu.HBM) for b in blocks],
      *[pltpu.with_memory_space_constraint(lax.empty(b.shape, b.dtype), pltpu.HBM) for b in lands])
    return send_sem, recv_sem, thru, token


def _exchange_wait(name, send_sem, recv_sem, thru, after, gather=False):
    na = len(thru) // 2

    def body(*refs):
        for cp in _exchange_copies(refs[:na], refs[na:2 * na], refs[2 * na], refs[2 * na + 1], gather):
            cp.wait_send()
            cp.wait_recv()

    return pl.pallas_call(
        body, name=name, out_shape=tuple(pltpu.HBM(t.shape, t.dtype) for t in thru),
        in_specs=[_HBM] * (2 * na) + [_SEM, _SEM, pl.BlockSpec(memory_space=pl.ANY)], out_specs=tuple([_HBM] * (2 * na)),
        input_output_aliases={i: i for i in range(2 * na)},
        compiler_params=pltpu.CompilerParams(has_side_effects=_EFFECT),
    )(*thru, send_sem, recv_sem, after)[na:]


def _layer_fwd(cfg, w, r, p):
    h1, proj = _k_in(cfg, r, w["norm1_g"], w["wcat_t"])
    qkv = _k_prep(cfg, proj, w["dn_conv_w"])
    o, *gdn_saved = _k_gdn_fwd(cfg, qkv, proj, w["alog_row"], w["dt_row"])
    hs = _k_lru_fwd(cfg, proj, w["lru_conv_w"], w["lru_conv_b"], w["wa"], w["wx"], w["lru_ba"], w["lru_bx"], w["lru_lambda"])
    if "late" in w:
        w.update(w.pop("late")(hs))
    mix, r_mid = _k_mix(cfg, o, proj, hs, r, w["dn_norm_g"], w["lru_norm_g"], w["w_out"])
    h2, gp, up = _k_ffn_a(cfg, r_mid, w["norm2_g"], w["ffn_wg_t"], w["ffn_wu_t"])
    act = _k_ffn_b(cfg, gp, up, w["ffn_conv_w"], w["ffn_conv_b"])
    r2 = _k_ffn_c(cfg, act, r_mid, w["ffn_wd"])
    pn, r3 = _k_ple(cfg, r2, p, w["ple_norm_g"], w["ple_wg"], w["ple_bg"], w["ple_wp_t"])
    saved = dict(r=r, h1=h1, proj=proj, qkv=qkv, o=o, gdn=gdn_saved, hs=hs, mix=mix, r_mid=r_mid, h2=h2, gp=gp, up=up,
                 r2=r2, pn=pn, p=p)
    return r3, saved


def _layer_bwd(cfg, w, sv, dr3, early=None):
    g = {}
    dt = min(512, cfg.D)
    dr2, dr2b, dlog, dpp, g["ple_norm_g"], g["ple_bg"] = _b_ple(cfg, dr3, sv["r2"], sv["p"], w["ple_norm_g"], w["ple_wg"], w["ple_bg"], w["ple_wp_t"])
    g["ple_wg"] = _mm_tn_call(cfg, "d_ple_wg", sv["pn"], dlog, dt)
    g["ple_wp_t"] = _mm_tn_call(cfg, "d_ple_wp", dpp, sv["p"], cfg.PD)
    dgp, dup, g["ffn_wd"], g["ffn_conv_w"], g["ffn_conv_b"] = _b_ffn_bc(cfg, dr2b, w["ffn_wd"], sv["gp"], sv["up"], w["ffn_conv_w"], w["ffn_conv_b"])
    dr_mid, g["norm2_g"] = _b_ffn_a(cfg, dgp, dup, dr2, sv["r_mid"], w["norm2_g"], w["ffn_wg_t"], w["ffn_wu_t"])
    g["ffn_wg_t"] = _mm_tn_call(cfg, "d_ffn_wg", dgp, sv["h2"], dt)
    g["ffn_wu_t"] = _mm_tn_call(cfg, "d_ffn_wu", dup, sv["h2"], dt)
    if early is not None:
        w = dict(w, dn_norm_g=w["dn_norm_g"] + early(g)[0, 0])
    do, dz, dlg, dhs, g["dn_norm_g"], g["lru_norm_g"] = _b_mix(cfg, dr_mid, sv["o"], sv["proj"], sv["hs"], w["dn_norm_g"], w["lru_norm_g"], w["w_out"])
    g["w_out"] = _mm_tn_call(cfg, "d_w_out", sv["mix"], dr_mid, dt)
    dlx, g["lru_conv_w"], g["lru_conv_b"], g["wa"], g["wx"], g["lru_ba"], g["lru_bx"], g["lru_lambda"] = _b_lru(
        cfg, sv["proj"], dhs, w["lru_conv_w"], w["lru_conv_b"], w["wa"], w["wx"], w["lru_ba"], w["lru_bx"], w["lru_lambda"])
    dqkv, dba, g["alog_row"], g["dt_row"] = _b_gdn(cfg, sv["qkv"], sv["proj"], do, w["alog_row"], w["dt_row"], sv["gdn"])
    dqkv_pre, g["dn_conv_w"] = _b_prep(cfg, sv["proj"], dqkv, w["dn_conv_w"])
    dr, dproj, g["norm1_g"] = _b_in(cfg, dqkv_pre, dz, dlx, dlg, dba, dr_mid, sv["r"], w["norm1_g"], w["wcat_t"])
    g["wcat_t"] = _mm_tn_call(cfg, "d_w_in", dproj, sv["h1"], dt)
    return dr, g


BIG = ("w_in", "w_out", "ffn_wg", "ffn_wu", "ffn_wd", "ple_wg", "ple_wp")
BIG_T = {"w_in": True, "w_out": False, "ffn_wg": True, "ffn_wu": True, "ffn_wd": False, "ple_wg": False, "ple_wp": True}
BIG_OPERAND = {"w_in": "wcat_t", "w_out": "w_out", "ffn_wg": "ffn_wg_t", "ffn_wu": "ffn_wu_t", "ffn_wd": "ffn_wd",
               "ple_wg": "ple_wg", "ple_wp": "ple_wp_t"}
EARLY = ("ffn_wg", "ffn_wu", "ffn_wd", "ple_wg", "ple_wp")
SMALL_SHARDED = ("dn_conv_w", "lru_conv_w", "lru_ba", "lru_bx", "lru_lambda", "ffn_conv_w")
SMALL_REPL = ("norm1_g", "dn_a_log", "dn_dt_bias", "dn_norm_g", "lru_conv_b", "lru_wa", "lru_wx", "lru_norm_g", "norm2_g",
              "ffn_conv_b", "ple_norm_g", "ple_bg", "final_g")
WEIGHTS = ("norm1_g", "w_in", "dn_conv_w", "dn_a_log", "dn_dt_bias", "dn_norm_g", "lru_conv_w", "lru_conv_b", "lru_wa",
           "lru_ba", "lru_wx", "lru_bx", "lru_lambda", "lru_norm_g", "w_out", "norm2_g", "ffn_wg", "ffn_wu", "ffn_conv_w",
           "ffn_conv_b", "ffn_wd", "ple_norm_g", "ple_wg", "ple_bg", "ple_wp", "final_g")


def _pad_rows(flat, cols, mult):
    n = flat.shape[0]
    rows = -(-n // cols)
    rows = -(-rows // mult) * mult
    return jnp.pad(flat, (0, rows * cols - n)).reshape(rows, cols)


def _pack(arrs, cols, mult, dtype):
    return _pad_rows(jnp.concatenate([a.reshape(-1).astype(dtype) for a in arrs]), cols, mult)


def _unpack(flat, shapes):
    out, off = [], 0
    for shp in shapes:
        n = math.prod(shp)
        piece = flat[off:off + n]
        if n < 4096:
            piece = lax.optimization_barrier(piece)
        out.append(piece.reshape(shp))
        off += n
    return out


def _unpack8(g8, shapes, axes):
    out, off = [], 0
    for shp, ax in zip(shapes, axes):
        n = math.prod(shp)
        a = g8[:, off:off + n].reshape((N_DEV,) + tuple(shp))
        a = jnp.moveaxis(a, 0, ax)
        out.append(a.reshape(shp[:ax] + (N_DEV * shp[ax],) + shp[ax + 1:]))
        off += n
    return out


def _wcat_t_from_w_in_t(cfg, wt):
    nba = 4 * cfg.H
    pad = jnp.zeros((LANES - nba, wt.shape[1]), wt.dtype)
    return jnp.concatenate([wt[:cfg.LXO], wt[cfg.LXO + nba:], wt[cfg.LXO:cfg.LXO + nba], pad], axis=0)


def _w_in_t_from_wcat_t(cfg, wc):
    nba = 4 * cfg.H
    return jnp.concatenate([wc[:cfg.LXO], wc[cfg.BAO:cfg.BAO + nba], wc[cfg.LXO:cfg.BAO]], axis=0)


def _gate_row(cfg, a):
    h2 = 2 * cfg.H
    return jnp.concatenate([jnp.zeros((1, h2), f32), a.reshape(1, h2), jnp.zeros((1, LANES - 2 * h2), f32)], axis=1)


def _blockdiag(cfg, w):
    w = w.reshape(2, cfg.NB // 2, 2, 64, 64)
    z = jnp.zeros_like(w[:, :, 0])
    top = jnp.concatenate([w[:, :, 0], z], axis=-1)
    bot = jnp.concatenate([z, w[:, :, 1]], axis=-1)
    return jnp.concatenate([top, bot], axis=-2).astype(bf16)


def _unblockdiag(cfg, g):
    a = g[:, :, :64, :64]
    b = g[:, :, 64:, 64:]
    return jnp.stack([a, b], axis=2).reshape(2, cfg.NB, 64, 64)


def _big_operands(cfg, big):
    return {BIG_OPERAND[n]: (_wcat_t_from_w_in_t(cfg, v) if n == "w_in" else v) for n, v in big.items()}


def _layer_operands(cfg, big, small, i):
    return dict(
        _big_operands(cfg, big),
        norm1_g=small["norm1_g"][i][None], dn_conv_w=small["dn_conv_w"][i], alog_row=_gate_row(cfg, small["dn_a_log"][i]),
        dt_row=_gate_row(cfg, small["dn_dt_bias"][i]), dn_norm_g=small["dn_norm_g"][i][None],
        lru_conv_w=small["lru_conv_w"][i], lru_conv_b=small["lru_conv_b"][i][None],
        wa=_blockdiag(cfg, small["lru_wa"][i]), wx=_blockdiag(cfg, small["lru_wx"][i]),
        lru_ba=small["lru_ba"][i], lru_bx=small["lru_bx"][i], lru_lambda=small["lru_lambda"][i],
        lru_norm_g=small["lru_norm_g"][i][None], norm2_g=small["norm2_g"][i][None], ffn_conv_w=small["ffn_conv_w"][i],
        ffn_conv_b=small["ffn_conv_b"][i][None], ple_norm_g=small["ple_norm_g"][i][None], ple_bg=small["ple_bg"][i][None],
    )


def _small_grads_to_problem(cfg, g):
    h = cfg.H
    return dict(
        norm1_g=g["norm1_g"][0], dn_conv_w=g["dn_conv_w"],
        dn_a_log=g["alog_row"][0, 2 * h:4 * h].reshape(2, h), dn_dt_bias=g["dt_row"][0, 2 * h:4 * h].reshape(2, h),
        dn_norm_g=g["dn_norm_g"][0], lru_conv_w=g["lru_conv_w"], lru_conv_b=g["lru_conv_b"][0],
        lru_wa=_unblockdiag(cfg, g["wa"]), lru_wx=_unblockdiag(cfg, g["wx"]), lru_ba=g["lru_ba"], lru_bx=g["lru_bx"],
        lru_lambda=g["lru_lambda"], lru_norm_g=g["lru_norm_g"][0], norm2_g=g["norm2_g"][0], ffn_conv_w=g["ffn_conv_w"],
        ffn_conv_b=g["ffn_conv_b"][0], ple_norm_g=g["ple_norm_g"][0], ple_bg=g["ple_bg"][0],
    )


def _local_step(cfg, get_big, small, x, p, target, on_big_grads):
    r = x.reshape(cfg.T, cfg.D)
    ops, saved = [], []
    for i in range(cfg.L):
        big, token, late = get_big(i, r)
        w = _layer_operands(cfg, big, small, i)
        if token is not None:
            w["norm1_g"] = w["norm1_g"] + token[0, 0]
        if late is not None:
            w["late"] = late
        r, sv = _layer_fwd(cfg, w, r, p[i].reshape(cfg.T, cfg.PD))
        ops.append(w)
        saved.append(sv)
    dr, loss, dgf = _k_loss(cfg, r, target.reshape(cfg.T, cfg.D), small["final_g"][None])
    gsmall = [None] * cfg.L
    for i in reversed(range(cfg.L)):
        first = EARLY if i == 0 else ()
        early = (lambda g: on_big_grads("%da" % i, {n: g[BIG_OPERAND[n]] for n in first})) if first else None
        dr, g = _layer_bwd(cfg, ops[i], saved[i], dr, early)
        token = on_big_grads("%d" % i, {n: (_w_in_t_from_wcat_t(cfg, g["wcat_t"]) if n == "w_in" else g[BIG_OPERAND[n]])
                                        for n in BIG if n not in first})
        if i > 0:
            ops[i - 1]["ple_bg"] = ops[i - 1]["ple_bg"] + token[0, 0]
        gsmall[i] = _small_grads_to_problem(cfg, g)
    gs = {k: jnp.stack([gl[k] for gl in gsmall]) for k in gsmall[0]}
    gs["final_g"] = dgf[0]
    return loss, dr, gs


def _row_tile(rows, limit=512):
    best = rows
    for t in range(8, min(rows, limit) + 1, 8):
        if rows % t == 0:
            best = t
    return best if best <= limit or rows <= limit else rows


def _adam_group(name, ws, gs, ms, vs, cols, tr):
    shapes = [w.shape for w in ws]
    pk = lambda arrs: _pack(arrs, cols, tr, f32)
    w2 = pk(ws)
    d, nm, nv = _adam_call(name, w2, pk(gs), pk(ms), pk(vs), min(tr, w2.shape[0]))
    return [_unpack(a.reshape(-1), shapes) for a in (d, nm, nv)]


def kernel(x, p, norm1_g, w_in, dn_conv_w, dn_a_log, dn_dt_bias, dn_norm_g, lru_conv_w, lru_conv_b, lru_wa, lru_ba, lru_wx, lru_bx, lru_lambda, lru_norm_g, w_out, norm2_g, ffn_wg, ffn_wu, ffn_conv_w, ffn_conv_b, ffn_wd, ple_norm_g, ple_wg, ple_bg, ple_wp, final_g, loss_target, m_norm1_g, m_w_in, m_dn_conv_w, m_dn_a_log, m_dn_dt_bias, m_dn_norm_g, m_lru_conv_w, m_lru_conv_b, m_lru_wa, m_lru_ba, m_lru_wx, m_lru_bx, m_lru_lambda, m_lru_norm_g, m_w_out, m_norm2_g, m_ffn_wg, m_ffn_wu, m_ffn_conv_w, m_ffn_conv_b, m_ffn_wd, m_ple_norm_g, m_ple_wg, m_ple_bg, m_ple_wp, m_final_g, v_norm1_g, v_w_in, v_dn_conv_w, v_dn_a_log, v_dn_dt_bias, v_dn_norm_g, v_lru_conv_w, v_lru_conv_b, v_lru_wa, v_lru_ba, v_lru_wx, v_lru_bx, v_lru_lambda, v_lru_norm_g, v_w_out, v_norm2_g, v_ffn_wg, v_ffn_wu, v_ffn_conv_w, v_ffn_conv_b, v_ffn_wd, v_ple_norm_g, v_ple_wg, v_ple_bg, v_ple_wp, v_final_g):
    cfg = CFG
    a = dict(locals())
    wl = {n: a[n] for n in WEIGHTS}
    ml = {n: a["m_" + n] for n in WEIGHTS}
    vl = {n: a["v_" + n] for n in WEIGHTS}
    me = 4 * lax.axis_index("x") + 2 * lax.axis_index("y") + lax.axis_index("c")
    nl = cfg.L

    blocks = [(jnp.swapaxes(wl[n], 1, 2) if BIG_T[n] else wl[n]).astype(bf16) for n in BIG]
    ss_shapes = [wl[n].shape for n in SMALL_SHARDED]
    first, s8 = _all_gather("gather_weights_0", [blocks[0][0], _pack([wl[n] for n in SMALL_SHARDED], LANES, 8, f32)])
    small = dict(zip(SMALL_SHARDED, _unpack8(s8.reshape(N_DEV, -1), ss_shapes, [2] * len(ss_shapes))))
    small.update({n: wl[n] for n in SMALL_REPL})

    def start_gather(key, i, names, behind):
        shards, _ = lax.optimization_barrier(([blk[i] for n, blk in zip(BIG, blocks) if n in names], behind))
        return _exchange_start("gather_start_" + key, shards, gather=True)

    def wait_gather(key, i, names, started, behind):
        send_sem, recv_sem, thru, _ = started
        lands = _exchange_wait("gather_wait_" + key, send_sem, recv_sem, thru, behind, gather=True)
        own = [blk[i][None] for n, blk in zip(BIG, blocks) if n in names]
        full = [lax.dynamic_update_slice_in_dim(land, o, me, 0) for land, o in zip(lands, own)]
        return {n: f.reshape(N_DEV * f.shape[1], f.shape[2]) for n, f in zip([n for n in BIG if n in names], full)}

    ahead = {}

    def get_big(i, r):
        if i > 0:
            big = wait_gather("%d" % i, i, BIG, ahead.pop(i), r)
            if i + 1 < nl:
                ahead[i + 1] = start_gather("%d" % (i + 1), i + 1, BIG, big["w_in"])
            return big, (ahead[i + 1][3] if i + 1 < nl else None), None
        rest = start_gather("0r", 0, BIG[1:], first)

        def late(x):
            big = wait_gather("0r", 0, BIG[1:], rest, x)
            ops = _big_operands(cfg, big)
            if nl > 1:
                ahead[1] = start_gather("1", 1, BIG, big["w_out"])
                ops["norm2_g"] = small["norm2_g"][0][None] + ahead[1][3][0, 0]
            return ops

        return {"w_in": first.reshape(N_DEV * first.shape[1], first.shape[2])}, rest[3], late

    pending = {}

    def on_big_grads(key, g):
        send = [g[n].reshape((N_DEV,) + blk.shape[1:]).astype(bf16) for n, blk in zip(BIG, blocks) if n in g]
        own = [lax.dynamic_index_in_dim(sd, me, 0, keepdims=True) for sd in send]
        send_sem, recv_sem, thru, token = _exchange_start("exchange_start_" + key, send)
        pending[key] = (send_sem, recv_sem, thru, own, [n for n in BIG if n in g])
        return token

    loss_part, dr, gsmall = _local_step(cfg, get_big, small, x, p, loss_target, on_big_grads)
    grad_x = dr.reshape(x.shape)

    sums = {n: [None] * nl for n in BIG}
    for key in sorted(pending):
        send_sem, recv_sem, thru, own, names = pending[key]
        lands = _exchange_wait("exchange_wait_" + key, send_sem, recv_sem, thru, dr)
        for n, land, o in zip(names, lands, own):
            slots = lax.dynamic_update_slice_in_dim(land, o, me, 0)
            sums[n][int(key[0])] = _sum8_call("sum_%s_%s" % (n, key), slots, slots.shape[1])
    gl = {}
    for n in BIG:
        s = jnp.stack(sums[n])
        gl[n] = jnp.swapaxes(s, 1, 2) if BIG_T[n] else s

    small_names = SMALL_REPL + SMALL_SHARDED
    small_shapes = [gsmall[n].shape for n in small_names]
    sv = _pack([gsmall[n] for n in small_names] + [loss_part[0, 0:1]], LANES, 512, f32)
    small_sum = _sum8_call("sum_small", _all_gather("gather_small_grads", [sv])[0], 512).reshape(-1)
    gl.update(zip(small_names, _unpack(small_sum, small_shapes)))
    loss = small_sum[sum(math.prod(s) for s in small_shapes)]
    for n in SMALL_SHARDED:
        shard = wl[n].shape[2]
        gl[n] = lax.dynamic_slice_in_dim(gl[n], me * shard, shard, axis=2)

    outs = {}
    for n in BIG:
        shp = wl[n].shape
        two = lambda t: t.reshape(-1, shp[-1])
        d, nm, nv = _adam_call("adam_" + n, two(wl[n]), two(gl[n]), two(ml[n]), two(vl[n]), _row_tile(math.prod(shp[:-1])))
        outs[n] = (d.reshape(shp), nm.reshape(shp), nv.reshape(shp))
    d, nm, nv = _adam_group("adam_small", [wl[n] for n in small_names], [gl[n] for n in small_names],
                            [ml[n] for n in small_names], [vl[n] for n in small_names], LANES, 64)
    for j, n in enumerate(small_names):
        outs[n] = (d[j], nm[j], nv[j])
    return (loss, grad_x, *[gl[n] for n in WEIGHTS], *[outs[n][0] for n in WEIGHTS], *[outs[n][1] for n in WEIGHTS],
            *[outs[n][2] for n in WEIGHTS])
```

```python
import functools
import math

import jax
import jax.numpy as jnp
from jax import lax
from jax.experimental import pallas as pl
from jax.experimental.pallas import tpu as pltpu

f32 = jnp.float32
bf16 = jnp.bfloat16
MESH = pl.DeviceIdType.MESH

N_DEV = 8
LANES = 128
EPS = 1e-6
LRU_C = 8.0
ADAM_LR, ADAM_B1, ADAM_B2, ADAM_EPS, ADAM_WD, ADAM_STEP = 0.001, 0.9, 0.999, 1e-08, 0.01, 10
VMEM_LIMIT = 56 * 1024 * 1024


class Cfg:
    def __init__(self, d_model=1024, bl=4, seq=2048, depth=4, heads=4, lru_width=512, d_ff=2816, ple=256,
                 tm=512, tm_ffn=256, ff_tile=256):
        self.D, self.BL, self.S, self.L, self.H = d_model, bl, seq, depth, heads
        self.DH = 128
        self.DN = heads * self.DH
        self.LW = lru_width
        self.NB = lru_width // 64
        self.FF, self.PD = d_ff, ple
        self.C = 64
        self.NC = seq // self.C
        self.T = bl * seq
        self.TM = min(tm, self.T)
        self.TMF = min(tm_ffn, self.T)
        self.FT = ff_tile
        self.ZO = 3 * self.DN
        self.LXO = 4 * self.DN
        self.LGO = self.LXO + self.LW
        self.BAO = self.LGO + self.LW
        self.PC = self.BAO + LANES
        self.IN_COLS = 4 * self.DN + 4 * heads + 2 * self.LW


CFG = Cfg()


def _mm(a, b):
    return jnp.dot(a.astype(bf16), b.astype(bf16), preferred_element_type=f32)


def _mm_nt(a, b):
    return lax.dot_general(a.astype(bf16), b.astype(bf16), (((1,), (1,)), ((), ())), preferred_element_type=f32)


def _mm_tn(a, b):
    return lax.dot_general(a.astype(bf16), b.astype(bf16), (((0,), (0,)), ((), ())), preferred_element_type=f32)


def _split2(a):
    hi = a.astype(bf16)
    return hi, (a - hi.astype(f32)).astype(bf16)


def _hmm3(a, b, dims=(((1,), (0,)), ((), ()))):
    ah, al = _split2(a)
    bh, bl = _split2(b)
    dot = functools.partial(lax.dot_general, dimension_numbers=dims, preferred_element_type=f32)
    return dot(ah, bh) + dot(ah, bl) + dot(al, bh)


def _hmm3_tn(a, b):
    return _hmm3(a, b, (((0,), (0,)), ((), ())))


def _cum_mm(mask, x, dims=(((1,), (0,)), ((), ()))):
    m = mask.astype(bf16)
    x1 = x.astype(bf16)
    r = x - x1.astype(f32)
    x2 = r.astype(bf16)
    x3 = (r - x2.astype(f32)).astype(bf16)
    dot = functools.partial(lax.dot_general, dimension_numbers=dims, preferred_element_type=f32)
    return dot(m, x1) + dot(m, x2) + dot(m, x3)


def _rms_fwd(x, g):
    inv = lax.rsqrt(jnp.mean(x * x, axis=-1, keepdims=True) + EPS)
    xh = x * inv
    return xh * g, xh, inv


def _rms_bwd(dy, xh, inv, g):
    dxh = dy * g
    dx = inv * (dxh - xh * jnp.mean(dxh * xh, axis=-1, keepdims=True))
    dg = jnp.sum(dy * xh, axis=0, keepdims=True)
    return dx, dg


def _sigmoid(x):
    return 1.0 / (1.0 + jnp.exp(-x))


def _softplus(x):
    return jnp.maximum(x, 0.0) + jnp.log(1.0 + jnp.exp(-jnp.abs(x)))


def _silu(x):
    s = _sigmoid(x)
    return x * s, s * (1.0 + x * (1.0 - s))


_GC = math.sqrt(2.0 / math.pi)


def _gelu(x):
    t = jnp.tanh(_GC * (x + 0.044715 * x * x * x))
    y = 0.5 * x * (1.0 + t)
    dy = 0.5 * (1.0 + t) + 0.5 * x * (1.0 - t * t) * _GC * (1.0 + 3.0 * 0.044715 * x * x)
    return y, dy


def _nexpm1(x):
    ser = -x * (1.0 + x * 0.5 * (1.0 + x * (1.0 / 3.0) * (1.0 + x * 0.25 * (1.0 + x * 0.2))))
    return jnp.where(x > -0.1, ser, 1.0 - jnp.exp(x))


def _shift(x, s, fill=0.0):
    if s == 0:
        return x
    n = x.shape[0]
    t = lax.broadcasted_iota(jnp.int32, x.shape, 0)
    r = pltpu.roll(x, (-s) % n, 0)
    return jnp.where((t + s >= 0) & (t + s < n), r, fill)


def _conv_fwd(x, w_ref, left):
    k = w_ref.shape[0]
    out = _shift(x, -left) * w_ref[0:1, :]
    for j in range(1, k):
        out = out + _shift(x, j - left) * w_ref[j:j + 1, :]
    return out


def _conv_bwd(dout, x, w_ref, left):
    k = w_ref.shape[0]
    dx = None
    dws = []
    for j in range(k):
        term = _shift(dout, -(j - left)) * w_ref[j:j + 1, :]
        dx = term if dx is None else dx + term
        dws.append(jnp.sum(dout * _shift(x, j - left), axis=0, keepdims=True))
    return dx, dws


def _scan_refs(scans, n):
    blk = 64
    nb = n // blk
    sub = lax.broadcasted_iota(jnp.int32, (blk, LANES), 0) & 7

    def local(a, b, rev):
        for d in (1, 2, 4):
            ok = (sub < 8 - d) if rev else (sub >= d)
            sh = (blk - d) if rev else d
            b = a * jnp.where(ok, pltpu.roll(b, sh, 0), 0.0) + b
            a = a * jnp.where(ok, pltpu.roll(a, sh, 0), 1.0)
        return a, b

    def body(i, carries):
        new = []
        for (a_ref, b_ref, h_ref, rev), carry in zip(scans, carries):
            j = (nb - 1 - i) if rev else i
            base = pl.multiple_of(j * blk, blk)
            a, b = local(a_ref[pl.ds(base, blk), :], b_ref[pl.ds(base, blk), :], rev)
            order = range(blk // 8 - 1, -1, -1) if rev else range(blk // 8)
            for v in order:
                h = b[8 * v:8 * v + 8, :] + a[8 * v:8 * v + 8, :] * carry
                h_ref[pl.ds(base + 8 * v, 8), :] = h
                carry = h[0:1, :] if rev else h[7:8, :]
            new.append(carry)
        return tuple(new)

    lax.fori_loop(0, nb, body, tuple(jnp.zeros((1, LANES), f32) for _ in scans))


def _tri_masks(c, rev):
    i = lax.broadcasted_iota(jnp.int32, (c, c), 0)
    j = lax.broadcasted_iota(jnp.int32, (c, c), 1)
    incl = (i <= j) if rev else (i >= j)
    strict = (i < j) if rev else (i > j)
    incl_t = (i >= j) if rev else (i <= j)
    return incl, strict, incl_t


def _map(f, *lists):
    return [f(*a) for a in zip(*lists)]


def _tri_inv(mats, tick=lambda: None):
    c = mats[0].shape[0]
    i = lax.broadcasted_iota(jnp.int32, (c, c), 0)
    j = lax.broadcasted_iota(jnp.int32, (c, c), 1)
    eye = jnp.where(i == j, 1.0, 0.0)
    t = [eye - a for a in mats]
    pw = _map(_hmm3, mats, mats)
    for it in range(5):
        tick()
        t = _map(lambda ti, ui: ti + ui, t, _map(_hmm3, t, pw))
        if it < 4:
            pw = _map(_hmm3, pw, pw)
    return t


def _gdn_decay(gs, revs):
    c = gs[0].shape[0]
    masks = [_tri_masks(c, r) for r in revs]
    gb = [jnp.broadcast_to(g, (c, c)) for g in gs]
    mcol = _map(lambda m, x: _cum_mm(m[0], x), masks, gb)
    mrow = _map(lambda m, x: jnp.sum(jnp.where(m[2], x, 0.0), axis=0, keepdims=True), masks, gb)
    dec = _map(lambda m, a, b: jnp.exp(jnp.where(m[0], a - b, -1e30)), masks, mcol, mrow)
    return [m[:, 0:1] for m in mcol], [jnp.sum(g, axis=0, keepdims=True) for g in gs], dec


def _gdn_prep(qs, k, v, g, beta, kk, qk, revs, tick=lambda: None):
    c = k[0].shape[0]
    masks = [_tri_masks(c, r) for r in revs]
    gcum, glast, dec = _gdn_decay(g, revs)
    e = [jnp.exp(x) for x in gcum]
    tick()
    tm = _tri_inv(_map(lambda m, b, x, d: jnp.where(m[1], b * x * d, 0.0), masks, beta, kk, dec), tick)
    tick()
    u = _map(lambda t, vi, b: _hmm3(t, vi * b), tm, v, beta)
    tick()
    w = _map(lambda t, ki, b, ei: _hmm3(t, ki * b * ei), tm, k, beta, e)
    p = _map(lambda m, x, d: jnp.where(m[0], x * d, 0.0), masks, qk, dec)
    return dict(dec=dec, cd=[jnp.exp(x) for x in glast], tm=tm, u=u, w=w, p=p, qd=_map(lambda a, b: a * b, qs, e),
                kd=_map(lambda ki, gl, gc: ki * jnp.exp(gl - gc), k, glast, gcum))


def _lane_pick(x, lane):
    l = lax.broadcasted_iota(jnp.int32, x.shape, 1)
    return jnp.sum(jnp.where(l == lane, x, 0.0), axis=1, keepdims=True)


def _params(sem, vmem=VMEM_LIMIT):
    return pltpu.CompilerParams(dimension_semantics=sem, vmem_limit_bytes=vmem)


def _row_call(name, body, t, tm, row_ins, full_ins, row_outs, acc_outs):
    in_specs = [pl.BlockSpec((tm, w), functools.partial(lambda i, c: (i, c), c=c)) for (_, w, c) in row_ins]
    for a in full_ins:
        in_specs.append(pl.BlockSpec(a.shape, functools.partial(lambda i, n: (0,) * n, n=a.ndim)))
    out_specs = [pl.BlockSpec((tm, w), lambda i: (i, 0)) for (w, _) in row_outs]
    out_shape = [jax.ShapeDtypeStruct((t, w), dt) for (w, dt) in row_outs]
    for shp, dt in acc_outs:
        out_specs.append(pl.BlockSpec(shp, functools.partial(lambda i, n: (0,) * n, n=len(shp))))
        out_shape.append(jax.ShapeDtypeStruct(shp, dt))
    return pl.pallas_call(
        body, name=name, grid=(t // tm,), in_specs=in_specs, out_specs=out_specs, out_shape=out_shape,
        compiler_params=_params(("arbitrary",)),
    )(*[a for (a, _, _) in row_ins], *full_ins)


def _k_in(cfg, r, g1, wcat):
    def body(r_ref, g_ref, w_ref, h_ref, proj_ref):
        y, _, _ = _rms_fwd(r_ref[...], g_ref[...])
        hb = y.astype(bf16)
        h_ref[...] = hb
        proj_ref[...] = _mm_nt(hb, w_ref[...])

    return _row_call("in_proj", body, cfg.T, cfg.TM, [(r, cfg.D, 0)], [g1, wcat],
                     [(cfg.D, bf16), (cfg.PC, f32)], [])


def _k_prep(cfg, proj, conv_w):
    dn, s = cfg.DN, cfg.S

    def body(x_ref, w_ref, o_ref):
        sec = pl.program_id(1)
        c = _conv_fwd(x_ref[...], w_ref, 2)
        y, _ = _silu(c)
        for h in range(cfg.H):
            yh = y[:, h * 128:(h + 1) * 128]
            nh = yh * lax.rsqrt(jnp.sum(yh * yh, axis=1, keepdims=True) + EPS)
            o_ref[:, h * 128:(h + 1) * 128] = jnp.where(sec < 2, nh, yh)

    return pl.pallas_call(
        body, name="dn_prep", grid=(cfg.BL, 3),
        in_specs=[pl.BlockSpec((s, dn), lambda b, j: (b, j)), pl.BlockSpec((4, dn), lambda b, j: (0, j))],
        out_specs=pl.BlockSpec((s, dn), lambda b, j: (b, j)),
        out_shape=jax.ShapeDtypeStruct((cfg.T, 3 * dn), f32),
        compiler_params=_params(("arbitrary", "arbitrary")),
    )(proj, conv_w)


def _gate_cols(ba, alog_row, dt_row, lane_b, lane_a):
    beta = _sigmoid(_lane_pick(ba, lane_b))
    alpha = _lane_pick(ba, lane_a)
    aexp = jnp.exp(_lane_pick(alog_row, lane_a))
    dtb = _lane_pick(dt_row, lane_a)
    xa = alpha + dtb
    g = -aexp * _softplus(xa)
    return beta, g, aexp, xa


def _gdn_specs(cfg):
    s, c, nc = cfg.S, cfg.C, cfg.NC
    tok2 = pl.BlockSpec((2, s, 128), lambda b, h: (0, b, h))
    mat2 = pl.BlockSpec((2, 1, nc, c, c), lambda b, h: (0, b, h, 0, 0))
    cd2 = pl.BlockSpec((2, 1, nc, 8, LANES), lambda b, h: (0, b, h, 0, 0))
    shapes = dict(
        tok32=jax.ShapeDtypeStruct((2, cfg.T, cfg.DN), f32), tok16=jax.ShapeDtypeStruct((2, cfg.T, cfg.DN), bf16),
        mat32=jax.ShapeDtypeStruct((2, cfg.BL, cfg.H * nc, c, c), f32), mat16=jax.ShapeDtypeStruct((2, cfg.BL, cfg.H * nc, c, c), bf16),
        cd=jax.ShapeDtypeStruct((2, cfg.BL, cfg.H * nc, 8, LANES), f32))
    return tok2, mat2, cd2, shapes


def _k_gdn_fwd(cfg, qkv, proj, alog_row, dt_row):
    s, c, nc, hh = cfg.S, cfg.C, cfg.NC, cfg.H
    ba_blk = cfg.BAO // LANES
    per = 4 if nc % 4 == 0 else 1

    def body(q_ref, k_ref, v_ref, ba_ref, al_ref, dt_ref, o_ref, u_o, w_o, qd_o, kd_o, p_o, t_o, dec_o, cd_o):
        h = pl.program_id(1)
        groups = nc // per

        def chunk_of(d, g, j):
            return (nc - 1 - (g * per + j)) if d == 1 else (g * per + j)

        def prep(g, tick=lambda: None):
            chains, qs, k, v, kk, qk, gt, beta = [], [], [], [], [], [], [], []
            for j in range(per):
                for d in range(2):
                    n = chunk_of(d, g, j)
                    rows = pl.ds(pl.multiple_of(n * c, c), c)
                    kj, vj = k_ref[rows, :], v_ref[rows, :]
                    qj = q_ref[rows, :] * (128 ** -0.5)
                    bd, gd, _, _ = _gate_cols(ba_ref[rows, :], al_ref[...], dt_ref[...], d * hh + h, 2 * hh + d * hh + h)
                    chains.append((d, n, rows))
                    for lst, val in ((qs, qj), (k, kj), (v, vj), (kk, _mm_nt(kj, kj)), (qk, _mm_nt(qj, kj)), (gt, gd), (beta, bd)):
                        lst.append(val)
            z = _gdn_prep(qs, k, v, gt, beta, kk, qk, [d == 1 for d, _, _ in chains], tick)
            for x, (d, n, rows) in enumerate(chains):
                u_o[d, rows, :] = z["u"][x]
                w_o[d, rows, :] = z["w"][x].astype(bf16)
                qd_o[d, rows, :] = z["qd"][x].astype(bf16)
                kd_o[d, rows, :] = z["kd"][x].astype(bf16)
                p_o[d, 0, n] = z["p"][x].astype(bf16)
                t_o[d, 0, n] = z["tm"][x]
                dec_o[d, 0, n] = z["dec"][x]
                cd_o[d, 0, n] = jnp.broadcast_to(z["cd"][x], (8, LANES))

        def steps(g, box):
            for j in range(per):
                sts = box[0]
                ns = [chunk_of(d, g, j) for d in range(2)]
                rows = [pl.ds(pl.multiple_of(n * c, c), c) for n in ns]
                ws = [_mm(w_o[d, rows[d], :], sts[d]) for d in range(2)]
                qs_ = [_mm(qd_o[d, rows[d], :], sts[d]) for d in range(2)]
                yield
                vn = [u_o[d, rows[d], :] - ws[d] for d in range(2)]
                box[0] = tuple(sts[d] * cd_o[d, 0, ns[d]][0:1, :] + _mm_tn(kd_o[d, rows[d], :], vn[d]) for d in range(2))
                for d in range(2):
                    o_ref[rows[d], :] = o_ref[rows[d], :] + qs_[d] + _mm(p_o[d, 0, ns[d]], vn[d])
                yield

        o_ref[...] = jnp.zeros_like(o_ref)
        prep(0)

        def fused(g, sts):
            box = [sts]
            chain = steps(g - 1, box)
            prep(g, lambda: next(chain, None))
            for _ in chain:
                pass
            return box[0]

        z0 = jnp.zeros((128, 128), f32)
        box = [lax.fori_loop(1, groups, fused, (z0, z0))]
        for _ in steps(groups - 1, box):
            pass

    blk = lambda off: pl.BlockSpec((s, 128), functools.partial(lambda b, h, off: (b, off + h), off=off))
    row = pl.BlockSpec((1, LANES), lambda b, h: (0, 0))
    tok2, mat2, cd2, shp = _gdn_specs(cfg)
    return pl.pallas_call(
        body, name="gdn_fwd", grid=(cfg.BL, hh),
        in_specs=[blk(0), blk(hh), blk(2 * hh), pl.BlockSpec((s, LANES), lambda b, h: (b, ba_blk)), row, row],
        out_specs=[pl.BlockSpec((s, 128), lambda b, h: (b, h)), tok2, tok2, tok2, tok2, mat2, mat2, mat2, cd2],
        out_shape=[jax.ShapeDtypeStruct((cfg.T, cfg.DN), f32), shp["tok32"], shp["tok16"], shp["tok16"], shp["tok16"],
                   shp["mat16"], shp["mat32"], shp["mat32"], shp["cd"]],
        compiler_params=_params(("arbitrary", "arbitrary")),
    )(qkv, qkv, qkv, proj, alog_row, dt_row)


def _lru_gates(xc, wa, wx, ba, bx, lam):
    ra = _sigmoid(_mm(xc, wa) + ba)
    ig = _sigmoid(_mm(xc, wx) + bx)
    sp = _softplus(-lam)
    la = -LRU_C * ra * sp
    a = jnp.exp(la)
    m = jnp.sqrt(_nexpm1(2.0 * la))
    return dict(ra=ra, ig=ig, sp=sp, a=a, m=m, gx=ig * xc)


def _lru_specs(cfg, outer_b):
    s = cfg.S
    lx_blk = cfg.LXO // LANES
    if outer_b:
        ix = lambda f: (lambda b, ct: f(b, ct))
    else:
        ix = lambda f: (lambda ct, b: f(b, ct))
    return dict(
        lx=pl.BlockSpec((s, LANES), ix(lambda b, ct: (b, lx_blk + ct))),
        tok=pl.BlockSpec((s, LANES), ix(lambda b, ct: (b, ct))),
        cw=pl.BlockSpec((4, LANES), ix(lambda b, ct: (0, ct))),
        row=pl.BlockSpec((1, LANES), ix(lambda b, ct: (0, ct))),
        w=pl.BlockSpec((2, 1, LANES, LANES), ix(lambda b, ct: (0, ct, 0, 0))),
        two=pl.BlockSpec((2, LANES), ix(lambda b, ct: (0, ct))),
    )


def _k_lru_fwd(cfg, proj, conv_w, conv_b, wa, wx, ba, bx, lam):
    def body(lx_ref, cw_ref, cb_ref, wa_ref, wx_ref, ba_ref, bx_ref, lam_ref, o_ref, a_s, b_s, h_s):
        xc = _conv_fwd(lx_ref[...], cw_ref, 2) + cb_ref[...]
        for d in range(2):
            z = _lru_gates(xc, wa_ref[d, 0], wx_ref[d, 0], ba_ref[d:d + 1, :], bx_ref[d:d + 1, :], lam_ref[d:d + 1, :])
            a_s[d] = z["a"]
            b_s[d] = z["m"] * z["gx"]
        _scan_refs([(a_s.at[d], b_s.at[d], h_s.at[d], d == 1) for d in range(2)], cfg.S)
        o_ref[...] = h_s[0] + h_s[1]

    sp = _lru_specs(cfg, True)
    return pl.pallas_call(
        body, name="lru_fwd", grid=(cfg.BL, cfg.LW // LANES),
        in_specs=[sp["lx"], sp["cw"], sp["row"], sp["w"], sp["w"], sp["two"], sp["two"], sp["two"]],
        out_specs=sp["tok"], out_shape=jax.ShapeDtypeStruct((cfg.T, cfg.LW), f32),
        scratch_shapes=[pltpu.VMEM((2, cfg.S, LANES), f32)] * 3,
        compiler_params=_params(("arbitrary", "arbitrary")),
    )(proj, conv_w, conv_b, wa, wx, ba, bx, lam)


def _mix_parts(cfg, o, z, lg, hs, dng, lrg):
    heads = []
    for h in range(cfg.H):
        sl = slice(h * 128, (h + 1) * 128)
        y, xh, inv = _rms_fwd(o[:, sl], dng)
        sz, dsz = _silu(z[:, sl])
        heads.append((y, xh, inv, sz, dsz))
    gl, dgl = _gelu(lg)
    y2, xh2, inv2 = _rms_fwd(gl * hs, lrg)
    return heads, (y2, xh2, inv2, gl, dgl)


def _k_mix(cfg, o, proj, hs, r, dng, lrg, wout):
    dn = cfg.DN

    def body(o_ref, z_ref, lg_ref, hs_ref, r_ref, dng_ref, lrg_ref, w_ref, mix_ref, out_ref):
        heads, lru = _mix_parts(cfg, o_ref[...], z_ref[...], lg_ref[...], hs_ref[...], dng_ref[...], lrg_ref[...])
        for h, (y, _, _, sz, _) in enumerate(heads):
            mix_ref[:, h * 128:(h + 1) * 128] = (y * sz).astype(bf16)
        mix_ref[:, dn:] = lru[0].astype(bf16)
        out_ref[...] = r_ref[...] + jnp.dot(mix_ref[...], w_ref[...], preferred_element_type=f32)

    return _row_call("mix_out", body, cfg.T, cfg.TM,
                     [(o, dn, 0), (proj, dn, cfg.ZO // dn), (proj, cfg.LW, cfg.LGO // cfg.LW), (hs, cfg.LW, 0), (r, cfg.D, 0)],
                     [dng, lrg, wout], [(cfg.D, bf16), (cfg.D, f32)], [])


def _k_ffn_a(cfg, r, g2, wg, wu):
    def body(r_ref, g_ref, wg_ref, wu_ref, h_ref, gp_ref, up_ref):
        y, _, _ = _rms_fwd(r_ref[...], g_ref[...])
        hb = y.astype(bf16)
        h_ref[...] = hb
        gp_ref[...] = _mm_nt(hb, wg_ref[...])
        up_ref[...] = _mm_nt(hb, wu_ref[...])

    return _row_call("ffn_in", body, cfg.T, cfg.TMF, [(r, cfg.D, 0)], [g2, wg, wu],
                     [(cfg.D, bf16), (cfg.FF, f32), (cfg.FF, f32)], [])


def _k_ffn_b(cfg, gp, up, conv_w, conv_b):
    s, ft = cfg.S, cfg.FT

    def body(gp_ref, up_ref, w_ref, b_ref, o_ref):
        gate = _conv_fwd(gp_ref[...], w_ref, 1) + b_ref[...]
        gl, _ = _gelu(gate)
        o_ref[...] = (gl * up_ref[...]).astype(bf16)

    tok = pl.BlockSpec((s, ft), lambda b, j: (b, j))
    return pl.pallas_call(
        body, name="ffn_act", grid=(cfg.BL, cfg.FF // ft),
        in_specs=[tok, tok, pl.BlockSpec((3, ft), lambda b, j: (0, j)), pl.BlockSpec((1, ft), lambda b, j: (0, j))],
        out_specs=tok, out_shape=jax.ShapeDtypeStruct((cfg.T, cfg.FF), bf16),
        compiler_params=_params(("arbitrary", "arbitrary")),
    )(gp, up, conv_w, conv_b)


def _k_ffn_c(cfg, act, r, wd):
    def body(a_ref, r_ref, w_ref, o_ref):
        o_ref[...] = r_ref[...] + jnp.dot(a_ref[...], w_ref[...], preferred_element_type=f32)

    return _row_call("ffn_out", body, cfg.T, cfg.TM, [(act, cfg.FF, 0), (r, cfg.D, 0)], [wd], [(cfg.D, f32)], [])[0]


def _k_ple(cfg, r, p, gp, wpg, bg, wpp):
    def body(r_ref, p_ref, g_ref, wg_ref, bg_ref, wp_ref, pn_ref, o_ref):
        x = r_ref[...]
        y, _, _ = _rms_fwd(x, g_ref[...])
        pn = y.astype(bf16)
        pn_ref[...] = pn
        pg = _sigmoid(jnp.dot(pn, wg_ref[...], preferred_element_type=f32) + bg_ref[...])
        o_ref[...] = x + pg * _mm_nt(p_ref[...], wp_ref[...])

    return _row_call("ple", body, cfg.T, cfg.TM, [(r, cfg.D, 0), (p, cfg.PD, 0)], [gp, wpg, bg, wpp],
                     [(cfg.D, bf16), (cfg.D, f32)], [])


def _k_loss(cfg, r, tgt, gf):
    d = cfg.D

    def body(r_ref, t_ref, g_ref, dr_ref, loss_ref, dg_ref):
        @pl.when(pl.program_id(0) == 0)
        def _():
            loss_ref[...] = jnp.zeros_like(loss_ref)
            dg_ref[...] = jnp.zeros_like(dg_ref)

        g = g_ref[...]
        y, xh, inv = _rms_fwd(r_ref[...], g)
        err = y - t_ref[...]
        loss_ref[...] = loss_ref[...] + (0.5 / d) * jnp.sum(err * err)
        dx, dg = _rms_bwd(err * (1.0 / d), xh, inv, g)
        dr_ref[...] = dx
        dg_ref[...] = dg_ref[...] + dg

    return _row_call("loss_head", body, cfg.T, cfg.TM, [(r, d, 0), (tgt, d, 0)], [gf], [(d, f32)],
                     [((1, LANES), f32), ((1, d), f32)])


def _zero_at_first(cond, *refs):
    @pl.when(cond)
    def _():
        for r in refs:
            r[...] = jnp.zeros_like(r)


def _b_ple(cfg, dr3, r2, p, gp, wpg, bg, wpp):
    d = cfg.D

    def body(dr_ref, r_ref, p_ref, g_ref, wg_ref, bg_ref, wp_ref, dr2_ref, dr2b_ref, dlog_ref, dpp_ref, dgp_ref, dbg_ref):
        _zero_at_first(pl.program_id(0) == 0, dgp_ref, dbg_ref)
        g = g_ref[...]
        dr = dr_ref[...]
        y, xh, inv = _rms_fwd(r_ref[...], g)
        pg = _sigmoid(jnp.dot(y.astype(bf16), wg_ref[...], preferred_element_type=f32) + bg_ref[...])
        pp = _mm_nt(p_ref[...], wp_ref[...])
        dpp_ref[...] = (dr * pg).astype(bf16)
        dlog = dr * pp * pg * (1.0 - pg)
        dlog_ref[...] = dlog.astype(bf16)
        dbg_ref[...] = dbg_ref[...] + jnp.sum(dlog, axis=0, keepdims=True)
        dx, dg = _rms_bwd(_mm_nt(dlog, wg_ref[...]), xh, inv, g)
        dgp_ref[...] = dgp_ref[...] + dg
        dr2_ref[...] = dr + dx
        dr2b_ref[...] = (dr + dx).astype(bf16)

    return _row_call("ple_bwd", body, cfg.T, cfg.TM, [(dr3, d, 0), (r2, d, 0), (p, cfg.PD, 0)], [gp, wpg, bg, wpp],
                     [(d, f32), (d, bf16), (d, bf16), (d, bf16)], [((1, d), f32), ((1, d), f32)])


def _b_ffn_bc(cfg, dr2, wd, gp, up, conv_w, conv_b):
    s, ft, d = cfg.S, cfg.FT, cfg.D

    def body(dr_ref, wd_ref, gp_ref, up_ref, w_ref, b_ref, dgp_ref, dup_ref, dwd_ref, dcw_ref, dcb_ref):
        _zero_at_first(pl.program_id(1) == 0, dwd_ref, dcw_ref, dcb_ref)
        drb = dr_ref[...]
        dact = _mm_nt(drb, wd_ref[...])
        gpre = gp_ref[...]
        up = up_ref[...]
        gl, dgl = _gelu(_conv_fwd(gpre, w_ref, 1) + b_ref[...])
        dup_ref[...] = (dact * gl).astype(bf16)
        dgate = dact * up * dgl
        dcb_ref[...] = dcb_ref[...] + jnp.sum(dgate, axis=0, keepdims=True)
        dx, dws = _conv_bwd(dgate, gpre, w_ref, 1)
        for j, dw in enumerate(dws):
            dcw_ref[j:j + 1, :] = dcw_ref[j:j + 1, :] + dw
        dgp_ref[...] = dx.astype(bf16)
        dwd_ref[...] = dwd_ref[...] + _mm_tn((gl * up).astype(bf16), drb)

    tok = pl.BlockSpec((s, ft), lambda j, b: (b, j))
    return pl.pallas_call(
        body, name="ffn_act_bwd", grid=(cfg.FF // ft, cfg.BL),
        in_specs=[pl.BlockSpec((s, d), lambda j, b: (b, 0)), pl.BlockSpec((ft, d), lambda j, b: (j, 0)), tok, tok,
                  pl.BlockSpec((3, ft), lambda j, b: (0, j)), pl.BlockSpec((1, ft), lambda j, b: (0, j))],
        out_specs=[tok, tok, pl.BlockSpec((ft, d), lambda j, b: (j, 0)), pl.BlockSpec((3, ft), lambda j, b: (0, j)),
                   pl.BlockSpec((1, ft), lambda j, b: (0, j))],
        out_shape=[jax.ShapeDtypeStruct((cfg.T, cfg.FF), bf16), jax.ShapeDtypeStruct((cfg.T, cfg.FF), bf16),
                   jax.ShapeDtypeStruct((cfg.FF, d), f32), jax.ShapeDtypeStruct((3, cfg.FF), f32),
                   jax.ShapeDtypeStruct((1, cfg.FF), f32)],
        compiler_params=_params(("arbitrary", "arbitrary")),
    )(dr2, wd, gp, up, conv_w, conv_b)


def _b_ffn_a(cfg, dgp, dup, dr2, r_mid, g2, wg, wu):
    d = cfg.D

    def body(dgp_ref, dup_ref, dr_ref, r_ref, g_ref, wg_ref, wu_ref, o_ref, dg_ref):
        _zero_at_first(pl.program_id(0) == 0, dg_ref)
        g = g_ref[...]
        _, xh, inv = _rms_fwd(r_ref[...], g)
        dh = _mm(dgp_ref[...], wg_ref[...]) + _mm(dup_ref[...], wu_ref[...])
        dx, dg = _rms_bwd(dh, xh, inv, g)
        dg_ref[...] = dg_ref[...] + dg
        o_ref[...] = dr_ref[...] + dx

    return _row_call("ffn_in_bwd", body, cfg.T, cfg.TMF, [(dgp, cfg.FF, 0), (dup, cfg.FF, 0), (dr2, d, 0), (r_mid, d, 0)],
                     [g2, wg, wu], [(d, f32)], [((1, d), f32)])


def _mm_tn_call(cfg, name, x, dy, tn):
    t, k = x.shape
    n = dy.shape[1]
    tm = cfg.TM
    last = t // tm - 1

    def body(x_ref, dy_ref, o_ref, acc):
        _zero_at_first(pl.program_id(1) == 0, acc)
        acc[...] = acc[...] + _mm_tn(x_ref[...], dy_ref[...])

        @pl.when(pl.program_id(1) == last)
        def _():
            o_ref[...] = acc[...].astype(bf16)

    return pl.pallas_call(
        body, name=name, grid=(n // tn, t // tm),
        in_specs=[pl.BlockSpec((tm, k), lambda j, i: (i, 0)), pl.BlockSpec((tm, tn), lambda j, i: (i, j))],
        out_specs=pl.BlockSpec((k, tn), lambda j, i: (0, j)),
        out_shape=jax.ShapeDtypeStruct((k, n), bf16), scratch_shapes=[pltpu.VMEM((k, tn), f32)],
        compiler_params=_params(("arbitrary", "arbitrary")),
    )(x, dy)


def _b_mix(cfg, dr, o, proj, hs, dng, lrg, wout):
    dn, lw, d = cfg.DN, cfg.LW, cfg.D

    def body(dr_ref, o_ref, z_ref, lg_ref, hs_ref, dng_ref, lrg_ref, w_ref, do_ref, dz_ref, dlg_ref, dhs_ref, ddn_ref, dlr_ref):
        _zero_at_first(pl.program_id(0) == 0, ddn_ref, dlr_ref)
        dng, lrg = dng_ref[...], lrg_ref[...]
        hs = hs_ref[...]
        o, z = o_ref[...], z_ref[...]
        heads, lru = _mix_parts(cfg, o, z, lg_ref[...], hs, dng, lrg)
        drb = dr_ref[...].astype(bf16)
        dmix_dn = _mm_nt(drb, w_ref[0:dn, :])
        dmix_lr = _mm_nt(drb, w_ref[dn:, :])
        dgn = jnp.zeros_like(dng)
        for h, (y, xh, inv, sz, dsz) in enumerate(heads):
            sl = slice(h * 128, (h + 1) * 128)
            dm = dmix_dn[:, sl]
            dz_ref[:, sl] = dm * y * dsz
            dx, dg = _rms_bwd(dm * sz, xh, inv, dng)
            do_ref[:, sl] = dx
            dgn = dgn + dg
        ddn_ref[...] = ddn_ref[...] + dgn
        _, xh2, inv2, gl, dgl = lru
        dx2, dg2 = _rms_bwd(dmix_lr, xh2, inv2, lrg)
        dlr_ref[...] = dlr_ref[...] + dg2
        dlg_ref[...] = dx2 * hs * dgl
        dhs_ref[...] = dx2 * gl

    return _row_call("mix_bwd", body, cfg.T, cfg.TM,
                     [(dr, d, 0), (o, dn, 0), (proj, dn, cfg.ZO // dn), (proj, lw, cfg.LGO // lw), (hs, lw, 0)],
                     [dng, lrg, wout], [(dn, f32), (dn, f32), (lw, f32), (lw, f32)], [((1, 128), f32), ((1, lw), f32)])


def _b_lru(cfg, proj, dhs, conv_w, conv_b, wa, wx, ba, bx, lam):
    def body(lx_ref, dh_ref, cw_ref, cb_ref, wa_ref, wx_ref, ba_ref, bx_ref, lam_ref,
             dlx_ref, dcw_ref, dcb_ref, dwa_ref, dwx_ref, dba_ref, dbx_ref, dlam_ref, a_s, b_s, h_s, an_s, l_s):
        _zero_at_first(pl.program_id(1) == 0, dcw_ref, dcb_ref, dwa_ref, dwx_ref, dba_ref, dbx_ref, dlam_ref)
        lx = lx_ref[...]
        xc = _conv_fwd(lx, cw_ref, 2) + cb_ref[...]
        dxc = jnp.zeros_like(xc)
        gates = []
        for d in range(2):
            z = _lru_gates(xc, wa_ref[d, 0], wx_ref[d, 0], ba_ref[d:d + 1, :], bx_ref[d:d + 1, :], lam_ref[d:d + 1, :])
            a_s[d] = z["a"]
            b_s[d] = z["m"] * z["gx"]
            an_s[d] = _shift(z["a"], -1 if d == 1 else 1, 0.0)
            gates.append(z)
        _scan_refs([(a_s.at[d], b_s.at[d], h_s.at[d], d == 1) for d in range(2)]
                   + [(an_s.at[d], dh_ref, l_s.at[d], d == 0) for d in range(2)], cfg.S)
        for d in range(2):
            rev = d == 1
            lam = lam_ref[d:d + 1, :]
            z = gates[d]
            a, m, ra, ig, sp = z["a"], z["m"], z["ra"], z["ig"], z["sp"]
            lmb = l_s[d]
            h_prev = _shift(h_s[d], 1 if rev else -1, 0.0)
            da = lmb * h_prev
            dm = lmb * z["gx"]
            dgx = lmb * m
            dla = da * a - dm * (a * a) / jnp.maximum(m, 1e-30)
            dra = dla * (-LRU_C) * sp
            dsp = jnp.sum(dla * (-LRU_C) * ra, axis=0, keepdims=True)
            dlam_ref[d:d + 1, :] = dlam_ref[d:d + 1, :] - dsp * _sigmoid(-lam)
            dpa = dra * ra * (1.0 - ra)
            dpx = dgx * xc * ig * (1.0 - ig)
            dba_ref[d:d + 1, :] = dba_ref[d:d + 1, :] + jnp.sum(dpa, axis=0, keepdims=True)
            dbx_ref[d:d + 1, :] = dbx_ref[d:d + 1, :] + jnp.sum(dpx, axis=0, keepdims=True)
            dwa_ref[d, 0] = dwa_ref[d, 0] + _mm_tn(xc, dpa)
            dwx_ref[d, 0] = dwx_ref[d, 0] + _mm_tn(xc, dpx)
            dxc = dxc + dgx * ig + _mm_nt(dpa, wa_ref[d, 0]) + _mm_nt(dpx, wx_ref[d, 0])
        dcb_ref[...] = dcb_ref[...] + jnp.sum(dxc, axis=0, keepdims=True)
        dx, dws = _conv_bwd(dxc, lx, cw_ref, 2)
        for j, dw in enumerate(dws):
            dcw_ref[j:j + 1, :] = dcw_ref[j:j + 1, :] + dw
        dlx_ref[...] = dx

    sp = _lru_specs(cfg, False)
    nct = cfg.LW // LANES
    return pl.pallas_call(
        body, name="lru_bwd", grid=(nct, cfg.BL),
        in_specs=[sp["lx"], sp["tok"], sp["cw"], sp["row"], sp["w"], sp["w"], sp["two"], sp["two"], sp["two"]],
        out_specs=[sp["tok"], sp["cw"], sp["row"], sp["w"], sp["w"], sp["two"], sp["two"], sp["two"]],
        out_shape=[jax.ShapeDtypeStruct((cfg.T, cfg.LW), f32), jax.ShapeDtypeStruct((4, cfg.LW), f32),
                   jax.ShapeDtypeStruct((1, cfg.LW), f32), jax.ShapeDtypeStruct((2, nct, LANES, LANES), f32),
                   jax.ShapeDtypeStruct((2, nct, LANES, LANES), f32), jax.ShapeDtypeStruct((2, cfg.LW), f32),
                   jax.ShapeDtypeStruct((2, cfg.LW), f32), jax.ShapeDtypeStruct((2, cfg.LW), f32)],
        scratch_shapes=[pltpu.VMEM((2, cfg.S, LANES), f32)] * 5,
        compiler_params=_params(("arbitrary", "arbitrary")),
    )(proj, dhs, conv_w, conv_b, wa, wx, ba, bx, lam)


def _b_gdn(cfg, qkv, proj, do, alog_row, dt_row, saved):
    s, c, nc, hh = cfg.S, cfg.C, cfg.NC, cfg.H
    ba_blk = cfg.BAO // LANES
    scale = 128 ** -0.5
    per = 4
    assert nc % per == 0

    def body(q_ref, k_ref, v_ref, ba_ref, do_ref, al_ref, dt_ref, u_i, w_i, qd_i, kd_i, p_i, t_i, dec_i, cd_i,
             dqkv_ref, dba_ref, dal_ref, ddt_ref, vn_s, dvn_s, st_s, dst_s):
        h = pl.program_id(1)
        _zero_at_first((pl.program_id(0) == 0) & (h == 0), dal_ref, ddt_ref)
        _zero_at_first(h == 0, dba_ref)
        lane = lax.broadcasted_iota(jnp.int32, (c, LANES), 1)
        lane1 = lax.broadcasted_iota(jnp.int32, (1, LANES), 1)
        dirs = (0, 1)
        lane_b = [d * hh + h for d in dirs]
        lane_a = [2 * hh + d * hh + h for d in dirs]

        def seq(i, box):
            sts, dst = box[0]
            live = i < nc
            ii = jnp.where(live, i, 0)
            nf = [ii, nc - 1 - ii]
            nb = [nc - 1 - ii, ii]
            sf = [jnp.where(live, n, nc) for n in nf]
            sb = [jnp.where(live, n, nc) for n in nb]
            rf = [pl.ds(pl.multiple_of(n * c, c), c) for n in nf]
            rb = [pl.ds(pl.multiple_of(n * c, c), c) for n in nb]
            for d in dirs:
                st_s[d, sf[d]] = sts[d]
                dst_s[d, sb[d]] = dst[d]
            dob = [do_ref[rb[d], :] for d in dirs]
            vn = [u_i[d, rf[d], :] - _mm(w_i[d, rf[d], :], sts[d]) for d in dirs]
            dvn = [_mm_tn(p_i[d, 0, nb[d]], dob[d]) + _mm(kd_i[d, rb[d], :], dst[d]) for d in dirs]
            yield
            for d in dirs:
                vn_s[d, pl.ds(pl.multiple_of(sf[d] * c, c), c), :] = vn[d]
                dvn_s[d, pl.ds(pl.multiple_of(sb[d] * c, c), c), :] = dvn[d]
            sts = tuple(sts[d] * cd_i[d, 0, nf[d]][0:1, :] + _mm_tn(kd_i[d, rf[d], :], vn[d]) for d in dirs)
            dst = tuple(_mm_tn(qd_i[d, rb[d], :], dob[d]) + cd_i[d, 0, nb[d]][0:1, 0:1] * dst[d]
                        - _mm_tn(w_i[d, rb[d], :], dvn[d]) for d in dirs)
            box[0] = (sts, dst)
            yield

        def bpar(chunks, tick=lambda: None):
            ch = [(d, n) for n in chunks for d in dirs]
            rows = [pl.ds(pl.multiple_of(n * c, c), c) for _, n in ch]
            masks = [_tri_masks(c, d == 1) for d, _ in ch]
            ld = lambda ref: [ref[d, r, :] for (d, _), r in zip(ch, rows)]
            ldm = lambda ref: [ref[d, 0, n] for d, n in ch]
            q, k, v, dob = ([ref[r, :] for r in rows] for ref in (q_ref, k_ref, v_ref, do_ref))
            gates = [_gate_cols(ba_ref[r, :], al_ref[...], dt_ref[...], lane_b[d], lane_a[d]) for (d, _), r in zip(ch, rows)]
            beta, g, aexp, xa = ([gt[x] for gt in gates] for x in range(4))
            st, dst = ([ref[d, n] for d, n in ch] for ref in (st_s, dst_s))
            vn, dvn, u, w = ld(vn_s), ld(dvn_s), ld(u_i), ld(w_i)
            p = [x.astype(f32) for x in ldm(p_i)]
            tm, dec = ldm(t_i), ldm(dec_i)
            cd = [cd_i[d, 0, n][0:1, 0:1] for d, n in ch]
            dp = _map(lambda m, a, b: jnp.where(m[0], _mm_nt(a, b), 0.0), masks, dob, vn)
            dqd = _map(_mm_nt, dob, st)
            dkd = _map(_mm_nt, vn, dst)
            dw = _map(lambda a, b: -_mm_nt(a, b), dvn, st)
            dcd = _map(lambda a, b: jnp.sum(jnp.sum(a * b, axis=1, keepdims=True), axis=0, keepdims=True), st, dst)
            tick()
            gcum = _map(lambda m, x: _cum_mm(m[0], jnp.broadcast_to(x, (c, c)))[:, 0:1], masks, g)
            glast = [jnp.sum(x, axis=0, keepdims=True) for x in g]
            e = [jnp.exp(x) for x in gcum]
            el = _map(lambda a, b: jnp.exp(a - b), glast, gcum)
            qs = [x * scale for x in q]
            kb = _map(lambda a, b: a * b, k, beta)
            a = _map(lambda m, b, ki, dc: jnp.where(m[1], b * _mm_nt(ki, ki) * dc, 0.0), masks, beta, k, dec)
            tick()
            dvb = _map(_hmm3_tn, tm, dvn)
            dkbe = _map(_hmm3_tn, tm, dw)
            tick()
            da = _map(lambda m, x, ui, y, wi: -jnp.where(m[1], _mm_nt(x, ui) + _mm_nt(y, wi), 0.0), masks, dvb, u, dkbe, w)
            g1 = _map(lambda x, y: x * y, da, dec)
            g2 = _map(lambda x, y: x * y, dp, dec)
            dkb = _map(lambda x, ki, y, ei: _mm(x, ki) + y * ei, g1, k, dkbe, e)
            tick()
            dk = _map(lambda x, kbi, y, qi, z, b, t, l: _mm_tn(x, kbi) + _mm_tn(y, qi) + z * b + t * l,
                      g1, kb, g2, qs, dkb, beta, dkd, el)
            dqs = _map(lambda y, ki, x, ei: _mm(y, ki) + x * ei, g2, k, dqd, e)
            tick()
            ddd = _map(lambda x, ai, y, pi: x * ai + y * pi, da, a, dp, p)
            ones = jnp.ones((c, LANES), f32)
            dgcum = _map(lambda x: jnp.sum(x, axis=1, keepdims=True) - _hmm3_tn(x, ones)[:, 0:1], ddd)
            for x, (d, n) in enumerate(ch):
                dbeta = jnp.sum(dvb[x] * v[x], axis=1, keepdims=True) + jnp.sum(dkb[x] * k[x], axis=1, keepdims=True)
                de = jnp.sum(dkbe[x] * kb[x], axis=1, keepdims=True) + jnp.sum(dqd[x] * qs[x], axis=1, keepdims=True)
                del_ = jnp.sum(dkd[x] * k[x], axis=1, keepdims=True)
                dgc = dgcum[x] + de * e[x] - del_ * el[x]
                dglast = jnp.sum(del_ * el[x], axis=0, keepdims=True) + dcd[x] * cd[x]
                dg = _cum_mm(masks[x][2], jnp.broadcast_to(dgc, (c, LANES)))[:, 0:1] + dglast
                r = rows[x]
                if d == 0:
                    dqkv_ref[0, r, :] = dqs[x] * scale
                    dqkv_ref[1, r, :] = dk[x]
                    dqkv_ref[2, r, :] = dvb[x] * beta[x]
                else:
                    dqkv_ref[0, r, :] = dqkv_ref[0, r, :] + dqs[x] * scale
                    dqkv_ref[1, r, :] = dqkv_ref[1, r, :] + dk[x]
                    dqkv_ref[2, r, :] = dqkv_ref[2, r, :] + dvb[x] * beta[x]
                dlb = dbeta * beta[x] * (1.0 - beta[x])
                dalpha = -dg * aexp[x] * _sigmoid(xa[x])
                dba_ref[r, :] = dba_ref[r, :] + jnp.where(lane == lane_b[d], dlb, 0.0) + jnp.where(lane == lane_a[d], dalpha, 0.0)
                dal_ref[...] = dal_ref[...] + jnp.where(lane1 == lane_a[d], jnp.sum(dg * g[x], axis=0, keepdims=True), 0.0)
                ddt_ref[...] = ddt_ref[...] + jnp.where(lane1 == lane_a[d], jnp.sum(dalpha, axis=0, keepdims=True), 0.0)

        half, groups = nc // 2, nc // per

        def group(k):
            return [half - 2 * k - 2, half - 2 * k - 1, half + 2 * k, half + 2 * k + 1]

        def run(gen):
            for _ in gen:
                pass

        def plain(i, carry):
            box = [carry]
            run(seq(i, box))
            return box[0]

        def woven(k, carry):
            box = [carry]

            def two_steps():
                yield from seq(half + 2 + 2 * k, box)
                yield from seq(half + 3 + 2 * k, box)

            chain = two_steps()
            bpar(group(k), lambda: next(chain, None))
            run(chain)
            return box[0]

        z0 = jnp.zeros((128, 128), f32)
        lax.fori_loop(0, groups, woven, lax.fori_loop(0, half + 2, plain, ((z0, z0), (z0, z0))))

    blk = lambda off: pl.BlockSpec((s, 128), functools.partial(lambda b, h, off: (b, off + h), off=off))
    row = pl.BlockSpec((1, LANES), lambda b, h: (0, 0))
    tok2, mat2, cd2, _ = _gdn_specs(cfg)
    return pl.pallas_call(
        body, name="gdn_bwd", grid=(cfg.BL, hh),
        in_specs=[blk(0), blk(hh), blk(2 * hh), pl.BlockSpec((s, LANES), lambda b, h: (b, ba_blk)),
                  pl.BlockSpec((s, 128), lambda b, h: (b, h)), row, row, tok2, tok2, tok2, tok2, mat2, mat2, mat2, cd2],
        out_specs=[pl.BlockSpec((3, s, 128), lambda b, h: (0, b, h)), pl.BlockSpec((s, LANES), lambda b, h: (b, 0)), row, row],
        out_shape=[jax.ShapeDtypeStruct((3, cfg.T, cfg.DN), f32), jax.ShapeDtypeStruct((cfg.T, LANES), f32),
                   jax.ShapeDtypeStruct((1, LANES), f32), jax.ShapeDtypeStruct((1, LANES), f32)],
        scratch_shapes=[pltpu.VMEM((2, s + c, 128), f32)] * 2 + [pltpu.VMEM((2, nc + 1, 128, 128), f32)] * 2,
        compiler_params=_params(("arbitrary", "arbitrary")),
    )(qkv, qkv, qkv, proj, do, alog_row, dt_row, *saved)


def _b_prep(cfg, proj, dqkv, conv_w):
    dn, s = cfg.DN, cfg.S

    def body(x_ref, dy_ref, w_ref, dx_ref, dw_ref):
        _zero_at_first(pl.program_id(1) == 0, dw_ref)
        sec = pl.program_id(0)
        x = x_ref[...]
        c = _conv_fwd(x, w_ref, 2)
        y, dsilu = _silu(c)
        dy = dy_ref[0]
        parts = []
        for h in range(cfg.H):
            sl = slice(h * 128, (h + 1) * 128)
            yh, dyh = y[:, sl], dy[:, sl]
            inv = lax.rsqrt(jnp.sum(yh * yh, axis=1, keepdims=True) + EPS)
            dn_h = inv * dyh - yh * (inv * inv * inv) * jnp.sum(dyh * yh, axis=1, keepdims=True)
            parts.append(jnp.where(sec < 2, dn_h, dyh))
        ds = jnp.concatenate(parts, axis=1) if len(parts) > 1 else parts[0]
        dx, dws = _conv_bwd(ds * dsilu, x, w_ref, 2)
        for j, dw in enumerate(dws):
            dw_ref[j:j + 1, :] = dw_ref[j:j + 1, :] + dw
        dx_ref[...] = dx

    return pl.pallas_call(
        body, name="dn_prep_bwd", grid=(3, cfg.BL),
        in_specs=[pl.BlockSpec((s, dn), lambda j, b: (b, j)), pl.BlockSpec((1, s, dn), lambda j, b: (j, b, 0)),
                  pl.BlockSpec((4, dn), lambda j, b: (0, j))],
        out_specs=[pl.BlockSpec((s, dn), lambda j, b: (b, j)), pl.BlockSpec((4, dn), lambda j, b: (0, j))],
        out_shape=[jax.ShapeDtypeStruct((cfg.T, 3 * dn), f32), jax.ShapeDtypeStruct((4, 3 * dn), f32)],
        compiler_params=_params(("arbitrary", "arbitrary")),
    )(proj, dqkv, conv_w)


def _b_in(cfg, dqkv_pre, dz, dlx, dlg, dba, dr_mid, r_in, g1, wcat):
    d, dn, lw = cfg.D, cfg.DN, cfg.LW

    def body(dq_ref, dz_ref, dlx_ref, dlg_ref, dba_ref, dr_ref, r_ref, g_ref, w_ref, o_ref, dp_ref, dg_ref):
        _zero_at_first(pl.program_id(0) == 0, dg_ref)
        dp_ref[:, 0:cfg.ZO] = dq_ref[...].astype(bf16)
        dp_ref[:, cfg.ZO:cfg.LXO] = dz_ref[...].astype(bf16)
        dp_ref[:, cfg.LXO:cfg.LGO] = dlx_ref[...].astype(bf16)
        dp_ref[:, cfg.LGO:cfg.BAO] = dlg_ref[...].astype(bf16)
        dp_ref[:, cfg.BAO:] = dba_ref[...].astype(bf16)
        g = g_ref[...]
        _, xh, inv = _rms_fwd(r_ref[...], g)
        dx, dg = _rms_bwd(_mm(dp_ref[...], w_ref[...]), xh, inv, g)
        dg_ref[...] = dg_ref[...] + dg
        o_ref[...] = dr_ref[...] + dx

    return _row_call("in_proj_bwd", body, cfg.T, cfg.TM,
                     [(dqkv_pre, 3 * dn, 0), (dz, dn, 0), (dlx, lw, 0), (dlg, lw, 0), (dba, LANES, 0), (dr_mid, d, 0), (r_in, d, 0)],
                     [g1, wcat], [(d, f32), (cfg.PC, bf16)], [((1, d), f32)])


def _adam_call(name, w, g, m, v, tr):
    rows, cols = w.shape
    bc1 = 1.0 - ADAM_B1 ** ADAM_STEP
    bc2 = 1.0 - ADAM_B2 ** ADAM_STEP

    def body(w_ref, g_ref, m_ref, v_ref, d_ref, nm_ref, nv_ref):
        g = g_ref[...]
        m = ADAM_B1 * m_ref[...] + (1.0 - ADAM_B1) * g
        v = ADAM_B2 * v_ref[...] + (1.0 - ADAM_B2) * (g * g)
        nm_ref[...] = m
        nv_ref[...] = v
        d_ref[...] = -ADAM_LR * ((m / bc1) / (jnp.sqrt(v / bc2) + ADAM_EPS) + ADAM_WD * w_ref[...])

    spec = pl.BlockSpec((tr, cols), lambda i: (i, 0))
    return pl.pallas_call(
        body, name=name, grid=(rows // tr,), in_specs=[spec] * 4, out_specs=[spec] * 3,
        out_shape=[jax.ShapeDtypeStruct((rows, cols), f32)] * 3, compiler_params=_params(("arbitrary",)),
    )(w, g, m, v)


def _sum8_call(name, x, tr):
    _, rows, cols = x.shape

    def body(x_ref, o_ref):
        acc = x_ref[0].astype(f32)
        for j in range(1, N_DEV):
            acc = acc + x_ref[j].astype(f32)
        o_ref[...] = acc

    return pl.pallas_call(
        body, name=name, grid=(rows // tr,), in_specs=[pl.BlockSpec((N_DEV, tr, cols), lambda i: (0, i, 0))],
        out_specs=pl.BlockSpec((tr, cols), lambda i: (i, 0)), out_shape=jax.ShapeDtypeStruct((rows, cols), f32),
        compiler_params=_params(("arbitrary",)),
    )(x)


def _all_gather(name, shards):
    na = len(shards)

    def body(*refs):
        xs, outs = refs[:na], refs[na:2 * na]
        send_sems, recv_sems, local_sems = refs[2 * na:]
        x, y, c = lax.axis_index("x"), lax.axis_index("y"), lax.axis_index("c")
        me, sibling = (x, y, c), (x, y, 1 - c)
        chips = [(1 - x, y), (x, 1 - y), (1 - x, 1 - y)]

        def copy(a, k, block, to, src=None):
            px, py, pc = block
            slot = outs[a].at[4 * px + 2 * py + pc]
            return pltpu.make_async_remote_copy(
                src_ref=slot if src is None else src, dst_ref=slot,
                send_sem=send_sems.at[a, k], recv_sem=recv_sems.at[a, k], device_id=to, device_id_type=MESH)

        mine = [pltpu.make_async_copy(xs[a], outs[a].at[4 * x + 2 * y + c], local_sems.at[a]) for a in range(na)]
        for cp in mine:
            cp.start()
        first = []
        for a in range(na):
            first.append(copy(a, 0, me, sibling, src=xs[a]))
            first += [copy(a, 1 + j, me, (*chip, c), src=xs[a]) for j, chip in enumerate(chips)]
        for cp in first:
            cp.start()
        passed = []
        for j, chip in enumerate(chips):
            for a in range(na):
                copy(a, 1 + j, (*chip, c), me).wait_recv()
                cp = copy(a, 4 + j, (*chip, c), sibling)
                cp.start()
                passed.append(cp)
        for a in range(na):
            copy(a, 0, sibling, me).wait_recv()
            for j, chip in enumerate(chips):
                copy(a, 4 + j, (*chip, 1 - c), me).wait_recv()
        for cp in first + passed:
            cp.wait_send()
        for cp in mine:
            cp.wait()

    hbm = pl.BlockSpec(memory_space=pltpu.HBM)
    return pl.pallas_call(
        body, name=name, out_shape=[jax.ShapeDtypeStruct((N_DEV,) + s.shape, s.dtype) for s in shards],
        in_specs=[hbm] * na, out_specs=[hbm] * na,
        scratch_shapes=[pltpu.SemaphoreType.DMA((na, 7)), pltpu.SemaphoreType.DMA((na, 7)), pltpu.SemaphoreType.DMA((na,))],
    )(*shards)


def _peer_list():
    x, y, c = lax.axis_index("x"), lax.axis_index("y"), lax.axis_index("c")
    return 4 * x + 2 * y + c, [(x ^ (k >> 2), y ^ ((k >> 1) & 1), c ^ (k & 1)) for k in range(1, N_DEV)]


_HBM = pl.BlockSpec(memory_space=pltpu.HBM)
_SEM = pl.BlockSpec(memory_space=pltpu.SEMAPHORE)
_EFFECT = pltpu.SideEffectType.DATAFLOW_SIDE_EFFECTING


def _exchange_copies(xs, lands, send_sem, recv_sem, gather):
    me, peers = _peer_list()
    return [pltpu.make_async_remote_copy(
        src_ref=xs[a] if gather else xs[a].at[4 * px + 2 * py + pc], dst_ref=lands[a].at[me],
        send_sem=send_sem.at[7 * a + k], recv_sem=recv_sem.at[7 * a + k], device_id=(px, py, pc), device_id_type=MESH)
        for k, (px, py, pc) in enumerate(peers) for a in range(len(xs))]


def _exchange_start(name, blocks, gather=False):
    na = len(blocks)

    def body(*refs):
        for cp in _exchange_copies(refs[:na], refs[na:2 * na], refs[2 * na], refs[2 * na + 1], gather):
            cp.start()
        refs[-1][...] = jnp.zeros_like(refs[-1])

    lands = [jax.ShapeDtypeStruct((N_DEV,) + b.shape if gather else b.shape, b.dtype) for b in blocks]
    hbm = [pltpu.HBM(b.shape, b.dtype) for b in blocks] + [pltpu.HBM(b.shape, b.dtype) for b in lands]
    send_sem, recv_sem, *thru, token = pl.pallas_call(
        body, name=name,
        out_shape=(pltpu.SemaphoreType.DMA((7 * na,)), pltpu.SemaphoreType.DMA((7 * na,)), *hbm,
                   jax.ShapeDtypeStruct((8, LANES), f32)),
        in_specs=[_HBM] * (2 * na), out_specs=(_SEM, _SEM, *([_HBM] * (2 * na)), pl.BlockSpec(memory_space=pltpu.VMEM)),
        input_output_aliases={i: 2 + i for i in range(2 * na)},
        compiler_params=pltpu.CompilerParams(has_side_effects=_EFFECT),
    )(*[pltpu.with_memory_space_constraint(b, pltpu.HBM) for b in blocks],
      *[pltpu.with_memory_space_constraint(lax.empty(b.shape, b.dtype), pltpu.HBM) for b in lands])
    return send_sem, recv_sem, thru, token


def _exchange_wait(name, send_sem, recv_sem, thru, after, gather=False):
    na = len(thru) // 2

    def body(*refs):
        for cp in _exchange_copies(refs[:na], refs[na:2 * na], refs[2 * na], refs[2 * na + 1], gather):
            cp.wait_send()
            cp.wait_recv()

    return pl.pallas_call(
        body, name=name, out_shape=tuple(pltpu.HBM(t.shape, t.dtype) for t in thru),
        in_specs=[_HBM] * (2 * na) + [_SEM, _SEM, pl.BlockSpec(memory_space=pl.ANY)], out_specs=tuple([_HBM] * (2 * na)),
        input_output_aliases={i: i for i in range(2 * na)},
        compiler_params=pltpu.CompilerParams(has_side_effects=_EFFECT),
    )(*thru, send_sem, recv_sem, after)[na:]


def _layer_fwd(cfg, w, r, p):
    h1, proj = _k_in(cfg, r, w["norm1_g"], w["wcat_t"])
    qkv = _k_prep(cfg, proj, w["dn_conv_w"])
    o, *gdn_saved = _k_gdn_fwd(cfg, qkv, proj, w["alog_row"], w["dt_row"])
    hs = _k_lru_fwd(cfg, proj, w["lru_conv_w"], w["lru_conv_b"], w["wa"], w["wx"], w["lru_ba"], w["lru_bx"], w["lru_lambda"])
    if "late" in w:
        w.update(w.pop("late")(hs))
    mix, r_mid = _k_mix(cfg, o, proj, hs, r, w["dn_norm_g"], w["lru_norm_g"], w["w_out"])
    h2, gp, up = _k_ffn_a(cfg, r_mid, w["norm2_g"], w["ffn_wg_t"], w["ffn_wu_t"])
    act = _k_ffn_b(cfg, gp, up, w["ffn_conv_w"], w["ffn_conv_b"])
    r2 = _k_ffn_c(cfg, act, r_mid, w["ffn_wd"])
    pn, r3 = _k_ple(cfg, r2, p, w["ple_norm_g"], w["ple_wg"], w["ple_bg"], w["ple_wp_t"])
    saved = dict(r=r, h1=h1, proj=proj, qkv=qkv, o=o, gdn=gdn_saved, hs=hs, mix=mix, r_mid=r_mid, h2=h2, gp=gp, up=up,
                 r2=r2, pn=pn, p=p)
    return r3, saved


def _layer_bwd(cfg, w, sv, dr3, early=None):
    g = {}
    dt = min(512, cfg.D)
    dr2, dr2b, dlog, dpp, g["ple_norm_g"], g["ple_bg"] = _b_ple(cfg, dr3, sv["r2"], sv["p"], w["ple_norm_g"], w["ple_wg"], w["ple_bg"], w["ple_wp_t"])
    g["ple_wg"] = _mm_tn_call(cfg, "d_ple_wg", sv["pn"], dlog, dt)
    g["ple_wp_t"] = _mm_tn_call(cfg, "d_ple_wp", dpp, sv["p"], cfg.PD)
    dgp, dup, g["ffn_wd"], g["ffn_conv_w"], g["ffn_conv_b"] = _b_ffn_bc(cfg, dr2b, w["ffn_wd"], sv["gp"], sv["up"], w["ffn_conv_w"], w["ffn_conv_b"])
    dr_mid, g["norm2_g"] = _b_ffn_a(cfg, dgp, dup, dr2, sv["r_mid"], w["norm2_g"], w["ffn_wg_t"], w["ffn_wu_t"])
    g["ffn_wg_t"] = _mm_tn_call(cfg, "d_ffn_wg", dgp, sv["h2"], dt)
    g["ffn_wu_t"] = _mm_tn_call(cfg, "d_ffn_wu", dup, sv["h2"], dt)
    if early is not None:
        w = dict(w, dn_norm_g=w["dn_norm_g"] + early(EARLY, g)[0, 0])
    do, dz, dlg, dhs, g["dn_norm_g"], g["lru_norm_g"] = _b_mix(cfg, dr_mid, sv["o"], sv["proj"], sv["hs"], w["dn_norm_g"], w["lru_norm_g"], w["w_out"])
    g["w_out"] = _mm_tn_call(cfg, "d_w_out", sv["mix"], dr_mid, dt)
    if early is not None:
        w = dict(w, lru_conv_b=w["lru_conv_b"] + early(("w_out",), g)[0, 0])
    dlx, g["lru_conv_w"], g["lru_conv_b"], g["wa"], g["wx"], g["lru_ba"], g["lru_bx"], g["lru_lambda"] = _b_lru(
        cfg, sv["proj"], dhs, w["lru_conv_w"], w["lru_conv_b"], w["wa"], w["wx"], w["lru_ba"], w["lru_bx"], w["lru_lambda"])
    dqkv, dba, g["alog_row"], g["dt_row"] = _b_gdn(cfg, sv["qkv"], sv["proj"], do, w["alog_row"], w["dt_row"], sv["gdn"])
    dqkv_pre, g["dn_conv_w"] = _b_prep(cfg, sv["proj"], dqkv, w["dn_conv_w"])
    dr, dproj, g["norm1_g"] = _b_in(cfg, dqkv_pre, dz, dlx, dlg, dba, dr_mid, sv["r"], w["norm1_g"], w["wcat_t"])
    g["wcat_t"] = _mm_tn_call(cfg, "d_w_in", dproj, sv["h1"], dt)
    return dr, g


BIG = ("w_in", "w_out", "ffn_wg", "ffn_wu", "ffn_wd", "ple_wg", "ple_wp")
BIG_T = {"w_in": True, "w_out": False, "ffn_wg": True, "ffn_wu": True, "ffn_wd": False, "ple_wg": False, "ple_wp": True}
BIG_OPERAND = {"w_in": "wcat_t", "w_out": "w_out", "ffn_wg": "ffn_wg_t", "ffn_wu": "ffn_wu_t", "ffn_wd": "ffn_wd",
               "ple_wg": "ple_wg", "ple_wp": "ple_wp_t"}
EARLY = ("ffn_wg", "ffn_wu", "ffn_wd", "ple_wg", "ple_wp")
SMALL_SHARDED = ("dn_conv_w", "lru_conv_w", "lru_ba", "lru_bx", "lru_lambda", "ffn_conv_w")
SMALL_REPL = ("norm1_g", "dn_a_log", "dn_dt_bias", "dn_norm_g", "lru_conv_b", "lru_wa", "lru_wx", "lru_norm_g", "norm2_g",
              "ffn_conv_b", "ple_norm_g", "ple_bg", "final_g")
WEIGHTS = ("norm1_g", "w_in", "dn_conv_w", "dn_a_log", "dn_dt_bias", "dn_norm_g", "lru_conv_w", "lru_conv_b", "lru_wa",
           "lru_ba", "lru_wx", "lru_bx", "lru_lambda", "lru_norm_g", "w_out", "norm2_g", "ffn_wg", "ffn_wu", "ffn_conv_w",
           "ffn_conv_b", "ffn_wd", "ple_norm_g", "ple_wg", "ple_bg", "ple_wp", "final_g")


def _pad_rows(flat, cols, mult):
    n = flat.shape[0]
    rows = -(-n // cols)
    rows = -(-rows // mult) * mult
    return jnp.pad(flat, (0, rows * cols - n)).reshape(rows, cols)


def _pack(arrs, cols, mult, dtype):
    return _pad_rows(jnp.concatenate([a.reshape(-1).astype(dtype) for a in arrs]), cols, mult)


def _unpack(flat, shapes):
    out, off = [], 0
    for shp in shapes:
        n = math.prod(shp)
        piece = flat[off:off + n]
        if n < 4096:
            piece = lax.optimization_barrier(piece)
        out.append(piece.reshape(shp))
        off += n
    return out


def _unpack8(g8, shapes, axes):
    out, off = [], 0
    for shp, ax in zip(shapes, axes):
        n = math.prod(shp)
        a = g8[:, off:off + n].reshape((N_DEV,) + tuple(shp))
        a = jnp.moveaxis(a, 0, ax)
        out.append(a.reshape(shp[:ax] + (N_DEV * shp[ax],) + shp[ax + 1:]))
        off += n
    return out


def _wcat_t_from_w_in_t(cfg, wt):
    nba = 4 * cfg.H
    pad = jnp.zeros((LANES - nba, wt.shape[1]), wt.dtype)
    return jnp.concatenate([wt[:cfg.LXO], wt[cfg.LXO + nba:], wt[cfg.LXO:cfg.LXO + nba], pad], axis=0)


def _w_in_t_from_wcat_t(cfg, wc):
    nba = 4 * cfg.H
    return jnp.concatenate([wc[:cfg.LXO], wc[cfg.BAO:cfg.BAO + nba], wc[cfg.LXO:cfg.BAO]], axis=0)


def _gate_row(cfg, a):
    h2 = 2 * cfg.H
    return jnp.concatenate([jnp.zeros((1, h2), f32), a.reshape(1, h2), jnp.zeros((1, LANES - 2 * h2), f32)], axis=1)


def _blockdiag(cfg, w):
    w = w.reshape(2, cfg.NB // 2, 2, 64, 64)
    z = jnp.zeros_like(w[:, :, 0])
    top = jnp.concatenate([w[:, :, 0], z], axis=-1)
    bot = jnp.concatenate([z, w[:, :, 1]], axis=-1)
    return jnp.concatenate([top, bot], axis=-2).astype(bf16)


def _unblockdiag(cfg, g):
    a = g[:, :, :64, :64]
    b = g[:, :, 64:, 64:]
    return jnp.stack([a, b], axis=2).reshape(2, cfg.NB, 64, 64)


def _big_operands(cfg, big):
    return {BIG_OPERAND[n]: (_wcat_t_from_w_in_t(cfg, v) if n == "w_in" else v) for n, v in big.items()}


def _layer_operands(cfg, big, small, i):
    return dict(
        _big_operands(cfg, big),
        norm1_g=small["norm1_g"][i][None], dn_conv_w=small["dn_conv_w"][i], alog_row=_gate_row(cfg, small["dn_a_log"][i]),
        dt_row=_gate_row(cfg, small["dn_dt_bias"][i]), dn_norm_g=small["dn_norm_g"][i][None],
        lru_conv_w=small["lru_conv_w"][i], lru_conv_b=small["lru_conv_b"][i][None],
        wa=_blockdiag(cfg, small["lru_wa"][i]), wx=_blockdiag(cfg, small["lru_wx"][i]),
        lru_ba=small["lru_ba"][i], lru_bx=small["lru_bx"][i], lru_lambda=small["lru_lambda"][i],
        lru_norm_g=small["lru_norm_g"][i][None], norm2_g=small["norm2_g"][i][None], ffn_conv_w=small["ffn_conv_w"][i],
        ffn_conv_b=small["ffn_conv_b"][i][None], ple_norm_g=small["ple_norm_g"][i][None], ple_bg=small["ple_bg"][i][None],
    )


def _small_grads_to_problem(cfg, g):
    h = cfg.H
    return dict(
        norm1_g=g["norm1_g"][0], dn_conv_w=g["dn_conv_w"],
        dn_a_log=g["alog_row"][0, 2 * h:4 * h].reshape(2, h), dn_dt_bias=g["dt_row"][0, 2 * h:4 * h].reshape(2, h),
        dn_norm_g=g["dn_norm_g"][0], lru_conv_w=g["lru_conv_w"], lru_conv_b=g["lru_conv_b"][0],
        lru_wa=_unblockdiag(cfg, g["wa"]), lru_wx=_unblockdiag(cfg, g["wx"]), lru_ba=g["lru_ba"], lru_bx=g["lru_bx"],
        lru_lambda=g["lru_lambda"], lru_norm_g=g["lru_norm_g"][0], norm2_g=g["norm2_g"][0], ffn_conv_w=g["ffn_conv_w"],
        ffn_conv_b=g["ffn_conv_b"][0], ple_norm_g=g["ple_norm_g"][0], ple_bg=g["ple_bg"][0],
    )


def _local_step(cfg, get_big, small, x, p, target, on_big_grads):
    r = x.reshape(cfg.T, cfg.D)
    ops, saved = [], []
    for i in range(cfg.L):
        big, token, late = get_big(i, r)
        w = _layer_operands(cfg, big, small, i)
        if token is not None:
            w["norm1_g"] = w["norm1_g"] + token[0, 0]
        if late is not None:
            w["late"] = late
        r, sv = _layer_fwd(cfg, w, r, p[i].reshape(cfg.T, cfg.PD))
        ops.append(w)
        saved.append(sv)
    dr, loss, dgf = _k_loss(cfg, r, target.reshape(cfg.T, cfg.D), small["final_g"][None])
    gsmall = [None] * cfg.L
    for i in reversed(range(cfg.L)):
        first = EARLY + ("w_out",) if i == 0 else ()
        early = (lambda names, g: on_big_grads("%d%s" % (i, names[0]), {n: g[BIG_OPERAND[n]] for n in names})) if first else None
        dr, g = _layer_bwd(cfg, ops[i], saved[i], dr, early)
        token = on_big_grads("%d" % i, {n: (_w_in_t_from_wcat_t(cfg, g["wcat_t"]) if n == "w_in" else g[BIG_OPERAND[n]])
                                        for n in BIG if n not in first})
        if i > 0:
            ops[i - 1]["ple_bg"] = ops[i - 1]["ple_bg"] + token[0, 0]
        gsmall[i] = _small_grads_to_problem(cfg, g)
    gs = {k: jnp.stack([gl[k] for gl in gsmall]) for k in gsmall[0]}
    gs["final_g"] = dgf[0]
    return loss, dr, gs


def _row_tile(rows, limit=512):
    best = rows
    for t in range(8, min(rows, limit) + 1, 8):
        if rows % t == 0:
            best = t
    return best if best <= limit or rows <= limit else rows


def _adam_group(name, ws, gs, ms, vs, cols, tr):
    shapes = [w.shape for w in ws]
    pk = lambda arrs: _pack(arrs, cols, tr, f32)
    w2 = pk(ws)
    d, nm, nv = _adam_call(name, w2, pk(gs), pk(ms), pk(vs), min(tr, w2.shape[0]))
    return [_unpack(a.reshape(-1), shapes) for a in (d, nm, nv)]


def kernel(x, p, norm1_g, w_in, dn_conv_w, dn_a_log, dn_dt_bias, dn_norm_g, lru_conv_w, lru_conv_b, lru_wa, lru_ba, lru_wx, lru_bx, lru_lambda, lru_norm_g, w_out, norm2_g, ffn_wg, ffn_wu, ffn_conv_w, ffn_conv_b, ffn_wd, ple_norm_g, ple_wg, ple_bg, ple_wp, final_g, loss_target, m_norm1_g, m_w_in, m_dn_conv_w, m_dn_a_log, m_dn_dt_bias, m_dn_norm_g, m_lru_conv_w, m_lru_conv_b, m_lru_wa, m_lru_ba, m_lru_wx, m_lru_bx, m_lru_lambda, m_lru_norm_g, m_w_out, m_norm2_g, m_ffn_wg, m_ffn_wu, m_ffn_conv_w, m_ffn_conv_b, m_ffn_wd, m_ple_norm_g, m_ple_wg, m_ple_bg, m_ple_wp, m_final_g, v_norm1_g, v_w_in, v_dn_conv_w, v_dn_a_log, v_dn_dt_bias, v_dn_norm_g, v_lru_conv_w, v_lru_conv_b, v_lru_wa, v_lru_ba, v_lru_wx, v_lru_bx, v_lru_lambda, v_lru_norm_g, v_w_out, v_norm2_g, v_ffn_wg, v_ffn_wu, v_ffn_conv_w, v_ffn_conv_b, v_ffn_wd, v_ple_norm_g, v_ple_wg, v_ple_bg, v_ple_wp, v_final_g):
    cfg = CFG
    a = dict(locals())
    wl = {n: a[n] for n in WEIGHTS}
    ml = {n: a["m_" + n] for n in WEIGHTS}
    vl = {n: a["v_" + n] for n in WEIGHTS}
    me = 4 * lax.axis_index("x") + 2 * lax.axis_index("y") + lax.axis_index("c")
    nl = cfg.L

    blocks = [(jnp.swapaxes(wl[n], 1, 2) if BIG_T[n] else wl[n]).astype(bf16) for n in BIG]
    ss_shapes = [wl[n].shape for n in SMALL_SHARDED]
    first, s8 = _all_gather("gather_weights_0", [blocks[0][0], _pack([wl[n] for n in SMALL_SHARDED], LANES, 8, f32)])
    small = dict(zip(SMALL_SHARDED, _unpack8(s8.reshape(N_DEV, -1), ss_shapes, [2] * len(ss_shapes))))
    small.update({n: wl[n] for n in SMALL_REPL})

    def start_gather(key, i, names, behind):
        shards, _ = lax.optimization_barrier(([blk[i] for n, blk in zip(BIG, blocks) if n in names], behind))
        return _exchange_start("gather_start_" + key, shards, gather=True)

    def wait_gather(key, i, names, started, behind):
        send_sem, recv_sem, thru, _ = started
        lands = _exchange_wait("gather_wait_" + key, send_sem, recv_sem, thru, behind, gather=True)
        own = [blk[i][None] for n, blk in zip(BIG, blocks) if n in names]
        full = [lax.dynamic_update_slice_in_dim(land, o, me, 0) for land, o in zip(lands, own)]
        return {n: f.reshape(N_DEV * f.shape[1], f.shape[2]) for n, f in zip([n for n in BIG if n in names], full)}

    ahead = {}

    def get_big(i, r):
        if i > 0:
            big = wait_gather("%d" % i, i, BIG, ahead.pop(i), r)
            if i + 1 < nl:
                ahead[i + 1] = start_gather("%d" % (i + 1), i + 1, BIG, big["w_in"])
            return big, (ahead[i + 1][3] if i + 1 < nl else None), None
        rest = start_gather("0r", 0, BIG[1:], first)

        def late(x):
            big = wait_gather("0r", 0, BIG[1:], rest, x)
            ops = _big_operands(cfg, big)
            if nl > 1:
                ahead[1] = start_gather("1", 1, BIG, big["w_out"])
                ops["norm2_g"] = small["norm2_g"][0][None] + ahead[1][3][0, 0]
            return ops

        return {"w_in": first.reshape(N_DEV * first.shape[1], first.shape[2])}, rest[3], late

    pending = {}

    def on_big_grads(key, g):
        send = [g[n].reshape((N_DEV,) + blk.shape[1:]).astype(bf16) for n, blk in zip(BIG, blocks) if n in g]
        own = [lax.dynamic_index_in_dim(sd, me, 0, keepdims=True) for sd in send]
        send_sem, recv_sem, thru, token = _exchange_start("exchange_start_" + key, send)
        pending[key] = (send_sem, recv_sem, thru, own, [n for n in BIG if n in g])
        return token

    loss_part, dr, gsmall = _local_step(cfg, get_big, small, x, p, loss_target, on_big_grads)
    grad_x = dr.reshape(x.shape)

    sums = {n: [None] * nl for n in BIG}
    for key in sorted(pending):
        send_sem, recv_sem, thru, own, names = pending[key]
        lands = _exchange_wait("exchange_wait_" + key, send_sem, recv_sem, thru, dr)
        for n, land, o in zip(names, lands, own):
            slots = lax.dynamic_update_slice_in_dim(land, o, me, 0)
            sums[n][int(key[0])] = _sum8_call("sum_%s_%s" % (n, key), slots, slots.shape[1])
    gl = {}
    for n in BIG:
        s = jnp.stack(sums[n])
        gl[n] = jnp.swapaxes(s, 1, 2) if BIG_T[n] else s

    small_names = SMALL_REPL + SMALL_SHARDED
    narrow = ("lru_wa", "lru_wx")
    wide = [n for n in small_names if n not in narrow]
    wide_shapes = [gsmall[n].shape for n in wide]
    sv = _pack([gsmall[n] for n in wide] + [loss_part[0, 0:1]], LANES, 512, f32)
    nv = _pack([gsmall[n] for n in narrow], LANES, 512, bf16)
    sv8, nv8 = _all_gather("gather_small_grads", [sv, nv])
    small_sum = _sum8_call("sum_small", sv8, 512).reshape(-1)
    narrow_sum = _sum8_call("sum_small_narrow", nv8, 512).reshape(-1)
    gl.update(zip(wide, _unpack(small_sum, wide_shapes)))
    gl.update(zip(narrow, _unpack(narrow_sum, [gsmall[n].shape for n in narrow])))
    loss = small_sum[sum(math.prod(s) for s in wide_shapes)]
    for n in SMALL_SHARDED:
        shard = wl[n].shape[2]
        gl[n] = lax.dynamic_slice_in_dim(gl[n], me * shard, shard, axis=2)

    outs = {}
    for n in BIG:
        shp = wl[n].shape
        two = lambda t: t.reshape(-1, shp[-1])
        d, nm, nv = _adam_call("adam_" + n, two(wl[n]), two(gl[n]), two(ml[n]), two(vl[n]), _row_tile(math.prod(shp[:-1])))
        outs[n] = (d.reshape(shp), nm.reshape(shp), nv.reshape(shp))
    d, nm, nv = _adam_group("adam_small", [wl[n] for n in small_names], [gl[n] for n in small_names],
                            [ml[n] for n in small_names], [vl[n] for n in small_names], LANES, 64)
    for j, n in enumerate(small_names):
        outs[n] = (d[j], nm[j], nv[j])
    return (loss, grad_x, *[gl[n] for n in WEIGHTS], *[outs[n][0] for n in WEIGHTS], *[outs[n][1] for n in WEIGHTS],
            *[outs[n][2] for n in WEIGHTS])
```

```python
import functools
import math

import jax
import jax.numpy as jnp
from jax import lax
from jax.experimental import pallas as pl
from jax.experimental.pallas import tpu as pltpu

f32 = jnp.float32
bf16 = jnp.bfloat16
MESH = pl.DeviceIdType.MESH

N_DEV = 8
LANES = 128
EPS = 1e-6
LRU_C = 8.0
ADAM_LR, ADAM_B1, ADAM_B2, ADAM_EPS, ADAM_WD, ADAM_STEP = 0.001, 0.9, 0.999, 1e-08, 0.01, 10
VMEM_LIMIT = 56 * 1024 * 1024


class Cfg:
    def __init__(self, d_model=1024, bl=4, seq=2048, depth=4, heads=4, lru_width=512, d_ff=2816, ple=256,
                 tm=512, tm_ffn=256, ff_tile=256):
        self.D, self.BL, self.S, self.L, self.H = d_model, bl, seq, depth, heads
        self.DH = 128
        self.DN = heads * self.DH
        self.LW = lru_width
        self.NB = lru_width // 64
        self.FF, self.PD = d_ff, ple
        self.C = 64
        self.NC = seq // self.C
        self.T = bl * seq
        self.TM = min(tm, self.T)
        self.TMF = min(tm_ffn, self.T)
        self.TMW = min(2 * tm, self.T)
        self.FT = ff_tile
        self.ZO = 3 * self.DN
        self.LXO = 4 * self.DN
        self.LGO = self.LXO + self.LW
        self.BAO = self.LGO + self.LW
        self.PC = self.BAO + LANES
        self.IN_COLS = 4 * self.DN + 4 * heads + 2 * self.LW


CFG = Cfg()


def _mm(a, b):
    return jnp.dot(a.astype(bf16), b.astype(bf16), preferred_element_type=f32)


def _mm_nt(a, b):
    return lax.dot_general(a.astype(bf16), b.astype(bf16), (((1,), (1,)), ((), ())), preferred_element_type=f32)


def _mm_tn(a, b):
    return lax.dot_general(a.astype(bf16), b.astype(bf16), (((0,), (0,)), ((), ())), preferred_element_type=f32)


def _split2(a):
    hi = a.astype(bf16)
    return hi, (a - hi.astype(f32)).astype(bf16)


def _hmm3(a, b, dims=(((1,), (0,)), ((), ()))):
    ah, al = _split2(a)
    bh, bl = _split2(b)
    dot = functools.partial(lax.dot_general, dimension_numbers=dims, preferred_element_type=f32)
    return dot(ah, bh) + dot(ah, bl) + dot(al, bh)


def _hmm3_tn(a, b):
    return _hmm3(a, b, (((0,), (0,)), ((), ())))


def _cum_mm(mask, x, dims=(((1,), (0,)), ((), ()))):
    m = mask.astype(bf16)
    x1 = x.astype(bf16)
    r = x - x1.astype(f32)
    x2 = r.astype(bf16)
    x3 = (r - x2.astype(f32)).astype(bf16)
    dot = functools.partial(lax.dot_general, dimension_numbers=dims, preferred_element_type=f32)
    return dot(m, x1) + dot(m, x2) + dot(m, x3)


def _rms_fwd(x, g):
    inv = lax.rsqrt(jnp.mean(x * x, axis=-1, keepdims=True) + EPS)
    xh = x * inv
    return xh * g, xh, inv


def _rms_bwd(dy, xh, inv, g):
    dxh = dy * g
    dx = inv * (dxh - xh * jnp.mean(dxh * xh, axis=-1, keepdims=True))
    dg = jnp.sum(dy * xh, axis=0, keepdims=True)
    return dx, dg


def _sigmoid(x):
    return 1.0 / (1.0 + jnp.exp(-x))


def _softplus(x):
    return jnp.maximum(x, 0.0) + jnp.log(1.0 + jnp.exp(-jnp.abs(x)))


def _silu(x):
    s = _sigmoid(x)
    return x * s, s * (1.0 + x * (1.0 - s))


_GC = math.sqrt(2.0 / math.pi)


def _gelu(x):
    t = jnp.tanh(_GC * (x + 0.044715 * x * x * x))
    y = 0.5 * x * (1.0 + t)
    dy = 0.5 * (1.0 + t) + 0.5 * x * (1.0 - t * t) * _GC * (1.0 + 3.0 * 0.044715 * x * x)
    return y, dy


def _nexpm1(x):
    ser = -x * (1.0 + x * 0.5 * (1.0 + x * (1.0 / 3.0) * (1.0 + x * 0.25 * (1.0 + x * 0.2))))
    return jnp.where(x > -0.1, ser, 1.0 - jnp.exp(x))


def _shift(x, s, fill=0.0):
    if s == 0:
        return x
    n = x.shape[0]
    t = lax.broadcasted_iota(jnp.int32, x.shape, 0)
    r = pltpu.roll(x, (-s) % n, 0)
    return jnp.where((t + s >= 0) & (t + s < n), r, fill)


def _conv_fwd(x, w_ref, left):
    k = w_ref.shape[0]
    out = _shift(x, -left) * w_ref[0:1, :]
    for j in range(1, k):
        out = out + _shift(x, j - left) * w_ref[j:j + 1, :]
    return out


def _conv_bwd(dout, x, w_ref, left):
    k = w_ref.shape[0]
    dx = None
    dws = []
    for j in range(k):
        term = _shift(dout, -(j - left)) * w_ref[j:j + 1, :]
        dx = term if dx is None else dx + term
        dws.append(jnp.sum(dout * _shift(x, j - left), axis=0, keepdims=True))
    return dx, dws


def _scan_refs(scans, n):
    blk = 64
    nb = n // blk
    sub = lax.broadcasted_iota(jnp.int32, (blk, LANES), 0) & 7

    def local(a, b, rev):
        for d in (1, 2, 4):
            ok = (sub < 8 - d) if rev else (sub >= d)
            sh = (blk - d) if rev else d
            b = a * jnp.where(ok, pltpu.roll(b, sh, 0), 0.0) + b
            a = a * jnp.where(ok, pltpu.roll(a, sh, 0), 1.0)
        return a, b

    def body(i, carries):
        new = []
        for (a_ref, b_ref, h_ref, rev), carry in zip(scans, carries):
            j = (nb - 1 - i) if rev else i
            base = pl.multiple_of(j * blk, blk)
            a, b = local(a_ref[pl.ds(base, blk), :], b_ref[pl.ds(base, blk), :], rev)
            order = range(blk // 8 - 1, -1, -1) if rev else range(blk // 8)
            for v in order:
                h = b[8 * v:8 * v + 8, :] + a[8 * v:8 * v + 8, :] * carry
                h_ref[pl.ds(base + 8 * v, 8), :] = h
                carry = h[0:1, :] if rev else h[7:8, :]
            new.append(carry)
        return tuple(new)

    lax.fori_loop(0, nb, body, tuple(jnp.zeros((1, LANES), f32) for _ in scans))


def _tri_masks(c, rev):
    i = lax.broadcasted_iota(jnp.int32, (c, c), 0)
    j = lax.broadcasted_iota(jnp.int32, (c, c), 1)
    incl = (i <= j) if rev else (i >= j)
    strict = (i < j) if rev else (i > j)
    incl_t = (i >= j) if rev else (i <= j)
    return incl, strict, incl_t


def _map(f, *lists):
    return [f(*a) for a in zip(*lists)]


def _tri_inv(mats, tick=lambda: None):
    c = mats[0].shape[0]
    i = lax.broadcasted_iota(jnp.int32, (c, c), 0)
    j = lax.broadcasted_iota(jnp.int32, (c, c), 1)
    eye = jnp.where(i == j, 1.0, 0.0)
    t = [eye - a for a in mats]
    pw = _map(_hmm3, mats, mats)
    for it in range(5):
        tick()
        t = _map(lambda ti, ui: ti + ui, t, _map(_hmm3, t, pw))
        if it < 4:
            pw = _map(_hmm3, pw, pw)
    return t


def _gdn_decay(gs, revs):
    c = gs[0].shape[0]
    masks = [_tri_masks(c, r) for r in revs]
    gb = [jnp.broadcast_to(g, (c, c)) for g in gs]
    mcol = _map(lambda m, x: _cum_mm(m[0], x), masks, gb)
    mrow = _map(lambda m, x: jnp.sum(jnp.where(m[2], x, 0.0), axis=0, keepdims=True), masks, gb)
    dec = _map(lambda m, a, b: jnp.exp(jnp.where(m[0], a - b, -1e30)), masks, mcol, mrow)
    return [m[:, 0:1] for m in mcol], [jnp.sum(g, axis=0, keepdims=True) for g in gs], dec


def _gdn_prep(qs, k, v, g, beta, kk, qk, revs, tick=lambda: None):
    c = k[0].shape[0]
    masks = [_tri_masks(c, r) for r in revs]
    gcum, glast, dec = _gdn_decay(g, revs)
    e = [jnp.exp(x) for x in gcum]
    tick()
    tm = _tri_inv(_map(lambda m, b, x, d: jnp.where(m[1], b * x * d, 0.0), masks, beta, kk, dec), tick)
    tick()
    u = _map(lambda t, vi, b: _hmm3(t, vi * b), tm, v, beta)
    tick()
    w = _map(lambda t, ki, b, ei: _hmm3(t, ki * b * ei), tm, k, beta, e)
    p = _map(lambda m, x, d: jnp.where(m[0], x * d, 0.0), masks, qk, dec)
    return dict(dec=dec, cd=[jnp.exp(x) for x in glast], tm=tm, u=u, w=w, p=p, qd=_map(lambda a, b: a * b, qs, e),
                kd=_map(lambda ki, gl, gc: ki * jnp.exp(gl - gc), k, glast, gcum))


def _lane_pick(x, lane):
    l = lax.broadcasted_iota(jnp.int32, x.shape, 1)
    return jnp.sum(jnp.where(l == lane, x, 0.0), axis=1, keepdims=True)


def _params(sem, vmem=VMEM_LIMIT):
    return pltpu.CompilerParams(dimension_semantics=sem, vmem_limit_bytes=vmem)


def _row_call(name, body, t, tm, row_ins, full_ins, row_outs, acc_outs):
    in_specs = [pl.BlockSpec((tm, w), functools.partial(lambda i, c: (i, c), c=c)) for (_, w, c) in row_ins]
    for a in full_ins:
        in_specs.append(pl.BlockSpec(a.shape, functools.partial(lambda i, n: (0,) * n, n=a.ndim)))
    out_specs = [pl.BlockSpec((tm, w), lambda i: (i, 0)) for (w, _) in row_outs]
    out_shape = [jax.ShapeDtypeStruct((t, w), dt) for (w, dt) in row_outs]
    for shp, dt in acc_outs:
        out_specs.append(pl.BlockSpec(shp, functools.partial(lambda i, n: (0,) * n, n=len(shp))))
        out_shape.append(jax.ShapeDtypeStruct(shp, dt))
    return pl.pallas_call(
        body, name=name, grid=(t // tm,), in_specs=in_specs, out_specs=out_specs, out_shape=out_shape,
        compiler_params=_params(("arbitrary",)),
    )(*[a for (a, _, _) in row_ins], *full_ins)


def _k_in(cfg, r, g1, wcat):
    def body(r_ref, g_ref, w_ref, h_ref, proj_ref):
        y, _, _ = _rms_fwd(r_ref[...], g_ref[...])
        hb = y.astype(bf16)
        h_ref[...] = hb
        proj_ref[...] = _mm_nt(hb, w_ref[...])

    return _row_call("in_proj", body, cfg.T, cfg.TM, [(r, cfg.D, 0)], [g1, wcat],
                     [(cfg.D, bf16), (cfg.PC, f32)], [])


def _k_prep(cfg, proj, conv_w):
    dn, s = cfg.DN, cfg.S

    def body(x_ref, w_ref, o_ref):
        sec = pl.program_id(1)
        c = _conv_fwd(x_ref[...], w_ref, 2)
        y, _ = _silu(c)
        for h in range(cfg.H):
            yh = y[:, h * 128:(h + 1) * 128]
            nh = yh * lax.rsqrt(jnp.sum(yh * yh, axis=1, keepdims=True) + EPS)
            o_ref[:, h * 128:(h + 1) * 128] = jnp.where(sec < 2, nh, yh)

    return pl.pallas_call(
        body, name="dn_prep", grid=(cfg.BL, 3),
        in_specs=[pl.BlockSpec((s, dn), lambda b, j: (b, j)), pl.BlockSpec((4, dn), lambda b, j: (0, j))],
        out_specs=pl.BlockSpec((s, dn), lambda b, j: (b, j)),
        out_shape=jax.ShapeDtypeStruct((cfg.T, 3 * dn), f32),
        compiler_params=_params(("arbitrary", "arbitrary")),
    )(proj, conv_w)


def _gate_cols(ba, alog_row, dt_row, lane_b, lane_a):
    beta = _sigmoid(_lane_pick(ba, lane_b))
    alpha = _lane_pick(ba, lane_a)
    aexp = jnp.exp(_lane_pick(alog_row, lane_a))
    dtb = _lane_pick(dt_row, lane_a)
    xa = alpha + dtb
    g = -aexp * _softplus(xa)
    return beta, g, aexp, xa


def _gdn_specs(cfg):
    s, c, nc = cfg.S, cfg.C, cfg.NC
    tok2 = pl.BlockSpec((2, s, 128), lambda b, h: (0, b, h))
    mat2 = pl.BlockSpec((2, 1, nc, c, c), lambda b, h: (0, b, h, 0, 0))
    cd2 = pl.BlockSpec((2, 1, nc, 8, LANES), lambda b, h: (0, b, h, 0, 0))
    shapes = dict(
        tok32=jax.ShapeDtypeStruct((2, cfg.T, cfg.DN), f32), tok16=jax.ShapeDtypeStruct((2, cfg.T, cfg.DN), bf16),
        mat32=jax.ShapeDtypeStruct((2, cfg.BL, cfg.H * nc, c, c), f32), mat16=jax.ShapeDtypeStruct((2, cfg.BL, cfg.H * nc, c, c), bf16),
        cd=jax.ShapeDtypeStruct((2, cfg.BL, cfg.H * nc, 8, LANES), f32))
    return tok2, mat2, cd2, shapes


def _k_gdn_fwd(cfg, qkv, proj, alog_row, dt_row):
    s, c, nc, hh = cfg.S, cfg.C, cfg.NC, cfg.H
    ba_blk = cfg.BAO // LANES
    per = 4 if nc % 4 == 0 else 1

    def body(q_ref, k_ref, v_ref, ba_ref, al_ref, dt_ref, o_ref, u_o, w_o, qd_o, kd_o, p_o, t_o, dec_o, cd_o):
        h = pl.program_id(1)
        groups = nc // per

        def chunk_of(d, g, j):
            return (nc - 1 - (g * per + j)) if d == 1 else (g * per + j)

        def prep(g, tick=lambda: None):
            chains, qs, k, v, kk, qk, gt, beta = [], [], [], [], [], [], [], []
            for j in range(per):
                for d in range(2):
                    n = chunk_of(d, g, j)
                    rows = pl.ds(pl.multiple_of(n * c, c), c)
                    kj, vj = k_ref[rows, :], v_ref[rows, :]
                    qj = q_ref[rows, :] * (128 ** -0.5)
                    bd, gd, _, _ = _gate_cols(ba_ref[rows, :], al_ref[...], dt_ref[...], d * hh + h, 2 * hh + d * hh + h)
                    chains.append((d, n, rows))
                    for lst, val in ((qs, qj), (k, kj), (v, vj), (kk, _mm_nt(kj, kj)), (qk, _mm_nt(qj, kj)), (gt, gd), (beta, bd)):
                        lst.append(val)
            z = _gdn_prep(qs, k, v, gt, beta, kk, qk, [d == 1 for d, _, _ in chains], tick)
            for x, (d, n, rows) in enumerate(chains):
                u_o[d, rows, :] = z["u"][x]
                w_o[d, rows, :] = z["w"][x].astype(bf16)
                qd_o[d, rows, :] = z["qd"][x].astype(bf16)
                kd_o[d, rows, :] = z["kd"][x].astype(bf16)
                p_o[d, 0, n] = z["p"][x].astype(bf16)
                t_o[d, 0, n] = z["tm"][x]
                dec_o[d, 0, n] = z["dec"][x]
                cd_o[d, 0, n] = jnp.broadcast_to(z["cd"][x], (8, LANES))

        def steps(g, box):
            for j in range(per):
                sts = box[0]
                ns = [chunk_of(d, g, j) for d in range(2)]
                rows = [pl.ds(pl.multiple_of(n * c, c), c) for n in ns]
                ws = [_mm(w_o[d, rows[d], :], sts[d]) for d in range(2)]
                qs_ = [_mm(qd_o[d, rows[d], :], sts[d]) for d in range(2)]
                yield
                vn = [u_o[d, rows[d], :] - ws[d] for d in range(2)]
                box[0] = tuple(sts[d] * cd_o[d, 0, ns[d]][0:1, :] + _mm_tn(kd_o[d, rows[d], :], vn[d]) for d in range(2))
                for d in range(2):
                    o_ref[rows[d], :] = o_ref[rows[d], :] + qs_[d] + _mm(p_o[d, 0, ns[d]], vn[d])
                yield

        o_ref[...] = jnp.zeros_like(o_ref)
        prep(0)

        def fused(g, sts):
            box = [sts]
            chain = steps(g - 1, box)
            prep(g, lambda: next(chain, None))
            for _ in chain:
                pass
            return box[0]

        z0 = jnp.zeros((128, 128), f32)
        box = [lax.fori_loop(1, groups, fused, (z0, z0))]
        for _ in steps(groups - 1, box):
            pass

    blk = lambda off: pl.BlockSpec((s, 128), functools.partial(lambda b, h, off: (b, off + h), off=off))
    row = pl.BlockSpec((1, LANES), lambda b, h: (0, 0))
    tok2, mat2, cd2, shp = _gdn_specs(cfg)
    return pl.pallas_call(
        body, name="gdn_fwd", grid=(cfg.BL, hh),
        in_specs=[blk(0), blk(hh), blk(2 * hh), pl.BlockSpec((s, LANES), lambda b, h: (b, ba_blk)), row, row],
        out_specs=[pl.BlockSpec((s, 128), lambda b, h: (b, h)), tok2, tok2, tok2, tok2, mat2, mat2, mat2, cd2],
        out_shape=[jax.ShapeDtypeStruct((cfg.T, cfg.DN), f32), shp["tok32"], shp["tok16"], shp["tok16"], shp["tok16"],
                   shp["mat16"], shp["mat32"], shp["mat32"], shp["cd"]],
        compiler_params=_params(("arbitrary", "arbitrary")),
    )(qkv, qkv, qkv, proj, alog_row, dt_row)


def _lru_gates(xc, wa, wx, ba, bx, lam):
    ra = _sigmoid(_mm(xc, wa) + ba)
    ig = _sigmoid(_mm(xc, wx) + bx)
    sp = _softplus(-lam)
    la = -LRU_C * ra * sp
    a = jnp.exp(la)
    m = jnp.sqrt(_nexpm1(2.0 * la))
    return dict(ra=ra, ig=ig, sp=sp, a=a, m=m, gx=ig * xc)


def _lru_specs(cfg, outer_b):
    s = cfg.S
    lx_blk = cfg.LXO // LANES
    if outer_b:
        ix = lambda f: (lambda b, ct: f(b, ct))
    else:
        ix = lambda f: (lambda ct, b: f(b, ct))
    return dict(
        lx=pl.BlockSpec((s, LANES), ix(lambda b, ct: (b, lx_blk + ct))),
        tok=pl.BlockSpec((s, LANES), ix(lambda b, ct: (b, ct))),
        cw=pl.BlockSpec((4, LANES), ix(lambda b, ct: (0, ct))),
        row=pl.BlockSpec((1, LANES), ix(lambda b, ct: (0, ct))),
        w=pl.BlockSpec((2, 1, LANES, LANES), ix(lambda b, ct: (0, ct, 0, 0))),
        two=pl.BlockSpec((2, LANES), ix(lambda b, ct: (0, ct))),
    )


def _k_lru_fwd(cfg, proj, conv_w, conv_b, wa, wx, ba, bx, lam):
    def body(lx_ref, cw_ref, cb_ref, wa_ref, wx_ref, ba_ref, bx_ref, lam_ref, o_ref, a_s, b_s, h_s):
        xc = _conv_fwd(lx_ref[...], cw_ref, 2) + cb_ref[...]
        for d in range(2):
            z = _lru_gates(xc, wa_ref[d, 0], wx_ref[d, 0], ba_ref[d:d + 1, :], bx_ref[d:d + 1, :], lam_ref[d:d + 1, :])
            a_s[d] = z["a"]
            b_s[d] = z["m"] * z["gx"]
        _scan_refs([(a_s.at[d], b_s.at[d], h_s.at[d], d == 1) for d in range(2)], cfg.S)
        o_ref[...] = h_s[0] + h_s[1]

    sp = _lru_specs(cfg, True)
    return pl.pallas_call(
        body, name="lru_fwd", grid=(cfg.BL, cfg.LW // LANES),
        in_specs=[sp["lx"], sp["cw"], sp["row"], sp["w"], sp["w"], sp["two"], sp["two"], sp["two"]],
        out_specs=sp["tok"], out_shape=jax.ShapeDtypeStruct((cfg.T, cfg.LW), f32),
        scratch_shapes=[pltpu.VMEM((2, cfg.S, LANES), f32)] * 3,
        compiler_params=_params(("arbitrary", "arbitrary")),
    )(proj, conv_w, conv_b, wa, wx, ba, bx, lam)


def _mix_parts(cfg, o, z, lg, hs, dng, lrg):
    heads = []
    for h in range(cfg.H):
        sl = slice(h * 128, (h + 1) * 128)
        y, xh, inv = _rms_fwd(o[:, sl], dng)
        sz, dsz = _silu(z[:, sl])
        heads.append((y, xh, inv, sz, dsz))
    gl, dgl = _gelu(lg)
    y2, xh2, inv2 = _rms_fwd(gl * hs, lrg)
    return heads, (y2, xh2, inv2, gl, dgl)


def _k_mix(cfg, o, proj, hs, r, dng, lrg, wout):
    dn = cfg.DN

    def body(o_ref, z_ref, lg_ref, hs_ref, r_ref, dng_ref, lrg_ref, w_ref, mix_ref, out_ref):
        heads, lru = _mix_parts(cfg, o_ref[...], z_ref[...], lg_ref[...], hs_ref[...], dng_ref[...], lrg_ref[...])
        for h, (y, _, _, sz, _) in enumerate(heads):
            mix_ref[:, h * 128:(h + 1) * 128] = (y * sz).astype(bf16)
        mix_ref[:, dn:] = lru[0].astype(bf16)
        out_ref[...] = r_ref[...] + jnp.dot(mix_ref[...], w_ref[...], preferred_element_type=f32)

    return _row_call("mix_out", body, cfg.T, cfg.TM,
                     [(o, dn, 0), (proj, dn, cfg.ZO // dn), (proj, cfg.LW, cfg.LGO // cfg.LW), (hs, cfg.LW, 0), (r, cfg.D, 0)],
                     [dng, lrg, wout], [(cfg.D, bf16), (cfg.D, f32)], [])


def _k_ffn_a(cfg, r, g2, wg, wu):
    def body(r_ref, g_ref, wg_ref, wu_ref, h_ref, gp_ref, up_ref):
        y, _, _ = _rms_fwd(r_ref[...], g_ref[...])
        hb = y.astype(bf16)
        h_ref[...] = hb
        gp_ref[...] = _mm_nt(hb, wg_ref[...])
        up_ref[...] = _mm_nt(hb, wu_ref[...])

    return _row_call("ffn_in", body, cfg.T, cfg.TMF, [(r, cfg.D, 0)], [g2, wg, wu],
                     [(cfg.D, bf16), (cfg.FF, f32), (cfg.FF, f32)], [])


def _k_ffn_b(cfg, gp, up, conv_w, conv_b):
    s, ft = cfg.S, cfg.FT

    def body(gp_ref, up_ref, w_ref, b_ref, o_ref):
        gate = _conv_fwd(gp_ref[...], w_ref, 1) + b_ref[...]
        gl, _ = _gelu(gate)
        o_ref[...] = (gl * up_ref[...]).astype(bf16)

    tok = pl.BlockSpec((s, ft), lambda b, j: (b, j))
    return pl.pallas_call(
        body, name="ffn_act", grid=(cfg.BL, cfg.FF // ft),
        in_specs=[tok, tok, pl.BlockSpec((3, ft), lambda b, j: (0, j)), pl.BlockSpec((1, ft), lambda b, j: (0, j))],
        out_specs=tok, out_shape=jax.ShapeDtypeStruct((cfg.T, cfg.FF), bf16),
        compiler_params=_params(("arbitrary", "arbitrary")),
    )(gp, up, conv_w, conv_b)


def _k_ffn_c(cfg, act, r, wd):
    def body(a_ref, r_ref, w_ref, o_ref):
        o_ref[...] = r_ref[...] + jnp.dot(a_ref[...], w_ref[...], preferred_element_type=f32)

    return _row_call("ffn_out", body, cfg.T, cfg.TM, [(act, cfg.FF, 0), (r, cfg.D, 0)], [wd], [(cfg.D, f32)], [])[0]


def _k_ple(cfg, r, p, gp, wpg, bg, wpp):
    def body(r_ref, p_ref, g_ref, wg_ref, bg_ref, wp_ref, pn_ref, o_ref):
        x = r_ref[...]
        y, _, _ = _rms_fwd(x, g_ref[...])
        pn = y.astype(bf16)
        pn_ref[...] = pn
        pg = _sigmoid(jnp.dot(pn, wg_ref[...], preferred_element_type=f32) + bg_ref[...])
        o_ref[...] = x + pg * _mm_nt(p_ref[...], wp_ref[...])

    return _row_call("ple", body, cfg.T, cfg.TM, [(r, cfg.D, 0), (p, cfg.PD, 0)], [gp, wpg, bg, wpp],
                     [(cfg.D, bf16), (cfg.D, f32)], [])


def _k_loss(cfg, r, tgt, gf):
    d = cfg.D

    def body(r_ref, t_ref, g_ref, dr_ref, loss_ref, dg_ref):
        @pl.when(pl.program_id(0) == 0)
        def _():
            loss_ref[...] = jnp.zeros_like(loss_ref)
            dg_ref[...] = jnp.zeros_like(dg_ref)

        g = g_ref[...]
        y, xh, inv = _rms_fwd(r_ref[...], g)
        err = y - t_ref[...]
        loss_ref[...] = loss_ref[...] + (0.5 / d) * jnp.sum(err * err)
        dx, dg = _rms_bwd(err * (1.0 / d), xh, inv, g)
        dr_ref[...] = dx
        dg_ref[...] = dg_ref[...] + dg

    return _row_call("loss_head", body, cfg.T, cfg.TM, [(r, d, 0), (tgt, d, 0)], [gf], [(d, f32)],
                     [((1, LANES), f32), ((1, d), f32)])


def _zero_at_first(cond, *refs):
    @pl.when(cond)
    def _():
        for r in refs:
            r[...] = jnp.zeros_like(r)


def _b_ple(cfg, dr3, r2, p, gp, wpg, bg, wpp):
    d = cfg.D

    def body(dr_ref, r_ref, p_ref, g_ref, wg_ref, bg_ref, wp_ref, dr2_ref, dr2b_ref, dlog_ref, dpp_ref, dgp_ref, dbg_ref):
        _zero_at_first(pl.program_id(0) == 0, dgp_ref, dbg_ref)
        g = g_ref[...]
        dr = dr_ref[...]
        y, xh, inv = _rms_fwd(r_ref[...], g)
        pg = _sigmoid(jnp.dot(y.astype(bf16), wg_ref[...], preferred_element_type=f32) + bg_ref[...])
        pp = _mm_nt(p_ref[...], wp_ref[...])
        dpp_ref[...] = (dr * pg).astype(bf16)
        dlog = dr * pp * pg * (1.0 - pg)
        dlog_ref[...] = dlog.astype(bf16)
        dbg_ref[...] = dbg_ref[...] + jnp.sum(dlog, axis=0, keepdims=True)
        dx, dg = _rms_bwd(_mm_nt(dlog, wg_ref[...]), xh, inv, g)
        dgp_ref[...] = dgp_ref[...] + dg
        dr2_ref[...] = dr + dx
        dr2b_ref[...] = (dr + dx).astype(bf16)

    return _row_call("ple_bwd", body, cfg.T, cfg.TM, [(dr3, d, 0), (r2, d, 0), (p, cfg.PD, 0)], [gp, wpg, bg, wpp],
                     [(d, f32), (d, bf16), (d, bf16), (d, bf16)], [((1, d), f32), ((1, d), f32)])


def _b_ffn_bc(cfg, dr2, wd, gp, up, conv_w, conv_b):
    s, ft, d = cfg.S, cfg.FT, cfg.D

    def body(dr_ref, wd_ref, gp_ref, up_ref, w_ref, b_ref, dgp_ref, dup_ref, dwd_ref, dcw_ref, dcb_ref):
        _zero_at_first(pl.program_id(1) == 0, dwd_ref, dcw_ref, dcb_ref)
        drb = dr_ref[...]
        dact = _mm_nt(drb, wd_ref[...])
        gpre = gp_ref[...]
        up = up_ref[...]
        gl, dgl = _gelu(_conv_fwd(gpre, w_ref, 1) + b_ref[...])
        dup_ref[...] = (dact * gl).astype(bf16)
        dgate = dact * up * dgl
        dcb_ref[...] = dcb_ref[...] + jnp.sum(dgate, axis=0, keepdims=True)
        dx, dws = _conv_bwd(dgate, gpre, w_ref, 1)
        for j, dw in enumerate(dws):
            dcw_ref[j:j + 1, :] = dcw_ref[j:j + 1, :] + dw
        dgp_ref[...] = dx.astype(bf16)
        dwd_ref[...] = dwd_ref[...] + _mm_tn((gl * up).astype(bf16), drb)

    tok = pl.BlockSpec((s, ft), lambda j, b: (b, j))
    return pl.pallas_call(
        body, name="ffn_act_bwd", grid=(cfg.FF // ft, cfg.BL),
        in_specs=[pl.BlockSpec((s, d), lambda j, b: (b, 0)), pl.BlockSpec((ft, d), lambda j, b: (j, 0)), tok, tok,
                  pl.BlockSpec((3, ft), lambda j, b: (0, j)), pl.BlockSpec((1, ft), lambda j, b: (0, j))],
        out_specs=[tok, tok, pl.BlockSpec((ft, d), lambda j, b: (j, 0)), pl.BlockSpec((3, ft), lambda j, b: (0, j)),
                   pl.BlockSpec((1, ft), lambda j, b: (0, j))],
        out_shape=[jax.ShapeDtypeStruct((cfg.T, cfg.FF), bf16), jax.ShapeDtypeStruct((cfg.T, cfg.FF), bf16),
                   jax.ShapeDtypeStruct((cfg.FF, d), f32), jax.ShapeDtypeStruct((3, cfg.FF), f32),
                   jax.ShapeDtypeStruct((1, cfg.FF), f32)],
        compiler_params=_params(("arbitrary", "arbitrary")),
    )(dr2, wd, gp, up, conv_w, conv_b)


def _b_ffn_a(cfg, dgp, dup, dr2, r_mid, g2, wg, wu):
    d = cfg.D

    def body(dgp_ref, dup_ref, dr_ref, r_ref, g_ref, wg_ref, wu_ref, o_ref, dg_ref):
        _zero_at_first(pl.program_id(0) == 0, dg_ref)
        g = g_ref[...]
        _, xh, inv = _rms_fwd(r_ref[...], g)
        dh = _mm(dgp_ref[...], wg_ref[...]) + _mm(dup_ref[...], wu_ref[...])
        dx, dg = _rms_bwd(dh, xh, inv, g)
        dg_ref[...] = dg_ref[...] + dg
        o_ref[...] = dr_ref[...] + dx

    return _row_call("ffn_in_bwd", body, cfg.T, cfg.TMF, [(dgp, cfg.FF, 0), (dup, cfg.FF, 0), (dr2, d, 0), (r_mid, d, 0)],
                     [g2, wg, wu], [(d, f32)], [((1, d), f32)])


def _mm_tn_call(cfg, name, x, dy, tn):
    t, k = x.shape
    n = dy.shape[1]
    tm = cfg.TMW
    last = t // tm - 1

    def body(x_ref, dy_ref, o_ref, acc):
        _zero_at_first(pl.program_id(1) == 0, acc)
        acc[...] = acc[...] + _mm_tn(x_ref[...], dy_ref[...])

        @pl.when(pl.program_id(1) == last)
        def _():
            o_ref[...] = acc[...].astype(bf16)

    return pl.pallas_call(
        body, name=name, grid=(n // tn, t // tm),
        in_specs=[pl.BlockSpec((tm, k), lambda j, i: (i, 0)), pl.BlockSpec((tm, tn), lambda j, i: (i, j))],
        out_specs=pl.BlockSpec((k, tn), lambda j, i: (0, j)),
        out_shape=jax.ShapeDtypeStruct((k, n), bf16), scratch_shapes=[pltpu.VMEM((k, tn), f32)],
        compiler_params=_params(("arbitrary", "arbitrary")),
    )(x, dy)


def _b_mix(cfg, dr, o, proj, hs, dng, lrg, wout):
    dn, lw, d = cfg.DN, cfg.LW, cfg.D

    def body(dr_ref, o_ref, z_ref, lg_ref, hs_ref, dng_ref, lrg_ref, w_ref, do_ref, dz_ref, dlg_ref, dhs_ref, ddn_ref, dlr_ref):
        _zero_at_first(pl.program_id(0) == 0, ddn_ref, dlr_ref)
        dng, lrg = dng_ref[...], lrg_ref[...]
        hs = hs_ref[...]
        o, z = o_ref[...], z_ref[...]
        heads, lru = _mix_parts(cfg, o, z, lg_ref[...], hs, dng, lrg)
        drb = dr_ref[...].astype(bf16)
        dmix_dn = _mm_nt(drb, w_ref[0:dn, :])
        dmix_lr = _mm_nt(drb, w_ref[dn:, :])
        dgn = jnp.zeros_like(dng)
        for h, (y, xh, inv, sz, dsz) in enumerate(heads):
            sl = slice(h * 128, (h + 1) * 128)
            dm = dmix_dn[:, sl]
            dz_ref[:, sl] = dm * y * dsz
            dx, dg = _rms_bwd(dm * sz, xh, inv, dng)
            do_ref[:, sl] = dx
            dgn = dgn + dg
        ddn_ref[...] = ddn_ref[...] + dgn
        _, xh2, inv2, gl, dgl = lru
        dx2, dg2 = _rms_bwd(dmix_lr, xh2, inv2, lrg)
        dlr_ref[...] = dlr_ref[...] + dg2
        dlg_ref[...] = dx2 * hs * dgl
        dhs_ref[...] = dx2 * gl

    return _row_call("mix_bwd", body, cfg.T, cfg.TM,
                     [(dr, d, 0), (o, dn, 0), (proj, dn, cfg.ZO // dn), (proj, lw, cfg.LGO // lw), (hs, lw, 0)],
                     [dng, lrg, wout], [(dn, f32), (dn, f32), (lw, f32), (lw, f32)], [((1, 128), f32), ((1, lw), f32)])


def _b_lru(cfg, proj, dhs, conv_w, conv_b, wa, wx, ba, bx, lam):
    def body(lx_ref, dh_ref, cw_ref, cb_ref, wa_ref, wx_ref, ba_ref, bx_ref, lam_ref,
             dlx_ref, dcw_ref, dcb_ref, dwa_ref, dwx_ref, dba_ref, dbx_ref, dlam_ref, a_s, b_s, h_s, an_s, l_s):
        _zero_at_first(pl.program_id(1) == 0, dcw_ref, dcb_ref, dwa_ref, dwx_ref, dba_ref, dbx_ref, dlam_ref)
        lx = lx_ref[...]
        xc = _conv_fwd(lx, cw_ref, 2) + cb_ref[...]
        dxc = jnp.zeros_like(xc)
        gates = []
        for d in range(2):
            z = _lru_gates(xc, wa_ref[d, 0], wx_ref[d, 0], ba_ref[d:d + 1, :], bx_ref[d:d + 1, :], lam_ref[d:d + 1, :])
            a_s[d] = z["a"]
            b_s[d] = z["m"] * z["gx"]
            an_s[d] = _shift(z["a"], -1 if d == 1 else 1, 0.0)
            gates.append(z)
        _scan_refs([(a_s.at[d], b_s.at[d], h_s.at[d], d == 1) for d in range(2)]
                   + [(an_s.at[d], dh_ref, l_s.at[d], d == 0) for d in range(2)], cfg.S)
        for d in range(2):
            rev = d == 1
            lam = lam_ref[d:d + 1, :]
            z = gates[d]
            a, m, ra, ig, sp = z["a"], z["m"], z["ra"], z["ig"], z["sp"]
            lmb = l_s[d]
            h_prev = _shift(h_s[d], 1 if rev else -1, 0.0)
            da = lmb * h_prev
            dm = lmb * z["gx"]
            dgx = lmb * m
            dla = da * a - dm * (a * a) / jnp.maximum(m, 1e-30)
            dra = dla * (-LRU_C) * sp
            dsp = jnp.sum(dla * (-LRU_C) * ra, axis=0, keepdims=True)
            dlam_ref[d:d + 1, :] = dlam_ref[d:d + 1, :] - dsp * _sigmoid(-lam)
            dpa = dra * ra * (1.0 - ra)
            dpx = dgx * xc * ig * (1.0 - ig)
            dba_ref[d:d + 1, :] = dba_ref[d:d + 1, :] + jnp.sum(dpa, axis=0, keepdims=True)
            dbx_ref[d:d + 1, :] = dbx_ref[d:d + 1, :] + jnp.sum(dpx, axis=0, keepdims=True)
            dwa_ref[d, 0] = dwa_ref[d, 0] + _mm_tn(xc, dpa)
            dwx_ref[d, 0] = dwx_ref[d, 0] + _mm_tn(xc, dpx)
            dxc = dxc + dgx * ig + _mm_nt(dpa, wa_ref[d, 0]) + _mm_nt(dpx, wx_ref[d, 0])
        dcb_ref[...] = dcb_ref[...] + jnp.sum(dxc, axis=0, keepdims=True)
        dx, dws = _conv_bwd(dxc, lx, cw_ref, 2)
        for j, dw in enumerate(dws):
            dcw_ref[j:j + 1, :] = dcw_ref[j:j + 1, :] + dw
        dlx_ref[...] = dx

    sp = _lru_specs(cfg, False)
    nct = cfg.LW // LANES
    return pl.pallas_call(
        body, name="lru_bwd", grid=(nct, cfg.BL),
        in_specs=[sp["lx"], sp["tok"], sp["cw"], sp["row"], sp["w"], sp["w"], sp["two"], sp["two"], sp["two"]],
        out_specs=[sp["tok"], sp["cw"], sp["row"], sp["w"], sp["w"], sp["two"], sp["two"], sp["two"]],
        out_shape=[jax.ShapeDtypeStruct((cfg.T, cfg.LW), f32), jax.ShapeDtypeStruct((4, cfg.LW), f32),
                   jax.ShapeDtypeStruct((1, cfg.LW), f32), jax.ShapeDtypeStruct((2, nct, LANES, LANES), f32),
                   jax.ShapeDtypeStruct((2, nct, LANES, LANES), f32), jax.ShapeDtypeStruct((2, cfg.LW), f32),
                   jax.ShapeDtypeStruct((2, cfg.LW), f32), jax.ShapeDtypeStruct((2, cfg.LW), f32)],
        scratch_shapes=[pltpu.VMEM((2, cfg.S, LANES), f32)] * 5,
        compiler_params=_params(("arbitrary", "arbitrary")),
    )(proj, dhs, conv_w, conv_b, wa, wx, ba, bx, lam)


def _b_gdn(cfg, qkv, proj, do, alog_row, dt_row, saved):
    s, c, nc, hh = cfg.S, cfg.C, cfg.NC, cfg.H
    ba_blk = cfg.BAO // LANES
    scale = 128 ** -0.5
    per = 4
    assert nc % per == 0

    def body(q_ref, k_ref, v_ref, ba_ref, do_ref, al_ref, dt_ref, u_i, w_i, qd_i, kd_i, p_i, t_i, dec_i, cd_i,
             dqkv_ref, dba_ref, dal_ref, ddt_ref, vn_s, dvn_s, st_s, dst_s):
        h = pl.program_id(1)
        _zero_at_first((pl.program_id(0) == 0) & (h == 0), dal_ref, ddt_ref)
        _zero_at_first(h == 0, dba_ref)
        lane = lax.broadcasted_iota(jnp.int32, (c, LANES), 1)
        lane1 = lax.broadcasted_iota(jnp.int32, (1, LANES), 1)
        dirs = (0, 1)
        lane_b = [d * hh + h for d in dirs]
        lane_a = [2 * hh + d * hh + h for d in dirs]

        def seq(i, box):
            sts, dst = box[0]
            live = i < nc
            ii = jnp.where(live, i, 0)
            nf = [ii, nc - 1 - ii]
            nb = [nc - 1 - ii, ii]
            sf = [jnp.where(live, n, nc) for n in nf]
            sb = [jnp.where(live, n, nc) for n in nb]
            rf = [pl.ds(pl.multiple_of(n * c, c), c) for n in nf]
            rb = [pl.ds(pl.multiple_of(n * c, c), c) for n in nb]
            for d in dirs:
                st_s[d, sf[d]] = sts[d]
                dst_s[d, sb[d]] = dst[d]
            dob = [do_ref[rb[d], :] for d in dirs]
            vn = [u_i[d, rf[d], :] - _mm(w_i[d, rf[d], :], sts[d]) for d in dirs]
            dvn = [_mm_tn(p_i[d, 0, nb[d]], dob[d]) + _mm(kd_i[d, rb[d], :], dst[d]) for d in dirs]
            yield
            for d in dirs:
                vn_s[d, pl.ds(pl.multiple_of(sf[d] * c, c), c), :] = vn[d]
                dvn_s[d, pl.ds(pl.multiple_of(sb[d] * c, c), c), :] = dvn[d]
            sts = tuple(sts[d] * cd_i[d, 0, nf[d]][0:1, :] + _mm_tn(kd_i[d, rf[d], :], vn[d]) for d in dirs)
            dst = tuple(_mm_tn(qd_i[d, rb[d], :], dob[d]) + cd_i[d, 0, nb[d]][0:1, 0:1] * dst[d]
                        - _mm_tn(w_i[d, rb[d], :], dvn[d]) for d in dirs)
            box[0] = (sts, dst)
            yield

        def bpar(chunks, tick=lambda: None):
            ch = [(d, n) for n in chunks for d in dirs]
            rows = [pl.ds(pl.multiple_of(n * c, c), c) for _, n in ch]
            masks = [_tri_masks(c, d == 1) for d, _ in ch]
            ld = lambda ref: [ref[d, r, :] for (d, _), r in zip(ch, rows)]
            ldm = lambda ref: [ref[d, 0, n] for d, n in ch]
            q, k, v, dob = ([ref[r, :] for r in rows] for ref in (q_ref, k_ref, v_ref, do_ref))
            gates = [_gate_cols(ba_ref[r, :], al_ref[...], dt_ref[...], lane_b[d], lane_a[d]) for (d, _), r in zip(ch, rows)]
            beta, g, aexp, xa = ([gt[x] for gt in gates] for x in range(4))
            st, dst = ([ref[d, n] for d, n in ch] for ref in (st_s, dst_s))
            vn, dvn, u, w = ld(vn_s), ld(dvn_s), ld(u_i), ld(w_i)
            p = [x.astype(f32) for x in ldm(p_i)]
            tm, dec = ldm(t_i), ldm(dec_i)
            cd = [cd_i[d, 0, n][0:1, 0:1] for d, n in ch]
            dp = _map(lambda m, a, b: jnp.where(m[0], _mm_nt(a, b), 0.0), masks, dob, vn)
            dqd = _map(_mm_nt, dob, st)
            dkd = _map(_mm_nt, vn, dst)
            dw = _map(lambda a, b: -_mm_nt(a, b), dvn, st)
            dcd = _map(lambda a, b: jnp.sum(jnp.sum(a * b, axis=1, keepdims=True), axis=0, keepdims=True), st, dst)
            tick()
            gcum = _map(lambda m, x: _cum_mm(m[0], jnp.broadcast_to(x, (c, c)))[:, 0:1], masks, g)
            glast = [jnp.sum(x, axis=0, keepdims=True) for x in g]
            e = [jnp.exp(x) for x in gcum]
            el = _map(lambda a, b: jnp.exp(a - b), glast, gcum)
            qs = [x * scale for x in q]
            kb = _map(lambda a, b: a * b, k, beta)
            a = _map(lambda m, b, ki, dc: jnp.where(m[1], b * _mm_nt(ki, ki) * dc, 0.0), masks, beta, k, dec)
            tick()
            dvb = _map(_hmm3_tn, tm, dvn)
            dkbe = _map(_hmm3_tn, tm, dw)
            tick()
            da = _map(lambda m, x, ui, y, wi: -jnp.where(m[1], _mm_nt(x, ui) + _mm_nt(y, wi), 0.0), masks, dvb, u, dkbe, w)
            g1 = _map(lambda x, y: x * y, da, dec)
            g2 = _map(lambda x, y: x * y, dp, dec)
            dkb = _map(lambda x, ki, y, ei: _mm(x, ki) + y * ei, g1, k, dkbe, e)
            tick()
            dk = _map(lambda x, kbi, y, qi, z, b, t, l: _mm_tn(x, kbi) + _mm_tn(y, qi) + z * b + t * l,
                      g1, kb, g2, qs, dkb, beta, dkd, el)
            dqs = _map(lambda y, ki, x, ei: _mm(y, ki) + x * ei, g2, k, dqd, e)
            tick()
            ddd = _map(lambda x, ai, y, pi: x * ai + y * pi, da, a, dp, p)
            ones = jnp.ones((c, LANES), f32)
            dgcum = _map(lambda x: jnp.sum(x, axis=1, keepdims=True) - _hmm3_tn(x, ones)[:, 0:1], ddd)
            for x, (d, n) in enumerate(ch):
                dbeta = jnp.sum(dvb[x] * v[x], axis=1, keepdims=True) + jnp.sum(dkb[x] * k[x], axis=1, keepdims=True)
                de = jnp.sum(dkbe[x] * kb[x], axis=1, keepdims=True) + jnp.sum(dqd[x] * qs[x], axis=1, keepdims=True)
                del_ = jnp.sum(dkd[x] * k[x], axis=1, keepdims=True)
                dgc = dgcum[x] + de * e[x] - del_ * el[x]
                dglast = jnp.sum(del_ * el[x], axis=0, keepdims=True) + dcd[x] * cd[x]
                dg = _cum_mm(masks[x][2], jnp.broadcast_to(dgc, (c, LANES)))[:, 0:1] + dglast
                r = rows[x]
                if d == 0:
                    dqkv_ref[0, r, :] = dqs[x] * scale
                    dqkv_ref[1, r, :] = dk[x]
                    dqkv_ref[2, r, :] = dvb[x] * beta[x]
                else:
                    dqkv_ref[0, r, :] = dqkv_ref[0, r, :] + dqs[x] * scale
                    dqkv_ref[1, r, :] = dqkv_ref[1, r, :] + dk[x]
                    dqkv_ref[2, r, :] = dqkv_ref[2, r, :] + dvb[x] * beta[x]
                dlb = dbeta * beta[x] * (1.0 - beta[x])
                dalpha = -dg * aexp[x] * _sigmoid(xa[x])
                dba_ref[r, :] = dba_ref[r, :] + jnp.where(lane == lane_b[d], dlb, 0.0) + jnp.where(lane == lane_a[d], dalpha, 0.0)
                dal_ref[...] = dal_ref[...] + jnp.where(lane1 == lane_a[d], jnp.sum(dg * g[x], axis=0, keepdims=True), 0.0)
                ddt_ref[...] = ddt_ref[...] + jnp.where(lane1 == lane_a[d], jnp.sum(dalpha, axis=0, keepdims=True), 0.0)

        half, groups = nc // 2, nc // per

        def group(k):
            return [half - 2 * k - 2, half - 2 * k - 1, half + 2 * k, half + 2 * k + 1]

        def run(gen):
            for _ in gen:
                pass

        def plain(i, carry):
            box = [carry]
            run(seq(i, box))
            return box[0]

        def woven(k, carry):
            box = [carry]

            def two_steps():
                yield from seq(half + 2 + 2 * k, box)
                yield from seq(half + 3 + 2 * k, box)

            chain = two_steps()
            bpar(group(k), lambda: next(chain, None))
            run(chain)
            return box[0]

        z0 = jnp.zeros((128, 128), f32)
        lax.fori_loop(0, groups, woven, lax.fori_loop(0, half + 2, plain, ((z0, z0), (z0, z0))))

    blk = lambda off: pl.BlockSpec((s, 128), functools.partial(lambda b, h, off: (b, off + h), off=off))
    row = pl.BlockSpec((1, LANES), lambda b, h: (0, 0))
    tok2, mat2, cd2, _ = _gdn_specs(cfg)
    return pl.pallas_call(
        body, name="gdn_bwd", grid=(cfg.BL, hh),
        in_specs=[blk(0), blk(hh), blk(2 * hh), pl.BlockSpec((s, LANES), lambda b, h: (b, ba_blk)),
                  pl.BlockSpec((s, 128), lambda b, h: (b, h)), row, row, tok2, tok2, tok2, tok2, mat2, mat2, mat2, cd2],
        out_specs=[pl.BlockSpec((3, s, 128), lambda b, h: (0, b, h)), pl.BlockSpec((s, LANES), lambda b, h: (b, 0)), row, row],
        out_shape=[jax.ShapeDtypeStruct((3, cfg.T, cfg.DN), f32), jax.ShapeDtypeStruct((cfg.T, LANES), f32),
                   jax.ShapeDtypeStruct((1, LANES), f32), jax.ShapeDtypeStruct((1, LANES), f32)],
        scratch_shapes=[pltpu.VMEM((2, s + c, 128), f32)] * 2 + [pltpu.VMEM((2, nc + 1, 128, 128), f32)] * 2,
        compiler_params=_params(("arbitrary", "arbitrary")),
    )(qkv, qkv, qkv, proj, do, alog_row, dt_row, *saved)


def _b_prep(cfg, proj, dqkv, conv_w):
    dn, s = cfg.DN, cfg.S

    def body(x_ref, dy_ref, w_ref, dx_ref, dw_ref):
        _zero_at_first(pl.program_id(1) == 0, dw_ref)
        sec = pl.program_id(0)
        x = x_ref[...]
        c = _conv_fwd(x, w_ref, 2)
        y, dsilu = _silu(c)
        dy = dy_ref[0]
        parts = []
        for h in range(cfg.H):
            sl = slice(h * 128, (h + 1) * 128)
            yh, dyh = y[:, sl], dy[:, sl]
            inv = lax.rsqrt(jnp.sum(yh * yh, axis=1, keepdims=True) + EPS)
            dn_h = inv * dyh - yh * (inv * inv * inv) * jnp.sum(dyh * yh, axis=1, keepdims=True)
            parts.append(jnp.where(sec < 2, dn_h, dyh))
        ds = jnp.concatenate(parts, axis=1) if len(parts) > 1 else parts[0]
        dx, dws = _conv_bwd(ds * dsilu, x, w_ref, 2)
        for j, dw in enumerate(dws):
            dw_ref[j:j + 1, :] = dw_ref[j:j + 1, :] + dw
        dx_ref[...] = dx

    return pl.pallas_call(
        body, name="dn_prep_bwd", grid=(3, cfg.BL),
        in_specs=[pl.BlockSpec((s, dn), lambda j, b: (b, j)), pl.BlockSpec((1, s, dn), lambda j, b: (j, b, 0)),
                  pl.BlockSpec((4, dn), lambda j, b: (0, j))],
        out_specs=[pl.BlockSpec((s, dn), lambda j, b: (b, j)), pl.BlockSpec((4, dn), lambda j, b: (0, j))],
        out_shape=[jax.ShapeDtypeStruct((cfg.T, 3 * dn), f32), jax.ShapeDtypeStruct((4, 3 * dn), f32)],
        compiler_params=_params(("arbitrary", "arbitrary")),
    )(proj, dqkv, conv_w)


def _b_in(cfg, dqkv_pre, dz, dlx, dlg, dba, dr_mid, r_in, g1, wcat):
    d, dn, lw = cfg.D, cfg.DN, cfg.LW

    def body(dq_ref, dz_ref, dlx_ref, dlg_ref, dba_ref, dr_ref, r_ref, g_ref, w_ref, o_ref, dp_ref, dg_ref):
        _zero_at_first(pl.program_id(0) == 0, dg_ref)
        dp_ref[:, 0:cfg.ZO] = dq_ref[...].astype(bf16)
        dp_ref[:, cfg.ZO:cfg.LXO] = dz_ref[...].astype(bf16)
        dp_ref[:, cfg.LXO:cfg.LGO] = dlx_ref[...].astype(bf16)
        dp_ref[:, cfg.LGO:cfg.BAO] = dlg_ref[...].astype(bf16)
        dp_ref[:, cfg.BAO:] = dba_ref[...].astype(bf16)
        g = g_ref[...]
        _, xh, inv = _rms_fwd(r_ref[...], g)
        dx, dg = _rms_bwd(_mm(dp_ref[...], w_ref[...]), xh, inv, g)
        dg_ref[...] = dg_ref[...] + dg
        o_ref[...] = dr_ref[...] + dx

    return _row_call("in_proj_bwd", body, cfg.T, cfg.TM,
                     [(dqkv_pre, 3 * dn, 0), (dz, dn, 0), (dlx, lw, 0), (dlg, lw, 0), (dba, LANES, 0), (dr_mid, d, 0), (r_in, d, 0)],
                     [g1, wcat], [(d, f32), (cfg.PC, bf16)], [((1, d), f32)])


def _adam_call(name, w, g, m, v, tr):
    rows, cols = w.shape
    bc1 = 1.0 - ADAM_B1 ** ADAM_STEP
    bc2 = 1.0 - ADAM_B2 ** ADAM_STEP

    def body(w_ref, g_ref, m_ref, v_ref, d_ref, nm_ref, nv_ref):
        g = g_ref[...]
        m = ADAM_B1 * m_ref[...] + (1.0 - ADAM_B1) * g
        v = ADAM_B2 * v_ref[...] + (1.0 - ADAM_B2) * (g * g)
        nm_ref[...] = m
        nv_ref[...] = v
        d_ref[...] = -ADAM_LR * ((m / bc1) / (jnp.sqrt(v / bc2) + ADAM_EPS) + ADAM_WD * w_ref[...])

    spec = pl.BlockSpec((tr, cols), lambda i: (i, 0))
    return pl.pallas_call(
        body, name=name, grid=(rows // tr,), in_specs=[spec] * 4, out_specs=[spec] * 3,
        out_shape=[jax.ShapeDtypeStruct((rows, cols), f32)] * 3, compiler_params=_params(("arbitrary",)),
    )(w, g, m, v)


def _sum8_call(name, x, tr):
    _, rows, cols = x.shape

    def body(x_ref, o_ref):
        acc = x_ref[0].astype(f32)
        for j in range(1, N_DEV):
            acc = acc + x_ref[j].astype(f32)
        o_ref[...] = acc

    return pl.pallas_call(
        body, name=name, grid=(rows // tr,), in_specs=[pl.BlockSpec((N_DEV, tr, cols), lambda i: (0, i, 0))],
        out_specs=pl.BlockSpec((tr, cols), lambda i: (i, 0)), out_shape=jax.ShapeDtypeStruct((rows, cols), f32),
        compiler_params=_params(("arbitrary",)),
    )(x)


def _all_gather(name, shards):
    na = len(shards)

    def body(*refs):
        xs, outs = refs[:na], refs[na:2 * na]
        send_sems, recv_sems, local_sems = refs[2 * na:]
        x, y, c = lax.axis_index("x"), lax.axis_index("y"), lax.axis_index("c")
        me, sibling = (x, y, c), (x, y, 1 - c)
        chips = [(1 - x, y), (x, 1 - y), (1 - x, 1 - y)]

        def copy(a, k, block, to, src=None):
            px, py, pc = block
            slot = outs[a].at[4 * px + 2 * py + pc]
            return pltpu.make_async_remote_copy(
                src_ref=slot if src is None else src, dst_ref=slot,
                send_sem=send_sems.at[a, k], recv_sem=recv_sems.at[a, k], device_id=to, device_id_type=MESH)

        mine = [pltpu.make_async_copy(xs[a], outs[a].at[4 * x + 2 * y + c], local_sems.at[a]) for a in range(na)]
        for cp in mine:
            cp.start()
        first = []
        for a in range(na):
            first.append(copy(a, 0, me, sibling, src=xs[a]))
            first += [copy(a, 1 + j, me, (*chip, c), src=xs[a]) for j, chip in enumerate(chips)]
        for cp in first:
            cp.start()
        passed = []
        for j, chip in enumerate(chips):
            for a in range(na):
                copy(a, 1 + j, (*chip, c), me).wait_recv()
                cp = copy(a, 4 + j, (*chip, c), sibling)
                cp.start()
                passed.append(cp)
        for a in range(na):
            copy(a, 0, sibling, me).wait_recv()
            for j, chip in enumerate(chips):
                copy(a, 4 + j, (*chip, 1 - c), me).wait_recv()
        for cp in first + passed:
            cp.wait_send()
        for cp in mine:
            cp.wait()

    hbm = pl.BlockSpec(memory_space=pltpu.HBM)
    return pl.pallas_call(
        body, name=name, out_shape=[jax.ShapeDtypeStruct((N_DEV,) + s.shape, s.dtype) for s in shards],
        in_specs=[hbm] * na, out_specs=[hbm] * na,
        scratch_shapes=[pltpu.SemaphoreType.DMA((na, 7)), pltpu.SemaphoreType.DMA((na, 7)), pltpu.SemaphoreType.DMA((na,))],
    )(*shards)


def _peer_list():
    x, y, c = lax.axis_index("x"), lax.axis_index("y"), lax.axis_index("c")
    return 4 * x + 2 * y + c, [(x ^ (k >> 2), y ^ ((k >> 1) & 1), c ^ (k & 1)) for k in range(1, N_DEV)]


_HBM = pl.BlockSpec(memory_space=pltpu.HBM)
_SEM = pl.BlockSpec(memory_space=pltpu.SEMAPHORE)
_EFFECT = pltpu.SideEffectType.DATAFLOW_SIDE_EFFECTING


def _exchange_copies(xs, lands, send_sem, recv_sem, gather):
    me, peers = _peer_list()
    return [pltpu.make_async_remote_copy(
        src_ref=xs[a] if gather else xs[a].at[4 * px + 2 * py + pc], dst_ref=lands[a].at[me],
        send_sem=send_sem.at[7 * a + k], recv_sem=recv_sem.at[7 * a + k], device_id=(px, py, pc), device_id_type=MESH)
        for k, (px, py, pc) in enumerate(peers) for a in range(len(xs))]


def _exchange_start(name, blocks, gather=False):
    na = len(blocks)

    def body(*refs):
        for cp in _exchange_copies(refs[:na], refs[na:2 * na], refs[2 * na], refs[2 * na + 1], gather):
            cp.start()
        refs[-1][...] = jnp.zeros_like(refs[-1])

    lands = [jax.ShapeDtypeStruct((N_DEV,) + b.shape if gather else b.shape, b.dtype) for b in blocks]
    hbm = [pltpu.HBM(b.shape, b.dtype) for b in blocks] + [pltpu.HBM(b.shape, b.dtype) for b in lands]
    send_sem, recv_sem, *thru, token = pl.pallas_call(
        body, name=name,
        out_shape=(pltpu.SemaphoreType.DMA((7 * na,)), pltpu.SemaphoreType.DMA((7 * na,)), *hbm,
                   jax.ShapeDtypeStruct((8, LANES), f32)),
        in_specs=[_HBM] * (2 * na), out_specs=(_SEM, _SEM, *([_HBM] * (2 * na)), pl.BlockSpec(memory_space=pltpu.VMEM)),
        input_output_aliases={i: 2 + i for i in range(2 * na)},
        compiler_params=pltpu.CompilerParams(has_side_effects=_EFFECT),
    )(*[pltpu.with_memory_space_constraint(b, pltpu.HBM) for b in blocks],
      *[pltpu.with_memory_space_constraint(lax.empty(b.shape, b.dtype), pltpu.HBM) for b in lands])
    return send_sem, recv_sem, thru, token


def _exchange_wait(name, send_sem, recv_sem, thru, after, gather=False):
    na = len(thru) // 2

    def body(*refs):
        for cp in _exchange_copies(refs[:na], refs[na:2 * na], refs[2 * na], refs[2 * na + 1], gather):
            cp.wait_send()
            cp.wait_recv()

    return pl.pallas_call(
        body, name=name, out_shape=tuple(pltpu.HBM(t.shape, t.dtype) for t in thru),
        in_specs=[_HBM] * (2 * na) + [_SEM, _SEM, pl.BlockSpec(memory_space=pl.ANY)], out_specs=tuple([_HBM] * (2 * na)),
        input_output_aliases={i: i for i in range(2 * na)},
        compiler_params=pltpu.CompilerParams(has_side_effects=_EFFECT),
    )(*thru, send_sem, recv_sem, after)[na:]


def _layer_fwd(cfg, w, r, p):
    h1, proj = _k_in(cfg, r, w["norm1_g"], w["wcat_t"])
    qkv = _k_prep(cfg, proj, w["dn_conv_w"])
    o, *gdn_saved = _k_gdn_fwd(cfg, qkv, proj, w["alog_row"], w["dt_row"])
    hs = _k_lru_fwd(cfg, proj, w["lru_conv_w"], w["lru_conv_b"], w["wa"], w["wx"], w["lru_ba"], w["lru_bx"], w["lru_lambda"])
    if "late" in w:
        w.update(w.pop("late")(hs))
    mix, r_mid = _k_mix(cfg, o, proj, hs, r, w["dn_norm_g"], w["lru_norm_g"], w["w_out"])
    h2, gp, up = _k_ffn_a(cfg, r_mid, w["norm2_g"], w["ffn_wg_t"], w["ffn_wu_t"])
    act = _k_ffn_b(cfg, gp, up, w["ffn_conv_w"], w["ffn_conv_b"])
    r2 = _k_ffn_c(cfg, act, r_mid, w["ffn_wd"])
    pn, r3 = _k_ple(cfg, r2, p, w["ple_norm_g"], w["ple_wg"], w["ple_bg"], w["ple_wp_t"])
    saved = dict(r=r, h1=h1, proj=proj, qkv=qkv, o=o, gdn=gdn_saved, hs=hs, mix=mix, r_mid=r_mid, h2=h2, gp=gp, up=up,
                 r2=r2, pn=pn, p=p)
    return r3, saved


def _layer_bwd(cfg, w, sv, dr3, early=None):
    g = {}
    dt = min(512, cfg.D)
    dr2, dr2b, dlog, dpp, g["ple_norm_g"], g["ple_bg"] = _b_ple(cfg, dr3, sv["r2"], sv["p"], w["ple_norm_g"], w["ple_wg"], w["ple_bg"], w["ple_wp_t"])
    g["ple_wg"] = _mm_tn_call(cfg, "d_ple_wg", sv["pn"], dlog, dt)
    g["ple_wp_t"] = _mm_tn_call(cfg, "d_ple_wp", dpp, sv["p"], cfg.PD)
    dgp, dup, g["ffn_wd"], g["ffn_conv_w"], g["ffn_conv_b"] = _b_ffn_bc(cfg, dr2b, w["ffn_wd"], sv["gp"], sv["up"], w["ffn_conv_w"], w["ffn_conv_b"])
    dr_mid, g["norm2_g"] = _b_ffn_a(cfg, dgp, dup, dr2, sv["r_mid"], w["norm2_g"], w["ffn_wg_t"], w["ffn_wu_t"])
    g["ffn_wg_t"] = _mm_tn_call(cfg, "d_ffn_wg", dgp, sv["h2"], dt)
    g["ffn_wu_t"] = _mm_tn_call(cfg, "d_ffn_wu", dup, sv["h2"], dt)
    if early is not None:
        w = dict(w, dn_norm_g=w["dn_norm_g"] + early(EARLY, g)[0, 0])
    do, dz, dlg, dhs, g["dn_norm_g"], g["lru_norm_g"] = _b_mix(cfg, dr_mid, sv["o"], sv["proj"], sv["hs"], w["dn_norm_g"], w["lru_norm_g"], w["w_out"])
    g["w_out"] = _mm_tn_call(cfg, "d_w_out", sv["mix"], dr_mid, dt)
    if early is not None:
        w = dict(w, lru_conv_b=w["lru_conv_b"] + early(("w_out",), g)[0, 0])
    dlx, g["lru_conv_w"], g["lru_conv_b"], g["wa"], g["wx"], g["lru_ba"], g["lru_bx"], g["lru_lambda"] = _b_lru(
        cfg, sv["proj"], dhs, w["lru_conv_w"], w["lru_conv_b"], w["wa"], w["wx"], w["lru_ba"], w["lru_bx"], w["lru_lambda"])
    dqkv, dba, g["alog_row"], g["dt_row"] = _b_gdn(cfg, sv["qkv"], sv["proj"], do, w["alog_row"], w["dt_row"], sv["gdn"])
    dqkv_pre, g["dn_conv_w"] = _b_prep(cfg, sv["proj"], dqkv, w["dn_conv_w"])
    dr, dproj, g["norm1_g"] = _b_in(cfg, dqkv_pre, dz, dlx, dlg, dba, dr_mid, sv["r"], w["norm1_g"], w["wcat_t"])
    g["wcat_t"] = _mm_tn_call(cfg, "d_w_in", dproj, sv["h1"], dt)
    return dr, g


BIG = ("w_in", "w_out", "ffn_wg", "ffn_wu", "ffn_wd", "ple_wg", "ple_wp")
BIG_T = {"w_in": True, "w_out": False, "ffn_wg": True, "ffn_wu": True, "ffn_wd": False, "ple_wg": False, "ple_wp": True}
BIG_OPERAND = {"w_in": "wcat_t", "w_out": "w_out", "ffn_wg": "ffn_wg_t", "ffn_wu": "ffn_wu_t", "ffn_wd": "ffn_wd",
               "ple_wg": "ple_wg", "ple_wp": "ple_wp_t"}
EARLY = ("ffn_wg", "ffn_wu", "ffn_wd", "ple_wg", "ple_wp")
SMALL_SHARDED = ("dn_conv_w", "lru_conv_w", "lru_ba", "lru_bx", "lru_lambda", "ffn_conv_w")
SMALL_REPL = ("norm1_g", "dn_a_log", "dn_dt_bias", "dn_norm_g", "lru_conv_b", "lru_wa", "lru_wx", "lru_norm_g", "norm2_g",
              "ffn_conv_b", "ple_norm_g", "ple_bg", "final_g")
WEIGHTS = ("norm1_g", "w_in", "dn_conv_w", "dn_a_log", "dn_dt_bias", "dn_norm_g", "lru_conv_w", "lru_conv_b", "lru_wa",
           "lru_ba", "lru_wx", "lru_bx", "lru_lambda", "lru_norm_g", "w_out", "norm2_g", "ffn_wg", "ffn_wu", "ffn_conv_w",
           "ffn_conv_b", "ffn_wd", "ple_norm_g", "ple_wg", "ple_bg", "ple_wp", "final_g")


def _pad_rows(flat, cols, mult):
    n = flat.shape[0]
    rows = -(-n // cols)
    rows = -(-rows // mult) * mult
    return jnp.pad(flat, (0, rows * cols - n)).reshape(rows, cols)


def _pack(arrs, cols, mult, dtype):
    return _pad_rows(jnp.concatenate([a.reshape(-1).astype(dtype) for a in arrs]), cols, mult)


def _unpack(flat, shapes):
    out, off = [], 0
    for shp in shapes:
        n = math.prod(shp)
        piece = flat[off:off + n]
        if n < 4096:
            piece = lax.optimization_barrier(piece)
        out.append(piece.reshape(shp))
        off += n
    return out


def _unpack8(g8, shapes, axes):
    out, off = [], 0
    for shp, ax in zip(shapes, axes):
        n = math.prod(shp)
        a = g8[:, off:off + n].reshape((N_DEV,) + tuple(shp))
        a = jnp.moveaxis(a, 0, ax)
        out.append(a.reshape(shp[:ax] + (N_DEV * shp[ax],) + shp[ax + 1:]))
        off += n
    return out


def _wcat_t_from_w_in_t(cfg, wt):
    nba = 4 * cfg.H
    pad = jnp.zeros((LANES - nba, wt.shape[1]), wt.dtype)
    return jnp.concatenate([wt[:cfg.LXO], wt[cfg.LXO + nba:], wt[cfg.LXO:cfg.LXO + nba], pad], axis=0)


def _w_in_t_from_wcat_t(cfg, wc):
    nba = 4 * cfg.H
    return jnp.concatenate([wc[:cfg.LXO], wc[cfg.BAO:cfg.BAO + nba], wc[cfg.LXO:cfg.BAO]], axis=0)


def _gate_row(cfg, a):
    h2 = 2 * cfg.H
    return jnp.concatenate([jnp.zeros((1, h2), f32), a.reshape(1, h2), jnp.zeros((1, LANES - 2 * h2), f32)], axis=1)


def _blockdiag(cfg, w):
    w = w.reshape(2, cfg.NB // 2, 2, 64, 64)
    z = jnp.zeros_like(w[:, :, 0])
    top = jnp.concatenate([w[:, :, 0], z], axis=-1)
    bot = jnp.concatenate([z, w[:, :, 1]], axis=-1)
    return jnp.concatenate([top, bot], axis=-2).astype(bf16)


def _unblockdiag(cfg, g):
    a = g[:, :, :64, :64]
    b = g[:, :, 64:, 64:]
    return jnp.stack([a, b], axis=2).reshape(2, cfg.NB, 64, 64)


def _big_operands(cfg, big):
    return {BIG_OPERAND[n]: (_wcat_t_from_w_in_t(cfg, v) if n == "w_in" else v) for n, v in big.items()}


def _layer_operands(cfg, big, small, i):
    return dict(
        _big_operands(cfg, big),
        norm1_g=small["norm1_g"][i][None], dn_conv_w=small["dn_conv_w"][i], alog_row=_gate_row(cfg, small["dn_a_log"][i]),
        dt_row=_gate_row(cfg, small["dn_dt_bias"][i]), dn_norm_g=small["dn_norm_g"][i][None],
        lru_conv_w=small["lru_conv_w"][i], lru_conv_b=small["lru_conv_b"][i][None],
        wa=_blockdiag(cfg, small["lru_wa"][i]), wx=_blockdiag(cfg, small["lru_wx"][i]),
        lru_ba=small["lru_ba"][i], lru_bx=small["lru_bx"][i], lru_lambda=small["lru_lambda"][i],
        lru_norm_g=small["lru_norm_g"][i][None], norm2_g=small["norm2_g"][i][None], ffn_conv_w=small["ffn_conv_w"][i],
        ffn_conv_b=small["ffn_conv_b"][i][None], ple_norm_g=small["ple_norm_g"][i][None], ple_bg=small["ple_bg"][i][None],
    )


def _small_grads_to_problem(cfg, g):
    h = cfg.H
    return dict(
        norm1_g=g["norm1_g"][0], dn_conv_w=g["dn_conv_w"],
        dn_a_log=g["alog_row"][0, 2 * h:4 * h].reshape(2, h), dn_dt_bias=g["dt_row"][0, 2 * h:4 * h].reshape(2, h),
        dn_norm_g=g["dn_norm_g"][0], lru_conv_w=g["lru_conv_w"], lru_conv_b=g["lru_conv_b"][0],
        lru_wa=_unblockdiag(cfg, g["wa"]), lru_wx=_unblockdiag(cfg, g["wx"]), lru_ba=g["lru_ba"], lru_bx=g["lru_bx"],
        lru_lambda=g["lru_lambda"], lru_norm_g=g["lru_norm_g"][0], norm2_g=g["norm2_g"][0], ffn_conv_w=g["ffn_conv_w"],
        ffn_conv_b=g["ffn_conv_b"][0], ple_norm_g=g["ple_norm_g"][0], ple_bg=g["ple_bg"][0],
    )


def _local_step(cfg, get_big, small, x, p, target, on_big_grads):
    r = x.reshape(cfg.T, cfg.D)
    ops, saved = [], []
    for i in range(cfg.L):
        big, token, late = get_big(i, r)
        w = _layer_operands(cfg, big, small, i)
        if token is not None:
            w["norm1_g"] = w["norm1_g"] + token[0, 0]
        if late is not None:
            w["late"] = late
        r, sv = _layer_fwd(cfg, w, r, p[i].reshape(cfg.T, cfg.PD))
        ops.append(w)
        saved.append(sv)
    dr, loss, dgf = _k_loss(cfg, r, target.reshape(cfg.T, cfg.D), small["final_g"][None])
    gsmall = [None] * cfg.L
    for i in reversed(range(cfg.L)):
        first = EARLY + ("w_out",) if i == 0 else ()
        early = (lambda names, g: on_big_grads("%d%s" % (i, names[0]), {n: g[BIG_OPERAND[n]] for n in names})) if first else None
        dr, g = _layer_bwd(cfg, ops[i], saved[i], dr, early)
        token = on_big_grads("%d" % i, {n: (_w_in_t_from_wcat_t(cfg, g["wcat_t"]) if n == "w_in" else g[BIG_OPERAND[n]])
                                        for n in BIG if n not in first})
        if i > 0:
            ops[i - 1]["ple_bg"] = ops[i - 1]["ple_bg"] + token[0, 0]
        gsmall[i] = _small_grads_to_problem(cfg, g)
    gs = {k: jnp.stack([gl[k] for gl in gsmall]) for k in gsmall[0]}
    gs["final_g"] = dgf[0]
    return loss, dr, gs


def _row_tile(rows, limit=512):
    best = rows
    for t in range(8, min(rows, limit) + 1, 8):
        if rows % t == 0:
            best = t
    return best if best <= limit or rows <= limit else rows


def _adam_group(name, ws, gs, ms, vs, cols, tr):
    shapes = [w.shape for w in ws]
    pk = lambda arrs: _pack(arrs, cols, tr, f32)
    w2 = pk(ws)
    d, nm, nv = _adam_call(name, w2, pk(gs), pk(ms), pk(vs), min(tr, w2.shape[0]))
    return [_unpack(a.reshape(-1), shapes) for a in (d, nm, nv)]


def kernel(x, p, norm1_g, w_in, dn_conv_w, dn_a_log, dn_dt_bias, dn_norm_g, lru_conv_w, lru_conv_b, lru_wa, lru_ba, lru_wx, lru_bx, lru_lambda, lru_norm_g, w_out, norm2_g, ffn_wg, ffn_wu, ffn_conv_w, ffn_conv_b, ffn_wd, ple_norm_g, ple_wg, ple_bg, ple_wp, final_g, loss_target, m_norm1_g, m_w_in, m_dn_conv_w, m_dn_a_log, m_dn_dt_bias, m_dn_norm_g, m_lru_conv_w, m_lru_conv_b, m_lru_wa, m_lru_ba, m_lru_wx, m_lru_bx, m_lru_lambda, m_lru_norm_g, m_w_out, m_norm2_g, m_ffn_wg, m_ffn_wu, m_ffn_conv_w, m_ffn_conv_b, m_ffn_wd, m_ple_norm_g, m_ple_wg, m_ple_bg, m_ple_wp, m_final_g, v_norm1_g, v_w_in, v_dn_conv_w, v_dn_a_log, v_dn_dt_bias, v_dn_norm_g, v_lru_conv_w, v_lru_conv_b, v_lru_wa, v_lru_ba, v_lru_wx, v_lru_bx, v_lru_lambda, v_lru_norm_g, v_w_out, v_norm2_g, v_ffn_wg, v_ffn_wu, v_ffn_conv_w, v_ffn_conv_b, v_ffn_wd, v_ple_norm_g, v_ple_wg, v_ple_bg, v_ple_wp, v_final_g):
    cfg = CFG
    a = dict(locals())
    wl = {n: a[n] for n in WEIGHTS}
    ml = {n: a["m_" + n] for n in WEIGHTS}
    vl = {n: a["v_" + n] for n in WEIGHTS}
    me = 4 * lax.axis_index("x") + 2 * lax.axis_index("y") + lax.axis_index("c")
    nl = cfg.L

    blocks = [(jnp.swapaxes(wl[n], 1, 2) if BIG_T[n] else wl[n]).astype(bf16) for n in BIG]
    ss_shapes = [wl[n].shape for n in SMALL_SHARDED]
    first, s8 = _all_gather("gather_weights_0", [blocks[0][0], _pack([wl[n] for n in SMALL_SHARDED], LANES, 8, f32)])
    small = dict(zip(SMALL_SHARDED, _unpack8(s8.reshape(N_DEV, -1), ss_shapes, [2] * len(ss_shapes))))
    small.update({n: wl[n] for n in SMALL_REPL})

    def start_gather(key, i, names, behind):
        shards, _ = lax.optimization_barrier(([blk[i] for n, blk in zip(BIG, blocks) if n in names], behind))
        return _exchange_start("gather_start_" + key, shards, gather=True)

    def wait_gather(key, i, names, started, behind):
        send_sem, recv_sem, thru, _ = started
        lands = _exchange_wait("gather_wait_" + key, send_sem, recv_sem, thru, behind, gather=True)
        own = [blk[i][None] for n, blk in zip(BIG, blocks) if n in names]
        full = [lax.dynamic_update_slice_in_dim(land, o, me, 0) for land, o in zip(lands, own)]
        return {n: f.reshape(N_DEV * f.shape[1], f.shape[2]) for n, f in zip([n for n in BIG if n in names], full)}

    ahead = {}

    def get_big(i, r):
        if i > 0:
            big = wait_gather("%d" % i, i, BIG, ahead.pop(i), r)
            if i + 1 < nl:
                ahead[i + 1] = start_gather("%d" % (i + 1), i + 1, BIG, big["w_in"])
            return big, (ahead[i + 1][3] if i + 1 < nl else None), None
        rest = start_gather("0r", 0, BIG[1:], first)

        def late(x):
            big = wait_gather("0r", 0, BIG[1:], rest, x)
            ops = _big_operands(cfg, big)
            if nl > 1:
                ahead[1] = start_gather("1", 1, BIG, big["w_out"])
                ops["norm2_g"] = small["norm2_g"][0][None] + ahead[1][3][0, 0]
            return ops

        return {"w_in": first.reshape(N_DEV * first.shape[1], first.shape[2])}, rest[3], late

    pending = {}

    def on_big_grads(key, g):
        send = [g[n].reshape((N_DEV,) + blk.shape[1:]).astype(bf16) for n, blk in zip(BIG, blocks) if n in g]
        own = [lax.dynamic_index_in_dim(sd, me, 0, keepdims=True) for sd in send]
        send_sem, recv_sem, thru, token = _exchange_start("exchange_start_" + key, send)
        pending[key] = (send_sem, recv_sem, thru, own, [n for n in BIG if n in g])
        return token

    loss_part, dr, gsmall = _local_step(cfg, get_big, small, x, p, loss_target, on_big_grads)
    grad_x = dr.reshape(x.shape)

    sums = {n: [None] * nl for n in BIG}
    for key in sorted(pending):
        send_sem, recv_sem, thru, own, names = pending[key]
        lands = _exchange_wait("exchange_wait_" + key, send_sem, recv_sem, thru, dr)
        for n, land, o in zip(names, lands, own):
            slots = lax.dynamic_update_slice_in_dim(land, o, me, 0)
            sums[n][int(key[0])] = _sum8_call("sum_%s_%s" % (n, key), slots, slots.shape[1])
    gl = {}
    for n in BIG:
        s = jnp.stack(sums[n])
        gl[n] = jnp.swapaxes(s, 1, 2) if BIG_T[n] else s

    small_names = SMALL_REPL + SMALL_SHARDED
    narrow = ("lru_wa", "lru_wx")
    wide = [n for n in small_names if n not in narrow]
    wide_shapes = [gsmall[n].shape for n in wide]
    sv = _pack([gsmall[n] for n in wide] + [loss_part[0, 0:1]], LANES, 512, f32)
    nv = _pack([gsmall[n] for n in narrow], LANES, 512, bf16)
    sv8, nv8 = _all_gather("gather_small_grads", [sv, nv])
    small_sum = _sum8_call("sum_small", sv8, 512).reshape(-1)
    narrow_sum = _sum8_call("sum_small_narrow", nv8, 512).reshape(-1)
    gl.update(zip(wide, _unpack(small_sum, wide_shapes)))
    gl.update(zip(narrow, _unpack(narrow_sum, [gsmall[n].shape for n in narrow])))
    loss = small_sum[sum(math.prod(s) for s in wide_shapes)]
    for n in SMALL_SHARDED:
        shard = wl[n].shape[2]
        gl[n] = lax.dynamic_slice_in_dim(gl[n], me * shard, shard, axis=2)

    outs = {}
    for n in BIG:
        shp = wl[n].shape
        two = lambda t: t.reshape(-1, shp[-1])
        d, nm, nv = _adam_call("adam_" + n, two(wl[n]), two(gl[n]), two(ml[n]), two(vl[n]), _row_tile(math.prod(shp[:-1])))
        outs[n] = (d.reshape(shp), nm.reshape(shp), nv.reshape(shp))
    d, nm, nv = _adam_group("adam_small", [wl[n] for n in small_names], [gl[n] for n in small_names],
                            [ml[n] for n in small_names], [vl[n] for n in small_names], LANES, 64)
    for j, n in enumerate(small_names):
        outs[n] = (d[j], nm[j], nv[j])
    return (loss, grad_x, *[gl[n] for n in WEIGHTS], *[outs[n][0] for n in WEIGHTS], *[outs[n][1] for n in WEIGHTS],
            *[outs[n][2] for n in WEIGHTS])
```

```python
import functools
import math

import jax
import jax.numpy as jnp
from jax import lax
from jax.experimental import pallas as pl
from jax.experimental.pallas import tpu as pltpu

f32 = jnp.float32
bf16 = jnp.bfloat16
MESH = pl.DeviceIdType.MESH

N_DEV = 8
LANES = 128
EPS = 1e-6
LRU_C = 8.0
ADAM_LR, ADAM_B1, ADAM_B2, ADAM_EPS, ADAM_WD, ADAM_STEP = 0.001, 0.9, 0.999, 1e-08, 0.01, 10
VMEM_LIMIT = 56 * 1024 * 1024


class Cfg:
    def __init__(self, d_model=1024, bl=4, seq=2048, depth=4, heads=4, lru_width=512, d_ff=2816, ple=256,
                 tm=512, tm_ffn=256, ff_tile=256):
        self.D, self.BL, self.S, self.L, self.H = d_model, bl, seq, depth, heads
        self.DH = 128
        self.DN = heads * self.DH
        self.LW = lru_width
        self.NB = lru_width // 64
        self.FF, self.PD = d_ff, ple
        self.C = 64
        self.NC = seq // self.C
        self.T = bl * seq
        self.TM = min(tm, self.T)
        self.TMF = min(tm_ffn, self.T)
        self.TMW = min(2 * tm, self.T)
        self.FT = ff_tile
        self.ZO = 3 * self.DN
        self.LXO = 4 * self.DN
        self.LGO = self.LXO + self.LW
        self.BAO = self.LGO + self.LW
        self.PC = self.BAO + LANES
        self.IN_COLS = 4 * self.DN + 4 * heads + 2 * self.LW


CFG = Cfg()


def _mm(a, b):
    return jnp.dot(a.astype(bf16), b.astype(bf16), preferred_element_type=f32)


def _mm_nt(a, b):
    return lax.dot_general(a.astype(bf16), b.astype(bf16), (((1,), (1,)), ((), ())), preferred_element_type=f32)


def _mm_tn(a, b):
    return lax.dot_general(a.astype(bf16), b.astype(bf16), (((0,), (0,)), ((), ())), preferred_element_type=f32)


def _split2(a):
    hi = a.astype(bf16)
    return hi, (a - hi.astype(f32)).astype(bf16)


def _hmm3(a, b, dims=(((1,), (0,)), ((), ()))):
    ah, al = _split2(a)
    bh, bl = _split2(b)
    dot = functools.partial(lax.dot_general, dimension_numbers=dims, preferred_element_type=f32)
    return dot(ah, bh) + dot(ah, bl) + dot(al, bh)


def _hmm3_tn(a, b):
    return _hmm3(a, b, (((0,), (0,)), ((), ())))


def _cum_mm(mask, x, dims=(((1,), (0,)), ((), ()))):
    m = mask.astype(bf16)
    x1 = x.astype(bf16)
    r = x - x1.astype(f32)
    x2 = r.astype(bf16)
    x3 = (r - x2.astype(f32)).astype(bf16)
    dot = functools.partial(lax.dot_general, dimension_numbers=dims, preferred_element_type=f32)
    return dot(m, x1) + dot(m, x2) + dot(m, x3)


def _rms_fwd(x, g):
    inv = lax.rsqrt(jnp.mean(x * x, axis=-1, keepdims=True) + EPS)
    xh = x * inv
    return xh * g, xh, inv


def _rms_bwd(dy, xh, inv, g):
    dxh = dy * g
    dx = inv * (dxh - xh * jnp.mean(dxh * xh, axis=-1, keepdims=True))
    dg = jnp.sum(dy * xh, axis=0, keepdims=True)
    return dx, dg


def _sigmoid(x):
    return 1.0 / (1.0 + jnp.exp(-x))


def _softplus(x):
    return jnp.maximum(x, 0.0) + jnp.log(1.0 + jnp.exp(-jnp.abs(x)))


def _silu(x):
    s = _sigmoid(x)
    return x * s, s * (1.0 + x * (1.0 - s))


_GC = math.sqrt(2.0 / math.pi)


def _gelu(x):
    t = jnp.tanh(_GC * (x + 0.044715 * x * x * x))
    y = 0.5 * x * (1.0 + t)
    dy = 0.5 * (1.0 + t) + 0.5 * x * (1.0 - t * t) * _GC * (1.0 + 3.0 * 0.044715 * x * x)
    return y, dy


def _nexpm1(x):
    ser = -x * (1.0 + x * 0.5 * (1.0 + x * (1.0 / 3.0) * (1.0 + x * 0.25 * (1.0 + x * 0.2))))
    return jnp.where(x > -0.1, ser, 1.0 - jnp.exp(x))


def _shift(x, s, fill=0.0):
    if s == 0:
        return x
    n = x.shape[0]
    t = lax.broadcasted_iota(jnp.int32, x.shape, 0)
    r = pltpu.roll(x, (-s) % n, 0)
    return jnp.where((t + s >= 0) & (t + s < n), r, fill)


def _conv_fwd(x, w_ref, left):
    k = w_ref.shape[0]
    out = _shift(x, -left) * w_ref[0:1, :]
    for j in range(1, k):
        out = out + _shift(x, j - left) * w_ref[j:j + 1, :]
    return out


def _conv_bwd(dout, x, w_ref, left):
    k = w_ref.shape[0]
    dx = None
    dws = []
    for j in range(k):
        term = _shift(dout, -(j - left)) * w_ref[j:j + 1, :]
        dx = term if dx is None else dx + term
        dws.append(jnp.sum(dout * _shift(x, j - left), axis=0, keepdims=True))
    return dx, dws


def _scan_refs(scans, n):
    blk = 64
    nb = n // blk
    sub = lax.broadcasted_iota(jnp.int32, (blk, LANES), 0) & 7

    def local(a, b, rev):
        for d in (1, 2, 4):
            ok = (sub < 8 - d) if rev else (sub >= d)
            sh = (blk - d) if rev else d
            b = a * jnp.where(ok, pltpu.roll(b, sh, 0), 0.0) + b
            a = a * jnp.where(ok, pltpu.roll(a, sh, 0), 1.0)
        return a, b

    def body(i, carries):
        new = []
        for (a_ref, b_ref, h_ref, rev), carry in zip(scans, carries):
            j = (nb - 1 - i) if rev else i
            base = pl.multiple_of(j * blk, blk)
            a, b = local(a_ref[pl.ds(base, blk), :], b_ref[pl.ds(base, blk), :], rev)
            order = range(blk // 8 - 1, -1, -1) if rev else range(blk // 8)
            for v in order:
                h = b[8 * v:8 * v + 8, :] + a[8 * v:8 * v + 8, :] * carry
                h_ref[pl.ds(base + 8 * v, 8), :] = h
                carry = h[0:1, :] if rev else h[7:8, :]
            new.append(carry)
        return tuple(new)

    lax.fori_loop(0, nb, body, tuple(jnp.zeros((1, LANES), f32) for _ in scans))


def _tri_masks(c, rev):
    i = lax.broadcasted_iota(jnp.int32, (c, c), 0)
    j = lax.broadcasted_iota(jnp.int32, (c, c), 1)
    incl = (i <= j) if rev else (i >= j)
    strict = (i < j) if rev else (i > j)
    incl_t = (i >= j) if rev else (i <= j)
    return incl, strict, incl_t


def _map(f, *lists):
    return [f(*a) for a in zip(*lists)]


def _tri_inv(mats, tick=lambda: None):
    c = mats[0].shape[0]
    i = lax.broadcasted_iota(jnp.int32, (c, c), 0)
    j = lax.broadcasted_iota(jnp.int32, (c, c), 1)
    eye = jnp.where(i == j, 1.0, 0.0)
    t = [eye - a for a in mats]
    pw = _map(_hmm3, mats, mats)
    for it in range(5):
        tick()
        t = _map(lambda ti, ui: ti + ui, t, _map(_hmm3, t, pw))
        if it < 4:
            pw = _map(_hmm3, pw, pw)
    return t


def _gdn_decay(gs, revs):
    c = gs[0].shape[0]
    masks = [_tri_masks(c, r) for r in revs]
    gb = [jnp.broadcast_to(g, (c, c)) for g in gs]
    mcol = _map(lambda m, x: _cum_mm(m[0], x), masks, gb)
    mrow = _map(lambda m, x: jnp.sum(jnp.where(m[2], x, 0.0), axis=0, keepdims=True), masks, gb)
    dec = _map(lambda m, a, b: jnp.exp(jnp.where(m[0], a - b, -1e30)), masks, mcol, mrow)
    return [m[:, 0:1] for m in mcol], [jnp.sum(g, axis=0, keepdims=True) for g in gs], dec


def _gdn_prep(qs, k, v, g, beta, kk, qk, revs, tick=lambda: None):
    c = k[0].shape[0]
    masks = [_tri_masks(c, r) for r in revs]
    gcum, glast, dec = _gdn_decay(g, revs)
    e = [jnp.exp(x) for x in gcum]
    tick()
    tm = _tri_inv(_map(lambda m, b, x, d: jnp.where(m[1], b * x * d, 0.0), masks, beta, kk, dec), tick)
    tick()
    u = _map(lambda t, vi, b: _hmm3(t, vi * b), tm, v, beta)
    tick()
    w = _map(lambda t, ki, b, ei: _hmm3(t, ki * b * ei), tm, k, beta, e)
    p = _map(lambda m, x, d: jnp.where(m[0], x * d, 0.0), masks, qk, dec)
    return dict(dec=dec, cd=[jnp.exp(x) for x in glast], tm=tm, u=u, w=w, p=p, qd=_map(lambda a, b: a * b, qs, e),
                kd=_map(lambda ki, gl, gc: ki * jnp.exp(gl - gc), k, glast, gcum))


def _lane_pick(x, lane):
    l = lax.broadcasted_iota(jnp.int32, x.shape, 1)
    return jnp.sum(jnp.where(l == lane, x, 0.0), axis=1, keepdims=True)


def _params(sem, vmem=VMEM_LIMIT):
    return pltpu.CompilerParams(dimension_semantics=sem, vmem_limit_bytes=vmem)


def _row_call(name, body, t, tm, row_ins, full_ins, row_outs, acc_outs):
    in_specs = [pl.BlockSpec((tm, w), functools.partial(lambda i, c: (i, c), c=c)) for (_, w, c) in row_ins]
    for a in full_ins:
        in_specs.append(pl.BlockSpec(a.shape, functools.partial(lambda i, n: (0,) * n, n=a.ndim)))
    out_specs = [pl.BlockSpec((tm, w), lambda i: (i, 0)) for (w, _) in row_outs]
    out_shape = [jax.ShapeDtypeStruct((t, w), dt) for (w, dt) in row_outs]
    for shp, dt in acc_outs:
        out_specs.append(pl.BlockSpec(shp, functools.partial(lambda i, n: (0,) * n, n=len(shp))))
        out_shape.append(jax.ShapeDtypeStruct(shp, dt))
    return pl.pallas_call(
        body, name=name, grid=(t // tm,), in_specs=in_specs, out_specs=out_specs, out_shape=out_shape,
        compiler_params=_params(("arbitrary",)),
    )(*[a for (a, _, _) in row_ins], *full_ins)


def _k_in(cfg, r, g1, wcat):
    def body(r_ref, g_ref, w_ref, h_ref, proj_ref):
        y, _, _ = _rms_fwd(r_ref[...], g_ref[...])
        hb = y.astype(bf16)
        h_ref[...] = hb
        proj_ref[...] = _mm_nt(hb, w_ref[...])

    return _row_call("in_proj", body, cfg.T, cfg.TM, [(r, cfg.D, 0)], [g1, wcat],
                     [(cfg.D, bf16), (cfg.PC, f32)], [])


def _k_prep(cfg, proj, conv_w):
    dn, s = cfg.DN, cfg.S

    def body(x_ref, w_ref, o_ref):
        sec = pl.program_id(1)
        c = _conv_fwd(x_ref[...], w_ref, 2)
        y, _ = _silu(c)
        for h in range(cfg.H):
            yh = y[:, h * 128:(h + 1) * 128]
            nh = yh * lax.rsqrt(jnp.sum(yh * yh, axis=1, keepdims=True) + EPS)
            o_ref[:, h * 128:(h + 1) * 128] = jnp.where(sec < 2, nh, yh)

    return pl.pallas_call(
        body, name="dn_prep", grid=(cfg.BL, 3),
        in_specs=[pl.BlockSpec((s, dn), lambda b, j: (b, j)), pl.BlockSpec((4, dn), lambda b, j: (0, j))],
        out_specs=pl.BlockSpec((s, dn), lambda b, j: (b, j)),
        out_shape=jax.ShapeDtypeStruct((cfg.T, 3 * dn), f32),
        compiler_params=_params(("arbitrary", "arbitrary")),
    )(proj, conv_w)


def _gate_cols(ba, alog_row, dt_row, lane_b, lane_a):
    beta = _sigmoid(_lane_pick(ba, lane_b))
    alpha = _lane_pick(ba, lane_a)
    aexp = jnp.exp(_lane_pick(alog_row, lane_a))
    dtb = _lane_pick(dt_row, lane_a)
    xa = alpha + dtb
    g = -aexp * _softplus(xa)
    return beta, g, aexp, xa


def _gdn_specs(cfg):
    s, c, nc = cfg.S, cfg.C, cfg.NC
    tok2 = pl.BlockSpec((2, s, 128), lambda b, h: (0, b, h))
    mat2 = pl.BlockSpec((2, 1, nc, c, c), lambda b, h: (0, b, h, 0, 0))
    cd2 = pl.BlockSpec((2, 1, nc, 8, LANES), lambda b, h: (0, b, h, 0, 0))
    shapes = dict(
        tok32=jax.ShapeDtypeStruct((2, cfg.T, cfg.DN), f32), tok16=jax.ShapeDtypeStruct((2, cfg.T, cfg.DN), bf16),
        mat32=jax.ShapeDtypeStruct((2, cfg.BL, cfg.H * nc, c, c), f32), mat16=jax.ShapeDtypeStruct((2, cfg.BL, cfg.H * nc, c, c), bf16),
        cd=jax.ShapeDtypeStruct((2, cfg.BL, cfg.H * nc, 8, LANES), f32))
    return tok2, mat2, cd2, shapes


def _k_gdn_fwd(cfg, qkv, proj, alog_row, dt_row):
    s, c, nc, hh = cfg.S, cfg.C, cfg.NC, cfg.H
    ba_blk = cfg.BAO // LANES
    per = 4 if nc % 4 == 0 else 1

    def body(q_ref, k_ref, v_ref, ba_ref, al_ref, dt_ref, o_ref, u_o, w_o, qd_o, kd_o, p_o, t_o, dec_o, cd_o):
        h = pl.program_id(1)
        groups = nc // per

        def chunk_of(d, g, j):
            return (nc - 1 - (g * per + j)) if d == 1 else (g * per + j)

        def prep(g, tick=lambda: None):
            chains, qs, k, v, kk, qk, gt, beta = [], [], [], [], [], [], [], []
            for j in range(per):
                for d in range(2):
                    n = chunk_of(d, g, j)
                    rows = pl.ds(pl.multiple_of(n * c, c), c)
                    kj, vj = k_ref[rows, :], v_ref[rows, :]
                    qj = q_ref[rows, :] * (128 ** -0.5)
                    bd, gd, _, _ = _gate_cols(ba_ref[rows, :], al_ref[...], dt_ref[...], d * hh + h, 2 * hh + d * hh + h)
                    chains.append((d, n, rows))
                    for lst, val in ((qs, qj), (k, kj), (v, vj), (kk, _mm_nt(kj, kj)), (qk, _mm_nt(qj, kj)), (gt, gd), (beta, bd)):
                        lst.append(val)
            z = _gdn_prep(qs, k, v, gt, beta, kk, qk, [d == 1 for d, _, _ in chains], tick)
            for x, (d, n, rows) in enumerate(chains):
                u_o[d, rows, :] = z["u"][x]
                w_o[d, rows, :] = z["w"][x].astype(bf16)
                qd_o[d, rows, :] = z["qd"][x].astype(bf16)
                kd_o[d, rows, :] = z["kd"][x].astype(bf16)
                p_o[d, 0, n] = z["p"][x].astype(bf16)
                t_o[d, 0, n] = z["tm"][x]
                dec_o[d, 0, n] = z["dec"][x]
                cd_o[d, 0, n] = jnp.broadcast_to(z["cd"][x], (8, LANES))

        def steps(g, box):
            for j in range(per):
                sts = box[0]
                ns = [chunk_of(d, g, j) for d in range(2)]
                rows = [pl.ds(pl.multiple_of(n * c, c), c) for n in ns]
                ws = [_mm(w_o[d, rows[d], :], sts[d]) for d in range(2)]
                qs_ = [_mm(qd_o[d, rows[d], :], sts[d]) for d in range(2)]
                yield
                vn = [u_o[d, rows[d], :] - ws[d] for d in range(2)]
                box[0] = tuple(sts[d] * cd_o[d, 0, ns[d]][0:1, :] + _mm_tn(kd_o[d, rows[d], :], vn[d]) for d in range(2))
                for d in range(2):
                    o_ref[rows[d], :] = o_ref[rows[d], :] + qs_[d] + _mm(p_o[d, 0, ns[d]], vn[d])
                yield

        o_ref[...] = jnp.zeros_like(o_ref)
        prep(0)

        def fused(g, sts):
            box = [sts]
            chain = steps(g - 1, box)
            prep(g, lambda: next(chain, None))
            for _ in chain:
                pass
            return box[0]

        z0 = jnp.zeros((128, 128), f32)
        box = [lax.fori_loop(1, groups, fused, (z0, z0))]
        for _ in steps(groups - 1, box):
            pass

    blk = lambda off: pl.BlockSpec((s, 128), functools.partial(lambda b, h, off: (b, off + h), off=off))
    row = pl.BlockSpec((1, LANES), lambda b, h: (0, 0))
    tok2, mat2, cd2, shp = _gdn_specs(cfg)
    return pl.pallas_call(
        body, name="gdn_fwd", grid=(cfg.BL, hh),
        in_specs=[blk(0), blk(hh), blk(2 * hh), pl.BlockSpec((s, LANES), lambda b, h: (b, ba_blk)), row, row],
        out_specs=[pl.BlockSpec((s, 128), lambda b, h: (b, h)), tok2, tok2, tok2, tok2, mat2, mat2, mat2, cd2],
        out_shape=[jax.ShapeDtypeStruct((cfg.T, cfg.DN), f32), shp["tok32"], shp["tok16"], shp["tok16"], shp["tok16"],
                   shp["mat16"], shp["mat32"], shp["mat32"], shp["cd"]],
        compiler_params=_params(("arbitrary", "arbitrary")),
    )(qkv, qkv, qkv, proj, alog_row, dt_row)


def _lru_gates(xc, wa, wx, ba, bx, lam):
    ra = _sigmoid(_mm(xc, wa) + ba)
    ig = _sigmoid(_mm(xc, wx) + bx)
    sp = _softplus(-lam)
    la = -LRU_C * ra * sp
    a = jnp.exp(la)
    m = jnp.sqrt(_nexpm1(2.0 * la))
    return dict(ra=ra, ig=ig, sp=sp, a=a, m=m, gx=ig * xc)


def _lru_specs(cfg, outer_b):
    s = cfg.S
    lx_blk = cfg.LXO // LANES
    if outer_b:
        ix = lambda f: (lambda b, ct: f(b, ct))
    else:
        ix = lambda f: (lambda ct, b: f(b, ct))
    return dict(
        lx=pl.BlockSpec((s, LANES), ix(lambda b, ct: (b, lx_blk + ct))),
        tok=pl.BlockSpec((s, LANES), ix(lambda b, ct: (b, ct))),
        cw=pl.BlockSpec((4, LANES), ix(lambda b, ct: (0, ct))),
        row=pl.BlockSpec((1, LANES), ix(lambda b, ct: (0, ct))),
        w=pl.BlockSpec((2, 1, LANES, LANES), ix(lambda b, ct: (0, ct, 0, 0))),
        two=pl.BlockSpec((2, LANES), ix(lambda b, ct: (0, ct))),
    )


def _k_lru_fwd(cfg, proj, conv_w, conv_b, wa, wx, ba, bx, lam):
    def body(lx_ref, cw_ref, cb_ref, wa_ref, wx_ref, ba_ref, bx_ref, lam_ref, o_ref, a_s, b_s, h_s):
        xc = _conv_fwd(lx_ref[...], cw_ref, 2) + cb_ref[...]
        for d in range(2):
            z = _lru_gates(xc, wa_ref[d, 0], wx_ref[d, 0], ba_ref[d:d + 1, :], bx_ref[d:d + 1, :], lam_ref[d:d + 1, :])
            a_s[d] = z["a"]
            b_s[d] = z["m"] * z["gx"]
        _scan_refs([(a_s.at[d], b_s.at[d], h_s.at[d], d == 1) for d in range(2)], cfg.S)
        o_ref[...] = h_s[0] + h_s[1]

    sp = _lru_specs(cfg, True)
    return pl.pallas_call(
        body, name="lru_fwd", grid=(cfg.BL, cfg.LW // LANES),
        in_specs=[sp["lx"], sp["cw"], sp["row"], sp["w"], sp["w"], sp["two"], sp["two"], sp["two"]],
        out_specs=sp["tok"], out_shape=jax.ShapeDtypeStruct((cfg.T, cfg.LW), f32),
        scratch_shapes=[pltpu.VMEM((2, cfg.S, LANES), f32)] * 3,
        compiler_params=_params(("arbitrary", "arbitrary")),
    )(proj, conv_w, conv_b, wa, wx, ba, bx, lam)


def _mix_parts(cfg, o, z, lg, hs, dng, lrg):
    heads = []
    for h in range(cfg.H):
        sl = slice(h * 128, (h + 1) * 128)
        y, xh, inv = _rms_fwd(o[:, sl], dng)
        sz, dsz = _silu(z[:, sl])
        heads.append((y, xh, inv, sz, dsz))
    gl, dgl = _gelu(lg)
    y2, xh2, inv2 = _rms_fwd(gl * hs, lrg)
    return heads, (y2, xh2, inv2, gl, dgl)


def _k_mix(cfg, o, proj, hs, r, dng, lrg, wout):
    dn = cfg.DN

    def body(o_ref, z_ref, lg_ref, hs_ref, r_ref, dng_ref, lrg_ref, w_ref, mix_ref, out_ref):
        heads, lru = _mix_parts(cfg, o_ref[...], z_ref[...], lg_ref[...], hs_ref[...], dng_ref[...], lrg_ref[...])
        for h, (y, _, _, sz, _) in enumerate(heads):
            mix_ref[:, h * 128:(h + 1) * 128] = (y * sz).astype(bf16)
        mix_ref[:, dn:] = lru[0].astype(bf16)
        out_ref[...] = r_ref[...] + jnp.dot(mix_ref[...], w_ref[...], preferred_element_type=f32)

    return _row_call("mix_out", body, cfg.T, cfg.TM,
                     [(o, dn, 0), (proj, dn, cfg.ZO // dn), (proj, cfg.LW, cfg.LGO // cfg.LW), (hs, cfg.LW, 0), (r, cfg.D, 0)],
                     [dng, lrg, wout], [(cfg.D, bf16), (cfg.D, f32)], [])


def _k_ffn_a(cfg, r, g2, wg, wu):
    def body(r_ref, g_ref, wg_ref, wu_ref, h_ref, gp_ref, up_ref):
        y, _, _ = _rms_fwd(r_ref[...], g_ref[...])
        hb = y.astype(bf16)
        h_ref[...] = hb
        gp_ref[...] = _mm_nt(hb, wg_ref[...])
        up_ref[...] = _mm_nt(hb, wu_ref[...])

    return _row_call("ffn_in", body, cfg.T, cfg.TMF, [(r, cfg.D, 0)], [g2, wg, wu],
                     [(cfg.D, bf16), (cfg.FF, f32), (cfg.FF, f32)], [])


def _k_ffn_b(cfg, gp, up, conv_w, conv_b):
    s, ft = cfg.S, cfg.FT

    def body(gp_ref, up_ref, w_ref, b_ref, o_ref):
        gate = _conv_fwd(gp_ref[...], w_ref, 1) + b_ref[...]
        gl, _ = _gelu(gate)
        o_ref[...] = (gl * up_ref[...]).astype(bf16)

    tok = pl.BlockSpec((s, ft), lambda b, j: (b, j))
    return pl.pallas_call(
        body, name="ffn_act", grid=(cfg.BL, cfg.FF // ft),
        in_specs=[tok, tok, pl.BlockSpec((3, ft), lambda b, j: (0, j)), pl.BlockSpec((1, ft), lambda b, j: (0, j))],
        out_specs=tok, out_shape=jax.ShapeDtypeStruct((cfg.T, cfg.FF), bf16),
        compiler_params=_params(("arbitrary", "arbitrary")),
    )(gp, up, conv_w, conv_b)


def _k_ffn_c(cfg, act, r, wd):
    def body(a_ref, r_ref, w_ref, o_ref):
        o_ref[...] = r_ref[...] + jnp.dot(a_ref[...], w_ref[...], preferred_element_type=f32)

    return _row_call("ffn_out", body, cfg.T, cfg.TM, [(act, cfg.FF, 0), (r, cfg.D, 0)], [wd], [(cfg.D, f32)], [])[0]


def _k_ple(cfg, r, p, gp, wpg, bg, wpp):
    def body(r_ref, p_ref, g_ref, wg_ref, bg_ref, wp_ref, pn_ref, o_ref):
        x = r_ref[...]
        y, _, _ = _rms_fwd(x, g_ref[...])
        pn = y.astype(bf16)
        pn_ref[...] = pn
        pg = _sigmoid(jnp.dot(pn, wg_ref[...], preferred_element_type=f32) + bg_ref[...])
        o_ref[...] = x + pg * _mm_nt(p_ref[...], wp_ref[...])

    return _row_call("ple", body, cfg.T, cfg.TM, [(r, cfg.D, 0), (p, cfg.PD, 0)], [gp, wpg, bg, wpp],
                     [(cfg.D, bf16), (cfg.D, f32)], [])


def _k_loss(cfg, r, tgt, gf):
    d = cfg.D

    def body(r_ref, t_ref, g_ref, dr_ref, loss_ref, dg_ref):
        @pl.when(pl.program_id(0) == 0)
        def _():
            loss_ref[...] = jnp.zeros_like(loss_ref)
            dg_ref[...] = jnp.zeros_like(dg_ref)

        g = g_ref[...]
        y, xh, inv = _rms_fwd(r_ref[...], g)
        err = y - t_ref[...]
        loss_ref[...] = loss_ref[...] + (0.5 / d) * jnp.sum(err * err)
        dx, dg = _rms_bwd(err * (1.0 / d), xh, inv, g)
        dr_ref[...] = dx
        dg_ref[...] = dg_ref[...] + dg

    return _row_call("loss_head", body, cfg.T, cfg.TM, [(r, d, 0), (tgt, d, 0)], [gf], [(d, f32)],
                     [((1, LANES), f32), ((1, d), f32)])


def _zero_at_first(cond, *refs):
    @pl.when(cond)
    def _():
        for r in refs:
            r[...] = jnp.zeros_like(r)


def _b_ple(cfg, dr3, r2, p, gp, wpg, bg, wpp):
    d = cfg.D

    def body(dr_ref, r_ref, p_ref, g_ref, wg_ref, bg_ref, wp_ref, dr2_ref, dr2b_ref, dlog_ref, dpp_ref, dgp_ref, dbg_ref):
        _zero_at_first(pl.program_id(0) == 0, dgp_ref, dbg_ref)
        g = g_ref[...]
        dr = dr_ref[...]
        y, xh, inv = _rms_fwd(r_ref[...], g)
        pg = _sigmoid(jnp.dot(y.astype(bf16), wg_ref[...], preferred_element_type=f32) + bg_ref[...])
        pp = _mm_nt(p_ref[...], wp_ref[...])
        dpp_ref[...] = (dr * pg).astype(bf16)
        dlog = dr * pp * pg * (1.0 - pg)
        dlog_ref[...] = dlog.astype(bf16)
        dbg_ref[...] = dbg_ref[...] + jnp.sum(dlog, axis=0, keepdims=True)
        dx, dg = _rms_bwd(_mm_nt(dlog, wg_ref[...]), xh, inv, g)
        dgp_ref[...] = dgp_ref[...] + dg
        dr2_ref[...] = dr + dx
        dr2b_ref[...] = (dr + dx).astype(bf16)

    return _row_call("ple_bwd", body, cfg.T, cfg.TM, [(dr3, d, 0), (r2, d, 0), (p, cfg.PD, 0)], [gp, wpg, bg, wpp],
                     [(d, f32), (d, bf16), (d, bf16), (d, bf16)], [((1, d), f32), ((1, d), f32)])


def _b_ffn_bc(cfg, dr2, wd, gp, up, conv_w, conv_b):
    s, ft, d = cfg.S, cfg.FT, cfg.D

    def body(dr_ref, wd_ref, gp_ref, up_ref, w_ref, b_ref, dgp_ref, dup_ref, dwd_ref, dcw_ref, dcb_ref):
        _zero_at_first(pl.program_id(1) == 0, dwd_ref, dcw_ref, dcb_ref)
        drb = dr_ref[...]
        dact = _mm_nt(drb, wd_ref[...])
        gpre = gp_ref[...]
        up = up_ref[...]
        gl, dgl = _gelu(_conv_fwd(gpre, w_ref, 1) + b_ref[...])
        dup_ref[...] = (dact * gl).astype(bf16)
        dgate = dact * up * dgl
        dcb_ref[...] = dcb_ref[...] + jnp.sum(dgate, axis=0, keepdims=True)
        dx, dws = _conv_bwd(dgate, gpre, w_ref, 1)
        for j, dw in enumerate(dws):
            dcw_ref[j:j + 1, :] = dcw_ref[j:j + 1, :] + dw
        dgp_ref[...] = dx.astype(bf16)
        dwd_ref[...] = dwd_ref[...] + _mm_tn((gl * up).astype(bf16), drb)

    tok = pl.BlockSpec((s, ft), lambda j, b: (b, j))
    return pl.pallas_call(
        body, name="ffn_act_bwd", grid=(cfg.FF // ft, cfg.BL),
        in_specs=[pl.BlockSpec((s, d), lambda j, b: (b, 0)), pl.BlockSpec((ft, d), lambda j, b: (j, 0)), tok, tok,
                  pl.BlockSpec((3, ft), lambda j, b: (0, j)), pl.BlockSpec((1, ft), lambda j, b: (0, j))],
        out_specs=[tok, tok, pl.BlockSpec((ft, d), lambda j, b: (j, 0)), pl.BlockSpec((3, ft), lambda j, b: (0, j)),
                   pl.BlockSpec((1, ft), lambda j, b: (0, j))],
        out_shape=[jax.ShapeDtypeStruct((cfg.T, cfg.FF), bf16), jax.ShapeDtypeStruct((cfg.T, cfg.FF), bf16),
                   jax.ShapeDtypeStruct((cfg.FF, d), f32), jax.ShapeDtypeStruct((3, cfg.FF), f32),
                   jax.ShapeDtypeStruct((1, cfg.FF), f32)],
        compiler_params=_params(("arbitrary", "arbitrary")),
    )(dr2, wd, gp, up, conv_w, conv_b)


def _b_ffn_a(cfg, dgp, dup, dr2, r_mid, g2, wg, wu):
    d = cfg.D

    def body(dgp_ref, dup_ref, dr_ref, r_ref, g_ref, wg_ref, wu_ref, o_ref, dg_ref):
        _zero_at_first(pl.program_id(0) == 0, dg_ref)
        g = g_ref[...]
        _, xh, inv = _rms_fwd(r_ref[...], g)
        dh = _mm(dgp_ref[...], wg_ref[...]) + _mm(dup_ref[...], wu_ref[...])
        dx, dg = _rms_bwd(dh, xh, inv, g)
        dg_ref[...] = dg_ref[...] + dg
        o_ref[...] = dr_ref[...] + dx

    return _row_call("ffn_in_bwd", body, cfg.T, cfg.TMF, [(dgp, cfg.FF, 0), (dup, cfg.FF, 0), (dr2, d, 0), (r_mid, d, 0)],
                     [g2, wg, wu], [(d, f32)], [((1, d), f32)])


def _mm_tn_call(cfg, name, x, dy, tn):
    t, k = x.shape
    n = dy.shape[1]
    tm = cfg.TMW
    last = t // tm - 1

    def body(x_ref, dy_ref, o_ref, acc):
        _zero_at_first(pl.program_id(1) == 0, acc)
        acc[...] = acc[...] + _mm_tn(x_ref[...], dy_ref[...])

        @pl.when(pl.program_id(1) == last)
        def _():
            o_ref[...] = acc[...].astype(bf16)

    return pl.pallas_call(
        body, name=name, grid=(n // tn, t // tm),
        in_specs=[pl.BlockSpec((tm, k), lambda j, i: (i, 0)), pl.BlockSpec((tm, tn), lambda j, i: (i, j))],
        out_specs=pl.BlockSpec((k, tn), lambda j, i: (0, j)),
        out_shape=jax.ShapeDtypeStruct((k, n), bf16), scratch_shapes=[pltpu.VMEM((k, tn), f32)],
        compiler_params=_params(("arbitrary", "arbitrary")),
    )(x, dy)


def _b_mix(cfg, dr, o, proj, hs, dng, lrg, wout):
    dn, lw, d = cfg.DN, cfg.LW, cfg.D

    def body(dr_ref, o_ref, z_ref, lg_ref, hs_ref, dng_ref, lrg_ref, w_ref, do_ref, dz_ref, dlg_ref, dhs_ref, ddn_ref, dlr_ref):
        _zero_at_first(pl.program_id(0) == 0, ddn_ref, dlr_ref)
        dng, lrg = dng_ref[...], lrg_ref[...]
        hs = hs_ref[...]
        o, z = o_ref[...], z_ref[...]
        heads, lru = _mix_parts(cfg, o, z, lg_ref[...], hs, dng, lrg)
        drb = dr_ref[...].astype(bf16)
        dmix_dn = _mm_nt(drb, w_ref[0:dn, :])
        dmix_lr = _mm_nt(drb, w_ref[dn:, :])
        dgn = jnp.zeros_like(dng)
        for h, (y, xh, inv, sz, dsz) in enumerate(heads):
            sl = slice(h * 128, (h + 1) * 128)
            dm = dmix_dn[:, sl]
            dz_ref[:, sl] = (dm * y * dsz).astype(bf16)
            dx, dg = _rms_bwd(dm * sz, xh, inv, dng)
            do_ref[:, sl] = dx
            dgn = dgn + dg
        ddn_ref[...] = ddn_ref[...] + dgn
        _, xh2, inv2, gl, dgl = lru
        dx2, dg2 = _rms_bwd(dmix_lr, xh2, inv2, lrg)
        dlr_ref[...] = dlr_ref[...] + dg2
        dlg_ref[...] = (dx2 * hs * dgl).astype(bf16)
        dhs_ref[...] = dx2 * gl

    return _row_call("mix_bwd", body, cfg.T, cfg.TM,
                     [(dr, d, 0), (o, dn, 0), (proj, dn, cfg.ZO // dn), (proj, lw, cfg.LGO // lw), (hs, lw, 0)],
                     [dng, lrg, wout], [(dn, f32), (dn, bf16), (lw, bf16), (lw, f32)], [((1, 128), f32), ((1, lw), f32)])


def _b_lru(cfg, proj, dhs, conv_w, conv_b, wa, wx, ba, bx, lam):
    def body(lx_ref, dh_ref, cw_ref, cb_ref, wa_ref, wx_ref, ba_ref, bx_ref, lam_ref,
             dlx_ref, dcw_ref, dcb_ref, dwa_ref, dwx_ref, dba_ref, dbx_ref, dlam_ref, a_s, b_s, h_s, an_s, l_s):
        _zero_at_first(pl.program_id(1) == 0, dcw_ref, dcb_ref, dwa_ref, dwx_ref, dba_ref, dbx_ref, dlam_ref)
        lx = lx_ref[...]
        xc = _conv_fwd(lx, cw_ref, 2) + cb_ref[...]
        dxc = jnp.zeros_like(xc)
        gates = []
        for d in range(2):
            z = _lru_gates(xc, wa_ref[d, 0], wx_ref[d, 0], ba_ref[d:d + 1, :], bx_ref[d:d + 1, :], lam_ref[d:d + 1, :])
            a_s[d] = z["a"]
            b_s[d] = z["m"] * z["gx"]
            an_s[d] = _shift(z["a"], -1 if d == 1 else 1, 0.0)
            gates.append(z)
        _scan_refs([(a_s.at[d], b_s.at[d], h_s.at[d], d == 1) for d in range(2)]
                   + [(an_s.at[d], dh_ref, l_s.at[d], d == 0) for d in range(2)], cfg.S)
        for d in range(2):
            rev = d == 1
            lam = lam_ref[d:d + 1, :]
            z = gates[d]
            a, m, ra, ig, sp = z["a"], z["m"], z["ra"], z["ig"], z["sp"]
            lmb = l_s[d]
            h_prev = _shift(h_s[d], 1 if rev else -1, 0.0)
            da = lmb * h_prev
            dm = lmb * z["gx"]
            dgx = lmb * m
            dla = da * a - dm * (a * a) / jnp.maximum(m, 1e-30)
            dra = dla * (-LRU_C) * sp
            dsp = jnp.sum(dla * (-LRU_C) * ra, axis=0, keepdims=True)
            dlam_ref[d:d + 1, :] = dlam_ref[d:d + 1, :] - dsp * _sigmoid(-lam)
            dpa = dra * ra * (1.0 - ra)
            dpx = dgx * xc * ig * (1.0 - ig)
            dba_ref[d:d + 1, :] = dba_ref[d:d + 1, :] + jnp.sum(dpa, axis=0, keepdims=True)
            dbx_ref[d:d + 1, :] = dbx_ref[d:d + 1, :] + jnp.sum(dpx, axis=0, keepdims=True)
            dwa_ref[d, 0] = dwa_ref[d, 0] + _mm_tn(xc, dpa)
            dwx_ref[d, 0] = dwx_ref[d, 0] + _mm_tn(xc, dpx)
            dxc = dxc + dgx * ig + _mm_nt(dpa, wa_ref[d, 0]) + _mm_nt(dpx, wx_ref[d, 0])
        dcb_ref[...] = dcb_ref[...] + jnp.sum(dxc, axis=0, keepdims=True)
        dx, dws = _conv_bwd(dxc, lx, cw_ref, 2)
        for j, dw in enumerate(dws):
            dcw_ref[j:j + 1, :] = dcw_ref[j:j + 1, :] + dw
        dlx_ref[...] = dx.astype(bf16)

    sp = _lru_specs(cfg, False)
    nct = cfg.LW // LANES
    return pl.pallas_call(
        body, name="lru_bwd", grid=(nct, cfg.BL),
        in_specs=[sp["lx"], sp["tok"], sp["cw"], sp["row"], sp["w"], sp["w"], sp["two"], sp["two"], sp["two"]],
        out_specs=[sp["tok"], sp["cw"], sp["row"], sp["w"], sp["w"], sp["two"], sp["two"], sp["two"]],
        out_shape=[jax.ShapeDtypeStruct((cfg.T, cfg.LW), bf16), jax.ShapeDtypeStruct((4, cfg.LW), f32),
                   jax.ShapeDtypeStruct((1, cfg.LW), f32), jax.ShapeDtypeStruct((2, nct, LANES, LANES), f32),
                   jax.ShapeDtypeStruct((2, nct, LANES, LANES), f32), jax.ShapeDtypeStruct((2, cfg.LW), f32),
                   jax.ShapeDtypeStruct((2, cfg.LW), f32), jax.ShapeDtypeStruct((2, cfg.LW), f32)],
        scratch_shapes=[pltpu.VMEM((2, cfg.S, LANES), f32)] * 5,
        compiler_params=_params(("arbitrary", "arbitrary")),
    )(proj, dhs, conv_w, conv_b, wa, wx, ba, bx, lam)


def _b_gdn(cfg, qkv, proj, do, alog_row, dt_row, saved):
    s, c, nc, hh = cfg.S, cfg.C, cfg.NC, cfg.H
    ba_blk = cfg.BAO // LANES
    scale = 128 ** -0.5
    per = 4
    assert nc % per == 0

    def body(q_ref, k_ref, v_ref, ba_ref, do_ref, al_ref, dt_ref, u_i, w_i, qd_i, kd_i, p_i, t_i, dec_i, cd_i,
             dqkv_ref, dba_ref, dal_ref, ddt_ref, vn_s, dvn_s, st_s, dst_s):
        h = pl.program_id(1)
        _zero_at_first((pl.program_id(0) == 0) & (h == 0), dal_ref, ddt_ref)
        _zero_at_first(h == 0, dba_ref)
        lane = lax.broadcasted_iota(jnp.int32, (c, LANES), 1)
        lane1 = lax.broadcasted_iota(jnp.int32, (1, LANES), 1)
        dirs = (0, 1)
        lane_b = [d * hh + h for d in dirs]
        lane_a = [2 * hh + d * hh + h for d in dirs]

        def seq(i, box):
            sts, dst = box[0]
            live = i < nc
            ii = jnp.where(live, i, 0)
            nf = [ii, nc - 1 - ii]
            nb = [nc - 1 - ii, ii]
            sf = [jnp.where(live, n, nc) for n in nf]
            sb = [jnp.where(live, n, nc) for n in nb]
            rf = [pl.ds(pl.multiple_of(n * c, c), c) for n in nf]
            rb = [pl.ds(pl.multiple_of(n * c, c), c) for n in nb]
            for d in dirs:
                st_s[d, sf[d]] = sts[d]
                dst_s[d, sb[d]] = dst[d]
            dob = [do_ref[rb[d], :] for d in dirs]
            vn = [u_i[d, rf[d], :] - _mm(w_i[d, rf[d], :], sts[d]) for d in dirs]
            dvn = [_mm_tn(p_i[d, 0, nb[d]], dob[d]) + _mm(kd_i[d, rb[d], :], dst[d]) for d in dirs]
            yield
            for d in dirs:
                vn_s[d, pl.ds(pl.multiple_of(sf[d] * c, c), c), :] = vn[d]
                dvn_s[d, pl.ds(pl.multiple_of(sb[d] * c, c), c), :] = dvn[d]
            sts = tuple(sts[d] * cd_i[d, 0, nf[d]][0:1, :] + _mm_tn(kd_i[d, rf[d], :], vn[d]) for d in dirs)
            dst = tuple(_mm_tn(qd_i[d, rb[d], :], dob[d]) + cd_i[d, 0, nb[d]][0:1, 0:1] * dst[d]
                        - _mm_tn(w_i[d, rb[d], :], dvn[d]) for d in dirs)
            box[0] = (sts, dst)
            yield

        def bpar(chunks, tick=lambda: None):
            ch = [(d, n) for n in chunks for d in dirs]
            rows = [pl.ds(pl.multiple_of(n * c, c), c) for _, n in ch]
            masks = [_tri_masks(c, d == 1) for d, _ in ch]
            ld = lambda ref: [ref[d, r, :] for (d, _), r in zip(ch, rows)]
            ldm = lambda ref: [ref[d, 0, n] for d, n in ch]
            q, k, v, dob = ([ref[r, :] for r in rows] for ref in (q_ref, k_ref, v_ref, do_ref))
            gates = [_gate_cols(ba_ref[r, :], al_ref[...], dt_ref[...], lane_b[d], lane_a[d]) for (d, _), r in zip(ch, rows)]
            beta, g, aexp, xa = ([gt[x] for gt in gates] for x in range(4))
            st, dst = ([ref[d, n] for d, n in ch] for ref in (st_s, dst_s))
            vn, dvn, u, w = ld(vn_s), ld(dvn_s), ld(u_i), ld(w_i)
            p = [x.astype(f32) for x in ldm(p_i)]
            tm, dec = ldm(t_i), ldm(dec_i)
            cd = [cd_i[d, 0, n][0:1, 0:1] for d, n in ch]
            dp = _map(lambda m, a, b: jnp.where(m[0], _mm_nt(a, b), 0.0), masks, dob, vn)
            dqd = _map(_mm_nt, dob, st)
            dkd = _map(_mm_nt, vn, dst)
            dw = _map(lambda a, b: -_mm_nt(a, b), dvn, st)
            dcd = _map(lambda a, b: jnp.sum(jnp.sum(a * b, axis=1, keepdims=True), axis=0, keepdims=True), st, dst)
            tick()
            gcum = _map(lambda m, x: _cum_mm(m[0], jnp.broadcast_to(x, (c, c)))[:, 0:1], masks, g)
            glast = [jnp.sum(x, axis=0, keepdims=True) for x in g]
            e = [jnp.exp(x) for x in gcum]
            el = _map(lambda a, b: jnp.exp(a - b), glast, gcum)
            qs = [x * scale for x in q]
            kb = _map(lambda a, b: a * b, k, beta)
            a = _map(lambda m, b, ki, dc: jnp.where(m[1], b * _mm_nt(ki, ki) * dc, 0.0), masks, beta, k, dec)
            tick()
            dvb = _map(_hmm3_tn, tm, dvn)
            dkbe = _map(_hmm3_tn, tm, dw)
            tick()
            da = _map(lambda m, x, ui, y, wi: -jnp.where(m[1], _mm_nt(x, ui) + _mm_nt(y, wi), 0.0), masks, dvb, u, dkbe, w)
            g1 = _map(lambda x, y: x * y, da, dec)
            g2 = _map(lambda x, y: x * y, dp, dec)
            dkb = _map(lambda x, ki, y, ei: _mm(x, ki) + y * ei, g1, k, dkbe, e)
            tick()
            dk = _map(lambda x, kbi, y, qi, z, b, t, l: _mm_tn(x, kbi) + _mm_tn(y, qi) + z * b + t * l,
                      g1, kb, g2, qs, dkb, beta, dkd, el)
            dqs = _map(lambda y, ki, x, ei: _mm(y, ki) + x * ei, g2, k, dqd, e)
            tick()
            ddd = _map(lambda x, ai, y, pi: x * ai + y * pi, da, a, dp, p)
            ones = jnp.ones((c, LANES), f32)
            dgcum = _map(lambda x: jnp.sum(x, axis=1, keepdims=True) - _hmm3_tn(x, ones)[:, 0:1], ddd)
            for x, (d, n) in enumerate(ch):
                dbeta = jnp.sum(dvb[x] * v[x], axis=1, keepdims=True) + jnp.sum(dkb[x] * k[x], axis=1, keepdims=True)
                de = jnp.sum(dkbe[x] * kb[x], axis=1, keepdims=True) + jnp.sum(dqd[x] * qs[x], axis=1, keepdims=True)
                del_ = jnp.sum(dkd[x] * k[x], axis=1, keepdims=True)
                dgc = dgcum[x] + de * e[x] - del_ * el[x]
                dglast = jnp.sum(del_ * el[x], axis=0, keepdims=True) + dcd[x] * cd[x]
                dg = _cum_mm(masks[x][2], jnp.broadcast_to(dgc, (c, LANES)))[:, 0:1] + dglast
                r = rows[x]
                if d == 0:
                    dqkv_ref[0, r, :] = dqs[x] * scale
                    dqkv_ref[1, r, :] = dk[x]
                    dqkv_ref[2, r, :] = dvb[x] * beta[x]
                else:
                    dqkv_ref[0, r, :] = dqkv_ref[0, r, :] + dqs[x] * scale
                    dqkv_ref[1, r, :] = dqkv_ref[1, r, :] + dk[x]
                    dqkv_ref[2, r, :] = dqkv_ref[2, r, :] + dvb[x] * beta[x]
                dlb = dbeta * beta[x] * (1.0 - beta[x])
                dalpha = -dg * aexp[x] * _sigmoid(xa[x])
                dba_ref[r, :] = dba_ref[r, :] + jnp.where(lane == lane_b[d], dlb, 0.0) + jnp.where(lane == lane_a[d], dalpha, 0.0)
                dal_ref[...] = dal_ref[...] + jnp.where(lane1 == lane_a[d], jnp.sum(dg * g[x], axis=0, keepdims=True), 0.0)
                ddt_ref[...] = ddt_ref[...] + jnp.where(lane1 == lane_a[d], jnp.sum(dalpha, axis=0, keepdims=True), 0.0)

        half, groups = nc // 2, nc // per

        def group(k):
            return [half - 2 * k - 2, half - 2 * k - 1, half + 2 * k, half + 2 * k + 1]

        def run(gen):
            for _ in gen:
                pass

        def plain(i, carry):
            box = [carry]
            run(seq(i, box))
            return box[0]

        def woven(k, carry):
            box = [carry]

            def two_steps():
                yield from seq(half + 2 + 2 * k, box)
                yield from seq(half + 3 + 2 * k, box)

            chain = two_steps()
            bpar(group(k), lambda: next(chain, None))
            run(chain)
            return box[0]

        z0 = jnp.zeros((128, 128), f32)
        lax.fori_loop(0, groups, woven, lax.fori_loop(0, half + 2, plain, ((z0, z0), (z0, z0))))

    blk = lambda off: pl.BlockSpec((s, 128), functools.partial(lambda b, h, off: (b, off + h), off=off))
    row = pl.BlockSpec((1, LANES), lambda b, h: (0, 0))
    tok2, mat2, cd2, _ = _gdn_specs(cfg)
    return pl.pallas_call(
        body, name="gdn_bwd", grid=(cfg.BL, hh),
        in_specs=[blk(0), blk(hh), blk(2 * hh), pl.BlockSpec((s, LANES), lambda b, h: (b, ba_blk)),
                  pl.BlockSpec((s, 128), lambda b, h: (b, h)), row, row, tok2, tok2, tok2, tok2, mat2, mat2, mat2, cd2],
        out_specs=[pl.BlockSpec((3, s, 128), lambda b, h: (0, b, h)), pl.BlockSpec((s, LANES), lambda b, h: (b, 0)), row, row],
        out_shape=[jax.ShapeDtypeStruct((3, cfg.T, cfg.DN), f32), jax.ShapeDtypeStruct((cfg.T, LANES), f32),
                   jax.ShapeDtypeStruct((1, LANES), f32), jax.ShapeDtypeStruct((1, LANES), f32)],
        scratch_shapes=[pltpu.VMEM((2, s + c, 128), f32)] * 2 + [pltpu.VMEM((2, nc + 1, 128, 128), f32)] * 2,
        compiler_params=_params(("arbitrary", "arbitrary")),
    )(qkv, qkv, qkv, proj, do, alog_row, dt_row, *saved)


def _b_prep(cfg, proj, dqkv, conv_w):
    dn, s = cfg.DN, cfg.S

    def body(x_ref, dy_ref, w_ref, dx_ref, dw_ref):
        _zero_at_first(pl.program_id(1) == 0, dw_ref)
        sec = pl.program_id(0)
        x = x_ref[...]
        c = _conv_fwd(x, w_ref, 2)
        y, dsilu = _silu(c)
        dy = dy_ref[0]
        parts = []
        for h in range(cfg.H):
            sl = slice(h * 128, (h + 1) * 128)
            yh, dyh = y[:, sl], dy[:, sl]
            inv = lax.rsqrt(jnp.sum(yh * yh, axis=1, keepdims=True) + EPS)
            dn_h = inv * dyh - yh * (inv * inv * inv) * jnp.sum(dyh * yh, axis=1, keepdims=True)
            parts.append(jnp.where(sec < 2, dn_h, dyh))
        ds = jnp.concatenate(parts, axis=1) if len(parts) > 1 else parts[0]
        dx, dws = _conv_bwd(ds * dsilu, x, w_ref, 2)
        for j, dw in enumerate(dws):
            dw_ref[j:j + 1, :] = dw_ref[j:j + 1, :] + dw
        dx_ref[...] = dx.astype(bf16)

    return pl.pallas_call(
        body, name="dn_prep_bwd", grid=(3, cfg.BL),
        in_specs=[pl.BlockSpec((s, dn), lambda j, b: (b, j)), pl.BlockSpec((1, s, dn), lambda j, b: (j, b, 0)),
                  pl.BlockSpec((4, dn), lambda j, b: (0, j))],
        out_specs=[pl.BlockSpec((s, dn), lambda j, b: (b, j)), pl.BlockSpec((4, dn), lambda j, b: (0, j))],
        out_shape=[jax.ShapeDtypeStruct((cfg.T, 3 * dn), bf16), jax.ShapeDtypeStruct((4, 3 * dn), f32)],
        compiler_params=_params(("arbitrary", "arbitrary")),
    )(proj, dqkv, conv_w)


def _b_in(cfg, dqkv_pre, dz, dlx, dlg, dba, dr_mid, r_in, g1, wcat):
    d, dn, lw = cfg.D, cfg.DN, cfg.LW

    def body(dq_ref, dz_ref, dlx_ref, dlg_ref, dba_ref, dr_ref, r_ref, g_ref, w_ref, o_ref, dp_ref, dg_ref):
        _zero_at_first(pl.program_id(0) == 0, dg_ref)
        dp_ref[:, 0:cfg.ZO] = dq_ref[...].astype(bf16)
        dp_ref[:, cfg.ZO:cfg.LXO] = dz_ref[...].astype(bf16)
        dp_ref[:, cfg.LXO:cfg.LGO] = dlx_ref[...].astype(bf16)
        dp_ref[:, cfg.LGO:cfg.BAO] = dlg_ref[...].astype(bf16)
        dp_ref[:, cfg.BAO:] = dba_ref[...].astype(bf16)
        g = g_ref[...]
        _, xh, inv = _rms_fwd(r_ref[...], g)
        dx, dg = _rms_bwd(_mm(dp_ref[...], w_ref[...]), xh, inv, g)
        dg_ref[...] = dg_ref[...] + dg
        o_ref[...] = dr_ref[...] + dx

    return _row_call("in_proj_bwd", body, cfg.T, cfg.TM,
                     [(dqkv_pre, 3 * dn, 0), (dz, dn, 0), (dlx, lw, 0), (dlg, lw, 0), (dba, LANES, 0), (dr_mid, d, 0), (r_in, d, 0)],
                     [g1, wcat], [(d, f32), (cfg.PC, bf16)], [((1, d), f32)])


def _adam_call(name, w, g, m, v, tr):
    rows, cols = w.shape
    bc1 = 1.0 - ADAM_B1 ** ADAM_STEP
    bc2 = 1.0 - ADAM_B2 ** ADAM_STEP

    def body(w_ref, g_ref, m_ref, v_ref, d_ref, nm_ref, nv_ref):
        g = g_ref[...]
        m = ADAM_B1 * m_ref[...] + (1.0 - ADAM_B1) * g
        v = ADAM_B2 * v_ref[...] + (1.0 - ADAM_B2) * (g * g)
        nm_ref[...] = m
        nv_ref[...] = v
        d_ref[...] = -ADAM_LR * ((m / bc1) / (jnp.sqrt(v / bc2) + ADAM_EPS) + ADAM_WD * w_ref[...])

    spec = pl.BlockSpec((tr, cols), lambda i: (i, 0))
    return pl.pallas_call(
        body, name=name, grid=(rows // tr,), in_specs=[spec] * 4, out_specs=[spec] * 3,
        out_shape=[jax.ShapeDtypeStruct((rows, cols), f32)] * 3, compiler_params=_params(("arbitrary",)),
    )(w, g, m, v)


def _sum8_call(name, x, tr):
    _, rows, cols = x.shape

    def body(x_ref, o_ref):
        acc = x_ref[0].astype(f32)
        for j in range(1, N_DEV):
            acc = acc + x_ref[j].astype(f32)
        o_ref[...] = acc

    return pl.pallas_call(
        body, name=name, grid=(rows // tr,), in_specs=[pl.BlockSpec((N_DEV, tr, cols), lambda i: (0, i, 0))],
        out_specs=pl.BlockSpec((tr, cols), lambda i: (i, 0)), out_shape=jax.ShapeDtypeStruct((rows, cols), f32),
        compiler_params=_params(("arbitrary",)),
    )(x)


def _all_gather(name, shards):
    na = len(shards)

    def body(*refs):
        xs, outs = refs[:na], refs[na:2 * na]
        send_sems, recv_sems, local_sems = refs[2 * na:]
        x, y, c = lax.axis_index("x"), lax.axis_index("y"), lax.axis_index("c")
        me, sibling = (x, y, c), (x, y, 1 - c)
        chips = [(1 - x, y), (x, 1 - y), (1 - x, 1 - y)]

        def copy(a, k, block, to, src=None):
            px, py, pc = block
            slot = outs[a].at[4 * px + 2 * py + pc]
            return pltpu.make_async_remote_copy(
                src_ref=slot if src is None else src, dst_ref=slot,
                send_sem=send_sems.at[a, k], recv_sem=recv_sems.at[a, k], device_id=to, device_id_type=MESH)

        mine = [pltpu.make_async_copy(xs[a], outs[a].at[4 * x + 2 * y + c], local_sems.at[a]) for a in range(na)]
        for cp in mine:
            cp.start()
        first = []
        for a in range(na):
            first.append(copy(a, 0, me, sibling, src=xs[a]))
            first += [copy(a, 1 + j, me, (*chip, c), src=xs[a]) for j, chip in enumerate(chips)]
        for cp in first:
            cp.start()
        passed = []
        for j, chip in enumerate(chips):
            for a in range(na):
                copy(a, 1 + j, (*chip, c), me).wait_recv()
                cp = copy(a, 4 + j, (*chip, c), sibling)
                cp.start()
                passed.append(cp)
        for a in range(na):
            copy(a, 0, sibling, me).wait_recv()
            for j, chip in enumerate(chips):
                copy(a, 4 + j, (*chip, 1 - c), me).wait_recv()
        for cp in first + passed:
            cp.wait_send()
        for cp in mine:
            cp.wait()

    hbm = pl.BlockSpec(memory_space=pltpu.HBM)
    return pl.pallas_call(
        body, name=name, out_shape=[jax.ShapeDtypeStruct((N_DEV,) + s.shape, s.dtype) for s in shards],
        in_specs=[hbm] * na, out_specs=[hbm] * na,
        scratch_shapes=[pltpu.SemaphoreType.DMA((na, 7)), pltpu.SemaphoreType.DMA((na, 7)), pltpu.SemaphoreType.DMA((na,))],
    )(*shards)


def _peer_list():
    x, y, c = lax.axis_index("x"), lax.axis_index("y"), lax.axis_index("c")
    return 4 * x + 2 * y + c, [(x ^ (k >> 2), y ^ ((k >> 1) & 1), c ^ (k & 1)) for k in range(1, N_DEV)]


_HBM = pl.BlockSpec(memory_space=pltpu.HBM)
_SEM = pl.BlockSpec(memory_space=pltpu.SEMAPHORE)
_EFFECT = pltpu.SideEffectType.DATAFLOW_SIDE_EFFECTING


def _exchange_copies(xs, lands, send_sem, recv_sem, gather):
    me, peers = _peer_list()
    return [pltpu.make_async_remote_copy(
        src_ref=xs[a] if gather else xs[a].at[4 * px + 2 * py + pc], dst_ref=lands[a].at[me],
        send_sem=send_sem.at[7 * a + k], recv_sem=recv_sem.at[7 * a + k], device_id=(px, py, pc), device_id_type=MESH)
        for k, (px, py, pc) in enumerate(peers) for a in range(len(xs))]


def _exchange_start(name, blocks, gather=False):
    na = len(blocks)

    def body(*refs):
        for cp in _exchange_copies(refs[:na], refs[na:2 * na], refs[2 * na], refs[2 * na + 1], gather):
            cp.start()
        refs[-1][...] = jnp.zeros_like(refs[-1])

    lands = [jax.ShapeDtypeStruct((N_DEV,) + b.shape if gather else b.shape, b.dtype) for b in blocks]
    hbm = [pltpu.HBM(b.shape, b.dtype) for b in blocks] + [pltpu.HBM(b.shape, b.dtype) for b in lands]
    send_sem, recv_sem, *thru, token = pl.pallas_call(
        body, name=name,
        out_shape=(pltpu.SemaphoreType.DMA((7 * na,)), pltpu.SemaphoreType.DMA((7 * na,)), *hbm,
                   jax.ShapeDtypeStruct((8, LANES), f32)),
        in_specs=[_HBM] * (2 * na), out_specs=(_SEM, _SEM, *([_HBM] * (2 * na)), pl.BlockSpec(memory_space=pltpu.VMEM)),
        input_output_aliases={i: 2 + i for i in range(2 * na)},
        compiler_params=pltpu.CompilerParams(has_side_effects=_EFFECT),
    )(*[pltpu.with_memory_space_constraint(b, pltpu.HBM) for b in blocks],
      *[pltpu.with_memory_space_constraint(lax.empty(b.shape, b.dtype), pltpu.HBM) for b in lands])
    return send_sem, recv_sem, thru, token


def _exchange_wait(name, send_sem, recv_sem, thru, after, gather=False):
    na = len(thru) // 2

    def body(*refs):
        for cp in _exchange_copies(refs[:na], refs[na:2 * na], refs[2 * na], refs[2 * na + 1], gather):
            cp.wait_send()
            cp.wait_recv()

    return pl.pallas_call(
        body, name=name, out_shape=tuple(pltpu.HBM(t.shape, t.dtype) for t in thru),
        in_specs=[_HBM] * (2 * na) + [_SEM, _SEM, pl.BlockSpec(memory_space=pl.ANY)], out_specs=tuple([_HBM] * (2 * na)),
        input_output_aliases={i: i for i in range(2 * na)},
        compiler_params=pltpu.CompilerParams(has_side_effects=_EFFECT),
    )(*thru, send_sem, recv_sem, after)[na:]


def _layer_fwd(cfg, w, r, p):
    h1, proj = _k_in(cfg, r, w["norm1_g"], w["wcat_t"])
    qkv = _k_prep(cfg, proj, w["dn_conv_w"])
    o, *gdn_saved = _k_gdn_fwd(cfg, qkv, proj, w["alog_row"], w["dt_row"])
    hs = _k_lru_fwd(cfg, proj, w["lru_conv_w"], w["lru_conv_b"], w["wa"], w["wx"], w["lru_ba"], w["lru_bx"], w["lru_lambda"])
    if "late" in w:
        w.update(w.pop("late")(hs))
    mix, r_mid = _k_mix(cfg, o, proj, hs, r, w["dn_norm_g"], w["lru_norm_g"], w["w_out"])
    h2, gp, up = _k_ffn_a(cfg, r_mid, w["norm2_g"], w["ffn_wg_t"], w["ffn_wu_t"])
    act = _k_ffn_b(cfg, gp, up, w["ffn_conv_w"], w["ffn_conv_b"])
    r2 = _k_ffn_c(cfg, act, r_mid, w["ffn_wd"])
    pn, r3 = _k_ple(cfg, r2, p, w["ple_norm_g"], w["ple_wg"], w["ple_bg"], w["ple_wp_t"])
    saved = dict(r=r, h1=h1, proj=proj, qkv=qkv, o=o, gdn=gdn_saved, hs=hs, mix=mix, r_mid=r_mid, h2=h2, gp=gp, up=up,
                 r2=r2, pn=pn, p=p)
    return r3, saved


def _layer_bwd(cfg, w, sv, dr3, early=None):
    g = {}
    dt = min(512, cfg.D)
    dr2, dr2b, dlog, dpp, g["ple_norm_g"], g["ple_bg"] = _b_ple(cfg, dr3, sv["r2"], sv["p"], w["ple_norm_g"], w["ple_wg"], w["ple_bg"], w["ple_wp_t"])
    g["ple_wg"] = _mm_tn_call(cfg, "d_ple_wg", sv["pn"], dlog, dt)
    g["ple_wp_t"] = _mm_tn_call(cfg, "d_ple_wp", dpp, sv["p"], cfg.PD)
    dgp, dup, g["ffn_wd"], g["ffn_conv_w"], g["ffn_conv_b"] = _b_ffn_bc(cfg, dr2b, w["ffn_wd"], sv["gp"], sv["up"], w["ffn_conv_w"], w["ffn_conv_b"])
    dr_mid, g["norm2_g"] = _b_ffn_a(cfg, dgp, dup, dr2, sv["r_mid"], w["norm2_g"], w["ffn_wg_t"], w["ffn_wu_t"])
    g["ffn_wg_t"] = _mm_tn_call(cfg, "d_ffn_wg", dgp, sv["h2"], dt)
    g["ffn_wu_t"] = _mm_tn_call(cfg, "d_ffn_wu", dup, sv["h2"], dt)
    if early is not None:
        w = dict(w, dn_norm_g=w["dn_norm_g"] + early(EARLY, g)[0, 0])
    do, dz, dlg, dhs, g["dn_norm_g"], g["lru_norm_g"] = _b_mix(cfg, dr_mid, sv["o"], sv["proj"], sv["hs"], w["dn_norm_g"], w["lru_norm_g"], w["w_out"])
    g["w_out"] = _mm_tn_call(cfg, "d_w_out", sv["mix"], dr_mid, dt)
    if early is not None:
        w = dict(w, lru_conv_b=w["lru_conv_b"] + early(("w_out",), g)[0, 0])
    dlx, g["lru_conv_w"], g["lru_conv_b"], g["wa"], g["wx"], g["lru_ba"], g["lru_bx"], g["lru_lambda"] = _b_lru(
        cfg, sv["proj"], dhs, w["lru_conv_w"], w["lru_conv_b"], w["wa"], w["wx"], w["lru_ba"], w["lru_bx"], w["lru_lambda"])
    dqkv, dba, g["alog_row"], g["dt_row"] = _b_gdn(cfg, sv["qkv"], sv["proj"], do, w["alog_row"], w["dt_row"], sv["gdn"])
    dqkv_pre, g["dn_conv_w"] = _b_prep(cfg, sv["proj"], dqkv, w["dn_conv_w"])
    dr, dproj, g["norm1_g"] = _b_in(cfg, dqkv_pre, dz, dlx, dlg, dba, dr_mid, sv["r"], w["norm1_g"], w["wcat_t"])
    g["wcat_t"] = _mm_tn_call(cfg, "d_w_in", dproj, sv["h1"], dt)
    return dr, g


BIG = ("w_in", "w_out", "ffn_wg", "ffn_wu", "ffn_wd", "ple_wg", "ple_wp")
BIG_T = {"w_in": True, "w_out": False, "ffn_wg": True, "ffn_wu": True, "ffn_wd": False, "ple_wg": False, "ple_wp": True}
BIG_OPERAND = {"w_in": "wcat_t", "w_out": "w_out", "ffn_wg": "ffn_wg_t", "ffn_wu": "ffn_wu_t", "ffn_wd": "ffn_wd",
               "ple_wg": "ple_wg", "ple_wp": "ple_wp_t"}
EARLY = ("ffn_wg", "ffn_wu", "ffn_wd", "ple_wg", "ple_wp")
SMALL_SHARDED = ("dn_conv_w", "lru_conv_w", "lru_ba", "lru_bx", "lru_lambda", "ffn_conv_w")
SMALL_REPL = ("norm1_g", "dn_a_log", "dn_dt_bias", "dn_norm_g", "lru_conv_b", "lru_wa", "lru_wx", "lru_norm_g", "norm2_g",
              "ffn_conv_b", "ple_norm_g", "ple_bg", "final_g")
WEIGHTS = ("norm1_g", "w_in", "dn_conv_w", "dn_a_log", "dn_dt_bias", "dn_norm_g", "lru_conv_w", "lru_conv_b", "lru_wa",
           "lru_ba", "lru_wx", "lru_bx", "lru_lambda", "lru_norm_g", "w_out", "norm2_g", "ffn_wg", "ffn_wu", "ffn_conv_w",
           "ffn_conv_b", "ffn_wd", "ple_norm_g", "ple_wg", "ple_bg", "ple_wp", "final_g")


def _pad_rows(flat, cols, mult):
    n = flat.shape[0]
    rows = -(-n // cols)
    rows = -(-rows // mult) * mult
    return jnp.pad(flat, (0, rows * cols - n)).reshape(rows, cols)


def _pack(arrs, cols, mult, dtype):
    return _pad_rows(jnp.concatenate([a.reshape(-1).astype(dtype) for a in arrs]), cols, mult)


def _unpack(flat, shapes):
    out, off = [], 0
    for shp in shapes:
        n = math.prod(shp)
        piece = flat[off:off + n]
        if n < 4096:
            piece = lax.optimization_barrier(piece)
        out.append(piece.reshape(shp))
        off += n
    return out


def _unpack8(g8, shapes, axes):
    out, off = [], 0
    for shp, ax in zip(shapes, axes):
        n = math.prod(shp)
        a = g8[:, off:off + n].reshape((N_DEV,) + tuple(shp))
        a = jnp.moveaxis(a, 0, ax)
        out.append(a.reshape(shp[:ax] + (N_DEV * shp[ax],) + shp[ax + 1:]))
        off += n
    return out


def _wcat_t_from_w_in_t(cfg, wt):
    nba = 4 * cfg.H
    pad = jnp.zeros((LANES - nba, wt.shape[1]), wt.dtype)
    return jnp.concatenate([wt[:cfg.LXO], wt[cfg.LXO + nba:], wt[cfg.LXO:cfg.LXO + nba], pad], axis=0)


def _w_in_t_from_wcat_t(cfg, wc):
    nba = 4 * cfg.H
    return jnp.concatenate([wc[:cfg.LXO], wc[cfg.BAO:cfg.BAO + nba], wc[cfg.LXO:cfg.BAO]], axis=0)


def _gate_row(cfg, a):
    h2 = 2 * cfg.H
    return jnp.concatenate([jnp.zeros((1, h2), f32), a.reshape(1, h2), jnp.zeros((1, LANES - 2 * h2), f32)], axis=1)


def _blockdiag(cfg, w):
    w = w.reshape(2, cfg.NB // 2, 2, 64, 64)
    z = jnp.zeros_like(w[:, :, 0])
    top = jnp.concatenate([w[:, :, 0], z], axis=-1)
    bot = jnp.concatenate([z, w[:, :, 1]], axis=-1)
    return jnp.concatenate([top, bot], axis=-2).astype(bf16)


def _unblockdiag(cfg, g):
    a = g[:, :, :64, :64]
    b = g[:, :, 64:, 64:]
    return jnp.stack([a, b], axis=2).reshape(2, cfg.NB, 64, 64)


def _big_operands(cfg, big):
    return {BIG_OPERAND[n]: (_wcat_t_from_w_in_t(cfg, v) if n == "w_in" else v) for n, v in big.items()}


def _layer_operands(cfg, big, small, i):
    return dict(
        _big_operands(cfg, big),
        norm1_g=small["norm1_g"][i][None], dn_conv_w=small["dn_conv_w"][i], alog_row=_gate_row(cfg, small["dn_a_log"][i]),
        dt_row=_gate_row(cfg, small["dn_dt_bias"][i]), dn_norm_g=small["dn_norm_g"][i][None],
        lru_conv_w=small["lru_conv_w"][i], lru_conv_b=small["lru_conv_b"][i][None],
        wa=_blockdiag(cfg, small["lru_wa"][i]), wx=_blockdiag(cfg, small["lru_wx"][i]),
        lru_ba=small["lru_ba"][i], lru_bx=small["lru_bx"][i], lru_lambda=small["lru_lambda"][i],
        lru_norm_g=small["lru_norm_g"][i][None], norm2_g=small["norm2_g"][i][None], ffn_conv_w=small["ffn_conv_w"][i],
        ffn_conv_b=small["ffn_conv_b"][i][None], ple_norm_g=small["ple_norm_g"][i][None], ple_bg=small["ple_bg"][i][None],
    )


def _small_grads_to_problem(cfg, g):
    h = cfg.H
    return dict(
        norm1_g=g["norm1_g"][0], dn_conv_w=g["dn_conv_w"],
        dn_a_log=g["alog_row"][0, 2 * h:4 * h].reshape(2, h), dn_dt_bias=g["dt_row"][0, 2 * h:4 * h].reshape(2, h),
        dn_norm_g=g["dn_norm_g"][0], lru_conv_w=g["lru_conv_w"], lru_conv_b=g["lru_conv_b"][0],
        lru_wa=_unblockdiag(cfg, g["wa"]), lru_wx=_unblockdiag(cfg, g["wx"]), lru_ba=g["lru_ba"], lru_bx=g["lru_bx"],
        lru_lambda=g["lru_lambda"], lru_norm_g=g["lru_norm_g"][0], norm2_g=g["norm2_g"][0], ffn_conv_w=g["ffn_conv_w"],
        ffn_conv_b=g["ffn_conv_b"][0], ple_norm_g=g["ple_norm_g"][0], ple_bg=g["ple_bg"][0],
    )


def _local_step(cfg, get_big, small, x, p, target, on_big_grads):
    r = x.reshape(cfg.T, cfg.D)
    ops, saved = [], []
    for i in range(cfg.L):
        big, token, late = get_big(i, r)
        w = _layer_operands(cfg, big, small, i)
        if token is not None:
            w["norm1_g"] = w["norm1_g"] + token[0, 0]
        if late is not None:
            w["late"] = late
        r, sv = _layer_fwd(cfg, w, r, p[i].reshape(cfg.T, cfg.PD))
        ops.append(w)
        saved.append(sv)
    dr, loss, dgf = _k_loss(cfg, r, target.reshape(cfg.T, cfg.D), small["final_g"][None])
    gsmall = [None] * cfg.L
    for i in reversed(range(cfg.L)):
        first = EARLY + ("w_out",) if i == 0 else ()
        early = (lambda names, g: on_big_grads("%d%s" % (i, names[0]), {n: g[BIG_OPERAND[n]] for n in names})) if first else None
        dr, g = _layer_bwd(cfg, ops[i], saved[i], dr, early)
        token = on_big_grads("%d" % i, {n: (_w_in_t_from_wcat_t(cfg, g["wcat_t"]) if n == "w_in" else g[BIG_OPERAND[n]])
                                        for n in BIG if n not in first})
        if i > 0:
            ops[i - 1]["ple_bg"] = ops[i - 1]["ple_bg"] + token[0, 0]
        gsmall[i] = _small_grads_to_problem(cfg, g)
    gs = {k: jnp.stack([gl[k] for gl in gsmall]) for k in gsmall[0]}
    gs["final_g"] = dgf[0]
    return loss, dr, gs


def _row_tile(rows, limit=512):
    best = rows
    for t in range(8, min(rows, limit) + 1, 8):
        if rows % t == 0:
            best = t
    return best if best <= limit or rows <= limit else rows


def _adam_group(name, ws, gs, ms, vs, cols, tr):
    shapes = [w.shape for w in ws]
    pk = lambda arrs: _pack(arrs, cols, tr, f32)
    w2 = pk(ws)
    d, nm, nv = _adam_call(name, w2, pk(gs), pk(ms), pk(vs), min(tr, w2.shape[0]))
    return [_unpack(a.reshape(-1), shapes) for a in (d, nm, nv)]


def kernel(x, p, norm1_g, w_in, dn_conv_w, dn_a_log, dn_dt_bias, dn_norm_g, lru_conv_w, lru_conv_b, lru_wa, lru_ba, lru_wx, lru_bx, lru_lambda, lru_norm_g, w_out, norm2_g, ffn_wg, ffn_wu, ffn_conv_w, ffn_conv_b, ffn_wd, ple_norm_g, ple_wg, ple_bg, ple_wp, final_g, loss_target, m_norm1_g, m_w_in, m_dn_conv_w, m_dn_a_log, m_dn_dt_bias, m_dn_norm_g, m_lru_conv_w, m_lru_conv_b, m_lru_wa, m_lru_ba, m_lru_wx, m_lru_bx, m_lru_lambda, m_lru_norm_g, m_w_out, m_norm2_g, m_ffn_wg, m_ffn_wu, m_ffn_conv_w, m_ffn_conv_b, m_ffn_wd, m_ple_norm_g, m_ple_wg, m_ple_bg, m_ple_wp, m_final_g, v_norm1_g, v_w_in, v_dn_conv_w, v_dn_a_log, v_dn_dt_bias, v_dn_norm_g, v_lru_conv_w, v_lru_conv_b, v_lru_wa, v_lru_ba, v_lru_wx, v_lru_bx, v_lru_lambda, v_lru_norm_g, v_w_out, v_norm2_g, v_ffn_wg, v_ffn_wu, v_ffn_conv_w, v_ffn_conv_b, v_ffn_wd, v_ple_norm_g, v_ple_wg, v_ple_bg, v_ple_wp, v_final_g):
    cfg = CFG
    a = dict(locals())
    wl = {n: a[n] for n in WEIGHTS}
    ml = {n: a["m_" + n] for n in WEIGHTS}
    vl = {n: a["v_" + n] for n in WEIGHTS}
    me = 4 * lax.axis_index("x") + 2 * lax.axis_index("y") + lax.axis_index("c")
    nl = cfg.L

    blocks = [(jnp.swapaxes(wl[n], 1, 2) if BIG_T[n] else wl[n]).astype(bf16) for n in BIG]
    ss_shapes = [wl[n].shape for n in SMALL_SHARDED]
    first, s8 = _all_gather("gather_weights_0", [blocks[0][0], _pack([wl[n] for n in SMALL_SHARDED], LANES, 8, f32)])
    small = dict(zip(SMALL_SHARDED, _unpack8(s8.reshape(N_DEV, -1), ss_shapes, [2] * len(ss_shapes))))
    small.update({n: wl[n] for n in SMALL_REPL})

    def start_gather(key, i, names, behind):
        shards, _ = lax.optimization_barrier(([blk[i] for n, blk in zip(BIG, blocks) if n in names], behind))
        return _exchange_start("gather_start_" + key, shards, gather=True)

    def wait_gather(key, i, names, started, behind):
        send_sem, recv_sem, thru, _ = started
        lands = _exchange_wait("gather_wait_" + key, send_sem, recv_sem, thru, behind, gather=True)
        own = [blk[i][None] for n, blk in zip(BIG, blocks) if n in names]
        full = [lax.dynamic_update_slice_in_dim(land, o, me, 0) for land, o in zip(lands, own)]
        return {n: f.reshape(N_DEV * f.shape[1], f.shape[2]) for n, f in zip([n for n in BIG if n in names], full)}

    ahead = {}

    def get_big(i, r):
        if i > 0:
            big = wait_gather("%d" % i, i, BIG, ahead.pop(i), r)
            if i + 1 < nl:
                ahead[i + 1] = start_gather("%d" % (i + 1), i + 1, BIG, big["w_in"])
            return big, (ahead[i + 1][3] if i + 1 < nl else None), None
        rest = start_gather("0r", 0, BIG[1:], first)

        def late(x):
            big = wait_gather("0r", 0, BIG[1:], rest, x)
            ops = _big_operands(cfg, big)
            if nl > 1:
                ahead[1] = start_gather("1", 1, BIG, big["w_out"])
                ops["norm2_g"] = small["norm2_g"][0][None] + ahead[1][3][0, 0]
            return ops

        return {"w_in": first.reshape(N_DEV * first.shape[1], first.shape[2])}, rest[3], late

    pending = {}

    def on_big_grads(key, g):
        send = [g[n].reshape((N_DEV,) + blk.shape[1:]).astype(bf16) for n, blk in zip(BIG, blocks) if n in g]
        own = [lax.dynamic_index_in_dim(sd, me, 0, keepdims=True) for sd in send]
        send_sem, recv_sem, thru, token = _exchange_start("exchange_start_" + key, send)
        pending[key] = (send_sem, recv_sem, thru, own, [n for n in BIG if n in g])
        return token

    loss_part, dr, gsmall = _local_step(cfg, get_big, small, x, p, loss_target, on_big_grads)
    grad_x = dr.reshape(x.shape)

    sums = {n: [None] * nl for n in BIG}
    for key in sorted(pending):
        send_sem, recv_sem, thru, own, names = pending[key]
        lands = _exchange_wait("exchange_wait_" + key, send_sem, recv_sem, thru, dr)
        for n, land, o in zip(names, lands, own):
            slots = lax.dynamic_update_slice_in_dim(land, o, me, 0)
            sums[n][int(key[0])] = _sum8_call("sum_%s_%s" % (n, key), slots, slots.shape[1])
    gl = {}
    for n in BIG:
        s = jnp.stack(sums[n])
        gl[n] = jnp.swapaxes(s, 1, 2) if BIG_T[n] else s

    small_names = SMALL_REPL + SMALL_SHARDED
    narrow = ("lru_wa", "lru_wx")
    wide = [n for n in small_names if n not in narrow]
    wide_shapes = [gsmall[n].shape for n in wide]
    sv = _pack([gsmall[n] for n in wide] + [loss_part[0, 0:1]], LANES, 512, f32)
    nv = _pack([gsmall[n] for n in narrow], LANES, 512, bf16)
    sv8, nv8 = _all_gather("gather_small_grads", [sv, nv])
    small_sum = _sum8_call("sum_small", sv8, 512).reshape(-1)
    narrow_sum = _sum8_call("sum_small_narrow", nv8, 512).reshape(-1)
    gl.update(zip(wide, _unpack(small_sum, wide_shapes)))
    gl.update(zip(narrow, _unpack(narrow_sum, [gsmall[n].shape for n in narrow])))
    loss = small_sum[sum(math.prod(s) for s in wide_shapes)]
    for n in SMALL_SHARDED:
        shard = wl[n].shape[2]
        gl[n] = lax.dynamic_slice_in_dim(gl[n], me * shard, shard, axis=2)

    outs = {}
    for n in BIG:
        shp = wl[n].shape
        two = lambda t: t.reshape(-1, shp[-1])
        d, nm, nv = _adam_call("adam_" + n, two(wl[n]), two(gl[n]), two(ml[n]), two(vl[n]), _row_tile(math.prod(shp[:-1])))
        outs[n] = (d.reshape(shp), nm.reshape(shp), nv.reshape(shp))
    d, nm, nv = _adam_group("adam_small", [wl[n] for n in small_names], [gl[n] for n in small_names],
                            [ml[n] for n in small_names], [vl[n] for n in small_names], LANES, 64)
    for j, n in enumerate(small_names):
        outs[n] = (d[j], nm[j], nv[j])
    return (loss, grad_x, *[gl[n] for n in WEIGHTS], *[outs[n][0] for n in WEIGHTS], *[outs[n][1] for n in WEIGHTS],
            *[outs[n][2] for n in WEIGHTS])
```

```python
import functools
import math

import jax
import jax.numpy as jnp
from jax import lax
from jax.experimental import pallas as pl
from jax.experimental.pallas import tpu as pltpu

f32 = jnp.float32
bf16 = jnp.bfloat16
MESH = pl.DeviceIdType.MESH

N_DEV = 8
LANES = 128
EPS = 1e-6
LRU_C = 8.0
ADAM_LR, ADAM_B1, ADAM_B2, ADAM_EPS, ADAM_WD, ADAM_STEP = 0.001, 0.9, 0.999, 1e-08, 0.01, 10
VMEM_LIMIT = 56 * 1024 * 1024


class Cfg:
    def __init__(self, d_model=1024, bl=4, seq=2048, depth=4, heads=4, lru_width=512, d_ff=2816, ple=256,
                 tm=512, tm_ffn=256, ff_tile=256):
        self.D, self.BL, self.S, self.L, self.H = d_model, bl, seq, depth, heads
        self.DH = 128
        self.DN = heads * self.DH
        self.LW = lru_width
        self.NB = lru_width // 64
        self.FF, self.PD = d_ff, ple
        self.C = 64
        self.NC = seq // self.C
        self.T = bl * seq
        self.TM = min(tm, self.T)
        self.TMF = min(tm_ffn, self.T)
        self.TMW = min(2 * tm, self.T)
        self.FT = ff_tile
        self.ZO = 3 * self.DN
        self.LXO = 4 * self.DN
        self.LGO = self.LXO + self.LW
        self.BAO = self.LGO + self.LW
        self.PC = self.BAO + LANES
        self.IN_COLS = 4 * self.DN + 4 * heads + 2 * self.LW


CFG = Cfg()


def _mm(a, b):
    return jnp.dot(a.astype(bf16), b.astype(bf16), preferred_element_type=f32)


def _mm_nt(a, b):
    return lax.dot_general(a.astype(bf16), b.astype(bf16), (((1,), (1,)), ((), ())), preferred_element_type=f32)


def _mm_tn(a, b):
    return lax.dot_general(a.astype(bf16), b.astype(bf16), (((0,), (0,)), ((), ())), preferred_element_type=f32)


def _split2(a):
    hi = a.astype(bf16)
    return hi, (a - hi.astype(f32)).astype(bf16)


def _hmm3(a, b, dims=(((1,), (0,)), ((), ()))):
    ah, al = _split2(a)
    bh, bl = _split2(b)
    dot = functools.partial(lax.dot_general, dimension_numbers=dims, preferred_element_type=f32)
    return dot(ah, bh) + dot(ah, bl) + dot(al, bh)


def _hmm3_tn(a, b):
    return _hmm3(a, b, (((0,), (0,)), ((), ())))


def _cum_mm(mask, x, dims=(((1,), (0,)), ((), ()))):
    m = mask.astype(bf16)
    x1 = x.astype(bf16)
    r = x - x1.astype(f32)
    x2 = r.astype(bf16)
    x3 = (r - x2.astype(f32)).astype(bf16)
    dot = functools.partial(lax.dot_general, dimension_numbers=dims, preferred_element_type=f32)
    return dot(m, x1) + dot(m, x2) + dot(m, x3)


def _rms_fwd(x, g):
    inv = lax.rsqrt(jnp.mean(x * x, axis=-1, keepdims=True) + EPS)
    xh = x * inv
    return xh * g, xh, inv


def _rms_bwd(dy, xh, inv, g):
    dxh = dy * g
    dx = inv * (dxh - xh * jnp.mean(dxh * xh, axis=-1, keepdims=True))
    dg = jnp.sum(dy * xh, axis=0, keepdims=True)
    return dx, dg


def _sigmoid(x):
    return 1.0 / (1.0 + jnp.exp(-x))


def _softplus(x):
    return jnp.maximum(x, 0.0) + jnp.log(1.0 + jnp.exp(-jnp.abs(x)))


def _silu(x):
    s = _sigmoid(x)
    return x * s, s * (1.0 + x * (1.0 - s))


_GC = math.sqrt(2.0 / math.pi)


def _gelu(x):
    t = jnp.tanh(_GC * (x + 0.044715 * x * x * x))
    y = 0.5 * x * (1.0 + t)
    dy = 0.5 * (1.0 + t) + 0.5 * x * (1.0 - t * t) * _GC * (1.0 + 3.0 * 0.044715 * x * x)
    return y, dy


def _nexpm1(x):
    ser = -x * (1.0 + x * 0.5 * (1.0 + x * (1.0 / 3.0) * (1.0 + x * 0.25 * (1.0 + x * 0.2))))
    return jnp.where(x > -0.1, ser, 1.0 - jnp.exp(x))


def _shift(x, s, fill=0.0):
    if s == 0:
        return x
    n = x.shape[0]
    t = lax.broadcasted_iota(jnp.int32, x.shape, 0)
    r = pltpu.roll(x, (-s) % n, 0)
    return jnp.where((t + s >= 0) & (t + s < n), r, fill)


def _conv_fwd(x, w_ref, left):
    k = w_ref.shape[0]
    out = _shift(x, -left) * w_ref[0:1, :]
    for j in range(1, k):
        out = out + _shift(x, j - left) * w_ref[j:j + 1, :]
    return out


def _conv_bwd(dout, x, w_ref, left):
    k = w_ref.shape[0]
    dx = None
    dws = []
    for j in range(k):
        term = _shift(dout, -(j - left)) * w_ref[j:j + 1, :]
        dx = term if dx is None else dx + term
        dws.append(jnp.sum(dout * _shift(x, j - left), axis=0, keepdims=True))
    return dx, dws


def _scan_refs(scans, n):
    blk = 64
    nb = n // blk
    sub = lax.broadcasted_iota(jnp.int32, (blk, LANES), 0) & 7

    def local(a, b, rev):
        for d in (1, 2, 4):
            ok = (sub < 8 - d) if rev else (sub >= d)
            sh = (blk - d) if rev else d
            b = a * jnp.where(ok, pltpu.roll(b, sh, 0), 0.0) + b
            a = a * jnp.where(ok, pltpu.roll(a, sh, 0), 1.0)
        return a, b

    def body(i, carries):
        new = []
        for (a_ref, b_ref, h_ref, rev), carry in zip(scans, carries):
            j = (nb - 1 - i) if rev else i
            base = pl.multiple_of(j * blk, blk)
            a, b = local(a_ref[pl.ds(base, blk), :], b_ref[pl.ds(base, blk), :], rev)
            order = range(blk // 8 - 1, -1, -1) if rev else range(blk // 8)
            for v in order:
                h = b[8 * v:8 * v + 8, :] + a[8 * v:8 * v + 8, :] * carry
                h_ref[pl.ds(base + 8 * v, 8), :] = h
                carry = h[0:1, :] if rev else h[7:8, :]
            new.append(carry)
        return tuple(new)

    lax.fori_loop(0, nb, body, tuple(jnp.zeros((1, LANES), f32) for _ in scans))


def _tri_masks(c, rev):
    i = lax.broadcasted_iota(jnp.int32, (c, c), 0)
    j = lax.broadcasted_iota(jnp.int32, (c, c), 1)
    incl = (i <= j) if rev else (i >= j)
    strict = (i < j) if rev else (i > j)
    incl_t = (i >= j) if rev else (i <= j)
    return incl, strict, incl_t


def _map(f, *lists):
    return [f(*a) for a in zip(*lists)]


def _tri_inv(mats, tick=lambda: None):
    c = mats[0].shape[0]
    i = lax.broadcasted_iota(jnp.int32, (c, c), 0)
    j = lax.broadcasted_iota(jnp.int32, (c, c), 1)
    eye = jnp.where(i == j, 1.0, 0.0)
    t = [eye - a for a in mats]
    pw = _map(_hmm3, mats, mats)
    for it in range(5):
        tick()
        t = _map(lambda ti, ui: ti + ui, t, _map(_hmm3, t, pw))
        if it < 4:
            pw = _map(_hmm3, pw, pw)
    return t


def _gdn_decay(gs, revs):
    c = gs[0].shape[0]
    masks = [_tri_masks(c, r) for r in revs]
    gb = [jnp.broadcast_to(g, (c, c)) for g in gs]
    mcol = _map(lambda m, x: _cum_mm(m[0], x), masks, gb)
    mrow = _map(lambda m, x: jnp.sum(jnp.where(m[2], x, 0.0), axis=0, keepdims=True), masks, gb)
    dec = _map(lambda m, a, b: jnp.exp(jnp.where(m[0], a - b, -1e30)), masks, mcol, mrow)
    return [m[:, 0:1] for m in mcol], [jnp.sum(g, axis=0, keepdims=True) for g in gs], dec


def _gdn_prep(qs, k, v, g, beta, kk, qk, revs, tick=lambda: None):
    c = k[0].shape[0]
    masks = [_tri_masks(c, r) for r in revs]
    gcum, glast, dec = _gdn_decay(g, revs)
    e = [jnp.exp(x) for x in gcum]
    tick()
    tm = _tri_inv(_map(lambda m, b, x, d: jnp.where(m[1], b * x * d, 0.0), masks, beta, kk, dec), tick)
    tick()
    u = _map(lambda t, vi, b: _hmm3(t, vi * b), tm, v, beta)
    tick()
    w = _map(lambda t, ki, b, ei: _hmm3(t, ki * b * ei), tm, k, beta, e)
    p = _map(lambda m, x, d: jnp.where(m[0], x * d, 0.0), masks, qk, dec)
    return dict(dec=dec, cd=[jnp.exp(x) for x in glast], tm=tm, u=u, w=w, p=p, qd=_map(lambda a, b: a * b, qs, e),
                kd=_map(lambda ki, gl, gc: ki * jnp.exp(gl - gc), k, glast, gcum))


def _lane_pick(x, lane):
    l = lax.broadcasted_iota(jnp.int32, x.shape, 1)
    return jnp.sum(jnp.where(l == lane, x, 0.0), axis=1, keepdims=True)


def _params(sem, vmem=VMEM_LIMIT):
    return pltpu.CompilerParams(dimension_semantics=sem, vmem_limit_bytes=vmem)


def _row_call(name, body, t, tm, row_ins, full_ins, row_outs, acc_outs):
    in_specs = [pl.BlockSpec((tm, w), functools.partial(lambda i, c: (i, c), c=c)) for (_, w, c) in row_ins]
    for a in full_ins:
        in_specs.append(pl.BlockSpec(a.shape, functools.partial(lambda i, n: (0,) * n, n=a.ndim)))
    out_specs = [pl.BlockSpec((tm, w), lambda i: (i, 0)) for (w, _) in row_outs]
    out_shape = [jax.ShapeDtypeStruct((t, w), dt) for (w, dt) in row_outs]
    for shp, dt in acc_outs:
        out_specs.append(pl.BlockSpec(shp, functools.partial(lambda i, n: (0,) * n, n=len(shp))))
        out_shape.append(jax.ShapeDtypeStruct(shp, dt))
    return pl.pallas_call(
        body, name=name, grid=(t // tm,), in_specs=in_specs, out_specs=out_specs, out_shape=out_shape,
        compiler_params=_params(("arbitrary",)),
    )(*[a for (a, _, _) in row_ins], *full_ins)


def _k_in(cfg, r, g1, wcat):
    def body(r_ref, g_ref, w_ref, h_ref, proj_ref):
        y, _, _ = _rms_fwd(r_ref[...], g_ref[...])
        hb = y.astype(bf16)
        h_ref[...] = hb
        proj_ref[...] = _mm_nt(hb, w_ref[...])

    return _row_call("in_proj", body, cfg.T, cfg.TM, [(r, cfg.D, 0)], [g1, wcat],
                     [(cfg.D, bf16), (cfg.PC, f32)], [])


def _k_prep(cfg, proj, conv_w):
    dn, s = cfg.DN, cfg.S

    def body(x_ref, w_ref, o_ref):
        sec = pl.program_id(1)
        c = _conv_fwd(x_ref[...], w_ref, 2)
        y, _ = _silu(c)
        for h in range(cfg.H):
            yh = y[:, h * 128:(h + 1) * 128]
            nh = yh * lax.rsqrt(jnp.sum(yh * yh, axis=1, keepdims=True) + EPS)
            o_ref[:, h * 128:(h + 1) * 128] = jnp.where(sec < 2, nh, yh)

    return pl.pallas_call(
        body, name="dn_prep", grid=(cfg.BL, 3),
        in_specs=[pl.BlockSpec((s, dn), lambda b, j: (b, j)), pl.BlockSpec((4, dn), lambda b, j: (0, j))],
        out_specs=pl.BlockSpec((s, dn), lambda b, j: (b, j)),
        out_shape=jax.ShapeDtypeStruct((cfg.T, 3 * dn), f32),
        compiler_params=_params(("arbitrary", "arbitrary")),
    )(proj, conv_w)


def _gate_cols(ba, alog_row, dt_row, lane_b, lane_a):
    beta = _sigmoid(_lane_pick(ba, lane_b))
    alpha = _lane_pick(ba, lane_a)
    aexp = jnp.exp(_lane_pick(alog_row, lane_a))
    dtb = _lane_pick(dt_row, lane_a)
    xa = alpha + dtb
    g = -aexp * _softplus(xa)
    return beta, g, aexp, xa


def _gdn_specs(cfg):
    s, c, nc = cfg.S, cfg.C, cfg.NC
    tok2 = pl.BlockSpec((2, s, 128), lambda b, h: (0, b, h))
    mat2 = pl.BlockSpec((2, 1, nc, c, c), lambda b, h: (0, b, h, 0, 0))
    cd2 = pl.BlockSpec((2, 1, nc, 8, LANES), lambda b, h: (0, b, h, 0, 0))
    shapes = dict(
        tok32=jax.ShapeDtypeStruct((2, cfg.T, cfg.DN), f32), tok16=jax.ShapeDtypeStruct((2, cfg.T, cfg.DN), bf16),
        mat32=jax.ShapeDtypeStruct((2, cfg.BL, cfg.H * nc, c, c), f32), mat16=jax.ShapeDtypeStruct((2, cfg.BL, cfg.H * nc, c, c), bf16),
        cd=jax.ShapeDtypeStruct((2, cfg.BL, cfg.H * nc, 8, LANES), f32))
    return tok2, mat2, cd2, shapes


def _k_gdn_fwd(cfg, qkv, proj, alog_row, dt_row):
    s, c, nc, hh = cfg.S, cfg.C, cfg.NC, cfg.H
    ba_blk = cfg.BAO // LANES
    per = 4 if nc % 4 == 0 else 1

    def body(q_ref, k_ref, v_ref, ba_ref, al_ref, dt_ref, o_ref, u_o, w_o, qd_o, kd_o, p_o, t_o, dec_o, cd_o):
        h = pl.program_id(1)
        groups = nc // per

        def chunk_of(d, g, j):
            return (nc - 1 - (g * per + j)) if d == 1 else (g * per + j)

        def prep(g, tick=lambda: None):
            chains, qs, k, v, kk, qk, gt, beta = [], [], [], [], [], [], [], []
            for j in range(per):
                for d in range(2):
                    n = chunk_of(d, g, j)
                    rows = pl.ds(pl.multiple_of(n * c, c), c)
                    kj, vj = k_ref[rows, :], v_ref[rows, :]
                    qj = q_ref[rows, :] * (128 ** -0.5)
                    bd, gd, _, _ = _gate_cols(ba_ref[rows, :], al_ref[...], dt_ref[...], d * hh + h, 2 * hh + d * hh + h)
                    chains.append((d, n, rows))
                    for lst, val in ((qs, qj), (k, kj), (v, vj), (kk, _mm_nt(kj, kj)), (qk, _mm_nt(qj, kj)), (gt, gd), (beta, bd)):
                        lst.append(val)
            z = _gdn_prep(qs, k, v, gt, beta, kk, qk, [d == 1 for d, _, _ in chains], tick)
            for x, (d, n, rows) in enumerate(chains):
                u_o[d, rows, :] = z["u"][x]
                w_o[d, rows, :] = z["w"][x].astype(bf16)
                qd_o[d, rows, :] = z["qd"][x].astype(bf16)
                kd_o[d, rows, :] = z["kd"][x].astype(bf16)
                p_o[d, 0, n] = z["p"][x].astype(bf16)
                t_o[d, 0, n] = z["tm"][x]
                dec_o[d, 0, n] = z["dec"][x]
                cd_o[d, 0, n] = jnp.broadcast_to(z["cd"][x], (8, LANES))

        def steps(g, box):
            for j in range(per):
                sts = box[0]
                ns = [chunk_of(d, g, j) for d in range(2)]
                rows = [pl.ds(pl.multiple_of(n * c, c), c) for n in ns]
                ws = [_mm(w_o[d, rows[d], :], sts[d]) for d in range(2)]
                qs_ = [_mm(qd_o[d, rows[d], :], sts[d]) for d in range(2)]
                yield
                vn = [u_o[d, rows[d], :] - ws[d] for d in range(2)]
                box[0] = tuple(sts[d] * cd_o[d, 0, ns[d]][0:1, :] + _mm_tn(kd_o[d, rows[d], :], vn[d]) for d in range(2))
                for d in range(2):
                    o_ref[rows[d], :] = o_ref[rows[d], :] + qs_[d] + _mm(p_o[d, 0, ns[d]], vn[d])
                yield

        o_ref[...] = jnp.zeros_like(o_ref)
        prep(0)

        def fused(g, sts):
            box = [sts]
            chain = steps(g - 1, box)
            prep(g, lambda: next(chain, None))
            for _ in chain:
                pass
            return box[0]

        z0 = jnp.zeros((128, 128), f32)
        box = [lax.fori_loop(1, groups, fused, (z0, z0))]
        for _ in steps(groups - 1, box):
            pass

    blk = lambda off: pl.BlockSpec((s, 128), functools.partial(lambda b, h, off: (b, off + h), off=off))
    row = pl.BlockSpec((1, LANES), lambda b, h: (0, 0))
    tok2, mat2, cd2, shp = _gdn_specs(cfg)
    return pl.pallas_call(
        body, name="gdn_fwd", grid=(cfg.BL, hh),
        in_specs=[blk(0), blk(hh), blk(2 * hh), pl.BlockSpec((s, LANES), lambda b, h: (b, ba_blk)), row, row],
        out_specs=[pl.BlockSpec((s, 128), lambda b, h: (b, h)), tok2, tok2, tok2, tok2, mat2, mat2, mat2, cd2],
        out_shape=[jax.ShapeDtypeStruct((cfg.T, cfg.DN), f32), shp["tok32"], shp["tok16"], shp["tok16"], shp["tok16"],
                   shp["mat16"], shp["mat32"], shp["mat32"], shp["cd"]],
        compiler_params=_params(("arbitrary", "arbitrary")),
    )(qkv, qkv, qkv, proj, alog_row, dt_row)


def _lru_gates(xc, wa, wx, ba, bx, lam):
    ra = _sigmoid(_mm(xc, wa) + ba)
    ig = _sigmoid(_mm(xc, wx) + bx)
    sp = _softplus(-lam)
    la = -LRU_C * ra * sp
    a = jnp.exp(la)
    m = jnp.sqrt(_nexpm1(2.0 * la))
    return dict(ra=ra, ig=ig, sp=sp, a=a, m=m, gx=ig * xc)


def _lru_specs(cfg, outer_b):
    s = cfg.S
    lx_blk = cfg.LXO // LANES
    if outer_b:
        ix = lambda f: (lambda b, ct: f(b, ct))
    else:
        ix = lambda f: (lambda ct, b: f(b, ct))
    return dict(
        lx=pl.BlockSpec((s, LANES), ix(lambda b, ct: (b, lx_blk + ct))),
        tok=pl.BlockSpec((s, LANES), ix(lambda b, ct: (b, ct))),
        cw=pl.BlockSpec((4, LANES), ix(lambda b, ct: (0, ct))),
        row=pl.BlockSpec((1, LANES), ix(lambda b, ct: (0, ct))),
        w=pl.BlockSpec((2, 1, LANES, LANES), ix(lambda b, ct: (0, ct, 0, 0))),
        two=pl.BlockSpec((2, LANES), ix(lambda b, ct: (0, ct))),
    )


def _k_lru_fwd(cfg, proj, conv_w, conv_b, wa, wx, ba, bx, lam):
    def body(lx_ref, cw_ref, cb_ref, wa_ref, wx_ref, ba_ref, bx_ref, lam_ref, o_ref, a_s, b_s, h_s):
        xc = _conv_fwd(lx_ref[...], cw_ref, 2) + cb_ref[...]
        for d in range(2):
            z = _lru_gates(xc, wa_ref[d, 0], wx_ref[d, 0], ba_ref[d:d + 1, :], bx_ref[d:d + 1, :], lam_ref[d:d + 1, :])
            a_s[d] = z["a"]
            b_s[d] = z["m"] * z["gx"]
        _scan_refs([(a_s.at[d], b_s.at[d], h_s.at[d], d == 1) for d in range(2)], cfg.S)
        o_ref[...] = h_s[0] + h_s[1]

    sp = _lru_specs(cfg, True)
    return pl.pallas_call(
        body, name="lru_fwd", grid=(cfg.BL, cfg.LW // LANES),
        in_specs=[sp["lx"], sp["cw"], sp["row"], sp["w"], sp["w"], sp["two"], sp["two"], sp["two"]],
        out_specs=sp["tok"], out_shape=jax.ShapeDtypeStruct((cfg.T, cfg.LW), f32),
        scratch_shapes=[pltpu.VMEM((2, cfg.S, LANES), f32)] * 3,
        compiler_params=_params(("arbitrary", "arbitrary")),
    )(proj, conv_w, conv_b, wa, wx, ba, bx, lam)


def _mix_parts(cfg, o, z, lg, hs, dng, lrg):
    heads = []
    for h in range(cfg.H):
        sl = slice(h * 128, (h + 1) * 128)
        y, xh, inv = _rms_fwd(o[:, sl], dng)
        sz, dsz = _silu(z[:, sl])
        heads.append((y, xh, inv, sz, dsz))
    gl, dgl = _gelu(lg)
    y2, xh2, inv2 = _rms_fwd(gl * hs, lrg)
    return heads, (y2, xh2, inv2, gl, dgl)


def _k_mix(cfg, o, proj, hs, r, dng, lrg, wout):
    dn = cfg.DN

    def body(o_ref, z_ref, lg_ref, hs_ref, r_ref, dng_ref, lrg_ref, w_ref, mix_ref, out_ref):
        heads, lru = _mix_parts(cfg, o_ref[...], z_ref[...], lg_ref[...], hs_ref[...], dng_ref[...], lrg_ref[...])
        for h, (y, _, _, sz, _) in enumerate(heads):
            mix_ref[:, h * 128:(h + 1) * 128] = (y * sz).astype(bf16)
        mix_ref[:, dn:] = lru[0].astype(bf16)
        out_ref[...] = r_ref[...] + jnp.dot(mix_ref[...], w_ref[...], preferred_element_type=f32)

    return _row_call("mix_out", body, cfg.T, cfg.TM,
                     [(o, dn, 0), (proj, dn, cfg.ZO // dn), (proj, cfg.LW, cfg.LGO // cfg.LW), (hs, cfg.LW, 0), (r, cfg.D, 0)],
                     [dng, lrg, wout], [(cfg.D, bf16), (cfg.D, f32)], [])


def _k_ffn_a(cfg, r, g2, wg, wu):
    def body(r_ref, g_ref, wg_ref, wu_ref, h_ref, gp_ref, up_ref):
        y, _, _ = _rms_fwd(r_ref[...], g_ref[...])
        hb = y.astype(bf16)
        h_ref[...] = hb
        gp_ref[...] = _mm_nt(hb, wg_ref[...])
        up_ref[...] = _mm_nt(hb, wu_ref[...])

    return _row_call("ffn_in", body, cfg.T, cfg.TMF, [(r, cfg.D, 0)], [g2, wg, wu],
                     [(cfg.D, bf16), (cfg.FF, f32), (cfg.FF, f32)], [])


def _k_ffn_b(cfg, gp, up, conv_w, conv_b):
    s, ft = cfg.S, cfg.FT

    def body(gp_ref, up_ref, w_ref, b_ref, o_ref):
        gate = _conv_fwd(gp_ref[...], w_ref, 1) + b_ref[...]
        gl, _ = _gelu(gate)
        o_ref[...] = (gl * up_ref[...]).astype(bf16)

    tok = pl.BlockSpec((s, ft), lambda b, j: (b, j))
    return pl.pallas_call(
        body, name="ffn_act", grid=(cfg.BL, cfg.FF // ft),
        in_specs=[tok, tok, pl.BlockSpec((3, ft), lambda b, j: (0, j)), pl.BlockSpec((1, ft), lambda b, j: (0, j))],
        out_specs=tok, out_shape=jax.ShapeDtypeStruct((cfg.T, cfg.FF), bf16),
        compiler_params=_params(("arbitrary", "arbitrary")),
    )(gp, up, conv_w, conv_b)


def _k_ffn_c(cfg, act, r, wd):
    def body(a_ref, r_ref, w_ref, o_ref):
        o_ref[...] = r_ref[...] + jnp.dot(a_ref[...], w_ref[...], preferred_element_type=f32)

    return _row_call("ffn_out", body, cfg.T, cfg.TM, [(act, cfg.FF, 0), (r, cfg.D, 0)], [wd], [(cfg.D, f32)], [])[0]


def _k_ple(cfg, r, p, gp, wpg, bg, wpp):
    def body(r_ref, p_ref, g_ref, wg_ref, bg_ref, wp_ref, pn_ref, o_ref):
        x = r_ref[...]
        y, _, _ = _rms_fwd(x, g_ref[...])
        pn = y.astype(bf16)
        pn_ref[...] = pn
        pg = _sigmoid(jnp.dot(pn, wg_ref[...], preferred_element_type=f32) + bg_ref[...])
        o_ref[...] = x + pg * _mm_nt(p_ref[...], wp_ref[...])

    return _row_call("ple", body, cfg.T, cfg.TM, [(r, cfg.D, 0), (p, cfg.PD, 0)], [gp, wpg, bg, wpp],
                     [(cfg.D, bf16), (cfg.D, f32)], [])


def _k_loss(cfg, r, tgt, gf):
    d = cfg.D

    def body(r_ref, t_ref, g_ref, dr_ref, loss_ref, dg_ref):
        @pl.when(pl.program_id(0) == 0)
        def _():
            loss_ref[...] = jnp.zeros_like(loss_ref)
            dg_ref[...] = jnp.zeros_like(dg_ref)

        g = g_ref[...]
        y, xh, inv = _rms_fwd(r_ref[...], g)
        err = y - t_ref[...]
        loss_ref[...] = loss_ref[...] + (0.5 / d) * jnp.sum(err * err)
        dx, dg = _rms_bwd(err * (1.0 / d), xh, inv, g)
        dr_ref[...] = dx
        dg_ref[...] = dg_ref[...] + dg

    return _row_call("loss_head", body, cfg.T, cfg.TM, [(r, d, 0), (tgt, d, 0)], [gf], [(d, f32)],
                     [((1, LANES), f32), ((1, d), f32)])


def _zero_at_first(cond, *refs):
    @pl.when(cond)
    def _():
        for r in refs:
            r[...] = jnp.zeros_like(r)


def _b_ple(cfg, dr3, r2, p, gp, wpg, bg, wpp):
    d = cfg.D

    def body(dr_ref, r_ref, p_ref, g_ref, wg_ref, bg_ref, wp_ref, dr2_ref, dr2b_ref, dlog_ref, dpp_ref, dgp_ref, dbg_ref):
        _zero_at_first(pl.program_id(0) == 0, dgp_ref, dbg_ref)
        g = g_ref[...]
        dr = dr_ref[...]
        y, xh, inv = _rms_fwd(r_ref[...], g)
        pg = _sigmoid(jnp.dot(y.astype(bf16), wg_ref[...], preferred_element_type=f32) + bg_ref[...])
        pp = _mm_nt(p_ref[...], wp_ref[...])
        dpp_ref[...] = (dr * pg).astype(bf16)
        dlog = dr * pp * pg * (1.0 - pg)
        dlog_ref[...] = dlog.astype(bf16)
        dbg_ref[...] = dbg_ref[...] + jnp.sum(dlog, axis=0, keepdims=True)
        dx, dg = _rms_bwd(_mm_nt(dlog, wg_ref[...]), xh, inv, g)
        dgp_ref[...] = dgp_ref[...] + dg
        dr2_ref[...] = dr + dx
        dr2b_ref[...] = (dr + dx).astype(bf16)

    return _row_call("ple_bwd", body, cfg.T, cfg.TM, [(dr3, d, 0), (r2, d, 0), (p, cfg.PD, 0)], [gp, wpg, bg, wpp],
                     [(d, f32), (d, bf16), (d, bf16), (d, bf16)], [((1, d), f32), ((1, d), f32)])


def _b_ffn_bc(cfg, dr2, wd, gp, up, conv_w, conv_b):
    s, ft, d = cfg.S, cfg.FT, cfg.D

    def body(dr_ref, wd_ref, gp_ref, up_ref, w_ref, b_ref, dgp_ref, dup_ref, dwd_ref, dcw_ref, dcb_ref):
        _zero_at_first(pl.program_id(1) == 0, dwd_ref, dcw_ref, dcb_ref)
        drb = dr_ref[...]
        dact = _mm_nt(drb, wd_ref[...])
        gpre = gp_ref[...]
        up = up_ref[...]
        gl, dgl = _gelu(_conv_fwd(gpre, w_ref, 1) + b_ref[...])
        dup_ref[...] = (dact * gl).astype(bf16)
        dgate = dact * up * dgl
        dcb_ref[...] = dcb_ref[...] + jnp.sum(dgate, axis=0, keepdims=True)
        dx, dws = _conv_bwd(dgate, gpre, w_ref, 1)
        for j, dw in enumerate(dws):
            dcw_ref[j:j + 1, :] = dcw_ref[j:j + 1, :] + dw
        dgp_ref[...] = dx.astype(bf16)
        dwd_ref[...] = dwd_ref[...] + _mm_tn((gl * up).astype(bf16), drb)

    tok = pl.BlockSpec((s, ft), lambda j, b: (b, j))
    return pl.pallas_call(
        body, name="ffn_act_bwd", grid=(cfg.FF // ft, cfg.BL),
        in_specs=[pl.BlockSpec((s, d), lambda j, b: (b, 0)), pl.BlockSpec((ft, d), lambda j, b: (j, 0)), tok, tok,
                  pl.BlockSpec((3, ft), lambda j, b: (0, j)), pl.BlockSpec((1, ft), lambda j, b: (0, j))],
        out_specs=[tok, tok, pl.BlockSpec((ft, d), lambda j, b: (j, 0)), pl.BlockSpec((3, ft), lambda j, b: (0, j)),
                   pl.BlockSpec((1, ft), lambda j, b: (0, j))],
        out_shape=[jax.ShapeDtypeStruct((cfg.T, cfg.FF), bf16), jax.ShapeDtypeStruct((cfg.T, cfg.FF), bf16),
                   jax.ShapeDtypeStruct((cfg.FF, d), f32), jax.ShapeDtypeStruct((3, cfg.FF), f32),
                   jax.ShapeDtypeStruct((1, cfg.FF), f32)],
        compiler_params=_params(("arbitrary", "arbitrary")),
    )(dr2, wd, gp, up, conv_w, conv_b)


def _b_ffn_a(cfg, dgp, dup, dr2, r_mid, g2, wg, wu):
    d = cfg.D

    def body(dgp_ref, dup_ref, dr_ref, r_ref, g_ref, wg_ref, wu_ref, o_ref, dg_ref):
        _zero_at_first(pl.program_id(0) == 0, dg_ref)
        g = g_ref[...]
        _, xh, inv = _rms_fwd(r_ref[...], g)
        dh = _mm(dgp_ref[...], wg_ref[...]) + _mm(dup_ref[...], wu_ref[...])
        dx, dg = _rms_bwd(dh, xh, inv, g)
        dg_ref[...] = dg_ref[...] + dg
        o_ref[...] = dr_ref[...] + dx

    return _row_call("ffn_in_bwd", body, cfg.T, cfg.TMF, [(dgp, cfg.FF, 0), (dup, cfg.FF, 0), (dr2, d, 0), (r_mid, d, 0)],
                     [g2, wg, wu], [(d, f32)], [((1, d), f32)])


def _mm_tn_call(cfg, name, x, dy, tn):
    t, k = x.shape
    n = dy.shape[1]
    tm = cfg.TMW
    last = t // tm - 1

    def body(x_ref, dy_ref, o_ref, acc):
        _zero_at_first(pl.program_id(1) == 0, acc)
        acc[...] = acc[...] + _mm_tn(x_ref[...], dy_ref[...])

        @pl.when(pl.program_id(1) == last)
        def _():
            o_ref[...] = acc[...].astype(bf16)

    return pl.pallas_call(
        body, name=name, grid=(n // tn, t // tm),
        in_specs=[pl.BlockSpec((tm, k), lambda j, i: (i, 0)), pl.BlockSpec((tm, tn), lambda j, i: (i, j))],
        out_specs=pl.BlockSpec((k, tn), lambda j, i: (0, j)),
        out_shape=jax.ShapeDtypeStruct((k, n), bf16), scratch_shapes=[pltpu.VMEM((k, tn), f32)],
        compiler_params=_params(("arbitrary", "arbitrary")),
    )(x, dy)


def _b_mix(cfg, dr, o, proj, hs, dng, lrg, wout):
    dn, lw, d = cfg.DN, cfg.LW, cfg.D

    def body(dr_ref, o_ref, z_ref, lg_ref, hs_ref, dng_ref, lrg_ref, w_ref, do_ref, dz_ref, dlg_ref, dhs_ref, ddn_ref, dlr_ref):
        _zero_at_first(pl.program_id(0) == 0, ddn_ref, dlr_ref)
        dng, lrg = dng_ref[...], lrg_ref[...]
        hs = hs_ref[...]
        o, z = o_ref[...], z_ref[...]
        heads, lru = _mix_parts(cfg, o, z, lg_ref[...], hs, dng, lrg)
        drb = dr_ref[...].astype(bf16)
        dmix_dn = _mm_nt(drb, w_ref[0:dn, :])
        dmix_lr = _mm_nt(drb, w_ref[dn:, :])
        dgn = jnp.zeros_like(dng)
        for h, (y, xh, inv, sz, dsz) in enumerate(heads):
            sl = slice(h * 128, (h + 1) * 128)
            dm = dmix_dn[:, sl]
            dz_ref[:, sl] = (dm * y * dsz).astype(bf16)
            dx, dg = _rms_bwd(dm * sz, xh, inv, dng)
            do_ref[:, sl] = dx.astype(bf16)
            dgn = dgn + dg
        ddn_ref[...] = ddn_ref[...] + dgn
        _, xh2, inv2, gl, dgl = lru
        dx2, dg2 = _rms_bwd(dmix_lr, xh2, inv2, lrg)
        dlr_ref[...] = dlr_ref[...] + dg2
        dlg_ref[...] = (dx2 * hs * dgl).astype(bf16)
        dhs_ref[...] = dx2 * gl

    return _row_call("mix_bwd", body, cfg.T, cfg.TM,
                     [(dr, d, 0), (o, dn, 0), (proj, dn, cfg.ZO // dn), (proj, lw, cfg.LGO // lw), (hs, lw, 0)],
                     [dng, lrg, wout], [(dn, bf16), (dn, bf16), (lw, bf16), (lw, f32)], [((1, 128), f32), ((1, lw), f32)])


def _b_lru(cfg, proj, dhs, conv_w, conv_b, wa, wx, ba, bx, lam):
    def body(lx_ref, dh_ref, cw_ref, cb_ref, wa_ref, wx_ref, ba_ref, bx_ref, lam_ref,
             dlx_ref, dcw_ref, dcb_ref, dwa_ref, dwx_ref, dba_ref, dbx_ref, dlam_ref, a_s, b_s, h_s, an_s, l_s):
        _zero_at_first(pl.program_id(1) == 0, dcw_ref, dcb_ref, dwa_ref, dwx_ref, dba_ref, dbx_ref, dlam_ref)
        lx = lx_ref[...]
        xc = _conv_fwd(lx, cw_ref, 2) + cb_ref[...]
        dxc = jnp.zeros_like(xc)
        gates = []
        for d in range(2):
            z = _lru_gates(xc, wa_ref[d, 0], wx_ref[d, 0], ba_ref[d:d + 1, :], bx_ref[d:d + 1, :], lam_ref[d:d + 1, :])
            a_s[d] = z["a"]
            b_s[d] = z["m"] * z["gx"]
            an_s[d] = _shift(z["a"], -1 if d == 1 else 1, 0.0)
            gates.append(z)
        _scan_refs([(a_s.at[d], b_s.at[d], h_s.at[d], d == 1) for d in range(2)]
                   + [(an_s.at[d], dh_ref, l_s.at[d], d == 0) for d in range(2)], cfg.S)
        for d in range(2):
            rev = d == 1
            lam = lam_ref[d:d + 1, :]
            z = gates[d]
            a, m, ra, ig, sp = z["a"], z["m"], z["ra"], z["ig"], z["sp"]
            lmb = l_s[d]
            h_prev = _shift(h_s[d], 1 if rev else -1, 0.0)
            da = lmb * h_prev
            dm = lmb * z["gx"]
            dgx = lmb * m
            dla = da * a - dm * (a * a) / jnp.maximum(m, 1e-30)
            dra = dla * (-LRU_C) * sp
            dsp = jnp.sum(dla * (-LRU_C) * ra, axis=0, keepdims=True)
            dlam_ref[d:d + 1, :] = dlam_ref[d:d + 1, :] - dsp * _sigmoid(-lam)
            dpa = dra * ra * (1.0 - ra)
            dpx = dgx * xc * ig * (1.0 - ig)
            dba_ref[d:d + 1, :] = dba_ref[d:d + 1, :] + jnp.sum(dpa, axis=0, keepdims=True)
            dbx_ref[d:d + 1, :] = dbx_ref[d:d + 1, :] + jnp.sum(dpx, axis=0, keepdims=True)
            dwa_ref[d, 0] = dwa_ref[d, 0] + _mm_tn(xc, dpa)
            dwx_ref[d, 0] = dwx_ref[d, 0] + _mm_tn(xc, dpx)
            dxc = dxc + dgx * ig + _mm_nt(dpa, wa_ref[d, 0]) + _mm_nt(dpx, wx_ref[d, 0])
        dcb_ref[...] = dcb_ref[...] + jnp.sum(dxc, axis=0, keepdims=True)
        dx, dws = _conv_bwd(dxc, lx, cw_ref, 2)
        for j, dw in enumerate(dws):
            dcw_ref[j:j + 1, :] = dcw_ref[j:j + 1, :] + dw
        dlx_ref[...] = dx.astype(bf16)

    sp = _lru_specs(cfg, False)
    nct = cfg.LW // LANES
    return pl.pallas_call(
        body, name="lru_bwd", grid=(nct, cfg.BL),
        in_specs=[sp["lx"], sp["tok"], sp["cw"], sp["row"], sp["w"], sp["w"], sp["two"], sp["two"], sp["two"]],
        out_specs=[sp["tok"], sp["cw"], sp["row"], sp["w"], sp["w"], sp["two"], sp["two"], sp["two"]],
        out_shape=[jax.ShapeDtypeStruct((cfg.T, cfg.LW), bf16), jax.ShapeDtypeStruct((4, cfg.LW), f32),
                   jax.ShapeDtypeStruct((1, cfg.LW), f32), jax.ShapeDtypeStruct((2, nct, LANES, LANES), f32),
                   jax.ShapeDtypeStruct((2, nct, LANES, LANES), f32), jax.ShapeDtypeStruct((2, cfg.LW), f32),
                   jax.ShapeDtypeStruct((2, cfg.LW), f32), jax.ShapeDtypeStruct((2, cfg.LW), f32)],
        scratch_shapes=[pltpu.VMEM((2, cfg.S, LANES), f32)] * 5,
        compiler_params=_params(("arbitrary", "arbitrary")),
    )(proj, dhs, conv_w, conv_b, wa, wx, ba, bx, lam)


def _b_gdn(cfg, qkv, proj, do, alog_row, dt_row, saved):
    s, c, nc, hh = cfg.S, cfg.C, cfg.NC, cfg.H
    ba_blk = cfg.BAO // LANES
    scale = 128 ** -0.5
    per = 4
    assert nc % per == 0

    def body(q_ref, k_ref, v_ref, ba_ref, do_ref, al_ref, dt_ref, u_i, w_i, qd_i, kd_i, p_i, t_i, dec_i, cd_i,
             dqkv_ref, dba_ref, dal_ref, ddt_ref, vn_s, dvn_s, st_s, dst_s):
        h = pl.program_id(1)
        _zero_at_first((pl.program_id(0) == 0) & (h == 0), dal_ref, ddt_ref)
        _zero_at_first(h == 0, dba_ref)
        lane = lax.broadcasted_iota(jnp.int32, (c, LANES), 1)
        lane1 = lax.broadcasted_iota(jnp.int32, (1, LANES), 1)
        dirs = (0, 1)
        lane_b = [d * hh + h for d in dirs]
        lane_a = [2 * hh + d * hh + h for d in dirs]

        def seq(i, box):
            sts, dst = box[0]
            live = i < nc
            ii = jnp.where(live, i, 0)
            nf = [ii, nc - 1 - ii]
            nb = [nc - 1 - ii, ii]
            sf = [jnp.where(live, n, nc) for n in nf]
            sb = [jnp.where(live, n, nc) for n in nb]
            rf = [pl.ds(pl.multiple_of(n * c, c), c) for n in nf]
            rb = [pl.ds(pl.multiple_of(n * c, c), c) for n in nb]
            for d in dirs:
                st_s[d, sf[d]] = sts[d]
                dst_s[d, sb[d]] = dst[d]
            dob = [do_ref[rb[d], :] for d in dirs]
            vn = [u_i[d, rf[d], :] - _mm(w_i[d, rf[d], :], sts[d]) for d in dirs]
            dvn = [_mm_tn(p_i[d, 0, nb[d]], dob[d]) + _mm(kd_i[d, rb[d], :], dst[d]) for d in dirs]
            yield
            for d in dirs:
                vn_s[d, pl.ds(pl.multiple_of(sf[d] * c, c), c), :] = vn[d]
                dvn_s[d, pl.ds(pl.multiple_of(sb[d] * c, c), c), :] = dvn[d]
            sts = tuple(sts[d] * cd_i[d, 0, nf[d]][0:1, :] + _mm_tn(kd_i[d, rf[d], :], vn[d]) for d in dirs)
            dst = tuple(_mm_tn(qd_i[d, rb[d], :], dob[d]) + cd_i[d, 0, nb[d]][0:1, 0:1] * dst[d]
                        - _mm_tn(w_i[d, rb[d], :], dvn[d]) for d in dirs)
            box[0] = (sts, dst)
            yield

        def bpar(chunks, tick=lambda: None):
            ch = [(d, n) for n in chunks for d in dirs]
            rows = [pl.ds(pl.multiple_of(n * c, c), c) for _, n in ch]
            masks = [_tri_masks(c, d == 1) for d, _ in ch]
            ld = lambda ref: [ref[d, r, :] for (d, _), r in zip(ch, rows)]
            ldm = lambda ref: [ref[d, 0, n] for d, n in ch]
            q, k, v, dob = ([ref[r, :] for r in rows] for ref in (q_ref, k_ref, v_ref, do_ref))
            gates = [_gate_cols(ba_ref[r, :], al_ref[...], dt_ref[...], lane_b[d], lane_a[d]) for (d, _), r in zip(ch, rows)]
            beta, g, aexp, xa = ([gt[x] for gt in gates] for x in range(4))
            st, dst = ([ref[d, n] for d, n in ch] for ref in (st_s, dst_s))
            vn, dvn, u, w = ld(vn_s), ld(dvn_s), ld(u_i), ld(w_i)
            p = [x.astype(f32) for x in ldm(p_i)]
            tm, dec = ldm(t_i), ldm(dec_i)
            cd = [cd_i[d, 0, n][0:1, 0:1] for d, n in ch]
            dp = _map(lambda m, a, b: jnp.where(m[0], _mm_nt(a, b), 0.0), masks, dob, vn)
            dqd = _map(_mm_nt, dob, st)
            dkd = _map(_mm_nt, vn, dst)
            dw = _map(lambda a, b: -_mm_nt(a, b), dvn, st)
            dcd = _map(lambda a, b: jnp.sum(jnp.sum(a * b, axis=1, keepdims=True), axis=0, keepdims=True), st, dst)
            tick()
            gcum = _map(lambda m, x: _cum_mm(m[0], jnp.broadcast_to(x, (c, c)))[:, 0:1], masks, g)
            glast = [jnp.sum(x, axis=0, keepdims=True) for x in g]
            e = [jnp.exp(x) for x in gcum]
            el = _map(lambda a, b: jnp.exp(a - b), glast, gcum)
            qs = [x * scale for x in q]
            kb = _map(lambda a, b: a * b, k, beta)
            a = _map(lambda m, b, ki, dc: jnp.where(m[1], b * _mm_nt(ki, ki) * dc, 0.0), masks, beta, k, dec)
            tick()
            dvb = _map(_hmm3_tn, tm, dvn)
            dkbe = _map(_hmm3_tn, tm, dw)
            tick()
            da = _map(lambda m, x, ui, y, wi: -jnp.where(m[1], _mm_nt(x, ui) + _mm_nt(y, wi), 0.0), masks, dvb, u, dkbe, w)
            g1 = _map(lambda x, y: x * y, da, dec)
            g2 = _map(lambda x, y: x * y, dp, dec)
            dkb = _map(lambda x, ki, y, ei: _mm(x, ki) + y * ei, g1, k, dkbe, e)
            tick()
            dk = _map(lambda x, kbi, y, qi, z, b, t, l: _mm_tn(x, kbi) + _mm_tn(y, qi) + z * b + t * l,
                      g1, kb, g2, qs, dkb, beta, dkd, el)
            dqs = _map(lambda y, ki, x, ei: _mm(y, ki) + x * ei, g2, k, dqd, e)
            tick()
            ddd = _map(lambda x, ai, y, pi: x * ai + y * pi, da, a, dp, p)
            ones = jnp.ones((c, LANES), f32)
            dgcum = _map(lambda x: jnp.sum(x, axis=1, keepdims=True) - _hmm3_tn(x, ones)[:, 0:1], ddd)
            for x, (d, n) in enumerate(ch):
                dbeta = jnp.sum(dvb[x] * v[x], axis=1, keepdims=True) + jnp.sum(dkb[x] * k[x], axis=1, keepdims=True)
                de = jnp.sum(dkbe[x] * kb[x], axis=1, keepdims=True) + jnp.sum(dqd[x] * qs[x], axis=1, keepdims=True)
                del_ = jnp.sum(dkd[x] * k[x], axis=1, keepdims=True)
                dgc = dgcum[x] + de * e[x] - del_ * el[x]
                dglast = jnp.sum(del_ * el[x], axis=0, keepdims=True) + dcd[x] * cd[x]
                dg = _cum_mm(masks[x][2], jnp.broadcast_to(dgc, (c, LANES)))[:, 0:1] + dglast
                r = rows[x]
                if d == 0:
                    dqkv_ref[0, r, :] = dqs[x] * scale
                    dqkv_ref[1, r, :] = dk[x]
                    dqkv_ref[2, r, :] = dvb[x] * beta[x]
                else:
                    dqkv_ref[0, r, :] = dqkv_ref[0, r, :] + dqs[x] * scale
                    dqkv_ref[1, r, :] = dqkv_ref[1, r, :] + dk[x]
                    dqkv_ref[2, r, :] = dqkv_ref[2, r, :] + dvb[x] * beta[x]
                dlb = dbeta * beta[x] * (1.0 - beta[x])
                dalpha = -dg * aexp[x] * _sigmoid(xa[x])
                dba_ref[r, :] = dba_ref[r, :] + jnp.where(lane == lane_b[d], dlb, 0.0) + jnp.where(lane == lane_a[d], dalpha, 0.0)
                dal_ref[...] = dal_ref[...] + jnp.where(lane1 == lane_a[d], jnp.sum(dg * g[x], axis=0, keepdims=True), 0.0)
                ddt_ref[...] = ddt_ref[...] + jnp.where(lane1 == lane_a[d], jnp.sum(dalpha, axis=0, keepdims=True), 0.0)

        half, groups = nc // 2, nc // per

        def group(k):
            return [half - 2 * k - 2, half - 2 * k - 1, half + 2 * k, half + 2 * k + 1]

        def run(gen):
            for _ in gen:
                pass

        def plain(i, carry):
            box = [carry]
            run(seq(i, box))
            return box[0]

        def woven(k, carry):
            box = [carry]

            def two_steps():
                yield from seq(half + 2 + 2 * k, box)
                yield from seq(half + 3 + 2 * k, box)

            chain = two_steps()
            bpar(group(k), lambda: next(chain, None))
            run(chain)
            return box[0]

        z0 = jnp.zeros((128, 128), f32)
        lax.fori_loop(0, groups, woven, lax.fori_loop(0, half + 2, plain, ((z0, z0), (z0, z0))))

    blk = lambda off: pl.BlockSpec((s, 128), functools.partial(lambda b, h, off: (b, off + h), off=off))
    row = pl.BlockSpec((1, LANES), lambda b, h: (0, 0))
    tok2, mat2, cd2, _ = _gdn_specs(cfg)
    return pl.pallas_call(
        body, name="gdn_bwd", grid=(cfg.BL, hh),
        in_specs=[blk(0), blk(hh), blk(2 * hh), pl.BlockSpec((s, LANES), lambda b, h: (b, ba_blk)),
                  pl.BlockSpec((s, 128), lambda b, h: (b, h)), row, row, tok2, tok2, tok2, tok2, mat2, mat2, mat2, cd2],
        out_specs=[pl.BlockSpec((3, s, 128), lambda b, h: (0, b, h)), pl.BlockSpec((s, LANES), lambda b, h: (b, 0)), row, row],
        out_shape=[jax.ShapeDtypeStruct((3, cfg.T, cfg.DN), f32), jax.ShapeDtypeStruct((cfg.T, LANES), f32),
                   jax.ShapeDtypeStruct((1, LANES), f32), jax.ShapeDtypeStruct((1, LANES), f32)],
        scratch_shapes=[pltpu.VMEM((2, s + c, 128), f32)] * 2 + [pltpu.VMEM((2, nc + 1, 128, 128), f32)] * 2,
        compiler_params=_params(("arbitrary", "arbitrary")),
    )(qkv, qkv, qkv, proj, do, alog_row, dt_row, *saved)


def _b_prep(cfg, proj, dqkv, conv_w):
    dn, s = cfg.DN, cfg.S

    def body(x_ref, dy_ref, w_ref, dx_ref, dw_ref):
        _zero_at_first(pl.program_id(1) == 0, dw_ref)
        sec = pl.program_id(0)
        x = x_ref[...]
        c = _conv_fwd(x, w_ref, 2)
        y, dsilu = _silu(c)
        dy = dy_ref[0]
        parts = []
        for h in range(cfg.H):
            sl = slice(h * 128, (h + 1) * 128)
            yh, dyh = y[:, sl], dy[:, sl]
            inv = lax.rsqrt(jnp.sum(yh * yh, axis=1, keepdims=True) + EPS)
            dn_h = inv * dyh - yh * (inv * inv * inv) * jnp.sum(dyh * yh, axis=1, keepdims=True)
            parts.append(jnp.where(sec < 2, dn_h, dyh))
        ds = jnp.concatenate(parts, axis=1) if len(parts) > 1 else parts[0]
        dx, dws = _conv_bwd(ds * dsilu, x, w_ref, 2)
        for j, dw in enumerate(dws):
            dw_ref[j:j + 1, :] = dw_ref[j:j + 1, :] + dw
        dx_ref[...] = dx.astype(bf16)

    return pl.pallas_call(
        body, name="dn_prep_bwd", grid=(3, cfg.BL),
        in_specs=[pl.BlockSpec((s, dn), lambda j, b: (b, j)), pl.BlockSpec((1, s, dn), lambda j, b: (j, b, 0)),
                  pl.BlockSpec((4, dn), lambda j, b: (0, j))],
        out_specs=[pl.BlockSpec((s, dn), lambda j, b: (b, j)), pl.BlockSpec((4, dn), lambda j, b: (0, j))],
        out_shape=[jax.ShapeDtypeStruct((cfg.T, 3 * dn), bf16), jax.ShapeDtypeStruct((4, 3 * dn), f32)],
        compiler_params=_params(("arbitrary", "arbitrary")),
    )(proj, dqkv, conv_w)


def _b_in(cfg, dqkv_pre, dz, dlx, dlg, dba, dr_mid, r_in, g1, wcat):
    d, dn, lw = cfg.D, cfg.DN, cfg.LW

    def body(dq_ref, dz_ref, dlx_ref, dlg_ref, dba_ref, dr_ref, r_ref, g_ref, w_ref, o_ref, dp_ref, dg_ref):
        _zero_at_first(pl.program_id(0) == 0, dg_ref)
        dp_ref[:, 0:cfg.ZO] = dq_ref[...].astype(bf16)
        dp_ref[:, cfg.ZO:cfg.LXO] = dz_ref[...].astype(bf16)
        dp_ref[:, cfg.LXO:cfg.LGO] = dlx_ref[...].astype(bf16)
        dp_ref[:, cfg.LGO:cfg.BAO] = dlg_ref[...].astype(bf16)
        dp_ref[:, cfg.BAO:] = dba_ref[...].astype(bf16)
        g = g_ref[...]
        _, xh, inv = _rms_fwd(r_ref[...], g)
        dx, dg = _rms_bwd(_mm(dp_ref[...], w_ref[...]), xh, inv, g)
        dg_ref[...] = dg_ref[...] + dg
        o_ref[...] = dr_ref[...] + dx

    return _row_call("in_proj_bwd", body, cfg.T, cfg.TM,
                     [(dqkv_pre, 3 * dn, 0), (dz, dn, 0), (dlx, lw, 0), (dlg, lw, 0), (dba, LANES, 0), (dr_mid, d, 0), (r_in, d, 0)],
                     [g1, wcat], [(d, f32), (cfg.PC, bf16)], [((1, d), f32)])


def _adam_call(name, w, g, m, v, tr):
    rows, cols = w.shape
    bc1 = 1.0 - ADAM_B1 ** ADAM_STEP
    bc2 = 1.0 - ADAM_B2 ** ADAM_STEP

    def body(w_ref, g_ref, m_ref, v_ref, d_ref, nm_ref, nv_ref):
        g = g_ref[...]
        m = ADAM_B1 * m_ref[...] + (1.0 - ADAM_B1) * g
        v = ADAM_B2 * v_ref[...] + (1.0 - ADAM_B2) * (g * g)
        nm_ref[...] = m
        nv_ref[...] = v
        d_ref[...] = -ADAM_LR * ((m / bc1) / (jnp.sqrt(v / bc2) + ADAM_EPS) + ADAM_WD * w_ref[...])

    spec = pl.BlockSpec((tr, cols), lambda i: (i, 0))
    return pl.pallas_call(
        body, name=name, grid=(rows // tr,), in_specs=[spec] * 4, out_specs=[spec] * 3,
        out_shape=[jax.ShapeDtypeStruct((rows, cols), f32)] * 3, compiler_params=_params(("arbitrary",)),
    )(w, g, m, v)


def _sum8_call(name, x, tr):
    _, rows, cols = x.shape

    def body(x_ref, o_ref):
        acc = x_ref[0].astype(f32)
        for j in range(1, N_DEV):
            acc = acc + x_ref[j].astype(f32)
        o_ref[...] = acc

    return pl.pallas_call(
        body, name=name, grid=(rows // tr,), in_specs=[pl.BlockSpec((N_DEV, tr, cols), lambda i: (0, i, 0))],
        out_specs=pl.BlockSpec((tr, cols), lambda i: (i, 0)), out_shape=jax.ShapeDtypeStruct((rows, cols), f32),
        compiler_params=_params(("arbitrary",)),
    )(x)


def _all_gather(name, shards):
    na = len(shards)

    def body(*refs):
        xs, outs = refs[:na], refs[na:2 * na]
        send_sems, recv_sems, local_sems = refs[2 * na:]
        x, y, c = lax.axis_index("x"), lax.axis_index("y"), lax.axis_index("c")
        me, sibling = (x, y, c), (x, y, 1 - c)
        chips = [(1 - x, y), (x, 1 - y), (1 - x, 1 - y)]

        def copy(a, k, block, to, src=None):
            px, py, pc = block
            slot = outs[a].at[4 * px + 2 * py + pc]
            return pltpu.make_async_remote_copy(
                src_ref=slot if src is None else src, dst_ref=slot,
                send_sem=send_sems.at[a, k], recv_sem=recv_sems.at[a, k], device_id=to, device_id_type=MESH)

        mine = [pltpu.make_async_copy(xs[a], outs[a].at[4 * x + 2 * y + c], local_sems.at[a]) for a in range(na)]
        for cp in mine:
            cp.start()
        first = []
        for a in range(na):
            first.append(copy(a, 0, me, sibling, src=xs[a]))
            first += [copy(a, 1 + j, me, (*chip, c), src=xs[a]) for j, chip in enumerate(chips)]
        for cp in first:
            cp.start()
        passed = []
        for j, chip in enumerate(chips):
            for a in range(na):
                copy(a, 1 + j, (*chip, c), me).wait_recv()
                cp = copy(a, 4 + j, (*chip, c), sibling)
                cp.start()
                passed.append(cp)
        for a in range(na):
            copy(a, 0, sibling, me).wait_recv()
            for j, chip in enumerate(chips):
                copy(a, 4 + j, (*chip, 1 - c), me).wait_recv()
        for cp in first + passed:
            cp.wait_send()
        for cp in mine:
            cp.wait()

    hbm = pl.BlockSpec(memory_space=pltpu.HBM)
    return pl.pallas_call(
        body, name=name, out_shape=[jax.ShapeDtypeStruct((N_DEV,) + s.shape, s.dtype) for s in shards],
        in_specs=[hbm] * na, out_specs=[hbm] * na,
        scratch_shapes=[pltpu.SemaphoreType.DMA((na, 7)), pltpu.SemaphoreType.DMA((na, 7)), pltpu.SemaphoreType.DMA((na,))],
    )(*shards)


def _peer_list():
    x, y, c = lax.axis_index("x"), lax.axis_index("y"), lax.axis_index("c")
    return 4 * x + 2 * y + c, [(x ^ (k >> 2), y ^ ((k >> 1) & 1), c ^ (k & 1)) for k in range(1, N_DEV)]


_HBM = pl.BlockSpec(memory_space=pltpu.HBM)
_SEM = pl.BlockSpec(memory_space=pltpu.SEMAPHORE)
_EFFECT = pltpu.SideEffectType.DATAFLOW_SIDE_EFFECTING


def _exchange_copies(xs, lands, send_sem, recv_sem, gather):
    me, peers = _peer_list()
    return [pltpu.make_async_remote_copy(
        src_ref=xs[a] if gather else xs[a].at[4 * px + 2 * py + pc], dst_ref=lands[a].at[me],
        send_sem=send_sem.at[7 * a + k], recv_sem=recv_sem.at[7 * a + k], device_id=(px, py, pc), device_id_type=MESH)
        for k, (px, py, pc) in enumerate(peers) for a in range(len(xs))]


def _exchange_start(name, blocks, gather=False):
    na = len(blocks)

    def body(*refs):
        for cp in _exchange_copies(refs[:na], refs[na:2 * na], refs[2 * na], refs[2 * na + 1], gather):
            cp.start()
        refs[-1][...] = jnp.zeros_like(refs[-1])

    lands = [jax.ShapeDtypeStruct((N_DEV,) + b.shape if gather else b.shape, b.dtype) for b in blocks]
    hbm = [pltpu.HBM(b.shape, b.dtype) for b in blocks] + [pltpu.HBM(b.shape, b.dtype) for b in lands]
    send_sem, recv_sem, *thru, token = pl.pallas_call(
        body, name=name,
        out_shape=(pltpu.SemaphoreType.DMA((7 * na,)), pltpu.SemaphoreType.DMA((7 * na,)), *hbm,
                   jax.ShapeDtypeStruct((8, LANES), f32)),
        in_specs=[_HBM] * (2 * na), out_specs=(_SEM, _SEM, *([_HBM] * (2 * na)), pl.BlockSpec(memory_space=pltpu.VMEM)),
        input_output_aliases={i: 2 + i for i in range(2 * na)},
        compiler_params=pltpu.CompilerParams(has_side_effects=_EFFECT),
    )(*[pltpu.with_memory_space_constraint(b, pltpu.HBM) for b in blocks],
      *[pltpu.with_memory_space_constraint(lax.empty(b.shape, b.dtype), pltpu.HBM) for b in lands])
    return send_sem, recv_sem, thru, token


def _exchange_wait(name, send_sem, recv_sem, thru, after, gather=False):
    na = len(thru) // 2

    def body(*refs):
        for cp in _exchange_copies(refs[:na], refs[na:2 * na], refs[2 * na], refs[2 * na + 1], gather):
            cp.wait_send()
            cp.wait_recv()

    return pl.pallas_call(
        body, name=name, out_shape=tuple(pltpu.HBM(t.shape, t.dtype) for t in thru),
        in_specs=[_HBM] * (2 * na) + [_SEM, _SEM, pl.BlockSpec(memory_space=pl.ANY)], out_specs=tuple([_HBM] * (2 * na)),
        input_output_aliases={i: i for i in range(2 * na)},
        compiler_params=pltpu.CompilerParams(has_side_effects=_EFFECT),
    )(*thru, send_sem, recv_sem, after)[na:]


def _layer_fwd(cfg, w, r, p):
    h1, proj = _k_in(cfg, r, w["norm1_g"], w["wcat_t"])
    qkv = _k_prep(cfg, proj, w["dn_conv_w"])
    o, *gdn_saved = _k_gdn_fwd(cfg, qkv, proj, w["alog_row"], w["dt_row"])
    hs = _k_lru_fwd(cfg, proj, w["lru_conv_w"], w["lru_conv_b"], w["wa"], w["wx"], w["lru_ba"], w["lru_bx"], w["lru_lambda"])
    if "late" in w:
        w.update(w.pop("late")(hs))
    mix, r_mid = _k_mix(cfg, o, proj, hs, r, w["dn_norm_g"], w["lru_norm_g"], w["w_out"])
    h2, gp, up = _k_ffn_a(cfg, r_mid, w["norm2_g"], w["ffn_wg_t"], w["ffn_wu_t"])
    act = _k_ffn_b(cfg, gp, up, w["ffn_conv_w"], w["ffn_conv_b"])
    r2 = _k_ffn_c(cfg, act, r_mid, w["ffn_wd"])
    pn, r3 = _k_ple(cfg, r2, p, w["ple_norm_g"], w["ple_wg"], w["ple_bg"], w["ple_wp_t"])
    saved = dict(r=r, h1=h1, proj=proj, qkv=qkv, o=o, gdn=gdn_saved, hs=hs, mix=mix, r_mid=r_mid, h2=h2, gp=gp, up=up,
                 r2=r2, pn=pn, p=p)
    return r3, saved


def _layer_bwd(cfg, w, sv, dr3, early=None):
    g = {}
    dt = min(512, cfg.D)
    dr2, dr2b, dlog, dpp, g["ple_norm_g"], g["ple_bg"] = _b_ple(cfg, dr3, sv["r2"], sv["p"], w["ple_norm_g"], w["ple_wg"], w["ple_bg"], w["ple_wp_t"])
    g["ple_wg"] = _mm_tn_call(cfg, "d_ple_wg", sv["pn"], dlog, dt)
    g["ple_wp_t"] = _mm_tn_call(cfg, "d_ple_wp", dpp, sv["p"], cfg.PD)
    dgp, dup, g["ffn_wd"], g["ffn_conv_w"], g["ffn_conv_b"] = _b_ffn_bc(cfg, dr2b, w["ffn_wd"], sv["gp"], sv["up"], w["ffn_conv_w"], w["ffn_conv_b"])
    dr_mid, g["norm2_g"] = _b_ffn_a(cfg, dgp, dup, dr2, sv["r_mid"], w["norm2_g"], w["ffn_wg_t"], w["ffn_wu_t"])
    g["ffn_wg_t"] = _mm_tn_call(cfg, "d_ffn_wg", dgp, sv["h2"], dt)
    g["ffn_wu_t"] = _mm_tn_call(cfg, "d_ffn_wu", dup, sv["h2"], dt)
    if early is not None:
        w = dict(w, dn_norm_g=w["dn_norm_g"] + early(EARLY, g)[0, 0])
    do, dz, dlg, dhs, g["dn_norm_g"], g["lru_norm_g"] = _b_mix(cfg, dr_mid, sv["o"], sv["proj"], sv["hs"], w["dn_norm_g"], w["lru_norm_g"], w["w_out"])
    g["w_out"] = _mm_tn_call(cfg, "d_w_out", sv["mix"], dr_mid, dt)
    if early is not None:
        w = dict(w, lru_conv_b=w["lru_conv_b"] + early(("w_out",), g)[0, 0])
    dlx, g["lru_conv_w"], g["lru_conv_b"], g["wa"], g["wx"], g["lru_ba"], g["lru_bx"], g["lru_lambda"] = _b_lru(
        cfg, sv["proj"], dhs, w["lru_conv_w"], w["lru_conv_b"], w["wa"], w["wx"], w["lru_ba"], w["lru_bx"], w["lru_lambda"])
    dqkv, dba, g["alog_row"], g["dt_row"] = _b_gdn(cfg, sv["qkv"], sv["proj"], do, w["alog_row"], w["dt_row"], sv["gdn"])
    dqkv_pre, g["dn_conv_w"] = _b_prep(cfg, sv["proj"], dqkv, w["dn_conv_w"])
    dr, dproj, g["norm1_g"] = _b_in(cfg, dqkv_pre, dz, dlx, dlg, dba, dr_mid, sv["r"], w["norm1_g"], w["wcat_t"])
    g["wcat_t"] = _mm_tn_call(cfg, "d_w_in", dproj, sv["h1"], dt)
    return dr, g


BIG = ("w_in", "w_out", "ffn_wg", "ffn_wu", "ffn_wd", "ple_wg", "ple_wp")
BIG_T = {"w_in": True, "w_out": False, "ffn_wg": True, "ffn_wu": True, "ffn_wd": False, "ple_wg": False, "ple_wp": True}
BIG_OPERAND = {"w_in": "wcat_t", "w_out": "w_out", "ffn_wg": "ffn_wg_t", "ffn_wu": "ffn_wu_t", "ffn_wd": "ffn_wd",
               "ple_wg": "ple_wg", "ple_wp": "ple_wp_t"}
EARLY = ("ffn_wg", "ffn_wu", "ffn_wd", "ple_wg", "ple_wp")
SMALL_SHARDED = ("dn_conv_w", "lru_conv_w", "lru_ba", "lru_bx", "lru_lambda", "ffn_conv_w")
SMALL_REPL = ("norm1_g", "dn_a_log", "dn_dt_bias", "dn_norm_g", "lru_conv_b", "lru_wa", "lru_wx", "lru_norm_g", "norm2_g",
              "ffn_conv_b", "ple_norm_g", "ple_bg", "final_g")
WEIGHTS = ("norm1_g", "w_in", "dn_conv_w", "dn_a_log", "dn_dt_bias", "dn_norm_g", "lru_conv_w", "lru_conv_b", "lru_wa",
           "lru_ba", "lru_wx", "lru_bx", "lru_lambda", "lru_norm_g", "w_out", "norm2_g", "ffn_wg", "ffn_wu", "ffn_conv_w",
           "ffn_conv_b", "ffn_wd", "ple_norm_g", "ple_wg", "ple_bg", "ple_wp", "final_g")


def _pad_rows(flat, cols, mult):
    n = flat.shape[0]
    rows = -(-n // cols)
    rows = -(-rows // mult) * mult
    return jnp.pad(flat, (0, rows * cols - n)).reshape(rows, cols)


def _pack(arrs, cols, mult, dtype):
    return _pad_rows(jnp.concatenate([a.reshape(-1).astype(dtype) for a in arrs]), cols, mult)


def _unpack(flat, shapes):
    out, off = [], 0
    for shp in shapes:
        n = math.prod(shp)
        piece = flat[off:off + n]
        if n < 4096:
            piece = lax.optimization_barrier(piece)
        out.append(piece.reshape(shp))
        off += n
    return out


def _unpack8(g8, shapes, axes):
    out, off = [], 0
    for shp, ax in zip(shapes, axes):
        n = math.prod(shp)
        a = g8[:, off:off + n].reshape((N_DEV,) + tuple(shp))
        a = jnp.moveaxis(a, 0, ax)
        out.append(a.reshape(shp[:ax] + (N_DEV * shp[ax],) + shp[ax + 1:]))
        off += n
    return out


def _wcat_t_from_w_in_t(cfg, wt):
    nba = 4 * cfg.H
    pad = jnp.zeros((LANES - nba, wt.shape[1]), wt.dtype)
    return jnp.concatenate([wt[:cfg.LXO], wt[cfg.LXO + nba:], wt[cfg.LXO:cfg.LXO + nba], pad], axis=0)


def _w_in_t_from_wcat_t(cfg, wc):
    nba = 4 * cfg.H
    return jnp.concatenate([wc[:cfg.LXO], wc[cfg.BAO:cfg.BAO + nba], wc[cfg.LXO:cfg.BAO]], axis=0)


def _gate_row(cfg, a):
    h2 = 2 * cfg.H
    return jnp.concatenate([jnp.zeros((1, h2), f32), a.reshape(1, h2), jnp.zeros((1, LANES - 2 * h2), f32)], axis=1)


def _blockdiag(cfg, w):
    w = w.reshape(2, cfg.NB // 2, 2, 64, 64)
    z = jnp.zeros_like(w[:, :, 0])
    top = jnp.concatenate([w[:, :, 0], z], axis=-1)
    bot = jnp.concatenate([z, w[:, :, 1]], axis=-1)
    return jnp.concatenate([top, bot], axis=-2).astype(bf16)


def _unblockdiag(cfg, g):
    a = g[:, :, :64, :64]
    b = g[:, :, 64:, 64:]
    return jnp.stack([a, b], axis=2).reshape(2, cfg.NB, 64, 64)


def _big_operands(cfg, big):
    return {BIG_OPERAND[n]: (_wcat_t_from_w_in_t(cfg, v) if n == "w_in" else v) for n, v in big.items()}


def _layer_operands(cfg, big, small, i):
    return dict(
        _big_operands(cfg, big),
        norm1_g=small["norm1_g"][i][None], dn_conv_w=small["dn_conv_w"][i], alog_row=_gate_row(cfg, small["dn_a_log"][i]),
        dt_row=_gate_row(cfg, small["dn_dt_bias"][i]), dn_norm_g=small["dn_norm_g"][i][None],
        lru_conv_w=small["lru_conv_w"][i], lru_conv_b=small["lru_conv_b"][i][None],
        wa=_blockdiag(cfg, small["lru_wa"][i]), wx=_blockdiag(cfg, small["lru_wx"][i]),
        lru_ba=small["lru_ba"][i], lru_bx=small["lru_bx"][i], lru_lambda=small["lru_lambda"][i],
        lru_norm_g=small["lru_norm_g"][i][None], norm2_g=small["norm2_g"][i][None], ffn_conv_w=small["ffn_conv_w"][i],
        ffn_conv_b=small["ffn_conv_b"][i][None], ple_norm_g=small["ple_norm_g"][i][None], ple_bg=small["ple_bg"][i][None],
    )


def _small_grads_to_problem(cfg, g):
    h = cfg.H
    return dict(
        norm1_g=g["norm1_g"][0], dn_conv_w=g["dn_conv_w"],
        dn_a_log=g["alog_row"][0, 2 * h:4 * h].reshape(2, h), dn_dt_bias=g["dt_row"][0, 2 * h:4 * h].reshape(2, h),
        dn_norm_g=g["dn_norm_g"][0], lru_conv_w=g["lru_conv_w"], lru_conv_b=g["lru_conv_b"][0],
        lru_wa=_unblockdiag(cfg, g["wa"]), lru_wx=_unblockdiag(cfg, g["wx"]), lru_ba=g["lru_ba"], lru_bx=g["lru_bx"],
        lru_lambda=g["lru_lambda"], lru_norm_g=g["lru_norm_g"][0], norm2_g=g["norm2_g"][0], ffn_conv_w=g["ffn_conv_w"],
        ffn_conv_b=g["ffn_conv_b"][0], ple_norm_g=g["ple_norm_g"][0], ple_bg=g["ple_bg"][0],
    )


def _local_step(cfg, get_big, small, x, p, target, on_big_grads):
    r = x.reshape(cfg.T, cfg.D)
    ops, saved = [], []
    for i in range(cfg.L):
        big, token, late = get_big(i, r)
        w = _layer_operands(cfg, big, small, i)
        if token is not None:
            w["norm1_g"] = w["norm1_g"] + token[0, 0]
        if late is not None:
            w["late"] = late
        r, sv = _layer_fwd(cfg, w, r, p[i].reshape(cfg.T, cfg.PD))
        ops.append(w)
        saved.append(sv)
    dr, loss, dgf = _k_loss(cfg, r, target.reshape(cfg.T, cfg.D), small["final_g"][None])
    gsmall = [None] * cfg.L
    for i in reversed(range(cfg.L)):
        first = EARLY + ("w_out",) if i == 0 else ()
        early = (lambda names, g: on_big_grads("%d%s" % (i, names[0]), {n: g[BIG_OPERAND[n]] for n in names})) if first else None
        dr, g = _layer_bwd(cfg, ops[i], saved[i], dr, early)
        token = on_big_grads("%d" % i, {n: (_w_in_t_from_wcat_t(cfg, g["wcat_t"]) if n == "w_in" else g[BIG_OPERAND[n]])
                                        for n in BIG if n not in first})
        if i > 0:
            ops[i - 1]["ple_bg"] = ops[i - 1]["ple_bg"] + token[0, 0]
        gsmall[i] = _small_grads_to_problem(cfg, g)
    gs = {k: jnp.stack([gl[k] for gl in gsmall]) for k in gsmall[0]}
    gs["final_g"] = dgf[0]
    return loss, dr, gs


def _row_tile(rows, limit=512):
    best = rows
    for t in range(8, min(rows, limit) + 1, 8):
        if rows % t == 0:
            best = t
    return best if best <= limit or rows <= limit else rows


def _adam_group(name, ws, gs, ms, vs, cols, tr):
    shapes = [w.shape for w in ws]
    pk = lambda arrs: _pack(arrs, cols, tr, f32)
    w2 = pk(ws)
    d, nm, nv = _adam_call(name, w2, pk(gs), pk(ms), pk(vs), min(tr, w2.shape[0]))
    return [_unpack(a.reshape(-1), shapes) for a in (d, nm, nv)]


def kernel(x, p, norm1_g, w_in, dn_conv_w, dn_a_log, dn_dt_bias, dn_norm_g, lru_conv_w, lru_conv_b, lru_wa, lru_ba, lru_wx, lru_bx, lru_lambda, lru_norm_g, w_out, norm2_g, ffn_wg, ffn_wu, ffn_conv_w, ffn_conv_b, ffn_wd, ple_norm_g, ple_wg, ple_bg, ple_wp, final_g, loss_target, m_norm1_g, m_w_in, m_dn_conv_w, m_dn_a_log, m_dn_dt_bias, m_dn_norm_g, m_lru_conv_w, m_lru_conv_b, m_lru_wa, m_lru_ba, m_lru_wx, m_lru_bx, m_lru_lambda, m_lru_norm_g, m_w_out, m_norm2_g, m_ffn_wg, m_ffn_wu, m_ffn_conv_w, m_ffn_conv_b, m_ffn_wd, m_ple_norm_g, m_ple_wg, m_ple_bg, m_ple_wp, m_final_g, v_norm1_g, v_w_in, v_dn_conv_w, v_dn_a_log, v_dn_dt_bias, v_dn_norm_g, v_lru_conv_w, v_lru_conv_b, v_lru_wa, v_lru_ba, v_lru_wx, v_lru_bx, v_lru_lambda, v_lru_norm_g, v_w_out, v_norm2_g, v_ffn_wg, v_ffn_wu, v_ffn_conv_w, v_ffn_conv_b, v_ffn_wd, v_ple_norm_g, v_ple_wg, v_ple_bg, v_ple_wp, v_final_g):
    cfg = CFG
    a = dict(locals())
    wl = {n: a[n] for n in WEIGHTS}
    ml = {n: a["m_" + n] for n in WEIGHTS}
    vl = {n: a["v_" + n] for n in WEIGHTS}
    me = 4 * lax.axis_index("x") + 2 * lax.axis_index("y") + lax.axis_index("c")
    nl = cfg.L

    blocks = [(jnp.swapaxes(wl[n], 1, 2) if BIG_T[n] else wl[n]).astype(bf16) for n in BIG]
    ss_shapes = [wl[n].shape for n in SMALL_SHARDED]
    first, s8 = _all_gather("gather_weights_0", [blocks[0][0], _pack([wl[n] for n in SMALL_SHARDED], LANES, 8, f32)])
    small = dict(zip(SMALL_SHARDED, _unpack8(s8.reshape(N_DEV, -1), ss_shapes, [2] * len(ss_shapes))))
    small.update({n: wl[n] for n in SMALL_REPL})

    def start_gather(key, i, names, behind):
        shards, _ = lax.optimization_barrier(([blk[i] for n, blk in zip(BIG, blocks) if n in names], behind))
        return _exchange_start("gather_start_" + key, shards, gather=True)

    def wait_gather(key, i, names, started, behind):
        send_sem, recv_sem, thru, _ = started
        lands = _exchange_wait("gather_wait_" + key, send_sem, recv_sem, thru, behind, gather=True)
        own = [blk[i][None] for n, blk in zip(BIG, blocks) if n in names]
        full = [lax.dynamic_update_slice_in_dim(land, o, me, 0) for land, o in zip(lands, own)]
        return {n: f.reshape(N_DEV * f.shape[1], f.shape[2]) for n, f in zip([n for n in BIG if n in names], full)}

    ahead = {}

    def get_big(i, r):
        if i > 0:
            big = wait_gather("%d" % i, i, BIG, ahead.pop(i), r)
            if i + 1 < nl:
                ahead[i + 1] = start_gather("%d" % (i + 1), i + 1, BIG, big["w_in"])
            return big, (ahead[i + 1][3] if i + 1 < nl else None), None
        rest = start_gather("0r", 0, BIG[1:], first)

        def late(x):
            big = wait_gather("0r", 0, BIG[1:], rest, x)
            ops = _big_operands(cfg, big)
            if nl > 1:
                ahead[1] = start_gather("1", 1, BIG, big["w_out"])
                ops["norm2_g"] = small["norm2_g"][0][None] + ahead[1][3][0, 0]
            return ops

        return {"w_in": first.reshape(N_DEV * first.shape[1], first.shape[2])}, rest[3], late

    pending = {}

    def on_big_grads(key, g):
        send = [g[n].reshape((N_DEV,) + blk.shape[1:]).astype(bf16) for n, blk in zip(BIG, blocks) if n in g]
        own = [lax.dynamic_index_in_dim(sd, me, 0, keepdims=True) for sd in send]
        send_sem, recv_sem, thru, token = _exchange_start("exchange_start_" + key, send)
        pending[key] = (send_sem, recv_sem, thru, own, [n for n in BIG if n in g])
        return token

    loss_part, dr, gsmall = _local_step(cfg, get_big, small, x, p, loss_target, on_big_grads)
    grad_x = dr.reshape(x.shape)

    sums = {n: [None] * nl for n in BIG}
    for key in sorted(pending):
        send_sem, recv_sem, thru, own, names = pending[key]
        lands = _exchange_wait("exchange_wait_" + key, send_sem, recv_sem, thru, dr)
        for n, land, o in zip(names, lands, own):
            slots = lax.dynamic_update_slice_in_dim(land, o, me, 0)
            sums[n][int(key[0])] = _sum8_call("sum_%s_%s" % (n, key), slots, slots.shape[1])
    gl = {}
    for n in BIG:
        s = jnp.stack(sums[n])
        gl[n] = jnp.swapaxes(s, 1, 2) if BIG_T[n] else s

    small_names = SMALL_REPL + SMALL_SHARDED
    narrow = ("lru_wa", "lru_wx")
    wide = [n for n in small_names if n not in narrow]
    wide_shapes = [gsmall[n].shape for n in wide]
    sv = _pack([gsmall[n] for n in wide] + [loss_part[0, 0:1]], LANES, 512, f32)
    nv = _pack([gsmall[n] for n in narrow], LANES, 512, bf16)
    sv8, nv8 = _all_gather("gather_small_grads", [sv, nv])
    small_sum = _sum8_call("sum_small", sv8, 512).reshape(-1)
    narrow_sum = _sum8_call("sum_small_narrow", nv8, 512).reshape(-1)
    gl.update(zip(wide, _unpack(small_sum, wide_shapes)))
    gl.update(zip(narrow, _unpack(narrow_sum, [gsmall[n].shape for n in narrow])))
    loss = small_sum[sum(math.prod(s) for s in wide_shapes)]
    for n in SMALL_SHARDED:
        shard = wl[n].shape[2]
        gl[n] = lax.dynamic_slice_in_dim(gl[n], me * shard, shard, axis=2)

    outs = {}
    for n in BIG:
        shp = wl[n].shape
        two = lambda t: t.reshape(-1, shp[-1])
        d, nm, nv = _adam_call("adam_" + n, two(wl[n]), two(gl[n]), two(ml[n]), two(vl[n]), _row_tile(math.prod(shp[:-1])))
        outs[n] = (d.reshape(shp), nm.reshape(shp), nv.reshape(shp))
    d, nm, nv = _adam_group("adam_small", [wl[n] for n in small_names], [gl[n] for n in small_names],
                            [ml[n] for n in small_names], [vl[n] for n in small_names], LANES, 64)
    for j, n in enumerate(small_names):
        outs[n] = (d[j], nm[j], nv[j])
    return (loss, grad_x, *[gl[n] for n in WEIGHTS], *[outs[n][0] for n in WEIGHTS], *[outs[n][1] for n in WEIGHTS],
            *[outs[n][2] for n in WEIGHTS])
```

```python
import functools
import math

import jax
import jax.numpy as jnp
from jax import lax
from jax.experimental import pallas as pl
from jax.experimental.pallas import tpu as pltpu

f32 = jnp.float32
bf16 = jnp.bfloat16
MESH = pl.DeviceIdType.MESH

N_DEV = 8
LANES = 128
EPS = 1e-6
LRU_C = 8.0
ADAM_LR, ADAM_B1, ADAM_B2, ADAM_EPS, ADAM_WD, ADAM_STEP = 0.001, 0.9, 0.999, 1e-08, 0.01, 10
VMEM_LIMIT = 56 * 1024 * 1024


class Cfg:
    def __init__(self, d_model=1024, bl=4, seq=2048, depth=4, heads=4, lru_width=512, d_ff=2816, ple=256,
                 tm=512, tm_ffn=256, ff_tile=256):
        self.D, self.BL, self.S, self.L, self.H = d_model, bl, seq, depth, heads
        self.DH = 128
        self.DN = heads * self.DH
        self.LW = lru_width
        self.NB = lru_width // 64
        self.FF, self.PD = d_ff, ple
        self.C = 64
        self.NC = seq // self.C
        self.T = bl * seq
        self.TM = min(tm, self.T)
        self.TMF = min(tm_ffn, self.T)
        self.TMW = min(2 * tm, self.T)
        self.FT = ff_tile
        self.ZO = 3 * self.DN
        self.LXO = 4 * self.DN
        self.LGO = self.LXO + self.LW
        self.BAO = self.LGO + self.LW
        self.PC = self.BAO + LANES
        self.IN_COLS = 4 * self.DN + 4 * heads + 2 * self.LW


CFG = Cfg()


def _mm(a, b):
    return jnp.dot(a.astype(bf16), b.astype(bf16), preferred_element_type=f32)


def _mm_nt(a, b):
    return lax.dot_general(a.astype(bf16), b.astype(bf16), (((1,), (1,)), ((), ())), preferred_element_type=f32)


def _mm_tn(a, b):
    return lax.dot_general(a.astype(bf16), b.astype(bf16), (((0,), (0,)), ((), ())), preferred_element_type=f32)


def _split2(a):
    hi = a.astype(bf16)
    return hi, (a - hi.astype(f32)).astype(bf16)


def _hmm3(a, b, dims=(((1,), (0,)), ((), ()))):
    ah, al = _split2(a)
    bh, bl = _split2(b)
    dot = functools.partial(lax.dot_general, dimension_numbers=dims, preferred_element_type=f32)
    return dot(ah, bh) + dot(ah, bl) + dot(al, bh)


def _hmm3_tn(a, b):
    return _hmm3(a, b, (((0,), (0,)), ((), ())))


def _cum_mm(mask, x, dims=(((1,), (0,)), ((), ()))):
    m = mask.astype(bf16)
    x1 = x.astype(bf16)
    r = x - x1.astype(f32)
    x2 = r.astype(bf16)
    x3 = (r - x2.astype(f32)).astype(bf16)
    dot = functools.partial(lax.dot_general, dimension_numbers=dims, preferred_element_type=f32)
    return dot(m, x1) + dot(m, x2) + dot(m, x3)


def _rms_fwd(x, g):
    inv = lax.rsqrt(jnp.mean(x * x, axis=-1, keepdims=True) + EPS)
    xh = x * inv
    return xh * g, xh, inv


def _rms_bwd(dy, xh, inv, g):
    dxh = dy * g
    dx = inv * (dxh - xh * jnp.mean(dxh * xh, axis=-1, keepdims=True))
    dg = jnp.sum(dy * xh, axis=0, keepdims=True)
    return dx, dg


def _sigmoid(x):
    return 1.0 / (1.0 + jnp.exp(-x))


def _softplus(x):
    return jnp.maximum(x, 0.0) + jnp.log(1.0 + jnp.exp(-jnp.abs(x)))


def _silu(x):
    s = _sigmoid(x)
    return x * s, s * (1.0 + x * (1.0 - s))


_GC = math.sqrt(2.0 / math.pi)


def _gelu(x):
    t = jnp.tanh(_GC * (x + 0.044715 * x * x * x))
    y = 0.5 * x * (1.0 + t)
    dy = 0.5 * (1.0 + t) + 0.5 * x * (1.0 - t * t) * _GC * (1.0 + 3.0 * 0.044715 * x * x)
    return y, dy


def _nexpm1(x):
    ser = -x * (1.0 + x * 0.5 * (1.0 + x * (1.0 / 3.0) * (1.0 + x * 0.25 * (1.0 + x * 0.2))))
    return jnp.where(x > -0.1, ser, 1.0 - jnp.exp(x))


def _shift(x, s, fill=0.0):
    if s == 0:
        return x
    n = x.shape[0]
    t = lax.broadcasted_iota(jnp.int32, x.shape, 0)
    r = pltpu.roll(x, (-s) % n, 0)
    return jnp.where((t + s >= 0) & (t + s < n), r, fill)


def _conv_fwd(x, w_ref, left):
    k = w_ref.shape[0]
    out = _shift(x, -left) * w_ref[0:1, :]
    for j in range(1, k):
        out = out + _shift(x, j - left) * w_ref[j:j + 1, :]
    return out


def _conv_bwd(dout, x, w_ref, left):
    k = w_ref.shape[0]
    dx = None
    dws = []
    for j in range(k):
        term = _shift(dout, -(j - left)) * w_ref[j:j + 1, :]
        dx = term if dx is None else dx + term
        dws.append(jnp.sum(dout * _shift(x, j - left), axis=0, keepdims=True))
    return dx, dws


def _scan_refs(scans, n):
    blk = 64
    nb = n // blk
    sub = lax.broadcasted_iota(jnp.int32, (blk, LANES), 0) & 7

    def local(a, b, rev):
        for d in (1, 2, 4):
            ok = (sub < 8 - d) if rev else (sub >= d)
            sh = (blk - d) if rev else d
            b = a * jnp.where(ok, pltpu.roll(b, sh, 0), 0.0) + b
            a = a * jnp.where(ok, pltpu.roll(a, sh, 0), 1.0)
        return a, b

    def body(i, carries):
        new = []
        for (a_ref, b_ref, h_ref, rev), carry in zip(scans, carries):
            j = (nb - 1 - i) if rev else i
            base = pl.multiple_of(j * blk, blk)
            a, b = local(a_ref[pl.ds(base, blk), :], b_ref[pl.ds(base, blk), :], rev)
            order = range(blk // 8 - 1, -1, -1) if rev else range(blk // 8)
            for v in order:
                h = b[8 * v:8 * v + 8, :] + a[8 * v:8 * v + 8, :] * carry
                h_ref[pl.ds(base + 8 * v, 8), :] = h
                carry = h[0:1, :] if rev else h[7:8, :]
            new.append(carry)
        return tuple(new)

    lax.fori_loop(0, nb, body, tuple(jnp.zeros((1, LANES), f32) for _ in scans))


def _tri_masks(c, rev):
    i = lax.broadcasted_iota(jnp.int32, (c, c), 0)
    j = lax.broadcasted_iota(jnp.int32, (c, c), 1)
    incl = (i <= j) if rev else (i >= j)
    strict = (i < j) if rev else (i > j)
    incl_t = (i >= j) if rev else (i <= j)
    return incl, strict, incl_t


def _map(f, *lists):
    return [f(*a) for a in zip(*lists)]


def _tri_inv(mats, tick=lambda: None):
    c = mats[0].shape[0]
    i = lax.broadcasted_iota(jnp.int32, (c, c), 0)
    j = lax.broadcasted_iota(jnp.int32, (c, c), 1)
    eye = jnp.where(i == j, 1.0, 0.0)
    t = [eye - a for a in mats]
    pw = _map(_hmm3, mats, mats)
    for it in range(5):
        tick()
        t = _map(lambda ti, ui: ti + ui, t, _map(_hmm3, t, pw))
        if it < 4:
            pw = _map(_hmm3, pw, pw)
    return t


def _gdn_decay(gs, revs):
    c = gs[0].shape[0]
    masks = [_tri_masks(c, r) for r in revs]
    gb = [jnp.broadcast_to(g, (c, c)) for g in gs]
    mcol = _map(lambda m, x: _cum_mm(m[0], x), masks, gb)
    mrow = _map(lambda m, x: jnp.sum(jnp.where(m[2], x, 0.0), axis=0, keepdims=True), masks, gb)
    dec = _map(lambda m, a, b: jnp.exp(jnp.where(m[0], a - b, -1e30)), masks, mcol, mrow)
    return [m[:, 0:1] for m in mcol], [jnp.sum(g, axis=0, keepdims=True) for g in gs], dec


def _gdn_prep(qs, k, v, g, beta, kk, qk, revs, tick=lambda: None):
    c = k[0].shape[0]
    masks = [_tri_masks(c, r) for r in revs]
    gcum, glast, dec = _gdn_decay(g, revs)
    e = [jnp.exp(x) for x in gcum]
    tick()
    tm = _tri_inv(_map(lambda m, b, x, d: jnp.where(m[1], b * x * d, 0.0), masks, beta, kk, dec), tick)
    tick()
    u = _map(lambda t, vi, b: _hmm3(t, vi * b), tm, v, beta)
    tick()
    w = _map(lambda t, ki, b, ei: _hmm3(t, ki * b * ei), tm, k, beta, e)
    p = _map(lambda m, x, d: jnp.where(m[0], x * d, 0.0), masks, qk, dec)
    return dict(dec=dec, cd=[jnp.exp(x) for x in glast], tm=tm, u=u, w=w, p=p, qd=_map(lambda a, b: a * b, qs, e),
                kd=_map(lambda ki, gl, gc: ki * jnp.exp(gl - gc), k, glast, gcum))


def _lane_pick(x, lane):
    l = lax.broadcasted_iota(jnp.int32, x.shape, 1)
    return jnp.sum(jnp.where(l == lane, x, 0.0), axis=1, keepdims=True)


def _params(sem, vmem=VMEM_LIMIT):
    return pltpu.CompilerParams(dimension_semantics=sem, vmem_limit_bytes=vmem)


def _row_call(name, body, t, tm, row_ins, full_ins, row_outs, acc_outs):
    in_specs = [pl.BlockSpec((tm, w), functools.partial(lambda i, c: (i, c), c=c)) for (_, w, c) in row_ins]
    for a in full_ins:
        in_specs.append(pl.BlockSpec(a.shape, functools.partial(lambda i, n: (0,) * n, n=a.ndim)))
    out_specs = [pl.BlockSpec((tm, w), lambda i: (i, 0)) for (w, _) in row_outs]
    out_shape = [jax.ShapeDtypeStruct((t, w), dt) for (w, dt) in row_outs]
    for shp, dt in acc_outs:
        out_specs.append(pl.BlockSpec(shp, functools.partial(lambda i, n: (0,) * n, n=len(shp))))
        out_shape.append(jax.ShapeDtypeStruct(shp, dt))
    return pl.pallas_call(
        body, name=name, grid=(t // tm,), in_specs=in_specs, out_specs=out_specs, out_shape=out_shape,
        compiler_params=_params(("arbitrary",)),
    )(*[a for (a, _, _) in row_ins], *full_ins)


def _k_in(cfg, r, g1, wcat):
    def body(r_ref, g_ref, w_ref, h_ref, proj_ref):
        y, _, _ = _rms_fwd(r_ref[...], g_ref[...])
        hb = y.astype(bf16)
        h_ref[...] = hb
        proj_ref[...] = _mm_nt(hb, w_ref[...])

    return _row_call("in_proj", body, cfg.T, cfg.TM, [(r, cfg.D, 0)], [g1, wcat],
                     [(cfg.D, bf16), (cfg.PC, f32)], [])


def _k_prep(cfg, proj, conv_w):
    dn, s = cfg.DN, cfg.S

    def body(x_ref, w_ref, o_ref):
        sec = pl.program_id(1)
        c = _conv_fwd(x_ref[...], w_ref, 2)
        y, _ = _silu(c)
        for h in range(cfg.H):
            yh = y[:, h * 128:(h + 1) * 128]
            nh = yh * lax.rsqrt(jnp.sum(yh * yh, axis=1, keepdims=True) + EPS)
            o_ref[:, h * 128:(h + 1) * 128] = jnp.where(sec < 2, nh, yh)

    return pl.pallas_call(
        body, name="dn_prep", grid=(cfg.BL, 3),
        in_specs=[pl.BlockSpec((s, dn), lambda b, j: (b, j)), pl.BlockSpec((4, dn), lambda b, j: (0, j))],
        out_specs=pl.BlockSpec((s, dn), lambda b, j: (b, j)),
        out_shape=jax.ShapeDtypeStruct((cfg.T, 3 * dn), f32),
        compiler_params=_params(("arbitrary", "arbitrary")),
    )(proj, conv_w)


def _gate_cols(ba, alog_row, dt_row, lane_b, lane_a):
    beta = _sigmoid(_lane_pick(ba, lane_b))
    alpha = _lane_pick(ba, lane_a)
    aexp = jnp.exp(_lane_pick(alog_row, lane_a))
    dtb = _lane_pick(dt_row, lane_a)
    xa = alpha + dtb
    g = -aexp * _softplus(xa)
    return beta, g, aexp, xa


def _gdn_specs(cfg):
    s, c, nc = cfg.S, cfg.C, cfg.NC
    tok2 = pl.BlockSpec((2, s, 128), lambda b, h: (0, b, h))
    mat2 = pl.BlockSpec((2, 1, nc, c, c), lambda b, h: (0, b, h, 0, 0))
    cd2 = pl.BlockSpec((2, 1, nc, 8, LANES), lambda b, h: (0, b, h, 0, 0))
    shapes = dict(
        tok32=jax.ShapeDtypeStruct((2, cfg.T, cfg.DN), f32), tok16=jax.ShapeDtypeStruct((2, cfg.T, cfg.DN), bf16),
        mat32=jax.ShapeDtypeStruct((2, cfg.BL, cfg.H * nc, c, c), f32), mat16=jax.ShapeDtypeStruct((2, cfg.BL, cfg.H * nc, c, c), bf16),
        cd=jax.ShapeDtypeStruct((2, cfg.BL, cfg.H * nc, 8, LANES), f32))
    return tok2, mat2, cd2, shapes


def _k_gdn_fwd(cfg, qkv, proj, alog_row, dt_row):
    s, c, nc, hh = cfg.S, cfg.C, cfg.NC, cfg.H
    ba_blk = cfg.BAO // LANES
    per = 4 if nc % 4 == 0 else 1

    def body(q_ref, k_ref, v_ref, ba_ref, al_ref, dt_ref, o_ref, u_o, w_o, qd_o, kd_o, p_o, t_o, dec_o, cd_o):
        h = pl.program_id(1)
        groups = nc // per

        def chunk_of(d, g, j):
            return (nc - 1 - (g * per + j)) if d == 1 else (g * per + j)

        def prep(g, tick=lambda: None):
            chains, qs, k, v, kk, qk, gt, beta = [], [], [], [], [], [], [], []
            for j in range(per):
                for d in range(2):
                    n = chunk_of(d, g, j)
                    rows = pl.ds(pl.multiple_of(n * c, c), c)
                    kj, vj = k_ref[rows, :], v_ref[rows, :]
                    qj = q_ref[rows, :] * (128 ** -0.5)
                    bd, gd, _, _ = _gate_cols(ba_ref[rows, :], al_ref[...], dt_ref[...], d * hh + h, 2 * hh + d * hh + h)
                    chains.append((d, n, rows))
                    for lst, val in ((qs, qj), (k, kj), (v, vj), (kk, _mm_nt(kj, kj)), (qk, _mm_nt(qj, kj)), (gt, gd), (beta, bd)):
                        lst.append(val)
            z = _gdn_prep(qs, k, v, gt, beta, kk, qk, [d == 1 for d, _, _ in chains], tick)
            for x, (d, n, rows) in enumerate(chains):
                u_o[d, rows, :] = z["u"][x]
                w_o[d, rows, :] = z["w"][x].astype(bf16)
                qd_o[d, rows, :] = z["qd"][x].astype(bf16)
                kd_o[d, rows, :] = z["kd"][x].astype(bf16)
                p_o[d, 0, n] = z["p"][x].astype(bf16)
                t_o[d, 0, n] = z["tm"][x]
                dec_o[d, 0, n] = z["dec"][x]
                cd_o[d, 0, n] = jnp.broadcast_to(z["cd"][x], (8, LANES))

        def steps(g, box):
            for j in range(per):
                sts = box[0]
                ns = [chunk_of(d, g, j) for d in range(2)]
                rows = [pl.ds(pl.multiple_of(n * c, c), c) for n in ns]
                ws = [_mm(w_o[d, rows[d], :], sts[d]) for d in range(2)]
                qs_ = [_mm(qd_o[d, rows[d], :], sts[d]) for d in range(2)]
                yield
                vn = [u_o[d, rows[d], :] - ws[d] for d in range(2)]
                box[0] = tuple(sts[d] * cd_o[d, 0, ns[d]][0:1, :] + _mm_tn(kd_o[d, rows[d], :], vn[d]) for d in range(2))
                for d in range(2):
                    o_ref[rows[d], :] = o_ref[rows[d], :] + qs_[d] + _mm(p_o[d, 0, ns[d]], vn[d])
                yield

        o_ref[...] = jnp.zeros_like(o_ref)
        prep(0)

        def fused(g, sts):
            box = [sts]
            chain = steps(g - 1, box)
            prep(g, lambda: next(chain, None))
            for _ in chain:
                pass
            return box[0]

        z0 = jnp.zeros((128, 128), f32)
        box = [lax.fori_loop(1, groups, fused, (z0, z0))]
        for _ in steps(groups - 1, box):
            pass

    blk = lambda off: pl.BlockSpec((s, 128), functools.partial(lambda b, h, off: (b, off + h), off=off))
    row = pl.BlockSpec((1, LANES), lambda b, h: (0, 0))
    tok2, mat2, cd2, shp = _gdn_specs(cfg)
    return pl.pallas_call(
        body, name="gdn_fwd", grid=(cfg.BL, hh),
        in_specs=[blk(0), blk(hh), blk(2 * hh), pl.BlockSpec((s, LANES), lambda b, h: (b, ba_blk)), row, row],
        out_specs=[pl.BlockSpec((s, 128), lambda b, h: (b, h)), tok2, tok2, tok2, tok2, mat2, mat2, mat2, cd2],
        out_shape=[jax.ShapeDtypeStruct((cfg.T, cfg.DN), f32), shp["tok32"], shp["tok16"], shp["tok16"], shp["tok16"],
                   shp["mat16"], shp["mat32"], shp["mat32"], shp["cd"]],
        compiler_params=_params(("arbitrary", "arbitrary")),
    )(qkv, qkv, qkv, proj, alog_row, dt_row)


def _lru_gates(xc, wa, wx, ba, bx, lam):
    ra = _sigmoid(_mm(xc, wa) + ba)
    ig = _sigmoid(_mm(xc, wx) + bx)
    sp = _softplus(-lam)
    la = -LRU_C * ra * sp
    a = jnp.exp(la)
    m = jnp.sqrt(_nexpm1(2.0 * la))
    return dict(ra=ra, ig=ig, sp=sp, a=a, m=m, gx=ig * xc)


def _lru_specs(cfg, outer_b):
    s = cfg.S
    lx_blk = cfg.LXO // LANES
    if outer_b:
        ix = lambda f: (lambda b, ct: f(b, ct))
    else:
        ix = lambda f: (lambda ct, b: f(b, ct))
    return dict(
        lx=pl.BlockSpec((s, LANES), ix(lambda b, ct: (b, lx_blk + ct))),
        tok=pl.BlockSpec((s, LANES), ix(lambda b, ct: (b, ct))),
        cw=pl.BlockSpec((4, LANES), ix(lambda b, ct: (0, ct))),
        row=pl.BlockSpec((1, LANES), ix(lambda b, ct: (0, ct))),
        w=pl.BlockSpec((2, 1, LANES, LANES), ix(lambda b, ct: (0, ct, 0, 0))),
        two=pl.BlockSpec((2, LANES), ix(lambda b, ct: (0, ct))),
    )


def _k_lru_fwd(cfg, proj, conv_w, conv_b, wa, wx, ba, bx, lam):
    def body(lx_ref, cw_ref, cb_ref, wa_ref, wx_ref, ba_ref, bx_ref, lam_ref, o_ref, a_s, b_s, h_s):
        xc = _conv_fwd(lx_ref[...], cw_ref, 2) + cb_ref[...]
        for d in range(2):
            z = _lru_gates(xc, wa_ref[d, 0], wx_ref[d, 0], ba_ref[d:d + 1, :], bx_ref[d:d + 1, :], lam_ref[d:d + 1, :])
            a_s[d] = z["a"]
            b_s[d] = z["m"] * z["gx"]
        _scan_refs([(a_s.at[d], b_s.at[d], h_s.at[d], d == 1) for d in range(2)], cfg.S)
        o_ref[...] = h_s[0] + h_s[1]

    sp = _lru_specs(cfg, True)
    return pl.pallas_call(
        body, name="lru_fwd", grid=(cfg.BL, cfg.LW // LANES),
        in_specs=[sp["lx"], sp["cw"], sp["row"], sp["w"], sp["w"], sp["two"], sp["two"], sp["two"]],
        out_specs=sp["tok"], out_shape=jax.ShapeDtypeStruct((cfg.T, cfg.LW), f32),
        scratch_shapes=[pltpu.VMEM((2, cfg.S, LANES), f32)] * 3,
        compiler_params=_params(("arbitrary", "arbitrary")),
    )(proj, conv_w, conv_b, wa, wx, ba, bx, lam)


def _mix_parts(cfg, o, z, lg, hs, dng, lrg):
    heads = []
    for h in range(cfg.H):
        sl = slice(h * 128, (h + 1) * 128)
        y, xh, inv = _rms_fwd(o[:, sl], dng)
        sz, dsz = _silu(z[:, sl])
        heads.append((y, xh, inv, sz, dsz))
    gl, dgl = _gelu(lg)
    y2, xh2, inv2 = _rms_fwd(gl * hs, lrg)
    return heads, (y2, xh2, inv2, gl, dgl)


def _k_mix(cfg, o, proj, hs, r, dng, lrg, wout):
    dn = cfg.DN

    def body(o_ref, z_ref, lg_ref, hs_ref, r_ref, dng_ref, lrg_ref, w_ref, mix_ref, out_ref):
        heads, lru = _mix_parts(cfg, o_ref[...], z_ref[...], lg_ref[...], hs_ref[...], dng_ref[...], lrg_ref[...])
        for h, (y, _, _, sz, _) in enumerate(heads):
            mix_ref[:, h * 128:(h + 1) * 128] = (y * sz).astype(bf16)
        mix_ref[:, dn:] = lru[0].astype(bf16)
        out_ref[...] = r_ref[...] + jnp.dot(mix_ref[...], w_ref[...], preferred_element_type=f32)

    return _row_call("mix_out", body, cfg.T, cfg.TM,
                     [(o, dn, 0), (proj, dn, cfg.ZO // dn), (proj, cfg.LW, cfg.LGO // cfg.LW), (hs, cfg.LW, 0), (r, cfg.D, 0)],
                     [dng, lrg, wout], [(cfg.D, bf16), (cfg.D, f32)], [])


def _k_ffn_a(cfg, r, g2, wg, wu):
    def body(r_ref, g_ref, wg_ref, wu_ref, h_ref, gp_ref, up_ref):
        y, _, _ = _rms_fwd(r_ref[...], g_ref[...])
        hb = y.astype(bf16)
        h_ref[...] = hb
        gp_ref[...] = _mm_nt(hb, wg_ref[...])
        up_ref[...] = _mm_nt(hb, wu_ref[...])

    return _row_call("ffn_in", body, cfg.T, cfg.TMF, [(r, cfg.D, 0)], [g2, wg, wu],
                     [(cfg.D, bf16), (cfg.FF, f32), (cfg.FF, f32)], [])


def _k_ffn_b(cfg, gp, up, conv_w, conv_b):
    s, ft = cfg.S, cfg.FT

    def body(gp_ref, up_ref, w_ref, b_ref, o_ref):
        gate = _conv_fwd(gp_ref[...], w_ref, 1) + b_ref[...]
        gl, _ = _gelu(gate)
        o_ref[...] = (gl * up_ref[...]).astype(bf16)

    tok = pl.BlockSpec((s, ft), lambda b, j: (b, j))
    return pl.pallas_call(
        body, name="ffn_act", grid=(cfg.BL, cfg.FF // ft),
        in_specs=[tok, tok, pl.BlockSpec((3, ft), lambda b, j: (0, j)), pl.BlockSpec((1, ft), lambda b, j: (0, j))],
        out_specs=tok, out_shape=jax.ShapeDtypeStruct((cfg.T, cfg.FF), bf16),
        compiler_params=_params(("arbitrary", "arbitrary")),
    )(gp, up, conv_w, conv_b)


def _k_ffn_c(cfg, act, r, wd):
    def body(a_ref, r_ref, w_ref, o_ref):
        o_ref[...] = r_ref[...] + jnp.dot(a_ref[...], w_ref[...], preferred_element_type=f32)

    return _row_call("ffn_out", body, cfg.T, cfg.TM, [(act, cfg.FF, 0), (r, cfg.D, 0)], [wd], [(cfg.D, f32)], [])[0]


def _k_ple(cfg, r, p, gp, wpg, bg, wpp):
    def body(r_ref, p_ref, g_ref, wg_ref, bg_ref, wp_ref, pn_ref, o_ref):
        x = r_ref[...]
        y, _, _ = _rms_fwd(x, g_ref[...])
        pn = y.astype(bf16)
        pn_ref[...] = pn
        pg = _sigmoid(jnp.dot(pn, wg_ref[...], preferred_element_type=f32) + bg_ref[...])
        o_ref[...] = x + pg * _mm_nt(p_ref[...], wp_ref[...])

    return _row_call("ple", body, cfg.T, cfg.TM, [(r, cfg.D, 0), (p, cfg.PD, 0)], [gp, wpg, bg, wpp],
                     [(cfg.D, bf16), (cfg.D, f32)], [])


def _k_loss(cfg, r, tgt, gf):
    d = cfg.D

    def body(r_ref, t_ref, g_ref, dr_ref, loss_ref, dg_ref):
        @pl.when(pl.program_id(0) == 0)
        def _():
            loss_ref[...] = jnp.zeros_like(loss_ref)
            dg_ref[...] = jnp.zeros_like(dg_ref)

        g = g_ref[...]
        y, xh, inv = _rms_fwd(r_ref[...], g)
        err = y - t_ref[...]
        loss_ref[...] = loss_ref[...] + (0.5 / d) * jnp.sum(err * err)
        dx, dg = _rms_bwd(err * (1.0 / d), xh, inv, g)
        dr_ref[...] = dx
        dg_ref[...] = dg_ref[...] + dg

    return _row_call("loss_head", body, cfg.T, cfg.TM, [(r, d, 0), (tgt, d, 0)], [gf], [(d, f32)],
                     [((1, LANES), f32), ((1, d), f32)])


def _zero_at_first(cond, *refs):
    @pl.when(cond)
    def _():
        for r in refs:
            r[...] = jnp.zeros_like(r)


def _b_ple(cfg, dr3, r2, p, gp, wpg, bg, wpp):
    d = cfg.D

    def body(dr_ref, r_ref, p_ref, g_ref, wg_ref, bg_ref, wp_ref, dr2_ref, dr2b_ref, dlog_ref, dpp_ref, dgp_ref, dbg_ref):
        _zero_at_first(pl.program_id(0) == 0, dgp_ref, dbg_ref)
        g = g_ref[...]
        dr = dr_ref[...]
        y, xh, inv = _rms_fwd(r_ref[...], g)
        pg = _sigmoid(jnp.dot(y.astype(bf16), wg_ref[...], preferred_element_type=f32) + bg_ref[...])
        pp = _mm_nt(p_ref[...], wp_ref[...])
        dpp_ref[...] = (dr * pg).astype(bf16)
        dlog = dr * pp * pg * (1.0 - pg)
        dlog_ref[...] = dlog.astype(bf16)
        dbg_ref[...] = dbg_ref[...] + jnp.sum(dlog, axis=0, keepdims=True)
        dx, dg = _rms_bwd(_mm_nt(dlog, wg_ref[...]), xh, inv, g)
        dgp_ref[...] = dgp_ref[...] + dg
        dr2_ref[...] = dr + dx
        dr2b_ref[...] = (dr + dx).astype(bf16)

    return _row_call("ple_bwd", body, cfg.T, cfg.TM, [(dr3, d, 0), (r2, d, 0), (p, cfg.PD, 0)], [gp, wpg, bg, wpp],
                     [(d, f32), (d, bf16), (d, bf16), (d, bf16)], [((1, d), f32), ((1, d), f32)])


def _b_ffn_bc(cfg, dr2, wd, gp, up, conv_w, conv_b):
    s, ft, d = cfg.S, cfg.FT, cfg.D

    def body(dr_ref, wd_ref, gp_ref, up_ref, w_ref, b_ref, dgp_ref, dup_ref, dwd_ref, dcw_ref, dcb_ref):
        _zero_at_first(pl.program_id(1) == 0, dwd_ref, dcw_ref, dcb_ref)
        drb = dr_ref[...]
        dact = _mm_nt(drb, wd_ref[...])
        gpre = gp_ref[...]
        up = up_ref[...]
        gl, dgl = _gelu(_conv_fwd(gpre, w_ref, 1) + b_ref[...])
        dup_ref[...] = (dact * gl).astype(bf16)
        dgate = dact * up * dgl
        dcb_ref[...] = dcb_ref[...] + jnp.sum(dgate, axis=0, keepdims=True)
        dx, dws = _conv_bwd(dgate, gpre, w_ref, 1)
        for j, dw in enumerate(dws):
            dcw_ref[j:j + 1, :] = dcw_ref[j:j + 1, :] + dw
        dgp_ref[...] = dx.astype(bf16)
        dwd_ref[...] = dwd_ref[...] + _mm_tn((gl * up).astype(bf16), drb)

    tok = pl.BlockSpec((s, ft), lambda j, b: (b, j))
    return pl.pallas_call(
        body, name="ffn_act_bwd", grid=(cfg.FF // ft, cfg.BL),
        in_specs=[pl.BlockSpec((s, d), lambda j, b: (b, 0)), pl.BlockSpec((ft, d), lambda j, b: (j, 0)), tok, tok,
                  pl.BlockSpec((3, ft), lambda j, b: (0, j)), pl.BlockSpec((1, ft), lambda j, b: (0, j))],
        out_specs=[tok, tok, pl.BlockSpec((ft, d), lambda j, b: (j, 0)), pl.BlockSpec((3, ft), lambda j, b: (0, j)),
                   pl.BlockSpec((1, ft), lambda j, b: (0, j))],
        out_shape=[jax.ShapeDtypeStruct((cfg.T, cfg.FF), bf16), jax.ShapeDtypeStruct((cfg.T, cfg.FF), bf16),
                   jax.ShapeDtypeStruct((cfg.FF, d), f32), jax.ShapeDtypeStruct((3, cfg.FF), f32),
                   jax.ShapeDtypeStruct((1, cfg.FF), f32)],
        compiler_params=_params(("arbitrary", "arbitrary")),
    )(dr2, wd, gp, up, conv_w, conv_b)


def _b_ffn_a(cfg, dgp, dup, dr2, r_mid, g2, wg, wu):
    d = cfg.D

    def body(dgp_ref, dup_ref, dr_ref, r_ref, g_ref, wg_ref, wu_ref, o_ref, dg_ref):
        _zero_at_first(pl.program_id(0) == 0, dg_ref)
        g = g_ref[...]
        _, xh, inv = _rms_fwd(r_ref[...], g)
        dh = _mm(dgp_ref[...], wg_ref[...]) + _mm(dup_ref[...], wu_ref[...])
        dx, dg = _rms_bwd(dh, xh, inv, g)
        dg_ref[...] = dg_ref[...] + dg
        o_ref[...] = dr_ref[...] + dx

    return _row_call("ffn_in_bwd", body, cfg.T, cfg.TMF, [(dgp, cfg.FF, 0), (dup, cfg.FF, 0), (dr2, d, 0), (r_mid, d, 0)],
                     [g2, wg, wu], [(d, f32)], [((1, d), f32)])


def _mm_tn_call(cfg, name, x, dy, tn):
    t, k = x.shape
    n = dy.shape[1]
    tm = cfg.TMW
    last = t // tm - 1

    def body(x_ref, dy_ref, o_ref, acc):
        _zero_at_first(pl.program_id(1) == 0, acc)
        acc[...] = acc[...] + _mm_tn(x_ref[...], dy_ref[...])

        @pl.when(pl.program_id(1) == last)
        def _():
            o_ref[...] = acc[...].astype(bf16)

    return pl.pallas_call(
        body, name=name, grid=(n // tn, t // tm),
        in_specs=[pl.BlockSpec((tm, k), lambda j, i: (i, 0)), pl.BlockSpec((tm, tn), lambda j, i: (i, j))],
        out_specs=pl.BlockSpec((k, tn), lambda j, i: (0, j)),
        out_shape=jax.ShapeDtypeStruct((k, n), bf16), scratch_shapes=[pltpu.VMEM((k, tn), f32)],
        compiler_params=_params(("arbitrary", "arbitrary")),
    )(x, dy)


def _b_mix(cfg, dr, o, proj, hs, dng, lrg, wout):
    dn, lw, d = cfg.DN, cfg.LW, cfg.D

    def body(dr_ref, o_ref, z_ref, lg_ref, hs_ref, dng_ref, lrg_ref, w_ref, do_ref, dz_ref, dlg_ref, dhs_ref, ddn_ref, dlr_ref):
        _zero_at_first(pl.program_id(0) == 0, ddn_ref, dlr_ref)
        dng, lrg = dng_ref[...], lrg_ref[...]
        hs = hs_ref[...]
        o, z = o_ref[...], z_ref[...]
        heads, lru = _mix_parts(cfg, o, z, lg_ref[...], hs, dng, lrg)
        drb = dr_ref[...].astype(bf16)
        dmix_dn = _mm_nt(drb, w_ref[0:dn, :])
        dmix_lr = _mm_nt(drb, w_ref[dn:, :])
        dgn = jnp.zeros_like(dng)
        for h, (y, xh, inv, sz, dsz) in enumerate(heads):
            sl = slice(h * 128, (h + 1) * 128)
            dm = dmix_dn[:, sl]
            dz_ref[:, sl] = (dm * y * dsz).astype(bf16)
            dx, dg = _rms_bwd(dm * sz, xh, inv, dng)
            do_ref[:, sl] = dx.astype(bf16)
            dgn = dgn + dg
        ddn_ref[...] = ddn_ref[...] + dgn
        _, xh2, inv2, gl, dgl = lru
        dx2, dg2 = _rms_bwd(dmix_lr, xh2, inv2, lrg)
        dlr_ref[...] = dlr_ref[...] + dg2
        dlg_ref[...] = (dx2 * hs * dgl).astype(bf16)
        dhs_ref[...] = dx2 * gl

    return _row_call("mix_bwd", body, cfg.T, cfg.TM,
                     [(dr, d, 0), (o, dn, 0), (proj, dn, cfg.ZO // dn), (proj, lw, cfg.LGO // lw), (hs, lw, 0)],
                     [dng, lrg, wout], [(dn, bf16), (dn, bf16), (lw, bf16), (lw, f32)], [((1, 128), f32), ((1, lw), f32)])


def _b_lru(cfg, proj, dhs, conv_w, conv_b, wa, wx, ba, bx, lam):
    def body(lx_ref, dh_ref, cw_ref, cb_ref, wa_ref, wx_ref, ba_ref, bx_ref, lam_ref,
             dlx_ref, dcw_ref, dcb_ref, dwa_ref, dwx_ref, dba_ref, dbx_ref, dlam_ref, a_s, b_s, h_s, an_s, l_s):
        _zero_at_first(pl.program_id(1) == 0, dcw_ref, dcb_ref, dwa_ref, dwx_ref, dba_ref, dbx_ref, dlam_ref)
        lx = lx_ref[...]
        xc = _conv_fwd(lx, cw_ref, 2) + cb_ref[...]
        dxc = jnp.zeros_like(xc)
        gates = []
        for d in range(2):
            z = _lru_gates(xc, wa_ref[d, 0], wx_ref[d, 0], ba_ref[d:d + 1, :], bx_ref[d:d + 1, :], lam_ref[d:d + 1, :])
            a_s[d] = z["a"]
            b_s[d] = z["m"] * z["gx"]
            an_s[d] = _shift(z["a"], -1 if d == 1 else 1, 0.0)
            gates.append(z)
        _scan_refs([(a_s.at[d], b_s.at[d], h_s.at[d], d == 1) for d in range(2)]
                   + [(an_s.at[d], dh_ref, l_s.at[d], d == 0) for d in range(2)], cfg.S)
        for d in range(2):
            rev = d == 1
            lam = lam_ref[d:d + 1, :]
            z = gates[d]
            a, m, ra, ig, sp = z["a"], z["m"], z["ra"], z["ig"], z["sp"]
            lmb = l_s[d]
            h_prev = _shift(h_s[d], 1 if rev else -1, 0.0)
            da = lmb * h_prev
            dm = lmb * z["gx"]
            dgx = lmb * m
            dla = da * a - dm * (a * a) / jnp.maximum(m, 1e-30)
            dra = dla * (-LRU_C) * sp
            dsp = jnp.sum(dla * (-LRU_C) * ra, axis=0, keepdims=True)
            dlam_ref[d:d + 1, :] = dlam_ref[d:d + 1, :] - dsp * _sigmoid(-lam)
            dpa = dra * ra * (1.0 - ra)
            dpx = dgx * xc * ig * (1.0 - ig)
            dba_ref[d:d + 1, :] = dba_ref[d:d + 1, :] + jnp.sum(dpa, axis=0, keepdims=True)
            dbx_ref[d:d + 1, :] = dbx_ref[d:d + 1, :] + jnp.sum(dpx, axis=0, keepdims=True)
            dwa_ref[d, 0] = dwa_ref[d, 0] + _mm_tn(xc, dpa)
            dwx_ref[d, 0] = dwx_ref[d, 0] + _mm_tn(xc, dpx)
            dxc = dxc + dgx * ig + _mm_nt(dpa, wa_ref[d, 0]) + _mm_nt(dpx, wx_ref[d, 0])
        dcb_ref[...] = dcb_ref[...] + jnp.sum(dxc, axis=0, keepdims=True)
        dx, dws = _conv_bwd(dxc, lx, cw_ref, 2)
        for j, dw in enumerate(dws):
            dcw_ref[j:j + 1, :] = dcw_ref[j:j + 1, :] + dw
        dlx_ref[...] = dx.astype(bf16)

    sp = _lru_specs(cfg, False)
    nct = cfg.LW // LANES
    return pl.pallas_call(
        body, name="lru_bwd", grid=(nct, cfg.BL),
        in_specs=[sp["lx"], sp["tok"], sp["cw"], sp["row"], sp["w"], sp["w"], sp["two"], sp["two"], sp["two"]],
        out_specs=[sp["tok"], sp["cw"], sp["row"], sp["w"], sp["w"], sp["two"], sp["two"], sp["two"]],
        out_shape=[jax.ShapeDtypeStruct((cfg.T, cfg.LW), bf16), jax.ShapeDtypeStruct((4, cfg.LW), f32),
                   jax.ShapeDtypeStruct((1, cfg.LW), f32), jax.ShapeDtypeStruct((2, nct, LANES, LANES), f32),
                   jax.ShapeDtypeStruct((2, nct, LANES, LANES), f32), jax.ShapeDtypeStruct((2, cfg.LW), f32),
                   jax.ShapeDtypeStruct((2, cfg.LW), f32), jax.ShapeDtypeStruct((2, cfg.LW), f32)],
        scratch_shapes=[pltpu.VMEM((2, cfg.S, LANES), f32)] * 5,
        compiler_params=_params(("arbitrary", "arbitrary")),
    )(proj, dhs, conv_w, conv_b, wa, wx, ba, bx, lam)


def _b_gdn(cfg, qkv, proj, do, alog_row, dt_row, saved):
    s, c, nc, hh = cfg.S, cfg.C, cfg.NC, cfg.H
    ba_blk = cfg.BAO // LANES
    scale = 128 ** -0.5
    per = 4
    assert nc % per == 0

    def body(q_ref, k_ref, v_ref, ba_ref, do_ref, al_ref, dt_ref, u_i, w_i, qd_i, kd_i, p_i, t_i, dec_i, cd_i,
             dqkv_ref, dba_ref, dal_ref, ddt_ref, vn_s, dvn_s, st_s, dst_s):
        h = pl.program_id(1)
        _zero_at_first((pl.program_id(0) == 0) & (h == 0), dal_ref, ddt_ref)
        _zero_at_first(h == 0, dba_ref)
        lane = lax.broadcasted_iota(jnp.int32, (c, LANES), 1)
        lane1 = lax.broadcasted_iota(jnp.int32, (1, LANES), 1)
        dirs = (0, 1)
        lane_b = [d * hh + h for d in dirs]
        lane_a = [2 * hh + d * hh + h for d in dirs]

        def seq(i, box):
            sts, dst = box[0]
            live = i < nc
            ii = jnp.where(live, i, 0)
            nf = [ii, nc - 1 - ii]
            nb = [nc - 1 - ii, ii]
            sf = [jnp.where(live, n, nc) for n in nf]
            sb = [jnp.where(live, n, nc) for n in nb]
            rf = [pl.ds(pl.multiple_of(n * c, c), c) for n in nf]
            rb = [pl.ds(pl.multiple_of(n * c, c), c) for n in nb]
            for d in dirs:
                st_s[d, sf[d]] = sts[d]
                dst_s[d, sb[d]] = dst[d]
            dob = [do_ref[rb[d], :] for d in dirs]
            vn = [u_i[d, rf[d], :] - _mm(w_i[d, rf[d], :], sts[d]) for d in dirs]
            dvn = [_mm_tn(p_i[d, 0, nb[d]], dob[d]) + _mm(kd_i[d, rb[d], :], dst[d]) for d in dirs]
            yield
            for d in dirs:
                vn_s[d, pl.ds(pl.multiple_of(sf[d] * c, c), c), :] = vn[d]
                dvn_s[d, pl.ds(pl.multiple_of(sb[d] * c, c), c), :] = dvn[d]
            sts = tuple(sts[d] * cd_i[d, 0, nf[d]][0:1, :] + _mm_tn(kd_i[d, rf[d], :], vn[d]) for d in dirs)
            dst = tuple(_mm_tn(qd_i[d, rb[d], :], dob[d]) + cd_i[d, 0, nb[d]][0:1, 0:1] * dst[d]
                        - _mm_tn(w_i[d, rb[d], :], dvn[d]) for d in dirs)
            box[0] = (sts, dst)
            yield

        def bpar(chunks, tick=lambda: None):
            ch = [(d, n) for n in chunks for d in dirs]
            rows = [pl.ds(pl.multiple_of(n * c, c), c) for _, n in ch]
            masks = [_tri_masks(c, d == 1) for d, _ in ch]
            ld = lambda ref: [ref[d, r, :] for (d, _), r in zip(ch, rows)]
            ldm = lambda ref: [ref[d, 0, n] for d, n in ch]
            q, k, v, dob = ([ref[r, :] for r in rows] for ref in (q_ref, k_ref, v_ref, do_ref))
            gates = [_gate_cols(ba_ref[r, :], al_ref[...], dt_ref[...], lane_b[d], lane_a[d]) for (d, _), r in zip(ch, rows)]
            beta, g, aexp, xa = ([gt[x] for gt in gates] for x in range(4))
            st, dst = ([ref[d, n] for d, n in ch] for ref in (st_s, dst_s))
            vn, dvn, u, w = ld(vn_s), ld(dvn_s), ld(u_i), ld(w_i)
            p = [x.astype(f32) for x in ldm(p_i)]
            tm, dec = ldm(t_i), ldm(dec_i)
            cd = [cd_i[d, 0, n][0:1, 0:1] for d, n in ch]
            dp = _map(lambda m, a, b: jnp.where(m[0], _mm_nt(a, b), 0.0), masks, dob, vn)
            dqd = _map(_mm_nt, dob, st)
            dkd = _map(_mm_nt, vn, dst)
            dw = _map(lambda a, b: -_mm_nt(a, b), dvn, st)
            dcd = _map(lambda a, b: jnp.sum(jnp.sum(a * b, axis=1, keepdims=True), axis=0, keepdims=True), st, dst)
            tick()
            gcum = _map(lambda m, x: _cum_mm(m[0], jnp.broadcast_to(x, (c, c)))[:, 0:1], masks, g)
            glast = [jnp.sum(x, axis=0, keepdims=True) for x in g]
            e = [jnp.exp(x) for x in gcum]
            el = _map(lambda a, b: jnp.exp(a - b), glast, gcum)
            qs = [x * scale for x in q]
            kb = _map(lambda a, b: a * b, k, beta)
            a = _map(lambda m, b, ki, dc: jnp.where(m[1], b * _mm_nt(ki, ki) * dc, 0.0), masks, beta, k, dec)
            tick()
            dvb = _map(_hmm3_tn, tm, dvn)
            dkbe = _map(_hmm3_tn, tm, dw)
            tick()
            da = _map(lambda m, x, ui, y, wi: -jnp.where(m[1], _mm_nt(x, ui) + _mm_nt(y, wi), 0.0), masks, dvb, u, dkbe, w)
            g1 = _map(lambda x, y: x * y, da, dec)
            g2 = _map(lambda x, y: x * y, dp, dec)
            dkb = _map(lambda x, ki, y, ei: _mm(x, ki) + y * ei, g1, k, dkbe, e)
            tick()
            dk = _map(lambda x, kbi, y, qi, z, b, t, l: _mm_tn(x, kbi) + _mm_tn(y, qi) + z * b + t * l,
                      g1, kb, g2, qs, dkb, beta, dkd, el)
            dqs = _map(lambda y, ki, x, ei: _mm(y, ki) + x * ei, g2, k, dqd, e)
            tick()
            ddd = _map(lambda x, ai, y, pi: x * ai + y * pi, da, a, dp, p)
            ones = jnp.ones((c, LANES), f32)
            dgcum = _map(lambda x: jnp.sum(x, axis=1, keepdims=True) - _hmm3_tn(x, ones)[:, 0:1], ddd)
            for x, (d, n) in enumerate(ch):
                dbeta = jnp.sum(dvb[x] * v[x], axis=1, keepdims=True) + jnp.sum(dkb[x] * k[x], axis=1, keepdims=True)
                de = jnp.sum(dkbe[x] * kb[x], axis=1, keepdims=True) + jnp.sum(dqd[x] * qs[x], axis=1, keepdims=True)
                del_ = jnp.sum(dkd[x] * k[x], axis=1, keepdims=True)
                dgc = dgcum[x] + de * e[x] - del_ * el[x]
                dglast = jnp.sum(del_ * el[x], axis=0, keepdims=True) + dcd[x] * cd[x]
                dg = _cum_mm(masks[x][2], jnp.broadcast_to(dgc, (c, LANES)))[:, 0:1] + dglast
                r = rows[x]
                if d == 0:
                    dqkv_ref[0, r, :] = dqs[x] * scale
                    dqkv_ref[1, r, :] = dk[x]
                    dqkv_ref[2, r, :] = dvb[x] * beta[x]
                else:
                    dqkv_ref[0, r, :] = dqkv_ref[0, r, :] + dqs[x] * scale
                    dqkv_ref[1, r, :] = dqkv_ref[1, r, :] + dk[x]
                    dqkv_ref[2, r, :] = dqkv_ref[2, r, :] + dvb[x] * beta[x]
                dlb = dbeta * beta[x] * (1.0 - beta[x])
                dalpha = -dg * aexp[x] * _sigmoid(xa[x])
                dba_ref[r, :] = dba_ref[r, :] + jnp.where(lane == lane_b[d], dlb, 0.0) + jnp.where(lane == lane_a[d], dalpha, 0.0)
                dal_ref[...] = dal_ref[...] + jnp.where(lane1 == lane_a[d], jnp.sum(dg * g[x], axis=0, keepdims=True), 0.0)
                ddt_ref[...] = ddt_ref[...] + jnp.where(lane1 == lane_a[d], jnp.sum(dalpha, axis=0, keepdims=True), 0.0)

        half, groups = nc // 2, nc // per

        def group(k):
            return [half - 2 * k - 2, half - 2 * k - 1, half + 2 * k, half + 2 * k + 1]

        def run(gen):
            for _ in gen:
                pass

        def plain(i, carry):
            box = [carry]
            run(seq(i, box))
            return box[0]

        def woven(k, carry):
            box = [carry]

            def two_steps():
                yield from seq(half + 2 + 2 * k, box)
                yield from seq(half + 3 + 2 * k, box)

            chain = two_steps()
            bpar(group(k), lambda: next(chain, None))
            run(chain)
            return box[0]

        z0 = jnp.zeros((128, 128), f32)
        lax.fori_loop(0, groups, woven, lax.fori_loop(0, half + 2, plain, ((z0, z0), (z0, z0))))

    blk = lambda off: pl.BlockSpec((s, 128), functools.partial(lambda b, h, off: (b, off + h), off=off))
    row = pl.BlockSpec((1, LANES), lambda b, h: (0, 0))
    tok2, mat2, cd2, _ = _gdn_specs(cfg)
    return pl.pallas_call(
        body, name="gdn_bwd", grid=(cfg.BL, hh),
        in_specs=[blk(0), blk(hh), blk(2 * hh), pl.BlockSpec((s, LANES), lambda b, h: (b, ba_blk)),
                  pl.BlockSpec((s, 128), lambda b, h: (b, h)), row, row, tok2, tok2, tok2, tok2, mat2, mat2, mat2, cd2],
        out_specs=[pl.BlockSpec((3, s, 128), lambda b, h: (0, b, h)), pl.BlockSpec((s, LANES), lambda b, h: (b, 0)), row, row],
        out_shape=[jax.ShapeDtypeStruct((3, cfg.T, cfg.DN), f32), jax.ShapeDtypeStruct((cfg.T, LANES), f32),
                   jax.ShapeDtypeStruct((1, LANES), f32), jax.ShapeDtypeStruct((1, LANES), f32)],
        scratch_shapes=[pltpu.VMEM((2, s + c, 128), f32)] * 2 + [pltpu.VMEM((2, nc + 1, 128, 128), f32)] * 2,
        compiler_params=_params(("arbitrary", "arbitrary")),
    )(qkv, qkv, qkv, proj, do, alog_row, dt_row, *saved)


def _b_prep(cfg, proj, dqkv, conv_w):
    dn, s = cfg.DN, cfg.S

    def body(x_ref, dy_ref, w_ref, dx_ref, dw_ref):
        _zero_at_first(pl.program_id(1) == 0, dw_ref)
        sec = pl.program_id(0)
        x = x_ref[...]
        c = _conv_fwd(x, w_ref, 2)
        y, dsilu = _silu(c)
        dy = dy_ref[0]
        parts = []
        for h in range(cfg.H):
            sl = slice(h * 128, (h + 1) * 128)
            yh, dyh = y[:, sl], dy[:, sl]
            inv = lax.rsqrt(jnp.sum(yh * yh, axis=1, keepdims=True) + EPS)
            dn_h = inv * dyh - yh * (inv * inv * inv) * jnp.sum(dyh * yh, axis=1, keepdims=True)
            parts.append(jnp.where(sec < 2, dn_h, dyh))
        ds = jnp.concatenate(parts, axis=1) if len(parts) > 1 else parts[0]
        dx, dws = _conv_bwd(ds * dsilu, x, w_ref, 2)
        for j, dw in enumerate(dws):
            dw_ref[j:j + 1, :] = dw_ref[j:j + 1, :] + dw
        dx_ref[...] = dx.astype(bf16)

    return pl.pallas_call(
        body, name="dn_prep_bwd", grid=(3, cfg.BL),
        in_specs=[pl.BlockSpec((s, dn), lambda j, b: (b, j)), pl.BlockSpec((1, s, dn), lambda j, b: (j, b, 0)),
                  pl.BlockSpec((4, dn), lambda j, b: (0, j))],
        out_specs=[pl.BlockSpec((s, dn), lambda j, b: (b, j)), pl.BlockSpec((4, dn), lambda j, b: (0, j))],
        out_shape=[jax.ShapeDtypeStruct((cfg.T, 3 * dn), bf16), jax.ShapeDtypeStruct((4, 3 * dn), f32)],
        compiler_params=_params(("arbitrary", "arbitrary")),
    )(proj, dqkv, conv_w)


def _b_in(cfg, dqkv_pre, dz, dlx, dlg, dba, dr_mid, r_in, g1, wcat):
    d, dn, lw = cfg.D, cfg.DN, cfg.LW

    def body(dq_ref, dz_ref, dlx_ref, dlg_ref, dba_ref, dr_ref, r_ref, g_ref, w_ref, o_ref, dp_ref, dg_ref):
        _zero_at_first(pl.program_id(0) == 0, dg_ref)
        dp_ref[:, 0:cfg.ZO] = dq_ref[...].astype(bf16)
        dp_ref[:, cfg.ZO:cfg.LXO] = dz_ref[...].astype(bf16)
        dp_ref[:, cfg.LXO:cfg.LGO] = dlx_ref[...].astype(bf16)
        dp_ref[:, cfg.LGO:cfg.BAO] = dlg_ref[...].astype(bf16)
        dp_ref[:, cfg.BAO:] = dba_ref[...].astype(bf16)
        g = g_ref[...]
        _, xh, inv = _rms_fwd(r_ref[...], g)
        dx, dg = _rms_bwd(_mm(dp_ref[...], w_ref[...]), xh, inv, g)
        dg_ref[...] = dg_ref[...] + dg
        o_ref[...] = dr_ref[...] + dx

    return _row_call("in_proj_bwd", body, cfg.T, cfg.TM,
                     [(dqkv_pre, 3 * dn, 0), (dz, dn, 0), (dlx, lw, 0), (dlg, lw, 0), (dba, LANES, 0), (dr_mid, d, 0), (r_in, d, 0)],
                     [g1, wcat], [(d, f32), (cfg.PC, bf16)], [((1, d), f32)])


def _adam_call(name, w, g, m, v, tr):
    rows, cols = w.shape
    bc1 = 1.0 - ADAM_B1 ** ADAM_STEP
    bc2 = 1.0 - ADAM_B2 ** ADAM_STEP

    def body(w_ref, g_ref, m_ref, v_ref, d_ref, nm_ref, nv_ref):
        g = g_ref[...]
        m = ADAM_B1 * m_ref[...] + (1.0 - ADAM_B1) * g
        v = ADAM_B2 * v_ref[...] + (1.0 - ADAM_B2) * (g * g)
        nm_ref[...] = m
        nv_ref[...] = v
        d_ref[...] = -ADAM_LR * ((m / bc1) / (jnp.sqrt(v / bc2) + ADAM_EPS) + ADAM_WD * w_ref[...])

    spec = pl.BlockSpec((tr, cols), lambda i: (i, 0))
    return pl.pallas_call(
        body, name=name, grid=(rows // tr,), in_specs=[spec] * 4, out_specs=[spec] * 3,
        out_shape=[jax.ShapeDtypeStruct((rows, cols), f32)] * 3, compiler_params=_params(("arbitrary",)),
    )(w, g, m, v)


def _sum8_call(name, x, tr):
    _, rows, cols = x.shape

    def body(x_ref, o_ref):
        acc = x_ref[0].astype(f32)
        for j in range(1, N_DEV):
            acc = acc + x_ref[j].astype(f32)
        o_ref[...] = acc

    return pl.pallas_call(
        body, name=name, grid=(rows // tr,), in_specs=[pl.BlockSpec((N_DEV, tr, cols), lambda i: (0, i, 0))],
        out_specs=pl.BlockSpec((tr, cols), lambda i: (i, 0)), out_shape=jax.ShapeDtypeStruct((rows, cols), f32),
        compiler_params=_params(("arbitrary",)),
    )(x)


def _all_gather(name, shards):
    na = len(shards)

    def body(*refs):
        xs, outs = refs[:na], refs[na:2 * na]
        send_sems, recv_sems, local_sems = refs[2 * na:]
        x, y, c = lax.axis_index("x"), lax.axis_index("y"), lax.axis_index("c")
        me, sibling = (x, y, c), (x, y, 1 - c)
        chips = [(1 - x, y), (x, 1 - y), (1 - x, 1 - y)]

        def copy(a, k, block, to, src=None):
            px, py, pc = block
            slot = outs[a].at[4 * px + 2 * py + pc]
            return pltpu.make_async_remote_copy(
                src_ref=slot if src is None else src, dst_ref=slot,
                send_sem=send_sems.at[a, k], recv_sem=recv_sems.at[a, k], device_id=to, device_id_type=MESH)

        mine = [pltpu.make_async_copy(xs[a], outs[a].at[4 * x + 2 * y + c], local_sems.at[a]) for a in range(na)]
        for cp in mine:
            cp.start()
        first = []
        for a in range(na):
            first.append(copy(a, 0, me, sibling, src=xs[a]))
            first += [copy(a, 1 + j, me, (*chip, c), src=xs[a]) for j, chip in enumerate(chips)]
        for cp in first:
            cp.start()
        passed = []
        for j, chip in enumerate(chips):
            for a in range(na):
                copy(a, 1 + j, (*chip, c), me).wait_recv()
                cp = copy(a, 4 + j, (*chip, c), sibling)
                cp.start()
                passed.append(cp)
        for a in range(na):
            copy(a, 0, sibling, me).wait_recv()
            for j, chip in enumerate(chips):
                copy(a, 4 + j, (*chip, 1 - c), me).wait_recv()
        for cp in first + passed:
            cp.wait_send()
        for cp in mine:
            cp.wait()

    hbm = pl.BlockSpec(memory_space=pltpu.HBM)
    return pl.pallas_call(
        body, name=name, out_shape=[jax.ShapeDtypeStruct((N_DEV,) + s.shape, s.dtype) for s in shards],
        in_specs=[hbm] * na, out_specs=[hbm] * na,
        scratch_shapes=[pltpu.SemaphoreType.DMA((na, 7)), pltpu.SemaphoreType.DMA((na, 7)), pltpu.SemaphoreType.DMA((na,))],
    )(*shards)


def _peer_list():
    x, y, c = lax.axis_index("x"), lax.axis_index("y"), lax.axis_index("c")
    return 4 * x + 2 * y + c, [(x ^ (k >> 2), y ^ ((k >> 1) & 1), c ^ (k & 1)) for k in range(1, N_DEV)]


_HBM = pl.BlockSpec(memory_space=pltpu.HBM)
_SEM = pl.BlockSpec(memory_space=pltpu.SEMAPHORE)
_EFFECT = pltpu.SideEffectType.DATAFLOW_SIDE_EFFECTING


def _exchange_copies(xs, lands, send_sem, recv_sem, gather):
    me, peers = _peer_list()
    return [pltpu.make_async_remote_copy(
        src_ref=xs[a] if gather else xs[a].at[4 * px + 2 * py + pc], dst_ref=lands[a].at[me],
        send_sem=send_sem.at[7 * a + k], recv_sem=recv_sem.at[7 * a + k], device_id=(px, py, pc), device_id_type=MESH)
        for k, (px, py, pc) in enumerate(peers) for a in range(len(xs))]


def _exchange_start(name, blocks, gather=False):
    na = len(blocks)

    def body(*refs):
        for cp in _exchange_copies(refs[:na], refs[na:2 * na], refs[2 * na], refs[2 * na + 1], gather):
            cp.start()
        refs[-1][...] = jnp.zeros_like(refs[-1])

    lands = [jax.ShapeDtypeStruct((N_DEV,) + b.shape if gather else b.shape, b.dtype) for b in blocks]
    hbm = [pltpu.HBM(b.shape, b.dtype) for b in blocks] + [pltpu.HBM(b.shape, b.dtype) for b in lands]
    send_sem, recv_sem, *thru, token = pl.pallas_call(
        body, name=name,
        out_shape=(pltpu.SemaphoreType.DMA((7 * na,)), pltpu.SemaphoreType.DMA((7 * na,)), *hbm,
                   jax.ShapeDtypeStruct((8, LANES), f32)),
        in_specs=[_HBM] * (2 * na), out_specs=(_SEM, _SEM, *([_HBM] * (2 * na)), pl.BlockSpec(memory_space=pltpu.VMEM)),
        input_output_aliases={i: 2 + i for i in range(2 * na)},
        compiler_params=pltpu.CompilerParams(has_side_effects=_EFFECT),
    )(*[pltpu.with_memory_space_constraint(b, pltpu.HBM) for b in blocks],
      *[pltpu.with_memory_space_constraint(lax.empty(b.shape, b.dtype), pltpu.HBM) for b in lands])
    return send_sem, recv_sem, thru, token


def _exchange_wait(name, send_sem, recv_sem, thru, after, gather=False):
    na = len(thru) // 2

    def body(*refs):
        for cp in _exchange_copies(refs[:na], refs[na:2 * na], refs[2 * na], refs[2 * na + 1], gather):
            cp.wait_send()
            cp.wait_recv()

    return pl.pallas_call(
        body, name=name, out_shape=tuple(pltpu.HBM(t.shape, t.dtype) for t in thru),
        in_specs=[_HBM] * (2 * na) + [_SEM, _SEM, pl.BlockSpec(memory_space=pl.ANY)], out_specs=tuple([_HBM] * (2 * na)),
        input_output_aliases={i: i for i in range(2 * na)},
        compiler_params=pltpu.CompilerParams(has_side_effects=_EFFECT),
    )(*thru, send_sem, recv_sem, after)[na:]


def _layer_fwd(cfg, w, r, p):
    h1, proj = _k_in(cfg, r, w["norm1_g"], w["wcat_t"])
    qkv = _k_prep(cfg, proj, w["dn_conv_w"])
    o, *gdn_saved = _k_gdn_fwd(cfg, qkv, proj, w["alog_row"], w["dt_row"])
    hs = _k_lru_fwd(cfg, proj, w["lru_conv_w"], w["lru_conv_b"], w["wa"], w["wx"], w["lru_ba"], w["lru_bx"], w["lru_lambda"])
    if "late" in w:
        w.update(w.pop("late")(hs))
    mix, r_mid = _k_mix(cfg, o, proj, hs, r, w["dn_norm_g"], w["lru_norm_g"], w["w_out"])
    h2, gp, up = _k_ffn_a(cfg, r_mid, w["norm2_g"], w["ffn_wg_t"], w["ffn_wu_t"])
    act = _k_ffn_b(cfg, gp, up, w["ffn_conv_w"], w["ffn_conv_b"])
    r2 = _k_ffn_c(cfg, act, r_mid, w["ffn_wd"])
    pn, r3 = _k_ple(cfg, r2, p, w["ple_norm_g"], w["ple_wg"], w["ple_bg"], w["ple_wp_t"])
    saved = dict(r=r, h1=h1, proj=proj, qkv=qkv, o=o, gdn=gdn_saved, hs=hs, mix=mix, r_mid=r_mid, h2=h2, gp=gp, up=up,
                 r2=r2, pn=pn, p=p)
    return r3, saved


def _layer_bwd(cfg, w, sv, dr3, early=None):
    g = {}
    dt = min(1024, cfg.D)
    dr2, dr2b, dlog, dpp, g["ple_norm_g"], g["ple_bg"] = _b_ple(cfg, dr3, sv["r2"], sv["p"], w["ple_norm_g"], w["ple_wg"], w["ple_bg"], w["ple_wp_t"])
    g["ple_wg"] = _mm_tn_call(cfg, "d_ple_wg", sv["pn"], dlog, dt)
    g["ple_wp_t"] = _mm_tn_call(cfg, "d_ple_wp", dpp, sv["p"], cfg.PD)
    dgp, dup, g["ffn_wd"], g["ffn_conv_w"], g["ffn_conv_b"] = _b_ffn_bc(cfg, dr2b, w["ffn_wd"], sv["gp"], sv["up"], w["ffn_conv_w"], w["ffn_conv_b"])
    dr_mid, g["norm2_g"] = _b_ffn_a(cfg, dgp, dup, dr2, sv["r_mid"], w["norm2_g"], w["ffn_wg_t"], w["ffn_wu_t"])
    g["ffn_wg_t"] = _mm_tn_call(cfg, "d_ffn_wg", dgp, sv["h2"], dt)
    g["ffn_wu_t"] = _mm_tn_call(cfg, "d_ffn_wu", dup, sv["h2"], dt)
    if early is not None:
        w = dict(w, dn_norm_g=w["dn_norm_g"] + early(EARLY, g)[0, 0])
    do, dz, dlg, dhs, g["dn_norm_g"], g["lru_norm_g"] = _b_mix(cfg, dr_mid, sv["o"], sv["proj"], sv["hs"], w["dn_norm_g"], w["lru_norm_g"], w["w_out"])
    g["w_out"] = _mm_tn_call(cfg, "d_w_out", sv["mix"], dr_mid, dt)
    if early is not None:
        w = dict(w, lru_conv_b=w["lru_conv_b"] + early(("w_out",), g)[0, 0])
    dlx, g["lru_conv_w"], g["lru_conv_b"], g["wa"], g["wx"], g["lru_ba"], g["lru_bx"], g["lru_lambda"] = _b_lru(
        cfg, sv["proj"], dhs, w["lru_conv_w"], w["lru_conv_b"], w["wa"], w["wx"], w["lru_ba"], w["lru_bx"], w["lru_lambda"])
    dqkv, dba, g["alog_row"], g["dt_row"] = _b_gdn(cfg, sv["qkv"], sv["proj"], do, w["alog_row"], w["dt_row"], sv["gdn"])
    dqkv_pre, g["dn_conv_w"] = _b_prep(cfg, sv["proj"], dqkv, w["dn_conv_w"])
    dr, dproj, g["norm1_g"] = _b_in(cfg, dqkv_pre, dz, dlx, dlg, dba, dr_mid, sv["r"], w["norm1_g"], w["wcat_t"])
    g["wcat_t"] = _mm_tn_call(cfg, "d_w_in", dproj, sv["h1"], dt)
    return dr, g


BIG = ("w_in", "w_out", "ffn_wg", "ffn_wu", "ffn_wd", "ple_wg", "ple_wp")
BIG_T = {"w_in": True, "w_out": False, "ffn_wg": True, "ffn_wu": True, "ffn_wd": False, "ple_wg": False, "ple_wp": True}
BIG_OPERAND = {"w_in": "wcat_t", "w_out": "w_out", "ffn_wg": "ffn_wg_t", "ffn_wu": "ffn_wu_t", "ffn_wd": "ffn_wd",
               "ple_wg": "ple_wg", "ple_wp": "ple_wp_t"}
EARLY = ("ffn_wg", "ffn_wu", "ffn_wd", "ple_wg", "ple_wp")
SMALL_SHARDED = ("dn_conv_w", "lru_conv_w", "lru_ba", "lru_bx", "lru_lambda", "ffn_conv_w")
SMALL_REPL = ("norm1_g", "dn_a_log", "dn_dt_bias", "dn_norm_g", "lru_conv_b", "lru_wa", "lru_wx", "lru_norm_g", "norm2_g",
              "ffn_conv_b", "ple_norm_g", "ple_bg", "final_g")
WEIGHTS = ("norm1_g", "w_in", "dn_conv_w", "dn_a_log", "dn_dt_bias", "dn_norm_g", "lru_conv_w", "lru_conv_b", "lru_wa",
           "lru_ba", "lru_wx", "lru_bx", "lru_lambda", "lru_norm_g", "w_out", "norm2_g", "ffn_wg", "ffn_wu", "ffn_conv_w",
           "ffn_conv_b", "ffn_wd", "ple_norm_g", "ple_wg", "ple_bg", "ple_wp", "final_g")


def _pad_rows(flat, cols, mult):
    n = flat.shape[0]
    rows = -(-n // cols)
    rows = -(-rows // mult) * mult
    return jnp.pad(flat, (0, rows * cols - n)).reshape(rows, cols)


def _pack(arrs, cols, mult, dtype):
    return _pad_rows(jnp.concatenate([a.reshape(-1).astype(dtype) for a in arrs]), cols, mult)


def _unpack(flat, shapes):
    out, off = [], 0
    for shp in shapes:
        n = math.prod(shp)
        piece = flat[off:off + n]
        if n < 4096:
            piece = lax.optimization_barrier(piece)
        out.append(piece.reshape(shp))
        off += n
    return out


def _unpack8(g8, shapes, axes):
    out, off = [], 0
    for shp, ax in zip(shapes, axes):
        n = math.prod(shp)
        a = g8[:, off:off + n].reshape((N_DEV,) + tuple(shp))
        a = jnp.moveaxis(a, 0, ax)
        out.append(a.reshape(shp[:ax] + (N_DEV * shp[ax],) + shp[ax + 1:]))
        off += n
    return out


def _wcat_t_from_w_in_t(cfg, wt):
    nba = 4 * cfg.H
    pad = jnp.zeros((LANES - nba, wt.shape[1]), wt.dtype)
    return jnp.concatenate([wt[:cfg.LXO], wt[cfg.LXO + nba:], wt[cfg.LXO:cfg.LXO + nba], pad], axis=0)


def _w_in_t_from_wcat_t(cfg, wc):
    nba = 4 * cfg.H
    return jnp.concatenate([wc[:cfg.LXO], wc[cfg.BAO:cfg.BAO + nba], wc[cfg.LXO:cfg.BAO]], axis=0)


def _gate_row(cfg, a):
    h2 = 2 * cfg.H
    return jnp.concatenate([jnp.zeros((1, h2), f32), a.reshape(1, h2), jnp.zeros((1, LANES - 2 * h2), f32)], axis=1)


def _blockdiag(cfg, w):
    w = w.reshape(2, cfg.NB // 2, 2, 64, 64)
    z = jnp.zeros_like(w[:, :, 0])
    top = jnp.concatenate([w[:, :, 0], z], axis=-1)
    bot = jnp.concatenate([z, w[:, :, 1]], axis=-1)
    return jnp.concatenate([top, bot], axis=-2).astype(bf16)


def _unblockdiag(cfg, g):
    a = g[:, :, :64, :64]
    b = g[:, :, 64:, 64:]
    return jnp.stack([a, b], axis=2).reshape(2, cfg.NB, 64, 64)


def _big_operands(cfg, big):
    return {BIG_OPERAND[n]: (_wcat_t_from_w_in_t(cfg, v) if n == "w_in" else v) for n, v in big.items()}


def _layer_operands(cfg, big, small, i):
    return dict(
        _big_operands(cfg, big),
        norm1_g=small["norm1_g"][i][None], dn_conv_w=small["dn_conv_w"][i], alog_row=_gate_row(cfg, small["dn_a_log"][i]),
        dt_row=_gate_row(cfg, small["dn_dt_bias"][i]), dn_norm_g=small["dn_norm_g"][i][None],
        lru_conv_w=small["lru_conv_w"][i], lru_conv_b=small["lru_conv_b"][i][None],
        wa=_blockdiag(cfg, small["lru_wa"][i]), wx=_blockdiag(cfg, small["lru_wx"][i]),
        lru_ba=small["lru_ba"][i], lru_bx=small["lru_bx"][i], lru_lambda=small["lru_lambda"][i],
        lru_norm_g=small["lru_norm_g"][i][None], norm2_g=small["norm2_g"][i][None], ffn_conv_w=small["ffn_conv_w"][i],
        ffn_conv_b=small["ffn_conv_b"][i][None], ple_norm_g=small["ple_norm_g"][i][None], ple_bg=small["ple_bg"][i][None],
    )


def _small_grads_to_problem(cfg, g):
    h = cfg.H
    return dict(
        norm1_g=g["norm1_g"][0], dn_conv_w=g["dn_conv_w"],
        dn_a_log=g["alog_row"][0, 2 * h:4 * h].reshape(2, h), dn_dt_bias=g["dt_row"][0, 2 * h:4 * h].reshape(2, h),
        dn_norm_g=g["dn_norm_g"][0], lru_conv_w=g["lru_conv_w"], lru_conv_b=g["lru_conv_b"][0],
        lru_wa=_unblockdiag(cfg, g["wa"]), lru_wx=_unblockdiag(cfg, g["wx"]), lru_ba=g["lru_ba"], lru_bx=g["lru_bx"],
        lru_lambda=g["lru_lambda"], lru_norm_g=g["lru_norm_g"][0], norm2_g=g["norm2_g"][0], ffn_conv_w=g["ffn_conv_w"],
        ffn_conv_b=g["ffn_conv_b"][0], ple_norm_g=g["ple_norm_g"][0], ple_bg=g["ple_bg"][0],
    )


def _local_step(cfg, get_big, small, x, p, target, on_big_grads):
    r = x.reshape(cfg.T, cfg.D)
    ops, saved = [], []
    for i in range(cfg.L):
        big, token, late = get_big(i, r)
        w = _layer_operands(cfg, big, small, i)
        if token is not None:
            w["norm1_g"] = w["norm1_g"] + token[0, 0]
        if late is not None:
            w["late"] = late
        r, sv = _layer_fwd(cfg, w, r, p[i].reshape(cfg.T, cfg.PD))
        ops.append(w)
        saved.append(sv)
    dr, loss, dgf = _k_loss(cfg, r, target.reshape(cfg.T, cfg.D), small["final_g"][None])
    gsmall = [None] * cfg.L
    for i in reversed(range(cfg.L)):
        first = EARLY + ("w_out",) if i == 0 else ()
        early = (lambda names, g: on_big_grads("%d%s" % (i, names[0]), {n: g[BIG_OPERAND[n]] for n in names})) if first else None
        dr, g = _layer_bwd(cfg, ops[i], saved[i], dr, early)
        token = on_big_grads("%d" % i, {n: (_w_in_t_from_wcat_t(cfg, g["wcat_t"]) if n == "w_in" else g[BIG_OPERAND[n]])
                                        for n in BIG if n not in first})
        if i > 0:
            ops[i - 1]["ple_bg"] = ops[i - 1]["ple_bg"] + token[0, 0]
        gsmall[i] = _small_grads_to_problem(cfg, g)
    gs = {k: jnp.stack([gl[k] for gl in gsmall]) for k in gsmall[0]}
    gs["final_g"] = dgf[0]
    return loss, dr, gs


def _row_tile(rows, limit=512):
    best = rows
    for t in range(8, min(rows, limit) + 1, 8):
        if rows % t == 0:
            best = t
    return best if best <= limit or rows <= limit else rows


def _adam_group(name, ws, gs, ms, vs, cols, tr):
    shapes = [w.shape for w in ws]
    pk = lambda arrs: _pack(arrs, cols, tr, f32)
    w2 = pk(ws)
    d, nm, nv = _adam_call(name, w2, pk(gs), pk(ms), pk(vs), min(tr, w2.shape[0]))
    return [_unpack(a.reshape(-1), shapes) for a in (d, nm, nv)]


def kernel(x, p, norm1_g, w_in, dn_conv_w, dn_a_log, dn_dt_bias, dn_norm_g, lru_conv_w, lru_conv_b, lru_wa, lru_ba, lru_wx, lru_bx, lru_lambda, lru_norm_g, w_out, norm2_g, ffn_wg, ffn_wu, ffn_conv_w, ffn_conv_b, ffn_wd, ple_norm_g, ple_wg, ple_bg, ple_wp, final_g, loss_target, m_norm1_g, m_w_in, m_dn_conv_w, m_dn_a_log, m_dn_dt_bias, m_dn_norm_g, m_lru_conv_w, m_lru_conv_b, m_lru_wa, m_lru_ba, m_lru_wx, m_lru_bx, m_lru_lambda, m_lru_norm_g, m_w_out, m_norm2_g, m_ffn_wg, m_ffn_wu, m_ffn_conv_w, m_ffn_conv_b, m_ffn_wd, m_ple_norm_g, m_ple_wg, m_ple_bg, m_ple_wp, m_final_g, v_norm1_g, v_w_in, v_dn_conv_w, v_dn_a_log, v_dn_dt_bias, v_dn_norm_g, v_lru_conv_w, v_lru_conv_b, v_lru_wa, v_lru_ba, v_lru_wx, v_lru_bx, v_lru_lambda, v_lru_norm_g, v_w_out, v_norm2_g, v_ffn_wg, v_ffn_wu, v_ffn_conv_w, v_ffn_conv_b, v_ffn_wd, v_ple_norm_g, v_ple_wg, v_ple_bg, v_ple_wp, v_final_g):
    cfg = CFG
    a = dict(locals())
    wl = {n: a[n] for n in WEIGHTS}
    ml = {n: a["m_" + n] for n in WEIGHTS}
    vl = {n: a["v_" + n] for n in WEIGHTS}
    me = 4 * lax.axis_index("x") + 2 * lax.axis_index("y") + lax.axis_index("c")
    nl = cfg.L

    blocks = [(jnp.swapaxes(wl[n], 1, 2) if BIG_T[n] else wl[n]).astype(bf16) for n in BIG]
    ss_shapes = [wl[n].shape for n in SMALL_SHARDED]
    first, s8 = _all_gather("gather_weights_0", [blocks[0][0], _pack([wl[n] for n in SMALL_SHARDED], LANES, 8, f32)])
    small = dict(zip(SMALL_SHARDED, _unpack8(s8.reshape(N_DEV, -1), ss_shapes, [2] * len(ss_shapes))))
    small.update({n: wl[n] for n in SMALL_REPL})

    def start_gather(key, i, names, behind):
        shards, _ = lax.optimization_barrier(([blk[i] for n, blk in zip(BIG, blocks) if n in names], behind))
        return _exchange_start("gather_start_" + key, shards, gather=True)

    def wait_gather(key, i, names, started, behind):
        send_sem, recv_sem, thru, _ = started
        lands = _exchange_wait("gather_wait_" + key, send_sem, recv_sem, thru, behind, gather=True)
        own = [blk[i][None] for n, blk in zip(BIG, blocks) if n in names]
        full = [lax.dynamic_update_slice_in_dim(land, o, me, 0) for land, o in zip(lands, own)]
        return {n: f.reshape(N_DEV * f.shape[1], f.shape[2]) for n, f in zip([n for n in BIG if n in names], full)}

    ahead = {}

    def get_big(i, r):
        if i > 0:
            big = wait_gather("%d" % i, i, BIG, ahead.pop(i), r)
            if i + 1 < nl:
                ahead[i + 1] = start_gather("%d" % (i + 1), i + 1, BIG, big["w_in"])
            return big, (ahead[i + 1][3] if i + 1 < nl else None), None
        rest = start_gather("0r", 0, BIG[1:], first)

        def late(x):
            big = wait_gather("0r", 0, BIG[1:], rest, x)
            ops = _big_operands(cfg, big)
            if nl > 1:
                ahead[1] = start_gather("1", 1, BIG, big["w_out"])
                ops["norm2_g"] = small["norm2_g"][0][None] + ahead[1][3][0, 0]
            return ops

        return {"w_in": first.reshape(N_DEV * first.shape[1], first.shape[2])}, rest[3], late

    pending = {}

    def on_big_grads(key, g):
        send = [g[n].reshape((N_DEV,) + blk.shape[1:]).astype(bf16) for n, blk in zip(BIG, blocks) if n in g]
        own = [lax.dynamic_index_in_dim(sd, me, 0, keepdims=True) for sd in send]
        send_sem, recv_sem, thru, token = _exchange_start("exchange_start_" + key, send)
        pending[key] = (send_sem, recv_sem, thru, own, [n for n in BIG if n in g])
        return token

    loss_part, dr, gsmall = _local_step(cfg, get_big, small, x, p, loss_target, on_big_grads)
    grad_x = dr.reshape(x.shape)

    sums = {n: [None] * nl for n in BIG}
    for key in sorted(pending):
        send_sem, recv_sem, thru, own, names = pending[key]
        lands = _exchange_wait("exchange_wait_" + key, send_sem, recv_sem, thru, dr)
        for n, land, o in zip(names, lands, own):
            slots = lax.dynamic_update_slice_in_dim(land, o, me, 0)
            sums[n][int(key[0])] = _sum8_call("sum_%s_%s" % (n, key), slots, slots.shape[1])
    gl = {}
    for n in BIG:
        s = jnp.stack(sums[n])
        gl[n] = jnp.swapaxes(s, 1, 2) if BIG_T[n] else s

    small_names = SMALL_REPL + SMALL_SHARDED
    narrow = ("lru_wa", "lru_wx")
    wide = [n for n in small_names if n not in narrow]
    wide_shapes = [gsmall[n].shape for n in wide]
    sv = _pack([gsmall[n] for n in wide] + [loss_part[0, 0:1]], LANES, 512, f32)
    nv = _pack([gsmall[n] for n in narrow], LANES, 512, bf16)
    sv8, nv8 = _all_gather("gather_small_grads", [sv, nv])
    small_sum = _sum8_call("sum_small", sv8, 512).reshape(-1)
    narrow_sum = _sum8_call("sum_small_narrow", nv8, 512).reshape(-1)
    gl.update(zip(wide, _unpack(small_sum, wide_shapes)))
    gl.update(zip(narrow, _unpack(narrow_sum, [gsmall[n].shape for n in narrow])))
    loss = small_sum[sum(math.prod(s) for s in wide_shapes)]
    for n in SMALL_SHARDED:
        shard = wl[n].shape[2]
        gl[n] = lax.dynamic_slice_in_dim(gl[n], me * shard, shard, axis=2)

    outs = {}
    for n in BIG:
        shp = wl[n].shape
        two = lambda t: t.reshape(-1, shp[-1])
        d, nm, nv = _adam_call("adam_" + n, two(wl[n]), two(gl[n]), two(ml[n]), two(vl[n]), _row_tile(math.prod(shp[:-1])))
        outs[n] = (d.reshape(shp), nm.reshape(shp), nv.reshape(shp))
    d, nm, nv = _adam_group("adam_small", [wl[n] for n in small_names], [gl[n] for n in small_names],
                            [ml[n] for n in small_names], [vl[n] for n in small_names], LANES, 64)
    for j, n in enumerate(small_names):
        outs[n] = (d[j], nm[j], nv[j])
    return (loss, grad_x, *[gl[n] for n in WEIGHTS], *[outs[n][0] for n in WEIGHTS], *[outs[n][1] for n in WEIGHTS],
            *[outs[n][2] for n in WEIGHTS])
```
